```python
import math
import jax, jax.numpy as jnp
from jax import lax
import numpy as np

D_MODEL = 1024
BATCH = 8
SEQ = 4096
DEPTH = 2
DEC_BATCH = 16
DEC_SEQ = 64
PAST_LEN = 1024

CHUNK = 64
N_PAR_LAYERS = (DEPTH + 1) // 2
N_GM_LAYERS = DEPTH // 2

POOL_WINDOWS = (2, 4, 8, 16)
POOL_GROUPS = 4
POOL_GROUP_DIM = D_MODEL // 8
POOL_WIDTH = POOL_GROUPS * POOL_GROUP_DIM
POOL_STATE = max(POOL_WINDOWS) - 1

N_HEADS = 8
HEAD_DIM = D_MODEL // 16
N_KV_HEADS = 4
Q_PER_KV = N_HEADS // N_KV_HEADS
ATT_WIDTH = N_HEADS * HEAD_DIM
N_IDX_HEADS = 8
IDX_DIM = 64
TOPK_MAX = 256
Q_BLOCK = 128
ROPE_THETA = 500000.0
MIX_WIDTH = POOL_WIDTH + ATT_WIDTH

GM_CHUNK = 128
GM_HALF = D_MODEL
GM_GROUPS = 8
GM_GROUP_DIM = GM_HALF // GM_GROUPS

N_EXPERT_GROUPS = 4
EXPERTS_PER_GROUP = 8
N_EXPERTS = N_EXPERT_GROUPS * EXPERTS_PER_GROUP
TOPK_IN_GROUP = 2
D_EXPERT = D_MODEL // 4
MOE_BLOCK = 128

RMS_EPS = 1e-6
LN_EPS = 1e-5

kernel_name = "hybrid_pool_dsa_gmlp_hmoe_stream_step"


def rmsnorm(x, g):
    xf = x.astype(jnp.float32)
    y = xf * lax.rsqrt(jnp.mean(xf * xf, axis=-1, keepdims=True) + RMS_EPS)
    return (y * g.astype(jnp.float32)).astype(x.dtype)


def layernorm(x, g, b):
    xf = x.astype(jnp.float32)
    mu = jnp.mean(xf, axis=-1, keepdims=True)
    var = jnp.mean(jnp.square(xf - mu), axis=-1, keepdims=True)
    y = (xf - mu) * lax.rsqrt(var + LN_EPS)
    return (y * g.astype(jnp.float32) + b.astype(jnp.float32)).astype(x.dtype)


def partial_rope(x, pos):
    rd = x.shape[-1] // 4
    half = rd // 2
    inv = 1.0 / (ROPE_THETA ** (jnp.arange(half, dtype=jnp.float32) / half))
    ang = pos.astype(jnp.float32)[:, None] * inv[None, :]
    cos = jnp.cos(ang)[None, :, None, :]
    sin = jnp.sin(ang)[None, :, None, :]
    xr = x[..., :rd].astype(jnp.float32)
    x1, x2 = xr[..., :half], xr[..., half:]
    rot = jnp.concatenate([x1 * cos - x2 * sin, x2 * cos + x1 * sin], axis=-1).astype(x.dtype)
    return jnp.concatenate([rot, x[..., rd:]], axis=-1)


def pool_mixer(xp, hist, pos, w_grp, scale):
    B, T, _ = xp.shape
    full = jnp.concatenate([hist, xp], axis=1)
    cs = jnp.cumsum(full.astype(jnp.float32), axis=1)
    cs = jnp.concatenate([jnp.zeros((B, 1, POOL_WIDTH), jnp.float32), cs], axis=1)
    means = []
    for gi, w in enumerate(POOL_WINDOWS):
        sl = slice(gi * POOL_GROUP_DIM, (gi + 1) * POOL_GROUP_DIM)
        s = cs[:, POOL_STATE + 1:POOL_STATE + 1 + T, sl] - cs[:, POOL_STATE + 1 - w:POOL_STATE + 1 - w + T, sl]
        cnt = jnp.minimum(pos + 1, w).astype(jnp.float32)[None, :, None]
        means.append(s / cnt)
    mean = jnp.concatenate(means, axis=-1)
    d = (mean - xp.astype(jnp.float32)).astype(xp.dtype).reshape(B, T, POOL_GROUPS, POOL_GROUP_DIM)
    y = jnp.einsum('btgc,gcd->btgd', d, w_grp).reshape(B, T, POOL_WIDTH) * scale
    return y, full[:, -POOL_STATE:]


def dsa_block(q, qi, wi, q_pos, k_all, v_all, ki_all):
    B, Tq = q.shape[:2]
    L = k_all.shape[1]
    n_sel = min(TOPK_MAX, L // 4)
    limit = (q_pos // CHUNK + 1) * CHUNK
    admissible = jnp.arange(L)[None, :] < limit[:, None]
    sc = jnp.einsum('bthd,bsd->bths', qi, ki_all, preferred_element_type=jnp.float32)
    index = jnp.einsum('bths,bth->bts', jax.nn.relu(sc), wi.astype(jnp.float32))
    index = jnp.where(admissible[None], index, -jnp.inf)
    _, top_idx = lax.top_k(index, n_sel)
    sel_ok = top_idx < limit[None, :, None]
    k_sel = jax.vmap(lambda kb, ib: kb[ib])(k_all, top_idx)
    v_sel = jax.vmap(lambda vb, ib: vb[ib])(v_all, top_idx)
    qg = q.reshape(B, Tq, N_KV_HEADS, Q_PER_KV, HEAD_DIM)
    s = jnp.einsum('btkgd,btjkd->btkgj', qg, k_sel, preferred_element_type=jnp.float32) * (HEAD_DIM ** -0.5)
    s = jnp.where(sel_ok[:, :, None, None, :], s, -1e30)
    p = jax.nn.softmax(s, axis=-1)
    o = jnp.einsum('btkgj,btjkd->btkgd', p, v_sel.astype(jnp.float32))
    return o.reshape(B, Tq, ATT_WIDTH).astype(q.dtype)


def dsa_mixer(q, qi, wi, pos, k_all, v_all, ki_all):
    B, T = q.shape[:2]
    if T % Q_BLOCK != 0:
        return dsa_block(q, qi, wi, pos, k_all, v_all, ki_all)
    nb = T // Q_BLOCK

    def to_blocks(a):
        return jnp.swapaxes(a.reshape((B, nb, Q_BLOCK) + a.shape[2:]), 0, 1)

    def one_block(args):
        qb, qib, wib, pb = args
        return dsa_block(qb, qib, wib, pb, k_all, v_all, ki_all)

    o = lax.map(one_block, (to_blocks(q), to_blocks(qi), to_blocks(wi), pos.reshape(nb, Q_BLOCK)))
    return jnp.swapaxes(o, 0, 1).reshape(B, T, ATT_WIDTH)


def parallel_mixer(h, pos, pool_hist, past_k, past_v, past_ki, w_in, pool_w, pool_scale, w_out):
    B, T, _ = h.shape
    proj = jnp.einsum('btd,de->bte', h, w_in)
    sizes = (POOL_WIDTH, N_HEADS * HEAD_DIM, N_KV_HEADS * HEAD_DIM, N_KV_HEADS * HEAD_DIM,
             N_IDX_HEADS * IDX_DIM, IDX_DIM, N_IDX_HEADS)
    cuts = [sum(sizes[:i + 1]) for i in range(len(sizes) - 1)]
    xp, q, k, v, qi, ki, wi = jnp.split(proj, cuts, axis=-1)
    q = partial_rope(q.reshape(B, T, N_HEADS, HEAD_DIM), pos)
    k = partial_rope(k.reshape(B, T, N_KV_HEADS, HEAD_DIM), pos)
    v = v.reshape(B, T, N_KV_HEADS, HEAD_DIM)
    qi = partial_rope(qi.reshape(B, T, N_IDX_HEADS, IDX_DIM), pos)
    ki = partial_rope(ki[:, :, None, :], pos)[:, :, 0, :]
    y_pool, new_hist = pool_mixer(xp, pool_hist, pos, pool_w, pool_scale)
    if past_k is None:
        k_all, v_all, ki_all = k, v, ki
    else:
        k_all = jnp.concatenate([past_k, k], axis=1)
        v_all = jnp.concatenate([past_v, v], axis=1)
        ki_all = jnp.concatenate([past_ki, ki], axis=1)
    y_att = dsa_mixer(q, qi, wi, pos, k_all, v_all, ki_all)
    y = jnp.einsum('btc,cd->btd', jnp.concatenate([y_pool, y_att], axis=-1), w_out)
    return y, k, v, ki, new_hist


def gmlp_mixer(h, w_in, ln_g, ln_b, ws, bs, w_out):
    B, T, _ = h.shape
    z = jax.nn.gelu(jnp.einsum('btd,de->bte', h, w_in))
    u, v = z[..., :GM_HALF], z[..., GM_HALF:]
    v = layernorm(v, ln_g, ln_b)
    if T % GM_CHUNK == 0:
        n, c = T // GM_CHUNK, GM_CHUNK
    else:
        n, c = 1, T
    mask = jnp.tril(jnp.ones((c, c), dtype=bool))
    wsm = jnp.where(mask[None], ws[:, :c, :c], jnp.zeros((), ws.dtype))
    vg = v.reshape(B, n, c, GM_GROUPS, GM_GROUP_DIM)
    mixed = jnp.einsum('gts,bnsgd->bntgd', wsm, vg) + jnp.transpose(bs[:, :c])[None, None, :, :, None]
    gated = u * mixed.reshape(B, T, GM_HALF)
    return jnp.einsum('btc,cd->btd', gated, w_out), v


def routed_experts(xt, expert_id, gates, w_gate, w_up, w_down):
    N, D = xt.shape
    A = N * TOPK_IN_GROUP
    flat_e = expert_id.reshape(-1).astype(jnp.int32)
    flat_tok = jnp.arange(A, dtype=jnp.int32) // TOPK_IN_GROUP
    order = jnp.argsort(flat_e)
    e_sorted = flat_e[order]
    tok_sorted = flat_tok[order]
    gate_sorted = gates.reshape(-1)[order]
    counts = jnp.bincount(flat_e, length=N_EXPERTS)
    padded = (counts + MOE_BLOCK - 1) // MOE_BLOCK * MOE_BLOCK
    pad_end = jnp.cumsum(padded)
    pad_start = pad_end - padded
    start = jnp.cumsum(counts) - counts
    dest = pad_start[e_sorted] + jnp.arange(A, dtype=jnp.int32) - start[e_sorted]
    n_blocks = -(-A // MOE_BLOCK) + N_EXPERTS
    n_rows = n_blocks * MOE_BLOCK
    row_tok = jnp.full((n_rows,), N, dtype=jnp.int32).at[dest].set(tok_sorted)
    x_rows = jnp.concatenate([xt, jnp.zeros((1, D), xt.dtype)], axis=0)[row_tok]
    block_start = jnp.arange(n_blocks, dtype=jnp.int32) * MOE_BLOCK
    block_e = jnp.minimum(jnp.searchsorted(pad_end, block_start, side='right'), N_EXPERTS - 1)

    def expert_block(args):
        xb, e = args
        return (jax.nn.silu(xb @ w_gate[e]) * (xb @ w_up[e])) @ w_down[e]

    y_rows = lax.map(expert_block, (x_rows.reshape(n_blocks, MOE_BLOCK, D), block_e)).reshape(n_rows, D)
    y_assign = y_rows[dest] * gate_sorted[:, None].astype(y_rows.dtype)
    return jnp.zeros_like(xt).at[tok_sorted].add(y_assign)


def hier_moe(h, w_rg, w_re, w_gate, w_up, w_down):
    B, T, D = h.shape
    xt = h.reshape(B * T, D)
    N = xt.shape[0]
    lg = jnp.einsum('nd,dg->ng', xt, w_rg, preferred_element_type=jnp.float32)
    g_sel = jnp.argmax(lg, axis=-1)
    p_grp = jnp.take_along_axis(jax.nn.softmax(lg, axis=-1), g_sel[:, None], axis=-1)[:, 0]
    le = jnp.einsum('nd,de->ne', xt, w_re, preferred_element_type=jnp.float32)
    le = le.reshape(N, N_EXPERT_GROUPS, EXPERTS_PER_GROUP)
    le_g = jnp.take_along_axis(le, g_sel[:, None, None], axis=1)[:, 0]
    top_v, top_j = lax.top_k(le_g, TOPK_IN_GROUP)
    gates = p_grp[:, None] * jax.nn.softmax(top_v, axis=-1)
    expert_id = g_sel[:, None].astype(jnp.int32) * EXPERTS_PER_GROUP + top_j.astype(jnp.int32)
    return routed_experts(xt, expert_id, gates, w_gate, w_up, w_down).reshape(B, T, D)


def setup_inputs(seed: int = 0) -> dict:
    key = jax.random.key(seed)
    ks = jax.random.split(key, 24)
    f32 = jnp.float32
    in_width = (POOL_WIDTH + N_HEADS * HEAD_DIM + 2 * N_KV_HEADS * HEAD_DIM
                + N_IDX_HEADS * IDX_DIM + IDX_DIM + N_IDX_HEADS)

    def nrm(k, shape, scale):
        return jax.random.normal(k, shape, f32) * scale

    return {
        "x_prompt": nrm(ks[0], (BATCH, SEQ, D_MODEL), 1.0),
        "x_sample": nrm(ks[1], (DEC_BATCH, DEC_SEQ, D_MODEL), 1.0),
        "cache_k": nrm(ks[2], (N_PAR_LAYERS, DEC_BATCH, PAST_LEN, N_KV_HEADS, HEAD_DIM), 1.0),
        "cache_v": nrm(ks[3], (N_PAR_LAYERS, DEC_BATCH, PAST_LEN, N_KV_HEADS, HEAD_DIM), 1.0),
        "cache_idx_k": nrm(ks[4], (N_PAR_LAYERS, DEC_BATCH, PAST_LEN, IDX_DIM), 1.0),
        "state_pool": nrm(ks[5], (N_PAR_LAYERS, DEC_BATCH, POOL_STATE, POOL_WIDTH), 1.0),
        "norm_mix": 1.0 + nrm(ks[6], (DEPTH, D_MODEL), 0.05),
        "norm_ffn": 1.0 + nrm(ks[7], (DEPTH, D_MODEL), 0.05),
        "norm_final": 1.0 + nrm(ks[8], (D_MODEL,), 0.05),
        "par_w_in": nrm(ks[9], (N_PAR_LAYERS, D_MODEL, in_width), D_MODEL ** -0.5),
        "par_pool_w": nrm(ks[10], (N_PAR_LAYERS, POOL_GROUPS, POOL_GROUP_DIM, POOL_GROUP_DIM), POOL_GROUP_DIM ** -0.5),
        "par_pool_scale": 1.0 + nrm(ks[11], (N_PAR_LAYERS, POOL_WIDTH), 0.1),
        "par_w_out": nrm(ks[12], (N_PAR_LAYERS, MIX_WIDTH, D_MODEL), MIX_WIDTH ** -0.5),
        "gm_w_in": nrm(ks[13], (N_GM_LAYERS, D_MODEL, 2 * GM_HALF), D_MODEL ** -0.5),
        "gm_ln_g": 1.0 + nrm(ks[14], (N_GM_LAYERS, GM_HALF), 0.05),
        "gm_ln_b": nrm(ks[15], (N_GM_LAYERS, GM_HALF), 0.02),
        "gm_ws": nrm(ks[16], (N_GM_LAYERS, GM_GROUPS, GM_CHUNK, GM_CHUNK), GM_CHUNK ** -0.5),
        "gm_bs": 1.0 + nrm(ks[17], (N_GM_LAYERS, GM_GROUPS, GM_CHUNK), 0.1),
        "gm_w_out": nrm(ks[18], (N_GM_LAYERS, GM_HALF, D_MODEL), GM_HALF ** -0.5),
        "moe_router_group": nrm(ks[19], (DEPTH, D_MODEL, N_EXPERT_GROUPS), D_MODEL ** -0.5),
        "moe_router_expert": nrm(ks[20], (DEPTH, D_MODEL, N_EXPERTS), D_MODEL ** -0.5),
        "moe_w_gate": nrm(ks[21], (DEPTH, N_EXPERTS, D_MODEL, D_EXPERT), D_MODEL ** -0.5),
        "moe_w_up": nrm(ks[22], (DEPTH, N_EXPERTS, D_MODEL, D_EXPERT), D_MODEL ** -0.5),
        "moe_w_down": nrm(ks[23], (DEPTH, N_EXPERTS, D_EXPERT, D_MODEL), D_EXPERT ** -0.5),
    }


def reference(x_prompt, x_sample, cache_k, cache_v, cache_idx_k, state_pool,
              norm_mix, norm_ffn, norm_final,
              par_w_in, par_pool_w, par_pool_scale, par_w_out,
              gm_w_in, gm_ln_g, gm_ln_b, gm_ws, gm_bs, gm_w_out,
              moe_router_group, moe_router_expert, moe_w_gate, moe_w_up, moe_w_down):
    Bp, Tp, _ = x_prompt.shape
    Bs, Ts, _ = x_sample.shape
    past_len = cache_k.shape[2]
    pos_p = jnp.arange(Tp, dtype=jnp.int32)
    pos_s = past_len + jnp.arange(Ts, dtype=jnp.int32)
    xp, xs = x_prompt, x_sample
    kp_l, vp_l, kip_l, poolp_l = [], [], [], []
    ks_l, vs_l, kis_l, pools_l, gmv_l = [], [], [], [], []
    for layer in range(DEPTH):
        j = layer // 2
        hp = rmsnorm(xp, norm_mix[layer])
        hs = rmsnorm(xs, norm_mix[layer])
        if layer % 2 == 0:
            zero_hist = jnp.zeros((Bp, POOL_STATE, POOL_WIDTH), hp.dtype)
            yp, kp, vp, kip, poolp = parallel_mixer(hp, pos_p, zero_hist, None, None, None,
                                                    par_w_in[j], par_pool_w[j], par_pool_scale[j], par_w_out[j])
            ys, ks_, vs_, kis, pools = parallel_mixer(hs, pos_s, state_pool[j], cache_k[j], cache_v[j], cache_idx_k[j],
                                                      par_w_in[j], par_pool_w[j], par_pool_scale[j], par_w_out[j])
            kp_l.append(kp); vp_l.append(vp); kip_l.append(kip); poolp_l.append(poolp)
            ks_l.append(ks_); vs_l.append(vs_); kis_l.append(kis); pools_l.append(pools)
        else:
            yp, _ = gmlp_mixer(hp, gm_w_in[j], gm_ln_g[j], gm_ln_b[j], gm_ws[j], gm_bs[j], gm_w_out[j])
            ys, gv = gmlp_mixer(hs, gm_w_in[j], gm_ln_g[j], gm_ln_b[j], gm_ws[j], gm_bs[j], gm_w_out[j])
            gmv_l.append(gv)
        xp = xp + yp
        xs = xs + ys
        xp = xp + hier_moe(rmsnorm(xp, norm_ffn[layer]), moe_router_group[layer], moe_router_expert[layer],
                           moe_w_gate[layer], moe_w_up[layer], moe_w_down[layer])
        xs = xs + hier_moe(rmsnorm(xs, norm_ffn[layer]), moe_router_group[layer], moe_router_expert[layer],
                           moe_w_gate[layer], moe_w_up[layer], moe_w_down[layer])
    y_prompt = rmsnorm(xp, norm_final)
    y_sample = rmsnorm(xs, norm_final)
    return (y_prompt, y_sample,
            jnp.stack(kp_l), jnp.stack(vp_l), jnp.stack(kip_l), jnp.stack(poolp_l),
            jnp.stack(ks_l), jnp.stack(vs_l), jnp.stack(kis_l), jnp.stack(pools_l),
            jnp.stack(gmv_l))
```

```python
import functools

import jax
import jax.numpy as jnp
from jax import lax
from jax.experimental import pallas as pl
from jax.experimental.pallas import tpu as pltpu

F32 = jnp.float32
BF16 = jnp.bfloat16
I32 = jnp.int32

LANES = 128
CHUNK = 64
POOL_WINDOWS = (2, 4, 8, 16)
POOL_GROUP_DIM = 128
POOL_WIDTH = 512
HIST_ROWS = 16
N_HEADS = 8
HEAD_DIM = 64
N_KV_HEADS = 4
Q_PER_KV = N_HEADS // N_KV_HEADS
ATT_WIDTH = N_HEADS * HEAD_DIM
KV_WIDTH = N_KV_HEADS * HEAD_DIM
N_IDX_HEADS = 8
IDX_DIM = 64
TOPK_MAX = 256
ROPE_THETA = 500000.0
ROT_HALF = HEAD_DIM // 8
GM_CHUNK = 128
GM_GROUPS = 8
N_EXPERT_GROUPS = 4
EXPERTS_PER_GROUP = 8
N_EXPERTS = 32
MOE_BLOCK = 128
RMS_EPS = 1e-6
LN_EPS = 1e-5

INT_MIN = -2147483648
NEG_BIG = -1e30
VMEM_LIMIT = 48 * 1024 * 1024

COL_XP, COL_Q, COL_K, COL_V, COL_QI, COL_KIWI = 0, 512, 1024, 1280, 1536, 2048
IN_WIDTH_PAD = 2176


def _rms(x, g):
    return x * lax.rsqrt(jnp.mean(x * x, axis=-1, keepdims=True) + RMS_EPS) * g


def _rope128(x, c, sa, sb):
    return x * c + pltpu.roll(x, LANES - ROT_HALF, 1) * sa + pltpu.roll(x, ROT_HALF, 1) * sb


def _inproj_kernel(x_ref, g_ref, w_ref, rope_ref, ropeki_ref, hist_ref, pw_ref, ps_ref,
                   q_ref, qi_ref, k_ref, v_ref, kiwi_ref, kbf_ref, vbf_ref, kibf_ref, yp_ref, state_ref,
                   buf_ref, *, tm, pos0):
    j = pl.program_id(1)
    h = _rms(x_ref[0], g_ref[...])
    proj = jnp.dot(h.astype(BF16), w_ref[...], preferred_element_type=F32)
    c, sa, sb = rope_ref[0], rope_ref[1], rope_ref[2]

    for i in range(ATT_WIDTH // LANES):
        sl = slice(i * LANES, (i + 1) * LANES)
        q_ref[0, :, sl] = _rope128(proj[:, COL_Q + i * LANES:COL_Q + (i + 1) * LANES], c, sa, sb).astype(BF16)
        qi_ref[0, :, sl] = _rope128(proj[:, COL_QI + i * LANES:COL_QI + (i + 1) * LANES], c, sa, sb).astype(BF16)
    for i in range(KV_WIDTH // LANES):
        sl = slice(i * LANES, (i + 1) * LANES)
        kr = _rope128(proj[:, COL_K + i * LANES:COL_K + (i + 1) * LANES], c, sa, sb)
        k_ref[0, :, sl] = kr
        kbf_ref[0, :, sl] = kr.astype(BF16)
    vv = proj[:, COL_V:COL_V + KV_WIDTH]
    v_ref[0] = vv
    vbf_ref[0] = vv.astype(BF16)
    kiwi = _rope128(proj[:, COL_KIWI:COL_KIWI + LANES], ropeki_ref[0], ropeki_ref[1], ropeki_ref[2])
    kiwi_ref[0] = kiwi
    kibf_ref[0] = kiwi[:, :IDX_DIM].astype(BF16)

    @pl.when(j == 0)
    def _():
        buf_ref[0:HIST_ROWS, :] = hist_ref[0]

    xp = proj[:, COL_XP:COL_XP + POOL_WIDTH]
    buf_ref[HIST_ROWS:HIST_ROWS + tm, :] = xp
    pos = pos0 + j * tm + lax.broadcasted_iota(I32, (tm, 1), 0)
    for gi, w in enumerate(POOL_WINDOWS):
        c0 = gi * POOL_GROUP_DIM
        s = xp[:, c0:c0 + POOL_GROUP_DIM]
        for i in range(1, w):
            s = s + buf_ref[HIST_ROWS - i:HIST_ROWS - i + tm, c0:c0 + POOL_GROUP_DIM]
        cnt = jnp.minimum(pos + 1, w).astype(F32)
        d = s / cnt - xp[:, c0:c0 + POOL_GROUP_DIM]
        y = jnp.dot(d.astype(BF16), pw_ref[gi], preferred_element_type=F32)
        yp_ref[0, :, c0:c0 + POOL_GROUP_DIM] = (y * ps_ref[:, c0:c0 + POOL_GROUP_DIM]).astype(BF16)
    tail = buf_ref[tm:tm + HIST_ROWS, :]
    state_ref[0] = tail
    buf_ref[0:HIST_ROWS, :] = tail


def _inproj(x, g, w_bf, rope, ropeki, hist, pw_bf, ps, *, tm, pos0):
    B, T, D = x.shape
    nt = T // tm
    f = lambda shape, dt: jax.ShapeDtypeStruct(shape, dt)
    out_shape = (
        f((B, T, ATT_WIDTH), BF16), f((B, T, ATT_WIDTH), BF16),
        f((B, T, KV_WIDTH), F32), f((B, T, KV_WIDTH), F32), f((B, T, LANES), F32),
        f((B, T, KV_WIDTH), BF16), f((B, T, KV_WIDTH), BF16), f((B, T, IDX_DIM), BF16),
        f((B, T, POOL_WIDTH), BF16), f((B, HIST_ROWS, POOL_WIDTH), F32),
    )
    tile = lambda wdt: pl.BlockSpec((1, tm, wdt), lambda b, j: (b, j, 0))
    const2 = lambda s: pl.BlockSpec(s, lambda b, j: (0, 0))
    in_specs = [
        tile(D), const2((1, D)), const2((D, IN_WIDTH_PAD)),
        pl.BlockSpec((3, tm, LANES), lambda b, j: (0, j, 0)),
        pl.BlockSpec((3, tm, LANES), lambda b, j: (0, j, 0)),
        pl.BlockSpec((1, HIST_ROWS, POOL_WIDTH), lambda b, j: (b, 0, 0)),
        pl.BlockSpec((len(POOL_WINDOWS), POOL_GROUP_DIM, POOL_GROUP_DIM), lambda b, j: (0, 0, 0)),
        const2((1, POOL_WIDTH)),
    ]
    out_specs = (
        tile(ATT_WIDTH), tile(ATT_WIDTH), tile(KV_WIDTH), tile(KV_WIDTH), tile(LANES),
        tile(KV_WIDTH), tile(KV_WIDTH), tile(IDX_DIM), tile(POOL_WIDTH),
        pl.BlockSpec((1, HIST_ROWS, POOL_WIDTH), lambda b, j: (b, 0, 0)),
    )
    return pl.pallas_call(
        functools.partial(_inproj_kernel, tm=tm, pos0=pos0),
        out_shape=out_shape, grid=(B, nt), in_specs=in_specs, out_specs=out_specs,
        scratch_shapes=[pltpu.VMEM((HIST_ROWS + tm, POOL_WIDTH), F32)],
        compiler_params=pltpu.CompilerParams(dimension_semantics=("parallel", "arbitrary"),
                                             vmem_limit_bytes=VMEM_LIMIT),
        name="inproj",
    )(x, g, w_bf, rope, ropeki, hist, pw_bf, ps)


def _dsa_kernel(q_ref, qi_ref, kiwi_ref, k_ref, v_ref, ki_ref, x_ref, yp_ref, wo_ref,
                o_ref, key_buf, bias_buf, m_scr, l_scr, acc_scr,
                *, tq, tk, pos0, n_keys, n_sel):
    j = pl.program_id(1)
    base = pos0 + j * tq
    pos = base + lax.broadcasted_iota(I32, (tq, 1), 0)
    limit = jnp.minimum((pos // CHUNK + 1) * CHUNK, n_keys)
    limit_max = jnp.minimum(((base + tq - 1) // CHUNK + 1) * CHUNK, n_keys)
    nkb = (limit_max + tk - 1) // tk
    nsub = tk // LANES
    nt = (((1,), (1,)), ((), ()))

    qi = qi_ref[0]
    wi = kiwi_ref[0][:, IDX_DIM:IDX_DIM + N_IDX_HEADS]
    qi_h = [qi[:, h * IDX_DIM:(h + 1) * IDX_DIM] for h in range(N_IDX_HEADS)]
    wi_h = [wi[:, h:h + 1] for h in range(N_IDX_HEADS)]

    def score_block(kb, carry):
        off = pl.multiple_of(kb * tk, tk)
        kiblk = ki_ref[0, pl.ds(off, tk), :]
        idx = jnp.zeros((tq, tk), F32)
        for h in range(N_IDX_HEADS):
            sc = lax.dot_general(qi_h[h], kiblk, nt, preferred_element_type=F32)
            idx = idx + jnp.maximum(sc, 0.0) * wi_h[h]
        idx = jnp.where(idx == 0.0, 0.0, idx)
        bits = lax.bitcast_convert_type(idx, I32)
        key = bits ^ ((bits >> 31) & 0x7FFFFFFF)
        kidx = kb * tk + lax.broadcasted_iota(I32, (tq, tk), 1)
        key_buf[kb] = jnp.where(kidx < limit, key, INT_MIN)
        return carry

    lax.fori_loop(0, nkb, score_block, 0)

    def count_ge(cand):
        def body(kb, acc):
            kblk = key_buf[kb]
            for c in range(nsub):
                acc = acc + jnp.where(kblk[:, c * LANES:(c + 1) * LANES] >= cand, 1.0, 0.0)
            return acc
        acc = lax.fori_loop(0, nkb, body, jnp.zeros((tq, LANES), F32))
        return jnp.sum(acc, axis=1, keepdims=True)

    kf = float(n_sel)
    t0 = jnp.where(count_ge(jnp.zeros((tq, 1), I32)) >= kf, 0, INT_MIN).astype(I32)

    def bit_body(i, t):
        cand = t | lax.shift_left(jnp.int32(1), 30 - i)
        return jnp.where(count_ge(cand) >= kf, cand, t)

    t = lax.fori_loop(0, 31, bit_body, t0)

    def count_gt_ge(t):
        def body(kb, accs):
            a_gt, a_ge = accs
            kblk = key_buf[kb]
            for c in range(nsub):
                blk = kblk[:, c * LANES:(c + 1) * LANES]
                a_gt = a_gt + jnp.where(blk > t, 1.0, 0.0)
                a_ge = a_ge + jnp.where(blk >= t, 1.0, 0.0)
            return a_gt, a_ge
        z = jnp.zeros((tq, LANES), F32)
        a_gt, a_ge = lax.fori_loop(0, nkb, body, (z, z))
        return jnp.sum(a_gt, axis=1, keepdims=True), jnp.sum(a_ge, axis=1, keepdims=True)

    cnt_gt, cnt_ge = count_gt_ge(t)
    need = kf - cnt_gt
    cnt_eq = cnt_ge - cnt_gt
    overfull = jnp.where(t != INT_MIN, cnt_eq - need, 0.0)
    slow = jnp.max(overfull) > 0.0

    @pl.when(jnp.logical_not(slow))
    def _():
        t_adm = jnp.maximum(t, INT_MIN + 1)

        def body(kb, carry):
            bias_buf[kb] = jnp.where(key_buf[kb] >= t_adm, 0.0, NEG_BIG)
            return carry
        lax.fori_loop(0, nkb, body, 0)

    @pl.when(slow)
    def _():
        tri = jnp.where(lax.broadcasted_iota(I32, (tk, tk), 0) <= lax.broadcasted_iota(I32, (tk, tk), 1),
                        1.0, 0.0).astype(BF16)

        def body(kb, seen):
            kblk = key_buf[kb]
            eq = jnp.where((kblk == t) & (kblk != INT_MIN), 1.0, 0.0)
            prefix = jnp.dot(eq.astype(BF16), tri, preferred_element_type=F32) + seen
            keep_tie = jnp.where(prefix <= need, eq, 0.0)
            sel = jnp.where(kblk > t, 1.0, keep_tie)
            bias_buf[kb] = jnp.where(sel > 0.0, 0.0, NEG_BIG)
            return seen + jnp.sum(eq, axis=1, keepdims=True)
        lax.fori_loop(0, nkb, body, jnp.zeros((tq, 1), F32))

    q = q_ref[0] * (HEAD_DIM ** -0.5)
    q_h = [q[:, h * HEAD_DIM:(h + 1) * HEAD_DIM] for h in range(N_HEADS)]
    m_scr[...] = jnp.full(m_scr.shape, NEG_BIG, F32)
    l_scr[...] = jnp.zeros(l_scr.shape, F32)
    acc_scr[...] = jnp.zeros(acc_scr.shape, F32)

    def attn_block(kb, carry):
        off = pl.multiple_of(kb * tk, tk)
        kblk = k_ref[0, pl.ds(off, tk), :]
        vblk = v_ref[0, pl.ds(off, tk), :]
        bias = bias_buf[kb]
        for h in range(N_HEADS):
            g = h // Q_PER_KV
            s = lax.dot_general(q_h[h], kblk[:, g * HEAD_DIM:(g + 1) * HEAD_DIM], nt,
                                preferred_element_type=F32) + bias
            m_prev = m_scr[h]
            m_new = jnp.maximum(m_prev, jnp.max(s, axis=1, keepdims=True))
            alpha = jnp.exp(m_prev - m_new)
            p = jnp.exp(s - m_new)
            l_scr[h] = alpha * l_scr[h] + jnp.sum(p, axis=1, keepdims=True)
            acc_scr[h] = alpha * acc_scr[h] + jnp.dot(p.astype(BF16), vblk[:, g * HEAD_DIM:(g + 1) * HEAD_DIM],
                                                      preferred_element_type=F32)
            m_scr[h] = m_new
        return carry

    lax.fori_loop(0, nkb, attn_block, 0)

    y = jnp.dot(yp_ref[0], wo_ref[0:POOL_WIDTH, :], preferred_element_type=F32)
    for h in range(N_HEADS):
        o_h = (acc_scr[h] / l_scr[h]).astype(BF16)
        r0 = POOL_WIDTH + h * HEAD_DIM
        y = y + jnp.dot(o_h, wo_ref[r0:r0 + HEAD_DIM, :], preferred_element_type=F32)
    o_ref[0] = x_ref[0] + y


def _dsa(q, qi, kiwi, k_all, v_all, ki_all, x, yp, wo_bf, *, tq, tk, pos0, n_keys, n_sel):
    B, T, D = x.shape
    L = k_all.shape[1]
    assert L % tk == 0 and T % tq == 0
    nkb_max = L // tk
    tile = lambda wdt: pl.BlockSpec((1, tq, wdt), lambda b, j: (b, j, 0))
    seq = lambda wdt: pl.BlockSpec((1, L, wdt), lambda b, j: (b, 0, 0))
    in_specs = [tile(ATT_WIDTH), tile(ATT_WIDTH), tile(LANES), seq(KV_WIDTH), seq(KV_WIDTH), seq(IDX_DIM),
                tile(D), tile(POOL_WIDTH), pl.BlockSpec((POOL_WIDTH + ATT_WIDTH, D), lambda b, j: (0, 0))]
    return pl.pallas_call(
        functools.partial(_dsa_kernel, tq=tq, tk=tk, pos0=pos0, n_keys=n_keys, n_sel=n_sel),
        out_shape=jax.ShapeDtypeStruct((B, T, D), F32), grid=(B, T // tq),
        in_specs=in_specs, out_specs=tile(D),
        scratch_shapes=[pltpu.VMEM((nkb_max, tq, tk), I32), pltpu.VMEM((nkb_max, tq, tk), F32),
                        pltpu.VMEM((N_HEADS, tq, 1), F32), pltpu.VMEM((N_HEADS, tq, 1), F32),
                        pltpu.VMEM((N_HEADS, tq, HEAD_DIM), F32)],
        compiler_params=pltpu.CompilerParams(dimension_semantics=("parallel", "arbitrary"),
                                             vmem_limit_bytes=VMEM_LIMIT),
        name="dsa",
    )(q, qi, kiwi, k_all, v_all, ki_all, x, yp, wo_bf)


def _gmlp_kernel(x_ref, g_ref, win_ref, lng_ref, lnb_ref, ws_ref, bias_ref, wout_ref, o_ref, v_ref, *, tm):
    x = x_ref[...]
    h = _rms(x, g_ref[...])
    z = jax.nn.gelu(jnp.dot(h.astype(BF16), win_ref[...], preferred_element_type=F32))
    half = z.shape[1] // 2
    u, v = z[:, :half], z[:, half:]
    mu = jnp.mean(v, axis=-1, keepdims=True)
    var = jnp.mean(jnp.square(v - mu), axis=-1, keepdims=True)
    vn = (v - mu) * lax.rsqrt(var + LN_EPS) * lng_ref[...] + lnb_ref[...]
    v_ref[...] = vn
    gd = half // GM_GROUPS
    gated = []
    for c in range(tm // GM_CHUNK):
        rows = slice(c * GM_CHUNK, (c + 1) * GM_CHUNK)
        vc = vn[rows].astype(BF16)
        mixed = jnp.concatenate(
            [jnp.dot(ws_ref[0, g], vc[:, g * gd:(g + 1) * gd], preferred_element_type=F32)
             for g in range(GM_GROUPS)], axis=1) + bias_ref[0]
        gated.append((u[rows] * mixed).astype(BF16))
    gated = jnp.concatenate(gated, axis=0)
    o_ref[...] = x + jnp.dot(gated, wout_ref[...], preferred_element_type=F32)


def _gmlp(x, g, win_bf, lng, lnb, ws2, bias2, wout_bf, *, tm, n_first, n_v_rows):
    N, D = x.shape
    half = win_bf.shape[1] // 2
    nt = N // tm
    t_first = n_first // tm
    variant = lambda i: jnp.where(i >= t_first, 1, 0)
    row = pl.BlockSpec((tm, D), lambda i: (i, 0))
    const = lambda s: pl.BlockSpec(s, lambda i: (0, 0))
    in_specs = [row, const((1, D)), const((D, 2 * half)), const((1, half)), const((1, half)),
                pl.BlockSpec((1, GM_GROUPS, GM_CHUNK, GM_CHUNK), lambda i: (variant(i), 0, 0, 0)),
                pl.BlockSpec((1, GM_CHUNK, half), lambda i: (variant(i), 0, 0)),
                const((half, D))]
    out_specs = (row, pl.BlockSpec((tm, half), lambda i: (jnp.maximum(i - t_first, 0), 0)))
    return pl.pallas_call(
        functools.partial(_gmlp_kernel, tm=tm),
        out_shape=(jax.ShapeDtypeStruct((N, D), F32), jax.ShapeDtypeStruct((n_v_rows, half), F32)),
        grid=(nt,), in_specs=in_specs, out_specs=out_specs,
        compiler_params=pltpu.CompilerParams(dimension_semantics=("arbitrary",), vmem_limit_bytes=VMEM_LIMIT),
        name="gmlp",
    )(x, g, win_bf, lng, lnb, ws2, bias2, wout_bf)


ROUTE_E0, ROUTE_E1, ROUTE_R0, ROUTE_R1, ROUTE_G0, ROUTE_G1 = range(6)


def _router_kernel(x_ref, g_ref, wr_ref, h_ref, route_ref, cnt_ref, carry_ref, *, tm):
    i = pl.program_id(0)

    @pl.when(i == 0)
    def _():
        carry_ref[...] = jnp.zeros(carry_ref.shape, F32)

    h = _rms(x_ref[...], g_ref[...])
    h_ref[...] = h.astype(BF16)
    logits = jnp.dot(h, wr_ref[...], preferred_element_type=F32, precision=lax.Precision.HIGHEST)
    lane = lax.broadcasted_iota(I32, (tm, LANES), 1).astype(F32)
    ninf = -jnp.inf
    big = 1e9
    rmax = lambda a: jnp.max(a, axis=1, keepdims=True)
    rmin = lambda a: jnp.min(a, axis=1, keepdims=True)
    rsum = lambda a: jnp.sum(a, axis=1, keepdims=True)

    is_grp = lane < N_EXPERT_GROUPS
    lg = jnp.where(is_grp, logits, ninf)
    mg = rmax(lg)
    g_sel = rmin(jnp.where(lg == mg, lane, big))
    p_grp = 1.0 / rsum(jnp.where(is_grp, jnp.exp(lg - mg), 0.0))
    lo = N_EXPERT_GROUPS + g_sel * EXPERTS_PER_GROUP
    le = jnp.where((lane >= lo) & (lane < lo + EXPERTS_PER_GROUP), logits, ninf)
    v1 = rmax(le)
    j1 = rmin(jnp.where(le == v1, lane, big))
    le2 = jnp.where(lane == j1, ninf, le)
    v2 = rmax(le2)
    j2 = rmin(jnp.where(le2 == v2, lane, big))
    e0 = j1 - N_EXPERT_GROUPS
    e1 = j2 - N_EXPERT_GROUPS
    r = jnp.exp(v2 - v1)
    g0 = p_grp / (1.0 + r)
    g1 = p_grp * r / (1.0 + r)

    oh0 = jnp.where(lane == e0, 1.0, 0.0)
    oh1 = jnp.where(lane == e1, 1.0, 0.0)
    oh = oh0 + oh1
    lower = jnp.where(lax.broadcasted_iota(I32, (tm, tm), 1) < lax.broadcasted_iota(I32, (tm, tm), 0),
                      1.0, 0.0).astype(BF16)
    before = jnp.dot(lower, oh.astype(BF16), preferred_element_type=F32) + carry_ref[...]
    r0 = rsum(oh0 * before)
    r1 = rsum(oh1 * before)
    carry_ref[...] = carry_ref[...] + jnp.sum(oh, axis=0, keepdims=True)
    cnt_ref[...] = carry_ref[...]

    route = jnp.zeros((tm, LANES), F32)
    for col, val in ((ROUTE_E0, e0), (ROUTE_E1, e1), (ROUTE_R0, r0), (ROUTE_R1, r1), (ROUTE_G0, g0), (ROUTE_G1, g1)):
        route = jnp.where(lane == col, val, route)
    route_ref[...] = route


def _router(x, g, wr, *, tm):
    N, D = x.shape
    row = lambda wdt: pl.BlockSpec((tm, wdt), lambda i: (i, 0))
    const = lambda s: pl.BlockSpec(s, lambda i: (0, 0))
    return pl.pallas_call(
        functools.partial(_router_kernel, tm=tm),
        out_shape=(jax.ShapeDtypeStruct((N, D), BF16), jax.ShapeDtypeStruct((N, LANES), F32),
                   jax.ShapeDtypeStruct((1, LANES), F32)),
        grid=(N // tm,), in_specs=[row(D), const((1, D)), const((D, LANES))],
        out_specs=(row(D), row(LANES), const((1, LANES))),
        scratch_shapes=[pltpu.VMEM((1, LANES), F32)],
        compiler_params=pltpu.CompilerParams(dimension_semantics=("arbitrary",), vmem_limit_bytes=VMEM_LIMIT),
        name="router",
    )(x, g, wr)


def _expert_kernel(be_ref, nu_ref, x_ref, wg_ref, wu_ref, wd_ref, y_ref):
    i = pl.program_id(0)

    @pl.when(i < nu_ref[0])
    def _():
        x = x_ref[...]
        a = jnp.dot(x, wg_ref[0].astype(BF16), preferred_element_type=F32)
        u = jnp.dot(x, wu_ref[0].astype(BF16), preferred_element_type=F32)
        act = (a * jax.nn.sigmoid(a)) * u
        y_ref[...] = jnp.dot(act.astype(BF16), wd_ref[0].astype(BF16), preferred_element_type=F32)

    @pl.when(i >= nu_ref[0])
    def _():
        y_ref[...] = jnp.zeros(y_ref.shape, F32)


def _experts(block_e, n_used, x_rows, w_gate, w_up, w_down):
    n_rows, D = x_rows.shape
    de = w_gate.shape[2]
    n_blocks = n_rows // MOE_BLOCK
    grid_spec = pltpu.PrefetchScalarGridSpec(
        num_scalar_prefetch=2, grid=(n_blocks,),
        in_specs=[pl.BlockSpec((MOE_BLOCK, D), lambda i, be, nu: (i, 0)),
                  pl.BlockSpec((1, D, de), lambda i, be, nu: (be[i], 0, 0)),
                  pl.BlockSpec((1, D, de), lambda i, be, nu: (be[i], 0, 0)),
                  pl.BlockSpec((1, de, D), lambda i, be, nu: (be[i], 0, 0))],
        out_specs=pl.BlockSpec((MOE_BLOCK, D), lambda i, be, nu: (i, 0)))
    return pl.pallas_call(
        _expert_kernel, out_shape=jax.ShapeDtypeStruct((n_rows, D), F32), grid_spec=grid_spec,
        compiler_params=pltpu.CompilerParams(dimension_semantics=("arbitrary",), vmem_limit_bytes=VMEM_LIMIT),
        name="experts",
    )(block_e, n_used, x_rows, w_gate, w_up, w_down)


def _combine_kernel(x_ref, y0_ref, y1_ref, route_ref, g_ref, o_ref, *, final_norm):
    route = route_ref[...]
    g0 = route[:, ROUTE_G0:ROUTE_G0 + 1]
    g1 = route[:, ROUTE_G1:ROUTE_G1 + 1]
    out = x_ref[...] + (y0_ref[...] * g0 + y1_ref[...] * g1)
    if final_norm:
        out = _rms(out, g_ref[...])
    o_ref[...] = out


def _combine(x, y0, y1, route, g, *, tm, final_norm):
    N, D = x.shape
    row = lambda wdt: pl.BlockSpec((tm, wdt), lambda i: (i, 0))
    return pl.pallas_call(
        functools.partial(_combine_kernel, final_norm=final_norm),
        out_shape=jax.ShapeDtypeStruct((N, D), F32), grid=(N // tm,),
        in_specs=[row(D), row(D), row(D), row(LANES), pl.BlockSpec((1, D), lambda i: (0, 0))],
        out_specs=row(D),
        compiler_params=pltpu.CompilerParams(dimension_semantics=("parallel",), vmem_limit_bytes=VMEM_LIMIT),
        name="combine",
    )(x, y0, y1, route, g)


def _hier_moe(x, g_ffn, w_rg, w_re, w_gate, w_up, w_down, g_final, *, final_norm):
    N, D = x.shape
    wr = jnp.concatenate([w_rg, w_re, jnp.zeros((D, LANES - N_EXPERT_GROUPS - N_EXPERTS), F32)], axis=1)
    h_bf, route, counts = _router(x, g_ffn.reshape(1, D), wr, tm=256)
    counts = counts[0, :N_EXPERTS].astype(I32)
    eid = route[:, ROUTE_E0:ROUTE_E1 + 1].astype(I32)
    rank = route[:, ROUTE_R0:ROUTE_R1 + 1].astype(I32)
    padded = (counts + MOE_BLOCK - 1) // MOE_BLOCK * MOE_BLOCK
    pad_end = jnp.cumsum(padded)
    pad_start = pad_end - padded
    dest = pad_start[eid] + rank
    n_blocks = -(-(2 * N) // MOE_BLOCK) + N_EXPERTS
    n_rows = n_blocks * MOE_BLOCK
    n_used = (pad_end[-1] // MOE_BLOCK).astype(I32)
    block_start = jnp.arange(n_blocks, dtype=I32) * MOE_BLOCK
    block_e = jnp.minimum(jnp.searchsorted(pad_end, block_start, side='right'), N_EXPERTS - 1).astype(I32)
    last_e = block_e[jnp.maximum(n_used - 1, 0)]
    block_e = jnp.where(jnp.arange(n_blocks) < n_used, block_e, last_e)
    tok = jnp.broadcast_to(jnp.arange(N, dtype=I32)[:, None], (N, 2))
    row_tok = jnp.zeros((n_rows,), I32).at[dest.reshape(-1)].set(tok.reshape(-1))
    x_rows = h_bf[row_tok]
    y_rows = _experts(block_e, n_used.reshape(1), x_rows, w_gate, w_up, w_down)
    y0 = y_rows[dest[:, 0]]
    y1 = y_rows[dest[:, 1]]
    return _combine(x, y0, y1, route, g_final.reshape(1, D), tm=256, final_norm=final_norm)


def _rope_tables(pos, rot_lanes):
    inv = 1.0 / (ROPE_THETA ** (jnp.arange(ROT_HALF, dtype=F32) / ROT_HALF))
    ang = pos.astype(F32)[:, None] * inv[None, :]
    cos, sin = jnp.cos(ang), jnp.sin(ang)
    lane = jnp.arange(LANES)
    r = lane % HEAD_DIM
    active = (lane < rot_lanes)
    first = active & (r < ROT_HALF)
    second = active & (r >= ROT_HALF) & (r < 2 * ROT_HALF)
    cos_l = cos[:, r % ROT_HALF]
    sin_l = sin[:, r % ROT_HALF]
    c = jnp.where((first | second)[None, :], cos_l, 1.0)
    sa = jnp.where(first[None, :], -sin_l, 0.0)
    sb = jnp.where(second[None, :], sin_l, 0.0)
    return jnp.stack([c, sa, sb]).astype(F32)


def _prep_w_in(w_in):
    D = w_in.shape[0]
    xp, q, k, v, qi, ki, wi = jnp.split(w_in, [512, 1024, 1280, 1536, 2048, 2112], axis=1)
    pad = jnp.zeros((D, IN_WIDTH_PAD - w_in.shape[1]), w_in.dtype)
    return jnp.concatenate([xp, q, k, v, qi, ki, wi, pad], axis=1).astype(BF16)


def _pad_keys(a, l_pad):
    return jnp.pad(a, ((0, 0), (0, l_pad - a.shape[1]), (0, 0)))


def kernel(x_prompt, x_sample, cache_k, cache_v, cache_idx_k, state_pool, norm_mix, norm_ffn, norm_final,
           par_w_in, par_pool_w, par_pool_scale, par_w_out, gm_w_in, gm_ln_g, gm_ln_b, gm_ws, gm_bs, gm_w_out,
           moe_router_group, moe_router_expert, moe_w_gate, moe_w_up, moe_w_down):
    Bp, Tp, D = x_prompt.shape
    Bs, Ts, _ = x_sample.shape
    past = cache_k.shape[2]
    Np, Ns = Bp * Tp, Bs * Ts
    depth = norm_mix.shape[0]
    assert depth == 2 and Ts == CHUNK and Tp % 256 == 0 and Ns % 256 == 0

    w_in_bf = _prep_w_in(par_w_in[0])
    pw_bf = par_pool_w[0].astype(BF16)
    ps = par_pool_scale[0].reshape(1, POOL_WIDTH)
    wo_bf = par_w_out[0].astype(BF16)
    g_mix0 = norm_mix[0].reshape(1, D)
    pos_p = jnp.arange(Tp, dtype=I32)
    pos_s = past + jnp.arange(Ts, dtype=I32)

    hist_p = jnp.zeros((Bp, HIST_ROWS, POOL_WIDTH), F32)
    hist_s = jnp.pad(state_pool[0], ((0, 0), (1, 0), (0, 0)))
    tk = 512
    (q_p, qi_p, k_p, v_p, kiwi_p, kbf_p, vbf_p, kibf_p, yp_p, st_p) = _inproj(
        x_prompt, g_mix0, w_in_bf, _rope_tables(pos_p, LANES), _rope_tables(pos_p, IDX_DIM), hist_p, pw_bf, ps,
        tm=256, pos0=0)
    (q_s, qi_s, k_s, v_s, kiwi_s, kbf_s, vbf_s, kibf_s, yp_s, st_s) = _inproj(
        x_sample, g_mix0, w_in_bf, _rope_tables(pos_s, LANES), _rope_tables(pos_s, IDX_DIM), hist_s, pw_bf, ps,
        tm=Ts, pos0=past)

    lp = -(-Tp // tk) * tk
    x1_p = _dsa(q_p, qi_p, kiwi_p, _pad_keys(kbf_p, lp), _pad_keys(vbf_p, lp), _pad_keys(kibf_p, lp),
                x_prompt, yp_p, wo_bf, tq=128, tk=tk, pos0=0, n_keys=Tp, n_sel=min(TOPK_MAX, Tp // 4))
    ls = past + Ts
    lsp = -(-ls // tk) * tk
    kall = _pad_keys(jnp.concatenate([cache_k[0].reshape(Bs, past, KV_WIDTH).astype(BF16), kbf_s], axis=1), lsp)
    vall = _pad_keys(jnp.concatenate([cache_v[0].reshape(Bs, past, KV_WIDTH).astype(BF16), vbf_s], axis=1), lsp)
    kiall = _pad_keys(jnp.concatenate([cache_idx_k[0].astype(BF16), kibf_s], axis=1), lsp)
    x1_s = _dsa(q_s, qi_s, kiwi_s, kall, vall, kiall, x_sample, yp_s, wo_bf,
                tq=Ts, tk=tk, pos0=past, n_keys=ls, n_sel=min(TOPK_MAX, ls // 4))

    x = jnp.concatenate([x1_p.reshape(Np, D), x1_s.reshape(Ns, D)], axis=0)
    x = _hier_moe(x, norm_ffn[0], moe_router_group[0], moe_router_expert[0],
                  moe_w_gate[0], moe_w_up[0], moe_w_down[0], norm_final, final_norm=False)

    cs = Ts
    tril = lambda n: jnp.tril(jnp.ones((n, n), bool))
    ws_p = jnp.where(tril(GM_CHUNK)[None], gm_ws[0], 0.0)
    ws_small = jnp.where(tril(cs)[None], gm_ws[0][:, :cs, :cs], 0.0)
    rep = GM_CHUNK // cs
    ws_s = jnp.einsum('ab,gts->gatbs', jnp.eye(rep, dtype=F32), ws_small).reshape(GM_GROUPS, GM_CHUNK, GM_CHUNK)
    ws2 = jnp.stack([ws_p, ws_s]).astype(BF16)
    gd = D // GM_GROUPS
    bias_p = jnp.repeat(jnp.transpose(gm_bs[0]), gd, axis=1)
    bias_s = jnp.tile(jnp.repeat(jnp.transpose(gm_bs[0][:, :cs]), gd, axis=1), (rep, 1))
    bias2 = jnp.stack([bias_p, bias_s])
    x, gm_v = _gmlp(x, norm_mix[1].reshape(1, D), gm_w_in[0].astype(BF16), gm_ln_g[0].reshape(1, D),
                    gm_ln_b[0].reshape(1, D), ws2, bias2, gm_w_out[0].astype(BF16),
                    tm=256, n_first=Np, n_v_rows=Ns)
    x = _hier_moe(x, norm_ffn[1], moe_router_group[1], moe_router_expert[1],
                  moe_w_gate[1], moe_w_up[1], moe_w_down[1], norm_final, final_norm=True)

    y_p = x[:Np].reshape(Bp, Tp, D)
    y_s = x[Np:].reshape(Bs, Ts, D)
    r4 = lambda a, b, t: a.reshape(1, b, t, N_KV_HEADS, HEAD_DIM)
    return (y_p, y_s,
            r4(k_p, Bp, Tp), r4(v_p, Bp, Tp), kiwi_p[:, :, :IDX_DIM][None], st_p[:, 1:][None],
            r4(k_s, Bs, Ts), r4(v_s, Bs, Ts), kiwi_s[:, :, :IDX_DIM][None], st_s[:, 1:][None],
            gm_v.reshape(1, Bs, Ts, D))
```

```python
import functools

import jax
import jax.numpy as jnp
from jax import lax
from jax.experimental import pallas as pl
from jax.experimental.pallas import tpu as pltpu

F32 = jnp.float32
BF16 = jnp.bfloat16
I32 = jnp.int32

LANES = 128
CHUNK = 64
POOL_WINDOWS = (2, 4, 8, 16)
POOL_GROUP_DIM = 128
POOL_WIDTH = 512
HIST_ROWS = 16
N_HEADS = 8
HEAD_DIM = 64
N_KV_HEADS = 4
Q_PER_KV = N_HEADS // N_KV_HEADS
ATT_WIDTH = N_HEADS * HEAD_DIM
KV_WIDTH = N_KV_HEADS * HEAD_DIM
N_IDX_HEADS = 8
IDX_DIM = 64
TOPK_MAX = 256
ROPE_THETA = 500000.0
ROT_HALF = HEAD_DIM // 8
GM_CHUNK = 128
GM_GROUPS = 8
N_EXPERT_GROUPS = 4
EXPERTS_PER_GROUP = 8
N_EXPERTS = 32
MOE_BLOCK = 128
RMS_EPS = 1e-6
LN_EPS = 1e-5

INT_MIN = -2147483648
NEG_BIG = -1e30
VMEM_LIMIT = 48 * 1024 * 1024

COL_XP, COL_Q, COL_K, COL_V, COL_QI, COL_KIWI = 0, 512, 1024, 1280, 1536, 2048
IN_WIDTH_PAD = 2176


def _rms(x, g):
    return x * lax.rsqrt(jnp.mean(x * x, axis=-1, keepdims=True) + RMS_EPS) * g


def _rope128(x, c, sa, sb):
    return x * c + pltpu.roll(x, LANES - ROT_HALF, 1) * sa + pltpu.roll(x, ROT_HALF, 1) * sb


def _inproj_kernel(x_ref, g_ref, w_ref, rope_ref, ropeki_ref, hist_ref, pw_ref, ps_ref,
                   q_ref, qi_ref, k_ref, v_ref, kiwi_ref, kbf_ref, vbf_ref, kibf_ref, yp_ref, state_ref,
                   buf_ref, *, tm, pos0):
    j = pl.program_id(1)
    h = _rms(x_ref[0], g_ref[...])
    proj = jnp.dot(h.astype(BF16), w_ref[...], preferred_element_type=F32)
    c, sa, sb = rope_ref[0], rope_ref[1], rope_ref[2]

    for i in range(ATT_WIDTH // LANES):
        sl = slice(i * LANES, (i + 1) * LANES)
        q_ref[0, :, sl] = (_rope128(proj[:, COL_Q + i * LANES:COL_Q + (i + 1) * LANES], c, sa, sb)
                           * (HEAD_DIM ** -0.5)).astype(BF16)
        qi_ref[0, :, sl] = _rope128(proj[:, COL_QI + i * LANES:COL_QI + (i + 1) * LANES], c, sa, sb).astype(BF16)
    for i in range(KV_WIDTH // LANES):
        sl = slice(i * LANES, (i + 1) * LANES)
        kr = _rope128(proj[:, COL_K + i * LANES:COL_K + (i + 1) * LANES], c, sa, sb)
        k_ref[0, :, sl] = kr
        kbf_ref[0, :, sl] = kr.astype(BF16)
    vv = proj[:, COL_V:COL_V + KV_WIDTH]
    v_ref[0] = vv
    vbf_ref[0] = vv.astype(BF16)
    kiwi = _rope128(proj[:, COL_KIWI:COL_KIWI + LANES], ropeki_ref[0], ropeki_ref[1], ropeki_ref[2])
    kiwi_ref[0] = kiwi
    kibf_ref[0] = kiwi[:, :IDX_DIM].astype(BF16)

    @pl.when(j == 0)
    def _():
        buf_ref[0:HIST_ROWS, :] = hist_ref[0]

    xp = proj[:, COL_XP:COL_XP + POOL_WIDTH]
    buf_ref[HIST_ROWS:HIST_ROWS + tm, :] = xp
    pos = pos0 + j * tm + lax.broadcasted_iota(I32, (tm, 1), 0)
    for gi, w in enumerate(POOL_WINDOWS):
        c0 = gi * POOL_GROUP_DIM
        s = xp[:, c0:c0 + POOL_GROUP_DIM]
        for i in range(1, w):
            s = s + buf_ref[HIST_ROWS - i:HIST_ROWS - i + tm, c0:c0 + POOL_GROUP_DIM]
        cnt = jnp.minimum(pos + 1, w).astype(F32)
        d = s / cnt - xp[:, c0:c0 + POOL_GROUP_DIM]
        y = jnp.dot(d.astype(BF16), pw_ref[gi], preferred_element_type=F32)
        yp_ref[0, :, c0:c0 + POOL_GROUP_DIM] = (y * ps_ref[:, c0:c0 + POOL_GROUP_DIM]).astype(BF16)
    tail = buf_ref[tm:tm + HIST_ROWS, :]
    state_ref[0] = tail
    buf_ref[0:HIST_ROWS, :] = tail


def _inproj(x, g, w_bf, rope, ropeki, hist, pw_bf, ps, *, tm, pos0):
    B, T, D = x.shape
    nt = T // tm
    f = lambda shape, dt: jax.ShapeDtypeStruct(shape, dt)
    out_shape = (
        f((B, T, ATT_WIDTH), BF16), f((B, T, ATT_WIDTH), BF16),
        f((B, T, KV_WIDTH), F32), f((B, T, KV_WIDTH), F32), f((B, T, LANES), F32),
        f((B, T, KV_WIDTH), BF16), f((B, T, KV_WIDTH), BF16), f((B, T, IDX_DIM), BF16),
        f((B, T, POOL_WIDTH), BF16), f((B, HIST_ROWS, POOL_WIDTH), F32),
    )
    tile = lambda wdt: pl.BlockSpec((1, tm, wdt), lambda b, j: (b, j, 0))
    const2 = lambda s: pl.BlockSpec(s, lambda b, j: (0, 0))
    in_specs = [
        tile(D), const2((1, D)), const2((D, IN_WIDTH_PAD)),
        pl.BlockSpec((3, tm, LANES), lambda b, j: (0, j, 0)),
        pl.BlockSpec((3, tm, LANES), lambda b, j: (0, j, 0)),
        pl.BlockSpec((1, HIST_ROWS, POOL_WIDTH), lambda b, j: (b, 0, 0)),
        pl.BlockSpec((len(POOL_WINDOWS), POOL_GROUP_DIM, POOL_GROUP_DIM), lambda b, j: (0, 0, 0)),
        const2((1, POOL_WIDTH)),
    ]
    out_specs = (
        tile(ATT_WIDTH), tile(ATT_WIDTH), tile(KV_WIDTH), tile(KV_WIDTH), tile(LANES),
        tile(KV_WIDTH), tile(KV_WIDTH), tile(IDX_DIM), tile(POOL_WIDTH),
        pl.BlockSpec((1, HIST_ROWS, POOL_WIDTH), lambda b, j: (b, 0, 0)),
    )
    return pl.pallas_call(
        functools.partial(_inproj_kernel, tm=tm, pos0=pos0),
        out_shape=out_shape, grid=(B, nt), in_specs=in_specs, out_specs=out_specs,
        scratch_shapes=[pltpu.VMEM((HIST_ROWS + tm, POOL_WIDTH), F32)],
        compiler_params=pltpu.CompilerParams(dimension_semantics=("parallel", "arbitrary"),
                                             vmem_limit_bytes=VMEM_LIMIT),
        name="inproj",
    )(x, g, w_bf, rope, ropeki, hist, pw_bf, ps)


def _dsa_kernel(q_ref, qi_ref, kiwi_ref, k_ref, vt_ref, ki_ref, x_ref, yp_ref, wo_ref,
                o_ref, key_buf, bias_buf, m_scr, l_scr, acc_scr,
                *, tk, pos0, n_keys, n_sel):
    tq = LANES
    slab = 64
    j = pl.program_id(1)
    base = pos0 + j * tq
    pos = base + lax.broadcasted_iota(I32, (1, tq), 1)
    limit = jnp.minimum((pos // CHUNK + 1) * CHUNK, n_keys)
    limit_max = jnp.minimum(((base + tq - 1) // CHUNK + 1) * CHUNK, n_keys)
    nkb = (limit_max + tk - 1) // tk
    nt = (((1,), (1,)), ((), ()))

    wi_t = jnp.transpose(kiwi_ref[0])[IDX_DIM:IDX_DIM + N_IDX_HEADS, :]

    def score_block(kb, carry):
        off = pl.multiple_of(kb * tk, tk)
        kiblk = ki_ref[0, pl.ds(off, tk), :]
        idx = jnp.zeros((tk, tq), F32)
        for h in range(N_IDX_HEADS):
            sc = lax.dot_general(kiblk, qi_ref[0, h], nt, preferred_element_type=F32)
            idx = idx + jnp.maximum(sc, 0.0) * wi_t[h:h + 1, :]
        idx = jnp.where(idx == 0.0, 0.0, idx)
        bits = lax.bitcast_convert_type(idx, I32)
        key = bits ^ ((bits >> 31) & 0x7FFFFFFF)
        kidx = kb * tk + lax.broadcasted_iota(I32, (tk, tq), 0)
        key_buf[kb] = jnp.where(kidx < limit, key, INT_MIN)
        return carry

    lax.fori_loop(0, nkb, score_block, 0)

    def col_sum(a):
        return jnp.sum(a, axis=0, keepdims=True)

    def count_ge(cand):
        def body(kb, acc):
            kblk = key_buf[kb]
            for c in range(tk // slab):
                acc = acc + jnp.where(kblk[c * slab:(c + 1) * slab] >= cand, 1.0, 0.0)
            return acc
        return col_sum(lax.fori_loop(0, nkb, body, jnp.zeros((slab, tq), F32)))

    kf = float(n_sel)
    t0 = jnp.where(count_ge(jnp.zeros((1, tq), I32)) >= kf, 0, INT_MIN).astype(I32)

    def bit_body(i, t):
        cand = t | lax.shift_left(jnp.int32(1), 30 - i)
        return jnp.where(count_ge(cand) >= kf, cand, t)

    t = lax.fori_loop(0, 31, bit_body, t0)

    def count_gt_ge(t):
        def body(kb, accs):
            a_gt, a_ge = accs
            kblk = key_buf[kb]
            for c in range(tk // slab):
                blk = kblk[c * slab:(c + 1) * slab]
                a_gt = a_gt + jnp.where(blk > t, 1.0, 0.0)
                a_ge = a_ge + jnp.where(blk >= t, 1.0, 0.0)
            return a_gt, a_ge
        z = jnp.zeros((slab, tq), F32)
        a_gt, a_ge = lax.fori_loop(0, nkb, body, (z, z))
        return col_sum(a_gt), col_sum(a_ge)

    cnt_gt, cnt_ge = count_gt_ge(t)
    need = kf - cnt_gt
    cnt_eq = cnt_ge - cnt_gt
    overfull = jnp.where(t != INT_MIN, cnt_eq - need, 0.0)
    slow = jnp.max(overfull) > 0.0

    @pl.when(jnp.logical_not(slow))
    def _():
        t_adm = jnp.maximum(t, INT_MIN + 1)

        def body(kb, carry):
            bias_buf[kb] = jnp.where(key_buf[kb] >= t_adm, 0.0, NEG_BIG)
            return carry
        lax.fori_loop(0, nkb, body, 0)

    @pl.when(slow)
    def _():
        tri = jnp.where(lax.broadcasted_iota(I32, (tk, tk), 1) <= lax.broadcasted_iota(I32, (tk, tk), 0),
                        1.0, 0.0).astype(BF16)

        def body(kb, seen):
            kblk = key_buf[kb]
            eq = jnp.where((kblk == t) & (kblk != INT_MIN), 1.0, 0.0)
            prefix = jnp.dot(tri, eq.astype(BF16), preferred_element_type=F32) + seen
            keep_tie = jnp.where(prefix <= need, eq, 0.0)
            sel = jnp.where(kblk > t, 1.0, keep_tie)
            bias_buf[kb] = jnp.where(sel > 0.0, 0.0, NEG_BIG)
            return seen + col_sum(eq)
        lax.fori_loop(0, nkb, body, jnp.zeros((1, tq), F32))

    m_scr[...] = jnp.full(m_scr.shape, NEG_BIG, F32)
    l_scr[...] = jnp.zeros(l_scr.shape, F32)
    acc_scr[...] = jnp.zeros(acc_scr.shape, F32)
    sub = LANES
    nsb = (limit_max + sub - 1) // sub

    def scores(c):
        off = pl.multiple_of(c * sub, sub)
        return tuple(
            lax.dot_general(k_ref[0, g, pl.ds(off, sub), :],
                            q_ref[0, Q_PER_KV * g:Q_PER_KV * (g + 1)].reshape(Q_PER_KV * tq, HEAD_DIM),
                            nt, preferred_element_type=F32)
            for g in range(N_KV_HEADS))

    def attn_step(c, s_cur):
        s_next = scores(jnp.minimum(c + 1, nsb - 1))
        row0 = pl.multiple_of((c % (tk // sub)) * sub, sub)
        bias = bias_buf[c // (tk // sub), pl.ds(row0, sub), :]
        bias2 = jnp.concatenate([bias] * Q_PER_KV, axis=1)
        for g in range(N_KV_HEADS):
            s = s_cur[g] + bias2
            m_prev = m_scr[g]
            m_new = jnp.maximum(m_prev, jnp.max(s, axis=0, keepdims=True))
            alpha = jnp.exp(m_prev - m_new)
            p = jnp.exp(s - m_new)
            l_scr[g] = alpha * l_scr[g] + col_sum(p)
            pv = jnp.dot(vt_ref[0, c, g * HEAD_DIM:(g + 1) * HEAD_DIM, :], p.astype(BF16),
                         preferred_element_type=F32)
            acc_scr[g] = alpha * acc_scr[g] + pv
            m_scr[g] = m_new
        return s_next

    lax.fori_loop(0, nsb, attn_step, scores(0))

    o_t = []
    for g in range(N_KV_HEADS):
        og = acc_scr[g] / l_scr[g]
        o_t.extend(og[:, hh * tq:(hh + 1) * tq] for hh in range(Q_PER_KV))
    y_att = jnp.transpose(jnp.concatenate(o_t, axis=0)).astype(BF16)
    y = jnp.dot(yp_ref[0], wo_ref[0:POOL_WIDTH, :], preferred_element_type=F32)
    y = y + jnp.dot(y_att, wo_ref[POOL_WIDTH:POOL_WIDTH + ATT_WIDTH, :], preferred_element_type=F32)
    o_ref[0] = x_ref[0] + y


def _dsa(q, qi, kiwi, k_all, v_all, ki_all, x, yp, wo_bf, *, tk, pos0, n_keys, n_sel):
    B, T, D = x.shape
    L = k_all.shape[1]
    tq = LANES
    assert L % tk == 0 and T % tq == 0
    nkb_max = L // tk
    heads = lambda a, n: jnp.transpose(a.reshape(a.shape[0], a.shape[1], n, a.shape[2] // n), (0, 2, 1, 3))
    q_hm, qi_hm, k_hm = heads(q, N_HEADS), heads(qi, N_IDX_HEADS), heads(k_all, N_KV_HEADS)
    v_t = jnp.transpose(v_all.reshape(B, L // LANES, LANES, KV_WIDTH), (0, 1, 3, 2))
    tile = lambda wdt: pl.BlockSpec((1, tq, wdt), lambda b, j: (b, j, 0))
    in_specs = [pl.BlockSpec((1, N_HEADS, tq, HEAD_DIM), lambda b, j: (b, 0, j, 0)),
                pl.BlockSpec((1, N_IDX_HEADS, tq, IDX_DIM), lambda b, j: (b, 0, j, 0)),
                tile(LANES),
                pl.BlockSpec((1, N_KV_HEADS, L, HEAD_DIM), lambda b, j: (b, 0, 0, 0)),
                pl.BlockSpec((1, L // LANES, KV_WIDTH, LANES), lambda b, j: (b, 0, 0, 0)),
                pl.BlockSpec((1, L, IDX_DIM), lambda b, j: (b, 0, 0)),
                tile(D), tile(POOL_WIDTH), pl.BlockSpec((POOL_WIDTH + ATT_WIDTH, D), lambda b, j: (0, 0))]
    return pl.pallas_call(
        functools.partial(_dsa_kernel, tk=tk, pos0=pos0, n_keys=n_keys, n_sel=n_sel),
        out_shape=jax.ShapeDtypeStruct((B, T, D), F32), grid=(B, T // tq),
        in_specs=in_specs, out_specs=tile(D),
        scratch_shapes=[pltpu.VMEM((nkb_max, tk, tq), I32), pltpu.VMEM((nkb_max, tk, tq), F32),
                        pltpu.VMEM((N_KV_HEADS, 1, Q_PER_KV * tq), F32),
                        pltpu.VMEM((N_KV_HEADS, 1, Q_PER_KV * tq), F32),
                        pltpu.VMEM((N_KV_HEADS, HEAD_DIM, Q_PER_KV * tq), F32)],
        compiler_params=pltpu.CompilerParams(dimension_semantics=("parallel", "arbitrary"),
                                             vmem_limit_bytes=VMEM_LIMIT),
        name="dsa",
    )(q_hm, qi_hm, kiwi, k_hm, v_t, ki_all, x, yp, wo_bf)


def _gmlp_kernel(x_ref, g_ref, win_ref, lng_ref, lnb_ref, ws_ref, bias_ref, wout_ref, o_ref, v_ref, *, tm):
    x = x_ref[...]
    h = _rms(x, g_ref[...])
    z = jax.nn.gelu(jnp.dot(h.astype(BF16), win_ref[...], preferred_element_type=F32))
    half = z.shape[1] // 2
    u, v = z[:, :half], z[:, half:]
    mu = jnp.mean(v, axis=-1, keepdims=True)
    var = jnp.mean(jnp.square(v - mu), axis=-1, keepdims=True)
    vn = (v - mu) * lax.rsqrt(var + LN_EPS) * lng_ref[...] + lnb_ref[...]
    v_ref[...] = vn
    gd = half // GM_GROUPS
    gated = []
    for c in range(tm // GM_CHUNK):
        rows = slice(c * GM_CHUNK, (c + 1) * GM_CHUNK)
        vc = vn[rows].astype(BF16)
        mixed = jnp.concatenate(
            [jnp.dot(ws_ref[0, g], vc[:, g * gd:(g + 1) * gd], preferred_element_type=F32)
             for g in range(GM_GROUPS)], axis=1) + bias_ref[0]
        gated.append((u[rows] * mixed).astype(BF16))
    gated = jnp.concatenate(gated, axis=0)
    o_ref[...] = x + jnp.dot(gated, wout_ref[...], preferred_element_type=F32)


def _gmlp(x, g, win_bf, lng, lnb, ws2, bias2, wout_bf, *, tm, n_first, n_v_rows):
    N, D = x.shape
    half = win_bf.shape[1] // 2
    nt = N // tm
    t_first = n_first // tm
    variant = lambda i: jnp.where(i >= t_first, 1, 0)
    row = pl.BlockSpec((tm, D), lambda i: (i, 0))
    const = lambda s: pl.BlockSpec(s, lambda i: (0, 0))
    in_specs = [row, const((1, D)), const((D, 2 * half)), const((1, half)), const((1, half)),
                pl.BlockSpec((1, GM_GROUPS, GM_CHUNK, GM_CHUNK), lambda i: (variant(i), 0, 0, 0)),
                pl.BlockSpec((1, GM_CHUNK, half), lambda i: (variant(i), 0, 0)),
                const((half, D))]
    out_specs = (row, pl.BlockSpec((tm, half), lambda i: (jnp.maximum(i - t_first, 0), 0)))
    return pl.pallas_call(
        functools.partial(_gmlp_kernel, tm=tm),
        out_shape=(jax.ShapeDtypeStruct((N, D), F32), jax.ShapeDtypeStruct((n_v_rows, half), F32)),
        grid=(nt,), in_specs=in_specs, out_specs=out_specs,
        compiler_params=pltpu.CompilerParams(dimension_semantics=("arbitrary",), vmem_limit_bytes=VMEM_LIMIT),
        name="gmlp",
    )(x, g, win_bf, lng, lnb, ws2, bias2, wout_bf)


ROUTE_E0, ROUTE_E1, ROUTE_R0, ROUTE_R1, ROUTE_G0, ROUTE_G1 = range(6)


def _router_kernel(x_ref, g_ref, wr_ref, h_ref, route_ref, cnt_ref, carry_ref, *, tm):
    i = pl.program_id(0)

    @pl.when(i == 0)
    def _():
        carry_ref[...] = jnp.zeros(carry_ref.shape, F32)

    h = _rms(x_ref[...], g_ref[...])
    h_ref[...] = h.astype(BF16)
    logits = jnp.dot(h, wr_ref[...], preferred_element_type=F32, precision=lax.Precision.HIGHEST)
    lane = lax.broadcasted_iota(I32, (tm, LANES), 1).astype(F32)
    ninf = -jnp.inf
    big = 1e9
    rmax = lambda a: jnp.max(a, axis=1, keepdims=True)
    rmin = lambda a: jnp.min(a, axis=1, keepdims=True)
    rsum = lambda a: jnp.sum(a, axis=1, keepdims=True)

    is_grp = lane < N_EXPERT_GROUPS
    lg = jnp.where(is_grp, logits, ninf)
    mg = rmax(lg)
    g_sel = rmin(jnp.where(lg == mg, lane, big))
    p_grp = 1.0 / rsum(jnp.where(is_grp, jnp.exp(lg - mg), 0.0))
    lo = N_EXPERT_GROUPS + g_sel * EXPERTS_PER_GROUP
    le = jnp.where((lane >= lo) & (lane < lo + EXPERTS_PER_GROUP), logits, ninf)
    v1 = rmax(le)
    j1 = rmin(jnp.where(le == v1, lane, big))
    le2 = jnp.where(lane == j1, ninf, le)
    v2 = rmax(le2)
    j2 = rmin(jnp.where(le2 == v2, lane, big))
    e0 = j1 - N_EXPERT_GROUPS
    e1 = j2 - N_EXPERT_GROUPS
    r = jnp.exp(v2 - v1)
    g0 = p_grp / (1.0 + r)
    g1 = p_grp * r / (1.0 + r)

    oh0 = jnp.where(lane == e0, 1.0, 0.0)
    oh1 = jnp.where(lane == e1, 1.0, 0.0)
    oh = oh0 + oh1
    lower = jnp.where(lax.broadcasted_iota(I32, (tm, tm), 1) < lax.broadcasted_iota(I32, (tm, tm), 0),
                      1.0, 0.0).astype(BF16)
    before = jnp.dot(lower, oh.astype(BF16), preferred_element_type=F32) + carry_ref[...]
    r0 = rsum(oh0 * before)
    r1 = rsum(oh1 * before)
    carry_ref[...] = carry_ref[...] + jnp.sum(oh, axis=0, keepdims=True)
    cnt_ref[...] = carry_ref[...]

    route = jnp.zeros((tm, LANES), F32)
    for col, val in ((ROUTE_E0, e0), (ROUTE_E1, e1), (ROUTE_R0, r0), (ROUTE_R1, r1), (ROUTE_G0, g0), (ROUTE_G1, g1)):
        route = jnp.where(lane == col, val, route)
    route_ref[...] = route


def _router(x, g, wr, *, tm):
    N, D = x.shape
    row = lambda wdt: pl.BlockSpec((tm, wdt), lambda i: (i, 0))
    const = lambda s: pl.BlockSpec(s, lambda i: (0, 0))
    return pl.pallas_call(
        functools.partial(_router_kernel, tm=tm),
        out_shape=(jax.ShapeDtypeStruct((N, D), BF16), jax.ShapeDtypeStruct((N, LANES), F32),
                   jax.ShapeDtypeStruct((1, LANES), F32)),
        grid=(N // tm,), in_specs=[row(D), const((1, D)), const((D, LANES))],
        out_specs=(row(D), row(LANES), const((1, LANES))),
        scratch_shapes=[pltpu.VMEM((1, LANES), F32)],
        compiler_params=pltpu.CompilerParams(dimension_semantics=("arbitrary",), vmem_limit_bytes=VMEM_LIMIT),
        name="router",
    )(x, g, wr)


def _expert_kernel(be_ref, nu_ref, x_ref, wg_ref, wu_ref, wd_ref, y_ref):
    i = pl.program_id(0)

    @pl.when(i < nu_ref[0])
    def _():
        x = x_ref[...]
        a = jnp.dot(x, wg_ref[0].astype(BF16), preferred_element_type=F32)
        u = jnp.dot(x, wu_ref[0].astype(BF16), preferred_element_type=F32)
        act = (a * jax.nn.sigmoid(a)) * u
        y_ref[...] = jnp.dot(act.astype(BF16), wd_ref[0].astype(BF16), preferred_element_type=F32)

    @pl.when(i >= nu_ref[0])
    def _():
        y_ref[...] = jnp.zeros(y_ref.shape, F32)


def _experts(block_e, n_used, x_rows, w_gate, w_up, w_down):
    n_rows, D = x_rows.shape
    de = w_gate.shape[2]
    n_blocks = n_rows // MOE_BLOCK
    grid_spec = pltpu.PrefetchScalarGridSpec(
        num_scalar_prefetch=2, grid=(n_blocks,),
        in_specs=[pl.BlockSpec((MOE_BLOCK, D), lambda i, be, nu: (i, 0)),
                  pl.BlockSpec((1, D, de), lambda i, be, nu: (be[i], 0, 0)),
                  pl.BlockSpec((1, D, de), lambda i, be, nu: (be[i], 0, 0)),
                  pl.BlockSpec((1, de, D), lambda i, be, nu: (be[i], 0, 0))],
        out_specs=pl.BlockSpec((MOE_BLOCK, D), lambda i, be, nu: (i, 0)))
    return pl.pallas_call(
        _expert_kernel, out_shape=jax.ShapeDtypeStruct((n_rows, D), F32), grid_spec=grid_spec,
        compiler_params=pltpu.CompilerParams(dimension_semantics=("arbitrary",), vmem_limit_bytes=VMEM_LIMIT),
        name="experts",
    )(block_e, n_used, x_rows, w_gate, w_up, w_down)


def _combine_kernel(x_ref, y0_ref, y1_ref, route_ref, g_ref, o_ref, *, final_norm):
    route = route_ref[...]
    g0 = route[:, ROUTE_G0:ROUTE_G0 + 1]
    g1 = route[:, ROUTE_G1:ROUTE_G1 + 1]
    out = x_ref[...] + (y0_ref[...] * g0 + y1_ref[...] * g1)
    if final_norm:
        out = _rms(out, g_ref[...])
    o_ref[...] = out


def _combine(x, y0, y1, route, g, *, tm, final_norm):
    N, D = x.shape
    row = lambda wdt: pl.BlockSpec((tm, wdt), lambda i: (i, 0))
    return pl.pallas_call(
        functools.partial(_combine_kernel, final_norm=final_norm),
        out_shape=jax.ShapeDtypeStruct((N, D), F32), grid=(N // tm,),
        in_specs=[row(D), row(D), row(D), row(LANES), pl.BlockSpec((1, D), lambda i: (0, 0))],
        out_specs=row(D),
        compiler_params=pltpu.CompilerParams(dimension_semantics=("parallel",), vmem_limit_bytes=VMEM_LIMIT),
        name="combine",
    )(x, y0, y1, route, g)


def _hier_moe(x, g_ffn, w_rg, w_re, w_gate, w_up, w_down, g_final, *, final_norm):
    N, D = x.shape
    wr = jnp.concatenate([w_rg, w_re, jnp.zeros((D, LANES - N_EXPERT_GROUPS - N_EXPERTS), F32)], axis=1)
    h_bf, route, counts = _router(x, g_ffn.reshape(1, D), wr, tm=256)
    counts = counts[0, :N_EXPERTS].astype(I32)
    eid = route[:, ROUTE_E0:ROUTE_E1 + 1].astype(I32)
    rank = route[:, ROUTE_R0:ROUTE_R1 + 1].astype(I32)
    padded = (counts + MOE_BLOCK - 1) // MOE_BLOCK * MOE_BLOCK
    pad_end = jnp.cumsum(padded)
    pad_start = pad_end - padded
    dest = pad_start[eid] + rank
    n_blocks = -(-(2 * N) // MOE_BLOCK) + N_EXPERTS
    n_rows = n_blocks * MOE_BLOCK
    n_used = (pad_end[-1] // MOE_BLOCK).astype(I32)
    block_start = jnp.arange(n_blocks, dtype=I32) * MOE_BLOCK
    block_e = jnp.minimum(jnp.searchsorted(pad_end, block_start, side='right'), N_EXPERTS - 1).astype(I32)
    last_e = block_e[jnp.maximum(n_used - 1, 0)]
    block_e = jnp.where(jnp.arange(n_blocks) < n_used, block_e, last_e)
    tok = jnp.broadcast_to(jnp.arange(N, dtype=I32)[:, None], (N, 2))
    row_tok = jnp.zeros((n_rows,), I32).at[dest.reshape(-1)].set(tok.reshape(-1))
    x_rows = h_bf[row_tok]
    y_rows = _experts(block_e, n_used.reshape(1), x_rows, w_gate, w_up, w_down)
    y0 = y_rows[dest[:, 0]]
    y1 = y_rows[dest[:, 1]]
    return _combine(x, y0, y1, route, g_final.reshape(1, D), tm=256, final_norm=final_norm)


def _rope_tables(pos, rot_lanes):
    inv = 1.0 / (ROPE_THETA ** (jnp.arange(ROT_HALF, dtype=F32) / ROT_HALF))
    ang = pos.astype(F32)[:, None] * inv[None, :]
    cos, sin = jnp.cos(ang), jnp.sin(ang)
    lane = jnp.arange(LANES)
    r = lane % HEAD_DIM
    active = (lane < rot_lanes)
    first = active & (r < ROT_HALF)
    second = active & (r >= ROT_HALF) & (r < 2 * ROT_HALF)
    cos_l = cos[:, r % ROT_HALF]
    sin_l = sin[:, r % ROT_HALF]
    c = jnp.where((first | second)[None, :], cos_l, 1.0)
    sa = jnp.where(first[None, :], -sin_l, 0.0)
    sb = jnp.where(second[None, :], sin_l, 0.0)
    return jnp.stack([c, sa, sb]).astype(F32)


def _prep_w_in(w_in):
    D = w_in.shape[0]
    xp, q, k, v, qi, ki, wi = jnp.split(w_in, [512, 1024, 1280, 1536, 2048, 2112], axis=1)
    pad = jnp.zeros((D, IN_WIDTH_PAD - w_in.shape[1]), w_in.dtype)
    return jnp.concatenate([xp, q, k, v, qi, ki, wi, pad], axis=1).astype(BF16)


def _pad_keys(a, l_pad):
    return jnp.pad(a, ((0, 0), (0, l_pad - a.shape[1]), (0, 0)))


def kernel(x_prompt, x_sample, cache_k, cache_v, cache_idx_k, state_pool, norm_mix, norm_ffn, norm_final,
           par_w_in, par_pool_w, par_pool_scale, par_w_out, gm_w_in, gm_ln_g, gm_ln_b, gm_ws, gm_bs, gm_w_out,
           moe_router_group, moe_router_expert, moe_w_gate, moe_w_up, moe_w_down):
    Bp, Tp, D = x_prompt.shape
    Bs, Ts, _ = x_sample.shape
    past = cache_k.shape[2]
    Np, Ns = Bp * Tp, Bs * Ts
    depth = norm_mix.shape[0]
    assert depth == 2 and Ts == CHUNK and Tp % 256 == 0 and Ns % 256 == 0

    w_in_bf = _prep_w_in(par_w_in[0])
    pw_bf = par_pool_w[0].astype(BF16)
    ps = par_pool_scale[0].reshape(1, POOL_WIDTH)
    wo_bf = par_w_out[0].astype(BF16)
    g_mix0 = norm_mix[0].reshape(1, D)
    pos_p = jnp.arange(Tp, dtype=I32)
    pos_s = past + jnp.arange(Ts, dtype=I32)

    hist_p = jnp.zeros((Bp, HIST_ROWS, POOL_WIDTH), F32)
    hist_s = jnp.pad(state_pool[0], ((0, 0), (1, 0), (0, 0)))
    tk = 512
    (q_p, qi_p, k_p, v_p, kiwi_p, kbf_p, vbf_p, kibf_p, yp_p, st_p) = _inproj(
        x_prompt, g_mix0, w_in_bf, _rope_tables(pos_p, LANES), _rope_tables(pos_p, IDX_DIM), hist_p, pw_bf, ps,
        tm=256, pos0=0)
    (q_s, qi_s, k_s, v_s, kiwi_s, kbf_s, vbf_s, kibf_s, yp_s, st_s) = _inproj(
        x_sample, g_mix0, w_in_bf, _rope_tables(pos_s, LANES), _rope_tables(pos_s, IDX_DIM), hist_s, pw_bf, ps,
        tm=Ts, pos0=past)

    lp = -(-Tp // tk) * tk
    x1_p = _dsa(q_p, qi_p, kiwi_p, _pad_keys(kbf_p, lp), _pad_keys(vbf_p, lp), _pad_keys(kibf_p, lp),
                x_prompt, yp_p, wo_bf, tk=tk, pos0=0, n_keys=Tp, n_sel=min(TOPK_MAX, Tp // 4))
    ls = past + Ts
    lsp = -(-ls // tk) * tk
    kall = _pad_keys(jnp.concatenate([cache_k[0].reshape(Bs, past, KV_WIDTH).astype(BF16), kbf_s], axis=1), lsp)
    vall = _pad_keys(jnp.concatenate([cache_v[0].reshape(Bs, past, KV_WIDTH).astype(BF16), vbf_s], axis=1), lsp)
    kiall = _pad_keys(jnp.concatenate([cache_idx_k[0].astype(BF16), kibf_s], axis=1), lsp)
    qpad = lambda a: jnp.pad(a, ((0, 0), (0, LANES - Ts), (0, 0)))
    x1_s = _dsa(qpad(q_s), qpad(qi_s), qpad(kiwi_s), kall, vall, kiall, qpad(x_sample), qpad(yp_s), wo_bf,
                tk=tk, pos0=past, n_keys=ls, n_sel=min(TOPK_MAX, ls // 4))[:, :Ts]

    x = jnp.concatenate([x1_p.reshape(Np, D), x1_s.reshape(Ns, D)], axis=0)
    x = _hier_moe(x, norm_ffn[0], moe_router_group[0], moe_router_expert[0],
                  moe_w_gate[0], moe_w_up[0], moe_w_down[0], norm_final, final_norm=False)

    cs = Ts
    tril = lambda n: jnp.tril(jnp.ones((n, n), bool))
    ws_p = jnp.where(tril(GM_CHUNK)[None], gm_ws[0], 0.0)
    ws_small = jnp.where(tril(cs)[None], gm_ws[0][:, :cs, :cs], 0.0)
    rep = GM_CHUNK // cs
    ws_s = jnp.einsum('ab,gts->gatbs', jnp.eye(rep, dtype=F32), ws_small).reshape(GM_GROUPS, GM_CHUNK, GM_CHUNK)
    ws2 = jnp.stack([ws_p, ws_s]).astype(BF16)
    gd = D // GM_GROUPS
    bias_p = jnp.repeat(jnp.transpose(gm_bs[0]), gd, axis=1)
    bias_s = jnp.tile(jnp.repeat(jnp.transpose(gm_bs[0][:, :cs]), gd, axis=1), (rep, 1))
    bias2 = jnp.stack([bias_p, bias_s])
    x, gm_v = _gmlp(x, norm_mix[1].reshape(1, D), gm_w_in[0].astype(BF16), gm_ln_g[0].reshape(1, D),
                    gm_ln_b[0].reshape(1, D), ws2, bias2, gm_w_out[0].astype(BF16),
                    tm=256, n_first=Np, n_v_rows=Ns)
    x = _hier_moe(x, norm_ffn[1], moe_router_group[1], moe_router_expert[1],
                  moe_w_gate[1], moe_w_up[1], moe_w_down[1], norm_final, final_norm=True)

    y_p = x[:Np].reshape(Bp, Tp, D)
    y_s = x[Np:].reshape(Bs, Ts, D)
    r4 = lambda a, b, t: a.reshape(1, b, t, N_KV_HEADS, HEAD_DIM)
    return (y_p, y_s,
            r4(k_p, Bp, Tp), r4(v_p, Bp, Tp), kiwi_p[:, :, :IDX_DIM][None], st_p[:, 1:][None],
            r4(k_s, Bs, Ts), r4(v_s, Bs, Ts), kiwi_s[:, :, :IDX_DIM][None], st_s[:, 1:][None],
            gm_v.reshape(1, Bs, Ts, D))
```

```python
import functools

import jax
import jax.numpy as jnp
from jax import lax
from jax.experimental import pallas as pl
from jax.experimental.pallas import tpu as pltpu

F32 = jnp.float32
BF16 = jnp.bfloat16
I32 = jnp.int32

LANES = 128
CHUNK = 64
POOL_WINDOWS = (2, 4, 8, 16)
POOL_GROUP_DIM = 128
POOL_WIDTH = 512
HIST_ROWS = 16
N_HEADS = 8
HEAD_DIM = 64
N_KV_HEADS = 4
Q_PER_KV = N_HEADS // N_KV_HEADS
ATT_WIDTH = N_HEADS * HEAD_DIM
KV_WIDTH = N_KV_HEADS * HEAD_DIM
N_IDX_HEADS = 8
IDX_DIM = 64
TOPK_MAX = 256
ROPE_THETA = 500000.0
ROT_HALF = HEAD_DIM // 8
GM_CHUNK = 128
GM_GROUPS = 8
N_EXPERT_GROUPS = 4
EXPERTS_PER_GROUP = 8
N_EXPERTS = 32
MOE_BLOCK = 128
RMS_EPS = 1e-6
LN_EPS = 1e-5

INT_MIN = -2147483648
NEG_BIG = -1e30
VMEM_LIMIT = 48 * 1024 * 1024

COL_XP, COL_Q, COL_K, COL_V, COL_QI, COL_KIWI = 0, 512, 1024, 1280, 1536, 2048
IN_WIDTH_PAD = 2176


def _rms(x, g):
    return x * lax.rsqrt(jnp.mean(x * x, axis=-1, keepdims=True) + RMS_EPS) * g


def _rope128(x, c, sa, sb):
    return x * c + pltpu.roll(x, LANES - ROT_HALF, 1) * sa + pltpu.roll(x, ROT_HALF, 1) * sb


def _inproj_kernel(x_ref, g_ref, w_ref, rope_ref, ropeki_ref, hist_ref, pw_ref, ps_ref,
                   q_ref, qi_ref, k_ref, v_ref, kiwi_ref, kbf_ref, vbf_ref, kibf_ref, yp_ref, state_ref,
                   buf_ref, *, tm, pos0):
    j = pl.program_id(1)
    h = _rms(x_ref[0], g_ref[...])
    proj = jnp.dot(h.astype(BF16), w_ref[...], preferred_element_type=F32)
    c, sa, sb = rope_ref[0], rope_ref[1], rope_ref[2]

    for i in range(ATT_WIDTH // LANES):
        sl = slice(i * LANES, (i + 1) * LANES)
        q_ref[0, :, sl] = (_rope128(proj[:, COL_Q + i * LANES:COL_Q + (i + 1) * LANES], c, sa, sb)
                           * (HEAD_DIM ** -0.5)).astype(BF16)
        qi_ref[0, :, sl] = _rope128(proj[:, COL_QI + i * LANES:COL_QI + (i + 1) * LANES], c, sa, sb).astype(BF16)
    for i in range(KV_WIDTH // LANES):
        sl = slice(i * LANES, (i + 1) * LANES)
        kr = _rope128(proj[:, COL_K + i * LANES:COL_K + (i + 1) * LANES], c, sa, sb)
        k_ref[0, :, sl] = kr
        kbf_ref[0, :, sl] = kr.astype(BF16)
    vv = proj[:, COL_V:COL_V + KV_WIDTH]
    v_ref[0] = vv
    vbf_ref[0] = vv.astype(BF16)
    kiwi = _rope128(proj[:, COL_KIWI:COL_KIWI + LANES], ropeki_ref[0], ropeki_ref[1], ropeki_ref[2])
    kiwi_ref[0] = kiwi
    kibf_ref[0] = kiwi[:, :IDX_DIM].astype(BF16)

    @pl.when(j == 0)
    def _():
        buf_ref[0:HIST_ROWS, :] = hist_ref[0]

    xp = proj[:, COL_XP:COL_XP + POOL_WIDTH]
    buf_ref[HIST_ROWS:HIST_ROWS + tm, :] = xp
    pos = pos0 + j * tm + lax.broadcasted_iota(I32, (tm, 1), 0)
    for gi, w in enumerate(POOL_WINDOWS):
        c0 = gi * POOL_GROUP_DIM
        s = xp[:, c0:c0 + POOL_GROUP_DIM]
        for i in range(1, w):
            s = s + buf_ref[HIST_ROWS - i:HIST_ROWS - i + tm, c0:c0 + POOL_GROUP_DIM]
        cnt = jnp.minimum(pos + 1, w).astype(F32)
        d = s / cnt - xp[:, c0:c0 + POOL_GROUP_DIM]
        y = jnp.dot(d.astype(BF16), pw_ref[gi], preferred_element_type=F32)
        yp_ref[0, :, c0:c0 + POOL_GROUP_DIM] = (y * ps_ref[:, c0:c0 + POOL_GROUP_DIM]).astype(BF16)
    tail = buf_ref[tm:tm + HIST_ROWS, :]
    state_ref[0] = tail
    buf_ref[0:HIST_ROWS, :] = tail


def _inproj(x, g, w_bf, rope, ropeki, hist, pw_bf, ps, *, tm, pos0):
    B, T, D = x.shape
    nt = T // tm
    f = lambda shape, dt: jax.ShapeDtypeStruct(shape, dt)
    out_shape = (
        f((B, T, ATT_WIDTH), BF16), f((B, T, ATT_WIDTH), BF16),
        f((B, T, KV_WIDTH), F32), f((B, T, KV_WIDTH), F32), f((B, T, LANES), F32),
        f((B, T, KV_WIDTH), BF16), f((B, T, KV_WIDTH), BF16), f((B, T, IDX_DIM), BF16),
        f((B, T, POOL_WIDTH), BF16), f((B, HIST_ROWS, POOL_WIDTH), F32),
    )
    tile = lambda wdt: pl.BlockSpec((1, tm, wdt), lambda b, j: (b, j, 0))
    const2 = lambda s: pl.BlockSpec(s, lambda b, j: (0, 0))
    in_specs = [
        tile(D), const2((1, D)), const2((D, IN_WIDTH_PAD)),
        pl.BlockSpec((3, tm, LANES), lambda b, j: (0, j, 0)),
        pl.BlockSpec((3, tm, LANES), lambda b, j: (0, j, 0)),
        pl.BlockSpec((1, HIST_ROWS, POOL_WIDTH), lambda b, j: (b, 0, 0)),
        pl.BlockSpec((len(POOL_WINDOWS), POOL_GROUP_DIM, POOL_GROUP_DIM), lambda b, j: (0, 0, 0)),
        const2((1, POOL_WIDTH)),
    ]
    out_specs = (
        tile(ATT_WIDTH), tile(ATT_WIDTH), tile(KV_WIDTH), tile(KV_WIDTH), tile(LANES),
        tile(KV_WIDTH), tile(KV_WIDTH), tile(IDX_DIM), tile(POOL_WIDTH),
        pl.BlockSpec((1, HIST_ROWS, POOL_WIDTH), lambda b, j: (b, 0, 0)),
    )
    return pl.pallas_call(
        functools.partial(_inproj_kernel, tm=tm, pos0=pos0),
        out_shape=out_shape, grid=(B, nt), in_specs=in_specs, out_specs=out_specs,
        scratch_shapes=[pltpu.VMEM((HIST_ROWS + tm, POOL_WIDTH), F32)],
        compiler_params=pltpu.CompilerParams(dimension_semantics=("parallel", "arbitrary"),
                                             vmem_limit_bytes=VMEM_LIMIT),
        name="inproj",
    )(x, g, w_bf, rope, ropeki, hist, pw_bf, ps)


def _dsa_kernel(q_ref, qi_ref, kiwi_ref, k_ref, vt_ref, ki_ref, x_ref, yp_ref, wo_ref,
                o_ref, key_buf, bias_buf, m_scr, l_scr, acc_scr,
                *, tk, pos0, n_keys, n_sel):
    tq = LANES
    slab = 64
    j = pl.program_id(1)
    base = pos0 + j * tq
    pos = base + lax.broadcasted_iota(I32, (1, tq), 1)
    limit = jnp.minimum((pos // CHUNK + 1) * CHUNK, n_keys)
    limit_max = jnp.minimum(((base + tq - 1) // CHUNK + 1) * CHUNK, n_keys)
    nkb = (limit_max + tk - 1) // tk
    nt = (((1,), (1,)), ((), ()))

    wi_t = jnp.transpose(kiwi_ref[0])[IDX_DIM:IDX_DIM + N_IDX_HEADS, :]

    def score_block(kb, carry):
        off = pl.multiple_of(kb * tk, tk)
        kiblk = ki_ref[0, pl.ds(off, tk), :]
        idx = jnp.zeros((tk, tq), F32)
        for h in range(N_IDX_HEADS):
            sc = lax.dot_general(kiblk, qi_ref[0, h], nt, preferred_element_type=F32)
            idx = idx + jnp.maximum(sc, 0.0) * wi_t[h:h + 1, :]
        idx = jnp.where(idx == 0.0, 0.0, idx)
        bits = lax.bitcast_convert_type(idx, I32)
        key = bits ^ ((bits >> 31) & 0x7FFFFFFF)
        kidx = kb * tk + lax.broadcasted_iota(I32, (tk, tq), 0)
        key_buf[kb] = jnp.where(kidx < limit, key, INT_MIN)
        return carry

    lax.fori_loop(0, nkb, score_block, 0)

    def col_sum(a):
        return jnp.sum(a, axis=0, keepdims=True)

    def count_ge(cand):
        def body(kb, acc):
            kblk = key_buf[kb]
            for c in range(tk // slab):
                acc = acc + jnp.where(kblk[c * slab:(c + 1) * slab] >= cand, 1.0, 0.0)
            return acc
        return col_sum(lax.fori_loop(0, nkb, body, jnp.zeros((slab, tq), F32)))

    kf = float(n_sel)
    t0 = jnp.where(count_ge(jnp.zeros((1, tq), I32)) >= kf, 0, INT_MIN).astype(I32)

    def bit_body(i, t):
        cand = t | lax.shift_left(jnp.int32(1), 30 - i)
        return jnp.where(count_ge(cand) >= kf, cand, t)

    t = lax.fori_loop(0, 31, bit_body, t0)

    def count_gt_ge(t):
        def body(kb, accs):
            a_gt, a_ge = accs
            kblk = key_buf[kb]
            for c in range(tk // slab):
                blk = kblk[c * slab:(c + 1) * slab]
                a_gt = a_gt + jnp.where(blk > t, 1.0, 0.0)
                a_ge = a_ge + jnp.where(blk >= t, 1.0, 0.0)
            return a_gt, a_ge
        z = jnp.zeros((slab, tq), F32)
        a_gt, a_ge = lax.fori_loop(0, nkb, body, (z, z))
        return col_sum(a_gt), col_sum(a_ge)

    cnt_gt, cnt_ge = count_gt_ge(t)
    need = kf - cnt_gt
    cnt_eq = cnt_ge - cnt_gt
    overfull = jnp.where(t != INT_MIN, cnt_eq - need, 0.0)
    slow = jnp.max(overfull) > 0.0

    @pl.when(jnp.logical_not(slow))
    def _():
        t_adm = jnp.maximum(t, INT_MIN + 1)

        def body(kb, carry):
            bias_buf[kb] = jnp.where(key_buf[kb] >= t_adm, 0.0, NEG_BIG)
            return carry
        lax.fori_loop(0, nkb, body, 0)

    @pl.when(slow)
    def _():
        tri = jnp.where(lax.broadcasted_iota(I32, (tk, tk), 1) <= lax.broadcasted_iota(I32, (tk, tk), 0),
                        1.0, 0.0).astype(BF16)

        def body(kb, seen):
            kblk = key_buf[kb]
            eq = jnp.where((kblk == t) & (kblk != INT_MIN), 1.0, 0.0)
            prefix = jnp.dot(tri, eq.astype(BF16), preferred_element_type=F32) + seen
            keep_tie = jnp.where(prefix <= need, eq, 0.0)
            sel = jnp.where(kblk > t, 1.0, keep_tie)
            bias_buf[kb] = jnp.where(sel > 0.0, 0.0, NEG_BIG)
            return seen + col_sum(eq)
        lax.fori_loop(0, nkb, body, jnp.zeros((1, tq), F32))

    m_scr[...] = jnp.full(m_scr.shape, NEG_BIG, F32)
    l_scr[...] = jnp.zeros(l_scr.shape, F32)
    acc_scr[...] = jnp.zeros(acc_scr.shape, F32)
    sub = LANES
    nsb = (limit_max + sub - 1) // sub

    def scores(c):
        off = pl.multiple_of(c * sub, sub)
        return tuple(
            lax.dot_general(k_ref[0, g, pl.ds(off, sub), :],
                            q_ref[0, Q_PER_KV * g:Q_PER_KV * (g + 1)].reshape(Q_PER_KV * tq, HEAD_DIM),
                            nt, preferred_element_type=F32)
            for g in range(N_KV_HEADS))

    def attn_step(c, s_cur):
        s_next = scores(jnp.minimum(c + 1, nsb - 1))
        row0 = pl.multiple_of((c % (tk // sub)) * sub, sub)
        bias = bias_buf[c // (tk // sub), pl.ds(row0, sub), :]
        bias2 = jnp.concatenate([bias] * Q_PER_KV, axis=1)
        for g in range(N_KV_HEADS):
            s = s_cur[g] + bias2
            m_prev = m_scr[g]
            m_new = jnp.maximum(m_prev, jnp.max(s, axis=0, keepdims=True))
            alpha = jnp.exp(m_prev - m_new)
            p = jnp.exp(s - m_new)
            l_scr[g] = alpha * l_scr[g] + col_sum(p)
            pv = jnp.dot(vt_ref[0, c, g * HEAD_DIM:(g + 1) * HEAD_DIM, :], p.astype(BF16),
                         preferred_element_type=F32)
            acc_scr[g] = alpha * acc_scr[g] + pv
            m_scr[g] = m_new
        return s_next

    lax.fori_loop(0, nsb, attn_step, scores(0))

    o_t = []
    for g in range(N_KV_HEADS):
        og = acc_scr[g] / l_scr[g]
        o_t.extend(og[:, hh * tq:(hh + 1) * tq] for hh in range(Q_PER_KV))
    y_att = jnp.transpose(jnp.concatenate(o_t, axis=0)).astype(BF16)
    y = jnp.dot(yp_ref[0], wo_ref[0:POOL_WIDTH, :], preferred_element_type=F32)
    y = y + jnp.dot(y_att, wo_ref[POOL_WIDTH:POOL_WIDTH + ATT_WIDTH, :], preferred_element_type=F32)
    o_ref[0] = x_ref[0] + y


def _dsa(q, qi, kiwi, k_all, v_all, ki_all, x, yp, wo_bf, *, tk, pos0, n_keys, n_sel):
    B, T, D = x.shape
    L = k_all.shape[1]
    tq = LANES
    assert L % tk == 0 and T % tq == 0
    nkb_max = L // tk
    heads = lambda a, n: jnp.transpose(a.reshape(a.shape[0], a.shape[1], n, a.shape[2] // n), (0, 2, 1, 3))
    q_hm, qi_hm, k_hm = heads(q, N_HEADS), heads(qi, N_IDX_HEADS), heads(k_all, N_KV_HEADS)
    v_t = jnp.transpose(v_all.reshape(B, L // LANES, LANES, KV_WIDTH), (0, 1, 3, 2))
    tile = lambda wdt: pl.BlockSpec((1, tq, wdt), lambda b, j: (b, j, 0))
    in_specs = [pl.BlockSpec((1, N_HEADS, tq, HEAD_DIM), lambda b, j: (b, 0, j, 0)),
                pl.BlockSpec((1, N_IDX_HEADS, tq, IDX_DIM), lambda b, j: (b, 0, j, 0)),
                tile(LANES),
                pl.BlockSpec((1, N_KV_HEADS, L, HEAD_DIM), lambda b, j: (b, 0, 0, 0)),
                pl.BlockSpec((1, L // LANES, KV_WIDTH, LANES), lambda b, j: (b, 0, 0, 0)),
                pl.BlockSpec((1, L, IDX_DIM), lambda b, j: (b, 0, 0)),
                tile(D), tile(POOL_WIDTH), pl.BlockSpec((POOL_WIDTH + ATT_WIDTH, D), lambda b, j: (0, 0))]
    return pl.pallas_call(
        functools.partial(_dsa_kernel, tk=tk, pos0=pos0, n_keys=n_keys, n_sel=n_sel),
        out_shape=jax.ShapeDtypeStruct((B, T, D), F32), grid=(B, T // tq),
        in_specs=in_specs, out_specs=tile(D),
        scratch_shapes=[pltpu.VMEM((nkb_max, tk, tq), I32), pltpu.VMEM((nkb_max, tk, tq), F32),
                        pltpu.VMEM((N_KV_HEADS, 1, Q_PER_KV * tq), F32),
                        pltpu.VMEM((N_KV_HEADS, 1, Q_PER_KV * tq), F32),
                        pltpu.VMEM((N_KV_HEADS, HEAD_DIM, Q_PER_KV * tq), F32)],
        compiler_params=pltpu.CompilerParams(dimension_semantics=("parallel", "arbitrary"),
                                             vmem_limit_bytes=VMEM_LIMIT),
        name="dsa",
    )(q_hm, qi_hm, kiwi, k_hm, v_t, ki_all, x, yp, wo_bf)


def _gmlp_kernel(x_ref, g_ref, win_ref, lng_ref, lnb_ref, ws_ref, bias_ref, wout_ref, o_ref, v_ref, *, tm):
    x = x_ref[...]
    h = _rms(x, g_ref[...])
    z = jax.nn.gelu(jnp.dot(h.astype(BF16), win_ref[...], preferred_element_type=F32))
    half = z.shape[1] // 2
    u, v = z[:, :half], z[:, half:]
    mu = jnp.mean(v, axis=-1, keepdims=True)
    var = jnp.mean(jnp.square(v - mu), axis=-1, keepdims=True)
    vn = (v - mu) * lax.rsqrt(var + LN_EPS) * lng_ref[...] + lnb_ref[...]
    v_ref[...] = vn
    gd = half // GM_GROUPS
    gated = []
    for c in range(tm // GM_CHUNK):
        rows = slice(c * GM_CHUNK, (c + 1) * GM_CHUNK)
        vc = vn[rows].astype(BF16)
        mixed = jnp.concatenate(
            [jnp.dot(ws_ref[0, g], vc[:, g * gd:(g + 1) * gd], preferred_element_type=F32)
             for g in range(GM_GROUPS)], axis=1) + bias_ref[0]
        gated.append((u[rows] * mixed).astype(BF16))
    gated = jnp.concatenate(gated, axis=0)
    o_ref[...] = x + jnp.dot(gated, wout_ref[...], preferred_element_type=F32)


def _gmlp(x, g, win_bf, lng, lnb, ws2, bias2, wout_bf, *, tm, n_first, n_v_rows):
    N, D = x.shape
    half = win_bf.shape[1] // 2
    nt = N // tm
    t_first = n_first // tm
    variant = lambda i: jnp.where(i >= t_first, 1, 0)
    row = pl.BlockSpec((tm, D), lambda i: (i, 0))
    const = lambda s: pl.BlockSpec(s, lambda i: (0, 0))
    in_specs = [row, const((1, D)), const((D, 2 * half)), const((1, half)), const((1, half)),
                pl.BlockSpec((1, GM_GROUPS, GM_CHUNK, GM_CHUNK), lambda i: (variant(i), 0, 0, 0)),
                pl.BlockSpec((1, GM_CHUNK, half), lambda i: (variant(i), 0, 0)),
                const((half, D))]
    out_specs = (row, pl.BlockSpec((tm, half), lambda i: (jnp.maximum(i - t_first, 0), 0)))
    return pl.pallas_call(
        functools.partial(_gmlp_kernel, tm=tm),
        out_shape=(jax.ShapeDtypeStruct((N, D), F32), jax.ShapeDtypeStruct((n_v_rows, half), F32)),
        grid=(nt,), in_specs=in_specs, out_specs=out_specs,
        compiler_params=pltpu.CompilerParams(dimension_semantics=("arbitrary",), vmem_limit_bytes=VMEM_LIMIT),
        name="gmlp",
    )(x, g, win_bf, lng, lnb, ws2, bias2, wout_bf)


ROUTE_E0, ROUTE_E1, ROUTE_R0, ROUTE_R1, ROUTE_G0, ROUTE_G1 = range(6)


def _router_kernel(x_ref, g_ref, wr_ref, route_ref, cnt_ref, carry_ref, *, tm):
    i = pl.program_id(0)

    @pl.when(i == 0)
    def _():
        carry_ref[...] = jnp.zeros(carry_ref.shape, F32)

    h = _rms(x_ref[...], g_ref[...])
    logits = jnp.dot(h, wr_ref[...], preferred_element_type=F32, precision=lax.Precision.HIGHEST)
    lane = lax.broadcasted_iota(I32, (tm, LANES), 1).astype(F32)
    ninf = -jnp.inf
    big = 1e9
    rmax = lambda a: jnp.max(a, axis=1, keepdims=True)
    rmin = lambda a: jnp.min(a, axis=1, keepdims=True)
    rsum = lambda a: jnp.sum(a, axis=1, keepdims=True)

    is_grp = lane < N_EXPERT_GROUPS
    lg = jnp.where(is_grp, logits, ninf)
    mg = rmax(lg)
    g_sel = rmin(jnp.where(lg == mg, lane, big))
    p_grp = 1.0 / rsum(jnp.where(is_grp, jnp.exp(lg - mg), 0.0))
    lo = N_EXPERT_GROUPS + g_sel * EXPERTS_PER_GROUP
    le = jnp.where((lane >= lo) & (lane < lo + EXPERTS_PER_GROUP), logits, ninf)
    v1 = rmax(le)
    j1 = rmin(jnp.where(le == v1, lane, big))
    le2 = jnp.where(lane == j1, ninf, le)
    v2 = rmax(le2)
    j2 = rmin(jnp.where(le2 == v2, lane, big))
    e0 = j1 - N_EXPERT_GROUPS
    e1 = j2 - N_EXPERT_GROUPS
    r = jnp.exp(v2 - v1)
    g0 = p_grp / (1.0 + r)
    g1 = p_grp * r / (1.0 + r)

    oh0 = jnp.where(lane == e0, 1.0, 0.0)
    oh1 = jnp.where(lane == e1, 1.0, 0.0)
    oh = oh0 + oh1
    lower = jnp.where(lax.broadcasted_iota(I32, (tm, tm), 1) < lax.broadcasted_iota(I32, (tm, tm), 0),
                      1.0, 0.0).astype(BF16)
    before = jnp.dot(lower, oh.astype(BF16), preferred_element_type=F32) + carry_ref[...]
    r0 = rsum(oh0 * before)
    r1 = rsum(oh1 * before)
    carry_ref[...] = carry_ref[...] + jnp.sum(oh, axis=0, keepdims=True)
    cnt_ref[...] = carry_ref[...]

    route = jnp.zeros((tm, LANES), F32)
    for col, val in ((ROUTE_E0, e0), (ROUTE_E1, e1), (ROUTE_R0, r0), (ROUTE_R1, r1), (ROUTE_G0, g0), (ROUTE_G1, g1)):
        route = jnp.where(lane == col, val, route)
    route_ref[...] = route


def _router(x, g, wr, *, tm):
    N, D = x.shape
    row = lambda wdt: pl.BlockSpec((tm, wdt), lambda i: (i, 0))
    const = lambda s: pl.BlockSpec(s, lambda i: (0, 0))
    return pl.pallas_call(
        functools.partial(_router_kernel, tm=tm),
        out_shape=(jax.ShapeDtypeStruct((N, LANES), F32), jax.ShapeDtypeStruct((1, LANES), F32)),
        grid=(N // tm,), in_specs=[row(D), const((1, D)), const((D, LANES))],
        out_specs=(row(LANES), const((1, LANES))),
        scratch_shapes=[pltpu.VMEM((1, LANES), F32)],
        compiler_params=pltpu.CompilerParams(dimension_semantics=("arbitrary",), vmem_limit_bytes=VMEM_LIMIT),
        name="router",
    )(x, g, wr)


def _row_copy(src_ref, src_row, dst_ref, dst_row, sem):
    return pltpu.make_async_copy(src_ref.at[pl.ds(src_row, 1)], dst_ref.at[pl.ds(dst_row, 1)], sem)


def _dispatch_kernel(dest_ref, x_ref, g_ref, zero_ref, rows_ref, h_scr, sems, *, tm):
    del zero_ref
    i = pl.program_id(0)
    n = pl.num_programs(0)
    slot = i % 2

    def wait_slot(s):
        for _ in range(2):
            pltpu.make_async_copy(h_scr.at[s], rows_ref.at[pl.ds(0, tm)], sems.at[s]).wait()

    @pl.when(i >= 2)
    def _():
        wait_slot(slot)

    h_scr[slot] = _rms(x_ref[...], g_ref[...])

    def body(r, carry):
        for a in range(2):
            _row_copy(h_scr.at[slot], r, rows_ref, dest_ref[a, r], sems.at[slot]).start()
        return carry

    lax.fori_loop(0, tm, body, 0, unroll=8)

    @pl.when(i == n - 1)
    def _():
        wait_slot(slot)

    @pl.when(jnp.logical_and(i == n - 1, n >= 2))
    def _():
        wait_slot(1 - slot)


def _dispatch(dest_t, x, g, *, tm, n_rows):
    N, D = x.shape
    zeros = jnp.zeros((n_rows, D), F32)
    return pl.pallas_call(
        functools.partial(_dispatch_kernel, tm=tm),
        out_shape=jax.ShapeDtypeStruct((n_rows, D), F32), grid=(N // tm,),
        in_specs=[pl.BlockSpec((2, tm), lambda i: (0, i), memory_space=pltpu.SMEM),
                  pl.BlockSpec((tm, D), lambda i: (i, 0)),
                  pl.BlockSpec((1, D), lambda i: (0, 0)),
                  pl.BlockSpec(memory_space=pl.ANY)],
        out_specs=pl.BlockSpec(memory_space=pl.ANY),
        scratch_shapes=[pltpu.VMEM((2, tm, D), F32), pltpu.SemaphoreType.DMA((2,))],
        input_output_aliases={3: 0},
        compiler_params=pltpu.CompilerParams(dimension_semantics=("arbitrary",), vmem_limit_bytes=VMEM_LIMIT),
        name="dispatch",
    )(dest_t, x, g, zeros)


def _expert_kernel(be_ref, nu_ref, x_ref, wg_ref, wu_ref, wd_ref, y_ref):
    i = pl.program_id(0)

    @pl.when(i < nu_ref[0])
    def _():
        x = x_ref[...].astype(BF16)
        a = jnp.dot(x, wg_ref[0].astype(BF16), preferred_element_type=F32)
        u = jnp.dot(x, wu_ref[0].astype(BF16), preferred_element_type=F32)
        act = (a * jax.nn.sigmoid(a)) * u
        y_ref[...] = jnp.dot(act.astype(BF16), wd_ref[0].astype(BF16), preferred_element_type=F32)

    @pl.when(i >= nu_ref[0])
    def _():
        y_ref[...] = jnp.zeros(y_ref.shape, F32)


def _experts(block_e, n_used, x_rows, w_gate, w_up, w_down):
    n_rows, D = x_rows.shape
    de = w_gate.shape[2]
    n_blocks = n_rows // MOE_BLOCK
    grid_spec = pltpu.PrefetchScalarGridSpec(
        num_scalar_prefetch=2, grid=(n_blocks,),
        in_specs=[pl.BlockSpec((MOE_BLOCK, D), lambda i, be, nu: (i, 0)),
                  pl.BlockSpec((1, D, de), lambda i, be, nu: (be[i], 0, 0)),
                  pl.BlockSpec((1, D, de), lambda i, be, nu: (be[i], 0, 0)),
                  pl.BlockSpec((1, de, D), lambda i, be, nu: (be[i], 0, 0))],
        out_specs=pl.BlockSpec((MOE_BLOCK, D), lambda i, be, nu: (i, 0)))
    return pl.pallas_call(
        _expert_kernel, out_shape=jax.ShapeDtypeStruct((n_rows, D), F32), grid_spec=grid_spec,
        compiler_params=pltpu.CompilerParams(dimension_semantics=("arbitrary",), vmem_limit_bytes=VMEM_LIMIT),
        name="experts",
    )(block_e, n_used, x_rows, w_gate, w_up, w_down)


def _combine_kernel(dcur_ref, dnxt_ref, x_ref, route_ref, g_ref, y_ref, o_ref, ybuf, sems, *, tm, final_norm):
    i = pl.program_id(0)
    n = pl.num_programs(0)
    slot = i % 2

    def gather(dref, s):
        def body(r, carry):
            for a in range(2):
                _row_copy(y_ref, dref[a, r], ybuf.at[s, a], r, sems.at[s]).start()
            return carry
        lax.fori_loop(0, tm, body, 0, unroll=8)

    @pl.when(i == 0)
    def _():
        gather(dcur_ref, 0)

    @pl.when(i + 1 < n)
    def _():
        gather(dnxt_ref, 1 - slot)

    for a in range(2):
        pltpu.make_async_copy(y_ref.at[pl.ds(0, tm)], ybuf.at[slot, a], sems.at[slot]).wait()
    route = route_ref[...]
    g0 = route[:, ROUTE_G0:ROUTE_G0 + 1]
    g1 = route[:, ROUTE_G1:ROUTE_G1 + 1]
    out = x_ref[...] + (ybuf[slot, 0] * g0 + ybuf[slot, 1] * g1)
    if final_norm:
        out = _rms(out, g_ref[...])
    o_ref[...] = out


def _combine(dest_t, x, route, g, y_rows, *, tm, final_norm):
    N, D = x.shape
    nt = N // tm
    row = lambda wdt: pl.BlockSpec((tm, wdt), lambda i: (i, 0))
    return pl.pallas_call(
        functools.partial(_combine_kernel, tm=tm, final_norm=final_norm),
        out_shape=jax.ShapeDtypeStruct((N, D), F32), grid=(nt,),
        in_specs=[pl.BlockSpec((2, tm), lambda i: (0, i), memory_space=pltpu.SMEM),
                  pl.BlockSpec((2, tm), lambda i: (0, jnp.minimum(i + 1, nt - 1)), memory_space=pltpu.SMEM),
                  row(D), row(LANES), pl.BlockSpec((1, D), lambda i: (0, 0)),
                  pl.BlockSpec(memory_space=pl.ANY)],
        out_specs=row(D),
        scratch_shapes=[pltpu.VMEM((2, 2, tm, D), F32), pltpu.SemaphoreType.DMA((2,))],
        compiler_params=pltpu.CompilerParams(dimension_semantics=("arbitrary",), vmem_limit_bytes=VMEM_LIMIT),
        name="combine",
    )(dest_t, dest_t, x, route, g, y_rows)


def _hier_moe(x, g_ffn, w_rg, w_re, w_gate, w_up, w_down, g_final, *, final_norm):
    N, D = x.shape
    tm = 256
    wr = jnp.concatenate([w_rg, w_re, jnp.zeros((D, LANES - N_EXPERT_GROUPS - N_EXPERTS), F32)], axis=1)
    g_ffn = g_ffn.reshape(1, D)
    route, counts = _router(x, g_ffn, wr, tm=tm)
    counts = counts[0, :N_EXPERTS].astype(I32)
    eid_t = jnp.transpose(route[:, ROUTE_E0:ROUTE_E1 + 1]).astype(I32)
    rank_t = jnp.transpose(route[:, ROUTE_R0:ROUTE_R1 + 1]).astype(I32)
    padded = (counts + MOE_BLOCK - 1) // MOE_BLOCK * MOE_BLOCK
    pad_end = jnp.cumsum(padded)
    pad_start = pad_end - padded
    dest_t = pad_start[eid_t] + rank_t
    n_blocks = -(-(2 * N) // MOE_BLOCK) + N_EXPERTS
    n_rows = n_blocks * MOE_BLOCK
    n_used = (pad_end[-1] // MOE_BLOCK).astype(I32)
    block_start = jnp.minimum(jnp.arange(n_blocks, dtype=I32), n_used - 1) * MOE_BLOCK
    block_e = jnp.minimum(jnp.sum(block_start[:, None] >= pad_end[None, :], axis=1), N_EXPERTS - 1).astype(I32)
    x_rows = _dispatch(dest_t, x, g_ffn, tm=tm, n_rows=n_rows)
    y_rows = _experts(block_e, n_used.reshape(1), x_rows, w_gate, w_up, w_down)
    return _combine(dest_t, x, route, g_final.reshape(1, D), y_rows, tm=tm, final_norm=final_norm)


def _rope_tables(pos, rot_lanes):
    inv = 1.0 / (ROPE_THETA ** (jnp.arange(ROT_HALF, dtype=F32) / ROT_HALF))
    ang = pos.astype(F32)[:, None] * inv[None, :]
    cos, sin = jnp.cos(ang), jnp.sin(ang)
    lane = jnp.arange(LANES)
    r = lane % HEAD_DIM
    active = (lane < rot_lanes)
    first = active & (r < ROT_HALF)
    second = active & (r >= ROT_HALF) & (r < 2 * ROT_HALF)
    cos_l = cos[:, r % ROT_HALF]
    sin_l = sin[:, r % ROT_HALF]
    c = jnp.where((first | second)[None, :], cos_l, 1.0)
    sa = jnp.where(first[None, :], -sin_l, 0.0)
    sb = jnp.where(second[None, :], sin_l, 0.0)
    return jnp.stack([c, sa, sb]).astype(F32)


def _prep_w_in(w_in):
    D = w_in.shape[0]
    xp, q, k, v, qi, ki, wi = jnp.split(w_in, [512, 1024, 1280, 1536, 2048, 2112], axis=1)
    pad = jnp.zeros((D, IN_WIDTH_PAD - w_in.shape[1]), w_in.dtype)
    return jnp.concatenate([xp, q, k, v, qi, ki, wi, pad], axis=1).astype(BF16)


def _pad_keys(a, l_pad):
    return jnp.pad(a, ((0, 0), (0, l_pad - a.shape[1]), (0, 0)))


def kernel(x_prompt, x_sample, cache_k, cache_v, cache_idx_k, state_pool, norm_mix, norm_ffn, norm_final,
           par_w_in, par_pool_w, par_pool_scale, par_w_out, gm_w_in, gm_ln_g, gm_ln_b, gm_ws, gm_bs, gm_w_out,
           moe_router_group, moe_router_expert, moe_w_gate, moe_w_up, moe_w_down):
    Bp, Tp, D = x_prompt.shape
    Bs, Ts, _ = x_sample.shape
    past = cache_k.shape[2]
    Np, Ns = Bp * Tp, Bs * Ts
    depth = norm_mix.shape[0]
    assert depth == 2 and Ts == CHUNK and Tp % 256 == 0 and Ns % 256 == 0

    w_in_bf = _prep_w_in(par_w_in[0])
    pw_bf = par_pool_w[0].astype(BF16)
    ps = par_pool_scale[0].reshape(1, POOL_WIDTH)
    wo_bf = par_w_out[0].astype(BF16)
    g_mix0 = norm_mix[0].reshape(1, D)
    pos_p = jnp.arange(Tp, dtype=I32)
    pos_s = past + jnp.arange(Ts, dtype=I32)

    hist_p = jnp.zeros((Bp, HIST_ROWS, POOL_WIDTH), F32)
    hist_s = jnp.pad(state_pool[0], ((0, 0), (1, 0), (0, 0)))
    tk = 512
    (q_p, qi_p, k_p, v_p, kiwi_p, kbf_p, vbf_p, kibf_p, yp_p, st_p) = _inproj(
        x_prompt, g_mix0, w_in_bf, _rope_tables(pos_p, LANES), _rope_tables(pos_p, IDX_DIM), hist_p, pw_bf, ps,
        tm=256, pos0=0)
    (q_s, qi_s, k_s, v_s, kiwi_s, kbf_s, vbf_s, kibf_s, yp_s, st_s) = _inproj(
        x_sample, g_mix0, w_in_bf, _rope_tables(pos_s, LANES), _rope_tables(pos_s, IDX_DIM), hist_s, pw_bf, ps,
        tm=Ts, pos0=past)

    lp = -(-Tp // tk) * tk
    x1_p = _dsa(q_p, qi_p, kiwi_p, _pad_keys(kbf_p, lp), _pad_keys(vbf_p, lp), _pad_keys(kibf_p, lp),
                x_prompt, yp_p, wo_bf, tk=tk, pos0=0, n_keys=Tp, n_sel=min(TOPK_MAX, Tp // 4))
    ls = past + Ts
    lsp = -(-ls // tk) * tk
    kall = _pad_keys(jnp.concatenate([cache_k[0].reshape(Bs, past, KV_WIDTH).astype(BF16), kbf_s], axis=1), lsp)
    vall = _pad_keys(jnp.concatenate([cache_v[0].reshape(Bs, past, KV_WIDTH).astype(BF16), vbf_s], axis=1), lsp)
    kiall = _pad_keys(jnp.concatenate([cache_idx_k[0].astype(BF16), kibf_s], axis=1), lsp)
    qpad = lambda a: jnp.pad(a, ((0, 0), (0, LANES - Ts), (0, 0)))
    x1_s = _dsa(qpad(q_s), qpad(qi_s), qpad(kiwi_s), kall, vall, kiall, qpad(x_sample), qpad(yp_s), wo_bf,
                tk=tk, pos0=past, n_keys=ls, n_sel=min(TOPK_MAX, ls // 4))[:, :Ts]

    x = jnp.concatenate([x1_p.reshape(Np, D), x1_s.reshape(Ns, D)], axis=0)
    x = _hier_moe(x, norm_ffn[0], moe_router_group[0], moe_router_expert[0],
                  moe_w_gate[0], moe_w_up[0], moe_w_down[0], norm_final, final_norm=False)

    cs = Ts
    tril = lambda n: jnp.tril(jnp.ones((n, n), bool))
    ws_p = jnp.where(tril(GM_CHUNK)[None], gm_ws[0], 0.0)
    ws_small = jnp.where(tril(cs)[None], gm_ws[0][:, :cs, :cs], 0.0)
    rep = GM_CHUNK // cs
    ws_s = jnp.einsum('ab,gts->gatbs', jnp.eye(rep, dtype=F32), ws_small).reshape(GM_GROUPS, GM_CHUNK, GM_CHUNK)
    ws2 = jnp.stack([ws_p, ws_s]).astype(BF16)
    gd = D // GM_GROUPS
    bias_p = jnp.repeat(jnp.transpose(gm_bs[0]), gd, axis=1)
    bias_s = jnp.tile(jnp.repeat(jnp.transpose(gm_bs[0][:, :cs]), gd, axis=1), (rep, 1))
    bias2 = jnp.stack([bias_p, bias_s])
    x, gm_v = _gmlp(x, norm_mix[1].reshape(1, D), gm_w_in[0].astype(BF16), gm_ln_g[0].reshape(1, D),
                    gm_ln_b[0].reshape(1, D), ws2, bias2, gm_w_out[0].astype(BF16),
                    tm=256, n_first=Np, n_v_rows=Ns)
    x = _hier_moe(x, norm_ffn[1], moe_router_group[1], moe_router_expert[1],
                  moe_w_gate[1], moe_w_up[1], moe_w_down[1], norm_final, final_norm=True)

    y_p = x[:Np].reshape(Bp, Tp, D)
    y_s = x[Np:].reshape(Bs, Ts, D)
    r4 = lambda a, b, t: a.reshape(1, b, t, N_KV_HEADS, HEAD_DIM)
    return (y_p, y_s,
            r4(k_p, Bp, Tp), r4(v_p, Bp, Tp), kiwi_p[:, :, :IDX_DIM][None], st_p[:, 1:][None],
            r4(k_s, Bs, Ts), r4(v_s, Bs, Ts), kiwi_s[:, :, :IDX_DIM][None], st_s[:, 1:][None],
            gm_v.reshape(1, Bs, Ts, D))
```

```python
import functools

import jax
import jax.numpy as jnp
from jax import lax
from jax.experimental import pallas as pl
from jax.experimental.pallas import tpu as pltpu

F32 = jnp.float32
BF16 = jnp.bfloat16
I32 = jnp.int32

LANES = 128
CHUNK = 64
POOL_WINDOWS = (2, 4, 8, 16)
POOL_GROUP_DIM = 128
POOL_WIDTH = 512
HIST_ROWS = 16
N_HEADS = 8
HEAD_DIM = 64
N_KV_HEADS = 4
Q_PER_KV = N_HEADS // N_KV_HEADS
ATT_WIDTH = N_HEADS * HEAD_DIM
KV_WIDTH = N_KV_HEADS * HEAD_DIM
N_IDX_HEADS = 8
IDX_DIM = 64
TOPK_MAX = 256
ROPE_THETA = 500000.0
ROT_HALF = HEAD_DIM // 8
GM_CHUNK = 128
GM_GROUPS = 8
N_EXPERT_GROUPS = 4
EXPERTS_PER_GROUP = 8
N_EXPERTS = 32
MOE_BLOCK = 128
RMS_EPS = 1e-6
LN_EPS = 1e-5

INT_MIN = -2147483648
NEG_BIG = -1e30
VMEM_LIMIT = 48 * 1024 * 1024

COL_XP, COL_Q, COL_K, COL_V, COL_QI, COL_KIWI = 0, 512, 1024, 1280, 1536, 2048
IN_WIDTH_PAD = 2176


def _rms(x, g):
    return x * lax.rsqrt(jnp.mean(x * x, axis=-1, keepdims=True) + RMS_EPS) * g


def _rope128(x, c, sa, sb):
    return x * c + pltpu.roll(x, LANES - ROT_HALF, 1) * sa + pltpu.roll(x, ROT_HALF, 1) * sb


def _inproj_kernel(x_ref, g_ref, w_ref, rope_ref, ropeki_ref, hist_ref, pw_ref, ps_ref,
                   q_ref, qi_ref, k_ref, v_ref, kiwi_ref, kbf_ref, vbf_ref, kibf_ref, yp_ref, state_ref,
                   buf_ref, *, tm, pos0):
    j = pl.program_id(1)
    h = _rms(x_ref[0], g_ref[...])
    proj = jnp.dot(h.astype(BF16), w_ref[...], preferred_element_type=F32)
    c, sa, sb = rope_ref[0], rope_ref[1], rope_ref[2]

    for i in range(ATT_WIDTH // LANES):
        sl = slice(i * LANES, (i + 1) * LANES)
        q_ref[0, :, sl] = (_rope128(proj[:, COL_Q + i * LANES:COL_Q + (i + 1) * LANES], c, sa, sb)
                           * (HEAD_DIM ** -0.5)).astype(BF16)
        qi_ref[0, :, sl] = _rope128(proj[:, COL_QI + i * LANES:COL_QI + (i + 1) * LANES], c, sa, sb).astype(BF16)
    for i in range(KV_WIDTH // LANES):
        sl = slice(i * LANES, (i + 1) * LANES)
        kr = _rope128(proj[:, COL_K + i * LANES:COL_K + (i + 1) * LANES], c, sa, sb)
        k_ref[0, :, sl] = kr
        kbf_ref[0, :, sl] = kr.astype(BF16)
    vv = proj[:, COL_V:COL_V + KV_WIDTH]
    v_ref[0] = vv
    vbf_ref[0] = vv.astype(BF16)
    kiwi = _rope128(proj[:, COL_KIWI:COL_KIWI + LANES], ropeki_ref[0], ropeki_ref[1], ropeki_ref[2])
    kiwi_ref[0] = kiwi
    kibf_ref[0] = kiwi[:, :IDX_DIM].astype(BF16)

    @pl.when(j == 0)
    def _():
        buf_ref[0:HIST_ROWS, :] = hist_ref[0]

    xp = proj[:, COL_XP:COL_XP + POOL_WIDTH]
    buf_ref[HIST_ROWS:HIST_ROWS + tm, :] = xp
    pos = pos0 + j * tm + lax.broadcasted_iota(I32, (tm, 1), 0)
    for gi, w in enumerate(POOL_WINDOWS):
        c0 = gi * POOL_GROUP_DIM
        s = xp[:, c0:c0 + POOL_GROUP_DIM]
        for i in range(1, w):
            s = s + buf_ref[HIST_ROWS - i:HIST_ROWS - i + tm, c0:c0 + POOL_GROUP_DIM]
        cnt = jnp.minimum(pos + 1, w).astype(F32)
        d = s / cnt - xp[:, c0:c0 + POOL_GROUP_DIM]
        y = jnp.dot(d.astype(BF16), pw_ref[gi], preferred_element_type=F32)
        yp_ref[0, :, c0:c0 + POOL_GROUP_DIM] = (y * ps_ref[:, c0:c0 + POOL_GROUP_DIM]).astype(BF16)
    tail = buf_ref[tm:tm + HIST_ROWS, :]
    state_ref[0] = tail
    buf_ref[0:HIST_ROWS, :] = tail


def _inproj(x, g, w_bf, rope, ropeki, hist, pw_bf, ps, *, tm, pos0):
    B, T, D = x.shape
    nt = T // tm
    f = lambda shape, dt: jax.ShapeDtypeStruct(shape, dt)
    out_shape = (
        f((B, T, ATT_WIDTH), BF16), f((B, T, ATT_WIDTH), BF16),
        f((B, T, KV_WIDTH), F32), f((B, T, KV_WIDTH), F32), f((B, T, LANES), F32),
        f((B, T, KV_WIDTH), BF16), f((B, T, KV_WIDTH), BF16), f((B, T, IDX_DIM), BF16),
        f((B, T, POOL_WIDTH), BF16), f((B, HIST_ROWS, POOL_WIDTH), F32),
    )
    tile = lambda wdt: pl.BlockSpec((1, tm, wdt), lambda b, j: (b, j, 0))
    const2 = lambda s: pl.BlockSpec(s, lambda b, j: (0, 0))
    in_specs = [
        tile(D), const2((1, D)), const2((D, IN_WIDTH_PAD)),
        pl.BlockSpec((3, tm, LANES), lambda b, j: (0, j, 0)),
        pl.BlockSpec((3, tm, LANES), lambda b, j: (0, j, 0)),
        pl.BlockSpec((1, HIST_ROWS, POOL_WIDTH), lambda b, j: (b, 0, 0)),
        pl.BlockSpec((len(POOL_WINDOWS), POOL_GROUP_DIM, POOL_GROUP_DIM), lambda b, j: (0, 0, 0)),
        const2((1, POOL_WIDTH)),
    ]
    out_specs = (
        tile(ATT_WIDTH), tile(ATT_WIDTH), tile(KV_WIDTH), tile(KV_WIDTH), tile(LANES),
        tile(KV_WIDTH), tile(KV_WIDTH), tile(IDX_DIM), tile(POOL_WIDTH),
        pl.BlockSpec((1, HIST_ROWS, POOL_WIDTH), lambda b, j: (b, 0, 0)),
    )
    return pl.pallas_call(
        functools.partial(_inproj_kernel, tm=tm, pos0=pos0),
        out_shape=out_shape, grid=(B, nt), in_specs=in_specs, out_specs=out_specs,
        scratch_shapes=[pltpu.VMEM((HIST_ROWS + tm, POOL_WIDTH), F32)],
        compiler_params=pltpu.CompilerParams(dimension_semantics=("parallel", "arbitrary"),
                                             vmem_limit_bytes=VMEM_LIMIT),
        name="inproj",
    )(x, g, w_bf, rope, ropeki, hist, pw_bf, ps)


def _dsa_kernel(q_ref, qi_ref, kiwi_ref, k_ref, vt_ref, ki_ref, x_ref, yp_ref, wo_ref,
                o_ref, key_buf, bias_buf, m_scr, l_scr, acc_scr, s_scr,
                *, tk, pos0, n_keys, n_sel):
    tq = LANES
    slab = 64
    j = pl.program_id(1)
    base = pos0 + j * tq
    pos = base + lax.broadcasted_iota(I32, (1, tq), 1)
    limit = jnp.minimum((pos // CHUNK + 1) * CHUNK, n_keys)
    limit_max = jnp.minimum(((base + tq - 1) // CHUNK + 1) * CHUNK, n_keys)
    nkb = (limit_max + tk - 1) // tk
    nt = (((1,), (1,)), ((), ()))

    wi_t = jnp.transpose(kiwi_ref[0])[IDX_DIM:IDX_DIM + N_IDX_HEADS, :]

    def score_block(kb, carry):
        off = pl.multiple_of(kb * tk, tk)
        kiblk = ki_ref[0, pl.ds(off, tk), :]
        idx = jnp.zeros((tk, tq), F32)
        for h in range(N_IDX_HEADS):
            sc = lax.dot_general(kiblk, qi_ref[0, h], nt, preferred_element_type=F32)
            idx = idx + jnp.maximum(sc, 0.0) * wi_t[h:h + 1, :]
        idx = jnp.where(idx == 0.0, 0.0, idx)
        bits = lax.bitcast_convert_type(idx, I32)
        key = bits ^ ((bits >> 31) & 0x7FFFFFFF)
        kidx = kb * tk + lax.broadcasted_iota(I32, (tk, tq), 0)
        key_buf[kb] = jnp.where(kidx < limit, key, INT_MIN)
        return carry

    lax.fori_loop(0, nkb, score_block, 0)

    def col_sum(a):
        return jnp.sum(a, axis=0, keepdims=True)

    def count_ge(cand):
        def body(kb, acc):
            kblk = key_buf[kb]
            for c in range(tk // slab):
                acc = acc + jnp.where(kblk[c * slab:(c + 1) * slab] >= cand, 1.0, 0.0)
            return acc
        return col_sum(lax.fori_loop(0, nkb, body, jnp.zeros((slab, tq), F32)))

    kf = float(n_sel)
    t0 = jnp.where(count_ge(jnp.zeros((1, tq), I32)) >= kf, 0, INT_MIN).astype(I32)

    def bit_body(i, t):
        cand = t | lax.shift_left(jnp.int32(1), 30 - i)
        return jnp.where(count_ge(cand) >= kf, cand, t)

    t = lax.fori_loop(0, 31, bit_body, t0)

    def count_gt_ge(t):
        def body(kb, accs):
            a_gt, a_ge = accs
            kblk = key_buf[kb]
            for c in range(tk // slab):
                blk = kblk[c * slab:(c + 1) * slab]
                a_gt = a_gt + jnp.where(blk > t, 1.0, 0.0)
                a_ge = a_ge + jnp.where(blk >= t, 1.0, 0.0)
            return a_gt, a_ge
        z = jnp.zeros((slab, tq), F32)
        a_gt, a_ge = lax.fori_loop(0, nkb, body, (z, z))
        return col_sum(a_gt), col_sum(a_ge)

    cnt_gt, cnt_ge = count_gt_ge(t)
    need = kf - cnt_gt
    cnt_eq = cnt_ge - cnt_gt
    overfull = jnp.where(t != INT_MIN, cnt_eq - need, 0.0)
    slow = jnp.max(overfull) > 0.0

    @pl.when(jnp.logical_not(slow))
    def _():
        t_adm = jnp.maximum(t, INT_MIN + 1)

        def body(kb, carry):
            bias_buf[kb] = jnp.where(key_buf[kb] >= t_adm, 0.0, NEG_BIG)
            return carry
        lax.fori_loop(0, nkb, body, 0)

    @pl.when(slow)
    def _():
        tri = jnp.where(lax.broadcasted_iota(I32, (tk, tk), 1) <= lax.broadcasted_iota(I32, (tk, tk), 0),
                        1.0, 0.0).astype(BF16)

        def body(kb, seen):
            kblk = key_buf[kb]
            eq = jnp.where((kblk == t) & (kblk != INT_MIN), 1.0, 0.0)
            prefix = jnp.dot(tri, eq.astype(BF16), preferred_element_type=F32) + seen
            keep_tie = jnp.where(prefix <= need, eq, 0.0)
            sel = jnp.where(kblk > t, 1.0, keep_tie)
            bias_buf[kb] = jnp.where(sel > 0.0, 0.0, NEG_BIG)
            return seen + col_sum(eq)
        lax.fori_loop(0, nkb, body, jnp.zeros((1, tq), F32))

    m_scr[...] = jnp.full(m_scr.shape, NEG_BIG, F32)
    l_scr[...] = jnp.zeros(l_scr.shape, F32)
    acc_scr[...] = jnp.zeros(acc_scr.shape, F32)
    sub = LANES
    nsub = tk // sub

    def attn_block(kb, carry):
        for c in range(nsub):
            off = pl.multiple_of(kb * tk + c * sub, sub)
            for g in range(N_KV_HEADS):
                s_scr[c, g] = lax.dot_general(
                    k_ref[0, g, pl.ds(off, sub), :],
                    q_ref[0, Q_PER_KV * g:Q_PER_KV * (g + 1)].reshape(Q_PER_KV * tq, HEAD_DIM),
                    nt, preferred_element_type=F32)
        m = [m_scr[g] for g in range(N_KV_HEADS)]
        l = [l_scr[g] for g in range(N_KV_HEADS)]
        for c in range(nsub):
            bias = bias_buf[kb, c * sub:(c + 1) * sub, :]
            bias2 = jnp.concatenate([bias] * Q_PER_KV, axis=1)
            for g in range(N_KV_HEADS):
                s = s_scr[c, g] + bias2
                m_new = jnp.maximum(m[g], jnp.max(s, axis=0, keepdims=True))
                alpha = jnp.exp(m[g] - m_new)
                p = jnp.exp(s - m_new)
                l[g] = alpha * l[g] + col_sum(p)
                pv = jnp.dot(vt_ref[0, kb * nsub + c, g * HEAD_DIM:(g + 1) * HEAD_DIM, :], p.astype(BF16),
                             preferred_element_type=F32)
                acc_scr[g] = alpha * acc_scr[g] + pv
                m[g] = m_new
        for g in range(N_KV_HEADS):
            m_scr[g] = m[g]
            l_scr[g] = l[g]
        return carry

    lax.fori_loop(0, nkb, attn_block, 0)

    o_t = []
    for g in range(N_KV_HEADS):
        og = acc_scr[g] / l_scr[g]
        o_t.extend(og[:, hh * tq:(hh + 1) * tq] for hh in range(Q_PER_KV))
    y_att = jnp.transpose(jnp.concatenate(o_t, axis=0)).astype(BF16)
    y = jnp.dot(yp_ref[0], wo_ref[0:POOL_WIDTH, :], preferred_element_type=F32)
    y = y + jnp.dot(y_att, wo_ref[POOL_WIDTH:POOL_WIDTH + ATT_WIDTH, :], preferred_element_type=F32)
    o_ref[0] = x_ref[0] + y


def _dsa(q, qi, kiwi, k_all, v_all, ki_all, x, yp, wo_bf, *, tk, pos0, n_keys, n_sel):
    B, T, D = x.shape
    L = k_all.shape[1]
    tq = LANES
    assert L % tk == 0 and T % tq == 0
    nkb_max = L // tk
    heads = lambda a, n: jnp.transpose(a.reshape(a.shape[0], a.shape[1], n, a.shape[2] // n), (0, 2, 1, 3))
    q_hm, qi_hm, k_hm = heads(q, N_HEADS), heads(qi, N_IDX_HEADS), heads(k_all, N_KV_HEADS)
    v_t = jnp.transpose(v_all.reshape(B, L // LANES, LANES, KV_WIDTH), (0, 1, 3, 2))
    tile = lambda wdt: pl.BlockSpec((1, tq, wdt), lambda b, j: (b, j, 0))
    in_specs = [pl.BlockSpec((1, N_HEADS, tq, HEAD_DIM), lambda b, j: (b, 0, j, 0)),
                pl.BlockSpec((1, N_IDX_HEADS, tq, IDX_DIM), lambda b, j: (b, 0, j, 0)),
                tile(LANES),
                pl.BlockSpec((1, N_KV_HEADS, L, HEAD_DIM), lambda b, j: (b, 0, 0, 0)),
                pl.BlockSpec((1, L // LANES, KV_WIDTH, LANES), lambda b, j: (b, 0, 0, 0)),
                pl.BlockSpec((1, L, IDX_DIM), lambda b, j: (b, 0, 0)),
                tile(D), tile(POOL_WIDTH), pl.BlockSpec((POOL_WIDTH + ATT_WIDTH, D), lambda b, j: (0, 0))]
    return pl.pallas_call(
        functools.partial(_dsa_kernel, tk=tk, pos0=pos0, n_keys=n_keys, n_sel=n_sel),
        out_shape=jax.ShapeDtypeStruct((B, T, D), F32), grid=(B, T // tq),
        in_specs=in_specs, out_specs=tile(D),
        scratch_shapes=[pltpu.VMEM((nkb_max, tk, tq), I32), pltpu.VMEM((nkb_max, tk, tq), F32),
                        pltpu.VMEM((N_KV_HEADS, 1, Q_PER_KV * tq), F32),
                        pltpu.VMEM((N_KV_HEADS, 1, Q_PER_KV * tq), F32),
                        pltpu.VMEM((N_KV_HEADS, HEAD_DIM, Q_PER_KV * tq), F32),
                        pltpu.VMEM((tk // LANES, N_KV_HEADS, LANES, Q_PER_KV * tq), F32)],
        compiler_params=pltpu.CompilerParams(dimension_semantics=("parallel", "arbitrary"),
                                             vmem_limit_bytes=VMEM_LIMIT),
        name="dsa",
    )(q_hm, qi_hm, kiwi, k_hm, v_t, ki_all, x, yp, wo_bf)


def _gmlp_kernel(x_ref, g_ref, win_ref, lng_ref, lnb_ref, ws_ref, bias_ref, wout_ref, o_ref, v_ref, *, tm):
    x = x_ref[...]
    h = _rms(x, g_ref[...])
    z = jax.nn.gelu(jnp.dot(h.astype(BF16), win_ref[...], preferred_element_type=F32))
    half = z.shape[1] // 2
    u, v = z[:, :half], z[:, half:]
    mu = jnp.mean(v, axis=-1, keepdims=True)
    var = jnp.mean(jnp.square(v - mu), axis=-1, keepdims=True)
    vn = (v - mu) * lax.rsqrt(var + LN_EPS) * lng_ref[...] + lnb_ref[...]
    v_ref[...] = vn
    gd = half // GM_GROUPS
    gated = []
    for c in range(tm // GM_CHUNK):
        rows = slice(c * GM_CHUNK, (c + 1) * GM_CHUNK)
        vc = vn[rows].astype(BF16)
        mixed = jnp.concatenate(
            [jnp.dot(ws_ref[0, g], vc[:, g * gd:(g + 1) * gd], preferred_element_type=F32)
             for g in range(GM_GROUPS)], axis=1) + bias_ref[0]
        gated.append((u[rows] * mixed).astype(BF16))
    gated = jnp.concatenate(gated, axis=0)
    o_ref[...] = x + jnp.dot(gated, wout_ref[...], preferred_element_type=F32)


def _gmlp(x, g, win_bf, lng, lnb, ws2, bias2, wout_bf, *, tm, n_first, n_v_rows):
    N, D = x.shape
    half = win_bf.shape[1] // 2
    nt = N // tm
    t_first = n_first // tm
    variant = lambda i: jnp.where(i >= t_first, 1, 0)
    row = pl.BlockSpec((tm, D), lambda i: (i, 0))
    const = lambda s: pl.BlockSpec(s, lambda i: (0, 0))
    in_specs = [row, const((1, D)), const((D, 2 * half)), const((1, half)), const((1, half)),
                pl.BlockSpec((1, GM_GROUPS, GM_CHUNK, GM_CHUNK), lambda i: (variant(i), 0, 0, 0)),
                pl.BlockSpec((1, GM_CHUNK, half), lambda i: (variant(i), 0, 0)),
                const((half, D))]
    out_specs = (row, pl.BlockSpec((tm, half), lambda i: (jnp.maximum(i - t_first, 0), 0)))
    return pl.pallas_call(
        functools.partial(_gmlp_kernel, tm=tm),
        out_shape=(jax.ShapeDtypeStruct((N, D), F32), jax.ShapeDtypeStruct((n_v_rows, half), F32)),
        grid=(nt,), in_specs=in_specs, out_specs=out_specs,
        compiler_params=pltpu.CompilerParams(dimension_semantics=("arbitrary",), vmem_limit_bytes=VMEM_LIMIT),
        name="gmlp",
    )(x, g, win_bf, lng, lnb, ws2, bias2, wout_bf)


ROUTE_E0, ROUTE_E1, ROUTE_R0, ROUTE_R1, ROUTE_G0, ROUTE_G1 = range(6)


def _router_kernel(x_ref, g_ref, wr_ref, route_ref, cnt_ref, carry_ref, *, tm):
    i = pl.program_id(0)

    @pl.when(i == 0)
    def _():
        carry_ref[...] = jnp.zeros(carry_ref.shape, F32)

    h = _rms(x_ref[...], g_ref[...])
    logits = jnp.dot(h, wr_ref[...], preferred_element_type=F32, precision=lax.Precision.HIGHEST)
    lane = lax.broadcasted_iota(I32, (tm, LANES), 1).astype(F32)
    ninf = -jnp.inf
    big = 1e9
    rmax = lambda a: jnp.max(a, axis=1, keepdims=True)
    rmin = lambda a: jnp.min(a, axis=1, keepdims=True)
    rsum = lambda a: jnp.sum(a, axis=1, keepdims=True)

    is_grp = lane < N_EXPERT_GROUPS
    lg = jnp.where(is_grp, logits, ninf)
    mg = rmax(lg)
    g_sel = rmin(jnp.where(lg == mg, lane, big))
    p_grp = 1.0 / rsum(jnp.where(is_grp, jnp.exp(lg - mg), 0.0))
    lo = N_EXPERT_GROUPS + g_sel * EXPERTS_PER_GROUP
    le = jnp.where((lane >= lo) & (lane < lo + EXPERTS_PER_GROUP), logits, ninf)
    v1 = rmax(le)
    j1 = rmin(jnp.where(le == v1, lane, big))
    le2 = jnp.where(lane == j1, ninf, le)
    v2 = rmax(le2)
    j2 = rmin(jnp.where(le2 == v2, lane, big))
    e0 = j1 - N_EXPERT_GROUPS
    e1 = j2 - N_EXPERT_GROUPS
    r = jnp.exp(v2 - v1)
    g0 = p_grp / (1.0 + r)
    g1 = p_grp * r / (1.0 + r)

    oh0 = jnp.where(lane == e0, 1.0, 0.0)
    oh1 = jnp.where(lane == e1, 1.0, 0.0)
    oh = oh0 + oh1
    lower = jnp.where(lax.broadcasted_iota(I32, (tm, tm), 1) < lax.broadcasted_iota(I32, (tm, tm), 0),
                      1.0, 0.0).astype(BF16)
    before = jnp.dot(lower, oh.astype(BF16), preferred_element_type=F32) + carry_ref[...]
    r0 = rsum(oh0 * before)
    r1 = rsum(oh1 * before)
    carry_ref[...] = carry_ref[...] + jnp.sum(oh, axis=0, keepdims=True)
    cnt_ref[...] = carry_ref[...]

    route = jnp.zeros((tm, LANES), F32)
    for col, val in ((ROUTE_E0, e0), (ROUTE_E1, e1), (ROUTE_R0, r0), (ROUTE_R1, r1), (ROUTE_G0, g0), (ROUTE_G1, g1)):
        route = jnp.where(lane == col, val, route)
    route_ref[...] = route


def _router(x, g, wr, *, tm):
    N, D = x.shape
    row = lambda wdt: pl.BlockSpec((tm, wdt), lambda i: (i, 0))
    const = lambda s: pl.BlockSpec(s, lambda i: (0, 0))
    return pl.pallas_call(
        functools.partial(_router_kernel, tm=tm),
        out_shape=(jax.ShapeDtypeStruct((N, LANES), F32), jax.ShapeDtypeStruct((1, LANES), F32)),
        grid=(N // tm,), in_specs=[row(D), const((1, D)), const((D, LANES))],
        out_specs=(row(LANES), const((1, LANES))),
        scratch_shapes=[pltpu.VMEM((1, LANES), F32)],
        compiler_params=pltpu.CompilerParams(dimension_semantics=("arbitrary",), vmem_limit_bytes=VMEM_LIMIT),
        name="router",
    )(x, g, wr)


def _row_copy(src_ref, src_row, dst_ref, dst_row, sem):
    return pltpu.make_async_copy(src_ref.at[pl.ds(src_row, 1)], dst_ref.at[pl.ds(dst_row, 1)], sem)


def _dispatch_kernel(dest_ref, pad_ref, x_ref, g_ref, rows_ref, h_scr, zblk, sems, *, tm):
    i = pl.program_id(0)
    n = pl.num_programs(0)
    slot = i % 2

    def wait_slot(s):
        for _ in range(2):
            pltpu.make_async_copy(h_scr.at[s], rows_ref.at[pl.ds(0, tm)], sems.at[s]).wait()

    @pl.when(i >= 2)
    def _():
        wait_slot(slot)

    h_scr[slot] = _rms(x_ref[...], g_ref[...])

    def body(r, carry):
        for a in range(2):
            _row_copy(h_scr.at[slot], r, rows_ref, dest_ref[a, r], sems.at[slot]).start()
        return carry

    lax.fori_loop(0, tm, body, 0, unroll=8)

    @pl.when(i == n - 1)
    def _():
        wait_slot(slot)

    @pl.when(jnp.logical_and(i == n - 1, n >= 2))
    def _():
        wait_slot(1 - slot)

    @pl.when(i == n - 1)
    def _():
        zblk[...] = jnp.zeros(zblk.shape, F32)
        for e in range(N_EXPERTS):
            first, count = pad_ref[0, e], pad_ref[1, e]

            def start(k, carry):
                _row_copy(zblk, 0, rows_ref, first + k, sems.at[2]).start()
                return carry

            def wait(k, carry):
                _row_copy(zblk, 0, rows_ref, first, sems.at[2]).wait()
                return carry

            lax.fori_loop(0, count, start, 0)
            lax.fori_loop(0, count, wait, 0)

        def blk_copy(b):
            return pltpu.make_async_copy(zblk, rows_ref.at[pl.ds(b * MOE_BLOCK, MOE_BLOCK)], sems.at[2])

        n_blocks = rows_ref.shape[0] // MOE_BLOCK
        n_used = pad_ref[2, 0]
        lax.fori_loop(n_used, n_blocks, lambda b, c: (blk_copy(b).start(), c)[1], 0)
        lax.fori_loop(n_used, n_blocks, lambda b, c: (blk_copy(n_used).wait(), c)[1], 0)


def _dispatch(dest_t, pad_info, x, g, *, tm, n_rows):
    N, D = x.shape
    return pl.pallas_call(
        functools.partial(_dispatch_kernel, tm=tm),
        out_shape=jax.ShapeDtypeStruct((n_rows, D), F32), grid=(N // tm,),
        in_specs=[pl.BlockSpec((2, tm), lambda i: (0, i), memory_space=pltpu.SMEM),
                  pl.BlockSpec(memory_space=pltpu.SMEM),
                  pl.BlockSpec((tm, D), lambda i: (i, 0)),
                  pl.BlockSpec((1, D), lambda i: (0, 0))],
        out_specs=pl.BlockSpec(memory_space=pl.ANY),
        scratch_shapes=[pltpu.VMEM((2, tm, D), F32), pltpu.VMEM((MOE_BLOCK, D), F32), pltpu.SemaphoreType.DMA((3,))],
        compiler_params=pltpu.CompilerParams(dimension_semantics=("arbitrary",), vmem_limit_bytes=VMEM_LIMIT),
        name="dispatch",
    )(dest_t, pad_info, x, g)


def _expert_kernel(be_ref, nu_ref, x_ref, wg_ref, wu_ref, wd_ref, y_ref):
    i = pl.program_id(0)

    @pl.when(i < nu_ref[0])
    def _():
        x = x_ref[...].astype(BF16)
        a = jnp.dot(x, wg_ref[0].astype(BF16), preferred_element_type=F32)
        u = jnp.dot(x, wu_ref[0].astype(BF16), preferred_element_type=F32)
        act = (a * jax.nn.sigmoid(a)) * u
        y_ref[...] = jnp.dot(act.astype(BF16), wd_ref[0].astype(BF16), preferred_element_type=F32)

    @pl.when(i >= nu_ref[0])
    def _():
        y_ref[...] = jnp.zeros(y_ref.shape, F32)


def _experts(block_e, n_used, x_rows, w_gate, w_up, w_down):
    n_rows, D = x_rows.shape
    de = w_gate.shape[2]
    n_blocks = n_rows // MOE_BLOCK
    grid_spec = pltpu.PrefetchScalarGridSpec(
        num_scalar_prefetch=2, grid=(n_blocks,),
        in_specs=[pl.BlockSpec((MOE_BLOCK, D), lambda i, be, nu: (jnp.minimum(i, nu[0] - 1), 0)),
                  pl.BlockSpec((1, D, de), lambda i, be, nu: (be[i], 0, 0)),
                  pl.BlockSpec((1, D, de), lambda i, be, nu: (be[i], 0, 0)),
                  pl.BlockSpec((1, de, D), lambda i, be, nu: (be[i], 0, 0))],
        out_specs=pl.BlockSpec((MOE_BLOCK, D), lambda i, be, nu: (i, 0)))
    return pl.pallas_call(
        _expert_kernel, out_shape=jax.ShapeDtypeStruct((n_rows, D), F32), grid_spec=grid_spec,
        compiler_params=pltpu.CompilerParams(dimension_semantics=("arbitrary",), vmem_limit_bytes=VMEM_LIMIT),
        name="experts",
    )(block_e, n_used, x_rows, w_gate, w_up, w_down)


def _combine_kernel(dcur_ref, dnxt_ref, x_ref, route_ref, g_ref, y_ref, o_ref, ybuf, sems, *, tm, final_norm):
    i = pl.program_id(0)
    n = pl.num_programs(0)
    slot = i % 2

    def gather(dref, s):
        def body(r, carry):
            for a in range(2):
                _row_copy(y_ref, dref[a, r], ybuf.at[s, a], r, sems.at[s]).start()
            return carry
        lax.fori_loop(0, tm, body, 0, unroll=8)

    @pl.when(i == 0)
    def _():
        gather(dcur_ref, 0)

    @pl.when(i + 1 < n)
    def _():
        gather(dnxt_ref, 1 - slot)

    for a in range(2):
        pltpu.make_async_copy(y_ref.at[pl.ds(0, tm)], ybuf.at[slot, a], sems.at[slot]).wait()
    route = route_ref[...]
    g0 = route[:, ROUTE_G0:ROUTE_G0 + 1]
    g1 = route[:, ROUTE_G1:ROUTE_G1 + 1]
    out = x_ref[...] + (ybuf[slot, 0] * g0 + ybuf[slot, 1] * g1)
    if final_norm:
        out = _rms(out, g_ref[...])
    o_ref[...] = out


def _combine(dest_t, x, route, g, y_rows, *, tm, final_norm):
    N, D = x.shape
    nt = N // tm
    row = lambda wdt: pl.BlockSpec((tm, wdt), lambda i: (i, 0))
    return pl.pallas_call(
        functools.partial(_combine_kernel, tm=tm, final_norm=final_norm),
        out_shape=jax.ShapeDtypeStruct((N, D), F32), grid=(nt,),
        in_specs=[pl.BlockSpec((2, tm), lambda i: (0, i), memory_space=pltpu.SMEM),
                  pl.BlockSpec((2, tm), lambda i: (0, jnp.minimum(i + 1, nt - 1)), memory_space=pltpu.SMEM),
                  row(D), row(LANES), pl.BlockSpec((1, D), lambda i: (0, 0)),
                  pl.BlockSpec(memory_space=pl.ANY)],
        out_specs=row(D),
        scratch_shapes=[pltpu.VMEM((2, 2, tm, D), F32), pltpu.SemaphoreType.DMA((2,))],
        compiler_params=pltpu.CompilerParams(dimension_semantics=("arbitrary",), vmem_limit_bytes=VMEM_LIMIT),
        name="combine",
    )(dest_t, dest_t, x, route, g, y_rows)


def _hier_moe(x, g_ffn, w_rg, w_re, w_gate, w_up, w_down, g_final, *, final_norm):
    N, D = x.shape
    tm = 256
    wr = jnp.concatenate([w_rg, w_re, jnp.zeros((D, LANES - N_EXPERT_GROUPS - N_EXPERTS), F32)], axis=1)
    g_ffn = g_ffn.reshape(1, D)
    route, counts = _router(x, g_ffn, wr, tm=tm)
    counts = counts[0, :N_EXPERTS].astype(I32)
    eid_t = jnp.transpose(route[:, ROUTE_E0:ROUTE_E1 + 1]).astype(I32)
    rank_t = jnp.transpose(route[:, ROUTE_R0:ROUTE_R1 + 1]).astype(I32)
    padded = (counts + MOE_BLOCK - 1) // MOE_BLOCK * MOE_BLOCK
    pad_end = jnp.cumsum(padded)
    pad_start = pad_end - padded
    expert_ids = jnp.arange(N_EXPERTS, dtype=I32)
    start_of = jnp.sum(jnp.where(eid_t[:, :, None] == expert_ids, pad_start, 0), axis=-1)
    dest_t = start_of + rank_t
    n_blocks = -(-(2 * N) // MOE_BLOCK) + N_EXPERTS
    n_rows = n_blocks * MOE_BLOCK
    n_used = (pad_end[-1] // MOE_BLOCK).astype(I32)
    block_start = jnp.minimum(jnp.arange(n_blocks, dtype=I32), n_used - 1) * MOE_BLOCK
    block_e = jnp.minimum(jnp.sum(block_start[:, None] >= pad_end[None, :], axis=1), N_EXPERTS - 1).astype(I32)
    pad_info = jnp.stack([pad_start + counts, padded - counts, jnp.broadcast_to(n_used, (N_EXPERTS,))])
    x_rows = _dispatch(dest_t, pad_info, x, g_ffn, tm=tm, n_rows=n_rows)
    y_rows = _experts(block_e, n_used.reshape(1), x_rows, w_gate, w_up, w_down)
    return _combine(dest_t, x, route, g_final.reshape(1, D), y_rows, tm=tm, final_norm=final_norm)


def _rope_tables(pos, rot_lanes):
    inv = 1.0 / (ROPE_THETA ** (jnp.arange(ROT_HALF, dtype=F32) / ROT_HALF))
    ang = pos.astype(F32)[:, None] * inv[None, :]
    cos, sin = jnp.cos(ang), jnp.sin(ang)
    lane = jnp.arange(LANES)
    r = lane % HEAD_DIM
    active = (lane < rot_lanes)
    first = active & (r < ROT_HALF)
    second = active & (r >= ROT_HALF) & (r < 2 * ROT_HALF)
    cos_l = cos[:, r % ROT_HALF]
    sin_l = sin[:, r % ROT_HALF]
    c = jnp.where((first | second)[None, :], cos_l, 1.0)
    sa = jnp.where(first[None, :], -sin_l, 0.0)
    sb = jnp.where(second[None, :], sin_l, 0.0)
    return jnp.stack([c, sa, sb]).astype(F32)


def _prep_w_in(w_in):
    D = w_in.shape[0]
    xp, q, k, v, qi, ki, wi = jnp.split(w_in, [512, 1024, 1280, 1536, 2048, 2112], axis=1)
    pad = jnp.zeros((D, IN_WIDTH_PAD - w_in.shape[1]), w_in.dtype)
    return jnp.concatenate([xp, q, k, v, qi, ki, wi, pad], axis=1).astype(BF16)


def _pad_keys(a, l_pad):
    return jnp.pad(a, ((0, 0), (0, l_pad - a.shape[1]), (0, 0)))


def kernel(x_prompt, x_sample, cache_k, cache_v, cache_idx_k, state_pool, norm_mix, norm_ffn, norm_final,
           par_w_in, par_pool_w, par_pool_scale, par_w_out, gm_w_in, gm_ln_g, gm_ln_b, gm_ws, gm_bs, gm_w_out,
           moe_router_group, moe_router_expert, moe_w_gate, moe_w_up, moe_w_down):
    Bp, Tp, D = x_prompt.shape
    Bs, Ts, _ = x_sample.shape
    past = cache_k.shape[2]
    Np, Ns = Bp * Tp, Bs * Ts
    depth = norm_mix.shape[0]
    assert depth == 2 and Ts == CHUNK and Tp % 256 == 0 and Ns % 256 == 0

    w_in_bf = _prep_w_in(par_w_in[0])
    pw_bf = par_pool_w[0].astype(BF16)
    ps = par_pool_scale[0].reshape(1, POOL_WIDTH)
    wo_bf = par_w_out[0].astype(BF16)
    g_mix0 = norm_mix[0].reshape(1, D)
    pos_p = jnp.arange(Tp, dtype=I32)
    pos_s = past + jnp.arange(Ts, dtype=I32)

    hist_p = jnp.zeros((Bp, HIST_ROWS, POOL_WIDTH), F32)
    hist_s = jnp.pad(state_pool[0], ((0, 0), (1, 0), (0, 0)))
    tk = 512
    (q_p, qi_p, k_p, v_p, kiwi_p, kbf_p, vbf_p, kibf_p, yp_p, st_p) = _inproj(
        x_prompt, g_mix0, w_in_bf, _rope_tables(pos_p, LANES), _rope_tables(pos_p, IDX_DIM), hist_p, pw_bf, ps,
        tm=256, pos0=0)
    (q_s, qi_s, k_s, v_s, kiwi_s, kbf_s, vbf_s, kibf_s, yp_s, st_s) = _inproj(
        x_sample, g_mix0, w_in_bf, _rope_tables(pos_s, LANES), _rope_tables(pos_s, IDX_DIM), hist_s, pw_bf, ps,
        tm=Ts, pos0=past)

    lp = -(-Tp // tk) * tk
    x1_p = _dsa(q_p, qi_p, kiwi_p, _pad_keys(kbf_p, lp), _pad_keys(vbf_p, lp), _pad_keys(kibf_p, lp),
                x_prompt, yp_p, wo_bf, tk=tk, pos0=0, n_keys=Tp, n_sel=min(TOPK_MAX, Tp // 4))
    ls = past + Ts
    lsp = -(-ls // tk) * tk
    kall = _pad_keys(jnp.concatenate([cache_k[0].reshape(Bs, past, KV_WIDTH).astype(BF16), kbf_s], axis=1), lsp)
    vall = _pad_keys(jnp.concatenate([cache_v[0].reshape(Bs, past, KV_WIDTH).astype(BF16), vbf_s], axis=1), lsp)
    kiall = _pad_keys(jnp.concatenate([cache_idx_k[0].astype(BF16), kibf_s], axis=1), lsp)
    qpad = lambda a: jnp.pad(a, ((0, 0), (0, LANES - Ts), (0, 0)))
    x1_s = _dsa(qpad(q_s), qpad(qi_s), qpad(kiwi_s), kall, vall, kiall, qpad(x_sample), qpad(yp_s), wo_bf,
                tk=tk, pos0=past, n_keys=ls, n_sel=min(TOPK_MAX, ls // 4))[:, :Ts]

    x = jnp.concatenate([x1_p.reshape(Np, D), x1_s.reshape(Ns, D)], axis=0)
    x = _hier_moe(x, norm_ffn[0], moe_router_group[0], moe_router_expert[0],
                  moe_w_gate[0], moe_w_up[0], moe_w_down[0], norm_final, final_norm=False)

    cs = Ts
    tril = lambda n: jnp.tril(jnp.ones((n, n), bool))
    ws_p = jnp.where(tril(GM_CHUNK)[None], gm_ws[0], 0.0)
    ws_small = jnp.where(tril(cs)[None], gm_ws[0][:, :cs, :cs], 0.0)
    rep = GM_CHUNK // cs
    ws_s = jnp.einsum('ab,gts->gatbs', jnp.eye(rep, dtype=F32), ws_small).reshape(GM_GROUPS, GM_CHUNK, GM_CHUNK)
    ws2 = jnp.stack([ws_p, ws_s]).astype(BF16)
    gd = D // GM_GROUPS
    bias_p = jnp.repeat(jnp.transpose(gm_bs[0]), gd, axis=1)
    bias_s = jnp.tile(jnp.repeat(jnp.transpose(gm_bs[0][:, :cs]), gd, axis=1), (rep, 1))
    bias2 = jnp.stack([bias_p, bias_s])
    x, gm_v = _gmlp(x, norm_mix[1].reshape(1, D), gm_w_in[0].astype(BF16), gm_ln_g[0].reshape(1, D),
                    gm_ln_b[0].reshape(1, D), ws2, bias2, gm_w_out[0].astype(BF16),
                    tm=256, n_first=Np, n_v_rows=Ns)
    x = _hier_moe(x, norm_ffn[1], moe_router_group[1], moe_router_expert[1],
                  moe_w_gate[1], moe_w_up[1], moe_w_down[1], norm_final, final_norm=True)

    y_p = x[:Np].reshape(Bp, Tp, D)
    y_s = x[Np:].reshape(Bs, Ts, D)
    r4 = lambda a, b, t: a.reshape(1, b, t, N_KV_HEADS, HEAD_DIM)
    return (y_p, y_s,
            r4(k_p, Bp, Tp), r4(v_p, Bp, Tp), kiwi_p[:, :, :IDX_DIM][None], st_p[:, 1:][None],
            r4(k_s, Bs, Ts), r4(v_s, Bs, Ts), kiwi_s[:, :, :IDX_DIM][None], st_s[:, 1:][None],
            gm_v.reshape(1, Bs, Ts, D))
```

```python
import functools

import jax
import jax.numpy as jnp
from jax import lax
from jax.experimental import pallas as pl
from jax.experimental.pallas import tpu as pltpu

F32 = jnp.float32
BF16 = jnp.bfloat16
I32 = jnp.int32
I16 = jnp.int16
BF16_COUNT = jnp.bfloat16

LANES = 128
CHUNK = 64
POOL_WINDOWS = (2, 4, 8, 16)
POOL_GROUP_DIM = 128
POOL_WIDTH = 512
HIST_ROWS = 16
N_HEADS = 8
HEAD_DIM = 64
N_KV_HEADS = 4
Q_PER_KV = N_HEADS // N_KV_HEADS
ATT_WIDTH = N_HEADS * HEAD_DIM
KV_WIDTH = N_KV_HEADS * HEAD_DIM
N_IDX_HEADS = 8
IDX_DIM = 64
TOPK_MAX = 256
ROPE_THETA = 500000.0
ROT_HALF = HEAD_DIM // 8
GM_CHUNK = 128
GM_GROUPS = 8
N_EXPERT_GROUPS = 4
EXPERTS_PER_GROUP = 8
N_EXPERTS = 32
MOE_BLOCK = 256
RMS_EPS = 1e-6
LN_EPS = 1e-5

INT_MIN = -2147483648
I16_MIN = -32768
NEG_BIG = -1e30
VMEM_LIMIT = 48 * 1024 * 1024

COL_XP, COL_Q, COL_K, COL_V, COL_QI, COL_KIWI = 0, 512, 1024, 1280, 1536, 2048
IN_WIDTH_PAD = 2176


def _rms(x, g):
    return x * lax.rsqrt(jnp.mean(x * x, axis=-1, keepdims=True) + RMS_EPS) * g


def _rope128(x, c, sa, sb):
    return x * c + pltpu.roll(x, LANES - ROT_HALF, 1) * sa + pltpu.roll(x, ROT_HALF, 1) * sb


def _inproj_kernel(x_ref, g_ref, w_ref, rope_ref, ropeki_ref, hist_ref, pw_ref, ps_ref,
                   q_ref, qi_ref, k_ref, v_ref, kiwi_ref, kbf_ref, vbf_ref, kibf_ref, yp_ref, state_ref,
                   buf_ref, *, tm, pos0):
    j = pl.program_id(1)
    h = _rms(x_ref[0], g_ref[...])
    proj = jnp.dot(h.astype(BF16), w_ref[...], preferred_element_type=F32)
    c, sa, sb = rope_ref[0], rope_ref[1], rope_ref[2]

    def put_heads(ref, i, chunk):
        ref[0, 2 * i] = chunk[:, :HEAD_DIM].astype(BF16)
        ref[0, 2 * i + 1] = chunk[:, HEAD_DIM:].astype(BF16)

    for i in range(ATT_WIDTH // LANES):
        put_heads(q_ref, i, _rope128(proj[:, COL_Q + i * LANES:COL_Q + (i + 1) * LANES], c, sa, sb)
                  * (HEAD_DIM ** -0.5))
        put_heads(qi_ref, i, _rope128(proj[:, COL_QI + i * LANES:COL_QI + (i + 1) * LANES], c, sa, sb))
    for i in range(KV_WIDTH // LANES):
        sl = slice(i * LANES, (i + 1) * LANES)
        kr = _rope128(proj[:, COL_K + i * LANES:COL_K + (i + 1) * LANES], c, sa, sb)
        k_ref[0, :, sl] = kr
        put_heads(kbf_ref, i, kr)
    vv = proj[:, COL_V:COL_V + KV_WIDTH]
    v_ref[0] = vv
    if tm % LANES == 0:
        for cc in range(tm // LANES):
            vbf_ref[0, cc] = jnp.transpose(vv[cc * LANES:(cc + 1) * LANES, :]).astype(BF16)
    else:
        vbf_ref[0] = vv.astype(BF16)
    kiwi = _rope128(proj[:, COL_KIWI:COL_KIWI + LANES], ropeki_ref[0], ropeki_ref[1], ropeki_ref[2])
    kiwi_ref[0] = kiwi
    kibf_ref[0] = kiwi[:, :IDX_DIM].astype(BF16)

    @pl.when(j == 0)
    def _():
        buf_ref[0:HIST_ROWS, :] = hist_ref[0]

    xp = proj[:, COL_XP:COL_XP + POOL_WIDTH]
    buf_ref[HIST_ROWS:HIST_ROWS + tm, :] = xp
    pos = pos0 + j * tm + lax.broadcasted_iota(I32, (tm, 1), 0)
    for gi, w in enumerate(POOL_WINDOWS):
        c0 = gi * POOL_GROUP_DIM
        s = xp[:, c0:c0 + POOL_GROUP_DIM]
        for i in range(1, w):
            s = s + buf_ref[HIST_ROWS - i:HIST_ROWS - i + tm, c0:c0 + POOL_GROUP_DIM]
        cnt = jnp.minimum(pos + 1, w).astype(F32)
        d = s / cnt - xp[:, c0:c0 + POOL_GROUP_DIM]
        y = jnp.dot(d.astype(BF16), pw_ref[gi], preferred_element_type=F32)
        yp_ref[0, :, c0:c0 + POOL_GROUP_DIM] = (y * ps_ref[:, c0:c0 + POOL_GROUP_DIM]).astype(BF16)
    tail = buf_ref[tm:tm + HIST_ROWS, :]
    state_ref[0] = tail
    buf_ref[0:HIST_ROWS, :] = tail


def _inproj(x, g, w_bf, rope, ropeki, hist, pw_bf, ps, *, tm, pos0):
    B, T, D = x.shape
    nt = T // tm
    f = lambda shape, dt: jax.ShapeDtypeStruct(shape, dt)
    v_t = tm % LANES == 0
    out_shape = (
        f((B, N_HEADS, T, HEAD_DIM), BF16), f((B, N_IDX_HEADS, T, IDX_DIM), BF16),
        f((B, T, KV_WIDTH), F32), f((B, T, KV_WIDTH), F32), f((B, T, LANES), F32),
        f((B, N_KV_HEADS, T, HEAD_DIM), BF16),
        f((B, T // LANES, KV_WIDTH, LANES) if v_t else (B, T, KV_WIDTH), BF16),
        f((B, T, IDX_DIM), BF16),
        f((B, T, POOL_WIDTH), BF16), f((B, HIST_ROWS, POOL_WIDTH), F32),
    )
    tile = lambda wdt: pl.BlockSpec((1, tm, wdt), lambda b, j: (b, j, 0))
    heads = lambda n, wdt: pl.BlockSpec((1, n, tm, wdt), lambda b, j: (b, 0, j, 0))
    const2 = lambda s: pl.BlockSpec(s, lambda b, j: (0, 0))
    in_specs = [
        tile(D), const2((1, D)), const2((D, IN_WIDTH_PAD)),
        pl.BlockSpec((3, tm, LANES), lambda b, j: (0, j, 0)),
        pl.BlockSpec((3, tm, LANES), lambda b, j: (0, j, 0)),
        pl.BlockSpec((1, HIST_ROWS, POOL_WIDTH), lambda b, j: (b, 0, 0)),
        pl.BlockSpec((len(POOL_WINDOWS), POOL_GROUP_DIM, POOL_GROUP_DIM), lambda b, j: (0, 0, 0)),
        const2((1, POOL_WIDTH)),
    ]
    out_specs = (
        heads(N_HEADS, HEAD_DIM), heads(N_IDX_HEADS, IDX_DIM), tile(KV_WIDTH), tile(KV_WIDTH), tile(LANES),
        heads(N_KV_HEADS, HEAD_DIM),
        pl.BlockSpec((1, tm // LANES, KV_WIDTH, LANES), lambda b, j: (b, j, 0, 0)) if v_t else tile(KV_WIDTH),
        tile(IDX_DIM), tile(POOL_WIDTH),
        pl.BlockSpec((1, HIST_ROWS, POOL_WIDTH), lambda b, j: (b, 0, 0)),
    )
    return pl.pallas_call(
        functools.partial(_inproj_kernel, tm=tm, pos0=pos0),
        out_shape=out_shape, grid=(B, nt), in_specs=in_specs, out_specs=out_specs,
        scratch_shapes=[pltpu.VMEM((HIST_ROWS + tm, POOL_WIDTH), F32)],
        compiler_params=pltpu.CompilerParams(dimension_semantics=("parallel", "arbitrary"),
                                             vmem_limit_bytes=VMEM_LIMIT),
        name="inproj",
    )(x, g, w_bf, rope, ropeki, hist, pw_bf, ps)


def _dsa_kernel(q_ref, qi_ref, kiwi_ref, k_ref, vt_ref, ki_ref, x_ref, yp_ref, wo_ref,
                o_ref, key_buf, bias_buf, m_scr, l_scr, acc_scr, s_scr, hi_buf, lo_buf,
                *, tk, pos0, n_keys, n_sel):
    tq = LANES
    slab = 64
    j = pl.program_id(1)
    base = pos0 + j * tq
    pos = base + lax.broadcasted_iota(I32, (1, tq), 1)
    limit = jnp.minimum((pos // CHUNK + 1) * CHUNK, n_keys)
    limit_max = jnp.minimum(((base + tq - 1) // CHUNK + 1) * CHUNK, n_keys)
    nkb = (limit_max + tk - 1) // tk
    nt = (((1,), (1,)), ((), ()))

    wi_t = jnp.transpose(kiwi_ref[0])[IDX_DIM:IDX_DIM + N_IDX_HEADS, :]

    def score_block(kb, carry):
        off = pl.multiple_of(kb * tk, tk)
        kiblk = ki_ref[0, pl.ds(off, tk), :]
        idx = jnp.zeros((tk, tq), F32)
        for h in range(N_IDX_HEADS):
            sc = lax.dot_general(kiblk, qi_ref[0, h], nt, preferred_element_type=F32)
            idx = idx + jnp.maximum(sc, 0.0) * wi_t[h:h + 1, :]
        idx = jnp.where(idx == 0.0, 0.0, idx)
        bits = lax.bitcast_convert_type(idx, I32)
        key = bits ^ ((bits >> 31) & 0x7FFFFFFF)
        kidx = kb * tk + lax.broadcasted_iota(I32, (tk, tq), 0)
        key = jnp.where(kidx < limit, key, INT_MIN)
        key_buf[kb] = key
        hi_buf[kb] = (key >> 16).astype(I16)
        return carry

    lax.fori_loop(0, nkb, score_block, 0)

    def col_sum(a):
        return jnp.sum(a, axis=0, keepdims=True)

    one16 = jnp.ones((slab, tq), BF16_COUNT)
    zero16 = jnp.zeros((slab, tq), BF16_COUNT)

    def count16(buf, cand, strict):
        c16 = cand.astype(I16)

        def body(kb, acc):
            blk = buf[kb]
            for c in range(tk // slab):
                part = blk[c * slab:(c + 1) * slab]
                hit = (part > c16) if strict else (part >= c16)
                acc = acc + jnp.where(hit, one16, zero16)
            return acc
        return col_sum(lax.fori_loop(0, nkb, body, zero16).astype(F32))

    def kth_largest16(buf, k):
        v0 = jnp.where(count16(buf, jnp.zeros((1, tq), I32), False) >= k, 0, I16_MIN).astype(I32)

        def bit_body(i, v):
            cand = v | lax.shift_left(jnp.int32(1), 14 - i)
            return jnp.where(count16(buf, cand, False) >= k, cand, v)
        return lax.fori_loop(0, 15, bit_body, v0)

    kf = float(n_sel)
    t_hi = kth_largest16(hi_buf, kf)
    need_lo = kf - count16(hi_buf, t_hi, True)
    t_hi16 = t_hi.astype(I16)

    def low_block(kb, carry):
        lo = ((key_buf[kb] & 0xFFFF) + I16_MIN).astype(I16)
        lo_buf[kb] = jnp.where(hi_buf[kb] == t_hi16, lo, jnp.int16(I16_MIN))
        return carry

    lax.fori_loop(0, nkb, low_block, 0)
    t_lo = kth_largest16(lo_buf, need_lo)
    t = t_hi * 65536 + (t_lo - I16_MIN)

    def count_gt_ge(t):
        def body(kb, accs):
            a_gt, a_ge = accs
            kblk = key_buf[kb]
            for c in range(tk // slab):
                blk = kblk[c * slab:(c + 1) * slab]
                a_gt = a_gt + jnp.where(blk > t, 1.0, 0.0)
                a_ge = a_ge + jnp.where(blk >= t, 1.0, 0.0)
            return a_gt, a_ge
        z = jnp.zeros((slab, tq), F32)
        a_gt, a_ge = lax.fori_loop(0, nkb, body, (z, z))
        return col_sum(a_gt), col_sum(a_ge)

    cnt_gt, cnt_ge = count_gt_ge(t)
    need = kf - cnt_gt
    cnt_eq = cnt_ge - cnt_gt
    overfull = jnp.where(t != INT_MIN, cnt_eq - need, 0.0)
    slow = jnp.max(overfull) > 0.0

    @pl.when(jnp.logical_not(slow))
    def _():
        t_adm = jnp.maximum(t, INT_MIN + 1)

        def body(kb, carry):
            bias_buf[kb] = jnp.where(key_buf[kb] >= t_adm, 0.0, NEG_BIG)
            return carry
        lax.fori_loop(0, nkb, body, 0)

    @pl.when(slow)
    def _():
        tri = jnp.where(lax.broadcasted_iota(I32, (tk, tk), 1) <= lax.broadcasted_iota(I32, (tk, tk), 0),
                        1.0, 0.0).astype(BF16)

        def body(kb, seen):
            kblk = key_buf[kb]
            eq = jnp.where((kblk == t) & (kblk != INT_MIN), 1.0, 0.0)
            prefix = jnp.dot(tri, eq.astype(BF16), preferred_element_type=F32) + seen
            keep_tie = jnp.where(prefix <= need, eq, 0.0)
            sel = jnp.where(kblk > t, 1.0, keep_tie)
            bias_buf[kb] = jnp.where(sel > 0.0, 0.0, NEG_BIG)
            return seen + col_sum(eq)
        lax.fori_loop(0, nkb, body, jnp.zeros((1, tq), F32))

    m_scr[...] = jnp.full(m_scr.shape, NEG_BIG, F32)
    l_scr[...] = jnp.zeros(l_scr.shape, F32)
    acc_scr[...] = jnp.zeros(acc_scr.shape, F32)
    sub = LANES
    nsub = tk // sub

    def attn_block(kb, carry):
        for c in range(nsub):
            off = pl.multiple_of(kb * tk + c * sub, sub)
            for g in range(N_KV_HEADS):
                s_scr[c, g] = lax.dot_general(
                    k_ref[0, g, pl.ds(off, sub), :],
                    q_ref[0, Q_PER_KV * g:Q_PER_KV * (g + 1)].reshape(Q_PER_KV * tq, HEAD_DIM),
                    nt, preferred_element_type=F32)
        m = [m_scr[g] for g in range(N_KV_HEADS)]
        l = [l_scr[g] for g in range(N_KV_HEADS)]
        for c in range(nsub):
            bias = bias_buf[kb, c * sub:(c + 1) * sub, :]
            bias2 = jnp.concatenate([bias] * Q_PER_KV, axis=1)
            for g in range(N_KV_HEADS):
                s = s_scr[c, g] + bias2
                m_new = jnp.maximum(m[g], jnp.max(s, axis=0, keepdims=True))
                alpha = jnp.exp(m[g] - m_new)
                p = jnp.exp(s - m_new)
                l[g] = alpha * l[g] + col_sum(p)
                pv = jnp.dot(vt_ref[0, kb * nsub + c, g * HEAD_DIM:(g + 1) * HEAD_DIM, :], p.astype(BF16),
                             preferred_element_type=F32)
                acc_scr[g] = alpha * acc_scr[g] + pv
                m[g] = m_new
        for g in range(N_KV_HEADS):
            m_scr[g] = m[g]
            l_scr[g] = l[g]
        return carry

    lax.fori_loop(0, nkb, attn_block, 0)

    o_t = []
    for g in range(N_KV_HEADS):
        og = acc_scr[g] / l_scr[g]
        o_t.extend(og[:, hh * tq:(hh + 1) * tq] for hh in range(Q_PER_KV))
    y_att = jnp.transpose(jnp.concatenate(o_t, axis=0)).astype(BF16)
    y = jnp.dot(yp_ref[0], wo_ref[0:POOL_WIDTH, :], preferred_element_type=F32)
    y = y + jnp.dot(y_att, wo_ref[POOL_WIDTH:POOL_WIDTH + ATT_WIDTH, :], preferred_element_type=F32)
    o_ref[0] = x_ref[0] + y


def _dsa(q_hm, qi_hm, kiwi, k_hm, v_t, ki_all, x, yp, wo_bf, *, tk, pos0, n_keys, n_sel):
    B, T, D = x.shape
    L = k_hm.shape[2]
    tq = LANES
    assert L % tk == 0 and T % tq == 0 and L // 64 <= 256
    nkb_max = L // tk
    tile = lambda wdt: pl.BlockSpec((1, tq, wdt), lambda b, j: (b, j, 0))
    in_specs = [pl.BlockSpec((1, N_HEADS, tq, HEAD_DIM), lambda b, j: (b, 0, j, 0)),
                pl.BlockSpec((1, N_IDX_HEADS, tq, IDX_DIM), lambda b, j: (b, 0, j, 0)),
                tile(LANES),
                pl.BlockSpec((1, N_KV_HEADS, L, HEAD_DIM), lambda b, j: (b, 0, 0, 0)),
                pl.BlockSpec((1, L // LANES, KV_WIDTH, LANES), lambda b, j: (b, 0, 0, 0)),
                pl.BlockSpec((1, L, IDX_DIM), lambda b, j: (b, 0, 0)),
                tile(D), tile(POOL_WIDTH), pl.BlockSpec((POOL_WIDTH + ATT_WIDTH, D), lambda b, j: (0, 0))]
    return pl.pallas_call(
        functools.partial(_dsa_kernel, tk=tk, pos0=pos0, n_keys=n_keys, n_sel=n_sel),
        out_shape=jax.ShapeDtypeStruct((B, T, D), F32), grid=(B, T // tq),
        in_specs=in_specs, out_specs=tile(D),
        scratch_shapes=[pltpu.VMEM((nkb_max, tk, tq), I32), pltpu.VMEM((nkb_max, tk, tq), F32),
                        pltpu.VMEM((N_KV_HEADS, 1, Q_PER_KV * tq), F32),
                        pltpu.VMEM((N_KV_HEADS, 1, Q_PER_KV * tq), F32),
                        pltpu.VMEM((N_KV_HEADS, HEAD_DIM, Q_PER_KV * tq), F32),
                        pltpu.VMEM((tk // LANES, N_KV_HEADS, LANES, Q_PER_KV * tq), F32),
                        pltpu.VMEM((nkb_max, tk, tq), I16), pltpu.VMEM((nkb_max, tk, tq), I16)],
        compiler_params=pltpu.CompilerParams(dimension_semantics=("parallel", "arbitrary"),
                                             vmem_limit_bytes=VMEM_LIMIT),
        name="dsa",
    )(q_hm, qi_hm, kiwi, k_hm, v_t, ki_all, x, yp, wo_bf)


def _gmlp_kernel(x_ref, g_ref, win_ref, lng_ref, lnb_ref, ws_ref, bias_ref, wout_ref, o_ref, v_ref, *, tm):
    x = x_ref[...]
    h = _rms(x, g_ref[...])
    z = jax.nn.gelu(jnp.dot(h.astype(BF16), win_ref[...], preferred_element_type=F32))
    half = z.shape[1] // 2
    u, v = z[:, :half], z[:, half:]
    mu = jnp.mean(v, axis=-1, keepdims=True)
    var = jnp.mean(jnp.square(v - mu), axis=-1, keepdims=True)
    vn = (v - mu) * lax.rsqrt(var + LN_EPS) * lng_ref[...] + lnb_ref[...]
    v_ref[...] = vn
    gd = half // GM_GROUPS
    gated = []
    for c in range(tm // GM_CHUNK):
        rows = slice(c * GM_CHUNK, (c + 1) * GM_CHUNK)
        vc = vn[rows].astype(BF16)
        mixed = jnp.concatenate(
            [jnp.dot(ws_ref[0, g], vc[:, g * gd:(g + 1) * gd], preferred_element_type=F32)
             for g in range(GM_GROUPS)], axis=1) + bias_ref[0]
        gated.append((u[rows] * mixed).astype(BF16))
    gated = jnp.concatenate(gated, axis=0)
    o_ref[...] = x + jnp.dot(gated, wout_ref[...], preferred_element_type=F32)


def _gmlp(x, g, win_bf, lng, lnb, ws2, bias2, wout_bf, *, tm, n_first, n_v_rows):
    N, D = x.shape
    half = win_bf.shape[1] // 2
    nt = N // tm
    t_first = n_first // tm
    variant = lambda i: jnp.where(i >= t_first, 1, 0)
    row = pl.BlockSpec((tm, D), lambda i: (i, 0))
    const = lambda s: pl.BlockSpec(s, lambda i: (0, 0))
    in_specs = [row, const((1, D)), const((D, 2 * half)), const((1, half)), const((1, half)),
                pl.BlockSpec((1, GM_GROUPS, GM_CHUNK, GM_CHUNK), lambda i: (variant(i), 0, 0, 0)),
                pl.BlockSpec((1, GM_CHUNK, half), lambda i: (variant(i), 0, 0)),
                const((half, D))]
    out_specs = (row, pl.BlockSpec((tm, half), lambda i: (jnp.maximum(i - t_first, 0), 0)))
    return pl.pallas_call(
        functools.partial(_gmlp_kernel, tm=tm),
        out_shape=(jax.ShapeDtypeStruct((N, D), F32), jax.ShapeDtypeStruct((n_v_rows, half), F32)),
        grid=(nt,), in_specs=in_specs, out_specs=out_specs,
        compiler_params=pltpu.CompilerParams(dimension_semantics=("arbitrary",), vmem_limit_bytes=VMEM_LIMIT),
        name="gmlp",
    )(x, g, win_bf, lng, lnb, ws2, bias2, wout_bf)


ROUTE_E0, ROUTE_E1, ROUTE_R0, ROUTE_R1, ROUTE_G0, ROUTE_G1 = range(6)


def _router_kernel(x_ref, g_ref, wr_ref, route_ref, cnt_ref, carry_ref, *, tm):
    i = pl.program_id(0)

    @pl.when(i == 0)
    def _():
        carry_ref[...] = jnp.zeros(carry_ref.shape, F32)

    h = _rms(x_ref[...], g_ref[...])
    logits = jnp.dot(h.astype(BF16), wr_ref[...], preferred_element_type=F32)
    lane = lax.broadcasted_iota(I32, (tm, LANES), 1).astype(F32)
    ninf = -jnp.inf
    big = 1e9
    rmax = lambda a: jnp.max(a, axis=1, keepdims=True)
    rmin = lambda a: jnp.min(a, axis=1, keepdims=True)
    rsum = lambda a: jnp.sum(a, axis=1, keepdims=True)

    is_grp = lane < N_EXPERT_GROUPS
    lg = jnp.where(is_grp, logits, ninf)
    mg = rmax(lg)
    g_sel = rmin(jnp.where(lg == mg, lane, big))
    p_grp = 1.0 / rsum(jnp.where(is_grp, jnp.exp(lg - mg), 0.0))
    lo = N_EXPERT_GROUPS + g_sel * EXPERTS_PER_GROUP
    le = jnp.where((lane >= lo) & (lane < lo + EXPERTS_PER_GROUP), logits, ninf)
    v1 = rmax(le)
    j1 = rmin(jnp.where(le == v1, lane, big))
    le2 = jnp.where(lane == j1, ninf, le)
    v2 = rmax(le2)
    j2 = rmin(jnp.where(le2 == v2, lane, big))
    e0 = j1 - N_EXPERT_GROUPS
    e1 = j2 - N_EXPERT_GROUPS
    r = jnp.exp(v2 - v1)
    g0 = p_grp / (1.0 + r)
    g1 = p_grp * r / (1.0 + r)

    oh0 = jnp.where(lane == e0, 1.0, 0.0)
    oh1 = jnp.where(lane == e1, 1.0, 0.0)
    oh = oh0 + oh1
    lower = jnp.where(lax.broadcasted_iota(I32, (tm, tm), 1) < lax.broadcasted_iota(I32, (tm, tm), 0),
                      1.0, 0.0).astype(BF16)
    before = jnp.dot(lower, oh.astype(BF16), preferred_element_type=F32) + carry_ref[...]
    r0 = rsum(oh0 * before)
    r1 = rsum(oh1 * before)
    carry_ref[...] = carry_ref[...] + jnp.sum(oh, axis=0, keepdims=True)
    cnt_ref[...] = carry_ref[...]

    route = jnp.zeros((tm, LANES), F32)
    for col, val in ((ROUTE_E0, e0), (ROUTE_E1, e1), (ROUTE_R0, r0), (ROUTE_R1, r1), (ROUTE_G0, g0), (ROUTE_G1, g1)):
        route = jnp.where(lane == col, val, route)
    route_ref[...] = route


def _router(x, g, wr, *, tm):
    N, D = x.shape
    row = lambda wdt: pl.BlockSpec((tm, wdt), lambda i: (i, 0))
    const = lambda s: pl.BlockSpec(s, lambda i: (0, 0))
    return pl.pallas_call(
        functools.partial(_router_kernel, tm=tm),
        out_shape=(jax.ShapeDtypeStruct((N, LANES), F32), jax.ShapeDtypeStruct((1, LANES), F32)),
        grid=(N // tm,), in_specs=[row(D), const((1, D)), const((D, LANES))],
        out_specs=(row(LANES), const((1, LANES))),
        scratch_shapes=[pltpu.VMEM((1, LANES), F32)],
        compiler_params=pltpu.CompilerParams(dimension_semantics=("arbitrary",), vmem_limit_bytes=VMEM_LIMIT),
        name="router",
    )(x, g, wr)


SUBLANES = 8


def _dispatch_kernel(d0_ref, d1_ref, pad_ref, x_ref, g_ref, rows_ref, h_scr, zblk, sems, *, tm):
    i = pl.program_id(0)
    n = pl.num_programs(0)
    slot = i % 2
    groups = tm // SUBLANES

    def wait_slot(s):
        for _ in range(2 * groups):
            pltpu.make_async_copy(h_scr.at[s, 0], rows_ref.at[pl.ds(0, SUBLANES)], sems.at[s]).wait()

    @pl.when(i >= 2)
    def _():
        wait_slot(slot)

    h_scr[slot] = _rms(x_ref[...], g_ref[...]).reshape(groups, SUBLANES, x_ref.shape[1])

    def body(k, carry):
        for u in range(SUBLANES):
            r = k * SUBLANES + u
            src = h_scr.at[slot, k, pl.ds(u, 1)]
            pltpu.make_async_copy(src, rows_ref.at[pl.ds(d0_ref[r], 1)], sems.at[slot]).start()
            pltpu.make_async_copy(src, rows_ref.at[pl.ds(d1_ref[r], 1)], sems.at[slot]).start()
        return carry

    lax.fori_loop(0, groups, body, 0)

    @pl.when(i == n - 1)
    def _():
        wait_slot(slot)

    @pl.when(jnp.logical_and(i == n - 1, n >= 2))
    def _():
        wait_slot(1 - slot)

    @pl.when(i == n - 1)
    def _():
        zblk[...] = jnp.zeros(zblk.shape, F32)
        zrow = zblk.at[pl.ds(0, 1)]
        for e in range(N_EXPERTS):
            first, count = pad_ref[0, e], pad_ref[1, e]

            def start(k, carry):
                pltpu.make_async_copy(zrow, rows_ref.at[pl.ds(first + k, 1)], sems.at[2]).start()
                return carry

            def wait(k, carry):
                pltpu.make_async_copy(zrow, rows_ref.at[pl.ds(first, 1)], sems.at[2]).wait()
                return carry

            lax.fori_loop(0, count, start, 0)
            lax.fori_loop(0, count, wait, 0)

        def blk_copy(b):
            return pltpu.make_async_copy(zblk, rows_ref.at[pl.ds(b * MOE_BLOCK, MOE_BLOCK)], sems.at[2])

        n_blocks = rows_ref.shape[0] // MOE_BLOCK
        n_used = pad_ref[2, 0]
        lax.fori_loop(n_used, n_blocks, lambda b, c: (blk_copy(b).start(), c)[1], 0)
        lax.fori_loop(n_used, n_blocks, lambda b, c: (blk_copy(n_used).wait(), c)[1], 0)


def _dispatch(dest_flat, pad_info, x, g, *, tm, n_rows):
    N, D = x.shape
    nt = N // tm
    return pl.pallas_call(
        functools.partial(_dispatch_kernel, tm=tm),
        out_shape=jax.ShapeDtypeStruct((n_rows, D), F32), grid=(nt,),
        in_specs=[pl.BlockSpec((tm,), lambda i: (i,), memory_space=pltpu.SMEM),
                  pl.BlockSpec((tm,), lambda i: (i + nt,), memory_space=pltpu.SMEM),
                  pl.BlockSpec(memory_space=pltpu.SMEM),
                  pl.BlockSpec((tm, D), lambda i: (i, 0)),
                  pl.BlockSpec((1, D), lambda i: (0, 0))],
        out_specs=pl.BlockSpec(memory_space=pl.ANY),
        scratch_shapes=[pltpu.VMEM((2, tm // SUBLANES, SUBLANES, D), F32), pltpu.VMEM((MOE_BLOCK, D), F32),
                        pltpu.SemaphoreType.DMA((3,))],
        compiler_params=pltpu.CompilerParams(dimension_semantics=("arbitrary",), vmem_limit_bytes=VMEM_LIMIT),
        name="dispatch",
    )(dest_flat, dest_flat, pad_info, x, g)


def _expert_kernel(be_ref, nu_ref, x_ref, wg_ref, wu_ref, wd_ref, y_ref, wg_bf, wu_bf, wd_bf):
    i = pl.program_id(0)

    @pl.when(jnp.logical_or(i == 0, be_ref[i] != be_ref[jnp.maximum(i - 1, 0)]))
    def _():
        wg_bf[...] = wg_ref[0].astype(BF16)
        wu_bf[...] = wu_ref[0].astype(BF16)
        wd_bf[...] = wd_ref[0].astype(BF16)

    @pl.when(i < nu_ref[0])
    def _():
        x = x_ref[...].astype(BF16)
        a = jnp.dot(x, wg_bf[...], preferred_element_type=F32)
        u = jnp.dot(x, wu_bf[...], preferred_element_type=F32)
        act = (a * jax.nn.sigmoid(a)) * u
        y_ref[...] = jnp.dot(act.astype(BF16), wd_bf[...], preferred_element_type=F32)

    @pl.when(i >= nu_ref[0])
    def _():
        y_ref[...] = jnp.zeros(y_ref.shape, F32)


def _experts(block_e, n_used, x_rows, w_gate, w_up, w_down):
    n_rows, D = x_rows.shape
    de = w_gate.shape[2]
    n_blocks = n_rows // MOE_BLOCK
    grid_spec = pltpu.PrefetchScalarGridSpec(
        num_scalar_prefetch=2, grid=(n_blocks,),
        in_specs=[pl.BlockSpec((MOE_BLOCK, D), lambda i, be, nu: (jnp.minimum(i, nu[0] - 1), 0)),
                  pl.BlockSpec((1, D, de), lambda i, be, nu: (be[i], 0, 0)),
                  pl.BlockSpec((1, D, de), lambda i, be, nu: (be[i], 0, 0)),
                  pl.BlockSpec((1, de, D), lambda i, be, nu: (be[i], 0, 0))],
        out_specs=pl.BlockSpec((MOE_BLOCK, D), lambda i, be, nu: (i, 0)),
        scratch_shapes=[pltpu.VMEM((D, de), BF16), pltpu.VMEM((D, de), BF16), pltpu.VMEM((de, D), BF16)])
    return pl.pallas_call(
        _expert_kernel, out_shape=jax.ShapeDtypeStruct((n_rows, D), F32), grid_spec=grid_spec,
        compiler_params=pltpu.CompilerParams(dimension_semantics=("arbitrary",), vmem_limit_bytes=VMEM_LIMIT),
        name="experts",
    )(block_e, n_used, x_rows, w_gate, w_up, w_down)


def _combine_kernel(dc0_ref, dc1_ref, dn0_ref, dn1_ref, x_ref, route_ref, g_ref, y_ref, o_ref, ybuf, sems,
                    *, tm, final_norm):
    i = pl.program_id(0)
    n = pl.num_programs(0)
    slot = i % 2
    groups = tm // SUBLANES
    D = x_ref.shape[1]

    def gather(d_refs, s):
        def body(k, carry):
            for u in range(SUBLANES):
                r = k * SUBLANES + u
                for a in range(2):
                    pltpu.make_async_copy(y_ref.at[pl.ds(d_refs[a][r], 1)], ybuf.at[s, a, k, pl.ds(u, 1)],
                                          sems.at[s]).start()
            return carry
        lax.fori_loop(0, groups, body, 0)

    @pl.when(i == 0)
    def _():
        gather((dc0_ref, dc1_ref), 0)

    @pl.when(i + 1 < n)
    def _():
        gather((dn0_ref, dn1_ref), 1 - slot)

    for _ in range(2 * groups):
        pltpu.make_async_copy(y_ref.at[pl.ds(0, SUBLANES)], ybuf.at[slot, 0, 0], sems.at[slot]).wait()
    route = route_ref[...]
    g0 = route[:, ROUTE_G0:ROUTE_G0 + 1]
    g1 = route[:, ROUTE_G1:ROUTE_G1 + 1]
    out = x_ref[...] + (ybuf[slot, 0].reshape(tm, D) * g0 + ybuf[slot, 1].reshape(tm, D) * g1)
    if final_norm:
        out = _rms(out, g_ref[...])
    o_ref[...] = out


def _combine(dest_flat, x, route, g, y_rows, *, tm, final_norm):
    N, D = x.shape
    nt = N // tm
    row = lambda wdt: pl.BlockSpec((tm, wdt), lambda i: (i, 0))
    nxt = lambda i: jnp.minimum(i + 1, nt - 1)
    smem = lambda f: pl.BlockSpec((tm,), f, memory_space=pltpu.SMEM)
    return pl.pallas_call(
        functools.partial(_combine_kernel, tm=tm, final_norm=final_norm),
        out_shape=jax.ShapeDtypeStruct((N, D), F32), grid=(nt,),
        in_specs=[smem(lambda i: (i,)), smem(lambda i: (i + nt,)),
                  smem(lambda i: (nxt(i),)), smem(lambda i: (nxt(i) + nt,)),
                  row(D), row(LANES), pl.BlockSpec((1, D), lambda i: (0, 0)),
                  pl.BlockSpec(memory_space=pl.ANY)],
        out_specs=row(D),
        scratch_shapes=[pltpu.VMEM((2, 2, tm // SUBLANES, SUBLANES, D), F32), pltpu.SemaphoreType.DMA((2,))],
        compiler_params=pltpu.CompilerParams(dimension_semantics=("arbitrary",), vmem_limit_bytes=VMEM_LIMIT),
        name="combine",
    )(dest_flat, dest_flat, dest_flat, dest_flat, x, route, g, y_rows)


def _hier_moe(x, g_ffn, w_rg, w_re, w_gate, w_up, w_down, g_final, *, final_norm):
    N, D = x.shape
    tm = 256
    wr = jnp.concatenate([w_rg, w_re, jnp.zeros((D, LANES - N_EXPERT_GROUPS - N_EXPERTS), F32)], axis=1).astype(BF16)
    g_ffn = g_ffn.reshape(1, D)
    route, counts = _router(x, g_ffn, wr, tm=tm)
    counts = counts[0, :N_EXPERTS].astype(I32)
    eid_t = jnp.transpose(route[:, ROUTE_E0:ROUTE_E1 + 1]).astype(I32)
    rank_t = jnp.transpose(route[:, ROUTE_R0:ROUTE_R1 + 1]).astype(I32)
    padded = (counts + MOE_BLOCK - 1) // MOE_BLOCK * MOE_BLOCK
    pad_end = jnp.cumsum(padded)
    pad_start = pad_end - padded
    expert_ids = jnp.arange(N_EXPERTS, dtype=I32)
    start_of = jnp.sum(jnp.where(eid_t[:, :, None] == expert_ids, pad_start, 0), axis=-1)
    dest_t = start_of + rank_t
    n_blocks = -(-(2 * N) // MOE_BLOCK) + N_EXPERTS
    n_rows = n_blocks * MOE_BLOCK
    n_used = (pad_end[-1] // MOE_BLOCK).astype(I32)
    block_start = jnp.minimum(jnp.arange(n_blocks, dtype=I32), n_used - 1) * MOE_BLOCK
    block_e = jnp.minimum(jnp.sum(block_start[:, None] >= pad_end[None, :], axis=1), N_EXPERTS - 1).astype(I32)
    pad_info = jnp.stack([pad_start + counts, padded - counts, jnp.broadcast_to(n_used, (N_EXPERTS,))])
    dest_flat = dest_t.reshape(-1)
    x_rows = _dispatch(dest_flat, pad_info, x, g_ffn, tm=tm, n_rows=n_rows)
    y_rows = _experts(block_e, n_used.reshape(1), x_rows, w_gate, w_up, w_down)
    return _combine(dest_flat, x, route, g_final.reshape(1, D), y_rows, tm=tm, final_norm=final_norm)


def _rope_tables(pos, rot_lanes):
    inv = 1.0 / (ROPE_THETA ** (jnp.arange(ROT_HALF, dtype=F32) / ROT_HALF))
    ang = pos.astype(F32)[:, None] * inv[None, :]
    cos, sin = jnp.cos(ang), jnp.sin(ang)
    lane = jnp.arange(LANES)
    r = lane % HEAD_DIM
    active = (lane < rot_lanes)
    first = active & (r < ROT_HALF)
    second = active & (r >= ROT_HALF) & (r < 2 * ROT_HALF)
    cos_l = cos[:, r % ROT_HALF]
    sin_l = sin[:, r % ROT_HALF]
    c = jnp.where((first | second)[None, :], cos_l, 1.0)
    sa = jnp.where(first[None, :], -sin_l, 0.0)
    sb = jnp.where(second[None, :], sin_l, 0.0)
    return jnp.stack([c, sa, sb]).astype(F32)


def _prep_w_in(w_in):
    D = w_in.shape[0]
    xp, q, k, v, qi, ki, wi = jnp.split(w_in, [512, 1024, 1280, 1536, 2048, 2112], axis=1)
    pad = jnp.zeros((D, IN_WIDTH_PAD - w_in.shape[1]), w_in.dtype)
    return jnp.concatenate([xp, q, k, v, qi, ki, wi, pad], axis=1).astype(BF16)


def kernel(x_prompt, x_sample, cache_k, cache_v, cache_idx_k, state_pool, norm_mix, norm_ffn, norm_final,
           par_w_in, par_pool_w, par_pool_scale, par_w_out, gm_w_in, gm_ln_g, gm_ln_b, gm_ws, gm_bs, gm_w_out,
           moe_router_group, moe_router_expert, moe_w_gate, moe_w_up, moe_w_down):
    Bp, Tp, D = x_prompt.shape
    Bs, Ts, _ = x_sample.shape
    past = cache_k.shape[2]
    Np, Ns = Bp * Tp, Bs * Ts
    depth = norm_mix.shape[0]
    assert depth == 2 and Ts == CHUNK and Tp % 256 == 0 and Ns % 256 == 0

    w_in_bf = _prep_w_in(par_w_in[0])
    pw_bf = par_pool_w[0].astype(BF16)
    ps = par_pool_scale[0].reshape(1, POOL_WIDTH)
    wo_bf = par_w_out[0].astype(BF16)
    g_mix0 = norm_mix[0].reshape(1, D)
    pos_p = jnp.arange(Tp, dtype=I32)
    pos_s = past + jnp.arange(Ts, dtype=I32)

    hist_p = jnp.zeros((Bp, HIST_ROWS, POOL_WIDTH), F32)
    hist_s = jnp.pad(state_pool[0], ((0, 0), (1, 0), (0, 0)))
    tk = 512
    (q_p, qi_p, k_p, v_p, kiwi_p, kbf_p, vbf_p, kibf_p, yp_p, st_p) = _inproj(
        x_prompt, g_mix0, w_in_bf, _rope_tables(pos_p, LANES), _rope_tables(pos_p, IDX_DIM), hist_p, pw_bf, ps,
        tm=256, pos0=0)
    (q_s, qi_s, k_s, v_s, kiwi_s, kbf_s, vbf_s, kibf_s, yp_s, st_s) = _inproj(
        x_sample, g_mix0, w_in_bf, _rope_tables(pos_s, LANES), _rope_tables(pos_s, IDX_DIM), hist_s, pw_bf, ps,
        tm=Ts, pos0=past)

    assert Tp % tk == 0
    x1_p = _dsa(q_p, qi_p, kiwi_p, kbf_p, vbf_p, kibf_p, x_prompt, yp_p, wo_bf,
                tk=tk, pos0=0, n_keys=Tp, n_sel=min(TOPK_MAX, Tp // 4))
    ls = past + Ts
    lsp = -(-ls // tk) * tk
    kpad = lambda a, ax: jnp.pad(a, [(0, lsp - ls) if d == ax else (0, 0) for d in range(a.ndim)])
    kall = kpad(jnp.concatenate([jnp.transpose(cache_k[0], (0, 2, 1, 3)).astype(BF16), kbf_s], axis=2), 2)
    vall = kpad(jnp.concatenate([cache_v[0].reshape(Bs, past, KV_WIDTH).astype(BF16), vbf_s], axis=1), 1)
    vall_t = jnp.transpose(vall.reshape(Bs, lsp // LANES, LANES, KV_WIDTH), (0, 1, 3, 2))
    kiall = kpad(jnp.concatenate([cache_idx_k[0].astype(BF16), kibf_s], axis=1), 1)
    qpad = lambda a, ax: jnp.pad(a, [(0, LANES - Ts) if d == ax else (0, 0) for d in range(a.ndim)])
    x1_s = _dsa(qpad(q_s, 2), qpad(qi_s, 2), qpad(kiwi_s, 1), kall, vall_t, kiall, qpad(x_sample, 1), qpad(yp_s, 1),
                wo_bf, tk=tk, pos0=past, n_keys=ls, n_sel=min(TOPK_MAX, ls // 4))[:, :Ts]

    x = jnp.concatenate([x1_p.reshape(Np, D), x1_s.reshape(Ns, D)], axis=0)
    x = _hier_moe(x, norm_ffn[0], moe_router_group[0], moe_router_expert[0],
                  moe_w_gate[0], moe_w_up[0], moe_w_down[0], norm_final, final_norm=False)

    cs = Ts
    tril = lambda n: jnp.tril(jnp.ones((n, n), bool))
    ws_p = jnp.where(tril(GM_CHUNK)[None], gm_ws[0], 0.0)
    ws_small = jnp.where(tril(cs)[None], gm_ws[0][:, :cs, :cs], 0.0)
    rep = GM_CHUNK // cs
    ws_s = jnp.einsum('ab,gts->gatbs', jnp.eye(rep, dtype=F32), ws_small).reshape(GM_GROUPS, GM_CHUNK, GM_CHUNK)
    ws2 = jnp.stack([ws_p, ws_s]).astype(BF16)
    gd = D // GM_GROUPS
    bias_p = jnp.repeat(jnp.transpose(gm_bs[0]), gd, axis=1)
    bias_s = jnp.tile(jnp.repeat(jnp.transpose(gm_bs[0][:, :cs]), gd, axis=1), (rep, 1))
    bias2 = jnp.stack([bias_p, bias_s])
    x, gm_v = _gmlp(x, norm_mix[1].reshape(1, D), gm_w_in[0].astype(BF16), gm_ln_g[0].reshape(1, D),
                    gm_ln_b[0].reshape(1, D), ws2, bias2, gm_w_out[0].astype(BF16),
                    tm=256, n_first=Np, n_v_rows=Ns)
    x = _hier_moe(x, norm_ffn[1], moe_router_group[1], moe_router_expert[1],
                  moe_w_gate[1], moe_w_up[1], moe_w_down[1], norm_final, final_norm=True)

    y_p = x[:Np].reshape(Bp, Tp, D)
    y_s = x[Np:].reshape(Bs, Ts, D)
    r4 = lambda a, b, t: a.reshape(1, b, t, N_KV_HEADS, HEAD_DIM)
    return (y_p, y_s,
            r4(k_p, Bp, Tp), r4(v_p, Bp, Tp), kiwi_p[:, :, :IDX_DIM][None], st_p[:, 1:][None],
            r4(k_s, Bs, Ts), r4(v_s, Bs, Ts), kiwi_s[:, :, :IDX_DIM][None], st_s[:, 1:][None],
            gm_v.reshape(1, Bs, Ts, D))
```

```python
import functools

import jax
import jax.numpy as jnp
from jax import lax
from jax.experimental import pallas as pl
from jax.experimental.pallas import tpu as pltpu

F32 = jnp.float32
BF16 = jnp.bfloat16
I32 = jnp.int32

LANES = 128
CHUNK = 64
POOL_WINDOWS = (2, 4, 8, 16)
POOL_GROUP_DIM = 128
POOL_WIDTH = 512
HIST_ROWS = 16
N_HEADS = 8
HEAD_DIM = 64
N_KV_HEADS = 4
Q_PER_KV = N_HEADS // N_KV_HEADS
ATT_WIDTH = N_HEADS * HEAD_DIM
KV_WIDTH = N_KV_HEADS * HEAD_DIM
N_IDX_HEADS = 8
IDX_DIM = 64
TOPK_MAX = 256
ROPE_THETA = 500000.0
ROT_HALF = HEAD_DIM // 8
GM_CHUNK = 128
GM_GROUPS = 8
N_EXPERT_GROUPS = 4
EXPERTS_PER_GROUP = 8
N_EXPERTS = 32
MOE_BLOCK = 256
RMS_EPS = 1e-6
LN_EPS = 1e-5

INT_MIN = -2147483648
LOG2_E = 1.4426950408889634
NEG_BIG = -1e30
VMEM_LIMIT = 48 * 1024 * 1024

COL_XP, COL_Q, COL_K, COL_V, COL_QI, COL_KIWI = 0, 512, 1024, 1280, 1536, 2048
IN_WIDTH_PAD = 2176


def _rms(x, g):
    return x * lax.rsqrt(jnp.mean(x * x, axis=-1, keepdims=True) + RMS_EPS) * g


def _rope128(x, c, sa, sb):
    return x * c + pltpu.roll(x, LANES - ROT_HALF, 1) * sa + pltpu.roll(x, ROT_HALF, 1) * sb


def _inproj_kernel(x_ref, g_ref, w_ref, rope_ref, ropeki_ref, hist_ref, pw_ref, ps_ref,
                   q_ref, qi_ref, k_ref, v_ref, kiwi_ref, kbf_ref, vbf_ref, kibf_ref, yp_ref, state_ref,
                   buf_ref, *, tm, pos0):
    j = pl.program_id(1)
    h = _rms(x_ref[0], g_ref[...])
    proj = jnp.dot(h.astype(BF16), w_ref[...], preferred_element_type=F32)
    c, sa, sb = rope_ref[0], rope_ref[1], rope_ref[2]

    def put_heads(ref, i, chunk):
        ref[0, 2 * i] = chunk[:, :HEAD_DIM].astype(BF16)
        ref[0, 2 * i + 1] = chunk[:, HEAD_DIM:].astype(BF16)

    for i in range(ATT_WIDTH // LANES):
        put_heads(q_ref, i, _rope128(proj[:, COL_Q + i * LANES:COL_Q + (i + 1) * LANES], c, sa, sb)
                  * (HEAD_DIM ** -0.5 * LOG2_E))
        put_heads(qi_ref, i, _rope128(proj[:, COL_QI + i * LANES:COL_QI + (i + 1) * LANES], c, sa, sb))
    for i in range(KV_WIDTH // LANES):
        sl = slice(i * LANES, (i + 1) * LANES)
        kr = _rope128(proj[:, COL_K + i * LANES:COL_K + (i + 1) * LANES], c, sa, sb)
        k_ref[0, :, sl] = kr
        put_heads(kbf_ref, i, kr)
    vv = proj[:, COL_V:COL_V + KV_WIDTH]
    v_ref[0] = vv
    if tm % LANES == 0:
        for cc in range(tm // LANES):
            vbf_ref[0, cc] = jnp.transpose(vv[cc * LANES:(cc + 1) * LANES, :]).astype(BF16)
    else:
        vbf_ref[0] = vv.astype(BF16)
    kiwi = _rope128(proj[:, COL_KIWI:COL_KIWI + LANES], ropeki_ref[0], ropeki_ref[1], ropeki_ref[2])
    kiwi_ref[0] = kiwi
    kibf_ref[0] = kiwi[:, :IDX_DIM].astype(BF16)

    @pl.when(j == 0)
    def _():
        buf_ref[0:HIST_ROWS, :] = hist_ref[0]

    xp = proj[:, COL_XP:COL_XP + POOL_WIDTH]
    buf_ref[HIST_ROWS:HIST_ROWS + tm, :] = xp
    pos = pos0 + j * tm + lax.broadcasted_iota(I32, (tm, 1), 0)
    for gi, w in enumerate(POOL_WINDOWS):
        c0 = gi * POOL_GROUP_DIM
        s = xp[:, c0:c0 + POOL_GROUP_DIM]
        for i in range(1, w):
            s = s + buf_ref[HIST_ROWS - i:HIST_ROWS - i + tm, c0:c0 + POOL_GROUP_DIM]
        cnt = jnp.minimum(pos + 1, w).astype(F32)
        d = s / cnt - xp[:, c0:c0 + POOL_GROUP_DIM]
        y = jnp.dot(d.astype(BF16), pw_ref[gi], preferred_element_type=F32)
        yp_ref[0, :, c0:c0 + POOL_GROUP_DIM] = (y * ps_ref[:, c0:c0 + POOL_GROUP_DIM]).astype(BF16)
    tail = buf_ref[tm:tm + HIST_ROWS, :]
    state_ref[0] = tail
    buf_ref[0:HIST_ROWS, :] = tail


def _inproj(x, g, w_bf, rope, ropeki, hist, pw_bf, ps, *, tm, pos0):
    B, T, D = x.shape
    nt = T // tm
    f = lambda shape, dt: jax.ShapeDtypeStruct(shape, dt)
    v_t = tm % LANES == 0
    out_shape = (
        f((B, N_HEADS, T, HEAD_DIM), BF16), f((B, N_IDX_HEADS, T, IDX_DIM), BF16),
        f((B, T, KV_WIDTH), F32), f((B, T, KV_WIDTH), F32), f((B, T, LANES), F32),
        f((B, N_KV_HEADS, T, HEAD_DIM), BF16),
        f((B, T // LANES, KV_WIDTH, LANES) if v_t else (B, T, KV_WIDTH), BF16),
        f((B, T, IDX_DIM), BF16),
        f((B, T, POOL_WIDTH), BF16), f((B, HIST_ROWS, POOL_WIDTH), F32),
    )
    tile = lambda wdt: pl.BlockSpec((1, tm, wdt), lambda b, j: (b, j, 0))
    heads = lambda n, wdt: pl.BlockSpec((1, n, tm, wdt), lambda b, j: (b, 0, j, 0))
    const2 = lambda s: pl.BlockSpec(s, lambda b, j: (0, 0))
    in_specs = [
        tile(D), const2((1, D)), const2((D, IN_WIDTH_PAD)),
        pl.BlockSpec((3, tm, LANES), lambda b, j: (0, j, 0)),
        pl.BlockSpec((3, tm, LANES), lambda b, j: (0, j, 0)),
        pl.BlockSpec((1, HIST_ROWS, POOL_WIDTH), lambda b, j: (b, 0, 0)),
        pl.BlockSpec((len(POOL_WINDOWS), POOL_GROUP_DIM, POOL_GROUP_DIM), lambda b, j: (0, 0, 0)),
        const2((1, POOL_WIDTH)),
    ]
    out_specs = (
        heads(N_HEADS, HEAD_DIM), heads(N_IDX_HEADS, IDX_DIM), tile(KV_WIDTH), tile(KV_WIDTH), tile(LANES),
        heads(N_KV_HEADS, HEAD_DIM),
        pl.BlockSpec((1, tm // LANES, KV_WIDTH, LANES), lambda b, j: (b, j, 0, 0)) if v_t else tile(KV_WIDTH),
        tile(IDX_DIM), tile(POOL_WIDTH),
        pl.BlockSpec((1, HIST_ROWS, POOL_WIDTH), lambda b, j: (b, 0, 0)),
    )
    return pl.pallas_call(
        functools.partial(_inproj_kernel, tm=tm, pos0=pos0),
        out_shape=out_shape, grid=(B, nt), in_specs=in_specs, out_specs=out_specs,
        scratch_shapes=[pltpu.VMEM((HIST_ROWS + tm, POOL_WIDTH), F32)],
        compiler_params=pltpu.CompilerParams(dimension_semantics=("parallel", "arbitrary"),
                                             vmem_limit_bytes=VMEM_LIMIT),
        name="inproj",
    )(x, g, w_bf, rope, ropeki, hist, pw_bf, ps)


def _dsa_kernel(q_ref, qi_ref, kiwi_ref, k_ref, vt_ref, ki_ref, x_ref, yp_ref, wo_ref,
                o_ref, key_buf, bias_buf, m_scr, l_scr, acc_scr, s_scr,
                *, tk, pos0, n_keys, n_sel):
    tq = LANES
    slab = 64
    j = pl.program_id(1)
    base = pos0 + j * tq
    pos = base + lax.broadcasted_iota(I32, (1, tq), 1)
    limit = jnp.minimum((pos // CHUNK + 1) * CHUNK, n_keys)
    limit_max = jnp.minimum(((base + tq - 1) // CHUNK + 1) * CHUNK, n_keys)
    nkb = (limit_max + tk - 1) // tk
    nt = (((1,), (1,)), ((), ()))

    wi_t = jnp.transpose(kiwi_ref[0])[IDX_DIM:IDX_DIM + N_IDX_HEADS, :]

    def score_block(kb, carry):
        off = pl.multiple_of(kb * tk, tk)
        kiblk = ki_ref[0, pl.ds(off, tk), :]
        idx = jnp.zeros((tk, tq), F32)
        for h in range(N_IDX_HEADS):
            sc = lax.dot_general(kiblk, qi_ref[0, h], nt, preferred_element_type=F32)
            idx = idx + jnp.maximum(sc, 0.0) * wi_t[h:h + 1, :]
        idx = jnp.where(idx == 0.0, 0.0, idx)
        bits = lax.bitcast_convert_type(idx, I32)
        key = bits ^ ((bits >> 31) & 0x7FFFFFFF)
        key_buf[kb] = jnp.where(lax.broadcasted_iota(I32, (tk, tq), 0) < limit - kb * tk, key, INT_MIN)
        return carry

    lax.fori_loop(0, nkb, score_block, 0)

    def col_sum(a):
        return jnp.sum(a, axis=0, keepdims=True)

    def count_ge(cand):
        def body(kb, acc):
            kblk = key_buf[kb]
            for c in range(tk // slab):
                acc = acc + jnp.where(kblk[c * slab:(c + 1) * slab] >= cand, 1.0, 0.0)
            return acc
        return col_sum(lax.fori_loop(0, nkb, body, jnp.zeros((slab, tq), F32)))

    kf = float(n_sel)
    t0 = jnp.where(count_ge(jnp.zeros((1, tq), I32)) >= kf, 0, INT_MIN).astype(I32)

    def bit_body(i, t):
        cand = t | lax.shift_left(jnp.int32(1), 30 - i)
        return jnp.where(count_ge(cand) >= kf, cand, t)

    t = lax.fori_loop(0, 31, bit_body, t0)

    def count_gt_ge(t):
        def body(kb, accs):
            a_gt, a_ge = accs
            kblk = key_buf[kb]
            for c in range(tk // slab):
                blk = kblk[c * slab:(c + 1) * slab]
                a_gt = a_gt + jnp.where(blk > t, 1.0, 0.0)
                a_ge = a_ge + jnp.where(blk >= t, 1.0, 0.0)
            return a_gt, a_ge
        z = jnp.zeros((slab, tq), F32)
        a_gt, a_ge = lax.fori_loop(0, nkb, body, (z, z))
        return col_sum(a_gt), col_sum(a_ge)

    cnt_gt, cnt_ge = count_gt_ge(t)
    need = kf - cnt_gt
    cnt_eq = cnt_ge - cnt_gt
    overfull = jnp.where(t != INT_MIN, cnt_eq - need, 0.0)
    slow = jnp.max(overfull) > 0.0

    @pl.when(jnp.logical_not(slow))
    def _():
        t_adm = jnp.maximum(t, INT_MIN + 1)

        def body(kb, carry):
            bias_buf[kb] = jnp.where(key_buf[kb] >= t_adm, 0.0, NEG_BIG)
            return carry
        lax.fori_loop(0, nkb, body, 0)

    @pl.when(slow)
    def _():
        tri = jnp.where(lax.broadcasted_iota(I32, (tk, tk), 1) <= lax.broadcasted_iota(I32, (tk, tk), 0),
                        1.0, 0.0).astype(BF16)

        def body(kb, seen):
            kblk = key_buf[kb]
            eq = jnp.where((kblk == t) & (kblk != INT_MIN), 1.0, 0.0)
            prefix = jnp.dot(tri, eq.astype(BF16), preferred_element_type=F32) + seen
            keep_tie = jnp.where(prefix <= need, eq, 0.0)
            sel = jnp.where(kblk > t, 1.0, keep_tie)
            bias_buf[kb] = jnp.where(sel > 0.0, 0.0, NEG_BIG)
            return seen + col_sum(eq)
        lax.fori_loop(0, nkb, body, jnp.zeros((1, tq), F32))

    m_scr[...] = jnp.full(m_scr.shape, NEG_BIG, F32)
    l_scr[...] = jnp.zeros(l_scr.shape, F32)
    acc_scr[...] = jnp.zeros(acc_scr.shape, F32)
    sub = LANES
    nsub = tk // sub

    def attn_block(kb, carry):
        for c in range(nsub):
            off = pl.multiple_of(kb * tk + c * sub, sub)
            for g in range(N_KV_HEADS):
                s_scr[c, g] = lax.dot_general(
                    k_ref[0, g, pl.ds(off, sub), :],
                    q_ref[0, Q_PER_KV * g:Q_PER_KV * (g + 1)].reshape(Q_PER_KV * tq, HEAD_DIM),
                    nt, preferred_element_type=F32)
        m = [m_scr[g] for g in range(N_KV_HEADS)]
        l = [l_scr[g] for g in range(N_KV_HEADS)]
        for c in range(nsub):
            bias = bias_buf[kb, c * sub:(c + 1) * sub, :]
            bias2 = jnp.concatenate([bias] * Q_PER_KV, axis=1)
            for g in range(N_KV_HEADS):
                s = s_scr[c, g] + bias2
                m_new = jnp.maximum(m[g], jnp.max(s, axis=0, keepdims=True))
                alpha = jnp.exp2(m[g] - m_new)
                p = jnp.exp2(s - m_new)
                l[g] = alpha * l[g] + col_sum(p)
                pv = jnp.dot(vt_ref[0, kb * nsub + c, g * HEAD_DIM:(g + 1) * HEAD_DIM, :], p.astype(BF16),
                             preferred_element_type=F32)
                acc_scr[g] = alpha * acc_scr[g] + pv
                m[g] = m_new
        for g in range(N_KV_HEADS):
            m_scr[g] = m[g]
            l_scr[g] = l[g]
        return carry

    lax.fori_loop(0, nkb, attn_block, 0)

    o_t = []
    for g in range(N_KV_HEADS):
        og = acc_scr[g] / l_scr[g]
        o_t.extend(og[:, hh * tq:(hh + 1) * tq] for hh in range(Q_PER_KV))
    y_att = jnp.transpose(jnp.concatenate(o_t, axis=0)).astype(BF16)
    y = jnp.dot(yp_ref[0], wo_ref[0:POOL_WIDTH, :], preferred_element_type=F32)
    y = y + jnp.dot(y_att, wo_ref[POOL_WIDTH:POOL_WIDTH + ATT_WIDTH, :], preferred_element_type=F32)
    o_ref[0] = x_ref[0] + y


def _dsa(q_hm, qi_hm, kiwi, k_hm, v_t, ki_all, x, yp, wo_bf, *, tk, pos0, n_keys, n_sel):
    B, T, D = x.shape
    L = k_hm.shape[2]
    tq = LANES
    assert L % tk == 0 and T % tq == 0
    nkb_max = L // tk
    tile = lambda wdt: pl.BlockSpec((1, tq, wdt), lambda b, j: (b, j, 0))
    in_specs = [pl.BlockSpec((1, N_HEADS, tq, HEAD_DIM), lambda b, j: (b, 0, j, 0)),
                pl.BlockSpec((1, N_IDX_HEADS, tq, IDX_DIM), lambda b, j: (b, 0, j, 0)),
                tile(LANES),
                pl.BlockSpec((1, N_KV_HEADS, L, HEAD_DIM), lambda b, j: (b, 0, 0, 0)),
                pl.BlockSpec((1, L // LANES, KV_WIDTH, LANES), lambda b, j: (b, 0, 0, 0)),
                pl.BlockSpec((1, L, IDX_DIM), lambda b, j: (b, 0, 0)),
                tile(D), tile(POOL_WIDTH), pl.BlockSpec((POOL_WIDTH + ATT_WIDTH, D), lambda b, j: (0, 0))]
    return pl.pallas_call(
        functools.partial(_dsa_kernel, tk=tk, pos0=pos0, n_keys=n_keys, n_sel=n_sel),
        out_shape=jax.ShapeDtypeStruct((B, T, D), F32), grid=(B, T // tq),
        in_specs=in_specs, out_specs=tile(D),
        scratch_shapes=[pltpu.VMEM((nkb_max, tk, tq), I32), pltpu.VMEM((nkb_max, tk, tq), F32),
                        pltpu.VMEM((N_KV_HEADS, 1, Q_PER_KV * tq), F32),
                        pltpu.VMEM((N_KV_HEADS, 1, Q_PER_KV * tq), F32),
                        pltpu.VMEM((N_KV_HEADS, HEAD_DIM, Q_PER_KV * tq), F32),
                        pltpu.VMEM((tk // LANES, N_KV_HEADS, LANES, Q_PER_KV * tq), F32)],
        compiler_params=pltpu.CompilerParams(dimension_semantics=("parallel", "arbitrary"),
                                             vmem_limit_bytes=VMEM_LIMIT),
        name="dsa",
    )(q_hm, qi_hm, kiwi, k_hm, v_t, ki_all, x, yp, wo_bf)


def _gmlp_kernel(x_ref, g_ref, win_ref, lng_ref, lnb_ref, ws_ref, bias_ref, wout_ref, o_ref, v_ref, *, tm):
    x = x_ref[...]
    h = _rms(x, g_ref[...])
    z = jax.nn.gelu(jnp.dot(h.astype(BF16), win_ref[...], preferred_element_type=F32))
    half = z.shape[1] // 2
    u, v = z[:, :half], z[:, half:]
    mu = jnp.mean(v, axis=-1, keepdims=True)
    var = jnp.mean(jnp.square(v - mu), axis=-1, keepdims=True)
    vn = (v - mu) * lax.rsqrt(var + LN_EPS) * lng_ref[...] + lnb_ref[...]
    v_ref[...] = vn
    gd = half // GM_GROUPS
    gated = []
    for c in range(tm // GM_CHUNK):
        rows = slice(c * GM_CHUNK, (c + 1) * GM_CHUNK)
        vc = vn[rows].astype(BF16)
        mixed = jnp.concatenate(
            [jnp.dot(ws_ref[0, g], vc[:, g * gd:(g + 1) * gd], preferred_element_type=F32)
             for g in range(GM_GROUPS)], axis=1) + bias_ref[0]
        gated.append((u[rows] * mixed).astype(BF16))
    gated = jnp.concatenate(gated, axis=0)
    o_ref[...] = x + jnp.dot(gated, wout_ref[...], preferred_element_type=F32)


def _gmlp(x, g, win_bf, lng, lnb, ws2, bias2, wout_bf, *, tm, n_first, n_v_rows):
    N, D = x.shape
    half = win_bf.shape[1] // 2
    nt = N // tm
    t_first = n_first // tm
    variant = lambda i: jnp.where(i >= t_first, 1, 0)
    row = pl.BlockSpec((tm, D), lambda i: (i, 0))
    const = lambda s: pl.BlockSpec(s, lambda i: (0, 0))
    in_specs = [row, const((1, D)), const((D, 2 * half)), const((1, half)), const((1, half)),
                pl.BlockSpec((1, GM_GROUPS, GM_CHUNK, GM_CHUNK), lambda i: (variant(i), 0, 0, 0)),
                pl.BlockSpec((1, GM_CHUNK, half), lambda i: (variant(i), 0, 0)),
                const((half, D))]
    out_specs = (row, pl.BlockSpec((tm, half), lambda i: (jnp.maximum(i - t_first, 0), 0)))
    return pl.pallas_call(
        functools.partial(_gmlp_kernel, tm=tm),
        out_shape=(jax.ShapeDtypeStruct((N, D), F32), jax.ShapeDtypeStruct((n_v_rows, half), F32)),
        grid=(nt,), in_specs=in_specs, out_specs=out_specs,
        compiler_params=pltpu.CompilerParams(dimension_semantics=("arbitrary",), vmem_limit_bytes=VMEM_LIMIT),
        name="gmlp",
    )(x, g, win_bf, lng, lnb, ws2, bias2, wout_bf)


ROUTE_E0, ROUTE_E1, ROUTE_R0, ROUTE_R1, ROUTE_G0, ROUTE_G1 = range(6)


def _router_kernel(x_ref, g_ref, wr_ref, route_ref, cnt_ref, carry_ref, *, tm):
    i = pl.program_id(0)

    @pl.when(i == 0)
    def _():
        carry_ref[...] = jnp.zeros(carry_ref.shape, F32)

    h = _rms(x_ref[...], g_ref[...])
    logits = jnp.dot(h.astype(BF16), wr_ref[...], preferred_element_type=F32)
    lane = lax.broadcasted_iota(I32, (tm, LANES), 1).astype(F32)
    ninf = -jnp.inf
    big = 1e9
    rmax = lambda a: jnp.max(a, axis=1, keepdims=True)
    rmin = lambda a: jnp.min(a, axis=1, keepdims=True)
    rsum = lambda a: jnp.sum(a, axis=1, keepdims=True)

    is_grp = lane < N_EXPERT_GROUPS
    lg = jnp.where(is_grp, logits, ninf)
    mg = rmax(lg)
    g_sel = rmin(jnp.where(lg == mg, lane, big))
    p_grp = 1.0 / rsum(jnp.where(is_grp, jnp.exp(lg - mg), 0.0))
    lo = N_EXPERT_GROUPS + g_sel * EXPERTS_PER_GROUP
    le = jnp.where((lane >= lo) & (lane < lo + EXPERTS_PER_GROUP), logits, ninf)
    v1 = rmax(le)
    j1 = rmin(jnp.where(le == v1, lane, big))
    le2 = jnp.where(lane == j1, ninf, le)
    v2 = rmax(le2)
    j2 = rmin(jnp.where(le2 == v2, lane, big))
    e0 = j1 - N_EXPERT_GROUPS
    e1 = j2 - N_EXPERT_GROUPS
    r = jnp.exp(v2 - v1)
    g0 = p_grp / (1.0 + r)
    g1 = p_grp * r / (1.0 + r)

    oh0 = jnp.where(lane == e0, 1.0, 0.0)
    oh1 = jnp.where(lane == e1, 1.0, 0.0)
    oh = oh0 + oh1
    lower = jnp.where(lax.broadcasted_iota(I32, (tm, tm), 1) < lax.broadcasted_iota(I32, (tm, tm), 0),
                      1.0, 0.0).astype(BF16)
    before = jnp.dot(lower, oh.astype(BF16), preferred_element_type=F32) + carry_ref[...]
    r0 = rsum(oh0 * before)
    r1 = rsum(oh1 * before)
    carry_ref[...] = carry_ref[...] + jnp.sum(oh, axis=0, keepdims=True)
    cnt_ref[...] = carry_ref[...]

    route = jnp.zeros((tm, LANES), F32)
    for col, val in ((ROUTE_E0, e0), (ROUTE_E1, e1), (ROUTE_R0, r0), (ROUTE_R1, r1), (ROUTE_G0, g0), (ROUTE_G1, g1)):
        route = jnp.where(lane == col, val, route)
    route_ref[...] = route


def _router(x, g, wr, *, tm):
    N, D = x.shape
    row = lambda wdt: pl.BlockSpec((tm, wdt), lambda i: (i, 0))
    const = lambda s: pl.BlockSpec(s, lambda i: (0, 0))
    return pl.pallas_call(
        functools.partial(_router_kernel, tm=tm),
        out_shape=(jax.ShapeDtypeStruct((N, LANES), F32), jax.ShapeDtypeStruct((1, LANES), F32)),
        grid=(N // tm,), in_specs=[row(D), const((1, D)), const((D, LANES))],
        out_specs=(row(LANES), const((1, LANES))),
        scratch_shapes=[pltpu.VMEM((1, LANES), F32)],
        compiler_params=pltpu.CompilerParams(dimension_semantics=("arbitrary",), vmem_limit_bytes=VMEM_LIMIT),
        name="router",
    )(x, g, wr)


SUBLANES = 8


def _dispatch_kernel(d0_ref, d1_ref, pad_ref, x_ref, g_ref, rows_ref, h_scr, zblk, sems, *, tm):
    i = pl.program_id(0)
    n = pl.num_programs(0)
    slot = i % 2
    groups = tm // SUBLANES

    def wait_slot(s):
        for _ in range(2 * groups):
            pltpu.make_async_copy(h_scr.at[s, 0], rows_ref.at[pl.ds(0, SUBLANES)], sems.at[s]).wait()

    @pl.when(i >= 2)
    def _():
        wait_slot(slot)

    h_scr[slot] = _rms(x_ref[...], g_ref[...]).reshape(groups, SUBLANES, x_ref.shape[1])

    def body(k, carry):
        for u in range(SUBLANES):
            r = k * SUBLANES + u
            src = h_scr.at[slot, k, pl.ds(u, 1)]
            pltpu.make_async_copy(src, rows_ref.at[pl.ds(d0_ref[r], 1)], sems.at[slot]).start()
            pltpu.make_async_copy(src, rows_ref.at[pl.ds(d1_ref[r], 1)], sems.at[slot]).start(priority=1)
        return carry

    lax.fori_loop(0, groups, body, 0)

    @pl.when(i == n - 1)
    def _():
        wait_slot(slot)

    @pl.when(jnp.logical_and(i == n - 1, n >= 2))
    def _():
        wait_slot(1 - slot)

    @pl.when(i == n - 1)
    def _():
        zblk[...] = jnp.zeros(zblk.shape, F32)
        zrow = zblk.at[pl.ds(0, 1)]
        for e in range(N_EXPERTS):
            first, count = pad_ref[0, e], pad_ref[1, e]

            def start(k, carry):
                pltpu.make_async_copy(zrow, rows_ref.at[pl.ds(first + k, 1)], sems.at[2]).start()
                return carry

            def wait(k, carry):
                pltpu.make_async_copy(zrow, rows_ref.at[pl.ds(first, 1)], sems.at[2]).wait()
                return carry

            lax.fori_loop(0, count, start, 0)
            lax.fori_loop(0, count, wait, 0)

        def blk_copy(b):
            return pltpu.make_async_copy(zblk, rows_ref.at[pl.ds(b * MOE_BLOCK, MOE_BLOCK)], sems.at[2])

        n_blocks = rows_ref.shape[0] // MOE_BLOCK
        n_used = pad_ref[2, 0]
        lax.fori_loop(n_used, n_blocks, lambda b, c: (blk_copy(b).start(), c)[1], 0)
        lax.fori_loop(n_used, n_blocks, lambda b, c: (blk_copy(n_used).wait(), c)[1], 0)


def _dispatch(dest_flat, pad_info, x, g, *, tm, n_rows):
    N, D = x.shape
    nt = N // tm
    return pl.pallas_call(
        functools.partial(_dispatch_kernel, tm=tm),
        out_shape=jax.ShapeDtypeStruct((n_rows, D), F32), grid=(nt,),
        in_specs=[pl.BlockSpec((tm,), lambda i: (i,), memory_space=pltpu.SMEM),
                  pl.BlockSpec((tm,), lambda i: (i + nt,), memory_space=pltpu.SMEM),
                  pl.BlockSpec(memory_space=pltpu.SMEM),
                  pl.BlockSpec((tm, D), lambda i: (i, 0)),
                  pl.BlockSpec((1, D), lambda i: (0, 0))],
        out_specs=pl.BlockSpec(memory_space=pl.ANY),
        scratch_shapes=[pltpu.VMEM((2, tm // SUBLANES, SUBLANES, D), F32), pltpu.VMEM((MOE_BLOCK, D), F32),
                        pltpu.SemaphoreType.DMA((3,))],
        compiler_params=pltpu.CompilerParams(dimension_semantics=("arbitrary",), vmem_limit_bytes=VMEM_LIMIT),
        name="dispatch",
    )(dest_flat, dest_flat, pad_info, x, g)


def _expert_kernel(be_ref, nu_ref, x_ref, wg_ref, wu_ref, wd_ref, y_ref, wg_bf, wu_bf, wd_bf):
    i = pl.program_id(0)

    @pl.when(jnp.logical_or(i == 0, be_ref[i] != be_ref[jnp.maximum(i - 1, 0)]))
    def _():
        wg_bf[...] = wg_ref[0].astype(BF16)
        wu_bf[...] = wu_ref[0].astype(BF16)
        wd_bf[...] = wd_ref[0].astype(BF16)

    @pl.when(i < nu_ref[0])
    def _():
        x = x_ref[...].astype(BF16)
        a = jnp.dot(x, wg_bf[...], preferred_element_type=F32)
        u = jnp.dot(x, wu_bf[...], preferred_element_type=F32)
        act = (a * jax.nn.sigmoid(a)) * u
        y_ref[...] = jnp.dot(act.astype(BF16), wd_bf[...], preferred_element_type=F32)

    @pl.when(i >= nu_ref[0])
    def _():
        y_ref[...] = jnp.zeros(y_ref.shape, F32)


def _experts(block_e, n_used, x_rows, w_gate, w_up, w_down):
    n_rows, D = x_rows.shape
    de = w_gate.shape[2]
    n_blocks = n_rows // MOE_BLOCK
    grid_spec = pltpu.PrefetchScalarGridSpec(
        num_scalar_prefetch=2, grid=(n_blocks,),
        in_specs=[pl.BlockSpec((MOE_BLOCK, D), lambda i, be, nu: (jnp.minimum(i, nu[0] - 1), 0)),
                  pl.BlockSpec((1, D, de), lambda i, be, nu: (be[i], 0, 0)),
                  pl.BlockSpec((1, D, de), lambda i, be, nu: (be[i], 0, 0)),
                  pl.BlockSpec((1, de, D), lambda i, be, nu: (be[i], 0, 0))],
        out_specs=pl.BlockSpec((MOE_BLOCK, D), lambda i, be, nu: (i, 0)),
        scratch_shapes=[pltpu.VMEM((D, de), BF16), pltpu.VMEM((D, de), BF16), pltpu.VMEM((de, D), BF16)])
    return pl.pallas_call(
        _expert_kernel, out_shape=jax.ShapeDtypeStruct((n_rows, D), F32), grid_spec=grid_spec,
        compiler_params=pltpu.CompilerParams(dimension_semantics=("arbitrary",), vmem_limit_bytes=VMEM_LIMIT),
        name="experts",
    )(block_e, n_used, x_rows, w_gate, w_up, w_down)


def _combine_kernel(dc0_ref, dc1_ref, dn0_ref, dn1_ref, x_ref, route_ref, g_ref, y_ref, o_ref, ybuf, sems,
                    *, tm, final_norm):
    i = pl.program_id(0)
    n = pl.num_programs(0)
    slot = i % 2
    groups = tm // SUBLANES
    D = x_ref.shape[1]

    def gather(d_refs, s):
        def body(k, carry):
            for u in range(SUBLANES):
                r = k * SUBLANES + u
                for a in range(2):
                    pltpu.make_async_copy(y_ref.at[pl.ds(d_refs[a][r], 1)], ybuf.at[s, a, k, pl.ds(u, 1)],
                                          sems.at[s]).start(priority=a)
            return carry
        lax.fori_loop(0, groups, body, 0)

    @pl.when(i == 0)
    def _():
        gather((dc0_ref, dc1_ref), 0)

    @pl.when(i + 1 < n)
    def _():
        gather((dn0_ref, dn1_ref), 1 - slot)

    for _ in range(2 * groups):
        pltpu.make_async_copy(y_ref.at[pl.ds(0, SUBLANES)], ybuf.at[slot, 0, 0], sems.at[slot]).wait()
    route = route_ref[...]
    g0 = route[:, ROUTE_G0:ROUTE_G0 + 1]
    g1 = route[:, ROUTE_G1:ROUTE_G1 + 1]
    out = x_ref[...] + (ybuf[slot, 0].reshape(tm, D) * g0 + ybuf[slot, 1].reshape(tm, D) * g1)
    if final_norm:
        out = _rms(out, g_ref[...])
    o_ref[...] = out


def _combine(dest_flat, x, route, g, y_rows, *, tm, final_norm):
    N, D = x.shape
    nt = N // tm
    row = lambda wdt: pl.BlockSpec((tm, wdt), lambda i: (i, 0))
    nxt = lambda i: jnp.minimum(i + 1, nt - 1)
    smem = lambda f: pl.BlockSpec((tm,), f, memory_space=pltpu.SMEM)
    return pl.pallas_call(
        functools.partial(_combine_kernel, tm=tm, final_norm=final_norm),
        out_shape=jax.ShapeDtypeStruct((N, D), F32), grid=(nt,),
        in_specs=[smem(lambda i: (i,)), smem(lambda i: (i + nt,)),
                  smem(lambda i: (nxt(i),)), smem(lambda i: (nxt(i) + nt,)),
                  row(D), row(LANES), pl.BlockSpec((1, D), lambda i: (0, 0)),
                  pl.BlockSpec(memory_space=pl.ANY)],
        out_specs=row(D),
        scratch_shapes=[pltpu.VMEM((2, 2, tm // SUBLANES, SUBLANES, D), F32), pltpu.SemaphoreType.DMA((2,))],
        compiler_params=pltpu.CompilerParams(dimension_semantics=("arbitrary",), vmem_limit_bytes=VMEM_LIMIT),
        name="combine",
    )(dest_flat, dest_flat, dest_flat, dest_flat, x, route, g, y_rows)


def _hier_moe(x, g_ffn, w_rg, w_re, w_gate, w_up, w_down, g_final, *, final_norm):
    N, D = x.shape
    tm = 256
    wr = jnp.concatenate([w_rg, w_re, jnp.zeros((D, LANES - N_EXPERT_GROUPS - N_EXPERTS), F32)], axis=1).astype(BF16)
    g_ffn = g_ffn.reshape(1, D)
    route, counts = _router(x, g_ffn, wr, tm=tm)
    counts = counts[0, :N_EXPERTS].astype(I32)
    eid_t = jnp.transpose(route[:, ROUTE_E0:ROUTE_E1 + 1]).astype(I32)
    rank_t = jnp.transpose(route[:, ROUTE_R0:ROUTE_R1 + 1]).astype(I32)
    padded = (counts + MOE_BLOCK - 1) // MOE_BLOCK * MOE_BLOCK
    pad_end = jnp.cumsum(padded)
    pad_start = pad_end - padded
    expert_ids = jnp.arange(N_EXPERTS, dtype=I32)
    start_of = jnp.sum(jnp.where(eid_t[:, :, None] == expert_ids, pad_start, 0), axis=-1)
    dest_t = start_of + rank_t
    n_blocks = -(-(2 * N) // MOE_BLOCK) + N_EXPERTS
    n_rows = n_blocks * MOE_BLOCK
    n_used = (pad_end[-1] // MOE_BLOCK).astype(I32)
    block_start = jnp.minimum(jnp.arange(n_blocks, dtype=I32), n_used - 1) * MOE_BLOCK
    block_e = jnp.minimum(jnp.sum(block_start[:, None] >= pad_end[None, :], axis=1), N_EXPERTS - 1).astype(I32)
    pad_info = jnp.stack([pad_start + counts, padded - counts, jnp.broadcast_to(n_used, (N_EXPERTS,))])
    dest_flat = dest_t.reshape(-1)
    x_rows = _dispatch(dest_flat, pad_info, x, g_ffn, tm=tm, n_rows=n_rows)
    y_rows = _experts(block_e, n_used.reshape(1), x_rows, w_gate, w_up, w_down)
    return _combine(dest_flat, x, route, g_final.reshape(1, D), y_rows, tm=tm, final_norm=final_norm)


def _rope_tables(pos, rot_lanes):
    inv = 1.0 / (ROPE_THETA ** (jnp.arange(ROT_HALF, dtype=F32) / ROT_HALF))
    ang = pos.astype(F32)[:, None] * inv[None, :]
    cos, sin = jnp.cos(ang), jnp.sin(ang)
    lane = jnp.arange(LANES)
    r = lane % HEAD_DIM
    active = (lane < rot_lanes)
    first = active & (r < ROT_HALF)
    second = active & (r >= ROT_HALF) & (r < 2 * ROT_HALF)
    cos_l = cos[:, r % ROT_HALF]
    sin_l = sin[:, r % ROT_HALF]
    c = jnp.where((first | second)[None, :], cos_l, 1.0)
    sa = jnp.where(first[None, :], -sin_l, 0.0)
    sb = jnp.where(second[None, :], sin_l, 0.0)
    return jnp.stack([c, sa, sb]).astype(F32)


def _prep_w_in(w_in):
    D = w_in.shape[0]
    xp, q, k, v, qi, ki, wi = jnp.split(w_in, [512, 1024, 1280, 1536, 2048, 2112], axis=1)
    pad = jnp.zeros((D, IN_WIDTH_PAD - w_in.shape[1]), w_in.dtype)
    return jnp.concatenate([xp, q, k, v, qi, ki, wi, pad], axis=1).astype(BF16)


def kernel(x_prompt, x_sample, cache_k, cache_v, cache_idx_k, state_pool, norm_mix, norm_ffn, norm_final,
           par_w_in, par_pool_w, par_pool_scale, par_w_out, gm_w_in, gm_ln_g, gm_ln_b, gm_ws, gm_bs, gm_w_out,
           moe_router_group, moe_router_expert, moe_w_gate, moe_w_up, moe_w_down):
    Bp, Tp, D = x_prompt.shape
    Bs, Ts, _ = x_sample.shape
    past = cache_k.shape[2]
    Np, Ns = Bp * Tp, Bs * Ts
    depth = norm_mix.shape[0]
    assert depth == 2 and Ts == CHUNK and Tp % 256 == 0 and Ns % 256 == 0

    w_in_bf = _prep_w_in(par_w_in[0])
    pw_bf = par_pool_w[0].astype(BF16)
    ps = par_pool_scale[0].reshape(1, POOL_WIDTH)
    wo_bf = par_w_out[0].astype(BF16)
    g_mix0 = norm_mix[0].reshape(1, D)
    pos_p = jnp.arange(Tp, dtype=I32)
    pos_s = past + jnp.arange(Ts, dtype=I32)

    hist_p = jnp.zeros((Bp, HIST_ROWS, POOL_WIDTH), F32)
    hist_s = jnp.pad(state_pool[0], ((0, 0), (1, 0), (0, 0)))
    tk = 512
    (q_p, qi_p, k_p, v_p, kiwi_p, kbf_p, vbf_p, kibf_p, yp_p, st_p) = _inproj(
        x_prompt, g_mix0, w_in_bf, _rope_tables(pos_p, LANES), _rope_tables(pos_p, IDX_DIM), hist_p, pw_bf, ps,
        tm=256, pos0=0)
    (q_s, qi_s, k_s, v_s, kiwi_s, kbf_s, vbf_s, kibf_s, yp_s, st_s) = _inproj(
        x_sample, g_mix0, w_in_bf, _rope_tables(pos_s, LANES), _rope_tables(pos_s, IDX_DIM), hist_s, pw_bf, ps,
        tm=Ts, pos0=past)

    assert Tp % tk == 0
    x1_p = _dsa(q_p, qi_p, kiwi_p, kbf_p, vbf_p, kibf_p, x_prompt, yp_p, wo_bf,
                tk=tk, pos0=0, n_keys=Tp, n_sel=min(TOPK_MAX, Tp // 4))
    ls = past + Ts
    lsp = -(-ls // tk) * tk
    kpad = lambda a, ax: jnp.pad(a, [(0, lsp - ls) if d == ax else (0, 0) for d in range(a.ndim)])
    kall = kpad(jnp.concatenate([jnp.transpose(cache_k[0], (0, 2, 1, 3)).astype(BF16), kbf_s], axis=2), 2)
    vall = kpad(jnp.concatenate([cache_v[0].reshape(Bs, past, KV_WIDTH).astype(BF16), vbf_s], axis=1), 1)
    vall_t = jnp.transpose(vall.reshape(Bs, lsp // LANES, LANES, KV_WIDTH), (0, 1, 3, 2))
    kiall = kpad(jnp.concatenate([cache_idx_k[0].astype(BF16), kibf_s], axis=1), 1)
    qpad = lambda a, ax: jnp.pad(a, [(0, LANES - Ts) if d == ax else (0, 0) for d in range(a.ndim)])
    x1_s = _dsa(qpad(q_s, 2), qpad(qi_s, 2), qpad(kiwi_s, 1), kall, vall_t, kiall, qpad(x_sample, 1), qpad(yp_s, 1),
                wo_bf, tk=tk, pos0=past, n_keys=ls, n_sel=min(TOPK_MAX, ls // 4))[:, :Ts]

    x = jnp.concatenate([x1_p.reshape(Np, D), x1_s.reshape(Ns, D)], axis=0)
    x = _hier_moe(x, norm_ffn[0], moe_router_group[0], moe_router_expert[0],
                  moe_w_gate[0], moe_w_up[0], moe_w_down[0], norm_final, final_norm=False)

    cs = Ts
    tril = lambda n: jnp.tril(jnp.ones((n, n), bool))
    ws_p = jnp.where(tril(GM_CHUNK)[None], gm_ws[0], 0.0)
    ws_small = jnp.where(tril(cs)[None], gm_ws[0][:, :cs, :cs], 0.0)
    rep = GM_CHUNK // cs
    ws_s = jnp.einsum('ab,gts->gatbs', jnp.eye(rep, dtype=F32), ws_small).reshape(GM_GROUPS, GM_CHUNK, GM_CHUNK)
    ws2 = jnp.stack([ws_p, ws_s]).astype(BF16)
    gd = D // GM_GROUPS
    bias_p = jnp.repeat(jnp.transpose(gm_bs[0]), gd, axis=1)
    bias_s = jnp.tile(jnp.repeat(jnp.transpose(gm_bs[0][:, :cs]), gd, axis=1), (rep, 1))
    bias2 = jnp.stack([bias_p, bias_s])
    x, gm_v = _gmlp(x, norm_mix[1].reshape(1, D), gm_w_in[0].astype(BF16), gm_ln_g[0].reshape(1, D),
                    gm_ln_b[0].reshape(1, D), ws2, bias2, gm_w_out[0].astype(BF16),
                    tm=256, n_first=Np, n_v_rows=Ns)
    x = _hier_moe(x, norm_ffn[1], moe_router_group[1], moe_router_expert[1],
                  moe_w_gate[1], moe_w_up[1], moe_w_down[1], norm_final, final_norm=True)

    y_p = x[:Np].reshape(Bp, Tp, D)
    y_s = x[Np:].reshape(Bs, Ts, D)
    r4 = lambda a, b, t: a.reshape(1, b, t, N_KV_HEADS, HEAD_DIM)
    return (y_p, y_s,
            r4(k_p, Bp, Tp), r4(v_p, Bp, Tp), kiwi_p[:, :, :IDX_DIM][None], st_p[:, 1:][None],
            r4(k_s, Bs, Ts), r4(v_s, Bs, Ts), kiwi_s[:, :, :IDX_DIM][None], st_s[:, 1:][None],
            gm_v.reshape(1, Bs, Ts, D))
```

```python
import functools

import jax
import jax.numpy as jnp
from jax import lax
from jax.experimental import pallas as pl
from jax.experimental.pallas import tpu as pltpu

F32 = jnp.float32
BF16 = jnp.bfloat16
I32 = jnp.int32

LANES = 128
CHUNK = 64
POOL_WINDOWS = (2, 4, 8, 16)
POOL_GROUP_DIM = 128
POOL_WIDTH = 512
HIST_ROWS = 16
N_HEADS = 8
HEAD_DIM = 64
N_KV_HEADS = 4
Q_PER_KV = N_HEADS // N_KV_HEADS
ATT_WIDTH = N_HEADS * HEAD_DIM
KV_WIDTH = N_KV_HEADS * HEAD_DIM
N_IDX_HEADS = 8
IDX_DIM = 64
TOPK_MAX = 256
ROPE_THETA = 500000.0
ROT_HALF = HEAD_DIM // 8
GM_CHUNK = 128
GM_GROUPS = 8
N_EXPERT_GROUPS = 4
EXPERTS_PER_GROUP = 8
N_EXPERTS = 32
MOE_BLOCK = 256
RMS_EPS = 1e-6
LN_EPS = 1e-5

INT_MIN = -2147483648
LOG2_E = 1.4426950408889634
NEG_BIG = -1e30
VMEM_LIMIT = 48 * 1024 * 1024

COL_XP, COL_Q, COL_K, COL_V, COL_QI, COL_KIWI = 0, 512, 1024, 1280, 1536, 2048
IN_WIDTH_PAD = 2176


def _rms(x, g):
    return x * lax.rsqrt(jnp.mean(x * x, axis=-1, keepdims=True) + RMS_EPS) * g


def _rope128(x, c, sa, sb):
    return x * c + pltpu.roll(x, LANES - ROT_HALF, 1) * sa + pltpu.roll(x, ROT_HALF, 1) * sb


def _inproj_kernel(x_ref, g_ref, w_ref, rope_ref, ropeki_ref, hist_ref, pw_ref, ps_ref,
                   q_ref, qi_ref, k_ref, v_ref, kiwi_ref, kbf_ref, vbf_ref, kibf_ref, yp_ref, state_ref,
                   buf_ref, *, tm, pos0):
    j = pl.program_id(1)
    h = _rms(x_ref[0], g_ref[...])
    proj = jnp.dot(h.astype(BF16), w_ref[...], preferred_element_type=F32)
    c, sa, sb = rope_ref[0], rope_ref[1], rope_ref[2]

    def put_heads(ref, i, chunk):
        ref[0, 2 * i] = chunk[:, :HEAD_DIM].astype(BF16)
        ref[0, 2 * i + 1] = chunk[:, HEAD_DIM:].astype(BF16)

    for i in range(ATT_WIDTH // LANES):
        put_heads(q_ref, i, _rope128(proj[:, COL_Q + i * LANES:COL_Q + (i + 1) * LANES], c, sa, sb)
                  * (HEAD_DIM ** -0.5 * LOG2_E))
        put_heads(qi_ref, i, _rope128(proj[:, COL_QI + i * LANES:COL_QI + (i + 1) * LANES], c, sa, sb))
    for i in range(KV_WIDTH // LANES):
        sl = slice(i * LANES, (i + 1) * LANES)
        kr = _rope128(proj[:, COL_K + i * LANES:COL_K + (i + 1) * LANES], c, sa, sb)
        k_ref[0, :, sl] = kr
        put_heads(kbf_ref, i, kr)
    vv = proj[:, COL_V:COL_V + KV_WIDTH]
    v_ref[0] = vv
    if tm % LANES == 0:
        for cc in range(tm // LANES):
            vbf_ref[0, cc] = jnp.transpose(vv[cc * LANES:(cc + 1) * LANES, :]).astype(BF16)
    else:
        vbf_ref[0] = vv.astype(BF16)
    kiwi = _rope128(proj[:, COL_KIWI:COL_KIWI + LANES], ropeki_ref[0], ropeki_ref[1], ropeki_ref[2])
    kiwi_ref[0] = kiwi
    kibf_ref[0] = kiwi[:, :IDX_DIM].astype(BF16)

    @pl.when(j == 0)
    def _():
        buf_ref[0:HIST_ROWS, :] = hist_ref[0]

    xp = proj[:, COL_XP:COL_XP + POOL_WIDTH]
    buf_ref[HIST_ROWS:HIST_ROWS + tm, :] = xp
    pos = pos0 + j * tm + lax.broadcasted_iota(I32, (tm, 1), 0)
    for gi, w in enumerate(POOL_WINDOWS):
        c0 = gi * POOL_GROUP_DIM
        s = xp[:, c0:c0 + POOL_GROUP_DIM]
        for i in range(1, w):
            s = s + buf_ref[HIST_ROWS - i:HIST_ROWS - i + tm, c0:c0 + POOL_GROUP_DIM]
        cnt = jnp.minimum(pos + 1, w).astype(F32)
        d = s / cnt - xp[:, c0:c0 + POOL_GROUP_DIM]
        y = jnp.dot(d.astype(BF16), pw_ref[gi], preferred_element_type=F32)
        yp_ref[0, :, c0:c0 + POOL_GROUP_DIM] = (y * ps_ref[:, c0:c0 + POOL_GROUP_DIM]).astype(BF16)
    tail = buf_ref[tm:tm + HIST_ROWS, :]
    state_ref[0] = tail
    buf_ref[0:HIST_ROWS, :] = tail


def _inproj(x, g, w_bf, rope, ropeki, hist, pw_bf, ps, *, tm, pos0):
    B, T, D = x.shape
    nt = T // tm
    f = lambda shape, dt: jax.ShapeDtypeStruct(shape, dt)
    v_t = tm % LANES == 0
    out_shape = (
        f((B, N_HEADS, T, HEAD_DIM), BF16), f((B, N_IDX_HEADS, T, IDX_DIM), BF16),
        f((B, T, KV_WIDTH), F32), f((B, T, KV_WIDTH), F32), f((B, T, LANES), F32),
        f((B, N_KV_HEADS, T, HEAD_DIM), BF16),
        f((B, T // LANES, KV_WIDTH, LANES) if v_t else (B, T, KV_WIDTH), BF16),
        f((B, T, IDX_DIM), BF16),
        f((B, T, POOL_WIDTH), BF16), f((B, HIST_ROWS, POOL_WIDTH), F32),
    )
    tile = lambda wdt: pl.BlockSpec((1, tm, wdt), lambda b, j: (b, j, 0))
    heads = lambda n, wdt: pl.BlockSpec((1, n, tm, wdt), lambda b, j: (b, 0, j, 0))
    const2 = lambda s: pl.BlockSpec(s, lambda b, j: (0, 0))
    in_specs = [
        tile(D), const2((1, D)), const2((D, IN_WIDTH_PAD)),
        pl.BlockSpec((3, tm, LANES), lambda b, j: (0, j, 0)),
        pl.BlockSpec((3, tm, LANES), lambda b, j: (0, j, 0)),
        pl.BlockSpec((1, HIST_ROWS, POOL_WIDTH), lambda b, j: (b, 0, 0)),
        pl.BlockSpec((len(POOL_WINDOWS), POOL_GROUP_DIM, POOL_GROUP_DIM), lambda b, j: (0, 0, 0)),
        const2((1, POOL_WIDTH)),
    ]
    out_specs = (
        heads(N_HEADS, HEAD_DIM), heads(N_IDX_HEADS, IDX_DIM), tile(KV_WIDTH), tile(KV_WIDTH), tile(LANES),
        heads(N_KV_HEADS, HEAD_DIM),
        pl.BlockSpec((1, tm // LANES, KV_WIDTH, LANES), lambda b, j: (b, j, 0, 0)) if v_t else tile(KV_WIDTH),
        tile(IDX_DIM), tile(POOL_WIDTH),
        pl.BlockSpec((1, HIST_ROWS, POOL_WIDTH), lambda b, j: (b, 0, 0)),
    )
    return pl.pallas_call(
        functools.partial(_inproj_kernel, tm=tm, pos0=pos0),
        out_shape=out_shape, grid=(B, nt), in_specs=in_specs, out_specs=out_specs,
        scratch_shapes=[pltpu.VMEM((HIST_ROWS + tm, POOL_WIDTH), F32)],
        compiler_params=pltpu.CompilerParams(dimension_semantics=("parallel", "arbitrary"),
                                             vmem_limit_bytes=VMEM_LIMIT),
        name="inproj",
    )(x, g, w_bf, rope, ropeki, hist, pw_bf, ps)


def _dsa_kernel(q_ref, qi_ref, kiwi_ref, k_ref, vt_ref, ki_ref, x_ref, yp_ref, wo_ref,
                o_ref, key_buf, bias_buf, m_scr, l_scr, acc_scr, s_scr,
                *, tk, pos0, n_keys, n_sel):
    tq = LANES
    slab = 64
    j = pl.program_id(1)
    base = pos0 + j * tq
    pos = base + lax.broadcasted_iota(I32, (1, tq), 1)
    limit = jnp.minimum((pos // CHUNK + 1) * CHUNK, n_keys)
    limit_max = jnp.minimum(((base + tq - 1) // CHUNK + 1) * CHUNK, n_keys)
    nkb = (limit_max + tk - 1) // tk
    nt = (((1,), (1,)), ((), ()))

    wi_t = jnp.transpose(kiwi_ref[0])[IDX_DIM:IDX_DIM + N_IDX_HEADS, :]

    def score_block(kb, carry):
        off = pl.multiple_of(kb * tk, tk)
        kiblk = ki_ref[0, pl.ds(off, tk), :]
        idx = jnp.zeros((tk, tq), F32)
        for h in range(N_IDX_HEADS):
            sc = lax.dot_general(kiblk, qi_ref[0, h], nt, preferred_element_type=F32)
            idx = idx + jnp.maximum(sc, 0.0) * wi_t[h:h + 1, :]
        idx = jnp.where(idx == 0.0, 0.0, idx)
        bits = lax.bitcast_convert_type(idx, I32)
        key = bits ^ ((bits >> 31) & 0x7FFFFFFF)
        key_buf[kb] = jnp.where(lax.broadcasted_iota(I32, (tk, tq), 0) < limit - kb * tk, key, INT_MIN)
        return carry

    lax.fori_loop(0, nkb, score_block, 0)

    def col_sum(a):
        return jnp.sum(a, axis=0, keepdims=True)

    def count_ge(cand):
        def body(kb, acc):
            kblk = key_buf[kb]
            for c in range(tk // slab):
                acc = acc + jnp.where(kblk[c * slab:(c + 1) * slab] >= cand, 1.0, 0.0)
            return acc
        return col_sum(lax.fori_loop(0, nkb, body, jnp.zeros((slab, tq), F32)))

    kf = float(n_sel)
    t0 = jnp.where(count_ge(jnp.zeros((1, tq), I32)) >= kf, 0, INT_MIN).astype(I32)

    def bit_body(i, t):
        cand = t | lax.shift_left(jnp.int32(1), 30 - i)
        return jnp.where(count_ge(cand) >= kf, cand, t)

    t = lax.fori_loop(0, 31, bit_body, t0)

    def count_gt_ge(t):
        def body(kb, accs):
            a_gt, a_ge = accs
            kblk = key_buf[kb]
            for c in range(tk // slab):
                blk = kblk[c * slab:(c + 1) * slab]
                a_gt = a_gt + jnp.where(blk > t, 1.0, 0.0)
                a_ge = a_ge + jnp.where(blk >= t, 1.0, 0.0)
            return a_gt, a_ge
        z = jnp.zeros((slab, tq), F32)
        a_gt, a_ge = lax.fori_loop(0, nkb, body, (z, z))
        return col_sum(a_gt), col_sum(a_ge)

    cnt_gt, cnt_ge = count_gt_ge(t)
    need = kf - cnt_gt
    cnt_eq = cnt_ge - cnt_gt
    overfull = jnp.where(t != INT_MIN, cnt_eq - need, 0.0)
    slow = jnp.max(overfull) > 0.0

    @pl.when(jnp.logical_not(slow))
    def _():
        t_adm = jnp.maximum(t, INT_MIN + 1)

        def body(kb, carry):
            bias_buf[kb] = jnp.where(key_buf[kb] >= t_adm, 0.0, NEG_BIG)
            return carry
        lax.fori_loop(0, nkb, body, 0)

    @pl.when(slow)
    def _():
        tri = jnp.where(lax.broadcasted_iota(I32, (tk, tk), 1) <= lax.broadcasted_iota(I32, (tk, tk), 0),
                        1.0, 0.0).astype(BF16)

        def body(kb, seen):
            kblk = key_buf[kb]
            eq = jnp.where((kblk == t) & (kblk != INT_MIN), 1.0, 0.0)
            prefix = jnp.dot(tri, eq.astype(BF16), preferred_element_type=F32) + seen
            keep_tie = jnp.where(prefix <= need, eq, 0.0)
            sel = jnp.where(kblk > t, 1.0, keep_tie)
            bias_buf[kb] = jnp.where(sel > 0.0, 0.0, NEG_BIG)
            return seen + col_sum(eq)
        lax.fori_loop(0, nkb, body, jnp.zeros((1, tq), F32))

    m_scr[...] = jnp.full(m_scr.shape, NEG_BIG, F32)
    l_scr[...] = jnp.zeros(l_scr.shape, F32)
    acc_scr[...] = jnp.zeros(acc_scr.shape, F32)
    sub = LANES
    nsub = tk // sub

    def attn_block(kb, carry):
        for c in range(nsub):
            off = pl.multiple_of(kb * tk + c * sub, sub)
            for g in range(N_KV_HEADS):
                s_scr[c, g] = lax.dot_general(
                    k_ref[0, g, pl.ds(off, sub), :],
                    q_ref[0, Q_PER_KV * g:Q_PER_KV * (g + 1)].reshape(Q_PER_KV * tq, HEAD_DIM),
                    nt, preferred_element_type=F32)
        m = [m_scr[g] for g in range(N_KV_HEADS)]
        l = [l_scr[g] for g in range(N_KV_HEADS)]
        for c in range(nsub):
            bias = bias_buf[kb, c * sub:(c + 1) * sub, :]
            bias2 = jnp.concatenate([bias] * Q_PER_KV, axis=1)
            for g in range(N_KV_HEADS):
                s = s_scr[c, g] + bias2
                m_new = jnp.maximum(m[g], jnp.max(s, axis=0, keepdims=True))
                alpha = jnp.exp2(m[g] - m_new)
                p = jnp.exp2(s - m_new)
                l[g] = alpha * l[g] + col_sum(p)
                pv = jnp.dot(vt_ref[0, kb * nsub + c, g * HEAD_DIM:(g + 1) * HEAD_DIM, :], p.astype(BF16),
                             preferred_element_type=F32)
                acc_scr[g] = alpha * acc_scr[g] + pv
                m[g] = m_new
        for g in range(N_KV_HEADS):
            m_scr[g] = m[g]
            l_scr[g] = l[g]
        return carry

    lax.fori_loop(0, nkb, attn_block, 0)

    o_t = []
    for g in range(N_KV_HEADS):
        og = acc_scr[g] / l_scr[g]
        o_t.extend(og[:, hh * tq:(hh + 1) * tq] for hh in range(Q_PER_KV))
    y_att = jnp.transpose(jnp.concatenate(o_t, axis=0)).astype(BF16)
    y = jnp.dot(yp_ref[0], wo_ref[0:POOL_WIDTH, :], preferred_element_type=F32)
    y = y + jnp.dot(y_att, wo_ref[POOL_WIDTH:POOL_WIDTH + ATT_WIDTH, :], preferred_element_type=F32)
    o_ref[0] = x_ref[0] + y


def _dsa(q_hm, qi_hm, kiwi, k_hm, v_t, ki_all, x, yp, wo_bf, *, tk, pos0, n_keys, n_sel):
    B, T, D = x.shape
    L = k_hm.shape[2]
    tq = LANES
    assert L % tk == 0 and T % tq == 0
    nkb_max = L // tk
    tile = lambda wdt: pl.BlockSpec((1, tq, wdt), lambda b, j: (b, j, 0))
    in_specs = [pl.BlockSpec((1, N_HEADS, tq, HEAD_DIM), lambda b, j: (b, 0, j, 0)),
                pl.BlockSpec((1, N_IDX_HEADS, tq, IDX_DIM), lambda b, j: (b, 0, j, 0)),
                tile(LANES),
                pl.BlockSpec((1, N_KV_HEADS, L, HEAD_DIM), lambda b, j: (b, 0, 0, 0)),
                pl.BlockSpec((1, L // LANES, KV_WIDTH, LANES), lambda b, j: (b, 0, 0, 0)),
                pl.BlockSpec((1, L, IDX_DIM), lambda b, j: (b, 0, 0)),
                tile(D), tile(POOL_WIDTH), pl.BlockSpec((POOL_WIDTH + ATT_WIDTH, D), lambda b, j: (0, 0))]
    return pl.pallas_call(
        functools.partial(_dsa_kernel, tk=tk, pos0=pos0, n_keys=n_keys, n_sel=n_sel),
        out_shape=jax.ShapeDtypeStruct((B, T, D), F32), grid=(B, T // tq),
        in_specs=in_specs, out_specs=tile(D),
        scratch_shapes=[pltpu.VMEM((nkb_max, tk, tq), I32), pltpu.VMEM((nkb_max, tk, tq), F32),
                        pltpu.VMEM((N_KV_HEADS, 1, Q_PER_KV * tq), F32),
                        pltpu.VMEM((N_KV_HEADS, 1, Q_PER_KV * tq), F32),
                        pltpu.VMEM((N_KV_HEADS, HEAD_DIM, Q_PER_KV * tq), F32),
                        pltpu.VMEM((tk // LANES, N_KV_HEADS, LANES, Q_PER_KV * tq), F32)],
        compiler_params=pltpu.CompilerParams(dimension_semantics=("parallel", "arbitrary"),
                                             vmem_limit_bytes=VMEM_LIMIT),
        name="dsa",
    )(q_hm, qi_hm, kiwi, k_hm, v_t, ki_all, x, yp, wo_bf)


def _gmlp_kernel(x_ref, g_ref, win_ref, lng_ref, lnb_ref, ws_ref, bias_ref, wout_ref, o_ref, v_ref, *, tm):
    x = x_ref[...]
    h = _rms(x, g_ref[...])
    z = jax.nn.gelu(jnp.dot(h.astype(BF16), win_ref[...], preferred_element_type=F32))
    half = z.shape[1] // 2
    u, v = z[:, :half], z[:, half:]
    mu = jnp.mean(v, axis=-1, keepdims=True)
    var = jnp.mean(jnp.square(v - mu), axis=-1, keepdims=True)
    vn = (v - mu) * lax.rsqrt(var + LN_EPS) * lng_ref[...] + lnb_ref[...]
    v_ref[...] = vn
    gd = half // GM_GROUPS
    gated = []
    for c in range(tm // GM_CHUNK):
        rows = slice(c * GM_CHUNK, (c + 1) * GM_CHUNK)
        vc = vn[rows].astype(BF16)
        mixed = jnp.concatenate(
            [jnp.dot(ws_ref[0, g], vc[:, g * gd:(g + 1) * gd], preferred_element_type=F32)
             for g in range(GM_GROUPS)], axis=1) + bias_ref[0]
        gated.append((u[rows] * mixed).astype(BF16))
    gated = jnp.concatenate(gated, axis=0)
    o_ref[...] = x + jnp.dot(gated, wout_ref[...], preferred_element_type=F32)


def _gmlp(x, g, win_bf, lng, lnb, ws2, bias2, wout_bf, *, tm, n_first, n_v_rows):
    N, D = x.shape
    half = win_bf.shape[1] // 2
    nt = N // tm
    t_first = n_first // tm
    variant = lambda i: jnp.where(i >= t_first, 1, 0)
    row = pl.BlockSpec((tm, D), lambda i: (i, 0))
    const = lambda s: pl.BlockSpec(s, lambda i: (0, 0))
    in_specs = [row, const((1, D)), const((D, 2 * half)), const((1, half)), const((1, half)),
                pl.BlockSpec((1, GM_GROUPS, GM_CHUNK, GM_CHUNK), lambda i: (variant(i), 0, 0, 0)),
                pl.BlockSpec((1, GM_CHUNK, half), lambda i: (variant(i), 0, 0)),
                const((half, D))]
    out_specs = (row, pl.BlockSpec((tm, half), lambda i: (jnp.maximum(i - t_first, 0), 0)))
    return pl.pallas_call(
        functools.partial(_gmlp_kernel, tm=tm),
        out_shape=(jax.ShapeDtypeStruct((N, D), F32), jax.ShapeDtypeStruct((n_v_rows, half), F32)),
        grid=(nt,), in_specs=in_specs, out_specs=out_specs,
        compiler_params=pltpu.CompilerParams(dimension_semantics=("arbitrary",), vmem_limit_bytes=VMEM_LIMIT),
        name="gmlp",
    )(x, g, win_bf, lng, lnb, ws2, bias2, wout_bf)


ROUTE_E0, ROUTE_E1, ROUTE_R0, ROUTE_R1, ROUTE_G0, ROUTE_G1 = range(6)


def _router_kernel(x_ref, g_ref, wr_ref, route_ref, cnt_ref, carry_ref, *, tm):
    i = pl.program_id(0)

    @pl.when(i == 0)
    def _():
        carry_ref[...] = jnp.zeros(carry_ref.shape, F32)

    h = _rms(x_ref[...], g_ref[...])
    logits = jnp.dot(h.astype(BF16), wr_ref[...], preferred_element_type=F32)
    lane = lax.broadcasted_iota(I32, (tm, LANES), 1).astype(F32)
    ninf = -jnp.inf
    big = 1e9
    rmax = lambda a: jnp.max(a, axis=1, keepdims=True)
    rmin = lambda a: jnp.min(a, axis=1, keepdims=True)
    rsum = lambda a: jnp.sum(a, axis=1, keepdims=True)

    is_grp = lane < N_EXPERT_GROUPS
    lg = jnp.where(is_grp, logits, ninf)
    mg = rmax(lg)
    g_sel = rmin(jnp.where(lg == mg, lane, big))
    p_grp = 1.0 / rsum(jnp.where(is_grp, jnp.exp(lg - mg), 0.0))
    lo = N_EXPERT_GROUPS + g_sel * EXPERTS_PER_GROUP
    le = jnp.where((lane >= lo) & (lane < lo + EXPERTS_PER_GROUP), logits, ninf)
    v1 = rmax(le)
    j1 = rmin(jnp.where(le == v1, lane, big))
    le2 = jnp.where(lane == j1, ninf, le)
    v2 = rmax(le2)
    j2 = rmin(jnp.where(le2 == v2, lane, big))
    e0 = j1 - N_EXPERT_GROUPS
    e1 = j2 - N_EXPERT_GROUPS
    r = jnp.exp(v2 - v1)
    g0 = p_grp / (1.0 + r)
    g1 = p_grp * r / (1.0 + r)

    oh0 = jnp.where(lane == e0, 1.0, 0.0)
    oh1 = jnp.where(lane == e1, 1.0, 0.0)
    oh = oh0 + oh1
    lower = jnp.where(lax.broadcasted_iota(I32, (tm, tm), 1) < lax.broadcasted_iota(I32, (tm, tm), 0),
                      1.0, 0.0).astype(BF16)
    before = jnp.dot(lower, oh.astype(BF16), preferred_element_type=F32) + carry_ref[...]
    r0 = rsum(oh0 * before)
    r1 = rsum(oh1 * before)
    carry_ref[...] = carry_ref[...] + jnp.sum(oh, axis=0, keepdims=True)
    cnt_ref[...] = carry_ref[...]

    route = jnp.zeros((tm, LANES), F32)
    for col, val in ((ROUTE_E0, e0), (ROUTE_E1, e1), (ROUTE_R0, r0), (ROUTE_R1, r1), (ROUTE_G0, g0), (ROUTE_G1, g1)):
        route = jnp.where(lane == col, val, route)
    route_ref[...] = route


def _router(x, g, wr, *, tm):
    N, D = x.shape
    row = lambda wdt: pl.BlockSpec((tm, wdt), lambda i: (i, 0))
    const = lambda s: pl.BlockSpec(s, lambda i: (0, 0))
    return pl.pallas_call(
        functools.partial(_router_kernel, tm=tm),
        out_shape=(jax.ShapeDtypeStruct((N, LANES), F32), jax.ShapeDtypeStruct((1, LANES), F32)),
        grid=(N // tm,), in_specs=[row(D), const((1, D)), const((D, LANES))],
        out_specs=(row(LANES), const((1, LANES))),
        scratch_shapes=[pltpu.VMEM((1, LANES), F32)],
        compiler_params=pltpu.CompilerParams(dimension_semantics=("arbitrary",), vmem_limit_bytes=VMEM_LIMIT),
        name="router",
    )(x, g, wr)


SUBLANES = 8


def _dispatch_kernel(d0_ref, d1_ref, pad_ref, x_ref, g_ref, rows_ref, h_scr, zblk, sems, *, tm):
    i = pl.program_id(0)
    n = pl.num_programs(0)
    slot = i % 2
    groups = tm // SUBLANES

    def wait_slot(s):
        for _ in range(2 * groups):
            pltpu.make_async_copy(h_scr.at[s, 0], rows_ref.at[pl.ds(0, SUBLANES)], sems.at[s]).wait()

    @pl.when(i >= 2)
    def _():
        wait_slot(slot)

    h_scr[slot] = _rms(x_ref[...], g_ref[...]).reshape(groups, SUBLANES, x_ref.shape[1])

    def body(k, carry):
        for u in range(SUBLANES):
            r = k * SUBLANES + u
            src = h_scr.at[slot, k, pl.ds(u, 1)]
            pltpu.make_async_copy(src, rows_ref.at[pl.ds(d0_ref[r], 1)], sems.at[slot]).start()
            pltpu.make_async_copy(src, rows_ref.at[pl.ds(d1_ref[r], 1)], sems.at[slot]).start(priority=1)
        return carry

    lax.fori_loop(0, groups, body, 0)

    @pl.when(i == n - 1)
    def _():
        wait_slot(slot)

    @pl.when(jnp.logical_and(i == n - 1, n >= 2))
    def _():
        wait_slot(1 - slot)

    @pl.when(i == n - 1)
    def _():
        zblk[...] = jnp.zeros(zblk.shape, F32)
        zrow = zblk.at[pl.ds(0, 1)]
        for e in range(N_EXPERTS):
            first, count = pad_ref[0, e], pad_ref[1, e]

            def start(k, carry):
                pltpu.make_async_copy(zrow, rows_ref.at[pl.ds(first + k, 1)], sems.at[2]).start()
                return carry

            def wait(k, carry):
                pltpu.make_async_copy(zrow, rows_ref.at[pl.ds(first, 1)], sems.at[2]).wait()
                return carry

            lax.fori_loop(0, count, start, 0)
            lax.fori_loop(0, count, wait, 0)

        def blk_copy(b):
            return pltpu.make_async_copy(zblk, rows_ref.at[pl.ds(b * MOE_BLOCK, MOE_BLOCK)], sems.at[2])

        n_blocks = rows_ref.shape[0] // MOE_BLOCK
        n_used = pad_ref[2, 0]
        lax.fori_loop(n_used, n_blocks, lambda b, c: (blk_copy(b).start(), c)[1], 0)
        lax.fori_loop(n_used, n_blocks, lambda b, c: (blk_copy(n_used).wait(), c)[1], 0)


def _dispatch(dest_flat, pad_info, x, g, *, tm, n_rows):
    N, D = x.shape
    nt = N // tm
    return pl.pallas_call(
        functools.partial(_dispatch_kernel, tm=tm),
        out_shape=jax.ShapeDtypeStruct((n_rows, D), F32), grid=(nt,),
        in_specs=[pl.BlockSpec((tm,), lambda i: (i,), memory_space=pltpu.SMEM),
                  pl.BlockSpec((tm,), lambda i: (i + nt,), memory_space=pltpu.SMEM),
                  pl.BlockSpec(memory_space=pltpu.SMEM),
                  pl.BlockSpec((tm, D), lambda i: (i, 0)),
                  pl.BlockSpec((1, D), lambda i: (0, 0))],
        out_specs=pl.BlockSpec(memory_space=pl.ANY),
        scratch_shapes=[pltpu.VMEM((2, tm // SUBLANES, SUBLANES, D), F32), pltpu.VMEM((MOE_BLOCK, D), F32),
                        pltpu.SemaphoreType.DMA((3,))],
        compiler_params=pltpu.CompilerParams(dimension_semantics=("arbitrary",), vmem_limit_bytes=VMEM_LIMIT),
        name="dispatch",
    )(dest_flat, dest_flat, pad_info, x, g)


def _expert_kernel(be_ref, nu_ref, x_ref, wg_ref, wu_ref, wd_ref, y_ref, wg_bf, wu_bf, wd_bf):
    i = pl.program_id(0)

    @pl.when(jnp.logical_or(i == 0, be_ref[i] != be_ref[jnp.maximum(i - 1, 0)]))
    def _():
        wg_bf[...] = wg_ref[0, 0].astype(BF16)
        wu_bf[...] = wu_ref[0, 0].astype(BF16)
        wd_bf[...] = wd_ref[0, 0].astype(BF16)

    @pl.when(i < nu_ref[0])
    def _():
        x = x_ref[...].astype(BF16)
        a = jnp.dot(x, wg_bf[...], preferred_element_type=F32)
        u = jnp.dot(x, wu_bf[...], preferred_element_type=F32)
        act = (a * jax.nn.sigmoid(a)) * u
        y_ref[...] = jnp.dot(act.astype(BF16), wd_bf[...], preferred_element_type=F32)

    @pl.when(i >= nu_ref[0])
    def _():
        y_ref[...] = jnp.zeros(y_ref.shape, F32)


def _experts(block_e, n_used, x_rows, w_gate, w_up, w_down, layer):
    n_rows, D = x_rows.shape
    de = w_gate.shape[3]
    n_blocks = n_rows // MOE_BLOCK
    grid_spec = pltpu.PrefetchScalarGridSpec(
        num_scalar_prefetch=2, grid=(n_blocks,),
        in_specs=[pl.BlockSpec((MOE_BLOCK, D), lambda i, be, nu: (jnp.minimum(i, nu[0] - 1), 0)),
                  pl.BlockSpec((1, 1, D, de), lambda i, be, nu: (layer, be[i], 0, 0)),
                  pl.BlockSpec((1, 1, D, de), lambda i, be, nu: (layer, be[i], 0, 0)),
                  pl.BlockSpec((1, 1, de, D), lambda i, be, nu: (layer, be[i], 0, 0))],
        out_specs=pl.BlockSpec((MOE_BLOCK, D), lambda i, be, nu: (i, 0)),
        scratch_shapes=[pltpu.VMEM((D, de), BF16), pltpu.VMEM((D, de), BF16), pltpu.VMEM((de, D), BF16)])
    return pl.pallas_call(
        _expert_kernel, out_shape=jax.ShapeDtypeStruct((n_rows, D), F32), grid_spec=grid_spec,
        compiler_params=pltpu.CompilerParams(dimension_semantics=("arbitrary",), vmem_limit_bytes=VMEM_LIMIT),
        name="experts",
    )(block_e, n_used, x_rows, w_gate, w_up, w_down)


def _combine_kernel(dc0_ref, dc1_ref, dn0_ref, dn1_ref, x_ref, route_ref, g_ref, y_ref, *rest,
                    tm, final_norm, t_first):
    *o_refs, ybuf, sems = rest
    i = pl.program_id(0)
    n = pl.num_programs(0)
    slot = i % 2
    groups = tm // SUBLANES
    D = x_ref.shape[1]

    def gather(d_refs, s):
        def body(k, carry):
            for u in range(SUBLANES):
                r = k * SUBLANES + u
                for a in range(2):
                    pltpu.make_async_copy(y_ref.at[pl.ds(d_refs[a][r], 1)], ybuf.at[s, a, k, pl.ds(u, 1)],
                                          sems.at[s]).start(priority=a)
            return carry
        lax.fori_loop(0, groups, body, 0)

    @pl.when(i == 0)
    def _():
        gather((dc0_ref, dc1_ref), 0)

    @pl.when(i + 1 < n)
    def _():
        gather((dn0_ref, dn1_ref), 1 - slot)

    for _ in range(2 * groups):
        pltpu.make_async_copy(y_ref.at[pl.ds(0, SUBLANES)], ybuf.at[slot, 0, 0], sems.at[slot]).wait()
    route = route_ref[...]
    g0 = route[:, ROUTE_G0:ROUTE_G0 + 1]
    g1 = route[:, ROUTE_G1:ROUTE_G1 + 1]
    out = x_ref[...] + (ybuf[slot, 0].reshape(tm, D) * g0 + ybuf[slot, 1].reshape(tm, D) * g1)
    if final_norm:
        out = _rms(out, g_ref[...])
    if t_first is None:
        o_refs[0][...] = out
    else:
        @pl.when(i < t_first)
        def _():
            o_refs[0][...] = out

        @pl.when(i >= t_first)
        def _():
            o_refs[1][...] = out


def _combine(dest_flat, x, route, g, y_rows, *, tm, final_norm, n_first=None):
    N, D = x.shape
    nt = N // tm
    row = lambda wdt: pl.BlockSpec((tm, wdt), lambda i: (i, 0))
    nxt = lambda i: jnp.minimum(i + 1, nt - 1)
    smem = lambda f: pl.BlockSpec((tm,), f, memory_space=pltpu.SMEM)
    if n_first is None:
        t_first, out_shape, out_specs = None, jax.ShapeDtypeStruct((N, D), F32), row(D)
    else:
        t_first = n_first // tm
        out_shape = (jax.ShapeDtypeStruct((n_first, D), F32), jax.ShapeDtypeStruct((N - n_first, D), F32))
        out_specs = (pl.BlockSpec((tm, D), lambda i: (jnp.minimum(i, t_first - 1), 0)),
                     pl.BlockSpec((tm, D), lambda i: (jnp.maximum(i - t_first, 0), 0)))
    return pl.pallas_call(
        functools.partial(_combine_kernel, tm=tm, final_norm=final_norm, t_first=t_first),
        out_shape=out_shape, grid=(nt,),
        in_specs=[smem(lambda i: (i,)), smem(lambda i: (i + nt,)),
                  smem(lambda i: (nxt(i),)), smem(lambda i: (nxt(i) + nt,)),
                  row(D), row(LANES), pl.BlockSpec((1, D), lambda i: (0, 0)),
                  pl.BlockSpec(memory_space=pl.ANY)],
        out_specs=out_specs,
        scratch_shapes=[pltpu.VMEM((2, 2, tm // SUBLANES, SUBLANES, D), F32), pltpu.SemaphoreType.DMA((2,))],
        compiler_params=pltpu.CompilerParams(dimension_semantics=("arbitrary",), vmem_limit_bytes=VMEM_LIMIT),
        name="combine",
    )(dest_flat, dest_flat, dest_flat, dest_flat, x, route, g, y_rows)


def _hier_moe(x, g_ffn, w_rg, w_re, w_gate, w_up, w_down, layer, g_final, *, final_norm, n_first=None):
    N, D = x.shape
    tm = 256
    wr = jnp.concatenate([w_rg, w_re, jnp.zeros((D, LANES - N_EXPERT_GROUPS - N_EXPERTS), F32)], axis=1).astype(BF16)
    g_ffn = g_ffn.reshape(1, D)
    route, counts = _router(x, g_ffn, wr, tm=tm)
    counts = counts[0, :N_EXPERTS].astype(I32)
    eid_t = jnp.transpose(route[:, ROUTE_E0:ROUTE_E1 + 1]).astype(I32)
    rank_t = jnp.transpose(route[:, ROUTE_R0:ROUTE_R1 + 1]).astype(I32)
    padded = (counts + MOE_BLOCK - 1) // MOE_BLOCK * MOE_BLOCK
    pad_end = jnp.cumsum(padded)
    pad_start = pad_end - padded
    expert_ids = jnp.arange(N_EXPERTS, dtype=I32)
    start_of = jnp.sum(jnp.where(eid_t[:, :, None] == expert_ids, pad_start, 0), axis=-1)
    dest_t = start_of + rank_t
    n_blocks = -(-(2 * N) // MOE_BLOCK) + N_EXPERTS
    n_rows = n_blocks * MOE_BLOCK
    n_used = (pad_end[-1] // MOE_BLOCK).astype(I32)
    block_start = jnp.minimum(jnp.arange(n_blocks, dtype=I32), n_used - 1) * MOE_BLOCK
    block_e = jnp.minimum(jnp.sum(block_start[:, None] >= pad_end[None, :], axis=1), N_EXPERTS - 1).astype(I32)
    pad_info = jnp.stack([pad_start + counts, padded - counts, jnp.broadcast_to(n_used, (N_EXPERTS,))])
    dest_flat = dest_t.reshape(-1)
    x_rows = _dispatch(dest_flat, pad_info, x, g_ffn, tm=tm, n_rows=n_rows)
    y_rows = _experts(block_e, n_used.reshape(1), x_rows, w_gate, w_up, w_down, layer)
    return _combine(dest_flat, x, route, g_final.reshape(1, D), y_rows, tm=tm, final_norm=final_norm,
                    n_first=n_first)


def _rope_tables(pos, rot_lanes):
    inv = 1.0 / (ROPE_THETA ** (jnp.arange(ROT_HALF, dtype=F32) / ROT_HALF))
    ang = pos.astype(F32)[:, None] * inv[None, :]
    cos, sin = jnp.cos(ang), jnp.sin(ang)
    lane = jnp.arange(LANES)
    r = lane % HEAD_DIM
    active = (lane < rot_lanes)
    first = active & (r < ROT_HALF)
    second = active & (r >= ROT_HALF) & (r < 2 * ROT_HALF)
    cos_l = cos[:, r % ROT_HALF]
    sin_l = sin[:, r % ROT_HALF]
    c = jnp.where((first | second)[None, :], cos_l, 1.0)
    sa = jnp.where(first[None, :], -sin_l, 0.0)
    sb = jnp.where(second[None, :], sin_l, 0.0)
    return jnp.stack([c, sa, sb]).astype(F32)


def _prep_w_in(w_in):
    D = w_in.shape[0]
    xp, q, k, v, qi, ki, wi = jnp.split(w_in, [512, 1024, 1280, 1536, 2048, 2112], axis=1)
    pad = jnp.zeros((D, IN_WIDTH_PAD - w_in.shape[1]), w_in.dtype)
    return jnp.concatenate([xp, q, k, v, qi, ki, wi, pad], axis=1).astype(BF16)


def kernel(x_prompt, x_sample, cache_k, cache_v, cache_idx_k, state_pool, norm_mix, norm_ffn, norm_final,
           par_w_in, par_pool_w, par_pool_scale, par_w_out, gm_w_in, gm_ln_g, gm_ln_b, gm_ws, gm_bs, gm_w_out,
           moe_router_group, moe_router_expert, moe_w_gate, moe_w_up, moe_w_down):
    Bp, Tp, D = x_prompt.shape
    Bs, Ts, _ = x_sample.shape
    past = cache_k.shape[2]
    Np, Ns = Bp * Tp, Bs * Ts
    depth = norm_mix.shape[0]
    assert depth == 2 and Ts == CHUNK and Tp % 256 == 0 and Ns % 256 == 0

    w_in_bf = _prep_w_in(par_w_in[0])
    pw_bf = par_pool_w[0].astype(BF16)
    ps = par_pool_scale[0].reshape(1, POOL_WIDTH)
    wo_bf = par_w_out[0].astype(BF16)
    g_mix0 = norm_mix[0].reshape(1, D)
    pos_p = jnp.arange(Tp, dtype=I32)
    pos_s = past + jnp.arange(Ts, dtype=I32)

    hist_p = jnp.zeros((Bp, HIST_ROWS, POOL_WIDTH), F32)
    hist_s = jnp.pad(state_pool[0], ((0, 0), (1, 0), (0, 0)))
    tk = 512
    (q_p, qi_p, k_p, v_p, kiwi_p, kbf_p, vbf_p, kibf_p, yp_p, st_p) = _inproj(
        x_prompt, g_mix0, w_in_bf, _rope_tables(pos_p, LANES), _rope_tables(pos_p, IDX_DIM), hist_p, pw_bf, ps,
        tm=256, pos0=0)
    (q_s, qi_s, k_s, v_s, kiwi_s, kbf_s, vbf_s, kibf_s, yp_s, st_s) = _inproj(
        x_sample, g_mix0, w_in_bf, _rope_tables(pos_s, LANES), _rope_tables(pos_s, IDX_DIM), hist_s, pw_bf, ps,
        tm=Ts, pos0=past)

    assert Tp % tk == 0
    x1_p = _dsa(q_p, qi_p, kiwi_p, kbf_p, vbf_p, kibf_p, x_prompt, yp_p, wo_bf,
                tk=tk, pos0=0, n_keys=Tp, n_sel=min(TOPK_MAX, Tp // 4))
    ls = past + Ts
    lsp = -(-ls // tk) * tk
    kpad = lambda a, ax: jnp.pad(a, [(0, lsp - ls) if d == ax else (0, 0) for d in range(a.ndim)])
    kall = kpad(jnp.concatenate([jnp.transpose(cache_k[0], (0, 2, 1, 3)).astype(BF16), kbf_s], axis=2), 2)
    vall = kpad(jnp.concatenate([cache_v[0].reshape(Bs, past, KV_WIDTH).astype(BF16), vbf_s], axis=1), 1)
    vall_t = jnp.transpose(vall.reshape(Bs, lsp // LANES, LANES, KV_WIDTH), (0, 1, 3, 2))
    kiall = kpad(jnp.concatenate([cache_idx_k[0].astype(BF16), kibf_s], axis=1), 1)
    qpad = lambda a, ax: jnp.pad(a, [(0, LANES - Ts) if d == ax else (0, 0) for d in range(a.ndim)])
    x1_s = _dsa(qpad(q_s, 2), qpad(qi_s, 2), qpad(kiwi_s, 1), kall, vall_t, kiall, qpad(x_sample, 1), qpad(yp_s, 1),
                wo_bf, tk=tk, pos0=past, n_keys=ls, n_sel=min(TOPK_MAX, ls // 4))[:, :Ts]

    x = jnp.concatenate([x1_p.reshape(Np, D), x1_s.reshape(Ns, D)], axis=0)
    x = _hier_moe(x, norm_ffn[0], moe_router_group[0], moe_router_expert[0],
                  moe_w_gate, moe_w_up, moe_w_down, 0, norm_final, final_norm=False)

    cs = Ts
    tril = lambda n: jnp.tril(jnp.ones((n, n), bool))
    ws_p = jnp.where(tril(GM_CHUNK)[None], gm_ws[0], 0.0)
    ws_small = jnp.where(tril(cs)[None], gm_ws[0][:, :cs, :cs], 0.0)
    rep = GM_CHUNK // cs
    ws_s = jnp.einsum('ab,gts->gatbs', jnp.eye(rep, dtype=F32), ws_small).reshape(GM_GROUPS, GM_CHUNK, GM_CHUNK)
    ws2 = jnp.stack([ws_p, ws_s]).astype(BF16)
    gd = D // GM_GROUPS
    bias_p = jnp.repeat(jnp.transpose(gm_bs[0]), gd, axis=1)
    bias_s = jnp.tile(jnp.repeat(jnp.transpose(gm_bs[0][:, :cs]), gd, axis=1), (rep, 1))
    bias2 = jnp.stack([bias_p, bias_s])
    x, gm_v = _gmlp(x, norm_mix[1].reshape(1, D), gm_w_in[0].astype(BF16), gm_ln_g[0].reshape(1, D),
                    gm_ln_b[0].reshape(1, D), ws2, bias2, gm_w_out[0].astype(BF16),
                    tm=256, n_first=Np, n_v_rows=Ns)
    y_p, y_s = _hier_moe(x, norm_ffn[1], moe_router_group[1], moe_router_expert[1],
                         moe_w_gate, moe_w_up, moe_w_down, 1, norm_final, final_norm=True, n_first=Np)
    y_p = y_p.reshape(Bp, Tp, D)
    y_s = y_s.reshape(Bs, Ts, D)
    r4 = lambda a, b, t: a.reshape(1, b, t, N_KV_HEADS, HEAD_DIM)
    return (y_p, y_s,
            r4(k_p, Bp, Tp), r4(v_p, Bp, Tp), kiwi_p[:, :, :IDX_DIM][None], st_p[:, 1:][None],
            r4(k_s, Bs, Ts), r4(v_s, Bs, Ts), kiwi_s[:, :, :IDX_DIM][None], st_s[:, 1:][None],
            gm_v.reshape(1, Bs, Ts, D))
```

```python
import functools

import jax
import jax.numpy as jnp
from jax import lax
from jax.experimental import pallas as pl
from jax.experimental.pallas import tpu as pltpu

F32 = jnp.float32
BF16 = jnp.bfloat16
I32 = jnp.int32
U32 = jnp.uint32

LANES = 128
CHUNK = 64
POOL_WINDOWS = (2, 4, 8, 16)
POOL_GROUP_DIM = 128
POOL_WIDTH = 512
HIST_ROWS = 16
N_HEADS = 8
HEAD_DIM = 64
N_KV_HEADS = 4
Q_PER_KV = N_HEADS // N_KV_HEADS
ATT_WIDTH = N_HEADS * HEAD_DIM
KV_WIDTH = N_KV_HEADS * HEAD_DIM
N_IDX_HEADS = 8
IDX_DIM = 64
TOPK_MAX = 256
ROPE_THETA = 500000.0
ROT_HALF = HEAD_DIM // 8
GM_CHUNK = 128
GM_GROUPS = 8
N_EXPERT_GROUPS = 4
EXPERTS_PER_GROUP = 8
N_EXPERTS = 32
MOE_BLOCK = 256
RMS_EPS = 1e-6
LN_EPS = 1e-5

INT_MIN = -2147483648
LOG2_E = 1.4426950408889634
NEG_BIG = -1e30
VMEM_LIMIT = 48 * 1024 * 1024

TOKEN_TILE = 256
KEY_BLOCK = 512
COUNT_SLAB = 64

IN_SIZES = (POOL_WIDTH, ATT_WIDTH, KV_WIDTH, KV_WIDTH, N_IDX_HEADS * IDX_DIM, IDX_DIM, N_IDX_HEADS)
COL_XP, COL_Q, COL_K, COL_V, COL_QI, COL_KIWI = (sum(IN_SIZES[:i]) for i in range(6))
IN_WIDTH_PAD = COL_KIWI + LANES


def _rms(x, g):
    return x * lax.rsqrt(jnp.mean(x * x, axis=-1, keepdims=True) + RMS_EPS) * g


def _rope128(x, c, sa, sb):
    return x * c + pltpu.roll(x, LANES - ROT_HALF, 1) * sa + pltpu.roll(x, ROT_HALF, 1) * sb


def _inproj_kernel(x_ref, g_ref, w_ref, rope_ref, ropeki_ref, hist_ref, pw_ref, ps_ref,
                   q_ref, qi_ref, k_ref, v_ref, kiwi_ref, kbf_ref, vbf_ref, kibf_ref, yp_ref, state_ref,
                   buf_ref, *, tm, pos0):
    j = pl.program_id(1)
    h = _rms(x_ref[0], g_ref[...])
    proj = jnp.dot(h.astype(BF16), w_ref[...], preferred_element_type=F32)
    c, sa, sb = rope_ref[0], rope_ref[1], rope_ref[2]

    def put_heads(ref, i, chunk):
        ref[0, 2 * i] = chunk[:, :HEAD_DIM].astype(BF16)
        ref[0, 2 * i + 1] = chunk[:, HEAD_DIM:].astype(BF16)

    for i in range(ATT_WIDTH // LANES):
        put_heads(q_ref, i, _rope128(proj[:, COL_Q + i * LANES:COL_Q + (i + 1) * LANES], c, sa, sb)
                  * (HEAD_DIM ** -0.5 * LOG2_E))
        put_heads(qi_ref, i, _rope128(proj[:, COL_QI + i * LANES:COL_QI + (i + 1) * LANES], c, sa, sb))
    for i in range(KV_WIDTH // LANES):
        sl = slice(i * LANES, (i + 1) * LANES)
        kr = _rope128(proj[:, COL_K + i * LANES:COL_K + (i + 1) * LANES], c, sa, sb)
        k_ref[0, :, sl] = kr
        put_heads(kbf_ref, i, kr)
    vv = proj[:, COL_V:COL_V + KV_WIDTH]
    v_ref[0] = vv
    if tm % LANES == 0:
        for cc in range(tm // LANES):
            vbf_ref[0, cc] = jnp.transpose(vv[cc * LANES:(cc + 1) * LANES, :]).astype(BF16)
    else:
        vbf_ref[0] = vv.astype(BF16)
    kiwi = _rope128(proj[:, COL_KIWI:COL_KIWI + LANES], ropeki_ref[0], ropeki_ref[1], ropeki_ref[2])
    kiwi_ref[0] = kiwi
    kibf_ref[0] = kiwi[:, :IDX_DIM].astype(BF16)

    @pl.when(j == 0)
    def _():
        buf_ref[0:HIST_ROWS, :] = hist_ref[0]

    xp = proj[:, COL_XP:COL_XP + POOL_WIDTH]
    buf_ref[HIST_ROWS:HIST_ROWS + tm, :] = xp
    pos = pos0 + j * tm + lax.broadcasted_iota(I32, (tm, 1), 0)
    for gi, w in enumerate(POOL_WINDOWS):
        c0 = gi * POOL_GROUP_DIM
        s = xp[:, c0:c0 + POOL_GROUP_DIM]
        for i in range(1, w):
            s = s + buf_ref[HIST_ROWS - i:HIST_ROWS - i + tm, c0:c0 + POOL_GROUP_DIM]
        cnt = jnp.minimum(pos + 1, w).astype(F32)
        d = s / cnt - xp[:, c0:c0 + POOL_GROUP_DIM]
        y = jnp.dot(d.astype(BF16), pw_ref[gi], preferred_element_type=F32)
        yp_ref[0, :, c0:c0 + POOL_GROUP_DIM] = (y * ps_ref[:, c0:c0 + POOL_GROUP_DIM]).astype(BF16)
    tail = buf_ref[tm:tm + HIST_ROWS, :]
    state_ref[0] = tail
    buf_ref[0:HIST_ROWS, :] = tail


def _inproj(x, g, w_bf, rope, ropeki, hist, pw_bf, ps, *, tm, pos0):
    B, T, D = x.shape
    nt = T // tm
    f = lambda shape, dt: jax.ShapeDtypeStruct(shape, dt)
    v_t = tm % LANES == 0
    out_shape = (
        f((B, N_HEADS, T, HEAD_DIM), BF16), f((B, N_IDX_HEADS, T, IDX_DIM), BF16),
        f((B, T, KV_WIDTH), F32), f((B, T, KV_WIDTH), F32), f((B, T, LANES), F32),
        f((B, N_KV_HEADS, T, HEAD_DIM), BF16),
        f((B, T // LANES, KV_WIDTH, LANES) if v_t else (B, T, KV_WIDTH), BF16),
        f((B, T, IDX_DIM), BF16),
        f((B, T, POOL_WIDTH), BF16), f((B, HIST_ROWS, POOL_WIDTH), F32),
    )
    tile = lambda wdt: pl.BlockSpec((1, tm, wdt), lambda b, j: (b, j, 0))
    heads = lambda n, wdt: pl.BlockSpec((1, n, tm, wdt), lambda b, j: (b, 0, j, 0))
    const2 = lambda s: pl.BlockSpec(s, lambda b, j: (0, 0))
    in_specs = [
        tile(D), const2((1, D)), const2((D, IN_WIDTH_PAD)),
        pl.BlockSpec((3, tm, LANES), lambda b, j: (0, j, 0)),
        pl.BlockSpec((3, tm, LANES), lambda b, j: (0, j, 0)),
        pl.BlockSpec((1, HIST_ROWS, POOL_WIDTH), lambda b, j: (b, 0, 0)),
        pl.BlockSpec((len(POOL_WINDOWS), POOL_GROUP_DIM, POOL_GROUP_DIM), lambda b, j: (0, 0, 0)),
        const2((1, POOL_WIDTH)),
    ]
    out_specs = (
        heads(N_HEADS, HEAD_DIM), heads(N_IDX_HEADS, IDX_DIM), tile(KV_WIDTH), tile(KV_WIDTH), tile(LANES),
        heads(N_KV_HEADS, HEAD_DIM),
        pl.BlockSpec((1, tm // LANES, KV_WIDTH, LANES), lambda b, j: (b, j, 0, 0)) if v_t else tile(KV_WIDTH),
        tile(IDX_DIM), tile(POOL_WIDTH),
        pl.BlockSpec((1, HIST_ROWS, POOL_WIDTH), lambda b, j: (b, 0, 0)),
    )
    return pl.pallas_call(
        functools.partial(_inproj_kernel, tm=tm, pos0=pos0),
        out_shape=out_shape, grid=(B, nt), in_specs=in_specs, out_specs=out_specs,
        scratch_shapes=[pltpu.VMEM((HIST_ROWS + tm, POOL_WIDTH), F32)],
        compiler_params=pltpu.CompilerParams(dimension_semantics=("parallel", "arbitrary"),
                                             vmem_limit_bytes=VMEM_LIMIT),
        name="inproj",
    )(x, g, w_bf, rope, ropeki, hist, pw_bf, ps)


def _dsa_kernel(q_ref, qi_ref, kiwi_ref, k_ref, vt_ref, ki_ref, x_ref, yp_ref, wo_ref, *rest,
                tk, pos0, n_keys, n_sel):
    o_ref, key_buf, bias_buf, m_scr, l_scr, acc_scr, s_scr = rest[-7:]
    tq = LANES
    slab = COUNT_SLAB
    j = pl.program_id(1)
    base = pos0 + j * tq
    pos = base + lax.broadcasted_iota(I32, (1, tq), 1)
    limit = jnp.minimum((pos // CHUNK + 1) * CHUNK, n_keys)
    limit_max = jnp.minimum(((base + tq - 1) // CHUNK + 1) * CHUNK, n_keys)
    nkb = (limit_max + tk - 1) // tk
    nt = (((1,), (1,)), ((), ()))

    wi_t = jnp.transpose(kiwi_ref[0])[IDX_DIM:IDX_DIM + N_IDX_HEADS, :]

    def score_block(kb, carry):
        off = pl.multiple_of(kb * tk, tk)
        kiblk = ki_ref[0, pl.ds(off, tk), :]
        idx = jnp.zeros((tk, tq), F32)
        for h in range(N_IDX_HEADS):
            sc = lax.dot_general(kiblk, qi_ref[0, h], nt, preferred_element_type=F32)
            idx = idx + jnp.maximum(sc, 0.0) * wi_t[h:h + 1, :]
        idx = jnp.where(idx == 0.0, 0.0, idx)
        bits = lax.bitcast_convert_type(idx, I32)
        key = bits ^ ((bits >> 31) & 0x7FFFFFFF)
        key_buf[kb] = jnp.where(lax.broadcasted_iota(I32, (tk, tq), 0) < limit - kb * tk, key, INT_MIN)
        return carry

    lax.fori_loop(0, nkb, score_block, 0)

    def col_sum(a):
        return jnp.sum(a, axis=0, keepdims=True)

    def count_ge(cand):
        def body(kb, acc):
            kblk = key_buf[kb]
            for c in range(tk // slab):
                acc = acc + jnp.where(kblk[c * slab:(c + 1) * slab] >= cand, 1.0, 0.0)
            return acc
        return col_sum(lax.fori_loop(0, nkb, body, jnp.zeros((slab, tq), F32)))

    kf = float(n_sel)
    t0 = jnp.where(count_ge(jnp.zeros((1, tq), I32)) >= kf, 0, INT_MIN).astype(I32)

    def bit_body(i, t):
        cand = t | lax.shift_left(jnp.int32(1), 30 - i)
        return jnp.where(count_ge(cand) >= kf, cand, t)

    t = lax.fori_loop(0, 31, bit_body, t0)

    def count_gt_ge(t):
        def body(kb, accs):
            a_gt, a_ge = accs
            kblk = key_buf[kb]
            for c in range(tk // slab):
                blk = kblk[c * slab:(c + 1) * slab]
                a_gt = a_gt + jnp.where(blk > t, 1.0, 0.0)
                a_ge = a_ge + jnp.where(blk >= t, 1.0, 0.0)
            return a_gt, a_ge
        z = jnp.zeros((slab, tq), F32)
        a_gt, a_ge = lax.fori_loop(0, nkb, body, (z, z))
        return col_sum(a_gt), col_sum(a_ge)

    cnt_gt, cnt_ge = count_gt_ge(t)
    need = kf - cnt_gt
    cnt_eq = cnt_ge - cnt_gt
    overfull = jnp.where(t != INT_MIN, cnt_eq - need, 0.0)
    slow = jnp.max(overfull) > 0.0

    @pl.when(jnp.logical_not(slow))
    def _():
        t_adm = jnp.maximum(t, INT_MIN + 1)

        def body(kb, carry):
            bias_buf[kb] = jnp.where(key_buf[kb] >= t_adm, 0.0, NEG_BIG)
            return carry
        lax.fori_loop(0, nkb, body, 0)

    @pl.when(slow)
    def _():
        tri = jnp.where(lax.broadcasted_iota(I32, (tk, tk), 1) <= lax.broadcasted_iota(I32, (tk, tk), 0),
                        1.0, 0.0).astype(BF16)

        def body(kb, seen):
            kblk = key_buf[kb]
            eq = jnp.where((kblk == t) & (kblk != INT_MIN), 1.0, 0.0)
            prefix = jnp.dot(tri, eq.astype(BF16), preferred_element_type=F32) + seen
            keep_tie = jnp.where(prefix <= need, eq, 0.0)
            sel = jnp.where(kblk > t, 1.0, keep_tie)
            bias_buf[kb] = jnp.where(sel > 0.0, 0.0, NEG_BIG)
            return seen + col_sum(eq)
        lax.fori_loop(0, nkb, body, jnp.zeros((1, tq), F32))

    m_scr[...] = jnp.full(m_scr.shape, NEG_BIG, F32)
    l_scr[...] = jnp.zeros(l_scr.shape, F32)
    acc_scr[...] = jnp.zeros(acc_scr.shape, F32)
    sub = LANES
    nsub = tk // sub

    def attn_block(kb, carry):
        for c in range(nsub):
            off = pl.multiple_of(kb * tk + c * sub, sub)
            for g in range(N_KV_HEADS):
                s_scr[c, g] = lax.dot_general(
                    k_ref[0, g, pl.ds(off, sub), :],
                    q_ref[0, Q_PER_KV * g:Q_PER_KV * (g + 1)].reshape(Q_PER_KV * tq, HEAD_DIM),
                    nt, preferred_element_type=F32)
        m = [m_scr[g] for g in range(N_KV_HEADS)]
        l = [l_scr[g] for g in range(N_KV_HEADS)]
        for c in range(nsub):
            bias = bias_buf[kb, c * sub:(c + 1) * sub, :]
            bias2 = jnp.concatenate([bias] * Q_PER_KV, axis=1)
            for g in range(N_KV_HEADS):
                s = s_scr[c, g] + bias2
                m_new = jnp.maximum(m[g], jnp.max(s, axis=0, keepdims=True))
                alpha = jnp.exp2(m[g] - m_new)
                p = jnp.exp2(s - m_new)
                l[g] = alpha * l[g] + col_sum(p)
                pv = jnp.dot(vt_ref[0, kb * nsub + c, g * HEAD_DIM:(g + 1) * HEAD_DIM, :], p.astype(BF16),
                             preferred_element_type=F32)
                acc_scr[g] = alpha * acc_scr[g] + pv
                m[g] = m_new
        for g in range(N_KV_HEADS):
            m_scr[g] = m[g]
            l_scr[g] = l[g]
        return carry

    lax.fori_loop(0, nkb, attn_block, 0)

    o_t = []
    for g in range(N_KV_HEADS):
        og = acc_scr[g] / l_scr[g]
        o_t.extend(og[:, hh * tq:(hh + 1) * tq] for hh in range(Q_PER_KV))
    y_att = jnp.transpose(jnp.concatenate(o_t, axis=0)).astype(BF16)
    y = jnp.dot(yp_ref[0], wo_ref[0:POOL_WIDTH, :], preferred_element_type=F32)
    y = y + jnp.dot(y_att, wo_ref[POOL_WIDTH:POOL_WIDTH + ATT_WIDTH, :], preferred_element_type=F32)
    o_ref[...] = (x_ref[0] + y).reshape(o_ref.shape)


def _dsa(q_hm, qi_hm, kiwi, k_hm, v_t, ki_all, x, yp, wo_bf, *, tk, pos0, n_keys, n_sel, out_base=None):
    B, T, D = x.shape
    L = k_hm.shape[2]
    tq = LANES
    assert L % tk == 0 and T % tq == 0
    nkb_max = L // tk
    tile = lambda wdt: pl.BlockSpec((1, tq, wdt), lambda b, j: (b, j, 0))
    in_specs = [pl.BlockSpec((1, N_HEADS, tq, HEAD_DIM), lambda b, j: (b, 0, j, 0)),
                pl.BlockSpec((1, N_IDX_HEADS, tq, IDX_DIM), lambda b, j: (b, 0, j, 0)),
                tile(LANES),
                pl.BlockSpec((1, N_KV_HEADS, L, HEAD_DIM), lambda b, j: (b, 0, 0, 0)),
                pl.BlockSpec((1, L // LANES, KV_WIDTH, LANES), lambda b, j: (b, 0, 0, 0)),
                pl.BlockSpec((1, L, IDX_DIM), lambda b, j: (b, 0, 0)),
                tile(D), tile(POOL_WIDTH), pl.BlockSpec((POOL_WIDTH + ATT_WIDTH, D), lambda b, j: (0, 0))]
    nq = T // tq
    args = (q_hm, qi_hm, kiwi, k_hm, v_t, ki_all, x, yp, wo_bf)
    if out_base is None:
        out_shape, out_spec, aliases = jax.ShapeDtypeStruct((B, T, D), F32), tile(D), {}
    else:
        out_shape = jax.ShapeDtypeStruct(out_base.shape, F32)
        out_spec = pl.BlockSpec((tq, D), lambda b, j: (b * nq + j, 0))
        in_specs, args, aliases = in_specs + [pl.BlockSpec(memory_space=pl.ANY)], args + (out_base,), {len(args): 0}
    return pl.pallas_call(
        functools.partial(_dsa_kernel, tk=tk, pos0=pos0, n_keys=n_keys, n_sel=n_sel),
        out_shape=out_shape, grid=(B, nq), in_specs=in_specs, out_specs=out_spec, input_output_aliases=aliases,
        scratch_shapes=[pltpu.VMEM((nkb_max, tk, tq), I32), pltpu.VMEM((nkb_max, tk, tq), F32),
                        pltpu.VMEM((N_KV_HEADS, 1, Q_PER_KV * tq), F32),
                        pltpu.VMEM((N_KV_HEADS, 1, Q_PER_KV * tq), F32),
                        pltpu.VMEM((N_KV_HEADS, HEAD_DIM, Q_PER_KV * tq), F32),
                        pltpu.VMEM((tk // LANES, N_KV_HEADS, LANES, Q_PER_KV * tq), F32)],
        compiler_params=pltpu.CompilerParams(dimension_semantics=("parallel", "arbitrary"),
                                             vmem_limit_bytes=VMEM_LIMIT),
        name="dsa",
    )(*args)


def _gmlp_kernel(x_ref, g_ref, win_ref, lng_ref, lnb_ref, ws_ref, bias_ref, wout_ref, o_ref, v_ref, *, tm):
    x = x_ref[...]
    h = _rms(x, g_ref[...])
    z = jax.nn.gelu(jnp.dot(h.astype(BF16), win_ref[...], preferred_element_type=F32))
    half = z.shape[1] // 2
    u, v = z[:, :half], z[:, half:]
    mu = jnp.mean(v, axis=-1, keepdims=True)
    var = jnp.mean(jnp.square(v - mu), axis=-1, keepdims=True)
    vn = (v - mu) * lax.rsqrt(var + LN_EPS) * lng_ref[...] + lnb_ref[...]
    v_ref[...] = vn
    gd = half // GM_GROUPS
    gated = []
    for c in range(tm // GM_CHUNK):
        rows = slice(c * GM_CHUNK, (c + 1) * GM_CHUNK)
        vc = vn[rows].astype(BF16)
        mixed = jnp.concatenate(
            [jnp.dot(ws_ref[0, g], vc[:, g * gd:(g + 1) * gd], preferred_element_type=F32)
             for g in range(GM_GROUPS)], axis=1) + bias_ref[0]
        gated.append((u[rows] * mixed).astype(BF16))
    gated = jnp.concatenate(gated, axis=0)
    o_ref[...] = x + jnp.dot(gated, wout_ref[...], preferred_element_type=F32)


def _gmlp(x, g, win_bf, lng, lnb, ws2, bias2, wout_bf, *, tm, n_first, n_v_rows):
    N, D = x.shape
    half = win_bf.shape[1] // 2
    nt = N // tm
    t_first = n_first // tm
    variant = lambda i: jnp.where(i >= t_first, 1, 0)
    row = pl.BlockSpec((tm, D), lambda i: (i, 0))
    const = lambda s: pl.BlockSpec(s, lambda i: (0, 0))
    in_specs = [row, const((1, D)), const((D, 2 * half)), const((1, half)), const((1, half)),
                pl.BlockSpec((1, GM_GROUPS, GM_CHUNK, GM_CHUNK), lambda i: (variant(i), 0, 0, 0)),
                pl.BlockSpec((1, GM_CHUNK, half), lambda i: (variant(i), 0, 0)),
                const((half, D))]
    out_specs = (row, pl.BlockSpec((tm, half), lambda i: (jnp.maximum(i - t_first, 0), 0)))
    return pl.pallas_call(
        functools.partial(_gmlp_kernel, tm=tm),
        out_shape=(jax.ShapeDtypeStruct((N, D), F32), jax.ShapeDtypeStruct((n_v_rows, half), F32)),
        grid=(nt,), in_specs=in_specs, out_specs=out_specs,
        compiler_params=pltpu.CompilerParams(dimension_semantics=("arbitrary",), vmem_limit_bytes=VMEM_LIMIT),
        name="gmlp",
    )(x, g, win_bf, lng, lnb, ws2, bias2, wout_bf)


ROUTE_E0, ROUTE_E1, ROUTE_R0, ROUTE_R1, ROUTE_G0, ROUTE_G1 = range(6)


def _router_kernel(x_ref, g_ref, wr_ref, route_ref, cnt_ref, carry_ref, *, tm):
    i = pl.program_id(0)

    @pl.when(i == 0)
    def _():
        carry_ref[...] = jnp.zeros(carry_ref.shape, F32)

    h = _rms(x_ref[...], g_ref[...])
    logits = jnp.dot(h.astype(BF16), wr_ref[...], preferred_element_type=F32)
    lane = lax.broadcasted_iota(I32, (tm, LANES), 1).astype(F32)
    ninf = -jnp.inf
    big = float(LANES)
    rmax = lambda a: jnp.max(a, axis=1, keepdims=True)
    rmin = lambda a: jnp.min(a, axis=1, keepdims=True)
    rsum = lambda a: jnp.sum(a, axis=1, keepdims=True)

    is_grp = lane < N_EXPERT_GROUPS
    lg = jnp.where(is_grp, logits, ninf)
    mg = rmax(lg)
    g_sel = rmin(jnp.where(lg == mg, lane, big))
    p_grp = 1.0 / rsum(jnp.where(is_grp, jnp.exp(lg - mg), 0.0))
    lo = N_EXPERT_GROUPS + g_sel * EXPERTS_PER_GROUP
    le = jnp.where((lane >= lo) & (lane < lo + EXPERTS_PER_GROUP), logits, ninf)
    v1 = rmax(le)
    j1 = rmin(jnp.where(le == v1, lane, big))
    le2 = jnp.where(lane == j1, ninf, le)
    v2 = rmax(le2)
    j2 = rmin(jnp.where(le2 == v2, lane, big))
    e0 = j1 - N_EXPERT_GROUPS
    e1 = j2 - N_EXPERT_GROUPS
    r = jnp.exp(v2 - v1)
    g0 = p_grp / (1.0 + r)
    g1 = p_grp * r / (1.0 + r)

    oh0 = jnp.where(lane == e0, 1.0, 0.0)
    oh1 = jnp.where(lane == e1, 1.0, 0.0)
    oh = oh0 + oh1
    lower = jnp.where(lax.broadcasted_iota(I32, (tm, tm), 1) < lax.broadcasted_iota(I32, (tm, tm), 0),
                      1.0, 0.0).astype(BF16)
    before = jnp.dot(lower, oh.astype(BF16), preferred_element_type=F32) + carry_ref[...]
    r0 = rsum(oh0 * before)
    r1 = rsum(oh1 * before)
    carry_ref[...] = carry_ref[...] + jnp.sum(oh, axis=0, keepdims=True)
    cnt_ref[...] = carry_ref[...]

    route = jnp.zeros((tm, LANES), F32)
    for col, val in ((ROUTE_E0, e0), (ROUTE_E1, e1), (ROUTE_R0, r0), (ROUTE_R1, r1), (ROUTE_G0, g0), (ROUTE_G1, g1)):
        route = jnp.where(lane == col, val, route)
    route_ref[...] = route


def _router(x, g, wr, *, tm):
    N, D = x.shape
    row = lambda wdt: pl.BlockSpec((tm, wdt), lambda i: (i, 0))
    const = lambda s: pl.BlockSpec(s, lambda i: (0, 0))
    return pl.pallas_call(
        functools.partial(_router_kernel, tm=tm),
        out_shape=(jax.ShapeDtypeStruct((N, LANES), F32), jax.ShapeDtypeStruct((1, LANES), F32)),
        grid=(N // tm,), in_specs=[row(D), const((1, D)), const((D, LANES))],
        out_specs=(row(LANES), const((1, LANES))),
        scratch_shapes=[pltpu.VMEM((1, LANES), F32)],
        compiler_params=pltpu.CompilerParams(dimension_semantics=("arbitrary",), vmem_limit_bytes=VMEM_LIMIT),
        name="router",
    )(x, g, wr)


SUBLANES = 8


def _pack_bf16_pairs(h):
    half = h.shape[1] // 2
    lo = lax.bitcast_convert_type(h[:, :half].astype(BF16).astype(F32), U32)
    hi = lax.bitcast_convert_type(h[:, half:].astype(BF16).astype(F32), U32)
    return (lo >> 16) | (hi & jnp.uint32(0xFFFF0000))


def _unpack_bf16_pairs(w):
    lo = lax.bitcast_convert_type(w << 16, F32).astype(BF16)
    hi = lax.bitcast_convert_type(w & jnp.uint32(0xFFFF0000), F32).astype(BF16)
    return jnp.concatenate([lo, hi], axis=1)


def _dispatch_kernel(d0_ref, d1_ref, pad_ref, x_ref, g_ref, rows_ref, h_scr, zblk, sems, *, tm):
    i = pl.program_id(0)
    n = pl.num_programs(0)
    slot = i % 2
    groups = tm // SUBLANES

    def wait_slot(s):
        for _ in range(2 * groups):
            pltpu.make_async_copy(h_scr.at[s, 0], rows_ref.at[pl.ds(0, SUBLANES)], sems.at[s]).wait()

    @pl.when(i >= 2)
    def _():
        wait_slot(slot)

    h_scr[slot] = _pack_bf16_pairs(_rms(x_ref[...], g_ref[...])).reshape(groups, SUBLANES, h_scr.shape[3])

    def body(k, carry):
        for u in range(SUBLANES):
            r = k * SUBLANES + u
            src = h_scr.at[slot, k, pl.ds(u, 1)]
            pltpu.make_async_copy(src, rows_ref.at[pl.ds(d0_ref[r], 1)], sems.at[slot]).start()
            pltpu.make_async_copy(src, rows_ref.at[pl.ds(d1_ref[r], 1)], sems.at[slot]).start(priority=1)
        return carry

    lax.fori_loop(0, groups, body, 0)

    @pl.when(i == n - 1)
    def _():
        wait_slot(slot)

    @pl.when(jnp.logical_and(i == n - 1, n >= 2))
    def _():
        wait_slot(1 - slot)

    @pl.when(i == n - 1)
    def _():
        zblk[...] = jnp.zeros(zblk.shape, U32)
        zrow = zblk.at[pl.ds(0, 1)]
        for e in range(N_EXPERTS):
            first, count = pad_ref[0, e], pad_ref[1, e]

            def start(k, carry):
                pltpu.make_async_copy(zrow, rows_ref.at[pl.ds(first + k, 1)], sems.at[2]).start()
                return carry

            def wait(k, carry):
                pltpu.make_async_copy(zrow, rows_ref.at[pl.ds(first, 1)], sems.at[2]).wait()
                return carry

            lax.fori_loop(0, count, start, 0)
            lax.fori_loop(0, count, wait, 0)

        def blk_copy(b):
            return pltpu.make_async_copy(zblk, rows_ref.at[pl.ds(b * MOE_BLOCK, MOE_BLOCK)], sems.at[2])

        n_blocks = rows_ref.shape[0] // MOE_BLOCK
        n_used = pad_ref[2, 0]
        lax.fori_loop(n_used, n_blocks, lambda b, c: (blk_copy(b).start(), c)[1], 0)
        lax.fori_loop(n_used, n_blocks, lambda b, c: (blk_copy(n_used).wait(), c)[1], 0)


def _dispatch(dest_flat, pad_info, x, g, *, tm, n_rows):
    N, D = x.shape
    nt = N // tm
    return pl.pallas_call(
        functools.partial(_dispatch_kernel, tm=tm),
        out_shape=jax.ShapeDtypeStruct((n_rows, D // 2), U32), grid=(nt,),
        in_specs=[pl.BlockSpec((tm,), lambda i: (i,), memory_space=pltpu.SMEM),
                  pl.BlockSpec((tm,), lambda i: (i + nt,), memory_space=pltpu.SMEM),
                  pl.BlockSpec(memory_space=pltpu.SMEM),
                  pl.BlockSpec((tm, D), lambda i: (i, 0)),
                  pl.BlockSpec((1, D), lambda i: (0, 0))],
        out_specs=pl.BlockSpec(memory_space=pl.ANY),
        scratch_shapes=[pltpu.VMEM((2, tm // SUBLANES, SUBLANES, D // 2), U32), pltpu.VMEM((MOE_BLOCK, D // 2), U32),
                        pltpu.SemaphoreType.DMA((3,))],
        compiler_params=pltpu.CompilerParams(dimension_semantics=("arbitrary",), vmem_limit_bytes=VMEM_LIMIT),
        name="dispatch",
    )(dest_flat, dest_flat, pad_info, x, g)


def _expert_kernel(be_ref, nu_ref, x_ref, wg_ref, wu_ref, wd_ref, y_ref, wg_bf, wu_bf, wd_bf):
    i = pl.program_id(0)

    @pl.when(jnp.logical_or(i == 0, be_ref[i] != be_ref[jnp.maximum(i - 1, 0)]))
    def _():
        wg_bf[...] = wg_ref[0, 0].astype(BF16)
        wu_bf[...] = wu_ref[0, 0].astype(BF16)
        wd_bf[...] = wd_ref[0, 0].astype(BF16)

    @pl.when(i < nu_ref[0])
    def _():
        x = _unpack_bf16_pairs(x_ref[...])
        a = jnp.dot(x, wg_bf[...], preferred_element_type=F32)
        u = jnp.dot(x, wu_bf[...], preferred_element_type=F32)
        act = (a * jax.nn.sigmoid(a)) * u
        y_ref[...] = jnp.dot(act.astype(BF16), wd_bf[...], preferred_element_type=F32)

    @pl.when(i >= nu_ref[0])
    def _():
        y_ref[...] = jnp.zeros(y_ref.shape, F32)


def _experts(block_e, n_used, x_rows, w_gate, w_up, w_down, layer):
    n_rows = x_rows.shape[0]
    D, de = w_gate.shape[2:]
    n_blocks = n_rows // MOE_BLOCK
    grid_spec = pltpu.PrefetchScalarGridSpec(
        num_scalar_prefetch=2, grid=(n_blocks,),
        in_specs=[pl.BlockSpec((MOE_BLOCK, D // 2), lambda i, be, nu: (jnp.minimum(i, nu[0] - 1), 0)),
                  pl.BlockSpec((1, 1, D, de), lambda i, be, nu: (layer, be[i], 0, 0)),
                  pl.BlockSpec((1, 1, D, de), lambda i, be, nu: (layer, be[i], 0, 0)),
                  pl.BlockSpec((1, 1, de, D), lambda i, be, nu: (layer, be[i], 0, 0))],
        out_specs=pl.BlockSpec((MOE_BLOCK, D), lambda i, be, nu: (i, 0)),
        scratch_shapes=[pltpu.VMEM((D, de), BF16), pltpu.VMEM((D, de), BF16), pltpu.VMEM((de, D), BF16)])
    return pl.pallas_call(
        _expert_kernel, out_shape=jax.ShapeDtypeStruct((n_rows, D), F32), grid_spec=grid_spec,
        compiler_params=pltpu.CompilerParams(dimension_semantics=("arbitrary",), vmem_limit_bytes=VMEM_LIMIT),
        name="experts",
    )(block_e, n_used, x_rows, w_gate, w_up, w_down)


def _combine_kernel(dc0_ref, dc1_ref, dn0_ref, dn1_ref, x_ref, route_ref, g_ref, y_ref, *rest,
                    tm, final_norm, t_first):
    *o_refs, ybuf, sems = rest
    i = pl.program_id(0)
    n = pl.num_programs(0)
    slot = i % 2
    groups = tm // SUBLANES
    D = x_ref.shape[1]

    def gather(d_refs, s):
        def body(k, carry):
            for u in range(SUBLANES):
                r = k * SUBLANES + u
                for a in range(2):
                    pltpu.make_async_copy(y_ref.at[pl.ds(d_refs[a][r], 1)], ybuf.at[s, a, k, pl.ds(u, 1)],
                                          sems.at[s]).start(priority=a)
            return carry
        lax.fori_loop(0, groups, body, 0)

    @pl.when(i == 0)
    def _():
        gather((dc0_ref, dc1_ref), 0)

    @pl.when(i + 1 < n)
    def _():
        gather((dn0_ref, dn1_ref), 1 - slot)

    for _ in range(2 * groups):
        pltpu.make_async_copy(y_ref.at[pl.ds(0, SUBLANES)], ybuf.at[slot, 0, 0], sems.at[slot]).wait()
    route = route_ref[...]
    g0 = route[:, ROUTE_G0:ROUTE_G0 + 1]
    g1 = route[:, ROUTE_G1:ROUTE_G1 + 1]
    out = x_ref[...] + (ybuf[slot, 0].reshape(tm, D) * g0 + ybuf[slot, 1].reshape(tm, D) * g1)
    if final_norm:
        out = _rms(out, g_ref[...])
    if t_first is None:
        o_refs[0][...] = out
    else:
        @pl.when(i < t_first)
        def _():
            o_refs[0][...] = out

        @pl.when(i >= t_first)
        def _():
            o_refs[1][...] = out


def _combine(dest_flat, x, route, g, y_rows, *, tm, final_norm, n_first=None):
    N, D = x.shape
    nt = N // tm
    row = lambda wdt: pl.BlockSpec((tm, wdt), lambda i: (i, 0))
    nxt = lambda i: jnp.minimum(i + 1, nt - 1)
    smem = lambda f: pl.BlockSpec((tm,), f, memory_space=pltpu.SMEM)
    if n_first is None:
        t_first, out_shape, out_specs = None, jax.ShapeDtypeStruct((N, D), F32), row(D)
    else:
        t_first = n_first // tm
        out_shape = (jax.ShapeDtypeStruct((n_first, D), F32), jax.ShapeDtypeStruct((N - n_first, D), F32))
        out_specs = (pl.BlockSpec((tm, D), lambda i: (jnp.minimum(i, t_first - 1), 0)),
                     pl.BlockSpec((tm, D), lambda i: (jnp.maximum(i - t_first, 0), 0)))
    return pl.pallas_call(
        functools.partial(_combine_kernel, tm=tm, final_norm=final_norm, t_first=t_first),
        out_shape=out_shape, grid=(nt,),
        in_specs=[smem(lambda i: (i,)), smem(lambda i: (i + nt,)),
                  smem(lambda i: (nxt(i),)), smem(lambda i: (nxt(i) + nt,)),
                  row(D), row(LANES), pl.BlockSpec((1, D), lambda i: (0, 0)),
                  pl.BlockSpec(memory_space=pl.ANY)],
        out_specs=out_specs,
        scratch_shapes=[pltpu.VMEM((2, 2, tm // SUBLANES, SUBLANES, D), F32), pltpu.SemaphoreType.DMA((2,))],
        compiler_params=pltpu.CompilerParams(dimension_semantics=("arbitrary",), vmem_limit_bytes=VMEM_LIMIT),
        name="combine",
    )(dest_flat, dest_flat, dest_flat, dest_flat, x, route, g, y_rows)


def _hier_moe(x, g_ffn, w_rg, w_re, w_gate, w_up, w_down, layer, g_final, *, final_norm, n_first=None):
    N, D = x.shape
    tm = TOKEN_TILE
    wr = jnp.concatenate([w_rg, w_re, jnp.zeros((D, LANES - N_EXPERT_GROUPS - N_EXPERTS), F32)], axis=1).astype(BF16)
    g_ffn = g_ffn.reshape(1, D)
    route, counts = _router(x, g_ffn, wr, tm=tm)
    counts = counts[0, :N_EXPERTS].astype(I32)
    eid_t = jnp.transpose(route[:, ROUTE_E0:ROUTE_E1 + 1]).astype(I32)
    rank_t = jnp.transpose(route[:, ROUTE_R0:ROUTE_R1 + 1]).astype(I32)
    padded = (counts + MOE_BLOCK - 1) // MOE_BLOCK * MOE_BLOCK
    pad_end = jnp.cumsum(padded)
    pad_start = pad_end - padded
    expert_ids = jnp.arange(N_EXPERTS, dtype=I32)
    start_of = jnp.sum(jnp.where(eid_t[:, :, None] == expert_ids, pad_start, 0), axis=-1)
    dest_t = start_of + rank_t
    n_blocks = -(-(2 * N) // MOE_BLOCK) + N_EXPERTS
    n_rows = n_blocks * MOE_BLOCK
    n_used = (pad_end[-1] // MOE_BLOCK).astype(I32)
    block_start = jnp.minimum(jnp.arange(n_blocks, dtype=I32), n_used - 1) * MOE_BLOCK
    block_e = jnp.minimum(jnp.sum(block_start[:, None] >= pad_end[None, :], axis=1), N_EXPERTS - 1).astype(I32)
    pad_info = jnp.stack([pad_start + counts, padded - counts, jnp.broadcast_to(n_used, (N_EXPERTS,))])
    dest_flat = dest_t.reshape(-1)
    x_rows = _dispatch(dest_flat, pad_info, x, g_ffn, tm=tm, n_rows=n_rows)
    y_rows = _experts(block_e, n_used.reshape(1), x_rows, w_gate, w_up, w_down, layer)
    return _combine(dest_flat, x, route, g_final.reshape(1, D), y_rows, tm=tm, final_norm=final_norm,
                    n_first=n_first)


def _rope_tables(pos, rot_lanes):
    inv = 1.0 / (ROPE_THETA ** (jnp.arange(ROT_HALF, dtype=F32) / ROT_HALF))
    ang = pos.astype(F32)[:, None] * inv[None, :]
    cos, sin = jnp.cos(ang), jnp.sin(ang)
    lane = jnp.arange(LANES)
    r = lane % HEAD_DIM
    active = (lane < rot_lanes)
    first = active & (r < ROT_HALF)
    second = active & (r >= ROT_HALF) & (r < 2 * ROT_HALF)
    cos_l = cos[:, r % ROT_HALF]
    sin_l = sin[:, r % ROT_HALF]
    c = jnp.where((first | second)[None, :], cos_l, 1.0)
    sa = jnp.where(first[None, :], -sin_l, 0.0)
    sb = jnp.where(second[None, :], sin_l, 0.0)
    return jnp.stack([c, sa, sb]).astype(F32)


def _prep_w_in(w_in):
    D = w_in.shape[0]
    assert w_in.shape[1] == sum(IN_SIZES)
    pad = jnp.zeros((D, IN_WIDTH_PAD - w_in.shape[1]), w_in.dtype)
    return jnp.concatenate([w_in, pad], axis=1).astype(BF16)


def kernel(x_prompt, x_sample, cache_k, cache_v, cache_idx_k, state_pool, norm_mix, norm_ffn, norm_final,
           par_w_in, par_pool_w, par_pool_scale, par_w_out, gm_w_in, gm_ln_g, gm_ln_b, gm_ws, gm_bs, gm_w_out,
           moe_router_group, moe_router_expert, moe_w_gate, moe_w_up, moe_w_down):
    Bp, Tp, D = x_prompt.shape
    Bs, Ts, _ = x_sample.shape
    past = cache_k.shape[2]
    Np, Ns = Bp * Tp, Bs * Ts
    depth = norm_mix.shape[0]
    assert depth == 2 and Ts == CHUNK and Tp % TOKEN_TILE == 0 and Ns % TOKEN_TILE == 0

    w_in_bf = _prep_w_in(par_w_in[0])
    pw_bf = par_pool_w[0].astype(BF16)
    ps = par_pool_scale[0].reshape(1, POOL_WIDTH)
    wo_bf = par_w_out[0].astype(BF16)
    g_mix0 = norm_mix[0].reshape(1, D)
    pos_p = jnp.arange(Tp, dtype=I32)
    pos_s = past + jnp.arange(Ts, dtype=I32)

    hist_p = jnp.zeros((Bp, HIST_ROWS, POOL_WIDTH), F32)
    hist_s = jnp.pad(state_pool[0], ((0, 0), (1, 0), (0, 0)))
    tk = KEY_BLOCK
    (q_p, qi_p, k_p, v_p, kiwi_p, kbf_p, vbf_p, kibf_p, yp_p, st_p) = _inproj(
        x_prompt, g_mix0, w_in_bf, _rope_tables(pos_p, LANES), _rope_tables(pos_p, IDX_DIM), hist_p, pw_bf, ps,
        tm=TOKEN_TILE, pos0=0)
    (q_s, qi_s, k_s, v_s, kiwi_s, kbf_s, vbf_s, kibf_s, yp_s, st_s) = _inproj(
        x_sample, g_mix0, w_in_bf, _rope_tables(pos_s, LANES), _rope_tables(pos_s, IDX_DIM), hist_s, pw_bf, ps,
        tm=Ts, pos0=past)

    ls = past + Ts
    lsp = -(-ls // tk) * tk
    kpad = lambda a, ax: jnp.pad(a, [(0, lsp - ls) if d == ax else (0, 0) for d in range(a.ndim)])
    kall = kpad(jnp.concatenate([jnp.transpose(cache_k[0], (0, 2, 1, 3)).astype(BF16), kbf_s], axis=2), 2)
    vall = kpad(jnp.concatenate([cache_v[0].reshape(Bs, past, KV_WIDTH).astype(BF16), vbf_s], axis=1), 1)
    vall_t = jnp.transpose(vall.reshape(Bs, lsp // LANES, LANES, KV_WIDTH), (0, 1, 3, 2))
    kiall = kpad(jnp.concatenate([cache_idx_k[0].astype(BF16), kibf_s], axis=1), 1)
    qpad = lambda a, ax: jnp.pad(a, [(0, LANES - Ts) if d == ax else (0, 0) for d in range(a.ndim)])
    x1_s = _dsa(qpad(q_s, 2), qpad(qi_s, 2), qpad(kiwi_s, 1), kall, vall_t, kiall, qpad(x_sample, 1), qpad(yp_s, 1),
                wo_bf, tk=tk, pos0=past, n_keys=ls, n_sel=min(TOPK_MAX, ls // 4))[:, :Ts]
    assert Tp % tk == 0
    x = jnp.pad(x1_s.reshape(Ns, D), ((Np, 0), (0, 0)))
    x = _dsa(q_p, qi_p, kiwi_p, kbf_p, vbf_p, kibf_p, x_prompt, yp_p, wo_bf,
             tk=tk, pos0=0, n_keys=Tp, n_sel=min(TOPK_MAX, Tp // 4), out_base=x)
    x = _hier_moe(x, norm_ffn[0], moe_router_group[0], moe_router_expert[0],
                  moe_w_gate, moe_w_up, moe_w_down, 0, norm_final, final_norm=False)

    cs = Ts
    tril = lambda n: jnp.tril(jnp.ones((n, n), bool))
    ws_p = jnp.where(tril(GM_CHUNK)[None], gm_ws[0], 0.0)
    ws_small = jnp.where(tril(cs)[None], gm_ws[0][:, :cs, :cs], 0.0)
    rep = GM_CHUNK // cs
    ws_s = jnp.einsum('ab,gts->gatbs', jnp.eye(rep, dtype=F32), ws_small).reshape(GM_GROUPS, GM_CHUNK, GM_CHUNK)
    ws2 = jnp.stack([ws_p, ws_s]).astype(BF16)
    gd = D // GM_GROUPS
    bias_p = jnp.repeat(jnp.transpose(gm_bs[0]), gd, axis=1)
    bias_s = jnp.tile(jnp.repeat(jnp.transpose(gm_bs[0][:, :cs]), gd, axis=1), (rep, 1))
    bias2 = jnp.stack([bias_p, bias_s])
    x, gm_v = _gmlp(x, norm_mix[1].reshape(1, D), gm_w_in[0].astype(BF16), gm_ln_g[0].reshape(1, D),
                    gm_ln_b[0].reshape(1, D), ws2, bias2, gm_w_out[0].astype(BF16),
                    tm=TOKEN_TILE, n_first=Np, n_v_rows=Ns)
    y_p, y_s = _hier_moe(x, norm_ffn[1], moe_router_group[1], moe_router_expert[1],
                         moe_w_gate, moe_w_up, moe_w_down, 1, norm_final, final_norm=True, n_first=Np)
    y_p = y_p.reshape(Bp, Tp, D)
    y_s = y_s.reshape(Bs, Ts, D)
    r4 = lambda a, b, t: a.reshape(1, b, t, N_KV_HEADS, HEAD_DIM)
    return (y_p, y_s,
            r4(k_p, Bp, Tp), r4(v_p, Bp, Tp), kiwi_p[:, :, :IDX_DIM][None], st_p[:, 1:][None],
            r4(k_s, Bs, Ts), r4(v_s, Bs, Ts), kiwi_s[:, :, :IDX_DIM][None], st_s[:, 1:][None],
            gm_v.reshape(1, Bs, Ts, D))
```

```python
import functools

import jax
import jax.numpy as jnp
from jax import lax
from jax.experimental import pallas as pl
from jax.experimental.pallas import tpu as pltpu

F32 = jnp.float32
BF16 = jnp.bfloat16
I32 = jnp.int32
U32 = jnp.uint32

LANES = 128
CHUNK = 64
POOL_WINDOWS = (2, 4, 8, 16)
POOL_GROUP_DIM = 128
POOL_WIDTH = 512
HIST_ROWS = 16
N_HEADS = 8
HEAD_DIM = 64
N_KV_HEADS = 4
Q_PER_KV = N_HEADS // N_KV_HEADS
ATT_WIDTH = N_HEADS * HEAD_DIM
KV_WIDTH = N_KV_HEADS * HEAD_DIM
N_IDX_HEADS = 8
IDX_DIM = 64
TOPK_MAX = 256
ROPE_THETA = 500000.0
ROT_HALF = HEAD_DIM // 8
GM_CHUNK = 128
GM_GROUPS = 8
N_EXPERT_GROUPS = 4
EXPERTS_PER_GROUP = 8
N_EXPERTS = 32
MOE_BLOCK = 256
RMS_EPS = 1e-6
LN_EPS = 1e-5

INT_MIN = -2147483648
LOG2_E = 1.4426950408889634
NEG_BIG = -1e30
VMEM_LIMIT = 48 * 1024 * 1024

TOKEN_TILE = 256
KEY_BLOCK = 512
COUNT_SLAB = 64

IN_SIZES = (POOL_WIDTH, ATT_WIDTH, KV_WIDTH, KV_WIDTH, N_IDX_HEADS * IDX_DIM, IDX_DIM, N_IDX_HEADS)
COL_XP, COL_Q, COL_K, COL_V, COL_QI, COL_KIWI = (sum(IN_SIZES[:i]) for i in range(6))
IN_WIDTH_PAD = COL_KIWI + LANES


def _rms(x, g):
    return x * lax.rsqrt(jnp.mean(x * x, axis=-1, keepdims=True) + RMS_EPS) * g


def _rope128(x, c, sa, sb):
    return x * c + pltpu.roll(x, LANES - ROT_HALF, 1) * sa + pltpu.roll(x, ROT_HALF, 1) * sb


def _inproj_kernel(x_ref, g_ref, w_ref, rope_ref, ropeki_ref, hist_ref, pw_ref, ps_ref,
                   q_ref, qi_ref, k_ref, v_ref, kiwi_ref, kbf_ref, vbf_ref, kibf_ref, yp_ref, state_ref,
                   buf_ref, *, tm, pos0):
    j = pl.program_id(1)
    h = _rms(x_ref[0], g_ref[...])
    proj = jnp.dot(h.astype(BF16), w_ref[...], preferred_element_type=F32)
    c, sa, sb = rope_ref[0], rope_ref[1], rope_ref[2]

    def put_heads(ref, i, chunk):
        ref[0, 2 * i] = chunk[:, :HEAD_DIM].astype(BF16)
        ref[0, 2 * i + 1] = chunk[:, HEAD_DIM:].astype(BF16)

    for i in range(ATT_WIDTH // LANES):
        put_heads(q_ref, i, _rope128(proj[:, COL_Q + i * LANES:COL_Q + (i + 1) * LANES], c, sa, sb)
                  * (HEAD_DIM ** -0.5 * LOG2_E))
        put_heads(qi_ref, i, _rope128(proj[:, COL_QI + i * LANES:COL_QI + (i + 1) * LANES], c, sa, sb))
    for i in range(KV_WIDTH // LANES):
        sl = slice(i * LANES, (i + 1) * LANES)
        kr = _rope128(proj[:, COL_K + i * LANES:COL_K + (i + 1) * LANES], c, sa, sb)
        k_ref[0, :, sl] = kr
        put_heads(kbf_ref, i, kr)
    vv = proj[:, COL_V:COL_V + KV_WIDTH]
    v_ref[0] = vv
    if tm % LANES == 0:
        for cc in range(tm // LANES):
            vbf_ref[0, cc] = jnp.transpose(vv[cc * LANES:(cc + 1) * LANES, :]).astype(BF16)
    else:
        vbf_ref[0] = vv.astype(BF16)
    kiwi = _rope128(proj[:, COL_KIWI:COL_KIWI + LANES], ropeki_ref[0], ropeki_ref[1], ropeki_ref[2])
    kiwi_ref[0] = kiwi
    kibf_ref[0] = kiwi[:, :IDX_DIM].astype(BF16)

    @pl.when(j == 0)
    def _():
        buf_ref[0:HIST_ROWS, :] = hist_ref[0]

    xp = proj[:, COL_XP:COL_XP + POOL_WIDTH]
    buf_ref[HIST_ROWS:HIST_ROWS + tm, :] = xp
    pos = pos0 + j * tm + lax.broadcasted_iota(I32, (tm, 1), 0)
    for gi, w in enumerate(POOL_WINDOWS):
        c0 = gi * POOL_GROUP_DIM
        s = xp[:, c0:c0 + POOL_GROUP_DIM]
        for i in range(1, w):
            s = s + buf_ref[HIST_ROWS - i:HIST_ROWS - i + tm, c0:c0 + POOL_GROUP_DIM]
        cnt = jnp.minimum(pos + 1, w).astype(F32)
        d = s / cnt - xp[:, c0:c0 + POOL_GROUP_DIM]
        y = jnp.dot(d.astype(BF16), pw_ref[gi], preferred_element_type=F32)
        yp_ref[0, :, c0:c0 + POOL_GROUP_DIM] = (y * ps_ref[:, c0:c0 + POOL_GROUP_DIM]).astype(BF16)
    tail = buf_ref[tm:tm + HIST_ROWS, :]
    state_ref[0] = tail
    buf_ref[0:HIST_ROWS, :] = tail


def _inproj(x, g, w_bf, rope, ropeki, hist, pw_bf, ps, *, tm, pos0):
    B, T, D = x.shape
    nt = T // tm
    f = lambda shape, dt: jax.ShapeDtypeStruct(shape, dt)
    v_t = tm % LANES == 0
    out_shape = (
        f((B, N_HEADS, T, HEAD_DIM), BF16), f((B, N_IDX_HEADS, T, IDX_DIM), BF16),
        f((B, T, KV_WIDTH), F32), f((B, T, KV_WIDTH), F32), f((B, T, LANES), F32),
        f((B, N_KV_HEADS, T, HEAD_DIM), BF16),
        f((B, T // LANES, KV_WIDTH, LANES) if v_t else (B, T, KV_WIDTH), BF16),
        f((B, T, IDX_DIM), BF16),
        f((B, T, POOL_WIDTH), BF16), f((B, HIST_ROWS, POOL_WIDTH), F32),
    )
    tile = lambda wdt: pl.BlockSpec((1, tm, wdt), lambda b, j: (b, j, 0))
    heads = lambda n, wdt: pl.BlockSpec((1, n, tm, wdt), lambda b, j: (b, 0, j, 0))
    const2 = lambda s: pl.BlockSpec(s, lambda b, j: (0, 0))
    in_specs = [
        tile(D), const2((1, D)), const2((D, IN_WIDTH_PAD)),
        pl.BlockSpec((3, tm, LANES), lambda b, j: (0, j, 0)),
        pl.BlockSpec((3, tm, LANES), lambda b, j: (0, j, 0)),
        pl.BlockSpec((1, HIST_ROWS, POOL_WIDTH), lambda b, j: (b, 0, 0)),
        pl.BlockSpec((len(POOL_WINDOWS), POOL_GROUP_DIM, POOL_GROUP_DIM), lambda b, j: (0, 0, 0)),
        const2((1, POOL_WIDTH)),
    ]
    out_specs = (
        heads(N_HEADS, HEAD_DIM), heads(N_IDX_HEADS, IDX_DIM), tile(KV_WIDTH), tile(KV_WIDTH), tile(LANES),
        heads(N_KV_HEADS, HEAD_DIM),
        pl.BlockSpec((1, tm // LANES, KV_WIDTH, LANES), lambda b, j: (b, j, 0, 0)) if v_t else tile(KV_WIDTH),
        tile(IDX_DIM), tile(POOL_WIDTH),
        pl.BlockSpec((1, HIST_ROWS, POOL_WIDTH), lambda b, j: (b, 0, 0)),
    )
    return pl.pallas_call(
        functools.partial(_inproj_kernel, tm=tm, pos0=pos0),
        out_shape=out_shape, grid=(B, nt), in_specs=in_specs, out_specs=out_specs,
        scratch_shapes=[pltpu.VMEM((HIST_ROWS + tm, POOL_WIDTH), F32)],
        compiler_params=pltpu.CompilerParams(dimension_semantics=("parallel", "arbitrary"),
                                             vmem_limit_bytes=VMEM_LIMIT),
        name="inproj",
    )(x, g, w_bf, rope, ropeki, hist, pw_bf, ps)


def _dsa_kernel(q_ref, qi_ref, kiwi_ref, k_ref, vt_ref, ki_ref, x_ref, yp_ref, wo_ref, *rest,
                tk, pos0, n_keys, n_sel):
    o_ref, key_buf, bias_buf, m_scr, l_scr, acc_scr, s_scr = rest[-7:]
    tq = LANES
    slab = COUNT_SLAB
    j = pl.program_id(1)
    base = pos0 + j * tq
    pos = base + lax.broadcasted_iota(I32, (1, tq), 1)
    limit = jnp.minimum((pos // CHUNK + 1) * CHUNK, n_keys)
    limit_max = jnp.minimum(((base + tq - 1) // CHUNK + 1) * CHUNK, n_keys)
    nkb = (limit_max + tk - 1) // tk
    nt = (((1,), (1,)), ((), ()))

    wi_t = jnp.transpose(kiwi_ref[0])[IDX_DIM:IDX_DIM + N_IDX_HEADS, :]

    def score_block(kb, carry):
        off = pl.multiple_of(kb * tk, tk)
        kiblk = ki_ref[0, pl.ds(off, tk), :]
        idx = jnp.zeros((tk, tq), F32)
        for h in range(N_IDX_HEADS):
            sc = lax.dot_general(kiblk, qi_ref[0, h], nt, preferred_element_type=F32)
            idx = idx + jnp.maximum(sc, 0.0) * wi_t[h:h + 1, :]
        idx = jnp.where(idx == 0.0, 0.0, idx)
        bits = lax.bitcast_convert_type(idx, I32)
        key = bits ^ ((bits >> 31) & 0x7FFFFFFF)
        key_buf[kb] = jnp.where(lax.broadcasted_iota(I32, (tk, tq), 0) < limit - kb * tk, key, INT_MIN)
        return carry

    lax.fori_loop(0, nkb, score_block, 0)

    def col_sum(a):
        return jnp.sum(a, axis=0, keepdims=True)

    def count_ge(cand):
        def body(kb, acc):
            kblk = key_buf[kb]
            for c in range(tk // slab):
                acc = acc + jnp.where(kblk[c * slab:(c + 1) * slab] >= cand, 1.0, 0.0)
            return acc
        return col_sum(lax.fori_loop(0, nkb, body, jnp.zeros((slab, tq), F32)))

    kf = float(n_sel)
    cnt0 = count_ge(jnp.zeros((1, tq), I32))
    t0 = jnp.where(cnt0 >= kf, 0, INT_MIN).astype(I32)

    def bit_body(i, carry):
        t, cnt = carry
        cand = t | lax.shift_left(jnp.int32(1), 30 - i)
        cnt_cand = count_ge(cand)
        keep = cnt_cand >= kf
        return jnp.where(keep, cand, t), jnp.where(keep, cnt_cand, cnt)

    t, cnt_ge = lax.fori_loop(0, 31, bit_body, (t0, cnt0))
    cnt_gt = count_ge(t + 1)
    need = kf - cnt_gt
    cnt_eq = cnt_ge - cnt_gt
    overfull = jnp.where(t != INT_MIN, cnt_eq - need, 0.0)
    slow = jnp.max(overfull) > 0.0

    @pl.when(jnp.logical_not(slow))
    def _():
        t_adm = jnp.maximum(t, INT_MIN + 1)

        def body(kb, carry):
            bias_buf[kb] = jnp.where(key_buf[kb] >= t_adm, 0.0, NEG_BIG)
            return carry
        lax.fori_loop(0, nkb, body, 0)

    @pl.when(slow)
    def _():
        tri = jnp.where(lax.broadcasted_iota(I32, (tk, tk), 1) <= lax.broadcasted_iota(I32, (tk, tk), 0),
                        1.0, 0.0).astype(BF16)

        def body(kb, seen):
            kblk = key_buf[kb]
            eq = jnp.where((kblk == t) & (kblk != INT_MIN), 1.0, 0.0)
            prefix = jnp.dot(tri, eq.astype(BF16), preferred_element_type=F32) + seen
            keep_tie = jnp.where(prefix <= need, eq, 0.0)
            sel = jnp.where(kblk > t, 1.0, keep_tie)
            bias_buf[kb] = jnp.where(sel > 0.0, 0.0, NEG_BIG)
            return seen + col_sum(eq)
        lax.fori_loop(0, nkb, body, jnp.zeros((1, tq), F32))

    m_scr[...] = jnp.full(m_scr.shape, NEG_BIG, F32)
    l_scr[...] = jnp.zeros(l_scr.shape, F32)
    acc_scr[...] = jnp.zeros(acc_scr.shape, F32)
    sub = LANES
    nsub = tk // sub

    def attn_block(kb, carry):
        for c in range(nsub):
            off = pl.multiple_of(kb * tk + c * sub, sub)
            for g in range(N_KV_HEADS):
                s_scr[c, g] = lax.dot_general(
                    k_ref[0, g, pl.ds(off, sub), :],
                    q_ref[0, Q_PER_KV * g:Q_PER_KV * (g + 1)].reshape(Q_PER_KV * tq, HEAD_DIM),
                    nt, preferred_element_type=F32)
        m = [m_scr[g] for g in range(N_KV_HEADS)]
        l = [l_scr[g] for g in range(N_KV_HEADS)]
        for c in range(nsub):
            bias = bias_buf[kb, c * sub:(c + 1) * sub, :]
            bias2 = jnp.concatenate([bias] * Q_PER_KV, axis=1)
            for g in range(N_KV_HEADS):
                s = s_scr[c, g] + bias2
                m_new = jnp.maximum(m[g], jnp.max(s, axis=0, keepdims=True))
                alpha = jnp.exp2(m[g] - m_new)
                p = jnp.exp2(s - m_new)
                l[g] = alpha * l[g] + col_sum(p)
                pv = jnp.dot(vt_ref[0, kb * nsub + c, g * HEAD_DIM:(g + 1) * HEAD_DIM, :], p.astype(BF16),
                             preferred_element_type=F32)
                acc_scr[g] = alpha * acc_scr[g] + pv
                m[g] = m_new
        for g in range(N_KV_HEADS):
            m_scr[g] = m[g]
            l_scr[g] = l[g]
        return carry

    lax.fori_loop(0, nkb, attn_block, 0)

    o_t = []
    for g in range(N_KV_HEADS):
        og = acc_scr[g] / l_scr[g]
        o_t.extend(og[:, hh * tq:(hh + 1) * tq] for hh in range(Q_PER_KV))
    y_att = jnp.transpose(jnp.concatenate(o_t, axis=0)).astype(BF16)
    y = jnp.dot(yp_ref[0], wo_ref[0:POOL_WIDTH, :], preferred_element_type=F32)
    y = y + jnp.dot(y_att, wo_ref[POOL_WIDTH:POOL_WIDTH + ATT_WIDTH, :], preferred_element_type=F32)
    o_ref[...] = (x_ref[0] + y).reshape(o_ref.shape)


def _dsa(q_hm, qi_hm, kiwi, k_hm, v_t, ki_all, x, yp, wo_bf, *, tk, pos0, n_keys, n_sel, out_base=None):
    B, T, D = x.shape
    L = k_hm.shape[2]
    tq = LANES
    assert L % tk == 0 and T % tq == 0
    nkb_max = L // tk
    tile = lambda wdt: pl.BlockSpec((1, tq, wdt), lambda b, j: (b, j, 0))
    in_specs = [pl.BlockSpec((1, N_HEADS, tq, HEAD_DIM), lambda b, j: (b, 0, j, 0)),
                pl.BlockSpec((1, N_IDX_HEADS, tq, IDX_DIM), lambda b, j: (b, 0, j, 0)),
                tile(LANES),
                pl.BlockSpec((1, N_KV_HEADS, L, HEAD_DIM), lambda b, j: (b, 0, 0, 0)),
                pl.BlockSpec((1, L // LANES, KV_WIDTH, LANES), lambda b, j: (b, 0, 0, 0)),
                pl.BlockSpec((1, L, IDX_DIM), lambda b, j: (b, 0, 0)),
                tile(D), tile(POOL_WIDTH), pl.BlockSpec((POOL_WIDTH + ATT_WIDTH, D), lambda b, j: (0, 0))]
    nq = T // tq
    args = (q_hm, qi_hm, kiwi, k_hm, v_t, ki_all, x, yp, wo_bf)
    if out_base is None:
        out_shape, out_spec, aliases = jax.ShapeDtypeStruct((B, T, D), F32), tile(D), {}
    else:
        out_shape = jax.ShapeDtypeStruct(out_base.shape, F32)
        out_spec = pl.BlockSpec((tq, D), lambda b, j: (b * nq + j, 0))
        in_specs, args, aliases = in_specs + [pl.BlockSpec(memory_space=pl.ANY)], args + (out_base,), {len(args): 0}
    return pl.pallas_call(
        functools.partial(_dsa_kernel, tk=tk, pos0=pos0, n_keys=n_keys, n_sel=n_sel),
        out_shape=out_shape, grid=(B, nq), in_specs=in_specs, out_specs=out_spec, input_output_aliases=aliases,
        scratch_shapes=[pltpu.VMEM((nkb_max, tk, tq), I32), pltpu.VMEM((nkb_max, tk, tq), F32),
                        pltpu.VMEM((N_KV_HEADS, 1, Q_PER_KV * tq), F32),
                        pltpu.VMEM((N_KV_HEADS, 1, Q_PER_KV * tq), F32),
                        pltpu.VMEM((N_KV_HEADS, HEAD_DIM, Q_PER_KV * tq), F32),
                        pltpu.VMEM((tk // LANES, N_KV_HEADS, LANES, Q_PER_KV * tq), F32)],
        compiler_params=pltpu.CompilerParams(dimension_semantics=("parallel", "arbitrary"),
                                             vmem_limit_bytes=VMEM_LIMIT),
        name="dsa",
    )(*args)


def _gmlp_kernel(x_ref, g_ref, win_ref, lng_ref, lnb_ref, ws_ref, bias_ref, wout_ref, o_ref, v_ref, *, tm):
    x = x_ref[...]
    h = _rms(x, g_ref[...])
    z = jax.nn.gelu(jnp.dot(h.astype(BF16), win_ref[...], preferred_element_type=F32))
    half = z.shape[1] // 2
    u, v = z[:, :half], z[:, half:]
    mu = jnp.mean(v, axis=-1, keepdims=True)
    var = jnp.mean(jnp.square(v - mu), axis=-1, keepdims=True)
    vn = (v - mu) * lax.rsqrt(var + LN_EPS) * lng_ref[...] + lnb_ref[...]
    v_ref[...] = vn
    gd = half // GM_GROUPS
    gated = []
    for c in range(tm // GM_CHUNK):
        rows = slice(c * GM_CHUNK, (c + 1) * GM_CHUNK)
        vc = vn[rows].astype(BF16)
        mixed = jnp.concatenate(
            [jnp.dot(ws_ref[0, g], vc[:, g * gd:(g + 1) * gd], preferred_element_type=F32)
             for g in range(GM_GROUPS)], axis=1) + bias_ref[0]
        gated.append((u[rows] * mixed).astype(BF16))
    gated = jnp.concatenate(gated, axis=0)
    o_ref[...] = x + jnp.dot(gated, wout_ref[...], preferred_element_type=F32)


def _gmlp(x, g, win_bf, lng, lnb, ws2, bias2, wout_bf, *, tm, n_first, n_v_rows):
    N, D = x.shape
    half = win_bf.shape[1] // 2
    nt = N // tm
    t_first = n_first // tm
    variant = lambda i: jnp.where(i >= t_first, 1, 0)
    row = pl.BlockSpec((tm, D), lambda i: (i, 0))
    const = lambda s: pl.BlockSpec(s, lambda i: (0, 0))
    in_specs = [row, const((1, D)), const((D, 2 * half)), const((1, half)), const((1, half)),
                pl.BlockSpec((1, GM_GROUPS, GM_CHUNK, GM_CHUNK), lambda i: (variant(i), 0, 0, 0)),
                pl.BlockSpec((1, GM_CHUNK, half), lambda i: (variant(i), 0, 0)),
                const((half, D))]
    out_specs = (row, pl.BlockSpec((tm, half), lambda i: (jnp.maximum(i - t_first, 0), 0)))
    return pl.pallas_call(
        functools.partial(_gmlp_kernel, tm=tm),
        out_shape=(jax.ShapeDtypeStruct((N, D), F32), jax.ShapeDtypeStruct((n_v_rows, half), F32)),
        grid=(nt,), in_specs=in_specs, out_specs=out_specs,
        compiler_params=pltpu.CompilerParams(dimension_semantics=("arbitrary",), vmem_limit_bytes=VMEM_LIMIT),
        name="gmlp",
    )(x, g, win_bf, lng, lnb, ws2, bias2, wout_bf)


ROUTE_E0, ROUTE_E1, ROUTE_R0, ROUTE_R1, ROUTE_G0, ROUTE_G1 = range(6)
ROUTE_ROWS = 8
N_LOGITS = N_EXPERT_GROUPS + N_EXPERTS
LOGIT_ROWS = 40


def _router_kernel(x_ref, g_ref, wr_ref, route_ref, cnt_ref, carry_ref, *, tm):
    i = pl.program_id(0)

    @pl.when(i == 0)
    def _():
        carry_ref[...] = jnp.zeros(carry_ref.shape, F32)

    h = _rms(x_ref[...], g_ref[...])
    nt = (((1,), (1,)), ((), ()))
    logits = lax.dot_general(wr_ref[...], h.astype(BF16), nt, preferred_element_type=F32)[:LOGIT_ROWS]
    row = lax.broadcasted_iota(I32, (LOGIT_ROWS, tm), 0).astype(F32)
    ninf = -jnp.inf
    big = float(LANES)
    cmax = lambda a: jnp.max(a, axis=0, keepdims=True)
    cmin = lambda a: jnp.min(a, axis=0, keepdims=True)
    csum = lambda a: jnp.sum(a, axis=0, keepdims=True)

    is_grp = row < N_EXPERT_GROUPS
    lg = jnp.where(is_grp, logits, ninf)
    mg = cmax(lg)
    g_sel = cmin(jnp.where(lg == mg, row, big))
    p_grp = 1.0 / csum(jnp.where(is_grp, jnp.exp(lg - mg), 0.0))
    lo = N_EXPERT_GROUPS + g_sel * EXPERTS_PER_GROUP
    le = jnp.where((row >= lo) & (row < lo + EXPERTS_PER_GROUP), logits, ninf)
    v1 = cmax(le)
    j1 = cmin(jnp.where(le == v1, row, big))
    le2 = jnp.where(row == j1, ninf, le)
    v2 = cmax(le2)
    j2 = cmin(jnp.where(le2 == v2, row, big))
    e0 = j1 - N_EXPERT_GROUPS
    e1 = j2 - N_EXPERT_GROUPS
    r = jnp.exp(v2 - v1)
    g0 = p_grp / (1.0 + r)
    g1 = p_grp * r / (1.0 + r)

    erow = lax.broadcasted_iota(I32, (N_EXPERTS, tm), 0).astype(F32)
    oh0 = jnp.where(erow == e0, 1.0, 0.0)
    oh1 = jnp.where(erow == e1, 1.0, 0.0)
    oh = oh0 + oh1
    earlier = jnp.where(lax.broadcasted_iota(I32, (tm, tm), 0) < lax.broadcasted_iota(I32, (tm, tm), 1),
                        1.0, 0.0).astype(BF16)
    before = jnp.dot(oh.astype(BF16), earlier, preferred_element_type=F32) + carry_ref[...]
    r0 = csum(oh0 * before)
    r1 = csum(oh1 * before)
    carry_ref[...] = carry_ref[...] + jnp.sum(oh, axis=1, keepdims=True)
    cnt_ref[...] = jnp.broadcast_to(carry_ref[...], cnt_ref.shape)

    rows = {ROUTE_E0: e0, ROUTE_E1: e1, ROUTE_R0: r0, ROUTE_R1: r1, ROUTE_G0: g0, ROUTE_G1: g1}
    zero = jnp.zeros((1, tm), F32)
    route_ref[...] = jnp.concatenate([rows.get(k, zero) for k in range(ROUTE_ROWS)], axis=0)


def _router(x, g, wr_t, *, tm):
    N, D = x.shape
    const = lambda s: pl.BlockSpec(s, lambda i: (0, 0))
    return pl.pallas_call(
        functools.partial(_router_kernel, tm=tm),
        out_shape=(jax.ShapeDtypeStruct((ROUTE_ROWS, N), F32), jax.ShapeDtypeStruct((N_EXPERTS, LANES), F32)),
        grid=(N // tm,), in_specs=[pl.BlockSpec((tm, D), lambda i: (i, 0)), const((1, D)), const((LANES, D))],
        out_specs=(pl.BlockSpec((ROUTE_ROWS, tm), lambda i: (0, i)), const((N_EXPERTS, LANES))),
        scratch_shapes=[pltpu.VMEM((N_EXPERTS, 1), F32)],
        compiler_params=pltpu.CompilerParams(dimension_semantics=("arbitrary",), vmem_limit_bytes=VMEM_LIMIT),
        name="router",
    )(x, g, wr_t)


SUBLANES = 8


def _pack_bf16_pairs(h):
    half = h.shape[1] // 2
    lo = lax.bitcast_convert_type(h[:, :half].astype(BF16).astype(F32), U32)
    hi = lax.bitcast_convert_type(h[:, half:].astype(BF16).astype(F32), U32)
    return (lo >> 16) | (hi & jnp.uint32(0xFFFF0000))


def _unpack_bf16_pairs(w):
    lo = lax.bitcast_convert_type(w << 16, F32).astype(BF16)
    hi = lax.bitcast_convert_type(w & jnp.uint32(0xFFFF0000), F32).astype(BF16)
    return jnp.concatenate([lo, hi], axis=1)


def _dispatch_kernel(d0_ref, d1_ref, pad_ref, x_ref, g_ref, rows_ref, h_scr, zblk, sems, *, tm):
    i = pl.program_id(0)
    n = pl.num_programs(0)
    slot = i % 2
    groups = tm // SUBLANES

    def wait_slot(s):
        for _ in range(2 * groups):
            pltpu.make_async_copy(h_scr.at[s, 0], rows_ref.at[pl.ds(0, SUBLANES)], sems.at[s]).wait()

    @pl.when(i >= 2)
    def _():
        wait_slot(slot)

    h_scr[slot] = _pack_bf16_pairs(_rms(x_ref[...], g_ref[...])).reshape(groups, SUBLANES, h_scr.shape[3])

    def body(k, carry):
        for u in range(SUBLANES):
            r = k * SUBLANES + u
            src = h_scr.at[slot, k, pl.ds(u, 1)]
            pltpu.make_async_copy(src, rows_ref.at[pl.ds(d0_ref[r], 1)], sems.at[slot]).start()
            pltpu.make_async_copy(src, rows_ref.at[pl.ds(d1_ref[r], 1)], sems.at[slot]).start(priority=1)
        return carry

    lax.fori_loop(0, groups, body, 0)

    @pl.when(i == n - 1)
    def _():
        wait_slot(slot)

    @pl.when(jnp.logical_and(i == n - 1, n >= 2))
    def _():
        wait_slot(1 - slot)

    @pl.when(i == n - 1)
    def _():
        zblk[...] = jnp.zeros(zblk.shape, U32)
        zrow = zblk.at[pl.ds(0, 1)]
        for e in range(N_EXPERTS):
            first, count = pad_ref[0, e], pad_ref[1, e]

            def start(k, carry):
                pltpu.make_async_copy(zrow, rows_ref.at[pl.ds(first + k, 1)], sems.at[2]).start()
                return carry

            def wait(k, carry):
                pltpu.make_async_copy(zrow, rows_ref.at[pl.ds(first, 1)], sems.at[2]).wait()
                return carry

            lax.fori_loop(0, count, start, 0)
            lax.fori_loop(0, count, wait, 0)

        def blk_copy(b):
            return pltpu.make_async_copy(zblk, rows_ref.at[pl.ds(b * MOE_BLOCK, MOE_BLOCK)], sems.at[2])

        n_blocks = rows_ref.shape[0] // MOE_BLOCK
        n_used = pad_ref[2, 0]
        lax.fori_loop(n_used, n_blocks, lambda b, c: (blk_copy(b).start(), c)[1], 0)
        lax.fori_loop(n_used, n_blocks, lambda b, c: (blk_copy(n_used).wait(), c)[1], 0)


def _dispatch(dest_flat, pad_info, x, g, *, tm, n_rows):
    N, D = x.shape
    nt = N // tm
    return pl.pallas_call(
        functools.partial(_dispatch_kernel, tm=tm),
        out_shape=jax.ShapeDtypeStruct((n_rows, D // 2), U32), grid=(nt,),
        in_specs=[pl.BlockSpec((tm,), lambda i: (i,), memory_space=pltpu.SMEM),
                  pl.BlockSpec((tm,), lambda i: (i + nt,), memory_space=pltpu.SMEM),
                  pl.BlockSpec(memory_space=pltpu.SMEM),
                  pl.BlockSpec((tm, D), lambda i: (i, 0)),
                  pl.BlockSpec((1, D), lambda i: (0, 0))],
        out_specs=pl.BlockSpec(memory_space=pl.ANY),
        scratch_shapes=[pltpu.VMEM((2, tm // SUBLANES, SUBLANES, D // 2), U32), pltpu.VMEM((MOE_BLOCK, D // 2), U32),
                        pltpu.SemaphoreType.DMA((3,))],
        compiler_params=pltpu.CompilerParams(dimension_semantics=("arbitrary",), vmem_limit_bytes=VMEM_LIMIT),
        name="dispatch",
    )(dest_flat, dest_flat, pad_info, x, g)


def _expert_kernel(be_ref, nu_ref, x_ref, wg_ref, wu_ref, wd_ref, y_ref, wg_bf, wu_bf, wd_bf):
    i = pl.program_id(0)

    @pl.when(jnp.logical_or(i == 0, be_ref[i] != be_ref[jnp.maximum(i - 1, 0)]))
    def _():
        wg_bf[...] = wg_ref[0, 0].astype(BF16)
        wu_bf[...] = wu_ref[0, 0].astype(BF16)
        wd_bf[...] = wd_ref[0, 0].astype(BF16)

    @pl.when(i < nu_ref[0])
    def _():
        x = _unpack_bf16_pairs(x_ref[...])
        a = jnp.dot(x, wg_bf[...], preferred_element_type=F32)
        u = jnp.dot(x, wu_bf[...], preferred_element_type=F32)
        act = (a * jax.nn.sigmoid(a)) * u
        y_ref[...] = jnp.dot(act.astype(BF16), wd_bf[...], preferred_element_type=F32)

    @pl.when(i >= nu_ref[0])
    def _():
        y_ref[...] = jnp.zeros(y_ref.shape, F32)


def _experts(block_e, n_used, x_rows, w_gate, w_up, w_down, layer):
    n_rows = x_rows.shape[0]
    D, de = w_gate.shape[2:]
    n_blocks = n_rows // MOE_BLOCK
    grid_spec = pltpu.PrefetchScalarGridSpec(
        num_scalar_prefetch=2, grid=(n_blocks,),
        in_specs=[pl.BlockSpec((MOE_BLOCK, D // 2), lambda i, be, nu: (jnp.minimum(i, nu[0] - 1), 0)),
                  pl.BlockSpec((1, 1, D, de), lambda i, be, nu: (layer, be[i], 0, 0)),
                  pl.BlockSpec((1, 1, D, de), lambda i, be, nu: (layer, be[i], 0, 0)),
                  pl.BlockSpec((1, 1, de, D), lambda i, be, nu: (layer, be[i], 0, 0))],
        out_specs=pl.BlockSpec((MOE_BLOCK, D), lambda i, be, nu: (i, 0)),
        scratch_shapes=[pltpu.VMEM((D, de), BF16), pltpu.VMEM((D, de), BF16), pltpu.VMEM((de, D), BF16)])
    return pl.pallas_call(
        _expert_kernel, out_shape=jax.ShapeDtypeStruct((n_rows, D), F32), grid_spec=grid_spec,
        compiler_params=pltpu.CompilerParams(dimension_semantics=("arbitrary",), vmem_limit_bytes=VMEM_LIMIT),
        name="experts",
    )(block_e, n_used, x_rows, w_gate, w_up, w_down)


def _combine_kernel(dc0_ref, dc1_ref, dn0_ref, dn1_ref, x_ref, route_ref, g_ref, y_ref, *rest,
                    tm, final_norm, t_first):
    *o_refs, ybuf, sems = rest
    i = pl.program_id(0)
    n = pl.num_programs(0)
    slot = i % 2
    groups = tm // SUBLANES
    D = x_ref.shape[1]

    def gather(d_refs, s):
        def body(k, carry):
            for u in range(SUBLANES):
                r = k * SUBLANES + u
                for a in range(2):
                    pltpu.make_async_copy(y_ref.at[pl.ds(d_refs[a][r], 1)], ybuf.at[s, a, k, pl.ds(u, 1)],
                                          sems.at[s]).start(priority=a)
            return carry
        lax.fori_loop(0, groups, body, 0)

    @pl.when(i == 0)
    def _():
        gather((dc0_ref, dc1_ref), 0)

    @pl.when(i + 1 < n)
    def _():
        gather((dn0_ref, dn1_ref), 1 - slot)

    for _ in range(2 * groups):
        pltpu.make_async_copy(y_ref.at[pl.ds(0, SUBLANES)], ybuf.at[slot, 0, 0], sems.at[slot]).wait()
    route = jnp.transpose(route_ref[...])
    g0 = route[:, ROUTE_G0:ROUTE_G0 + 1]
    g1 = route[:, ROUTE_G1:ROUTE_G1 + 1]
    out = x_ref[...] + (ybuf[slot, 0].reshape(tm, D) * g0 + ybuf[slot, 1].reshape(tm, D) * g1)
    if final_norm:
        out = _rms(out, g_ref[...])
    if t_first is None:
        o_refs[0][...] = out
    else:
        @pl.when(i < t_first)
        def _():
            o_refs[0][...] = out

        @pl.when(i >= t_first)
        def _():
            o_refs[1][...] = out


def _combine(dest_flat, x, route, g, y_rows, *, tm, final_norm, n_first=None):
    N, D = x.shape
    nt = N // tm
    row = lambda wdt: pl.BlockSpec((tm, wdt), lambda i: (i, 0))
    nxt = lambda i: jnp.minimum(i + 1, nt - 1)
    smem = lambda f: pl.BlockSpec((tm,), f, memory_space=pltpu.SMEM)
    if n_first is None:
        t_first, out_shape, out_specs = None, jax.ShapeDtypeStruct((N, D), F32), row(D)
    else:
        t_first = n_first // tm
        out_shape = (jax.ShapeDtypeStruct((n_first, D), F32), jax.ShapeDtypeStruct((N - n_first, D), F32))
        out_specs = (pl.BlockSpec((tm, D), lambda i: (jnp.minimum(i, t_first - 1), 0)),
                     pl.BlockSpec((tm, D), lambda i: (jnp.maximum(i - t_first, 0), 0)))
    return pl.pallas_call(
        functools.partial(_combine_kernel, tm=tm, final_norm=final_norm, t_first=t_first),
        out_shape=out_shape, grid=(nt,),
        in_specs=[smem(lambda i: (i,)), smem(lambda i: (i + nt,)),
                  smem(lambda i: (nxt(i),)), smem(lambda i: (nxt(i) + nt,)),
                  row(D), pl.BlockSpec((ROUTE_ROWS, tm), lambda i: (0, i)), pl.BlockSpec((1, D), lambda i: (0, 0)),
                  pl.BlockSpec(memory_space=pl.ANY)],
        out_specs=out_specs,
        scratch_shapes=[pltpu.VMEM((2, 2, tm // SUBLANES, SUBLANES, D), F32), pltpu.SemaphoreType.DMA((2,))],
        compiler_params=pltpu.CompilerParams(dimension_semantics=("arbitrary",), vmem_limit_bytes=VMEM_LIMIT),
        name="combine",
    )(dest_flat, dest_flat, dest_flat, dest_flat, x, route, g, y_rows)


def _hier_moe(x, g_ffn, w_rg, w_re, w_gate, w_up, w_down, layer, g_final, *, final_norm, n_first=None):
    N, D = x.shape
    tm = TOKEN_TILE
    wr_t = jnp.concatenate([w_rg.T, w_re.T, jnp.zeros((LANES - N_LOGITS, D), F32)], axis=0).astype(BF16)
    g_ffn = g_ffn.reshape(1, D)
    route, counts = _router(x, g_ffn, wr_t, tm=tm)
    counts = counts[:, 0].astype(I32)
    eid_t = route[ROUTE_E0:ROUTE_E1 + 1].astype(I32)
    rank_t = route[ROUTE_R0:ROUTE_R1 + 1].astype(I32)
    padded = (counts + MOE_BLOCK - 1) // MOE_BLOCK * MOE_BLOCK
    pad_end = jnp.cumsum(padded)
    pad_start = pad_end - padded
    expert_ids = jnp.arange(N_EXPERTS, dtype=I32)
    start_of = jnp.sum(jnp.where(eid_t[:, :, None] == expert_ids, pad_start, 0), axis=-1)
    dest_t = start_of + rank_t
    n_blocks = -(-(2 * N) // MOE_BLOCK) + N_EXPERTS
    n_rows = n_blocks * MOE_BLOCK
    n_used = (pad_end[-1] // MOE_BLOCK).astype(I32)
    block_start = jnp.minimum(jnp.arange(n_blocks, dtype=I32), n_used - 1) * MOE_BLOCK
    block_e = jnp.minimum(jnp.sum(block_start[:, None] >= pad_end[None, :], axis=1), N_EXPERTS - 1).astype(I32)
    pad_info = jnp.stack([pad_start + counts, padded - counts, jnp.broadcast_to(n_used, (N_EXPERTS,))])
    dest_flat = dest_t.reshape(-1)
    x_rows = _dispatch(dest_flat, pad_info, x, g_ffn, tm=tm, n_rows=n_rows)
    y_rows = _experts(block_e, n_used.reshape(1), x_rows, w_gate, w_up, w_down, layer)
    return _combine(dest_flat, x, route, g_final.reshape(1, D), y_rows, tm=tm, final_norm=final_norm,
                    n_first=n_first)


def _rope_tables(pos, rot_lanes):
    inv = 1.0 / (ROPE_THETA ** (jnp.arange(ROT_HALF, dtype=F32) / ROT_HALF))
    ang = pos.astype(F32)[:, None] * inv[None, :]
    cos, sin = jnp.cos(ang), jnp.sin(ang)
    lane = jnp.arange(LANES)
    r = lane % HEAD_DIM
    active = (lane < rot_lanes)
    first = active & (r < ROT_HALF)
    second = active & (r >= ROT_HALF) & (r < 2 * ROT_HALF)
    cos_l = cos[:, r % ROT_HALF]
    sin_l = sin[:, r % ROT_HALF]
    c = jnp.where((first | second)[None, :], cos_l, 1.0)
    sa = jnp.where(first[None, :], -sin_l, 0.0)
    sb = jnp.where(second[None, :], sin_l, 0.0)
    return jnp.stack([c, sa, sb]).astype(F32)


def _prep_w_in(w_in):
    D = w_in.shape[0]
    assert w_in.shape[1] == sum(IN_SIZES)
    pad = jnp.zeros((D, IN_WIDTH_PAD - w_in.shape[1]), w_in.dtype)
    return jnp.concatenate([w_in, pad], axis=1).astype(BF16)


def kernel(x_prompt, x_sample, cache_k, cache_v, cache_idx_k, state_pool, norm_mix, norm_ffn, norm_final,
           par_w_in, par_pool_w, par_pool_scale, par_w_out, gm_w_in, gm_ln_g, gm_ln_b, gm_ws, gm_bs, gm_w_out,
           moe_router_group, moe_router_expert, moe_w_gate, moe_w_up, moe_w_down):
    Bp, Tp, D = x_prompt.shape
    Bs, Ts, _ = x_sample.shape
    past = cache_k.shape[2]
    Np, Ns = Bp * Tp, Bs * Ts
    depth = norm_mix.shape[0]
    assert depth == 2 and Ts == CHUNK and Tp % TOKEN_TILE == 0 and Ns % TOKEN_TILE == 0

    w_in_bf = _prep_w_in(par_w_in[0])
    pw_bf = par_pool_w[0].astype(BF16)
    ps = par_pool_scale[0].reshape(1, POOL_WIDTH)
    wo_bf = par_w_out[0].astype(BF16)
    g_mix0 = norm_mix[0].reshape(1, D)
    pos_p = jnp.arange(Tp, dtype=I32)
    pos_s = past + jnp.arange(Ts, dtype=I32)

    hist_p = jnp.zeros((Bp, HIST_ROWS, POOL_WIDTH), F32)
    hist_s = jnp.pad(state_pool[0], ((0, 0), (1, 0), (0, 0)))
    tk = KEY_BLOCK
    (q_p, qi_p, k_p, v_p, kiwi_p, kbf_p, vbf_p, kibf_p, yp_p, st_p) = _inproj(
        x_prompt, g_mix0, w_in_bf, _rope_tables(pos_p, LANES), _rope_tables(pos_p, IDX_DIM), hist_p, pw_bf, ps,
        tm=TOKEN_TILE, pos0=0)
    (q_s, qi_s, k_s, v_s, kiwi_s, kbf_s, vbf_s, kibf_s, yp_s, st_s) = _inproj(
        x_sample, g_mix0, w_in_bf, _rope_tables(pos_s, LANES), _rope_tables(pos_s, IDX_DIM), hist_s, pw_bf, ps,
        tm=Ts, pos0=past)

    ls = past + Ts
    lsp = -(-ls // tk) * tk
    kpad = lambda a, ax: jnp.pad(a, [(0, lsp - ls) if d == ax else (0, 0) for d in range(a.ndim)])
    kall = kpad(jnp.concatenate([jnp.transpose(cache_k[0], (0, 2, 1, 3)).astype(BF16), kbf_s], axis=2), 2)
    vall = kpad(jnp.concatenate([cache_v[0].reshape(Bs, past, KV_WIDTH).astype(BF16), vbf_s], axis=1), 1)
    vall_t = jnp.transpose(vall.reshape(Bs, lsp // LANES, LANES, KV_WIDTH), (0, 1, 3, 2))
    kiall = kpad(jnp.concatenate([cache_idx_k[0].astype(BF16), kibf_s], axis=1), 1)
    qpad = lambda a, ax: jnp.pad(a, [(0, LANES - Ts) if d == ax else (0, 0) for d in range(a.ndim)])
    x1_s = _dsa(qpad(q_s, 2), qpad(qi_s, 2), qpad(kiwi_s, 1), kall, vall_t, kiall, qpad(x_sample, 1), qpad(yp_s, 1),
                wo_bf, tk=tk, pos0=past, n_keys=ls, n_sel=min(TOPK_MAX, ls // 4))[:, :Ts]
    assert Tp % tk == 0
    x = jnp.pad(x1_s.reshape(Ns, D), ((Np, 0), (0, 0)))
    x = _dsa(q_p, qi_p, kiwi_p, kbf_p, vbf_p, kibf_p, x_prompt, yp_p, wo_bf,
             tk=tk, pos0=0, n_keys=Tp, n_sel=min(TOPK_MAX, Tp // 4), out_base=x)
    x = _hier_moe(x, norm_ffn[0], moe_router_group[0], moe_router_expert[0],
                  moe_w_gate, moe_w_up, moe_w_down, 0, norm_final, final_norm=False)

    cs = Ts
    tril = lambda n: jnp.tril(jnp.ones((n, n), bool))
    ws_p = jnp.where(tril(GM_CHUNK)[None], gm_ws[0], 0.0)
    ws_small = jnp.where(tril(cs)[None], gm_ws[0][:, :cs, :cs], 0.0)
    rep = GM_CHUNK // cs
    ws_s = jnp.einsum('ab,gts->gatbs', jnp.eye(rep, dtype=F32), ws_small).reshape(GM_GROUPS, GM_CHUNK, GM_CHUNK)
    ws2 = jnp.stack([ws_p, ws_s]).astype(BF16)
    gd = D // GM_GROUPS
    bias_p = jnp.repeat(jnp.transpose(gm_bs[0]), gd, axis=1)
    bias_s = jnp.tile(jnp.repeat(jnp.transpose(gm_bs[0][:, :cs]), gd, axis=1), (rep, 1))
    bias2 = jnp.stack([bias_p, bias_s])
    x, gm_v = _gmlp(x, norm_mix[1].reshape(1, D), gm_w_in[0].astype(BF16), gm_ln_g[0].reshape(1, D),
                    gm_ln_b[0].reshape(1, D), ws2, bias2, gm_w_out[0].astype(BF16),
                    tm=TOKEN_TILE, n_first=Np, n_v_rows=Ns)
    y_p, y_s = _hier_moe(x, norm_ffn[1], moe_router_group[1], moe_router_expert[1],
                         moe_w_gate, moe_w_up, moe_w_down, 1, norm_final, final_norm=True, n_first=Np)
    y_p = y_p.reshape(Bp, Tp, D)
    y_s = y_s.reshape(Bs, Ts, D)
    r4 = lambda a, b, t: a.reshape(1, b, t, N_KV_HEADS, HEAD_DIM)
    return (y_p, y_s,
            r4(k_p, Bp, Tp), r4(v_p, Bp, Tp), kiwi_p[:, :, :IDX_DIM][None], st_p[:, 1:][None],
            r4(k_s, Bs, Ts), r4(v_s, Bs, Ts), kiwi_s[:, :, :IDX_DIM][None], st_s[:, 1:][None],
            gm_v.reshape(1, Bs, Ts, D))
```

```python
import functools

import jax
import jax.numpy as jnp
from jax import lax
from jax.experimental import pallas as pl
from jax.experimental.pallas import tpu as pltpu

F32 = jnp.float32
BF16 = jnp.bfloat16
I32 = jnp.int32
U32 = jnp.uint32

LANES = 128
SUBLANES = 8
CHUNK = 64
POOL_WINDOWS = (2, 4, 8, 16)
POOL_GROUP_DIM = 128
POOL_WIDTH = 512
HIST_ROWS = 16
N_HEADS = 8
HEAD_DIM = 64
N_KV_HEADS = 4
Q_PER_KV = N_HEADS // N_KV_HEADS
ATT_WIDTH = N_HEADS * HEAD_DIM
KV_WIDTH = N_KV_HEADS * HEAD_DIM
N_IDX_HEADS = 8
IDX_DIM = 64
TOPK_MAX = 256
ROPE_THETA = 500000.0
ROT_HALF = HEAD_DIM // 8
GM_CHUNK = 128
GM_GROUPS = 8
N_EXPERT_GROUPS = 4
EXPERTS_PER_GROUP = 8
N_EXPERTS = 32
MOE_BLOCK = 256
RMS_EPS = 1e-6
LN_EPS = 1e-5

INT_MIN = -2147483648
LOG2_E = 1.4426950408889634
NEG_BIG = -1e30
VMEM_LIMIT = 48 * 1024 * 1024

TOKEN_TILE = 256
PROJ_TILE = 512
KEY_BLOCK = 512
COUNT_SLAB = 64

IN_SIZES = (POOL_WIDTH, ATT_WIDTH, KV_WIDTH, KV_WIDTH, N_IDX_HEADS * IDX_DIM, IDX_DIM, N_IDX_HEADS)
COL_XP, COL_Q, COL_K, COL_V, COL_QI, COL_KIWI = (sum(IN_SIZES[:i]) for i in range(6))
IN_WIDTH_PAD = COL_KIWI + LANES


def _rms(x, g):
    return x * lax.rsqrt(jnp.mean(x * x, axis=-1, keepdims=True) + RMS_EPS) * g


def _rope128(x, c, sa, sb):
    return x * c + pltpu.roll(x, LANES - ROT_HALF, 1) * sa + pltpu.roll(x, ROT_HALF, 1) * sb


def _inproj_kernel(x_ref, g_ref, w_ref, rope_ref, ropeki_ref, hist_ref, pw_ref, ps_ref,
                   q_ref, qi_ref, k_ref, v_ref, kiwi_ref, kbf_ref, vbf_ref, kibf_ref, yp_ref, state_ref,
                   buf_ref, *, tm, pos0):
    j = pl.program_id(1)
    h = _rms(x_ref[0], g_ref[...])
    proj = jnp.dot(h.astype(BF16), w_ref[...], preferred_element_type=F32)
    c, sa, sb = rope_ref[0], rope_ref[1], rope_ref[2]

    def put_heads(ref, i, chunk):
        ref[0, 2 * i] = chunk[:, :HEAD_DIM].astype(BF16)
        ref[0, 2 * i + 1] = chunk[:, HEAD_DIM:].astype(BF16)

    for i in range(ATT_WIDTH // LANES):
        put_heads(q_ref, i, _rope128(proj[:, COL_Q + i * LANES:COL_Q + (i + 1) * LANES], c, sa, sb)
                  * (HEAD_DIM ** -0.5 * LOG2_E))
        put_heads(qi_ref, i, _rope128(proj[:, COL_QI + i * LANES:COL_QI + (i + 1) * LANES], c, sa, sb))
    for i in range(KV_WIDTH // LANES):
        sl = slice(i * LANES, (i + 1) * LANES)
        kr = _rope128(proj[:, COL_K + i * LANES:COL_K + (i + 1) * LANES], c, sa, sb)
        k_ref[0, :, sl] = kr
        put_heads(kbf_ref, i, kr)
    vv = proj[:, COL_V:COL_V + KV_WIDTH]
    v_ref[0] = vv
    if tm % LANES == 0:
        for cc in range(tm // LANES):
            vbf_ref[0, cc] = jnp.transpose(vv[cc * LANES:(cc + 1) * LANES, :]).astype(BF16)
    else:
        vbf_ref[0] = vv.astype(BF16)
    kiwi = _rope128(proj[:, COL_KIWI:COL_KIWI + LANES], ropeki_ref[0], ropeki_ref[1], ropeki_ref[2])
    kiwi_ref[0] = kiwi
    kibf_ref[0] = kiwi[:, :IDX_DIM].astype(BF16)

    @pl.when(j == 0)
    def _():
        buf_ref[0:HIST_ROWS, :] = hist_ref[0]

    xp = proj[:, COL_XP:COL_XP + POOL_WIDTH]
    buf_ref[HIST_ROWS:HIST_ROWS + tm, :] = xp
    pos = pos0 + j * tm + lax.broadcasted_iota(I32, (tm, 1), 0)
    for gi, w in enumerate(POOL_WINDOWS):
        c0 = gi * POOL_GROUP_DIM
        s = xp[:, c0:c0 + POOL_GROUP_DIM]
        for i in range(1, w):
            s = s + buf_ref[HIST_ROWS - i:HIST_ROWS - i + tm, c0:c0 + POOL_GROUP_DIM]
        cnt = jnp.minimum(pos + 1, w).astype(F32)
        d = s / cnt - xp[:, c0:c0 + POOL_GROUP_DIM]
        y = jnp.dot(d.astype(BF16), pw_ref[gi], preferred_element_type=F32)
        yp_ref[0, :, c0:c0 + POOL_GROUP_DIM] = (y * ps_ref[:, c0:c0 + POOL_GROUP_DIM]).astype(BF16)
    tail = buf_ref[tm:tm + HIST_ROWS, :]
    state_ref[0] = tail
    buf_ref[0:HIST_ROWS, :] = tail


def _inproj(x, g, w_bf, rope, ropeki, hist, pw_bf, ps, *, tm, pos0):
    B, T, D = x.shape
    nt = T // tm
    f = lambda shape, dt: jax.ShapeDtypeStruct(shape, dt)
    v_t = tm % LANES == 0
    out_shape = (
        f((B, N_HEADS, T, HEAD_DIM), BF16), f((B, N_IDX_HEADS, T, IDX_DIM), BF16),
        f((B, T, KV_WIDTH), F32), f((B, T, KV_WIDTH), F32), f((B, T, LANES), F32),
        f((B, N_KV_HEADS, T, HEAD_DIM), BF16),
        f((B, T // LANES, KV_WIDTH, LANES) if v_t else (B, T, KV_WIDTH), BF16),
        f((B, T, IDX_DIM), BF16),
        f((B, T, POOL_WIDTH), BF16), f((B, HIST_ROWS, POOL_WIDTH), F32),
    )
    tile = lambda wdt: pl.BlockSpec((1, tm, wdt), lambda b, j: (b, j, 0))
    heads = lambda n, wdt: pl.BlockSpec((1, n, tm, wdt), lambda b, j: (b, 0, j, 0))
    const2 = lambda s: pl.BlockSpec(s, lambda b, j: (0, 0))
    in_specs = [
        tile(D), const2((1, D)), const2((D, IN_WIDTH_PAD)),
        pl.BlockSpec((3, tm, LANES), lambda b, j: (0, j, 0)),
        pl.BlockSpec((3, tm, LANES), lambda b, j: (0, j, 0)),
        pl.BlockSpec((1, HIST_ROWS, POOL_WIDTH), lambda b, j: (b, 0, 0)),
        pl.BlockSpec((len(POOL_WINDOWS), POOL_GROUP_DIM, POOL_GROUP_DIM), lambda b, j: (0, 0, 0)),
        const2((1, POOL_WIDTH)),
    ]
    out_specs = (
        heads(N_HEADS, HEAD_DIM), heads(N_IDX_HEADS, IDX_DIM), tile(KV_WIDTH), tile(KV_WIDTH), tile(LANES),
        heads(N_KV_HEADS, HEAD_DIM),
        pl.BlockSpec((1, tm // LANES, KV_WIDTH, LANES), lambda b, j: (b, j, 0, 0)) if v_t else tile(KV_WIDTH),
        tile(IDX_DIM), tile(POOL_WIDTH),
        pl.BlockSpec((1, HIST_ROWS, POOL_WIDTH), lambda b, j: (b, 0, 0)),
    )
    return pl.pallas_call(
        functools.partial(_inproj_kernel, tm=tm, pos0=pos0),
        out_shape=out_shape, grid=(B, nt), in_specs=in_specs, out_specs=out_specs,
        scratch_shapes=[pltpu.VMEM((HIST_ROWS + tm, POOL_WIDTH), F32)],
        compiler_params=pltpu.CompilerParams(dimension_semantics=("parallel", "arbitrary"),
                                             vmem_limit_bytes=VMEM_LIMIT),
        name="inproj",
    )(x, g, w_bf, rope, ropeki, hist, pw_bf, ps)


def _dsa_kernel(q_ref, qi_ref, kiwi_ref, k_ref, vt_ref, ki_ref, x_ref, yp_ref, wo_ref, *rest,
                tk, pos0, n_keys, n_sel):
    o_ref, key_buf, bias_buf, m_scr, l_scr, acc_scr, s_scr = rest[-7:]
    tq = LANES
    slab = COUNT_SLAB
    j = pl.program_id(1)
    base = pos0 + j * tq
    pos = base + lax.broadcasted_iota(I32, (1, tq), 1)
    limit = jnp.minimum((pos // CHUNK + 1) * CHUNK, n_keys)
    limit_max = jnp.minimum(((base + tq - 1) // CHUNK + 1) * CHUNK, n_keys)
    nkb = (limit_max + tk - 1) // tk
    nt = (((1,), (1,)), ((), ()))

    wi_t = jnp.transpose(kiwi_ref[0])[IDX_DIM:IDX_DIM + N_IDX_HEADS, :]

    def score_block(kb, carry):
        off = pl.multiple_of(kb * tk, tk)
        kiblk = ki_ref[0, pl.ds(off, tk), :]
        idx = jnp.zeros((tk, tq), F32)
        for h in range(N_IDX_HEADS):
            sc = lax.dot_general(kiblk, qi_ref[0, h], nt, preferred_element_type=F32)
            idx = idx + jnp.maximum(sc, 0.0) * wi_t[h:h + 1, :]
        idx = jnp.where(idx == 0.0, 0.0, idx)
        bits = lax.bitcast_convert_type(idx, I32)
        key = bits ^ ((bits >> 31) & 0x7FFFFFFF)
        key_buf[kb] = jnp.where(lax.broadcasted_iota(I32, (tk, tq), 0) < limit - kb * tk, key, INT_MIN)
        return carry

    lax.fori_loop(0, nkb, score_block, 0)

    def col_sum(a):
        return jnp.sum(a, axis=0, keepdims=True)

    def count_ge(cand):
        def body(kb, acc):
            kblk = key_buf[kb]
            for c in range(tk // slab):
                acc = acc + jnp.where(kblk[c * slab:(c + 1) * slab] >= cand, 1.0, 0.0)
            return acc
        return col_sum(lax.fori_loop(0, nkb, body, jnp.zeros((slab, tq), F32)))

    kf = float(n_sel)
    cnt0 = count_ge(jnp.zeros((1, tq), I32))
    t0 = jnp.where(cnt0 >= kf, 0, INT_MIN).astype(I32)

    def bit_body(i, carry):
        t, cnt = carry
        cand = t | lax.shift_left(jnp.int32(1), 30 - i)
        cnt_cand = count_ge(cand)
        keep = cnt_cand >= kf
        return jnp.where(keep, cand, t), jnp.where(keep, cnt_cand, cnt)

    t, cnt_ge = lax.fori_loop(0, 31, bit_body, (t0, cnt0))
    cnt_gt = count_ge(t + 1)
    need = kf - cnt_gt
    cnt_eq = cnt_ge - cnt_gt
    overfull = jnp.where(t != INT_MIN, cnt_eq - need, 0.0)
    slow = jnp.max(overfull) > 0.0

    @pl.when(jnp.logical_not(slow))
    def _():
        t_adm = jnp.maximum(t, INT_MIN + 1)

        def body(kb, carry):
            bias_buf[kb] = jnp.where(key_buf[kb] >= t_adm, 0.0, NEG_BIG)
            return carry
        lax.fori_loop(0, nkb, body, 0)

    @pl.when(slow)
    def _():
        tri = jnp.where(lax.broadcasted_iota(I32, (tk, tk), 1) <= lax.broadcasted_iota(I32, (tk, tk), 0),
                        1.0, 0.0).astype(BF16)

        def body(kb, seen):
            kblk = key_buf[kb]
            eq = jnp.where((kblk == t) & (kblk != INT_MIN), 1.0, 0.0)
            prefix = jnp.dot(tri, eq.astype(BF16), preferred_element_type=F32) + seen
            keep_tie = jnp.where(prefix <= need, eq, 0.0)
            sel = jnp.where(kblk > t, 1.0, keep_tie)
            bias_buf[kb] = jnp.where(sel > 0.0, 0.0, NEG_BIG)
            return seen + col_sum(eq)
        lax.fori_loop(0, nkb, body, jnp.zeros((1, tq), F32))

    m_scr[...] = jnp.full(m_scr.shape, NEG_BIG, F32)
    l_scr[...] = jnp.zeros(l_scr.shape, F32)
    acc_scr[...] = jnp.zeros(acc_scr.shape, F32)
    sub = LANES
    nsub = tk // sub

    def attn_block(kb, carry):
        for c in range(nsub):
            off = pl.multiple_of(kb * tk + c * sub, sub)
            for g in range(N_KV_HEADS):
                s_scr[c, g] = lax.dot_general(
                    k_ref[0, g, pl.ds(off, sub), :],
                    q_ref[0, Q_PER_KV * g:Q_PER_KV * (g + 1)].reshape(Q_PER_KV * tq, HEAD_DIM),
                    nt, preferred_element_type=F32)
        m = [m_scr[g] for g in range(N_KV_HEADS)]
        l = [l_scr[g] for g in range(N_KV_HEADS)]
        for c in range(nsub):
            bias = bias_buf[kb, c * sub:(c + 1) * sub, :]
            bias2 = jnp.concatenate([bias] * Q_PER_KV, axis=1)
            for g in range(N_KV_HEADS):
                s = s_scr[c, g] + bias2
                m_new = jnp.maximum(m[g], jnp.max(s, axis=0, keepdims=True))
                alpha = jnp.exp2(m[g] - m_new)
                p = jnp.exp2(s - m_new)
                l[g] = alpha * l[g] + col_sum(p)
                pv = jnp.dot(vt_ref[0, kb * nsub + c, g * HEAD_DIM:(g + 1) * HEAD_DIM, :], p.astype(BF16),
                             preferred_element_type=F32)
                acc_scr[g] = alpha * acc_scr[g] + pv
                m[g] = m_new
        for g in range(N_KV_HEADS):
            m_scr[g] = m[g]
            l_scr[g] = l[g]
        return carry

    lax.fori_loop(0, nkb, attn_block, 0)

    o_t = []
    for g in range(N_KV_HEADS):
        og = acc_scr[g] / l_scr[g]
        o_t.extend(og[:, hh * tq:(hh + 1) * tq] for hh in range(Q_PER_KV))
    y_att = jnp.transpose(jnp.concatenate(o_t, axis=0)).astype(BF16)
    y = jnp.dot(yp_ref[0], wo_ref[0:POOL_WIDTH, :], preferred_element_type=F32)
    y = y + jnp.dot(y_att, wo_ref[POOL_WIDTH:POOL_WIDTH + ATT_WIDTH, :], preferred_element_type=F32)
    o_ref[...] = (x_ref[0] + y).reshape(o_ref.shape)


def _dsa(q_hm, qi_hm, kiwi, k_hm, v_t, ki_all, x, yp, wo_bf, *, tk, pos0, n_keys, n_sel, out_base=None):
    B, T, D = x.shape
    L = k_hm.shape[2]
    tq = LANES
    assert L % tk == 0 and T % tq == 0
    nkb_max = L // tk
    tile = lambda wdt: pl.BlockSpec((1, tq, wdt), lambda b, j: (b, j, 0))
    in_specs = [pl.BlockSpec((1, N_HEADS, tq, HEAD_DIM), lambda b, j: (b, 0, j, 0)),
                pl.BlockSpec((1, N_IDX_HEADS, tq, IDX_DIM), lambda b, j: (b, 0, j, 0)),
                tile(LANES),
                pl.BlockSpec((1, N_KV_HEADS, L, HEAD_DIM), lambda b, j: (b, 0, 0, 0)),
                pl.BlockSpec((1, L // LANES, KV_WIDTH, LANES), lambda b, j: (b, 0, 0, 0)),
                pl.BlockSpec((1, L, IDX_DIM), lambda b, j: (b, 0, 0)),
                tile(D), tile(POOL_WIDTH), pl.BlockSpec((POOL_WIDTH + ATT_WIDTH, D), lambda b, j: (0, 0))]
    nq = T // tq
    args = (q_hm, qi_hm, kiwi, k_hm, v_t, ki_all, x, yp, wo_bf)
    if out_base is None:
        out_shape, out_spec, aliases = jax.ShapeDtypeStruct((B, T, D), F32), tile(D), {}
    else:
        out_shape = jax.ShapeDtypeStruct(out_base.shape, F32)
        out_spec = pl.BlockSpec((tq, D), lambda b, j: (b * nq + j, 0))
        in_specs, args, aliases = in_specs + [pl.BlockSpec(memory_space=pl.ANY)], args + (out_base,), {len(args): 0}
    return pl.pallas_call(
        functools.partial(_dsa_kernel, tk=tk, pos0=pos0, n_keys=n_keys, n_sel=n_sel),
        out_shape=out_shape, grid=(B, nq), in_specs=in_specs, out_specs=out_spec, input_output_aliases=aliases,
        scratch_shapes=[pltpu.VMEM((nkb_max, tk, tq), I32), pltpu.VMEM((nkb_max, tk, tq), F32),
                        pltpu.VMEM((N_KV_HEADS, 1, Q_PER_KV * tq), F32),
                        pltpu.VMEM((N_KV_HEADS, 1, Q_PER_KV * tq), F32),
                        pltpu.VMEM((N_KV_HEADS, HEAD_DIM, Q_PER_KV * tq), F32),
                        pltpu.VMEM((tk // LANES, N_KV_HEADS, LANES, Q_PER_KV * tq), F32)],
        compiler_params=pltpu.CompilerParams(dimension_semantics=("parallel", "arbitrary"),
                                             vmem_limit_bytes=VMEM_LIMIT),
        name="dsa",
    )(*args)


def _gmlp_kernel(x_ref, g_ref, win_ref, lng_ref, lnb_ref, ws_ref, bias_ref, wout_ref, o_ref, v_ref, *, tm):
    x = x_ref[...]
    h = _rms(x, g_ref[...])
    z = jax.nn.gelu(jnp.dot(h.astype(BF16), win_ref[...], preferred_element_type=F32))
    half = z.shape[1] // 2
    u, v = z[:, :half], z[:, half:]
    mu = jnp.mean(v, axis=-1, keepdims=True)
    var = jnp.mean(jnp.square(v - mu), axis=-1, keepdims=True)
    vn = (v - mu) * lax.rsqrt(var + LN_EPS) * lng_ref[...] + lnb_ref[...]
    v_ref[...] = vn
    gd = half // GM_GROUPS
    gated = []
    for c in range(tm // GM_CHUNK):
        rows = slice(c * GM_CHUNK, (c + 1) * GM_CHUNK)
        vc = vn[rows].astype(BF16)
        mixed = jnp.concatenate(
            [jnp.dot(ws_ref[0, g], vc[:, g * gd:(g + 1) * gd], preferred_element_type=F32)
             for g in range(GM_GROUPS)], axis=1) + bias_ref[0]
        gated.append((u[rows] * mixed).astype(BF16))
    gated = jnp.concatenate(gated, axis=0)
    o_ref[...] = x + jnp.dot(gated, wout_ref[...], preferred_element_type=F32)


def _gmlp(x, g, win_bf, lng, lnb, ws2, bias2, wout_bf, *, tm, n_first, n_v_rows):
    N, D = x.shape
    half = win_bf.shape[1] // 2
    nt = N // tm
    t_first = n_first // tm
    variant = lambda i: jnp.where(i >= t_first, 1, 0)
    row = pl.BlockSpec((tm, D), lambda i: (i, 0))
    const = lambda s: pl.BlockSpec(s, lambda i: (0, 0))
    in_specs = [row, const((1, D)), const((D, 2 * half)), const((1, half)), const((1, half)),
                pl.BlockSpec((1, GM_GROUPS, GM_CHUNK, GM_CHUNK), lambda i: (variant(i), 0, 0, 0)),
                pl.BlockSpec((1, GM_CHUNK, half), lambda i: (variant(i), 0, 0)),
                const((half, D))]
    out_specs = (row, pl.BlockSpec((tm, half), lambda i: (jnp.maximum(i - t_first, 0), 0)))
    return pl.pallas_call(
        functools.partial(_gmlp_kernel, tm=tm),
        out_shape=(jax.ShapeDtypeStruct((N, D), F32), jax.ShapeDtypeStruct((n_v_rows, half), F32)),
        grid=(nt,), in_specs=in_specs, out_specs=out_specs,
        compiler_params=pltpu.CompilerParams(dimension_semantics=("arbitrary",), vmem_limit_bytes=VMEM_LIMIT),
        name="gmlp",
    )(x, g, win_bf, lng, lnb, ws2, bias2, wout_bf)


ROUTE_E0, ROUTE_E1, ROUTE_R0, ROUTE_R1, ROUTE_G0, ROUTE_G1 = range(6)
ROUTE_ROWS = 8
N_LOGITS = N_EXPERT_GROUPS + N_EXPERTS
LOGIT_ROWS = -(-N_LOGITS // SUBLANES) * SUBLANES


def _router_kernel(x_ref, g_ref, wr_ref, route_ref, cnt_ref, carry_ref, *, tm):
    i = pl.program_id(0)

    @pl.when(i == 0)
    def _():
        carry_ref[...] = jnp.zeros(carry_ref.shape, F32)

    h = _rms(x_ref[...], g_ref[...])
    nt = (((1,), (1,)), ((), ()))
    logits = lax.dot_general(wr_ref[...], h.astype(BF16), nt, preferred_element_type=F32)[:LOGIT_ROWS]
    row = lax.broadcasted_iota(I32, (LOGIT_ROWS, tm), 0).astype(F32)
    ninf = -jnp.inf
    big = float(LANES)
    cmax = lambda a: jnp.max(a, axis=0, keepdims=True)
    cmin = lambda a: jnp.min(a, axis=0, keepdims=True)
    csum = lambda a: jnp.sum(a, axis=0, keepdims=True)

    is_grp = row < N_EXPERT_GROUPS
    lg = jnp.where(is_grp, logits, ninf)
    mg = cmax(lg)
    g_sel = cmin(jnp.where(lg == mg, row, big))
    p_grp = 1.0 / csum(jnp.where(is_grp, jnp.exp(lg - mg), 0.0))
    lo = N_EXPERT_GROUPS + g_sel * EXPERTS_PER_GROUP
    le = jnp.where((row >= lo) & (row < lo + EXPERTS_PER_GROUP), logits, ninf)
    v1 = cmax(le)
    j1 = cmin(jnp.where(le == v1, row, big))
    le2 = jnp.where(row == j1, ninf, le)
    v2 = cmax(le2)
    j2 = cmin(jnp.where(le2 == v2, row, big))
    e0 = j1 - N_EXPERT_GROUPS
    e1 = j2 - N_EXPERT_GROUPS
    r = jnp.exp(v2 - v1)
    g0 = p_grp / (1.0 + r)
    g1 = p_grp * r / (1.0 + r)

    erow = lax.broadcasted_iota(I32, (N_EXPERTS, tm), 0).astype(F32)
    oh0 = jnp.where(erow == e0, 1.0, 0.0)
    oh1 = jnp.where(erow == e1, 1.0, 0.0)
    oh = oh0 + oh1
    earlier = jnp.where(lax.broadcasted_iota(I32, (tm, tm), 0) < lax.broadcasted_iota(I32, (tm, tm), 1),
                        1.0, 0.0).astype(BF16)
    before = jnp.dot(oh.astype(BF16), earlier, preferred_element_type=F32) + carry_ref[...]
    r0 = csum(oh0 * before)
    r1 = csum(oh1 * before)
    carry_ref[...] = carry_ref[...] + jnp.sum(oh, axis=1, keepdims=True)
    cnt_ref[...] = jnp.broadcast_to(carry_ref[...], cnt_ref.shape)

    rows = {ROUTE_E0: e0, ROUTE_E1: e1, ROUTE_R0: r0, ROUTE_R1: r1, ROUTE_G0: g0, ROUTE_G1: g1}
    zero = jnp.zeros((1, tm), F32)
    route_ref[...] = jnp.concatenate([rows.get(k, zero) for k in range(ROUTE_ROWS)], axis=0)


def _router(x, g, wr_t, *, tm):
    N, D = x.shape
    const = lambda s: pl.BlockSpec(s, lambda i: (0, 0))
    return pl.pallas_call(
        functools.partial(_router_kernel, tm=tm),
        out_shape=(jax.ShapeDtypeStruct((ROUTE_ROWS, N), F32), jax.ShapeDtypeStruct((N_EXPERTS, LANES), F32)),
        grid=(N // tm,), in_specs=[pl.BlockSpec((tm, D), lambda i: (i, 0)), const((1, D)), const((LANES, D))],
        out_specs=(pl.BlockSpec((ROUTE_ROWS, tm), lambda i: (0, i)), const((N_EXPERTS, LANES))),
        scratch_shapes=[pltpu.VMEM((N_EXPERTS, 1), F32)],
        compiler_params=pltpu.CompilerParams(dimension_semantics=("arbitrary",), vmem_limit_bytes=VMEM_LIMIT),
        name="router",
    )(x, g, wr_t)


def _pack_bf16_pairs(h):
    half = h.shape[1] // 2
    lo = lax.bitcast_convert_type(h[:, :half].astype(BF16).astype(F32), U32)
    hi = lax.bitcast_convert_type(h[:, half:].astype(BF16).astype(F32), U32)
    return (lo >> 16) | (hi & jnp.uint32(0xFFFF0000))


def _unpack_bf16_pairs(w):
    lo = lax.bitcast_convert_type(w << 16, F32).astype(BF16)
    hi = lax.bitcast_convert_type(w & jnp.uint32(0xFFFF0000), F32).astype(BF16)
    return jnp.concatenate([lo, hi], axis=1)


def _dispatch_kernel(d0_ref, d1_ref, pad_ref, x_ref, g_ref, rows_ref, h_scr, zblk, sems, *, tm):
    i = pl.program_id(0)
    n = pl.num_programs(0)
    slot = i % 2
    groups = tm // SUBLANES

    def wait_slot(s):
        for _ in range(2 * groups):
            pltpu.make_async_copy(h_scr.at[s, 0], rows_ref.at[pl.ds(0, SUBLANES)], sems.at[s]).wait()

    @pl.when(i >= 2)
    def _():
        wait_slot(slot)

    h_scr[slot] = _pack_bf16_pairs(_rms(x_ref[...], g_ref[...])).reshape(groups, SUBLANES, h_scr.shape[3])

    def body(k, carry):
        for u in range(SUBLANES):
            r = k * SUBLANES + u
            src = h_scr.at[slot, k, pl.ds(u, 1)]
            pltpu.make_async_copy(src, rows_ref.at[pl.ds(d0_ref[r], 1)], sems.at[slot]).start()
            pltpu.make_async_copy(src, rows_ref.at[pl.ds(d1_ref[r], 1)], sems.at[slot]).start()
        return carry

    lax.fori_loop(0, groups, body, 0)

    @pl.when(i == n - 1)
    def _():
        wait_slot(slot)

    @pl.when(jnp.logical_and(i == n - 1, n >= 2))
    def _():
        wait_slot(1 - slot)

    @pl.when(i == n - 1)
    def _():
        zblk[...] = jnp.zeros(zblk.shape, U32)
        zrow = zblk.at[pl.ds(0, 1)]
        for e in range(N_EXPERTS):
            first, count = pad_ref[0, e], pad_ref[1, e]

            def start(k, carry):
                pltpu.make_async_copy(zrow, rows_ref.at[pl.ds(first + k, 1)], sems.at[2]).start()
                return carry

            def wait(k, carry):
                pltpu.make_async_copy(zrow, rows_ref.at[pl.ds(first, 1)], sems.at[2]).wait()
                return carry

            lax.fori_loop(0, count, start, 0)
            lax.fori_loop(0, count, wait, 0)

        def blk_copy(b):
            return pltpu.make_async_copy(zblk, rows_ref.at[pl.ds(b * MOE_BLOCK, MOE_BLOCK)], sems.at[2])

        n_blocks = rows_ref.shape[0] // MOE_BLOCK
        n_used = pad_ref[2, 0]
        lax.fori_loop(n_used, n_blocks, lambda b, c: (blk_copy(b).start(), c)[1], 0)
        lax.fori_loop(n_used, n_blocks, lambda b, c: (blk_copy(n_used).wait(), c)[1], 0)


def _dispatch(dest_flat, pad_info, x, g, *, tm, n_rows):
    N, D = x.shape
    nt = N // tm
    return pl.pallas_call(
        functools.partial(_dispatch_kernel, tm=tm),
        out_shape=jax.ShapeDtypeStruct((n_rows, D // 2), U32), grid=(nt,),
        in_specs=[pl.BlockSpec((tm,), lambda i: (i,), memory_space=pltpu.SMEM),
                  pl.BlockSpec((tm,), lambda i: (i + nt,), memory_space=pltpu.SMEM),
                  pl.BlockSpec(memory_space=pltpu.SMEM),
                  pl.BlockSpec((tm, D), lambda i: (i, 0)),
                  pl.BlockSpec((1, D), lambda i: (0, 0))],
        out_specs=pl.BlockSpec(memory_space=pl.ANY),
        scratch_shapes=[pltpu.VMEM((2, tm // SUBLANES, SUBLANES, D // 2), U32), pltpu.VMEM((MOE_BLOCK, D // 2), U32),
                        pltpu.SemaphoreType.DMA((3,))],
        compiler_params=pltpu.CompilerParams(dimension_semantics=("arbitrary",), vmem_limit_bytes=VMEM_LIMIT),
        name="dispatch",
    )(dest_flat, dest_flat, pad_info, x, g)


def _expert_kernel(be_ref, nu_ref, x_ref, wg_ref, wu_ref, wd_ref, y_ref, wg_bf, wu_bf, wd_bf):
    i = pl.program_id(0)

    @pl.when(jnp.logical_or(i == 0, be_ref[i] != be_ref[jnp.maximum(i - 1, 0)]))
    def _():
        wg_bf[...] = wg_ref[0, 0].astype(BF16)
        wu_bf[...] = wu_ref[0, 0].astype(BF16)
        wd_bf[...] = wd_ref[0, 0].astype(BF16)

    @pl.when(i < nu_ref[0])
    def _():
        x = _unpack_bf16_pairs(x_ref[...])
        a = jnp.dot(x, wg_bf[...], preferred_element_type=F32)
        u = jnp.dot(x, wu_bf[...], preferred_element_type=F32)
        act = (a * jax.nn.sigmoid(a)) * u
        y_ref[...] = jnp.dot(act.astype(BF16), wd_bf[...], preferred_element_type=F32)

    @pl.when(i >= nu_ref[0])
    def _():
        y_ref[...] = jnp.zeros(y_ref.shape, F32)


def _experts(block_e, n_used, x_rows, w_gate, w_up, w_down, layer):
    n_rows = x_rows.shape[0]
    D, de = w_gate.shape[2:]
    n_blocks = n_rows // MOE_BLOCK
    grid_spec = pltpu.PrefetchScalarGridSpec(
        num_scalar_prefetch=2, grid=(n_blocks,),
        in_specs=[pl.BlockSpec((MOE_BLOCK, D // 2), lambda i, be, nu: (jnp.minimum(i, nu[0] - 1), 0)),
                  pl.BlockSpec((1, 1, D, de), lambda i, be, nu: (layer, be[i], 0, 0)),
                  pl.BlockSpec((1, 1, D, de), lambda i, be, nu: (layer, be[i], 0, 0)),
                  pl.BlockSpec((1, 1, de, D), lambda i, be, nu: (layer, be[i], 0, 0))],
        out_specs=pl.BlockSpec((MOE_BLOCK, D), lambda i, be, nu: (i, 0)),
        scratch_shapes=[pltpu.VMEM((D, de), BF16), pltpu.VMEM((D, de), BF16), pltpu.VMEM((de, D), BF16)])
    return pl.pallas_call(
        _expert_kernel, out_shape=jax.ShapeDtypeStruct((n_rows, D), F32), grid_spec=grid_spec,
        compiler_params=pltpu.CompilerParams(dimension_semantics=("arbitrary",), vmem_limit_bytes=VMEM_LIMIT),
        name="experts",
    )(block_e, n_used, x_rows, w_gate, w_up, w_down)


def _combine_kernel(dc0_ref, dc1_ref, dn0_ref, dn1_ref, x_ref, route_ref, g_ref, y_ref, *rest,
                    tm, final_norm, t_first):
    *o_refs, ybuf, sems = rest
    i = pl.program_id(0)
    n = pl.num_programs(0)
    slot = i % 2
    groups = tm // SUBLANES
    D = x_ref.shape[1]

    def gather(d_refs, s):
        def body(k, carry):
            for u in range(SUBLANES):
                r = k * SUBLANES + u
                for a in range(2):
                    pltpu.make_async_copy(y_ref.at[pl.ds(d_refs[a][r], 1)], ybuf.at[s, a, k, pl.ds(u, 1)],
                                          sems.at[s]).start()
            return carry
        lax.fori_loop(0, groups, body, 0)

    @pl.when(i == 0)
    def _():
        gather((dc0_ref, dc1_ref), 0)

    @pl.when(i + 1 < n)
    def _():
        gather((dn0_ref, dn1_ref), 1 - slot)

    for _ in range(2 * groups):
        pltpu.make_async_copy(y_ref.at[pl.ds(0, SUBLANES)], ybuf.at[slot, 0, 0], sems.at[slot]).wait()
    route = jnp.transpose(route_ref[...])
    g0 = route[:, ROUTE_G0:ROUTE_G0 + 1]
    g1 = route[:, ROUTE_G1:ROUTE_G1 + 1]
    out = x_ref[...] + (ybuf[slot, 0].reshape(tm, D) * g0 + ybuf[slot, 1].reshape(tm, D) * g1)
    if final_norm:
        out = _rms(out, g_ref[...])
    if t_first is None:
        o_refs[0][...] = out
    else:
        @pl.when(i < t_first)
        def _():
            o_refs[0][...] = out

        @pl.when(i >= t_first)
        def _():
            o_refs[1][...] = out


def _combine(dest_flat, x, route, g, y_rows, *, tm, final_norm, n_first=None):
    N, D = x.shape
    nt = N // tm
    row = lambda wdt: pl.BlockSpec((tm, wdt), lambda i: (i, 0))
    nxt = lambda i: jnp.minimum(i + 1, nt - 1)
    smem = lambda f: pl.BlockSpec((tm,), f, memory_space=pltpu.SMEM)
    if n_first is None:
        t_first, out_shape, out_specs = None, jax.ShapeDtypeStruct((N, D), F32), row(D)
    else:
        t_first = n_first // tm
        out_shape = (jax.ShapeDtypeStruct((n_first, D), F32), jax.ShapeDtypeStruct((N - n_first, D), F32))
        out_specs = (pl.BlockSpec((tm, D), lambda i: (jnp.minimum(i, t_first - 1), 0)),
                     pl.BlockSpec((tm, D), lambda i: (jnp.maximum(i - t_first, 0), 0)))
    return pl.pallas_call(
        functools.partial(_combine_kernel, tm=tm, final_norm=final_norm, t_first=t_first),
        out_shape=out_shape, grid=(nt,),
        in_specs=[smem(lambda i: (i,)), smem(lambda i: (i + nt,)),
                  smem(lambda i: (nxt(i),)), smem(lambda i: (nxt(i) + nt,)),
                  row(D), pl.BlockSpec((ROUTE_ROWS, tm), lambda i: (0, i)), pl.BlockSpec((1, D), lambda i: (0, 0)),
                  pl.BlockSpec(memory_space=pl.ANY)],
        out_specs=out_specs,
        scratch_shapes=[pltpu.VMEM((2, 2, tm // SUBLANES, SUBLANES, D), F32), pltpu.SemaphoreType.DMA((2,))],
        compiler_params=pltpu.CompilerParams(dimension_semantics=("arbitrary",), vmem_limit_bytes=VMEM_LIMIT),
        name="combine",
    )(dest_flat, dest_flat, dest_flat, dest_flat, x, route, g, y_rows)


def _hier_moe(x, g_ffn, w_rg, w_re, w_gate, w_up, w_down, layer, g_final, *, final_norm, n_first=None):
    N, D = x.shape
    tm = TOKEN_TILE
    wr_t = jnp.concatenate([w_rg.T, w_re.T, jnp.zeros((LANES - N_LOGITS, D), F32)], axis=0).astype(BF16)
    g_ffn = g_ffn.reshape(1, D)
    route, counts = _router(x, g_ffn, wr_t, tm=tm)
    counts = counts[:, 0].astype(I32)
    eid_t = route[ROUTE_E0:ROUTE_E1 + 1].astype(I32)
    rank_t = route[ROUTE_R0:ROUTE_R1 + 1].astype(I32)
    padded = (counts + MOE_BLOCK - 1) // MOE_BLOCK * MOE_BLOCK
    pad_end = jnp.cumsum(padded)
    pad_start = pad_end - padded
    expert_ids = jnp.arange(N_EXPERTS, dtype=I32)
    start_of = jnp.sum(jnp.where(eid_t[:, :, None] == expert_ids, pad_start, 0), axis=-1)
    dest_t = start_of + rank_t
    n_blocks = -(-(2 * N) // MOE_BLOCK) + N_EXPERTS
    n_rows = n_blocks * MOE_BLOCK
    n_used = (pad_end[-1] // MOE_BLOCK).astype(I32)
    block_start = jnp.minimum(jnp.arange(n_blocks, dtype=I32), n_used - 1) * MOE_BLOCK
    block_e = jnp.minimum(jnp.sum(block_start[:, None] >= pad_end[None, :], axis=1), N_EXPERTS - 1).astype(I32)
    pad_info = jnp.stack([pad_start + counts, padded - counts, jnp.broadcast_to(n_used, (N_EXPERTS,))])
    dest_flat = dest_t.reshape(-1)
    x_rows = _dispatch(dest_flat, pad_info, x, g_ffn, tm=tm, n_rows=n_rows)
    y_rows = _experts(block_e, n_used.reshape(1), x_rows, w_gate, w_up, w_down, layer)
    return _combine(dest_flat, x, route, g_final.reshape(1, D), y_rows, tm=tm, final_norm=final_norm,
                    n_first=n_first)


def _rope_tables(pos, rot_lanes):
    inv = 1.0 / (ROPE_THETA ** (jnp.arange(ROT_HALF, dtype=F32) / ROT_HALF))
    ang = pos.astype(F32)[:, None] * inv[None, :]
    cos, sin = jnp.cos(ang), jnp.sin(ang)
    lane = jnp.arange(LANES)
    r = lane % HEAD_DIM
    active = (lane < rot_lanes)
    first = active & (r < ROT_HALF)
    second = active & (r >= ROT_HALF) & (r < 2 * ROT_HALF)
    cos_l = cos[:, r % ROT_HALF]
    sin_l = sin[:, r % ROT_HALF]
    c = jnp.where((first | second)[None, :], cos_l, 1.0)
    sa = jnp.where(first[None, :], -sin_l, 0.0)
    sb = jnp.where(second[None, :], sin_l, 0.0)
    return jnp.stack([c, sa, sb]).astype(F32)


def _prep_w_in(w_in):
    D = w_in.shape[0]
    assert w_in.shape[1] == sum(IN_SIZES)
    pad = jnp.zeros((D, IN_WIDTH_PAD - w_in.shape[1]), w_in.dtype)
    return jnp.concatenate([w_in, pad], axis=1).astype(BF16)


def kernel(x_prompt, x_sample, cache_k, cache_v, cache_idx_k, state_pool, norm_mix, norm_ffn, norm_final,
           par_w_in, par_pool_w, par_pool_scale, par_w_out, gm_w_in, gm_ln_g, gm_ln_b, gm_ws, gm_bs, gm_w_out,
           moe_router_group, moe_router_expert, moe_w_gate, moe_w_up, moe_w_down):
    Bp, Tp, D = x_prompt.shape
    Bs, Ts, _ = x_sample.shape
    past = cache_k.shape[2]
    Np, Ns = Bp * Tp, Bs * Ts
    depth = norm_mix.shape[0]
    assert depth == 2 and Ts == CHUNK and Tp % PROJ_TILE == 0 and Tp % TOKEN_TILE == 0 and Ns % TOKEN_TILE == 0

    w_in_bf = _prep_w_in(par_w_in[0])
    pw_bf = par_pool_w[0].astype(BF16)
    ps = par_pool_scale[0].reshape(1, POOL_WIDTH)
    wo_bf = par_w_out[0].astype(BF16)
    g_mix0 = norm_mix[0].reshape(1, D)
    pos_p = jnp.arange(Tp, dtype=I32)
    pos_s = past + jnp.arange(Ts, dtype=I32)

    hist_p = jnp.zeros((Bp, HIST_ROWS, POOL_WIDTH), F32)
    hist_s = jnp.pad(state_pool[0], ((0, 0), (1, 0), (0, 0)))
    tk = KEY_BLOCK
    (q_p, qi_p, k_p, v_p, kiwi_p, kbf_p, vbf_p, kibf_p, yp_p, st_p) = _inproj(
        x_prompt, g_mix0, w_in_bf, _rope_tables(pos_p, LANES), _rope_tables(pos_p, IDX_DIM), hist_p, pw_bf, ps,
        tm=PROJ_TILE, pos0=0)
    (q_s, qi_s, k_s, v_s, kiwi_s, kbf_s, vbf_s, kibf_s, yp_s, st_s) = _inproj(
        x_sample, g_mix0, w_in_bf, _rope_tables(pos_s, LANES), _rope_tables(pos_s, IDX_DIM), hist_s, pw_bf, ps,
        tm=Ts, pos0=past)

    ls = past + Ts
    lsp = -(-ls // tk) * tk
    kpad = lambda a, ax: jnp.pad(a, [(0, lsp - ls) if d == ax else (0, 0) for d in range(a.ndim)])
    kall = kpad(jnp.concatenate([jnp.transpose(cache_k[0], (0, 2, 1, 3)).astype(BF16), kbf_s], axis=2), 2)
    vall = kpad(jnp.concatenate([cache_v[0].reshape(Bs, past, KV_WIDTH).astype(BF16), vbf_s], axis=1), 1)
    vall_t = jnp.transpose(vall.reshape(Bs, lsp // LANES, LANES, KV_WIDTH), (0, 1, 3, 2))
    kiall = kpad(jnp.concatenate([cache_idx_k[0].astype(BF16), kibf_s], axis=1), 1)
    qpad = lambda a, ax: jnp.pad(a, [(0, LANES - Ts) if d == ax else (0, 0) for d in range(a.ndim)])
    x1_s = _dsa(qpad(q_s, 2), qpad(qi_s, 2), qpad(kiwi_s, 1), kall, vall_t, kiall, qpad(x_sample, 1), qpad(yp_s, 1),
                wo_bf, tk=tk, pos0=past, n_keys=ls, n_sel=min(TOPK_MAX, ls // 4))[:, :Ts]
    assert Tp % tk == 0
    x = jnp.pad(x1_s.reshape(Ns, D), ((Np, 0), (0, 0)))
    x = _dsa(q_p, qi_p, kiwi_p, kbf_p, vbf_p, kibf_p, x_prompt, yp_p, wo_bf,
             tk=tk, pos0=0, n_keys=Tp, n_sel=min(TOPK_MAX, Tp // 4), out_base=x)
    x = _hier_moe(x, norm_ffn[0], moe_router_group[0], moe_router_expert[0],
                  moe_w_gate, moe_w_up, moe_w_down, 0, norm_final, final_norm=False)

    cs = Ts
    tril = lambda n: jnp.tril(jnp.ones((n, n), bool))
    ws_p = jnp.where(tril(GM_CHUNK)[None], gm_ws[0], 0.0)
    ws_small = jnp.where(tril(cs)[None], gm_ws[0][:, :cs, :cs], 0.0)
    rep = GM_CHUNK // cs
    ws_s = jnp.einsum('ab,gts->gatbs', jnp.eye(rep, dtype=F32), ws_small).reshape(GM_GROUPS, GM_CHUNK, GM_CHUNK)
    ws2 = jnp.stack([ws_p, ws_s]).astype(BF16)
    gd = D // GM_GROUPS
    bias_p = jnp.repeat(jnp.transpose(gm_bs[0]), gd, axis=1)
    bias_s = jnp.tile(jnp.repeat(jnp.transpose(gm_bs[0][:, :cs]), gd, axis=1), (rep, 1))
    bias2 = jnp.stack([bias_p, bias_s])
    x, gm_v = _gmlp(x, norm_mix[1].reshape(1, D), gm_w_in[0].astype(BF16), gm_ln_g[0].reshape(1, D),
                    gm_ln_b[0].reshape(1, D), ws2, bias2, gm_w_out[0].astype(BF16),
                    tm=TOKEN_TILE, n_first=Np, n_v_rows=Ns)
    y_p, y_s = _hier_moe(x, norm_ffn[1], moe_router_group[1], moe_router_expert[1],
                         moe_w_gate, moe_w_up, moe_w_down, 1, norm_final, final_norm=True, n_first=Np)
    y_p = y_p.reshape(Bp, Tp, D)
    y_s = y_s.reshape(Bs, Ts, D)
    r4 = lambda a, b, t: a.reshape(1, b, t, N_KV_HEADS, HEAD_DIM)
    return (y_p, y_s,
            r4(k_p, Bp, Tp), r4(v_p, Bp, Tp), kiwi_p[:, :, :IDX_DIM][None], st_p[:, 1:][None],
            r4(k_s, Bs, Ts), r4(v_s, Bs, Ts), kiwi_s[:, :, :IDX_DIM][None], st_s[:, 1:][None],
            gm_v.reshape(1, Bs, Ts, D))
```

```python
import functools

import jax
import jax.numpy as jnp
from jax import lax
from jax.experimental import pallas as pl
from jax.experimental.pallas import tpu as pltpu

F32 = jnp.float32
BF16 = jnp.bfloat16
I32 = jnp.int32
U32 = jnp.uint32

LANES = 128
SUBLANES = 8
CHUNK = 64
POOL_WINDOWS = (2, 4, 8, 16)
POOL_GROUP_DIM = 128
POOL_WIDTH = 512
HIST_ROWS = 16
N_HEADS = 8
HEAD_DIM = 64
N_KV_HEADS = 4
Q_PER_KV = N_HEADS // N_KV_HEADS
ATT_WIDTH = N_HEADS * HEAD_DIM
KV_WIDTH = N_KV_HEADS * HEAD_DIM
N_IDX_HEADS = 8
IDX_DIM = 64
TOPK_MAX = 256
ROPE_THETA = 500000.0
ROT_HALF = HEAD_DIM // 8
GM_CHUNK = 128
GM_GROUPS = 8
N_EXPERT_GROUPS = 4
EXPERTS_PER_GROUP = 8
N_EXPERTS = 32
MOE_BLOCK = 512
RMS_EPS = 1e-6
LN_EPS = 1e-5

INT_MIN = -2147483648
LOG2_E = 1.4426950408889634
NEG_BIG = -1e30
VMEM_LIMIT = 48 * 1024 * 1024

TOKEN_TILE = 512
PROJ_TILE = 512
KEY_BLOCK = 512
COUNT_SLAB = 64

IN_SIZES = (POOL_WIDTH, ATT_WIDTH, KV_WIDTH, KV_WIDTH, N_IDX_HEADS * IDX_DIM, IDX_DIM, N_IDX_HEADS)
COL_XP, COL_Q, COL_K, COL_V, COL_QI, COL_KIWI = (sum(IN_SIZES[:i]) for i in range(6))
IN_WIDTH_PAD = COL_KIWI + LANES


def _rms(x, g):
    return x * lax.rsqrt(jnp.mean(x * x, axis=-1, keepdims=True) + RMS_EPS) * g


def _rope128(x, c, sa, sb):
    return x * c + pltpu.roll(x, LANES - ROT_HALF, 1) * sa + pltpu.roll(x, ROT_HALF, 1) * sb


def _inproj_kernel(x_ref, g_ref, w_ref, rope_ref, ropeki_ref, hist_ref, pw_ref, ps_ref,
                   q_ref, qi_ref, k_ref, v_ref, kiwi_ref, kbf_ref, vbf_ref, kibf_ref, yp_ref, state_ref,
                   buf_ref, *, tm, pos0):
    j = pl.program_id(1)
    h = _rms(x_ref[0], g_ref[...])
    proj = jnp.dot(h.astype(BF16), w_ref[...], preferred_element_type=F32)
    c, sa, sb = rope_ref[0], rope_ref[1], rope_ref[2]

    def put_heads(ref, i, chunk):
        ref[0, 2 * i] = chunk[:, :HEAD_DIM].astype(BF16)
        ref[0, 2 * i + 1] = chunk[:, HEAD_DIM:].astype(BF16)

    for i in range(ATT_WIDTH // LANES):
        put_heads(q_ref, i, _rope128(proj[:, COL_Q + i * LANES:COL_Q + (i + 1) * LANES], c, sa, sb)
                  * (HEAD_DIM ** -0.5 * LOG2_E))
        put_heads(qi_ref, i, _rope128(proj[:, COL_QI + i * LANES:COL_QI + (i + 1) * LANES], c, sa, sb))
    for i in range(KV_WIDTH // LANES):
        sl = slice(i * LANES, (i + 1) * LANES)
        kr = _rope128(proj[:, COL_K + i * LANES:COL_K + (i + 1) * LANES], c, sa, sb)
        k_ref[0, :, sl] = kr
        put_heads(kbf_ref, i, kr)
    vv = proj[:, COL_V:COL_V + KV_WIDTH]
    v_ref[0] = vv
    if tm % LANES == 0:
        for cc in range(tm // LANES):
            vbf_ref[0, cc] = jnp.transpose(vv[cc * LANES:(cc + 1) * LANES, :]).astype(BF16)
    else:
        vbf_ref[0] = vv.astype(BF16)
    kiwi = _rope128(proj[:, COL_KIWI:COL_KIWI + LANES], ropeki_ref[0], ropeki_ref[1], ropeki_ref[2])
    kiwi_ref[0] = kiwi
    kibf_ref[0] = kiwi[:, :IDX_DIM].astype(BF16)

    @pl.when(j == 0)
    def _():
        buf_ref[0:HIST_ROWS, :] = hist_ref[0]

    xp = proj[:, COL_XP:COL_XP + POOL_WIDTH]
    buf_ref[HIST_ROWS:HIST_ROWS + tm, :] = xp
    pos = pos0 + j * tm + lax.broadcasted_iota(I32, (tm, 1), 0)
    for gi, w in enumerate(POOL_WINDOWS):
        c0 = gi * POOL_GROUP_DIM
        s = xp[:, c0:c0 + POOL_GROUP_DIM]
        for i in range(1, w):
            s = s + buf_ref[HIST_ROWS - i:HIST_ROWS - i + tm, c0:c0 + POOL_GROUP_DIM]
        cnt = jnp.minimum(pos + 1, w).astype(F32)
        d = s / cnt - xp[:, c0:c0 + POOL_GROUP_DIM]
        y = jnp.dot(d.astype(BF16), pw_ref[gi], preferred_element_type=F32)
        yp_ref[0, :, c0:c0 + POOL_GROUP_DIM] = (y * ps_ref[:, c0:c0 + POOL_GROUP_DIM]).astype(BF16)
    tail = buf_ref[tm:tm + HIST_ROWS, :]
    state_ref[0] = tail
    buf_ref[0:HIST_ROWS, :] = tail


def _inproj(x, g, w_bf, rope, ropeki, hist, pw_bf, ps, *, tm, pos0):
    B, T, D = x.shape
    nt = T // tm
    f = lambda shape, dt: jax.ShapeDtypeStruct(shape, dt)
    v_t = tm % LANES == 0
    out_shape = (
        f((B, N_HEADS, T, HEAD_DIM), BF16), f((B, N_IDX_HEADS, T, IDX_DIM), BF16),
        f((B, T, KV_WIDTH), F32), f((B, T, KV_WIDTH), F32), f((B, T, LANES), F32),
        f((B, N_KV_HEADS, T, HEAD_DIM), BF16),
        f((B, T // LANES, KV_WIDTH, LANES) if v_t else (B, T, KV_WIDTH), BF16),
        f((B, T, IDX_DIM), BF16),
        f((B, T, POOL_WIDTH), BF16), f((B, HIST_ROWS, POOL_WIDTH), F32),
    )
    tile = lambda wdt: pl.BlockSpec((1, tm, wdt), lambda b, j: (b, j, 0))
    heads = lambda n, wdt: pl.BlockSpec((1, n, tm, wdt), lambda b, j: (b, 0, j, 0))
    const2 = lambda s: pl.BlockSpec(s, lambda b, j: (0, 0))
    in_specs = [
        tile(D), const2((1, D)), const2((D, IN_WIDTH_PAD)),
        pl.BlockSpec((3, tm, LANES), lambda b, j: (0, j, 0)),
        pl.BlockSpec((3, tm, LANES), lambda b, j: (0, j, 0)),
        pl.BlockSpec((1, HIST_ROWS, POOL_WIDTH), lambda b, j: (b, 0, 0)),
        pl.BlockSpec((len(POOL_WINDOWS), POOL_GROUP_DIM, POOL_GROUP_DIM), lambda b, j: (0, 0, 0)),
        const2((1, POOL_WIDTH)),
    ]
    out_specs = (
        heads(N_HEADS, HEAD_DIM), heads(N_IDX_HEADS, IDX_DIM), tile(KV_WIDTH), tile(KV_WIDTH), tile(LANES),
        heads(N_KV_HEADS, HEAD_DIM),
        pl.BlockSpec((1, tm // LANES, KV_WIDTH, LANES), lambda b, j: (b, j, 0, 0)) if v_t else tile(KV_WIDTH),
        tile(IDX_DIM), tile(POOL_WIDTH),
        pl.BlockSpec((1, HIST_ROWS, POOL_WIDTH), lambda b, j: (b, 0, 0)),
    )
    return pl.pallas_call(
        functools.partial(_inproj_kernel, tm=tm, pos0=pos0),
        out_shape=out_shape, grid=(B, nt), in_specs=in_specs, out_specs=out_specs,
        scratch_shapes=[pltpu.VMEM((HIST_ROWS + tm, POOL_WIDTH), F32)],
        compiler_params=pltpu.CompilerParams(dimension_semantics=("parallel", "arbitrary"),
                                             vmem_limit_bytes=VMEM_LIMIT),
        name="inproj",
    )(x, g, w_bf, rope, ropeki, hist, pw_bf, ps)


def _dsa_kernel(q_ref, qi_ref, kiwi_ref, k_ref, vt_ref, ki_ref, x_ref, yp_ref, wo_ref, *rest,
                tk, pos0, n_keys, n_sel):
    o_ref, key_buf, bias_buf, m_scr, l_scr, acc_scr, s_scr = rest[-7:]
    tq = LANES
    slab = COUNT_SLAB
    j = pl.program_id(1)
    base = pos0 + j * tq
    pos = base + lax.broadcasted_iota(I32, (1, tq), 1)
    limit = jnp.minimum((pos // CHUNK + 1) * CHUNK, n_keys)
    limit_max = jnp.minimum(((base + tq - 1) // CHUNK + 1) * CHUNK, n_keys)
    nkb = (limit_max + tk - 1) // tk
    nt = (((1,), (1,)), ((), ()))

    wi_t = jnp.transpose(kiwi_ref[0])[IDX_DIM:IDX_DIM + N_IDX_HEADS, :]

    def score_block(kb, carry):
        off = pl.multiple_of(kb * tk, tk)
        kiblk = ki_ref[0, pl.ds(off, tk), :]
        idx = jnp.zeros((tk, tq), F32)
        for h in range(N_IDX_HEADS):
            sc = lax.dot_general(kiblk, qi_ref[0, h], nt, preferred_element_type=F32)
            idx = idx + jnp.maximum(sc, 0.0) * wi_t[h:h + 1, :]
        idx = jnp.where(idx == 0.0, 0.0, idx)
        bits = lax.bitcast_convert_type(idx, I32)
        key = bits ^ ((bits >> 31) & 0x7FFFFFFF)
        key_buf[kb] = jnp.where(lax.broadcasted_iota(I32, (tk, tq), 0) < limit - kb * tk, key, INT_MIN)
        return carry

    lax.fori_loop(0, nkb, score_block, 0)

    def col_sum(a):
        return jnp.sum(a, axis=0, keepdims=True)

    def count_ge(cand):
        def body(kb, acc):
            kblk = key_buf[kb]
            for c in range(tk // slab):
                acc = acc + jnp.where(kblk[c * slab:(c + 1) * slab] >= cand, 1.0, 0.0)
            return acc
        return col_sum(lax.fori_loop(0, nkb, body, jnp.zeros((slab, tq), F32)))

    kf = float(n_sel)
    cnt0 = count_ge(jnp.zeros((1, tq), I32))
    t0 = jnp.where(cnt0 >= kf, 0, INT_MIN).astype(I32)

    def bit_body(i, carry):
        t, cnt = carry
        cand = t | lax.shift_left(jnp.int32(1), 30 - i)
        cnt_cand = count_ge(cand)
        keep = cnt_cand >= kf
        return jnp.where(keep, cand, t), jnp.where(keep, cnt_cand, cnt)

    t, cnt_ge = lax.fori_loop(0, 31, bit_body, (t0, cnt0))
    cnt_gt = count_ge(t + 1)
    need = kf - cnt_gt
    cnt_eq = cnt_ge - cnt_gt
    overfull = jnp.where(t != INT_MIN, cnt_eq - need, 0.0)
    slow = jnp.max(overfull) > 0.0

    @pl.when(jnp.logical_not(slow))
    def _():
        t_adm = jnp.maximum(t, INT_MIN + 1)

        def body(kb, carry):
            bias_buf[kb] = jnp.where(key_buf[kb] >= t_adm, 0.0, NEG_BIG)
            return carry
        lax.fori_loop(0, nkb, body, 0)

    @pl.when(slow)
    def _():
        tri = jnp.where(lax.broadcasted_iota(I32, (tk, tk), 1) <= lax.broadcasted_iota(I32, (tk, tk), 0),
                        1.0, 0.0).astype(BF16)

        def body(kb, seen):
            kblk = key_buf[kb]
            eq = jnp.where((kblk == t) & (kblk != INT_MIN), 1.0, 0.0)
            prefix = jnp.dot(tri, eq.astype(BF16), preferred_element_type=F32) + seen
            keep_tie = jnp.where(prefix <= need, eq, 0.0)
            sel = jnp.where(kblk > t, 1.0, keep_tie)
            bias_buf[kb] = jnp.where(sel > 0.0, 0.0, NEG_BIG)
            return seen + col_sum(eq)
        lax.fori_loop(0, nkb, body, jnp.zeros((1, tq), F32))

    m_scr[...] = jnp.full(m_scr.shape, NEG_BIG, F32)
    l_scr[...] = jnp.zeros(l_scr.shape, F32)
    acc_scr[...] = jnp.zeros(acc_scr.shape, F32)
    sub = LANES
    nsub = tk // sub

    def attn_block(kb, carry):
        for c in range(nsub):
            off = pl.multiple_of(kb * tk + c * sub, sub)
            for g in range(N_KV_HEADS):
                s_scr[c, g] = lax.dot_general(
                    k_ref[0, g, pl.ds(off, sub), :],
                    q_ref[0, Q_PER_KV * g:Q_PER_KV * (g + 1)].reshape(Q_PER_KV * tq, HEAD_DIM),
                    nt, preferred_element_type=F32)
        m = [m_scr[g] for g in range(N_KV_HEADS)]
        l = [l_scr[g] for g in range(N_KV_HEADS)]
        for c in range(nsub):
            bias = bias_buf[kb, c * sub:(c + 1) * sub, :]
            bias2 = jnp.concatenate([bias] * Q_PER_KV, axis=1)
            for g in range(N_KV_HEADS):
                s = s_scr[c, g] + bias2
                m_new = jnp.maximum(m[g], jnp.max(s, axis=0, keepdims=True))
                alpha = jnp.exp2(m[g] - m_new)
                p = jnp.exp2(s - m_new)
                l[g] = alpha * l[g] + col_sum(p)
                pv = jnp.dot(vt_ref[0, kb * nsub + c, g * HEAD_DIM:(g + 1) * HEAD_DIM, :], p.astype(BF16),
                             preferred_element_type=F32)
                acc_scr[g] = alpha * acc_scr[g] + pv
                m[g] = m_new
        for g in range(N_KV_HEADS):
            m_scr[g] = m[g]
            l_scr[g] = l[g]
        return carry

    lax.fori_loop(0, nkb, attn_block, 0)

    o_t = []
    for g in range(N_KV_HEADS):
        og = acc_scr[g] / l_scr[g]
        o_t.extend(og[:, hh * tq:(hh + 1) * tq] for hh in range(Q_PER_KV))
    y_att = jnp.transpose(jnp.concatenate(o_t, axis=0)).astype(BF16)
    y = jnp.dot(yp_ref[0], wo_ref[0:POOL_WIDTH, :], preferred_element_type=F32)
    y = y + jnp.dot(y_att, wo_ref[POOL_WIDTH:POOL_WIDTH + ATT_WIDTH, :], preferred_element_type=F32)
    o_ref[...] = (x_ref[0] + y).reshape(o_ref.shape)


def _dsa(q_hm, qi_hm, kiwi, k_hm, v_t, ki_all, x, yp, wo_bf, *, tk, pos0, n_keys, n_sel, out_base=None):
    B, T, D = x.shape
    L = k_hm.shape[2]
    tq = LANES
    assert L % tk == 0 and T % tq == 0
    nkb_max = L // tk
    tile = lambda wdt: pl.BlockSpec((1, tq, wdt), lambda b, j: (b, j, 0))
    in_specs = [pl.BlockSpec((1, N_HEADS, tq, HEAD_DIM), lambda b, j: (b, 0, j, 0)),
                pl.BlockSpec((1, N_IDX_HEADS, tq, IDX_DIM), lambda b, j: (b, 0, j, 0)),
                tile(LANES),
                pl.BlockSpec((1, N_KV_HEADS, L, HEAD_DIM), lambda b, j: (b, 0, 0, 0)),
                pl.BlockSpec((1, L // LANES, KV_WIDTH, LANES), lambda b, j: (b, 0, 0, 0)),
                pl.BlockSpec((1, L, IDX_DIM), lambda b, j: (b, 0, 0)),
                tile(D), tile(POOL_WIDTH), pl.BlockSpec((POOL_WIDTH + ATT_WIDTH, D), lambda b, j: (0, 0))]
    nq = T // tq
    args = (q_hm, qi_hm, kiwi, k_hm, v_t, ki_all, x, yp, wo_bf)
    if out_base is None:
        out_shape, out_spec, aliases = jax.ShapeDtypeStruct((B, T, D), F32), tile(D), {}
    else:
        out_shape = jax.ShapeDtypeStruct(out_base.shape, F32)
        out_spec = pl.BlockSpec((tq, D), lambda b, j: (b * nq + j, 0))
        in_specs, args, aliases = in_specs + [pl.BlockSpec(memory_space=pl.ANY)], args + (out_base,), {len(args): 0}
    return pl.pallas_call(
        functools.partial(_dsa_kernel, tk=tk, pos0=pos0, n_keys=n_keys, n_sel=n_sel),
        out_shape=out_shape, grid=(B, nq), in_specs=in_specs, out_specs=out_spec, input_output_aliases=aliases,
        scratch_shapes=[pltpu.VMEM((nkb_max, tk, tq), I32), pltpu.VMEM((nkb_max, tk, tq), F32),
                        pltpu.VMEM((N_KV_HEADS, 1, Q_PER_KV * tq), F32),
                        pltpu.VMEM((N_KV_HEADS, 1, Q_PER_KV * tq), F32),
                        pltpu.VMEM((N_KV_HEADS, HEAD_DIM, Q_PER_KV * tq), F32),
                        pltpu.VMEM((tk // LANES, N_KV_HEADS, LANES, Q_PER_KV * tq), F32)],
        compiler_params=pltpu.CompilerParams(dimension_semantics=("parallel", "arbitrary"),
                                             vmem_limit_bytes=VMEM_LIMIT),
        name="dsa",
    )(*args)


def _gmlp_kernel(x_ref, g_ref, win_ref, lng_ref, lnb_ref, ws_ref, bias_ref, wout_ref, o_ref, v_ref, *, tm):
    x = x_ref[...]
    h = _rms(x, g_ref[...])
    z = jax.nn.gelu(jnp.dot(h.astype(BF16), win_ref[...], preferred_element_type=F32))
    half = z.shape[1] // 2
    u, v = z[:, :half], z[:, half:]
    mu = jnp.mean(v, axis=-1, keepdims=True)
    var = jnp.mean(jnp.square(v - mu), axis=-1, keepdims=True)
    vn = (v - mu) * lax.rsqrt(var + LN_EPS) * lng_ref[...] + lnb_ref[...]
    v_ref[...] = vn
    gd = half // GM_GROUPS
    gated = []
    for c in range(tm // GM_CHUNK):
        rows = slice(c * GM_CHUNK, (c + 1) * GM_CHUNK)
        vc = vn[rows].astype(BF16)
        mixed = jnp.concatenate(
            [jnp.dot(ws_ref[0, g], vc[:, g * gd:(g + 1) * gd], preferred_element_type=F32)
             for g in range(GM_GROUPS)], axis=1) + bias_ref[0]
        gated.append((u[rows] * mixed).astype(BF16))
    gated = jnp.concatenate(gated, axis=0)
    o_ref[...] = x + jnp.dot(gated, wout_ref[...], preferred_element_type=F32)


def _gmlp(x, g, win_bf, lng, lnb, ws2, bias2, wout_bf, *, tm, n_first, n_v_rows):
    N, D = x.shape
    half = win_bf.shape[1] // 2
    nt = N // tm
    t_first = n_first // tm
    variant = lambda i: jnp.where(i >= t_first, 1, 0)
    row = pl.BlockSpec((tm, D), lambda i: (i, 0))
    const = lambda s: pl.BlockSpec(s, lambda i: (0, 0))
    in_specs = [row, const((1, D)), const((D, 2 * half)), const((1, half)), const((1, half)),
                pl.BlockSpec((1, GM_GROUPS, GM_CHUNK, GM_CHUNK), lambda i: (variant(i), 0, 0, 0)),
                pl.BlockSpec((1, GM_CHUNK, half), lambda i: (variant(i), 0, 0)),
                const((half, D))]
    out_specs = (row, pl.BlockSpec((tm, half), lambda i: (jnp.maximum(i - t_first, 0), 0)))
    return pl.pallas_call(
        functools.partial(_gmlp_kernel, tm=tm),
        out_shape=(jax.ShapeDtypeStruct((N, D), F32), jax.ShapeDtypeStruct((n_v_rows, half), F32)),
        grid=(nt,), in_specs=in_specs, out_specs=out_specs,
        compiler_params=pltpu.CompilerParams(dimension_semantics=("arbitrary",), vmem_limit_bytes=VMEM_LIMIT),
        name="gmlp",
    )(x, g, win_bf, lng, lnb, ws2, bias2, wout_bf)


ROUTE_E0, ROUTE_E1, ROUTE_R0, ROUTE_R1, ROUTE_G0, ROUTE_G1 = range(6)
ROUTE_ROWS = 8
N_LOGITS = N_EXPERT_GROUPS + N_EXPERTS
LOGIT_ROWS = -(-N_LOGITS // SUBLANES) * SUBLANES


def _router_kernel(x_ref, g_ref, wr_ref, route_ref, cnt_ref, carry_ref, *, tm):
    i = pl.program_id(0)

    @pl.when(i == 0)
    def _():
        carry_ref[...] = jnp.zeros(carry_ref.shape, F32)

    h = _rms(x_ref[...], g_ref[...])
    nt = (((1,), (1,)), ((), ()))
    logits = lax.dot_general(wr_ref[...], h.astype(BF16), nt, preferred_element_type=F32)[:LOGIT_ROWS]
    row = lax.broadcasted_iota(I32, (LOGIT_ROWS, tm), 0).astype(F32)
    ninf = -jnp.inf
    big = float(LANES)
    cmax = lambda a: jnp.max(a, axis=0, keepdims=True)
    cmin = lambda a: jnp.min(a, axis=0, keepdims=True)
    csum = lambda a: jnp.sum(a, axis=0, keepdims=True)

    is_grp = row < N_EXPERT_GROUPS
    lg = jnp.where(is_grp, logits, ninf)
    mg = cmax(lg)
    g_sel = cmin(jnp.where(lg == mg, row, big))
    p_grp = 1.0 / csum(jnp.where(is_grp, jnp.exp(lg - mg), 0.0))
    lo = N_EXPERT_GROUPS + g_sel * EXPERTS_PER_GROUP
    le = jnp.where((row >= lo) & (row < lo + EXPERTS_PER_GROUP), logits, ninf)
    v1 = cmax(le)
    j1 = cmin(jnp.where(le == v1, row, big))
    le2 = jnp.where(row == j1, ninf, le)
    v2 = cmax(le2)
    j2 = cmin(jnp.where(le2 == v2, row, big))
    e0 = j1 - N_EXPERT_GROUPS
    e1 = j2 - N_EXPERT_GROUPS
    r = jnp.exp(v2 - v1)
    g0 = p_grp / (1.0 + r)
    g1 = p_grp * r / (1.0 + r)

    erow = lax.broadcasted_iota(I32, (N_EXPERTS, tm), 0).astype(F32)
    oh0 = jnp.where(erow == e0, 1.0, 0.0)
    oh1 = jnp.where(erow == e1, 1.0, 0.0)
    oh = oh0 + oh1
    earlier = jnp.where(lax.broadcasted_iota(I32, (tm, tm), 0) < lax.broadcasted_iota(I32, (tm, tm), 1),
                        1.0, 0.0).astype(BF16)
    before = jnp.dot(oh.astype(BF16), earlier, preferred_element_type=F32) + carry_ref[...]
    r0 = csum(oh0 * before)
    r1 = csum(oh1 * before)
    carry_ref[...] = carry_ref[...] + jnp.sum(oh, axis=1, keepdims=True)
    cnt_ref[...] = jnp.broadcast_to(carry_ref[...], cnt_ref.shape)

    rows = {ROUTE_E0: e0, ROUTE_E1: e1, ROUTE_R0: r0, ROUTE_R1: r1, ROUTE_G0: g0, ROUTE_G1: g1}
    zero = jnp.zeros((1, tm), F32)
    route_ref[...] = jnp.concatenate([rows.get(k, zero) for k in range(ROUTE_ROWS)], axis=0)


def _router(x, g, wr_t, *, tm):
    N, D = x.shape
    const = lambda s: pl.BlockSpec(s, lambda i: (0, 0))
    return pl.pallas_call(
        functools.partial(_router_kernel, tm=tm),
        out_shape=(jax.ShapeDtypeStruct((ROUTE_ROWS, N), F32), jax.ShapeDtypeStruct((N_EXPERTS, LANES), F32)),
        grid=(N // tm,), in_specs=[pl.BlockSpec((tm, D), lambda i: (i, 0)), const((1, D)), const((LANES, D))],
        out_specs=(pl.BlockSpec((ROUTE_ROWS, tm), lambda i: (0, i)), const((N_EXPERTS, LANES))),
        scratch_shapes=[pltpu.VMEM((N_EXPERTS, 1), F32)],
        compiler_params=pltpu.CompilerParams(dimension_semantics=("arbitrary",), vmem_limit_bytes=VMEM_LIMIT),
        name="router",
    )(x, g, wr_t)


def _pack_bf16_pairs(h):
    half = h.shape[1] // 2
    lo = lax.bitcast_convert_type(h[:, :half].astype(BF16).astype(F32), U32)
    hi = lax.bitcast_convert_type(h[:, half:].astype(BF16).astype(F32), U32)
    return (lo >> 16) | (hi & jnp.uint32(0xFFFF0000))


def _unpack_bf16_pairs(w):
    lo = lax.bitcast_convert_type(w << 16, F32).astype(BF16)
    hi = lax.bitcast_convert_type(w & jnp.uint32(0xFFFF0000), F32).astype(BF16)
    return jnp.concatenate([lo, hi], axis=1)


def _dispatch_kernel(d0_ref, d1_ref, pad_ref, x_ref, g_ref, rows_ref, h_scr, zblk, sems, *, tm):
    i = pl.program_id(0)
    n = pl.num_programs(0)
    slot = i % 2
    groups = tm // SUBLANES

    def wait_slot(s):
        for _ in range(2 * groups):
            pltpu.make_async_copy(h_scr.at[s, 0], rows_ref.at[pl.ds(0, SUBLANES)], sems.at[s]).wait()

    @pl.when(i >= 2)
    def _():
        wait_slot(slot)

    h_scr[slot] = _pack_bf16_pairs(_rms(x_ref[...], g_ref[...])).reshape(groups, SUBLANES, h_scr.shape[3])

    def body(k, carry):
        for u in range(SUBLANES):
            r = k * SUBLANES + u
            src = h_scr.at[slot, k, pl.ds(u, 1)]
            pltpu.make_async_copy(src, rows_ref.at[pl.ds(d0_ref[r], 1)], sems.at[slot]).start()
            pltpu.make_async_copy(src, rows_ref.at[pl.ds(d1_ref[r], 1)], sems.at[slot]).start()
        return carry

    lax.fori_loop(0, groups, body, 0)

    @pl.when(i == n - 1)
    def _():
        wait_slot(slot)

    @pl.when(jnp.logical_and(i == n - 1, n >= 2))
    def _():
        wait_slot(1 - slot)

    @pl.when(i == n - 1)
    def _():
        zblk[...] = jnp.zeros(zblk.shape, U32)
        zrow = zblk.at[pl.ds(0, 1)]
        for e in range(N_EXPERTS):
            first, count = pad_ref[0, e], pad_ref[1, e]

            def start(k, carry):
                pltpu.make_async_copy(zrow, rows_ref.at[pl.ds(first + k, 1)], sems.at[2]).start()
                return carry

            def wait(k, carry):
                pltpu.make_async_copy(zrow, rows_ref.at[pl.ds(first, 1)], sems.at[2]).wait()
                return carry

            lax.fori_loop(0, count, start, 0)
            lax.fori_loop(0, count, wait, 0)

        def blk_copy(b):
            return pltpu.make_async_copy(zblk, rows_ref.at[pl.ds(b * MOE_BLOCK, MOE_BLOCK)], sems.at[2])

        n_blocks = rows_ref.shape[0] // MOE_BLOCK
        n_used = pad_ref[2, 0]
        lax.fori_loop(n_used, n_blocks, lambda b, c: (blk_copy(b).start(), c)[1], 0)
        lax.fori_loop(n_used, n_blocks, lambda b, c: (blk_copy(n_used).wait(), c)[1], 0)


def _dispatch(dest_flat, pad_info, x, g, *, tm, n_rows):
    N, D = x.shape
    nt = N // tm
    return pl.pallas_call(
        functools.partial(_dispatch_kernel, tm=tm),
        out_shape=jax.ShapeDtypeStruct((n_rows, D // 2), U32), grid=(nt,),
        in_specs=[pl.BlockSpec((tm,), lambda i: (i,), memory_space=pltpu.SMEM),
                  pl.BlockSpec((tm,), lambda i: (i + nt,), memory_space=pltpu.SMEM),
                  pl.BlockSpec(memory_space=pltpu.SMEM),
                  pl.BlockSpec((tm, D), lambda i: (i, 0)),
                  pl.BlockSpec((1, D), lambda i: (0, 0))],
        out_specs=pl.BlockSpec(memory_space=pl.ANY),
        scratch_shapes=[pltpu.VMEM((2, tm // SUBLANES, SUBLANES, D // 2), U32), pltpu.VMEM((MOE_BLOCK, D // 2), U32),
                        pltpu.SemaphoreType.DMA((3,))],
        compiler_params=pltpu.CompilerParams(dimension_semantics=("arbitrary",), vmem_limit_bytes=VMEM_LIMIT),
        name="dispatch",
    )(dest_flat, dest_flat, pad_info, x, g)


def _expert_kernel(be_ref, nu_ref, x_ref, wg_ref, wu_ref, wd_ref, y_ref, wg_bf, wu_bf, wd_bf):
    i = pl.program_id(0)

    @pl.when(jnp.logical_or(i == 0, be_ref[i] != be_ref[jnp.maximum(i - 1, 0)]))
    def _():
        wg_bf[...] = wg_ref[0, 0].astype(BF16)
        wu_bf[...] = wu_ref[0, 0].astype(BF16)
        wd_bf[...] = wd_ref[0, 0].astype(BF16)

    @pl.when(i < nu_ref[0])
    def _():
        x = _unpack_bf16_pairs(x_ref[...])
        a = jnp.dot(x, wg_bf[...], preferred_element_type=F32)
        u = jnp.dot(x, wu_bf[...], preferred_element_type=F32)
        act = (a * jax.nn.sigmoid(a)) * u
        y_ref[...] = jnp.dot(act.astype(BF16), wd_bf[...], preferred_element_type=F32)

    @pl.when(i >= nu_ref[0])
    def _():
        y_ref[...] = jnp.zeros(y_ref.shape, F32)


def _experts(block_e, n_used, x_rows, w_gate, w_up, w_down, layer):
    n_rows = x_rows.shape[0]
    D, de = w_gate.shape[2:]
    n_blocks = n_rows // MOE_BLOCK
    grid_spec = pltpu.PrefetchScalarGridSpec(
        num_scalar_prefetch=2, grid=(n_blocks,),
        in_specs=[pl.BlockSpec((MOE_BLOCK, D // 2), lambda i, be, nu: (jnp.minimum(i, nu[0] - 1), 0)),
                  pl.BlockSpec((1, 1, D, de), lambda i, be, nu: (layer, be[i], 0, 0)),
                  pl.BlockSpec((1, 1, D, de), lambda i, be, nu: (layer, be[i], 0, 0)),
                  pl.BlockSpec((1, 1, de, D), lambda i, be, nu: (layer, be[i], 0, 0))],
        out_specs=pl.BlockSpec((MOE_BLOCK, D), lambda i, be, nu: (i, 0)),
        scratch_shapes=[pltpu.VMEM((D, de), BF16), pltpu.VMEM((D, de), BF16), pltpu.VMEM((de, D), BF16)])
    return pl.pallas_call(
        _expert_kernel, out_shape=jax.ShapeDtypeStruct((n_rows, D), F32), grid_spec=grid_spec,
        compiler_params=pltpu.CompilerParams(dimension_semantics=("arbitrary",), vmem_limit_bytes=VMEM_LIMIT),
        name="experts",
    )(block_e, n_used, x_rows, w_gate, w_up, w_down)


def _combine_kernel(dc0_ref, dc1_ref, dn0_ref, dn1_ref, x_ref, route_ref, g_ref, y_ref, *rest,
                    tm, final_norm, t_first):
    *o_refs, ybuf, sems = rest
    i = pl.program_id(0)
    n = pl.num_programs(0)
    slot = i % 2
    groups = tm // SUBLANES
    D = x_ref.shape[1]

    def gather(d_refs, s):
        def body(k, carry):
            for u in range(SUBLANES):
                r = k * SUBLANES + u
                for a in range(2):
                    pltpu.make_async_copy(y_ref.at[pl.ds(d_refs[a][r], 1)], ybuf.at[s, a, k, pl.ds(u, 1)],
                                          sems.at[s]).start()
            return carry
        lax.fori_loop(0, groups, body, 0)

    @pl.when(i == 0)
    def _():
        gather((dc0_ref, dc1_ref), 0)

    @pl.when(i + 1 < n)
    def _():
        gather((dn0_ref, dn1_ref), 1 - slot)

    for _ in range(2 * groups):
        pltpu.make_async_copy(y_ref.at[pl.ds(0, SUBLANES)], ybuf.at[slot, 0, 0], sems.at[slot]).wait()
    route = jnp.transpose(route_ref[...])
    g0 = route[:, ROUTE_G0:ROUTE_G0 + 1]
    g1 = route[:, ROUTE_G1:ROUTE_G1 + 1]
    out = x_ref[...] + (ybuf[slot, 0].reshape(tm, D) * g0 + ybuf[slot, 1].reshape(tm, D) * g1)
    if final_norm:
        out = _rms(out, g_ref[...])
    if t_first is None:
        o_refs[0][...] = out
    else:
        @pl.when(i < t_first)
        def _():
            o_refs[0][...] = out

        @pl.when(i >= t_first)
        def _():
            o_refs[1][...] = out


def _combine(dest_flat, x, route, g, y_rows, *, tm, final_norm, n_first=None):
    N, D = x.shape
    nt = N // tm
    row = lambda wdt: pl.BlockSpec((tm, wdt), lambda i: (i, 0))
    nxt = lambda i: jnp.minimum(i + 1, nt - 1)
    smem = lambda f: pl.BlockSpec((tm,), f, memory_space=pltpu.SMEM)
    if n_first is None:
        t_first, out_shape, out_specs = None, jax.ShapeDtypeStruct((N, D), F32), row(D)
    else:
        t_first = n_first // tm
        out_shape = (jax.ShapeDtypeStruct((n_first, D), F32), jax.ShapeDtypeStruct((N - n_first, D), F32))
        out_specs = (pl.BlockSpec((tm, D), lambda i: (jnp.minimum(i, t_first - 1), 0)),
                     pl.BlockSpec((tm, D), lambda i: (jnp.maximum(i - t_first, 0), 0)))
    return pl.pallas_call(
        functools.partial(_combine_kernel, tm=tm, final_norm=final_norm, t_first=t_first),
        out_shape=out_shape, grid=(nt,),
        in_specs=[smem(lambda i: (i,)), smem(lambda i: (i + nt,)),
                  smem(lambda i: (nxt(i),)), smem(lambda i: (nxt(i) + nt,)),
                  row(D), pl.BlockSpec((ROUTE_ROWS, tm), lambda i: (0, i)), pl.BlockSpec((1, D), lambda i: (0, 0)),
                  pl.BlockSpec(memory_space=pl.ANY)],
        out_specs=out_specs,
        scratch_shapes=[pltpu.VMEM((2, 2, tm // SUBLANES, SUBLANES, D), F32), pltpu.SemaphoreType.DMA((2,))],
        compiler_params=pltpu.CompilerParams(dimension_semantics=("arbitrary",), vmem_limit_bytes=VMEM_LIMIT),
        name="combine",
    )(dest_flat, dest_flat, dest_flat, dest_flat, x, route, g, y_rows)


def _hier_moe(x, g_ffn, w_rg, w_re, w_gate, w_up, w_down, layer, g_final, *, final_norm, n_first=None):
    N, D = x.shape
    tm = TOKEN_TILE
    wr_t = jnp.concatenate([w_rg.T, w_re.T, jnp.zeros((LANES - N_LOGITS, D), F32)], axis=0).astype(BF16)
    g_ffn = g_ffn.reshape(1, D)
    route, counts = _router(x, g_ffn, wr_t, tm=tm)
    counts = counts[:, 0].astype(I32)
    eid_t = route[ROUTE_E0:ROUTE_E1 + 1].astype(I32)
    rank_t = route[ROUTE_R0:ROUTE_R1 + 1].astype(I32)
    padded = (counts + MOE_BLOCK - 1) // MOE_BLOCK * MOE_BLOCK
    pad_end = jnp.cumsum(padded)
    pad_start = pad_end - padded
    expert_ids = jnp.arange(N_EXPERTS, dtype=I32)
    start_of = jnp.sum(jnp.where(eid_t[:, :, None] == expert_ids, pad_start, 0), axis=-1)
    dest_t = start_of + rank_t
    n_blocks = -(-(2 * N) // MOE_BLOCK) + N_EXPERTS
    n_rows = n_blocks * MOE_BLOCK
    n_used = (pad_end[-1] // MOE_BLOCK).astype(I32)
    block_start = jnp.minimum(jnp.arange(n_blocks, dtype=I32), n_used - 1) * MOE_BLOCK
    block_e = jnp.minimum(jnp.sum(block_start[:, None] >= pad_end[None, :], axis=1), N_EXPERTS - 1).astype(I32)
    pad_info = jnp.stack([pad_start + counts, padded - counts, jnp.broadcast_to(n_used, (N_EXPERTS,))])
    dest_flat = dest_t.reshape(-1)
    x_rows = _dispatch(dest_flat, pad_info, x, g_ffn, tm=tm, n_rows=n_rows)
    y_rows = _experts(block_e, n_used.reshape(1), x_rows, w_gate, w_up, w_down, layer)
    return _combine(dest_flat, x, route, g_final.reshape(1, D), y_rows, tm=tm, final_norm=final_norm,
                    n_first=n_first)


def _rope_tables(pos, rot_lanes):
    inv = 1.0 / (ROPE_THETA ** (jnp.arange(ROT_HALF, dtype=F32) / ROT_HALF))
    ang = pos.astype(F32)[:, None] * inv[None, :]
    cos, sin = jnp.cos(ang), jnp.sin(ang)
    lane = jnp.arange(LANES)
    r = lane % HEAD_DIM
    active = (lane < rot_lanes)
    first = active & (r < ROT_HALF)
    second = active & (r >= ROT_HALF) & (r < 2 * ROT_HALF)
    cos_l = cos[:, r % ROT_HALF]
    sin_l = sin[:, r % ROT_HALF]
    c = jnp.where((first | second)[None, :], cos_l, 1.0)
    sa = jnp.where(first[None, :], -sin_l, 0.0)
    sb = jnp.where(second[None, :], sin_l, 0.0)
    return jnp.stack([c, sa, sb]).astype(F32)


def _prep_w_in(w_in):
    D = w_in.shape[0]
    assert w_in.shape[1] == sum(IN_SIZES)
    pad = jnp.zeros((D, IN_WIDTH_PAD - w_in.shape[1]), w_in.dtype)
    return jnp.concatenate([w_in, pad], axis=1).astype(BF16)


def kernel(x_prompt, x_sample, cache_k, cache_v, cache_idx_k, state_pool, norm_mix, norm_ffn, norm_final,
           par_w_in, par_pool_w, par_pool_scale, par_w_out, gm_w_in, gm_ln_g, gm_ln_b, gm_ws, gm_bs, gm_w_out,
           moe_router_group, moe_router_expert, moe_w_gate, moe_w_up, moe_w_down):
    Bp, Tp, D = x_prompt.shape
    Bs, Ts, _ = x_sample.shape
    past = cache_k.shape[2]
    Np, Ns = Bp * Tp, Bs * Ts
    depth = norm_mix.shape[0]
    assert depth == 2 and Ts == CHUNK and Tp % PROJ_TILE == 0 and Tp % TOKEN_TILE == 0 and Ns % TOKEN_TILE == 0

    w_in_bf = _prep_w_in(par_w_in[0])
    pw_bf = par_pool_w[0].astype(BF16)
    ps = par_pool_scale[0].reshape(1, POOL_WIDTH)
    wo_bf = par_w_out[0].astype(BF16)
    g_mix0 = norm_mix[0].reshape(1, D)
    pos_p = jnp.arange(Tp, dtype=I32)
    pos_s = past + jnp.arange(Ts, dtype=I32)

    hist_p = jnp.zeros((Bp, HIST_ROWS, POOL_WIDTH), F32)
    hist_s = jnp.pad(state_pool[0], ((0, 0), (1, 0), (0, 0)))
    tk = KEY_BLOCK
    (q_p, qi_p, k_p, v_p, kiwi_p, kbf_p, vbf_p, kibf_p, yp_p, st_p) = _inproj(
        x_prompt, g_mix0, w_in_bf, _rope_tables(pos_p, LANES), _rope_tables(pos_p, IDX_DIM), hist_p, pw_bf, ps,
        tm=PROJ_TILE, pos0=0)
    (q_s, qi_s, k_s, v_s, kiwi_s, kbf_s, vbf_s, kibf_s, yp_s, st_s) = _inproj(
        x_sample, g_mix0, w_in_bf, _rope_tables(pos_s, LANES), _rope_tables(pos_s, IDX_DIM), hist_s, pw_bf, ps,
        tm=Ts, pos0=past)

    ls = past + Ts
    lsp = -(-ls // tk) * tk
    kpad = lambda a, ax: jnp.pad(a, [(0, lsp - ls) if d == ax else (0, 0) for d in range(a.ndim)])
    kall = kpad(jnp.concatenate([jnp.transpose(cache_k[0], (0, 2, 1, 3)).astype(BF16), kbf_s], axis=2), 2)
    vall = kpad(jnp.concatenate([cache_v[0].reshape(Bs, past, KV_WIDTH).astype(BF16), vbf_s], axis=1), 1)
    vall_t = jnp.transpose(vall.reshape(Bs, lsp // LANES, LANES, KV_WIDTH), (0, 1, 3, 2))
    kiall = kpad(jnp.concatenate([cache_idx_k[0].astype(BF16), kibf_s], axis=1), 1)
    qpad = lambda a, ax: jnp.pad(a, [(0, LANES - Ts) if d == ax else (0, 0) for d in range(a.ndim)])
    x1_s = _dsa(qpad(q_s, 2), qpad(qi_s, 2), qpad(kiwi_s, 1), kall, vall_t, kiall, qpad(x_sample, 1), qpad(yp_s, 1),
                wo_bf, tk=tk, pos0=past, n_keys=ls, n_sel=min(TOPK_MAX, ls // 4))[:, :Ts]
    assert Tp % tk == 0
    x = jnp.pad(x1_s.reshape(Ns, D), ((Np, 0), (0, 0)))
    x = _dsa(q_p, qi_p, kiwi_p, kbf_p, vbf_p, kibf_p, x_prompt, yp_p, wo_bf,
             tk=tk, pos0=0, n_keys=Tp, n_sel=min(TOPK_MAX, Tp // 4), out_base=x)
    x = _hier_moe(x, norm_ffn[0], moe_router_group[0], moe_router_expert[0],
                  moe_w_gate, moe_w_up, moe_w_down, 0, norm_final, final_norm=False)

    cs = Ts
    tril = lambda n: jnp.tril(jnp.ones((n, n), bool))
    ws_p = jnp.where(tril(GM_CHUNK)[None], gm_ws[0], 0.0)
    ws_small = jnp.where(tril(cs)[None], gm_ws[0][:, :cs, :cs], 0.0)
    rep = GM_CHUNK // cs
    ws_s = jnp.einsum('ab,gts->gatbs', jnp.eye(rep, dtype=F32), ws_small).reshape(GM_GROUPS, GM_CHUNK, GM_CHUNK)
    ws2 = jnp.stack([ws_p, ws_s]).astype(BF16)
    gd = D // GM_GROUPS
    bias_p = jnp.repeat(jnp.transpose(gm_bs[0]), gd, axis=1)
    bias_s = jnp.tile(jnp.repeat(jnp.transpose(gm_bs[0][:, :cs]), gd, axis=1), (rep, 1))
    bias2 = jnp.stack([bias_p, bias_s])
    x, gm_v = _gmlp(x, norm_mix[1].reshape(1, D), gm_w_in[0].astype(BF16), gm_ln_g[0].reshape(1, D),
                    gm_ln_b[0].reshape(1, D), ws2, bias2, gm_w_out[0].astype(BF16),
                    tm=TOKEN_TILE, n_first=Np, n_v_rows=Ns)
    y_p, y_s = _hier_moe(x, norm_ffn[1], moe_router_group[1], moe_router_expert[1],
                         moe_w_gate, moe_w_up, moe_w_down, 1, norm_final, final_norm=True, n_first=Np)
    y_p = y_p.reshape(Bp, Tp, D)
    y_s = y_s.reshape(Bs, Ts, D)
    r4 = lambda a, b, t: a.reshape(1, b, t, N_KV_HEADS, HEAD_DIM)
    return (y_p, y_s,
            r4(k_p, Bp, Tp), r4(v_p, Bp, Tp), kiwi_p[:, :, :IDX_DIM][None], st_p[:, 1:][None],
            r4(k_s, Bs, Ts), r4(v_s, Bs, Ts), kiwi_s[:, :, :IDX_DIM][None], st_s[:, 1:][None],
            gm_v.reshape(1, Bs, Ts, D))
```

```python
import functools

import jax
import jax.numpy as jnp
from jax import lax
from jax.experimental import pallas as pl
from jax.experimental.pallas import tpu as pltpu

F32 = jnp.float32
BF16 = jnp.bfloat16
I32 = jnp.int32
U32 = jnp.uint32

LANES = 128
SUBLANES = 8
CHUNK = 64
POOL_WINDOWS = (2, 4, 8, 16)
POOL_GROUP_DIM = 128
POOL_WIDTH = 512
HIST_ROWS = 16
N_HEADS = 8
HEAD_DIM = 64
N_KV_HEADS = 4
Q_PER_KV = N_HEADS // N_KV_HEADS
ATT_WIDTH = N_HEADS * HEAD_DIM
KV_WIDTH = N_KV_HEADS * HEAD_DIM
N_IDX_HEADS = 8
IDX_DIM = 64
TOPK_MAX = 256
ROPE_THETA = 500000.0
ROT_HALF = HEAD_DIM // 8
GM_CHUNK = 128
GM_GROUPS = 8
N_EXPERT_GROUPS = 4
EXPERTS_PER_GROUP = 8
N_EXPERTS = 32
MOE_BLOCK = 512
RMS_EPS = 1e-6
LN_EPS = 1e-5

INT_MIN = -2147483648
LOG2_E = 1.4426950408889634
NEG_BIG = -1e30
VMEM_LIMIT = 48 * 1024 * 1024

TOKEN_TILE = 1024
PROJ_TILE = 512
KEY_BLOCK = 512
COUNT_SLAB = 64

IN_SIZES = (POOL_WIDTH, ATT_WIDTH, KV_WIDTH, KV_WIDTH, N_IDX_HEADS * IDX_DIM, IDX_DIM, N_IDX_HEADS)
COL_XP, COL_Q, COL_K, COL_V, COL_QI, COL_KIWI = (sum(IN_SIZES[:i]) for i in range(6))
IN_WIDTH_PAD = COL_KIWI + LANES


def _rms(x, g):
    return x * lax.rsqrt(jnp.mean(x * x, axis=-1, keepdims=True) + RMS_EPS) * g


def _rope128(x, c, sa, sb):
    return x * c + pltpu.roll(x, LANES - ROT_HALF, 1) * sa + pltpu.roll(x, ROT_HALF, 1) * sb


def _inproj_kernel(x_ref, g_ref, w_ref, rope_ref, ropeki_ref, hist_ref, pw_ref, ps_ref,
                   q_ref, qi_ref, k_ref, v_ref, kiwi_ref, kbf_ref, vbf_ref, kibf_ref, yp_ref, state_ref,
                   buf_ref, *, tm, pos0):
    j = pl.program_id(1)
    h = _rms(x_ref[0], g_ref[...])
    proj = jnp.dot(h.astype(BF16), w_ref[...], preferred_element_type=F32)
    c, sa, sb = rope_ref[0], rope_ref[1], rope_ref[2]

    def put_heads(ref, i, chunk):
        ref[0, 2 * i] = chunk[:, :HEAD_DIM].astype(BF16)
        ref[0, 2 * i + 1] = chunk[:, HEAD_DIM:].astype(BF16)

    for i in range(ATT_WIDTH // LANES):
        put_heads(q_ref, i, _rope128(proj[:, COL_Q + i * LANES:COL_Q + (i + 1) * LANES], c, sa, sb)
                  * (HEAD_DIM ** -0.5 * LOG2_E))
        put_heads(qi_ref, i, _rope128(proj[:, COL_QI + i * LANES:COL_QI + (i + 1) * LANES], c, sa, sb))
    for i in range(KV_WIDTH // LANES):
        sl = slice(i * LANES, (i + 1) * LANES)
        kr = _rope128(proj[:, COL_K + i * LANES:COL_K + (i + 1) * LANES], c, sa, sb)
        k_ref[0, :, sl] = kr
        put_heads(kbf_ref, i, kr)
    vv = proj[:, COL_V:COL_V + KV_WIDTH]
    v_ref[0] = vv
    if tm % LANES == 0:
        for cc in range(tm // LANES):
            vbf_ref[0, cc] = jnp.transpose(vv[cc * LANES:(cc + 1) * LANES, :]).astype(BF16)
    else:
        vbf_ref[0] = vv.astype(BF16)
    kiwi = _rope128(proj[:, COL_KIWI:COL_KIWI + LANES], ropeki_ref[0], ropeki_ref[1], ropeki_ref[2])
    kiwi_ref[0] = kiwi
    kibf_ref[0] = kiwi[:, :IDX_DIM].astype(BF16)

    @pl.when(j == 0)
    def _():
        buf_ref[0:HIST_ROWS, :] = hist_ref[0]

    xp = proj[:, COL_XP:COL_XP + POOL_WIDTH]
    buf_ref[HIST_ROWS:HIST_ROWS + tm, :] = xp
    pos = pos0 + j * tm + lax.broadcasted_iota(I32, (tm, 1), 0)
    for gi, w in enumerate(POOL_WINDOWS):
        c0 = gi * POOL_GROUP_DIM
        s = xp[:, c0:c0 + POOL_GROUP_DIM]
        for i in range(1, w):
            s = s + buf_ref[HIST_ROWS - i:HIST_ROWS - i + tm, c0:c0 + POOL_GROUP_DIM]
        cnt = jnp.minimum(pos + 1, w).astype(F32)
        d = s / cnt - xp[:, c0:c0 + POOL_GROUP_DIM]
        y = jnp.dot(d.astype(BF16), pw_ref[gi], preferred_element_type=F32)
        yp_ref[0, :, c0:c0 + POOL_GROUP_DIM] = (y * ps_ref[:, c0:c0 + POOL_GROUP_DIM]).astype(BF16)
    tail = buf_ref[tm:tm + HIST_ROWS, :]
    state_ref[0] = tail
    buf_ref[0:HIST_ROWS, :] = tail


def _inproj(x, g, w_bf, rope, ropeki, hist, pw_bf, ps, *, tm, pos0):
    B, T, D = x.shape
    nt = T // tm
    f = lambda shape, dt: jax.ShapeDtypeStruct(shape, dt)
    v_t = tm % LANES == 0
    out_shape = (
        f((B, N_HEADS, T, HEAD_DIM), BF16), f((B, N_IDX_HEADS, T, IDX_DIM), BF16),
        f((B, T, KV_WIDTH), F32), f((B, T, KV_WIDTH), F32), f((B, T, LANES), F32),
        f((B, N_KV_HEADS, T, HEAD_DIM), BF16),
        f((B, T // LANES, KV_WIDTH, LANES) if v_t else (B, T, KV_WIDTH), BF16),
        f((B, T, IDX_DIM), BF16),
        f((B, T, POOL_WIDTH), BF16), f((B, HIST_ROWS, POOL_WIDTH), F32),
    )
    tile = lambda wdt: pl.BlockSpec((1, tm, wdt), lambda b, j: (b, j, 0))
    heads = lambda n, wdt: pl.BlockSpec((1, n, tm, wdt), lambda b, j: (b, 0, j, 0))
    const2 = lambda s: pl.BlockSpec(s, lambda b, j: (0, 0))
    in_specs = [
        tile(D), const2((1, D)), const2((D, IN_WIDTH_PAD)),
        pl.BlockSpec((3, tm, LANES), lambda b, j: (0, j, 0)),
        pl.BlockSpec((3, tm, LANES), lambda b, j: (0, j, 0)),
        pl.BlockSpec((1, HIST_ROWS, POOL_WIDTH), lambda b, j: (b, 0, 0)),
        pl.BlockSpec((len(POOL_WINDOWS), POOL_GROUP_DIM, POOL_GROUP_DIM), lambda b, j: (0, 0, 0)),
        const2((1, POOL_WIDTH)),
    ]
    out_specs = (
        heads(N_HEADS, HEAD_DIM), heads(N_IDX_HEADS, IDX_DIM), tile(KV_WIDTH), tile(KV_WIDTH), tile(LANES),
        heads(N_KV_HEADS, HEAD_DIM),
        pl.BlockSpec((1, tm // LANES, KV_WIDTH, LANES), lambda b, j: (b, j, 0, 0)) if v_t else tile(KV_WIDTH),
        tile(IDX_DIM), tile(POOL_WIDTH),
        pl.BlockSpec((1, HIST_ROWS, POOL_WIDTH), lambda b, j: (b, 0, 0)),
    )
    return pl.pallas_call(
        functools.partial(_inproj_kernel, tm=tm, pos0=pos0),
        out_shape=out_shape, grid=(B, nt), in_specs=in_specs, out_specs=out_specs,
        scratch_shapes=[pltpu.VMEM((HIST_ROWS + tm, POOL_WIDTH), F32)],
        compiler_params=pltpu.CompilerParams(dimension_semantics=("parallel", "arbitrary"),
                                             vmem_limit_bytes=VMEM_LIMIT),
        name="inproj",
    )(x, g, w_bf, rope, ropeki, hist, pw_bf, ps)


def _dsa_kernel(q_ref, qi_ref, kiwi_ref, k_ref, vt_ref, ki_ref, x_ref, yp_ref, wo_ref, *rest,
                tk, pos0, n_keys, n_sel):
    o_ref, key_buf, bias_buf, m_scr, l_scr, acc_scr, s_scr = rest[-7:]
    tq = LANES
    slab = COUNT_SLAB
    j = pl.program_id(1)
    base = pos0 + j * tq
    pos = base + lax.broadcasted_iota(I32, (1, tq), 1)
    limit = jnp.minimum((pos // CHUNK + 1) * CHUNK, n_keys)
    limit_max = jnp.minimum(((base + tq - 1) // CHUNK + 1) * CHUNK, n_keys)
    nkb = (limit_max + tk - 1) // tk
    nt = (((1,), (1,)), ((), ()))

    wi_t = jnp.transpose(kiwi_ref[0])[IDX_DIM:IDX_DIM + N_IDX_HEADS, :]

    def score_block(kb, carry):
        off = pl.multiple_of(kb * tk, tk)
        kiblk = ki_ref[0, pl.ds(off, tk), :]
        idx = jnp.zeros((tk, tq), F32)
        for h in range(N_IDX_HEADS):
            sc = lax.dot_general(kiblk, qi_ref[0, h], nt, preferred_element_type=F32)
            idx = idx + jnp.maximum(sc, 0.0) * wi_t[h:h + 1, :]
        idx = jnp.where(idx == 0.0, 0.0, idx)
        bits = lax.bitcast_convert_type(idx, I32)
        key = bits ^ ((bits >> 31) & 0x7FFFFFFF)
        key_buf[kb] = jnp.where(lax.broadcasted_iota(I32, (tk, tq), 0) < limit - kb * tk, key, INT_MIN)
        return carry

    lax.fori_loop(0, nkb, score_block, 0)

    def col_sum(a):
        return jnp.sum(a, axis=0, keepdims=True)

    def count_ge(cand):
        def body(kb, acc):
            kblk = key_buf[kb]
            for c in range(tk // slab):
                acc = acc + jnp.where(kblk[c * slab:(c + 1) * slab] >= cand, 1.0, 0.0)
            return acc
        return col_sum(lax.fori_loop(0, nkb, body, jnp.zeros((slab, tq), F32)))

    kf = float(n_sel)
    cnt0 = count_ge(jnp.zeros((1, tq), I32))
    t0 = jnp.where(cnt0 >= kf, 0, INT_MIN).astype(I32)

    def bit_body(i, carry):
        t, cnt = carry
        cand = t | lax.shift_left(jnp.int32(1), 30 - i)
        cnt_cand = count_ge(cand)
        keep = cnt_cand >= kf
        return jnp.where(keep, cand, t), jnp.where(keep, cnt_cand, cnt)

    t, cnt_ge = lax.fori_loop(0, 31, bit_body, (t0, cnt0))
    cnt_gt = count_ge(t + 1)
    need = kf - cnt_gt
    cnt_eq = cnt_ge - cnt_gt
    overfull = jnp.where(t != INT_MIN, cnt_eq - need, 0.0)
    slow = jnp.max(overfull) > 0.0

    @pl.when(jnp.logical_not(slow))
    def _():
        t_adm = jnp.maximum(t, INT_MIN + 1)

        def body(kb, carry):
            bias_buf[kb] = jnp.where(key_buf[kb] >= t_adm, 0.0, NEG_BIG)
            return carry
        lax.fori_loop(0, nkb, body, 0)

    @pl.when(slow)
    def _():
        tri = jnp.where(lax.broadcasted_iota(I32, (tk, tk), 1) <= lax.broadcasted_iota(I32, (tk, tk), 0),
                        1.0, 0.0).astype(BF16)

        def body(kb, seen):
            kblk = key_buf[kb]
            eq = jnp.where((kblk == t) & (kblk != INT_MIN), 1.0, 0.0)
            prefix = jnp.dot(tri, eq.astype(BF16), preferred_element_type=F32) + seen
            keep_tie = jnp.where(prefix <= need, eq, 0.0)
            sel = jnp.where(kblk > t, 1.0, keep_tie)
            bias_buf[kb] = jnp.where(sel > 0.0, 0.0, NEG_BIG)
            return seen + col_sum(eq)
        lax.fori_loop(0, nkb, body, jnp.zeros((1, tq), F32))

    m_scr[...] = jnp.full(m_scr.shape, NEG_BIG, F32)
    l_scr[...] = jnp.zeros(l_scr.shape, F32)
    acc_scr[...] = jnp.zeros(acc_scr.shape, F32)
    sub = LANES
    nsub = tk // sub

    def attn_block(kb, carry):
        for c in range(nsub):
            off = pl.multiple_of(kb * tk + c * sub, sub)
            for g in range(N_KV_HEADS):
                s_scr[c, g] = lax.dot_general(
                    k_ref[0, g, pl.ds(off, sub), :],
                    q_ref[0, Q_PER_KV * g:Q_PER_KV * (g + 1)].reshape(Q_PER_KV * tq, HEAD_DIM),
                    nt, preferred_element_type=F32)
        m = [m_scr[g] for g in range(N_KV_HEADS)]
        l = [l_scr[g] for g in range(N_KV_HEADS)]
        for c in range(nsub):
            bias = bias_buf[kb, c * sub:(c + 1) * sub, :]
            bias2 = jnp.concatenate([bias] * Q_PER_KV, axis=1)
            for g in range(N_KV_HEADS):
                s = s_scr[c, g] + bias2
                m_new = jnp.maximum(m[g], jnp.max(s, axis=0, keepdims=True))
                alpha = jnp.exp2(m[g] - m_new)
                p = jnp.exp2(s - m_new)
                l[g] = alpha * l[g] + col_sum(p)
                pv = jnp.dot(vt_ref[0, kb * nsub + c, g * HEAD_DIM:(g + 1) * HEAD_DIM, :], p.astype(BF16),
                             preferred_element_type=F32)
                acc_scr[g] = alpha * acc_scr[g] + pv
                m[g] = m_new
        for g in range(N_KV_HEADS):
            m_scr[g] = m[g]
            l_scr[g] = l[g]
        return carry

    lax.fori_loop(0, nkb, attn_block, 0)

    o_t = []
    for g in range(N_KV_HEADS):
        og = acc_scr[g] / l_scr[g]
        o_t.extend(og[:, hh * tq:(hh + 1) * tq] for hh in range(Q_PER_KV))
    y_att = jnp.transpose(jnp.concatenate(o_t, axis=0)).astype(BF16)
    y = jnp.dot(yp_ref[0], wo_ref[0:POOL_WIDTH, :], preferred_element_type=F32)
    y = y + jnp.dot(y_att, wo_ref[POOL_WIDTH:POOL_WIDTH + ATT_WIDTH, :], preferred_element_type=F32)
    o_ref[...] = (x_ref[0] + y).reshape(o_ref.shape)


def _dsa(q_hm, qi_hm, kiwi, k_hm, v_t, ki_all, x, yp, wo_bf, *, tk, pos0, n_keys, n_sel, out_base=None):
    B, T, D = x.shape
    L = k_hm.shape[2]
    tq = LANES
    assert L % tk == 0 and T % tq == 0
    nkb_max = L // tk
    tile = lambda wdt: pl.BlockSpec((1, tq, wdt), lambda b, j: (b, j, 0))
    in_specs = [pl.BlockSpec((1, N_HEADS, tq, HEAD_DIM), lambda b, j: (b, 0, j, 0)),
                pl.BlockSpec((1, N_IDX_HEADS, tq, IDX_DIM), lambda b, j: (b, 0, j, 0)),
                tile(LANES),
                pl.BlockSpec((1, N_KV_HEADS, L, HEAD_DIM), lambda b, j: (b, 0, 0, 0)),
                pl.BlockSpec((1, L // LANES, KV_WIDTH, LANES), lambda b, j: (b, 0, 0, 0)),
                pl.BlockSpec((1, L, IDX_DIM), lambda b, j: (b, 0, 0)),
                tile(D), tile(POOL_WIDTH), pl.BlockSpec((POOL_WIDTH + ATT_WIDTH, D), lambda b, j: (0, 0))]
    nq = T // tq
    args = (q_hm, qi_hm, kiwi, k_hm, v_t, ki_all, x, yp, wo_bf)
    if out_base is None:
        out_shape, out_spec, aliases = jax.ShapeDtypeStruct((B, T, D), F32), tile(D), {}
    else:
        out_shape = jax.ShapeDtypeStruct(out_base.shape, F32)
        out_spec = pl.BlockSpec((tq, D), lambda b, j: (b * nq + j, 0))
        in_specs, args, aliases = in_specs + [pl.BlockSpec(memory_space=pl.ANY)], args + (out_base,), {len(args): 0}
    return pl.pallas_call(
        functools.partial(_dsa_kernel, tk=tk, pos0=pos0, n_keys=n_keys, n_sel=n_sel),
        out_shape=out_shape, grid=(B, nq), in_specs=in_specs, out_specs=out_spec, input_output_aliases=aliases,
        scratch_shapes=[pltpu.VMEM((nkb_max, tk, tq), I32), pltpu.VMEM((nkb_max, tk, tq), F32),
                        pltpu.VMEM((N_KV_HEADS, 1, Q_PER_KV * tq), F32),
                        pltpu.VMEM((N_KV_HEADS, 1, Q_PER_KV * tq), F32),
                        pltpu.VMEM((N_KV_HEADS, HEAD_DIM, Q_PER_KV * tq), F32),
                        pltpu.VMEM((tk // LANES, N_KV_HEADS, LANES, Q_PER_KV * tq), F32)],
        compiler_params=pltpu.CompilerParams(dimension_semantics=("parallel", "arbitrary"),
                                             vmem_limit_bytes=VMEM_LIMIT),
        name="dsa",
    )(*args)


def _gmlp_kernel(x_ref, g_ref, win_ref, lng_ref, lnb_ref, ws_ref, bias_ref, wout_ref, o_ref, v_ref, *, tm):
    x = x_ref[...]
    h = _rms(x, g_ref[...])
    z = jax.nn.gelu(jnp.dot(h.astype(BF16), win_ref[...], preferred_element_type=F32))
    half = z.shape[1] // 2
    u, v = z[:, :half], z[:, half:]
    mu = jnp.mean(v, axis=-1, keepdims=True)
    var = jnp.mean(jnp.square(v - mu), axis=-1, keepdims=True)
    vn = (v - mu) * lax.rsqrt(var + LN_EPS) * lng_ref[...] + lnb_ref[...]
    v_ref[...] = vn
    gd = half // GM_GROUPS
    gated = []
    for c in range(tm // GM_CHUNK):
        rows = slice(c * GM_CHUNK, (c + 1) * GM_CHUNK)
        vc = vn[rows].astype(BF16)
        mixed = jnp.concatenate(
            [jnp.dot(ws_ref[0, g], vc[:, g * gd:(g + 1) * gd], preferred_element_type=F32)
             for g in range(GM_GROUPS)], axis=1) + bias_ref[0]
        gated.append((u[rows] * mixed).astype(BF16))
    gated = jnp.concatenate(gated, axis=0)
    o_ref[...] = x + jnp.dot(gated, wout_ref[...], preferred_element_type=F32)


def _gmlp(x, g, win_bf, lng, lnb, ws2, bias2, wout_bf, *, tm, n_first, n_v_rows):
    N, D = x.shape
    half = win_bf.shape[1] // 2
    nt = N // tm
    t_first = n_first // tm
    variant = lambda i: jnp.where(i >= t_first, 1, 0)
    row = pl.BlockSpec((tm, D), lambda i: (i, 0))
    const = lambda s: pl.BlockSpec(s, lambda i: (0, 0))
    in_specs = [row, const((1, D)), const((D, 2 * half)), const((1, half)), const((1, half)),
                pl.BlockSpec((1, GM_GROUPS, GM_CHUNK, GM_CHUNK), lambda i: (variant(i), 0, 0, 0)),
                pl.BlockSpec((1, GM_CHUNK, half), lambda i: (variant(i), 0, 0)),
                const((half, D))]
    out_specs = (row, pl.BlockSpec((tm, half), lambda i: (jnp.maximum(i - t_first, 0), 0)))
    return pl.pallas_call(
        functools.partial(_gmlp_kernel, tm=tm),
        out_shape=(jax.ShapeDtypeStruct((N, D), F32), jax.ShapeDtypeStruct((n_v_rows, half), F32)),
        grid=(nt,), in_specs=in_specs, out_specs=out_specs,
        compiler_params=pltpu.CompilerParams(dimension_semantics=("arbitrary",), vmem_limit_bytes=VMEM_LIMIT),
        name="gmlp",
    )(x, g, win_bf, lng, lnb, ws2, bias2, wout_bf)


ROUTE_E0, ROUTE_E1, ROUTE_R0, ROUTE_R1, ROUTE_G0, ROUTE_G1 = range(6)
ROUTE_ROWS = 8
N_LOGITS = N_EXPERT_GROUPS + N_EXPERTS
LOGIT_ROWS = -(-N_LOGITS // SUBLANES) * SUBLANES


def _router_kernel(x_ref, g_ref, wr_ref, route_ref, cnt_ref, carry_ref, *, tm):
    i = pl.program_id(0)

    @pl.when(i == 0)
    def _():
        carry_ref[...] = jnp.zeros(carry_ref.shape, F32)

    h = _rms(x_ref[...], g_ref[...])
    nt = (((1,), (1,)), ((), ()))
    logits = lax.dot_general(wr_ref[...], h.astype(BF16), nt, preferred_element_type=F32)[:LOGIT_ROWS]
    row = lax.broadcasted_iota(I32, (LOGIT_ROWS, tm), 0).astype(F32)
    ninf = -jnp.inf
    big = float(LANES)
    cmax = lambda a: jnp.max(a, axis=0, keepdims=True)
    cmin = lambda a: jnp.min(a, axis=0, keepdims=True)
    csum = lambda a: jnp.sum(a, axis=0, keepdims=True)

    is_grp = row < N_EXPERT_GROUPS
    lg = jnp.where(is_grp, logits, ninf)
    mg = cmax(lg)
    g_sel = cmin(jnp.where(lg == mg, row, big))
    p_grp = 1.0 / csum(jnp.where(is_grp, jnp.exp(lg - mg), 0.0))
    lo = N_EXPERT_GROUPS + g_sel * EXPERTS_PER_GROUP
    le = jnp.where((row >= lo) & (row < lo + EXPERTS_PER_GROUP), logits, ninf)
    v1 = cmax(le)
    j1 = cmin(jnp.where(le == v1, row, big))
    le2 = jnp.where(row == j1, ninf, le)
    v2 = cmax(le2)
    j2 = cmin(jnp.where(le2 == v2, row, big))
    e0 = j1 - N_EXPERT_GROUPS
    e1 = j2 - N_EXPERT_GROUPS
    r = jnp.exp(v2 - v1)
    g0 = p_grp / (1.0 + r)
    g1 = p_grp * r / (1.0 + r)

    erow = lax.broadcasted_iota(I32, (N_EXPERTS, tm), 0).astype(F32)
    oh0 = jnp.where(erow == e0, 1.0, 0.0)
    oh1 = jnp.where(erow == e1, 1.0, 0.0)
    oh = oh0 + oh1
    earlier = jnp.where(lax.broadcasted_iota(I32, (tm, tm), 0) < lax.broadcasted_iota(I32, (tm, tm), 1),
                        1.0, 0.0).astype(BF16)
    before = jnp.dot(oh.astype(BF16), earlier, preferred_element_type=F32) + carry_ref[...]
    r0 = csum(oh0 * before)
    r1 = csum(oh1 * before)
    carry_ref[...] = carry_ref[...] + jnp.sum(oh, axis=1, keepdims=True)
    cnt_ref[...] = jnp.broadcast_to(carry_ref[...], cnt_ref.shape)

    rows = {ROUTE_E0: e0, ROUTE_E1: e1, ROUTE_R0: r0, ROUTE_R1: r1, ROUTE_G0: g0, ROUTE_G1: g1}
    zero = jnp.zeros((1, tm), F32)
    route_ref[...] = jnp.concatenate([rows.get(k, zero) for k in range(ROUTE_ROWS)], axis=0)


def _router(x, g, wr_t, *, tm):
    N, D = x.shape
    const = lambda s: pl.BlockSpec(s, lambda i: (0, 0))
    return pl.pallas_call(
        functools.partial(_router_kernel, tm=tm),
        out_shape=(jax.ShapeDtypeStruct((ROUTE_ROWS, N), F32), jax.ShapeDtypeStruct((N_EXPERTS, LANES), F32)),
        grid=(N // tm,), in_specs=[pl.BlockSpec((tm, D), lambda i: (i, 0)), const((1, D)), const((LANES, D))],
        out_specs=(pl.BlockSpec((ROUTE_ROWS, tm), lambda i: (0, i)), const((N_EXPERTS, LANES))),
        scratch_shapes=[pltpu.VMEM((N_EXPERTS, 1), F32)],
        compiler_params=pltpu.CompilerParams(dimension_semantics=("arbitrary",), vmem_limit_bytes=VMEM_LIMIT),
        name="router",
    )(x, g, wr_t)


def _pack_bf16_pairs(h):
    half = h.shape[1] // 2
    lo = lax.bitcast_convert_type(h[:, :half].astype(BF16).astype(F32), U32)
    hi = lax.bitcast_convert_type(h[:, half:].astype(BF16).astype(F32), U32)
    return (lo >> 16) | (hi & jnp.uint32(0xFFFF0000))


def _unpack_bf16_pairs(w):
    lo = lax.bitcast_convert_type(w << 16, F32).astype(BF16)
    hi = lax.bitcast_convert_type(w & jnp.uint32(0xFFFF0000), F32).astype(BF16)
    return jnp.concatenate([lo, hi], axis=1)


def _dispatch_kernel(d0_ref, d1_ref, pad_ref, x_ref, g_ref, rows_ref, h_scr, zblk, sems, *, tm):
    i = pl.program_id(0)
    n = pl.num_programs(0)
    slot = i % 2
    groups = tm // SUBLANES

    def wait_slot(s):
        for _ in range(2 * groups):
            pltpu.make_async_copy(h_scr.at[s, 0], rows_ref.at[pl.ds(0, SUBLANES)], sems.at[s]).wait()

    @pl.when(i == 0)
    def _():
        zblk[...] = jnp.zeros(zblk.shape, U32)
        n_blocks = rows_ref.shape[0] // MOE_BLOCK
        n_used = pad_ref[2, 0]

        def blk_copy(row0):
            return pltpu.make_async_copy(zblk, rows_ref.at[pl.ds(row0, MOE_BLOCK)], sems.at[2])

        for wait in (False, True):
            for e in range(N_EXPERTS):
                @pl.when(pad_ref[1, e] > 0)
                def _():
                    cp = blk_copy(pl.multiple_of(pad_ref[0, e], MOE_BLOCK))
                    cp.wait() if wait else cp.start()

            def unused(b, carry):
                cp = blk_copy(pl.multiple_of(b * MOE_BLOCK, MOE_BLOCK))
                cp.wait() if wait else cp.start()
                return carry
            lax.fori_loop(n_used, n_blocks, unused, 0)

    @pl.when(i >= 2)
    def _():
        wait_slot(slot)

    h_scr[slot] = _pack_bf16_pairs(_rms(x_ref[...], g_ref[...])).reshape(groups, SUBLANES, h_scr.shape[3])

    def body(k, carry):
        for u in range(SUBLANES):
            r = k * SUBLANES + u
            src = h_scr.at[slot, k, pl.ds(u, 1)]
            pltpu.make_async_copy(src, rows_ref.at[pl.ds(d0_ref[r], 1)], sems.at[slot]).start()
            pltpu.make_async_copy(src, rows_ref.at[pl.ds(d1_ref[r], 1)], sems.at[slot]).start()
        return carry

    lax.fori_loop(0, groups, body, 0)

    @pl.when(i == n - 1)
    def _():
        wait_slot(slot)

    @pl.when(jnp.logical_and(i == n - 1, n >= 2))
    def _():
        wait_slot(1 - slot)


def _dispatch(dest_flat, pad_info, x, g, *, tm, n_rows):
    N, D = x.shape
    nt = N // tm
    return pl.pallas_call(
        functools.partial(_dispatch_kernel, tm=tm),
        out_shape=jax.ShapeDtypeStruct((n_rows, D // 2), U32), grid=(nt,),
        in_specs=[pl.BlockSpec((tm,), lambda i: (i,), memory_space=pltpu.SMEM),
                  pl.BlockSpec((tm,), lambda i: (i + nt,), memory_space=pltpu.SMEM),
                  pl.BlockSpec(memory_space=pltpu.SMEM),
                  pl.BlockSpec((tm, D), lambda i: (i, 0)),
                  pl.BlockSpec((1, D), lambda i: (0, 0))],
        out_specs=pl.BlockSpec(memory_space=pl.ANY),
        scratch_shapes=[pltpu.VMEM((2, tm // SUBLANES, SUBLANES, D // 2), U32), pltpu.VMEM((MOE_BLOCK, D // 2), U32),
                        pltpu.SemaphoreType.DMA((3,))],
        compiler_params=pltpu.CompilerParams(dimension_semantics=("arbitrary",), vmem_limit_bytes=VMEM_LIMIT),
        name="dispatch",
    )(dest_flat, dest_flat, pad_info, x, g)


def _expert_kernel(be_ref, nu_ref, x_ref, wg_ref, wu_ref, wd_ref, y_ref, wg_bf, wu_bf, wd_bf):
    i = pl.program_id(0)

    @pl.when(jnp.logical_or(i == 0, be_ref[i] != be_ref[jnp.maximum(i - 1, 0)]))
    def _():
        wg_bf[...] = wg_ref[0, 0].astype(BF16)
        wu_bf[...] = wu_ref[0, 0].astype(BF16)
        wd_bf[...] = wd_ref[0, 0].astype(BF16)

    @pl.when(i < nu_ref[0])
    def _():
        x = _unpack_bf16_pairs(x_ref[...])
        a = jnp.dot(x, wg_bf[...], preferred_element_type=F32)
        u = jnp.dot(x, wu_bf[...], preferred_element_type=F32)
        act = (a * jax.nn.sigmoid(a)) * u
        y_ref[...] = jnp.dot(act.astype(BF16), wd_bf[...], preferred_element_type=F32)

    @pl.when(i >= nu_ref[0])
    def _():
        y_ref[...] = jnp.zeros(y_ref.shape, F32)


def _experts(block_e, n_used, x_rows, w_gate, w_up, w_down, layer):
    n_rows = x_rows.shape[0]
    D, de = w_gate.shape[2:]
    n_blocks = n_rows // MOE_BLOCK
    grid_spec = pltpu.PrefetchScalarGridSpec(
        num_scalar_prefetch=2, grid=(n_blocks,),
        in_specs=[pl.BlockSpec((MOE_BLOCK, D // 2), lambda i, be, nu: (jnp.minimum(i, nu[0] - 1), 0)),
                  pl.BlockSpec((1, 1, D, de), lambda i, be, nu: (layer, be[i], 0, 0)),
                  pl.BlockSpec((1, 1, D, de), lambda i, be, nu: (layer, be[i], 0, 0)),
                  pl.BlockSpec((1, 1, de, D), lambda i, be, nu: (layer, be[i], 0, 0))],
        out_specs=pl.BlockSpec((MOE_BLOCK, D), lambda i, be, nu: (i, 0)),
        scratch_shapes=[pltpu.VMEM((D, de), BF16), pltpu.VMEM((D, de), BF16), pltpu.VMEM((de, D), BF16)])
    return pl.pallas_call(
        _expert_kernel, out_shape=jax.ShapeDtypeStruct((n_rows, D), F32), grid_spec=grid_spec,
        compiler_params=pltpu.CompilerParams(dimension_semantics=("arbitrary",), vmem_limit_bytes=VMEM_LIMIT),
        name="experts",
    )(block_e, n_used, x_rows, w_gate, w_up, w_down)


def _combine_kernel(dc0_ref, dc1_ref, dn0_ref, dn1_ref, x_ref, route_ref, g_ref, y_ref, *rest,
                    tm, final_norm, t_first):
    *o_refs, ybuf, sems = rest
    i = pl.program_id(0)
    n = pl.num_programs(0)
    slot = i % 2
    groups = tm // SUBLANES
    D = x_ref.shape[1]

    def gather(d_refs, s):
        def body(k, carry):
            for u in range(SUBLANES):
                r = k * SUBLANES + u
                for a in range(2):
                    pltpu.make_async_copy(y_ref.at[pl.ds(d_refs[a][r], 1)], ybuf.at[s, a, k, pl.ds(u, 1)],
                                          sems.at[s]).start()
            return carry
        lax.fori_loop(0, groups, body, 0)

    @pl.when(i == 0)
    def _():
        gather((dc0_ref, dc1_ref), 0)

    @pl.when(i + 1 < n)
    def _():
        gather((dn0_ref, dn1_ref), 1 - slot)

    for _ in range(2 * groups):
        pltpu.make_async_copy(y_ref.at[pl.ds(0, SUBLANES)], ybuf.at[slot, 0, 0], sems.at[slot]).wait()
    route = jnp.transpose(route_ref[...])
    g0 = route[:, ROUTE_G0:ROUTE_G0 + 1]
    g1 = route[:, ROUTE_G1:ROUTE_G1 + 1]
    out = x_ref[...] + (ybuf[slot, 0].reshape(tm, D) * g0 + ybuf[slot, 1].reshape(tm, D) * g1)
    if final_norm:
        out = _rms(out, g_ref[...])
    if t_first is None:
        o_refs[0][...] = out
    else:
        @pl.when(i < t_first)
        def _():
            o_refs[0][...] = out

        @pl.when(i >= t_first)
        def _():
            o_refs[1][...] = out


def _combine(dest_flat, x, route, g, y_rows, *, tm, final_norm, n_first=None):
    N, D = x.shape
    nt = N // tm
    row = lambda wdt: pl.BlockSpec((tm, wdt), lambda i: (i, 0))
    nxt = lambda i: jnp.minimum(i + 1, nt - 1)
    smem = lambda f: pl.BlockSpec((tm,), f, memory_space=pltpu.SMEM)
    if n_first is None:
        t_first, out_shape, out_specs = None, jax.ShapeDtypeStruct((N, D), F32), row(D)
    else:
        t_first = n_first // tm
        out_shape = (jax.ShapeDtypeStruct((n_first, D), F32), jax.ShapeDtypeStruct((N - n_first, D), F32))
        out_specs = (pl.BlockSpec((tm, D), lambda i: (jnp.minimum(i, t_first - 1), 0)),
                     pl.BlockSpec((tm, D), lambda i: (jnp.maximum(i - t_first, 0), 0)))
    return pl.pallas_call(
        functools.partial(_combine_kernel, tm=tm, final_norm=final_norm, t_first=t_first),
        out_shape=out_shape, grid=(nt,),
        in_specs=[smem(lambda i: (i,)), smem(lambda i: (i + nt,)),
                  smem(lambda i: (nxt(i),)), smem(lambda i: (nxt(i) + nt,)),
                  row(D), pl.BlockSpec((ROUTE_ROWS, tm), lambda i: (0, i)), pl.BlockSpec((1, D), lambda i: (0, 0)),
                  pl.BlockSpec(memory_space=pl.ANY)],
        out_specs=out_specs,
        scratch_shapes=[pltpu.VMEM((2, 2, tm // SUBLANES, SUBLANES, D), F32), pltpu.SemaphoreType.DMA((2,))],
        compiler_params=pltpu.CompilerParams(dimension_semantics=("arbitrary",), vmem_limit_bytes=VMEM_LIMIT),
        name="combine",
    )(dest_flat, dest_flat, dest_flat, dest_flat, x, route, g, y_rows)


def _hier_moe(x, g_ffn, w_rg, w_re, w_gate, w_up, w_down, layer, g_final, *, final_norm, n_first=None):
    N, D = x.shape
    tm = TOKEN_TILE
    wr_t = jnp.concatenate([w_rg.T, w_re.T, jnp.zeros((LANES - N_LOGITS, D), F32)], axis=0).astype(BF16)
    g_ffn = g_ffn.reshape(1, D)
    route, counts = _router(x, g_ffn, wr_t, tm=tm)
    counts = counts[:, 0].astype(I32)
    eid_t = route[ROUTE_E0:ROUTE_E1 + 1].astype(I32)
    rank_t = route[ROUTE_R0:ROUTE_R1 + 1].astype(I32)
    padded = (counts + MOE_BLOCK - 1) // MOE_BLOCK * MOE_BLOCK
    pad_end = jnp.cumsum(padded)
    pad_start = pad_end - padded
    expert_ids = jnp.arange(N_EXPERTS, dtype=I32)
    start_of = jnp.sum(jnp.where(eid_t[:, :, None] == expert_ids, pad_start, 0), axis=-1)
    dest_t = start_of + rank_t
    n_blocks = -(-(2 * N) // MOE_BLOCK) + N_EXPERTS
    n_rows = n_blocks * MOE_BLOCK
    n_used = (pad_end[-1] // MOE_BLOCK).astype(I32)
    block_start = jnp.minimum(jnp.arange(n_blocks, dtype=I32), n_used - 1) * MOE_BLOCK
    block_e = jnp.minimum(jnp.sum(block_start[:, None] >= pad_end[None, :], axis=1), N_EXPERTS - 1).astype(I32)
    pad_info = jnp.stack([pad_end - MOE_BLOCK, padded - counts, jnp.broadcast_to(n_used, (N_EXPERTS,))])
    dest_flat = dest_t.reshape(-1)
    x_rows = _dispatch(dest_flat, pad_info, x, g_ffn, tm=tm, n_rows=n_rows)
    y_rows = _experts(block_e, n_used.reshape(1), x_rows, w_gate, w_up, w_down, layer)
    return _combine(dest_flat, x, route, g_final.reshape(1, D), y_rows, tm=tm, final_norm=final_norm,
                    n_first=n_first)


def _rope_tables(pos, rot_lanes):
    inv = 1.0 / (ROPE_THETA ** (jnp.arange(ROT_HALF, dtype=F32) / ROT_HALF))
    ang = pos.astype(F32)[:, None] * inv[None, :]
    cos, sin = jnp.cos(ang), jnp.sin(ang)
    lane = jnp.arange(LANES)
    r = lane % HEAD_DIM
    active = (lane < rot_lanes)
    first = active & (r < ROT_HALF)
    second = active & (r >= ROT_HALF) & (r < 2 * ROT_HALF)
    cos_l = cos[:, r % ROT_HALF]
    sin_l = sin[:, r % ROT_HALF]
    c = jnp.where((first | second)[None, :], cos_l, 1.0)
    sa = jnp.where(first[None, :], -sin_l, 0.0)
    sb = jnp.where(second[None, :], sin_l, 0.0)
    return jnp.stack([c, sa, sb]).astype(F32)


def _prep_w_in(w_in):
    D = w_in.shape[0]
    assert w_in.shape[1] == sum(IN_SIZES)
    pad = jnp.zeros((D, IN_WIDTH_PAD - w_in.shape[1]), w_in.dtype)
    return jnp.concatenate([w_in, pad], axis=1).astype(BF16)


def kernel(x_prompt, x_sample, cache_k, cache_v, cache_idx_k, state_pool, norm_mix, norm_ffn, norm_final,
           par_w_in, par_pool_w, par_pool_scale, par_w_out, gm_w_in, gm_ln_g, gm_ln_b, gm_ws, gm_bs, gm_w_out,
           moe_router_group, moe_router_expert, moe_w_gate, moe_w_up, moe_w_down):
    Bp, Tp, D = x_prompt.shape
    Bs, Ts, _ = x_sample.shape
    past = cache_k.shape[2]
    Np, Ns = Bp * Tp, Bs * Ts
    depth = norm_mix.shape[0]
    assert depth == 2 and Ts == CHUNK and Tp % PROJ_TILE == 0 and Tp % TOKEN_TILE == 0 and Ns % TOKEN_TILE == 0

    w_in_bf = _prep_w_in(par_w_in[0])
    pw_bf = par_pool_w[0].astype(BF16)
    ps = par_pool_scale[0].reshape(1, POOL_WIDTH)
    wo_bf = par_w_out[0].astype(BF16)
    g_mix0 = norm_mix[0].reshape(1, D)
    pos_p = jnp.arange(Tp, dtype=I32)
    pos_s = past + jnp.arange(Ts, dtype=I32)

    hist_p = jnp.zeros((Bp, HIST_ROWS, POOL_WIDTH), F32)
    hist_s = jnp.pad(state_pool[0], ((0, 0), (1, 0), (0, 0)))
    tk = KEY_BLOCK
    (q_p, qi_p, k_p, v_p, kiwi_p, kbf_p, vbf_p, kibf_p, yp_p, st_p) = _inproj(
        x_prompt, g_mix0, w_in_bf, _rope_tables(pos_p, LANES), _rope_tables(pos_p, IDX_DIM), hist_p, pw_bf, ps,
        tm=PROJ_TILE, pos0=0)
    (q_s, qi_s, k_s, v_s, kiwi_s, kbf_s, vbf_s, kibf_s, yp_s, st_s) = _inproj(
        x_sample, g_mix0, w_in_bf, _rope_tables(pos_s, LANES), _rope_tables(pos_s, IDX_DIM), hist_s, pw_bf, ps,
        tm=Ts, pos0=past)

    ls = past + Ts
    lsp = -(-ls // tk) * tk
    kpad = lambda a, ax: jnp.pad(a, [(0, lsp - ls) if d == ax else (0, 0) for d in range(a.ndim)])
    kall = kpad(jnp.concatenate([jnp.transpose(cache_k[0], (0, 2, 1, 3)).astype(BF16), kbf_s], axis=2), 2)
    vall = kpad(jnp.concatenate([cache_v[0].reshape(Bs, past, KV_WIDTH).astype(BF16), vbf_s], axis=1), 1)
    vall_t = jnp.transpose(vall.reshape(Bs, lsp // LANES, LANES, KV_WIDTH), (0, 1, 3, 2))
    kiall = kpad(jnp.concatenate([cache_idx_k[0].astype(BF16), kibf_s], axis=1), 1)
    qpad = lambda a, ax: jnp.pad(a, [(0, LANES - Ts) if d == ax else (0, 0) for d in range(a.ndim)])
    x1_s = _dsa(qpad(q_s, 2), qpad(qi_s, 2), qpad(kiwi_s, 1), kall, vall_t, kiall, qpad(x_sample, 1), qpad(yp_s, 1),
                wo_bf, tk=tk, pos0=past, n_keys=ls, n_sel=min(TOPK_MAX, ls // 4))[:, :Ts]
    assert Tp % tk == 0
    x = jnp.pad(x1_s.reshape(Ns, D), ((Np, 0), (0, 0)))
    x = _dsa(q_p, qi_p, kiwi_p, kbf_p, vbf_p, kibf_p, x_prompt, yp_p, wo_bf,
             tk=tk, pos0=0, n_keys=Tp, n_sel=min(TOPK_MAX, Tp // 4), out_base=x)
    x = _hier_moe(x, norm_ffn[0], moe_router_group[0], moe_router_expert[0],
                  moe_w_gate, moe_w_up, moe_w_down, 0, norm_final, final_norm=False)

    cs = Ts
    tril = lambda n: jnp.tril(jnp.ones((n, n), bool))
    ws_p = jnp.where(tril(GM_CHUNK)[None], gm_ws[0], 0.0)
    ws_small = jnp.where(tril(cs)[None], gm_ws[0][:, :cs, :cs], 0.0)
    rep = GM_CHUNK // cs
    ws_s = jnp.einsum('ab,gts->gatbs', jnp.eye(rep, dtype=F32), ws_small).reshape(GM_GROUPS, GM_CHUNK, GM_CHUNK)
    ws2 = jnp.stack([ws_p, ws_s]).astype(BF16)
    gd = D // GM_GROUPS
    bias_p = jnp.repeat(jnp.transpose(gm_bs[0]), gd, axis=1)
    bias_s = jnp.tile(jnp.repeat(jnp.transpose(gm_bs[0][:, :cs]), gd, axis=1), (rep, 1))
    bias2 = jnp.stack([bias_p, bias_s])
    x, gm_v = _gmlp(x, norm_mix[1].reshape(1, D), gm_w_in[0].astype(BF16), gm_ln_g[0].reshape(1, D),
                    gm_ln_b[0].reshape(1, D), ws2, bias2, gm_w_out[0].astype(BF16),
                    tm=TOKEN_TILE, n_first=Np, n_v_rows=Ns)
    y_p, y_s = _hier_moe(x, norm_ffn[1], moe_router_group[1], moe_router_expert[1],
                         moe_w_gate, moe_w_up, moe_w_down, 1, norm_final, final_norm=True, n_first=Np)
    y_p = y_p.reshape(Bp, Tp, D)
    y_s = y_s.reshape(Bs, Ts, D)
    r4 = lambda a, b, t: a.reshape(1, b, t, N_KV_HEADS, HEAD_DIM)
    return (y_p, y_s,
            r4(k_p, Bp, Tp), r4(v_p, Bp, Tp), kiwi_p[:, :, :IDX_DIM][None], st_p[:, 1:][None],
            r4(k_s, Bs, Ts), r4(v_s, Bs, Ts), kiwi_s[:, :, :IDX_DIM][None], st_s[:, 1:][None],
            gm_v.reshape(1, Bs, Ts, D))
```

```python
import functools

import jax
import jax.numpy as jnp
from jax import lax
from jax.experimental import pallas as pl
from jax.experimental.pallas import tpu as pltpu

F32 = jnp.float32
BF16 = jnp.bfloat16
I32 = jnp.int32
U32 = jnp.uint32

LANES = 128
SUBLANES = 8
CHUNK = 64
POOL_WINDOWS = (2, 4, 8, 16)
POOL_GROUP_DIM = 128
POOL_WIDTH = 512
HIST_ROWS = 16
N_HEADS = 8
HEAD_DIM = 64
N_KV_HEADS = 4
Q_PER_KV = N_HEADS // N_KV_HEADS
ATT_WIDTH = N_HEADS * HEAD_DIM
KV_WIDTH = N_KV_HEADS * HEAD_DIM
N_IDX_HEADS = 8
IDX_DIM = 64
TOPK_MAX = 256
ROPE_THETA = 500000.0
ROT_HALF = HEAD_DIM // 8
GM_CHUNK = 128
GM_GROUPS = 8
N_EXPERT_GROUPS = 4
EXPERTS_PER_GROUP = 8
N_EXPERTS = 32
MOE_BLOCK = 512
RMS_EPS = 1e-6
LN_EPS = 1e-5

INT_MIN = -2147483648
LOG2_E = 1.4426950408889634
NEG_BIG = -1e30
VMEM_LIMIT = 48 * 1024 * 1024

TOKEN_TILE = 1024
PROJ_TILE = 512
KEY_BLOCK = 512
QUERY_TILE = 256
COUNT_SLAB = 64

IN_SIZES = (POOL_WIDTH, ATT_WIDTH, KV_WIDTH, KV_WIDTH, N_IDX_HEADS * IDX_DIM, IDX_DIM, N_IDX_HEADS)
COL_XP, COL_Q, COL_K, COL_V, COL_QI, COL_KIWI = (sum(IN_SIZES[:i]) for i in range(6))
IN_WIDTH_PAD = COL_KIWI + LANES


def _rms(x, g):
    return x * lax.rsqrt(jnp.mean(x * x, axis=-1, keepdims=True) + RMS_EPS) * g


def _rope128(x, c, sa, sb):
    return x * c + pltpu.roll(x, LANES - ROT_HALF, 1) * sa + pltpu.roll(x, ROT_HALF, 1) * sb


def _inproj_kernel(x_ref, g_ref, w_ref, rope_ref, ropeki_ref, hist_ref, pw_ref, ps_ref,
                   q_ref, qi_ref, k_ref, v_ref, kiwi_ref, kbf_ref, vbf_ref, kibf_ref, yp_ref, state_ref,
                   buf_ref, *, tm, pos0):
    j = pl.program_id(1)
    h = _rms(x_ref[0], g_ref[...])
    proj = jnp.dot(h.astype(BF16), w_ref[...], preferred_element_type=F32)
    c, sa, sb = rope_ref[0], rope_ref[1], rope_ref[2]

    def put_heads(ref, i, chunk):
        ref[0, 2 * i] = chunk[:, :HEAD_DIM].astype(BF16)
        ref[0, 2 * i + 1] = chunk[:, HEAD_DIM:].astype(BF16)

    for i in range(ATT_WIDTH // LANES):
        put_heads(q_ref, i, _rope128(proj[:, COL_Q + i * LANES:COL_Q + (i + 1) * LANES], c, sa, sb)
                  * (HEAD_DIM ** -0.5 * LOG2_E))
        put_heads(qi_ref, i, _rope128(proj[:, COL_QI + i * LANES:COL_QI + (i + 1) * LANES], c, sa, sb))
    for i in range(KV_WIDTH // LANES):
        sl = slice(i * LANES, (i + 1) * LANES)
        kr = _rope128(proj[:, COL_K + i * LANES:COL_K + (i + 1) * LANES], c, sa, sb)
        k_ref[0, :, sl] = kr
        put_heads(kbf_ref, i, kr)
    vv = proj[:, COL_V:COL_V + KV_WIDTH]
    v_ref[0] = vv
    if tm % LANES == 0:
        for cc in range(tm // LANES):
            vbf_ref[0, cc] = jnp.transpose(vv[cc * LANES:(cc + 1) * LANES, :]).astype(BF16)
    else:
        vbf_ref[0] = vv.astype(BF16)
    kiwi = _rope128(proj[:, COL_KIWI:COL_KIWI + LANES], ropeki_ref[0], ropeki_ref[1], ropeki_ref[2])
    kiwi_ref[0] = kiwi
    kibf_ref[0] = kiwi[:, :IDX_DIM].astype(BF16)

    @pl.when(j == 0)
    def _():
        buf_ref[0:HIST_ROWS, :] = hist_ref[0]

    xp = proj[:, COL_XP:COL_XP + POOL_WIDTH]
    buf_ref[HIST_ROWS:HIST_ROWS + tm, :] = xp
    pos = pos0 + j * tm + lax.broadcasted_iota(I32, (tm, 1), 0)
    for gi, w in enumerate(POOL_WINDOWS):
        c0 = gi * POOL_GROUP_DIM
        s = xp[:, c0:c0 + POOL_GROUP_DIM]
        for i in range(1, w):
            s = s + buf_ref[HIST_ROWS - i:HIST_ROWS - i + tm, c0:c0 + POOL_GROUP_DIM]
        cnt = jnp.minimum(pos + 1, w).astype(F32)
        d = s / cnt - xp[:, c0:c0 + POOL_GROUP_DIM]
        y = jnp.dot(d.astype(BF16), pw_ref[gi], preferred_element_type=F32)
        yp_ref[0, :, c0:c0 + POOL_GROUP_DIM] = (y * ps_ref[:, c0:c0 + POOL_GROUP_DIM]).astype(BF16)
    tail = buf_ref[tm:tm + HIST_ROWS, :]
    state_ref[0] = tail
    buf_ref[0:HIST_ROWS, :] = tail


def _inproj(x, g, w_bf, rope, ropeki, hist, pw_bf, ps, *, tm, pos0):
    B, T, D = x.shape
    nt = T // tm
    f = lambda shape, dt: jax.ShapeDtypeStruct(shape, dt)
    v_t = tm % LANES == 0
    out_shape = (
        f((B, N_HEADS, T, HEAD_DIM), BF16), f((B, N_IDX_HEADS, T, IDX_DIM), BF16),
        f((B, T, KV_WIDTH), F32), f((B, T, KV_WIDTH), F32), f((B, T, LANES), F32),
        f((B, N_KV_HEADS, T, HEAD_DIM), BF16),
        f((B, T // LANES, KV_WIDTH, LANES) if v_t else (B, T, KV_WIDTH), BF16),
        f((B, T, IDX_DIM), BF16),
        f((B, T, POOL_WIDTH), BF16), f((B, HIST_ROWS, POOL_WIDTH), F32),
    )
    tile = lambda wdt: pl.BlockSpec((1, tm, wdt), lambda b, j: (b, j, 0))
    heads = lambda n, wdt: pl.BlockSpec((1, n, tm, wdt), lambda b, j: (b, 0, j, 0))
    const2 = lambda s: pl.BlockSpec(s, lambda b, j: (0, 0))
    in_specs = [
        tile(D), const2((1, D)), const2((D, IN_WIDTH_PAD)),
        pl.BlockSpec((3, tm, LANES), lambda b, j: (0, j, 0)),
        pl.BlockSpec((3, tm, LANES), lambda b, j: (0, j, 0)),
        pl.BlockSpec((1, HIST_ROWS, POOL_WIDTH), lambda b, j: (b, 0, 0)),
        pl.BlockSpec((len(POOL_WINDOWS), POOL_GROUP_DIM, POOL_GROUP_DIM), lambda b, j: (0, 0, 0)),
        const2((1, POOL_WIDTH)),
    ]
    out_specs = (
        heads(N_HEADS, HEAD_DIM), heads(N_IDX_HEADS, IDX_DIM), tile(KV_WIDTH), tile(KV_WIDTH), tile(LANES),
        heads(N_KV_HEADS, HEAD_DIM),
        pl.BlockSpec((1, tm // LANES, KV_WIDTH, LANES), lambda b, j: (b, j, 0, 0)) if v_t else tile(KV_WIDTH),
        tile(IDX_DIM), tile(POOL_WIDTH),
        pl.BlockSpec((1, HIST_ROWS, POOL_WIDTH), lambda b, j: (b, 0, 0)),
    )
    return pl.pallas_call(
        functools.partial(_inproj_kernel, tm=tm, pos0=pos0),
        out_shape=out_shape, grid=(B, nt), in_specs=in_specs, out_specs=out_specs,
        scratch_shapes=[pltpu.VMEM((HIST_ROWS + tm, POOL_WIDTH), F32)],
        compiler_params=pltpu.CompilerParams(dimension_semantics=("parallel", "arbitrary"),
                                             vmem_limit_bytes=VMEM_LIMIT),
        name="inproj",
    )(x, g, w_bf, rope, ropeki, hist, pw_bf, ps)


def _dsa_kernel(q_ref, qi_ref, kiwi_ref, k_ref, vt_ref, ki_ref, x_ref, yp_ref, wo_ref, *rest,
                tq, tk, pos0, n_keys, n_sel):
    o_ref, key_buf, bias_buf, m_scr, l_scr, acc_scr, s_scr = rest[-7:]
    slab = COUNT_SLAB
    j = pl.program_id(1)
    base = pos0 + j * tq
    pos = base + lax.broadcasted_iota(I32, (1, tq), 1)
    limit = jnp.minimum((pos // CHUNK + 1) * CHUNK, n_keys)
    limit_max = jnp.minimum(((base + tq - 1) // CHUNK + 1) * CHUNK, n_keys)
    nkb = (limit_max + tk - 1) // tk
    nt = (((1,), (1,)), ((), ()))

    wi_t = jnp.transpose(kiwi_ref[0])[IDX_DIM:IDX_DIM + N_IDX_HEADS, :]

    def score_block(kb, carry):
        off = pl.multiple_of(kb * tk, tk)
        kiblk = ki_ref[0, pl.ds(off, tk), :]
        idx = jnp.zeros((tk, tq), F32)
        for h in range(N_IDX_HEADS):
            sc = lax.dot_general(kiblk, qi_ref[0, h], nt, preferred_element_type=F32)
            idx = idx + jnp.maximum(sc, 0.0) * wi_t[h:h + 1, :]
        idx = jnp.where(idx == 0.0, 0.0, idx)
        bits = lax.bitcast_convert_type(idx, I32)
        key = bits ^ ((bits >> 31) & 0x7FFFFFFF)
        key_buf[kb] = jnp.where(lax.broadcasted_iota(I32, (tk, tq), 0) < limit - kb * tk, key, INT_MIN)
        return carry

    lax.fori_loop(0, nkb, score_block, 0)

    def col_sum(a):
        return jnp.sum(a, axis=0, keepdims=True)

    def count_ge(cand):
        def body(kb, acc):
            kblk = key_buf[kb]
            for c in range(tk // slab):
                acc = acc + jnp.where(kblk[c * slab:(c + 1) * slab] >= cand, 1.0, 0.0)
            return acc
        return col_sum(lax.fori_loop(0, nkb, body, jnp.zeros((slab, tq), F32)))

    kf = float(n_sel)
    cnt0 = count_ge(jnp.zeros((1, tq), I32))
    t0 = jnp.where(cnt0 >= kf, 0, INT_MIN).astype(I32)

    def bit_body(i, carry):
        t, cnt = carry
        cand = t | lax.shift_left(jnp.int32(1), 30 - i)
        cnt_cand = count_ge(cand)
        keep = cnt_cand >= kf
        return jnp.where(keep, cand, t), jnp.where(keep, cnt_cand, cnt)

    t, cnt_ge = lax.fori_loop(0, 31, bit_body, (t0, cnt0))
    cnt_gt = count_ge(t + 1)
    need = kf - cnt_gt
    cnt_eq = cnt_ge - cnt_gt
    overfull = jnp.where(t != INT_MIN, cnt_eq - need, 0.0)
    slow = jnp.max(overfull) > 0.0

    @pl.when(jnp.logical_not(slow))
    def _():
        t_adm = jnp.maximum(t, INT_MIN + 1)

        def body(kb, carry):
            bias_buf[kb] = jnp.where(key_buf[kb] >= t_adm, 0.0, NEG_BIG)
            return carry
        lax.fori_loop(0, nkb, body, 0)

    @pl.when(slow)
    def _():
        tri = jnp.where(lax.broadcasted_iota(I32, (tk, tk), 1) <= lax.broadcasted_iota(I32, (tk, tk), 0),
                        1.0, 0.0).astype(BF16)

        def body(kb, seen):
            kblk = key_buf[kb]
            eq = jnp.where((kblk == t) & (kblk != INT_MIN), 1.0, 0.0)
            prefix = jnp.dot(tri, eq.astype(BF16), preferred_element_type=F32) + seen
            keep_tie = jnp.where(prefix <= need, eq, 0.0)
            sel = jnp.where(kblk > t, 1.0, keep_tie)
            bias_buf[kb] = jnp.where(sel > 0.0, 0.0, NEG_BIG)
            return seen + col_sum(eq)
        lax.fori_loop(0, nkb, body, jnp.zeros((1, tq), F32))

    sub = LANES
    nsub = tk // sub
    tl = LANES
    y_att = []
    for hq in range(tq // tl):
        lanes = slice(hq * tl, (hq + 1) * tl)
        nkb_h = (jnp.minimum(((base + (hq + 1) * tl - 1) // CHUNK + 1) * CHUNK, n_keys) + tk - 1) // tk
        m_scr[...] = jnp.full(m_scr.shape, NEG_BIG, F32)
        l_scr[...] = jnp.zeros(l_scr.shape, F32)
        acc_scr[...] = jnp.zeros(acc_scr.shape, F32)

        def attn_block(kb, carry, lanes=lanes):
            for c in range(nsub):
                off = pl.multiple_of(kb * tk + c * sub, sub)
                for g in range(N_KV_HEADS):
                    s_scr[c, g] = lax.dot_general(
                        k_ref[0, g, pl.ds(off, sub), :],
                        q_ref[0, Q_PER_KV * g:Q_PER_KV * (g + 1), lanes].reshape(Q_PER_KV * tl, HEAD_DIM),
                        nt, preferred_element_type=F32)
            m = [m_scr[g] for g in range(N_KV_HEADS)]
            l = [l_scr[g] for g in range(N_KV_HEADS)]
            for c in range(nsub):
                bias = bias_buf[kb, c * sub:(c + 1) * sub, lanes]
                bias2 = jnp.concatenate([bias] * Q_PER_KV, axis=1)
                for g in range(N_KV_HEADS):
                    s = s_scr[c, g] + bias2
                    m_new = jnp.maximum(m[g], jnp.max(s, axis=0, keepdims=True))
                    alpha = jnp.exp2(m[g] - m_new)
                    p = jnp.exp2(s - m_new)
                    l[g] = alpha * l[g] + col_sum(p)
                    pv = jnp.dot(vt_ref[0, kb * nsub + c, g * HEAD_DIM:(g + 1) * HEAD_DIM, :], p.astype(BF16),
                                 preferred_element_type=F32)
                    acc_scr[g] = alpha * acc_scr[g] + pv
                    m[g] = m_new
            for g in range(N_KV_HEADS):
                m_scr[g] = m[g]
                l_scr[g] = l[g]
            return carry

        lax.fori_loop(0, nkb_h, attn_block, 0)
        o_t = []
        for g in range(N_KV_HEADS):
            og = acc_scr[g] / l_scr[g]
            o_t.extend(og[:, hh * tl:(hh + 1) * tl] for hh in range(Q_PER_KV))
        y_att.append(jnp.transpose(jnp.concatenate(o_t, axis=0)).astype(BF16))
    y_att = jnp.concatenate(y_att, axis=0)

    y = jnp.dot(yp_ref[0], wo_ref[0:POOL_WIDTH, :], preferred_element_type=F32)
    y = y + jnp.dot(y_att, wo_ref[POOL_WIDTH:POOL_WIDTH + ATT_WIDTH, :], preferred_element_type=F32)
    o_ref[...] = (x_ref[0] + y).reshape(o_ref.shape)


def _dsa(q_hm, qi_hm, kiwi, k_hm, v_t, ki_all, x, yp, wo_bf, *, tq, tk, pos0, n_keys, n_sel, out_base=None):
    B, T, D = x.shape
    L = k_hm.shape[2]
    assert L % tk == 0 and T % tq == 0 and tq % LANES == 0
    nkb_max = L // tk
    tile = lambda wdt: pl.BlockSpec((1, tq, wdt), lambda b, j: (b, j, 0))
    in_specs = [pl.BlockSpec((1, N_HEADS, tq, HEAD_DIM), lambda b, j: (b, 0, j, 0)),
                pl.BlockSpec((1, N_IDX_HEADS, tq, IDX_DIM), lambda b, j: (b, 0, j, 0)),
                tile(LANES),
                pl.BlockSpec((1, N_KV_HEADS, L, HEAD_DIM), lambda b, j: (b, 0, 0, 0)),
                pl.BlockSpec((1, L // LANES, KV_WIDTH, LANES), lambda b, j: (b, 0, 0, 0)),
                pl.BlockSpec((1, L, IDX_DIM), lambda b, j: (b, 0, 0)),
                tile(D), tile(POOL_WIDTH), pl.BlockSpec((POOL_WIDTH + ATT_WIDTH, D), lambda b, j: (0, 0))]
    nq = T // tq
    args = (q_hm, qi_hm, kiwi, k_hm, v_t, ki_all, x, yp, wo_bf)
    if out_base is None:
        out_shape, out_spec, aliases = jax.ShapeDtypeStruct((B, T, D), F32), tile(D), {}
    else:
        out_shape = jax.ShapeDtypeStruct(out_base.shape, F32)
        out_spec = pl.BlockSpec((tq, D), lambda b, j: (b * nq + j, 0))
        in_specs, args, aliases = in_specs + [pl.BlockSpec(memory_space=pl.ANY)], args + (out_base,), {len(args): 0}
    return pl.pallas_call(
        functools.partial(_dsa_kernel, tq=tq, tk=tk, pos0=pos0, n_keys=n_keys, n_sel=n_sel),
        out_shape=out_shape, grid=(B, nq), in_specs=in_specs, out_specs=out_spec, input_output_aliases=aliases,
        scratch_shapes=[pltpu.VMEM((nkb_max, tk, tq), I32), pltpu.VMEM((nkb_max, tk, tq), F32),
                        pltpu.VMEM((N_KV_HEADS, 1, Q_PER_KV * LANES), F32),
                        pltpu.VMEM((N_KV_HEADS, 1, Q_PER_KV * LANES), F32),
                        pltpu.VMEM((N_KV_HEADS, HEAD_DIM, Q_PER_KV * LANES), F32),
                        pltpu.VMEM((tk // LANES, N_KV_HEADS, LANES, Q_PER_KV * LANES), F32)],
        compiler_params=pltpu.CompilerParams(dimension_semantics=("parallel", "arbitrary"),
                                             vmem_limit_bytes=VMEM_LIMIT),
        name="dsa",
    )(*args)


def _gmlp_kernel(x_ref, g_ref, win_ref, lng_ref, lnb_ref, ws_ref, bias_ref, wout_ref, o_ref, v_ref, *, tm):
    x = x_ref[...]
    h = _rms(x, g_ref[...])
    z = jax.nn.gelu(jnp.dot(h.astype(BF16), win_ref[...], preferred_element_type=F32))
    half = z.shape[1] // 2
    u, v = z[:, :half], z[:, half:]
    mu = jnp.mean(v, axis=-1, keepdims=True)
    var = jnp.mean(jnp.square(v - mu), axis=-1, keepdims=True)
    vn = (v - mu) * lax.rsqrt(var + LN_EPS) * lng_ref[...] + lnb_ref[...]
    v_ref[...] = vn
    gd = half // GM_GROUPS
    gated = []
    for c in range(tm // GM_CHUNK):
        rows = slice(c * GM_CHUNK, (c + 1) * GM_CHUNK)
        vc = vn[rows].astype(BF16)
        mixed = jnp.concatenate(
            [jnp.dot(ws_ref[0, g], vc[:, g * gd:(g + 1) * gd], preferred_element_type=F32)
             for g in range(GM_GROUPS)], axis=1) + bias_ref[0]
        gated.append((u[rows] * mixed).astype(BF16))
    gated = jnp.concatenate(gated, axis=0)
    o_ref[...] = x + jnp.dot(gated, wout_ref[...], preferred_element_type=F32)


def _gmlp(x, g, win_bf, lng, lnb, ws2, bias2, wout_bf, *, tm, n_first, n_v_rows):
    N, D = x.shape
    half = win_bf.shape[1] // 2
    nt = N // tm
    t_first = n_first // tm
    variant = lambda i: jnp.where(i >= t_first, 1, 0)
    row = pl.BlockSpec((tm, D), lambda i: (i, 0))
    const = lambda s: pl.BlockSpec(s, lambda i: (0, 0))
    in_specs = [row, const((1, D)), const((D, 2 * half)), const((1, half)), const((1, half)),
                pl.BlockSpec((1, GM_GROUPS, GM_CHUNK, GM_CHUNK), lambda i: (variant(i), 0, 0, 0)),
                pl.BlockSpec((1, GM_CHUNK, half), lambda i: (variant(i), 0, 0)),
                const((half, D))]
    out_specs = (row, pl.BlockSpec((tm, half), lambda i: (jnp.maximum(i - t_first, 0), 0)))
    return pl.pallas_call(
        functools.partial(_gmlp_kernel, tm=tm),
        out_shape=(jax.ShapeDtypeStruct((N, D), F32), jax.ShapeDtypeStruct((n_v_rows, half), F32)),
        grid=(nt,), in_specs=in_specs, out_specs=out_specs,
        compiler_params=pltpu.CompilerParams(dimension_semantics=("arbitrary",), vmem_limit_bytes=VMEM_LIMIT),
        name="gmlp",
    )(x, g, win_bf, lng, lnb, ws2, bias2, wout_bf)


ROUTE_E0, ROUTE_E1, ROUTE_R0, ROUTE_R1, ROUTE_G0, ROUTE_G1 = range(6)
ROUTE_ROWS = 8
N_LOGITS = N_EXPERT_GROUPS + N_EXPERTS
LOGIT_ROWS = -(-N_LOGITS // SUBLANES) * SUBLANES


def _router_kernel(x_ref, g_ref, wr_ref, route_ref, cnt_ref, carry_ref, *, tm):
    i = pl.program_id(0)

    @pl.when(i == 0)
    def _():
        carry_ref[...] = jnp.zeros(carry_ref.shape, F32)

    h = _rms(x_ref[...], g_ref[...])
    nt = (((1,), (1,)), ((), ()))
    logits = lax.dot_general(wr_ref[...], h.astype(BF16), nt, preferred_element_type=F32)[:LOGIT_ROWS]
    row = lax.broadcasted_iota(I32, (LOGIT_ROWS, tm), 0).astype(F32)
    ninf = -jnp.inf
    big = float(LANES)
    cmax = lambda a: jnp.max(a, axis=0, keepdims=True)
    cmin = lambda a: jnp.min(a, axis=0, keepdims=True)
    csum = lambda a: jnp.sum(a, axis=0, keepdims=True)

    is_grp = row < N_EXPERT_GROUPS
    lg = jnp.where(is_grp, logits, ninf)
    mg = cmax(lg)
    g_sel = cmin(jnp.where(lg == mg, row, big))
    p_grp = 1.0 / csum(jnp.where(is_grp, jnp.exp(lg - mg), 0.0))
    lo = N_EXPERT_GROUPS + g_sel * EXPERTS_PER_GROUP
    le = jnp.where((row >= lo) & (row < lo + EXPERTS_PER_GROUP), logits, ninf)
    v1 = cmax(le)
    j1 = cmin(jnp.where(le == v1, row, big))
    le2 = jnp.where(row == j1, ninf, le)
    v2 = cmax(le2)
    j2 = cmin(jnp.where(le2 == v2, row, big))
    e0 = j1 - N_EXPERT_GROUPS
    e1 = j2 - N_EXPERT_GROUPS
    r = jnp.exp(v2 - v1)
    g0 = p_grp / (1.0 + r)
    g1 = p_grp * r / (1.0 + r)

    erow = lax.broadcasted_iota(I32, (N_EXPERTS, tm), 0).astype(F32)
    oh0 = jnp.where(erow == e0, 1.0, 0.0)
    oh1 = jnp.where(erow == e1, 1.0, 0.0)
    oh = oh0 + oh1
    earlier = jnp.where(lax.broadcasted_iota(I32, (tm, tm), 0) < lax.broadcasted_iota(I32, (tm, tm), 1),
                        1.0, 0.0).astype(BF16)
    before = jnp.dot(oh.astype(BF16), earlier, preferred_element_type=F32) + carry_ref[...]
    r0 = csum(oh0 * before)
    r1 = csum(oh1 * before)
    carry_ref[...] = carry_ref[...] + jnp.sum(oh, axis=1, keepdims=True)
    cnt_ref[...] = jnp.broadcast_to(carry_ref[...], cnt_ref.shape)

    rows = {ROUTE_E0: e0, ROUTE_E1: e1, ROUTE_R0: r0, ROUTE_R1: r1, ROUTE_G0: g0, ROUTE_G1: g1}
    zero = jnp.zeros((1, tm), F32)
    route_ref[...] = jnp.concatenate([rows.get(k, zero) for k in range(ROUTE_ROWS)], axis=0)


def _router(x, g, wr_t, *, tm):
    N, D = x.shape
    const = lambda s: pl.BlockSpec(s, lambda i: (0, 0))
    return pl.pallas_call(
        functools.partial(_router_kernel, tm=tm),
        out_shape=(jax.ShapeDtypeStruct((ROUTE_ROWS, N), F32), jax.ShapeDtypeStruct((N_EXPERTS, LANES), F32)),
        grid=(N // tm,), in_specs=[pl.BlockSpec((tm, D), lambda i: (i, 0)), const((1, D)), const((LANES, D))],
        out_specs=(pl.BlockSpec((ROUTE_ROWS, tm), lambda i: (0, i)), const((N_EXPERTS, LANES))),
        scratch_shapes=[pltpu.VMEM((N_EXPERTS, 1), F32)],
        compiler_params=pltpu.CompilerParams(dimension_semantics=("arbitrary",), vmem_limit_bytes=VMEM_LIMIT),
        name="router",
    )(x, g, wr_t)


def _pack_bf16_pairs(h):
    half = h.shape[1] // 2
    lo = lax.bitcast_convert_type(h[:, :half].astype(BF16).astype(F32), U32)
    hi = lax.bitcast_convert_type(h[:, half:].astype(BF16).astype(F32), U32)
    return (lo >> 16) | (hi & jnp.uint32(0xFFFF0000))


def _unpack_bf16_pairs(w):
    lo = lax.bitcast_convert_type(w << 16, F32).astype(BF16)
    hi = lax.bitcast_convert_type(w & jnp.uint32(0xFFFF0000), F32).astype(BF16)
    return jnp.concatenate([lo, hi], axis=1)


def _dispatch_kernel(d0_ref, d1_ref, pad_ref, x_ref, g_ref, rows_ref, h_scr, zblk, sems, *, tm):
    i = pl.program_id(0)
    n = pl.num_programs(0)
    slot = i % 2
    groups = tm // SUBLANES

    def wait_slot(s):
        for _ in range(2 * groups):
            pltpu.make_async_copy(h_scr.at[s, 0], rows_ref.at[pl.ds(0, SUBLANES)], sems.at[s]).wait()

    @pl.when(i == 0)
    def _():
        zblk[...] = jnp.zeros(zblk.shape, U32)
        n_blocks = rows_ref.shape[0] // MOE_BLOCK
        n_used = pad_ref[2, 0]

        def blk_copy(row0):
            return pltpu.make_async_copy(zblk, rows_ref.at[pl.ds(row0, MOE_BLOCK)], sems.at[2])

        for wait in (False, True):
            for e in range(N_EXPERTS):
                @pl.when(pad_ref[1, e] > 0)
                def _():
                    cp = blk_copy(pl.multiple_of(pad_ref[0, e], MOE_BLOCK))
                    cp.wait() if wait else cp.start()

            def unused(b, carry):
                cp = blk_copy(pl.multiple_of(b * MOE_BLOCK, MOE_BLOCK))
                cp.wait() if wait else cp.start()
                return carry
            lax.fori_loop(n_used, n_blocks, unused, 0)

    @pl.when(i >= 2)
    def _():
        wait_slot(slot)

    h_scr[slot] = _pack_bf16_pairs(_rms(x_ref[...], g_ref[...])).reshape(groups, SUBLANES, h_scr.shape[3])

    def body(k, carry):
        for u in range(SUBLANES):
            r = k * SUBLANES + u
            src = h_scr.at[slot, k, pl.ds(u, 1)]
            pltpu.make_async_copy(src, rows_ref.at[pl.ds(d0_ref[r], 1)], sems.at[slot]).start()
            pltpu.make_async_copy(src, rows_ref.at[pl.ds(d1_ref[r], 1)], sems.at[slot]).start()
        return carry

    lax.fori_loop(0, groups, body, 0)

    @pl.when(i == n - 1)
    def _():
        wait_slot(slot)

    @pl.when(jnp.logical_and(i == n - 1, n >= 2))
    def _():
        wait_slot(1 - slot)


def _dispatch(dest_flat, pad_info, x, g, *, tm, n_rows):
    N, D = x.shape
    nt = N // tm
    return pl.pallas_call(
        functools.partial(_dispatch_kernel, tm=tm),
        out_shape=jax.ShapeDtypeStruct((n_rows, D // 2), U32), grid=(nt,),
        in_specs=[pl.BlockSpec((tm,), lambda i: (i,), memory_space=pltpu.SMEM),
                  pl.BlockSpec((tm,), lambda i: (i + nt,), memory_space=pltpu.SMEM),
                  pl.BlockSpec(memory_space=pltpu.SMEM),
                  pl.BlockSpec((tm, D), lambda i: (i, 0)),
                  pl.BlockSpec((1, D), lambda i: (0, 0))],
        out_specs=pl.BlockSpec(memory_space=pl.ANY),
        scratch_shapes=[pltpu.VMEM((2, tm // SUBLANES, SUBLANES, D // 2), U32), pltpu.VMEM((MOE_BLOCK, D // 2), U32),
                        pltpu.SemaphoreType.DMA((3,))],
        compiler_params=pltpu.CompilerParams(dimension_semantics=("arbitrary",), vmem_limit_bytes=VMEM_LIMIT),
        name="dispatch",
    )(dest_flat, dest_flat, pad_info, x, g)


def _expert_kernel(be_ref, nu_ref, x_ref, wg_ref, wu_ref, wd_ref, y_ref, wg_bf, wu_bf, wd_bf):
    i = pl.program_id(0)

    @pl.when(jnp.logical_or(i == 0, be_ref[i] != be_ref[jnp.maximum(i - 1, 0)]))
    def _():
        wg_bf[...] = wg_ref[0, 0].astype(BF16)
        wu_bf[...] = wu_ref[0, 0].astype(BF16)
        wd_bf[...] = wd_ref[0, 0].astype(BF16)

    @pl.when(i < nu_ref[0])
    def _():
        x = _unpack_bf16_pairs(x_ref[...])
        a = jnp.dot(x, wg_bf[...], preferred_element_type=F32)
        u = jnp.dot(x, wu_bf[...], preferred_element_type=F32)
        act = (a * jax.nn.sigmoid(a)) * u
        y_ref[...] = jnp.dot(act.astype(BF16), wd_bf[...], preferred_element_type=F32)

    @pl.when(i >= nu_ref[0])
    def _():
        y_ref[...] = jnp.zeros(y_ref.shape, F32)


def _experts(block_e, n_used, x_rows, w_gate, w_up, w_down, layer):
    n_rows = x_rows.shape[0]
    D, de = w_gate.shape[2:]
    n_blocks = n_rows // MOE_BLOCK
    grid_spec = pltpu.PrefetchScalarGridSpec(
        num_scalar_prefetch=2, grid=(n_blocks,),
        in_specs=[pl.BlockSpec((MOE_BLOCK, D // 2), lambda i, be, nu: (jnp.minimum(i, nu[0] - 1), 0)),
                  pl.BlockSpec((1, 1, D, de), lambda i, be, nu: (layer, be[i], 0, 0)),
                  pl.BlockSpec((1, 1, D, de), lambda i, be, nu: (layer, be[i], 0, 0)),
                  pl.BlockSpec((1, 1, de, D), lambda i, be, nu: (layer, be[i], 0, 0))],
        out_specs=pl.BlockSpec((MOE_BLOCK, D), lambda i, be, nu: (i, 0)),
        scratch_shapes=[pltpu.VMEM((D, de), BF16), pltpu.VMEM((D, de), BF16), pltpu.VMEM((de, D), BF16)])
    return pl.pallas_call(
        _expert_kernel, out_shape=jax.ShapeDtypeStruct((n_rows, D), F32), grid_spec=grid_spec,
        compiler_params=pltpu.CompilerParams(dimension_semantics=("arbitrary",), vmem_limit_bytes=VMEM_LIMIT),
        name="experts",
    )(block_e, n_used, x_rows, w_gate, w_up, w_down)


def _combine_kernel(dc0_ref, dc1_ref, dn0_ref, dn1_ref, x_ref, route_ref, g_ref, y_ref, *rest,
                    tm, final_norm, t_first):
    *o_refs, ybuf, sems = rest
    i = pl.program_id(0)
    n = pl.num_programs(0)
    slot = i % 2
    groups = tm // SUBLANES
    D = x_ref.shape[1]

    def gather(d_refs, s):
        def body(k, carry):
            for u in range(SUBLANES):
                r = k * SUBLANES + u
                for a in range(2):
                    pltpu.make_async_copy(y_ref.at[pl.ds(d_refs[a][r], 1)], ybuf.at[s, a, k, pl.ds(u, 1)],
                                          sems.at[s]).start()
            return carry
        lax.fori_loop(0, groups, body, 0)

    @pl.when(i == 0)
    def _():
        gather((dc0_ref, dc1_ref), 0)

    @pl.when(i + 1 < n)
    def _():
        gather((dn0_ref, dn1_ref), 1 - slot)

    for _ in range(2 * groups):
        pltpu.make_async_copy(y_ref.at[pl.ds(0, SUBLANES)], ybuf.at[slot, 0, 0], sems.at[slot]).wait()
    route = jnp.transpose(route_ref[...])
    g0 = route[:, ROUTE_G0:ROUTE_G0 + 1]
    g1 = route[:, ROUTE_G1:ROUTE_G1 + 1]
    out = x_ref[...] + (ybuf[slot, 0].reshape(tm, D) * g0 + ybuf[slot, 1].reshape(tm, D) * g1)
    if final_norm:
        out = _rms(out, g_ref[...])
    if t_first is None:
        o_refs[0][...] = out
    else:
        @pl.when(i < t_first)
        def _():
            o_refs[0][...] = out

        @pl.when(i >= t_first)
        def _():
            o_refs[1][...] = out


def _combine(dest_flat, x, route, g, y_rows, *, tm, final_norm, n_first=None):
    N, D = x.shape
    nt = N // tm
    row = lambda wdt: pl.BlockSpec((tm, wdt), lambda i: (i, 0))
    nxt = lambda i: jnp.minimum(i + 1, nt - 1)
    smem = lambda f: pl.BlockSpec((tm,), f, memory_space=pltpu.SMEM)
    if n_first is None:
        t_first, out_shape, out_specs = None, jax.ShapeDtypeStruct((N, D), F32), row(D)
    else:
        t_first = n_first // tm
        out_shape = (jax.ShapeDtypeStruct((n_first, D), F32), jax.ShapeDtypeStruct((N - n_first, D), F32))
        out_specs = (pl.BlockSpec((tm, D), lambda i: (jnp.minimum(i, t_first - 1), 0)),
                     pl.BlockSpec((tm, D), lambda i: (jnp.maximum(i - t_first, 0), 0)))
    return pl.pallas_call(
        functools.partial(_combine_kernel, tm=tm, final_norm=final_norm, t_first=t_first),
        out_shape=out_shape, grid=(nt,),
        in_specs=[smem(lambda i: (i,)), smem(lambda i: (i + nt,)),
                  smem(lambda i: (nxt(i),)), smem(lambda i: (nxt(i) + nt,)),
                  row(D), pl.BlockSpec((ROUTE_ROWS, tm), lambda i: (0, i)), pl.BlockSpec((1, D), lambda i: (0, 0)),
                  pl.BlockSpec(memory_space=pl.ANY)],
        out_specs=out_specs,
        scratch_shapes=[pltpu.VMEM((2, 2, tm // SUBLANES, SUBLANES, D), F32), pltpu.SemaphoreType.DMA((2,))],
        compiler_params=pltpu.CompilerParams(dimension_semantics=("arbitrary",), vmem_limit_bytes=VMEM_LIMIT),
        name="combine",
    )(dest_flat, dest_flat, dest_flat, dest_flat, x, route, g, y_rows)


def _hier_moe(x, g_ffn, w_rg, w_re, w_gate, w_up, w_down, layer, g_final, *, final_norm, n_first=None):
    N, D = x.shape
    tm = TOKEN_TILE
    wr_t = jnp.concatenate([w_rg.T, w_re.T, jnp.zeros((LANES - N_LOGITS, D), F32)], axis=0).astype(BF16)
    g_ffn = g_ffn.reshape(1, D)
    route, counts = _router(x, g_ffn, wr_t, tm=tm)
    counts = counts[:, 0].astype(I32)
    eid_t = route[ROUTE_E0:ROUTE_E1 + 1].astype(I32)
    rank_t = route[ROUTE_R0:ROUTE_R1 + 1].astype(I32)
    padded = (counts + MOE_BLOCK - 1) // MOE_BLOCK * MOE_BLOCK
    pad_end = jnp.cumsum(padded)
    pad_start = pad_end - padded
    expert_ids = jnp.arange(N_EXPERTS, dtype=I32)
    start_of = jnp.sum(jnp.where(eid_t[:, :, None] == expert_ids, pad_start, 0), axis=-1)
    dest_t = start_of + rank_t
    n_blocks = -(-(2 * N) // MOE_BLOCK) + N_EXPERTS
    n_rows = n_blocks * MOE_BLOCK
    n_used = (pad_end[-1] // MOE_BLOCK).astype(I32)
    block_start = jnp.minimum(jnp.arange(n_blocks, dtype=I32), n_used - 1) * MOE_BLOCK
    block_e = jnp.minimum(jnp.sum(block_start[:, None] >= pad_end[None, :], axis=1), N_EXPERTS - 1).astype(I32)
    pad_info = jnp.stack([pad_end - MOE_BLOCK, padded - counts, jnp.broadcast_to(n_used, (N_EXPERTS,))])
    dest_flat = dest_t.reshape(-1)
    x_rows = _dispatch(dest_flat, pad_info, x, g_ffn, tm=tm, n_rows=n_rows)
    y_rows = _experts(block_e, n_used.reshape(1), x_rows, w_gate, w_up, w_down, layer)
    return _combine(dest_flat, x, route, g_final.reshape(1, D), y_rows, tm=tm, final_norm=final_norm,
                    n_first=n_first)


def _rope_tables(pos, rot_lanes):
    inv = 1.0 / (ROPE_THETA ** (jnp.arange(ROT_HALF, dtype=F32) / ROT_HALF))
    ang = pos.astype(F32)[:, None] * inv[None, :]
    cos, sin = jnp.cos(ang), jnp.sin(ang)
    lane = jnp.arange(LANES)
    r = lane % HEAD_DIM
    active = (lane < rot_lanes)
    first = active & (r < ROT_HALF)
    second = active & (r >= ROT_HALF) & (r < 2 * ROT_HALF)
    cos_l = cos[:, r % ROT_HALF]
    sin_l = sin[:, r % ROT_HALF]
    c = jnp.where((first | second)[None, :], cos_l, 1.0)
    sa = jnp.where(first[None, :], -sin_l, 0.0)
    sb = jnp.where(second[None, :], sin_l, 0.0)
    return jnp.stack([c, sa, sb]).astype(F32)


def _prep_w_in(w_in):
    D = w_in.shape[0]
    assert w_in.shape[1] == sum(IN_SIZES)
    pad = jnp.zeros((D, IN_WIDTH_PAD - w_in.shape[1]), w_in.dtype)
    return jnp.concatenate([w_in, pad], axis=1).astype(BF16)


def kernel(x_prompt, x_sample, cache_k, cache_v, cache_idx_k, state_pool, norm_mix, norm_ffn, norm_final,
           par_w_in, par_pool_w, par_pool_scale, par_w_out, gm_w_in, gm_ln_g, gm_ln_b, gm_ws, gm_bs, gm_w_out,
           moe_router_group, moe_router_expert, moe_w_gate, moe_w_up, moe_w_down):
    Bp, Tp, D = x_prompt.shape
    Bs, Ts, _ = x_sample.shape
    past = cache_k.shape[2]
    Np, Ns = Bp * Tp, Bs * Ts
    depth = norm_mix.shape[0]
    assert depth == 2 and Ts == CHUNK and Tp % PROJ_TILE == 0 and Tp % QUERY_TILE == 0 and Tp % TOKEN_TILE == 0 and Ns % TOKEN_TILE == 0

    w_in_bf = _prep_w_in(par_w_in[0])
    pw_bf = par_pool_w[0].astype(BF16)
    ps = par_pool_scale[0].reshape(1, POOL_WIDTH)
    wo_bf = par_w_out[0].astype(BF16)
    g_mix0 = norm_mix[0].reshape(1, D)
    pos_p = jnp.arange(Tp, dtype=I32)
    pos_s = past + jnp.arange(Ts, dtype=I32)

    hist_p = jnp.zeros((Bp, HIST_ROWS, POOL_WIDTH), F32)
    hist_s = jnp.pad(state_pool[0], ((0, 0), (1, 0), (0, 0)))
    tk = KEY_BLOCK
    (q_p, qi_p, k_p, v_p, kiwi_p, kbf_p, vbf_p, kibf_p, yp_p, st_p) = _inproj(
        x_prompt, g_mix0, w_in_bf, _rope_tables(pos_p, LANES), _rope_tables(pos_p, IDX_DIM), hist_p, pw_bf, ps,
        tm=PROJ_TILE, pos0=0)
    (q_s, qi_s, k_s, v_s, kiwi_s, kbf_s, vbf_s, kibf_s, yp_s, st_s) = _inproj(
        x_sample, g_mix0, w_in_bf, _rope_tables(pos_s, LANES), _rope_tables(pos_s, IDX_DIM), hist_s, pw_bf, ps,
        tm=Ts, pos0=past)

    ls = past + Ts
    lsp = -(-ls // tk) * tk
    kpad = lambda a, ax: jnp.pad(a, [(0, lsp - ls) if d == ax else (0, 0) for d in range(a.ndim)])
    kall = kpad(jnp.concatenate([jnp.transpose(cache_k[0], (0, 2, 1, 3)).astype(BF16), kbf_s], axis=2), 2)
    vall = kpad(jnp.concatenate([cache_v[0].reshape(Bs, past, KV_WIDTH).astype(BF16), vbf_s], axis=1), 1)
    vall_t = jnp.transpose(vall.reshape(Bs, lsp // LANES, LANES, KV_WIDTH), (0, 1, 3, 2))
    kiall = kpad(jnp.concatenate([cache_idx_k[0].astype(BF16), kibf_s], axis=1), 1)
    qpad = lambda a, ax: jnp.pad(a, [(0, LANES - Ts) if d == ax else (0, 0) for d in range(a.ndim)])
    x1_s = _dsa(qpad(q_s, 2), qpad(qi_s, 2), qpad(kiwi_s, 1), kall, vall_t, kiall, qpad(x_sample, 1), qpad(yp_s, 1),
                wo_bf, tq=LANES, tk=tk, pos0=past, n_keys=ls, n_sel=min(TOPK_MAX, ls // 4))[:, :Ts]
    assert Tp % tk == 0
    x = jnp.pad(x1_s.reshape(Ns, D), ((Np, 0), (0, 0)))
    x = _dsa(q_p, qi_p, kiwi_p, kbf_p, vbf_p, kibf_p, x_prompt, yp_p, wo_bf,
             tq=QUERY_TILE, tk=tk, pos0=0, n_keys=Tp, n_sel=min(TOPK_MAX, Tp // 4), out_base=x)
    x = _hier_moe(x, norm_ffn[0], moe_router_group[0], moe_router_expert[0],
                  moe_w_gate, moe_w_up, moe_w_down, 0, norm_final, final_norm=False)

    cs = Ts
    tril = lambda n: jnp.tril(jnp.ones((n, n), bool))
    ws_p = jnp.where(tril(GM_CHUNK)[None], gm_ws[0], 0.0)
    ws_small = jnp.where(tril(cs)[None], gm_ws[0][:, :cs, :cs], 0.0)
    rep = GM_CHUNK // cs
    ws_s = jnp.einsum('ab,gts->gatbs', jnp.eye(rep, dtype=F32), ws_small).reshape(GM_GROUPS, GM_CHUNK, GM_CHUNK)
    ws2 = jnp.stack([ws_p, ws_s]).astype(BF16)
    gd = D // GM_GROUPS
    bias_p = jnp.repeat(jnp.transpose(gm_bs[0]), gd, axis=1)
    bias_s = jnp.tile(jnp.repeat(jnp.transpose(gm_bs[0][:, :cs]), gd, axis=1), (rep, 1))
    bias2 = jnp.stack([bias_p, bias_s])
    x, gm_v = _gmlp(x, norm_mix[1].reshape(1, D), gm_w_in[0].astype(BF16), gm_ln_g[0].reshape(1, D),
                    gm_ln_b[0].reshape(1, D), ws2, bias2, gm_w_out[0].astype(BF16),
                    tm=TOKEN_TILE, n_first=Np, n_v_rows=Ns)
    y_p, y_s = _hier_moe(x, norm_ffn[1], moe_router_group[1], moe_router_expert[1],
                         moe_w_gate, moe_w_up, moe_w_down, 1, norm_final, final_norm=True, n_first=Np)
    y_p = y_p.reshape(Bp, Tp, D)
    y_s = y_s.reshape(Bs, Ts, D)
    r4 = lambda a, b, t: a.reshape(1, b, t, N_KV_HEADS, HEAD_DIM)
    return (y_p, y_s,
            r4(k_p, Bp, Tp), r4(v_p, Bp, Tp), kiwi_p[:, :, :IDX_DIM][None], st_p[:, 1:][None],
            r4(k_s, Bs, Ts), r4(v_s, Bs, Ts), kiwi_s[:, :, :IDX_DIM][None], st_s[:, 1:][None],
            gm_v.reshape(1, Bs, Ts, D))
```

```python
import functools

import jax
import jax.numpy as jnp
from jax import lax
from jax.experimental import pallas as pl
from jax.experimental.pallas import tpu as pltpu

F32 = jnp.float32
BF16 = jnp.bfloat16
I32 = jnp.int32
U32 = jnp.uint32

LANES = 128
SUBLANES = 8
CHUNK = 64
POOL_WINDOWS = (2, 4, 8, 16)
POOL_GROUP_DIM = 128
POOL_WIDTH = 512
HIST_ROWS = 16
N_HEADS = 8
HEAD_DIM = 64
N_KV_HEADS = 4
Q_PER_KV = N_HEADS // N_KV_HEADS
ATT_WIDTH = N_HEADS * HEAD_DIM
KV_WIDTH = N_KV_HEADS * HEAD_DIM
N_IDX_HEADS = 8
IDX_DIM = 64
TOPK_MAX = 256
ROPE_THETA = 500000.0
ROT_HALF = HEAD_DIM // 8
GM_CHUNK = 128
GM_GROUPS = 8
N_EXPERT_GROUPS = 4
EXPERTS_PER_GROUP = 8
N_EXPERTS = 32
MOE_BLOCK = 512
RMS_EPS = 1e-6
LN_EPS = 1e-5

INT_MIN = -2147483648
LOG2_E = 1.4426950408889634
NEG_BIG = -1e30
VMEM_LIMIT = 48 * 1024 * 1024

TOKEN_TILE = 1024
PROJ_TILE = 512
KEY_BLOCK = 512
QUERY_TILE = 512
COUNT_SLAB = 64

IN_SIZES = (POOL_WIDTH, ATT_WIDTH, KV_WIDTH, KV_WIDTH, N_IDX_HEADS * IDX_DIM, IDX_DIM, N_IDX_HEADS)
COL_XP, COL_Q, COL_K, COL_V, COL_QI, COL_KIWI = (sum(IN_SIZES[:i]) for i in range(6))
IN_WIDTH_PAD = COL_KIWI + LANES


def _rms(x, g):
    return x * lax.rsqrt(jnp.mean(x * x, axis=-1, keepdims=True) + RMS_EPS) * g


def _rope128(x, c, sa, sb):
    return x * c + pltpu.roll(x, LANES - ROT_HALF, 1) * sa + pltpu.roll(x, ROT_HALF, 1) * sb


def _inproj_kernel(x_ref, g_ref, w_ref, rope_ref, ropeki_ref, hist_ref, pw_ref, ps_ref,
                   q_ref, qi_ref, k_ref, v_ref, kiwi_ref, kbf_ref, vbf_ref, kibf_ref, yp_ref, state_ref,
                   buf_ref, *, tm, pos0):
    j = pl.program_id(1)
    h = _rms(x_ref[0], g_ref[...])
    proj = jnp.dot(h.astype(BF16), w_ref[...], preferred_element_type=F32)
    c, sa, sb = rope_ref[0], rope_ref[1], rope_ref[2]

    def put_heads(ref, i, chunk):
        ref[0, 2 * i] = chunk[:, :HEAD_DIM].astype(BF16)
        ref[0, 2 * i + 1] = chunk[:, HEAD_DIM:].astype(BF16)

    for i in range(ATT_WIDTH // LANES):
        put_heads(q_ref, i, _rope128(proj[:, COL_Q + i * LANES:COL_Q + (i + 1) * LANES], c, sa, sb)
                  * (HEAD_DIM ** -0.5 * LOG2_E))
        put_heads(qi_ref, i, _rope128(proj[:, COL_QI + i * LANES:COL_QI + (i + 1) * LANES], c, sa, sb))
    for i in range(KV_WIDTH // LANES):
        sl = slice(i * LANES, (i + 1) * LANES)
        kr = _rope128(proj[:, COL_K + i * LANES:COL_K + (i + 1) * LANES], c, sa, sb)
        k_ref[0, :, sl] = kr
        put_heads(kbf_ref, i, kr)
    vv = proj[:, COL_V:COL_V + KV_WIDTH]
    v_ref[0] = vv
    if tm % LANES == 0:
        for cc in range(tm // LANES):
            vbf_ref[0, cc] = jnp.transpose(vv[cc * LANES:(cc + 1) * LANES, :]).astype(BF16)
    else:
        vbf_ref[0] = vv.astype(BF16)
    kiwi = _rope128(proj[:, COL_KIWI:COL_KIWI + LANES], ropeki_ref[0], ropeki_ref[1], ropeki_ref[2])
    kiwi_ref[0] = kiwi
    kibf_ref[0] = kiwi[:, :IDX_DIM].astype(BF16)

    @pl.when(j == 0)
    def _():
        buf_ref[0:HIST_ROWS, :] = hist_ref[0]

    xp = proj[:, COL_XP:COL_XP + POOL_WIDTH]
    buf_ref[HIST_ROWS:HIST_ROWS + tm, :] = xp
    pos = pos0 + j * tm + lax.broadcasted_iota(I32, (tm, 1), 0)
    for gi, w in enumerate(POOL_WINDOWS):
        c0 = gi * POOL_GROUP_DIM
        s = xp[:, c0:c0 + POOL_GROUP_DIM]
        for i in range(1, w):
            s = s + buf_ref[HIST_ROWS - i:HIST_ROWS - i + tm, c0:c0 + POOL_GROUP_DIM]
        cnt = jnp.minimum(pos + 1, w).astype(F32)
        d = s / cnt - xp[:, c0:c0 + POOL_GROUP_DIM]
        y = jnp.dot(d.astype(BF16), pw_ref[gi], preferred_element_type=F32)
        yp_ref[0, :, c0:c0 + POOL_GROUP_DIM] = (y * ps_ref[:, c0:c0 + POOL_GROUP_DIM]).astype(BF16)
    tail = buf_ref[tm:tm + HIST_ROWS, :]
    state_ref[0] = tail
    buf_ref[0:HIST_ROWS, :] = tail


def _inproj(x, g, w_bf, rope, ropeki, hist, pw_bf, ps, *, tm, pos0):
    B, T, D = x.shape
    nt = T // tm
    f = lambda shape, dt: jax.ShapeDtypeStruct(shape, dt)
    v_t = tm % LANES == 0
    out_shape = (
        f((B, N_HEADS, T, HEAD_DIM), BF16), f((B, N_IDX_HEADS, T, IDX_DIM), BF16),
        f((B, T, KV_WIDTH), F32), f((B, T, KV_WIDTH), F32), f((B, T, LANES), F32),
        f((B, N_KV_HEADS, T, HEAD_DIM), BF16),
        f((B, T // LANES, KV_WIDTH, LANES) if v_t else (B, T, KV_WIDTH), BF16),
        f((B, T, IDX_DIM), BF16),
        f((B, T, POOL_WIDTH), BF16), f((B, HIST_ROWS, POOL_WIDTH), F32),
    )
    tile = lambda wdt: pl.BlockSpec((1, tm, wdt), lambda b, j: (b, j, 0))
    heads = lambda n, wdt: pl.BlockSpec((1, n, tm, wdt), lambda b, j: (b, 0, j, 0))
    const2 = lambda s: pl.BlockSpec(s, lambda b, j: (0, 0))
    in_specs = [
        tile(D), const2((1, D)), const2((D, IN_WIDTH_PAD)),
        pl.BlockSpec((3, tm, LANES), lambda b, j: (0, j, 0)),
        pl.BlockSpec((3, tm, LANES), lambda b, j: (0, j, 0)),
        pl.BlockSpec((1, HIST_ROWS, POOL_WIDTH), lambda b, j: (b, 0, 0)),
        pl.BlockSpec((len(POOL_WINDOWS), POOL_GROUP_DIM, POOL_GROUP_DIM), lambda b, j: (0, 0, 0)),
        const2((1, POOL_WIDTH)),
    ]
    out_specs = (
        heads(N_HEADS, HEAD_DIM), heads(N_IDX_HEADS, IDX_DIM), tile(KV_WIDTH), tile(KV_WIDTH), tile(LANES),
        heads(N_KV_HEADS, HEAD_DIM),
        pl.BlockSpec((1, tm // LANES, KV_WIDTH, LANES), lambda b, j: (b, j, 0, 0)) if v_t else tile(KV_WIDTH),
        tile(IDX_DIM), tile(POOL_WIDTH),
        pl.BlockSpec((1, HIST_ROWS, POOL_WIDTH), lambda b, j: (b, 0, 0)),
    )
    return pl.pallas_call(
        functools.partial(_inproj_kernel, tm=tm, pos0=pos0),
        out_shape=out_shape, grid=(B, nt), in_specs=in_specs, out_specs=out_specs,
        scratch_shapes=[pltpu.VMEM((HIST_ROWS + tm, POOL_WIDTH), F32)],
        compiler_params=pltpu.CompilerParams(dimension_semantics=("parallel", "arbitrary"),
                                             vmem_limit_bytes=VMEM_LIMIT),
        name="inproj",
    )(x, g, w_bf, rope, ropeki, hist, pw_bf, ps)


def _dsa_kernel(q_ref, qi_ref, kiwi_ref, k_ref, vt_ref, ki_ref, x_ref, yp_ref, wo_ref, *rest,
                tq, tk, pos0, n_keys, n_sel):
    o_ref, key_buf, bias_buf, m_scr, l_scr, acc_scr, s_scr = rest[-7:]
    slab = COUNT_SLAB
    j = pl.program_id(1)
    base = pos0 + j * tq
    pos = base + lax.broadcasted_iota(I32, (1, tq), 1)
    limit = jnp.minimum((pos // CHUNK + 1) * CHUNK, n_keys)
    limit_max = jnp.minimum(((base + tq - 1) // CHUNK + 1) * CHUNK, n_keys)
    nkb = (limit_max + tk - 1) // tk
    nt = (((1,), (1,)), ((), ()))

    wi_t = jnp.transpose(kiwi_ref[0])[IDX_DIM:IDX_DIM + N_IDX_HEADS, :]

    def score_block(kb, carry):
        off = pl.multiple_of(kb * tk, tk)
        kiblk = ki_ref[0, pl.ds(off, tk), :]
        idx = jnp.zeros((tk, tq), F32)
        for h in range(N_IDX_HEADS):
            sc = lax.dot_general(kiblk, qi_ref[0, h], nt, preferred_element_type=F32)
            idx = idx + jnp.maximum(sc, 0.0) * wi_t[h:h + 1, :]
        idx = jnp.where(idx == 0.0, 0.0, idx)
        bits = lax.bitcast_convert_type(idx, I32)
        key = bits ^ ((bits >> 31) & 0x7FFFFFFF)
        key_buf[kb] = jnp.where(lax.broadcasted_iota(I32, (tk, tq), 0) < limit - kb * tk, key, INT_MIN)
        return carry

    lax.fori_loop(0, nkb, score_block, 0)

    def col_sum(a):
        return jnp.sum(a, axis=0, keepdims=True)

    def count_ge(cand):
        def body(kb, acc):
            kblk = key_buf[kb]
            for c in range(tk // slab):
                acc = acc + jnp.where(kblk[c * slab:(c + 1) * slab] >= cand, 1.0, 0.0)
            return acc
        return col_sum(lax.fori_loop(0, nkb, body, jnp.zeros((slab, tq), F32)))

    kf = float(n_sel)
    cnt0 = count_ge(jnp.zeros((1, tq), I32))
    t0 = jnp.where(cnt0 >= kf, 0, INT_MIN).astype(I32)

    def bit_body(i, carry):
        t, cnt = carry
        cand = t | lax.shift_left(jnp.int32(1), 30 - i)
        cnt_cand = count_ge(cand)
        keep = cnt_cand >= kf
        return jnp.where(keep, cand, t), jnp.where(keep, cnt_cand, cnt)

    t, cnt_ge = lax.fori_loop(0, 31, bit_body, (t0, cnt0))
    cnt_gt = count_ge(t + 1)
    need = kf - cnt_gt
    cnt_eq = cnt_ge - cnt_gt
    overfull = jnp.where(t != INT_MIN, cnt_eq - need, 0.0)
    slow = jnp.max(overfull) > 0.0

    @pl.when(jnp.logical_not(slow))
    def _():
        t_adm = jnp.maximum(t, INT_MIN + 1)

        def body(kb, carry):
            bias_buf[kb] = jnp.where(key_buf[kb] >= t_adm, 0.0, NEG_BIG)
            return carry
        lax.fori_loop(0, nkb, body, 0)

    @pl.when(slow)
    def _():
        tri = jnp.where(lax.broadcasted_iota(I32, (tk, tk), 1) <= lax.broadcasted_iota(I32, (tk, tk), 0),
                        1.0, 0.0).astype(BF16)

        def body(kb, seen):
            kblk = key_buf[kb]
            eq = jnp.where((kblk == t) & (kblk != INT_MIN), 1.0, 0.0)
            prefix = jnp.dot(tri, eq.astype(BF16), preferred_element_type=F32) + seen
            keep_tie = jnp.where(prefix <= need, eq, 0.0)
            sel = jnp.where(kblk > t, 1.0, keep_tie)
            bias_buf[kb] = jnp.where(sel > 0.0, 0.0, NEG_BIG)
            return seen + col_sum(eq)
        lax.fori_loop(0, nkb, body, jnp.zeros((1, tq), F32))

    sub = LANES
    nsub = tk // sub
    tl = LANES
    y_att = []
    for hq in range(tq // tl):
        lanes = slice(hq * tl, (hq + 1) * tl)
        nkb_h = (jnp.minimum(((base + (hq + 1) * tl - 1) // CHUNK + 1) * CHUNK, n_keys) + tk - 1) // tk
        m_scr[...] = jnp.full(m_scr.shape, NEG_BIG, F32)
        l_scr[...] = jnp.zeros(l_scr.shape, F32)
        acc_scr[...] = jnp.zeros(acc_scr.shape, F32)

        def attn_block(kb, carry, lanes=lanes):
            for c in range(nsub):
                off = pl.multiple_of(kb * tk + c * sub, sub)
                for g in range(N_KV_HEADS):
                    s_scr[c, g] = lax.dot_general(
                        k_ref[0, g, pl.ds(off, sub), :],
                        q_ref[0, Q_PER_KV * g:Q_PER_KV * (g + 1), lanes].reshape(Q_PER_KV * tl, HEAD_DIM),
                        nt, preferred_element_type=F32)
            m = [m_scr[g] for g in range(N_KV_HEADS)]
            l = [l_scr[g] for g in range(N_KV_HEADS)]
            for c in range(nsub):
                bias = bias_buf[kb, c * sub:(c + 1) * sub, lanes]
                bias2 = jnp.concatenate([bias] * Q_PER_KV, axis=1)
                for g in range(N_KV_HEADS):
                    s = s_scr[c, g] + bias2
                    m_new = jnp.maximum(m[g], jnp.max(s, axis=0, keepdims=True))
                    alpha = jnp.exp2(m[g] - m_new)
                    p = jnp.exp2(s - m_new)
                    l[g] = alpha * l[g] + col_sum(p)
                    pv = jnp.dot(vt_ref[0, kb * nsub + c, g * HEAD_DIM:(g + 1) * HEAD_DIM, :], p.astype(BF16),
                                 preferred_element_type=F32)
                    acc_scr[g] = alpha * acc_scr[g] + pv
                    m[g] = m_new
            for g in range(N_KV_HEADS):
                m_scr[g] = m[g]
                l_scr[g] = l[g]
            return carry

        lax.fori_loop(0, nkb_h, attn_block, 0)
        o_t = []
        for g in range(N_KV_HEADS):
            og = acc_scr[g] / l_scr[g]
            o_t.extend(og[:, hh * tl:(hh + 1) * tl] for hh in range(Q_PER_KV))
        y_att.append(jnp.transpose(jnp.concatenate(o_t, axis=0)).astype(BF16))
    y_att = jnp.concatenate(y_att, axis=0)

    y = jnp.dot(yp_ref[0], wo_ref[0:POOL_WIDTH, :], preferred_element_type=F32)
    y = y + jnp.dot(y_att, wo_ref[POOL_WIDTH:POOL_WIDTH + ATT_WIDTH, :], preferred_element_type=F32)
    o_ref[...] = (x_ref[0] + y).reshape(o_ref.shape)


def _dsa(q_hm, qi_hm, kiwi, k_hm, v_t, ki_all, x, yp, wo_bf, *, tq, tk, pos0, n_keys, n_sel, out_base=None):
    B, T, D = x.shape
    L = k_hm.shape[2]
    assert L % tk == 0 and T % tq == 0 and tq % LANES == 0
    nkb_max = L // tk
    tile = lambda wdt: pl.BlockSpec((1, tq, wdt), lambda b, j: (b, j, 0))
    in_specs = [pl.BlockSpec((1, N_HEADS, tq, HEAD_DIM), lambda b, j: (b, 0, j, 0)),
                pl.BlockSpec((1, N_IDX_HEADS, tq, IDX_DIM), lambda b, j: (b, 0, j, 0)),
                tile(LANES),
                pl.BlockSpec((1, N_KV_HEADS, L, HEAD_DIM), lambda b, j: (b, 0, 0, 0)),
                pl.BlockSpec((1, L // LANES, KV_WIDTH, LANES), lambda b, j: (b, 0, 0, 0)),
                pl.BlockSpec((1, L, IDX_DIM), lambda b, j: (b, 0, 0)),
                tile(D), tile(POOL_WIDTH), pl.BlockSpec((POOL_WIDTH + ATT_WIDTH, D), lambda b, j: (0, 0))]
    nq = T // tq
    args = (q_hm, qi_hm, kiwi, k_hm, v_t, ki_all, x, yp, wo_bf)
    if out_base is None:
        out_shape, out_spec, aliases = jax.ShapeDtypeStruct((B, T, D), F32), tile(D), {}
    else:
        out_shape = jax.ShapeDtypeStruct(out_base.shape, F32)
        out_spec = pl.BlockSpec((tq, D), lambda b, j: (b * nq + j, 0))
        in_specs, args, aliases = in_specs + [pl.BlockSpec(memory_space=pl.ANY)], args + (out_base,), {len(args): 0}
    return pl.pallas_call(
        functools.partial(_dsa_kernel, tq=tq, tk=tk, pos0=pos0, n_keys=n_keys, n_sel=n_sel),
        out_shape=out_shape, grid=(B, nq), in_specs=in_specs, out_specs=out_spec, input_output_aliases=aliases,
        scratch_shapes=[pltpu.VMEM((nkb_max, tk, tq), I32), pltpu.VMEM((nkb_max, tk, tq), F32),
                        pltpu.VMEM((N_KV_HEADS, 1, Q_PER_KV * LANES), F32),
                        pltpu.VMEM((N_KV_HEADS, 1, Q_PER_KV * LANES), F32),
                        pltpu.VMEM((N_KV_HEADS, HEAD_DIM, Q_PER_KV * LANES), F32),
                        pltpu.VMEM((tk // LANES, N_KV_HEADS, LANES, Q_PER_KV * LANES), F32)],
        compiler_params=pltpu.CompilerParams(dimension_semantics=("parallel", "arbitrary"),
                                             vmem_limit_bytes=VMEM_LIMIT),
        name="dsa",
    )(*args)


def _gmlp_kernel(x_ref, g_ref, win_ref, lng_ref, lnb_ref, ws_ref, bias_ref, wout_ref, o_ref, v_ref, *, tm):
    x = x_ref[...]
    h = _rms(x, g_ref[...])
    z = jax.nn.gelu(jnp.dot(h.astype(BF16), win_ref[...], preferred_element_type=F32))
    half = z.shape[1] // 2
    u, v = z[:, :half], z[:, half:]
    mu = jnp.mean(v, axis=-1, keepdims=True)
    var = jnp.mean(jnp.square(v - mu), axis=-1, keepdims=True)
    vn = (v - mu) * lax.rsqrt(var + LN_EPS) * lng_ref[...] + lnb_ref[...]
    v_ref[...] = vn
    gd = half // GM_GROUPS
    gated = []
    for c in range(tm // GM_CHUNK):
        rows = slice(c * GM_CHUNK, (c + 1) * GM_CHUNK)
        vc = vn[rows].astype(BF16)
        mixed = jnp.concatenate(
            [jnp.dot(ws_ref[0, g], vc[:, g * gd:(g + 1) * gd], preferred_element_type=F32)
             for g in range(GM_GROUPS)], axis=1) + bias_ref[0]
        gated.append((u[rows] * mixed).astype(BF16))
    gated = jnp.concatenate(gated, axis=0)
    o_ref[...] = x + jnp.dot(gated, wout_ref[...], preferred_element_type=F32)


def _gmlp(x, g, win_bf, lng, lnb, ws2, bias2, wout_bf, *, tm, n_first, n_v_rows):
    N, D = x.shape
    half = win_bf.shape[1] // 2
    nt = N // tm
    t_first = n_first // tm
    variant = lambda i: jnp.where(i >= t_first, 1, 0)
    row = pl.BlockSpec((tm, D), lambda i: (i, 0))
    const = lambda s: pl.BlockSpec(s, lambda i: (0, 0))
    in_specs = [row, const((1, D)), const((D, 2 * half)), const((1, half)), const((1, half)),
                pl.BlockSpec((1, GM_GROUPS, GM_CHUNK, GM_CHUNK), lambda i: (variant(i), 0, 0, 0)),
                pl.BlockSpec((1, GM_CHUNK, half), lambda i: (variant(i), 0, 0)),
                const((half, D))]
    out_specs = (row, pl.BlockSpec((tm, half), lambda i: (jnp.maximum(i - t_first, 0), 0)))
    return pl.pallas_call(
        functools.partial(_gmlp_kernel, tm=tm),
        out_shape=(jax.ShapeDtypeStruct((N, D), F32), jax.ShapeDtypeStruct((n_v_rows, half), F32)),
        grid=(nt,), in_specs=in_specs, out_specs=out_specs,
        compiler_params=pltpu.CompilerParams(dimension_semantics=("arbitrary",), vmem_limit_bytes=VMEM_LIMIT),
        name="gmlp",
    )(x, g, win_bf, lng, lnb, ws2, bias2, wout_bf)


ROUTE_E0, ROUTE_E1, ROUTE_R0, ROUTE_R1, ROUTE_G0, ROUTE_G1 = range(6)
ROUTE_ROWS = 8
N_LOGITS = N_EXPERT_GROUPS + N_EXPERTS
LOGIT_ROWS = -(-N_LOGITS // SUBLANES) * SUBLANES


def _router_kernel(x_ref, g_ref, wr_ref, route_ref, cnt_ref, carry_ref, *, tm):
    i = pl.program_id(0)

    @pl.when(i == 0)
    def _():
        carry_ref[...] = jnp.zeros(carry_ref.shape, F32)

    h = _rms(x_ref[...], g_ref[...])
    nt = (((1,), (1,)), ((), ()))
    logits = lax.dot_general(wr_ref[...], h.astype(BF16), nt, preferred_element_type=F32)[:LOGIT_ROWS]
    row = lax.broadcasted_iota(I32, (LOGIT_ROWS, tm), 0).astype(F32)
    ninf = -jnp.inf
    big = float(LANES)
    cmax = lambda a: jnp.max(a, axis=0, keepdims=True)
    cmin = lambda a: jnp.min(a, axis=0, keepdims=True)
    csum = lambda a: jnp.sum(a, axis=0, keepdims=True)

    is_grp = row < N_EXPERT_GROUPS
    lg = jnp.where(is_grp, logits, ninf)
    mg = cmax(lg)
    g_sel = cmin(jnp.where(lg == mg, row, big))
    p_grp = 1.0 / csum(jnp.where(is_grp, jnp.exp(lg - mg), 0.0))
    lo = N_EXPERT_GROUPS + g_sel * EXPERTS_PER_GROUP
    le = jnp.where((row >= lo) & (row < lo + EXPERTS_PER_GROUP), logits, ninf)
    v1 = cmax(le)
    j1 = cmin(jnp.where(le == v1, row, big))
    le2 = jnp.where(row == j1, ninf, le)
    v2 = cmax(le2)
    j2 = cmin(jnp.where(le2 == v2, row, big))
    e0 = j1 - N_EXPERT_GROUPS
    e1 = j2 - N_EXPERT_GROUPS
    r = jnp.exp(v2 - v1)
    g0 = p_grp / (1.0 + r)
    g1 = p_grp * r / (1.0 + r)

    erow = lax.broadcasted_iota(I32, (N_EXPERTS, tm), 0).astype(F32)
    oh0 = jnp.where(erow == e0, 1.0, 0.0)
    oh1 = jnp.where(erow == e1, 1.0, 0.0)
    oh = oh0 + oh1
    earlier = jnp.where(lax.broadcasted_iota(I32, (tm, tm), 0) < lax.broadcasted_iota(I32, (tm, tm), 1),
                        1.0, 0.0).astype(BF16)
    before = jnp.dot(oh.astype(BF16), earlier, preferred_element_type=F32) + carry_ref[...]
    r0 = csum(oh0 * before)
    r1 = csum(oh1 * before)
    carry_ref[...] = carry_ref[...] + jnp.sum(oh, axis=1, keepdims=True)
    cnt_ref[...] = jnp.broadcast_to(carry_ref[...], cnt_ref.shape)

    rows = {ROUTE_E0: e0, ROUTE_E1: e1, ROUTE_R0: r0, ROUTE_R1: r1, ROUTE_G0: g0, ROUTE_G1: g1}
    zero = jnp.zeros((1, tm), F32)
    route_ref[...] = jnp.concatenate([rows.get(k, zero) for k in range(ROUTE_ROWS)], axis=0)


def _router(x, g, wr_t, *, tm):
    N, D = x.shape
    const = lambda s: pl.BlockSpec(s, lambda i: (0, 0))
    return pl.pallas_call(
        functools.partial(_router_kernel, tm=tm),
        out_shape=(jax.ShapeDtypeStruct((ROUTE_ROWS, N), F32), jax.ShapeDtypeStruct((N_EXPERTS, LANES), F32)),
        grid=(N // tm,), in_specs=[pl.BlockSpec((tm, D), lambda i: (i, 0)), const((1, D)), const((LANES, D))],
        out_specs=(pl.BlockSpec((ROUTE_ROWS, tm), lambda i: (0, i)), const((N_EXPERTS, LANES))),
        scratch_shapes=[pltpu.VMEM((N_EXPERTS, 1), F32)],
        compiler_params=pltpu.CompilerParams(dimension_semantics=("arbitrary",), vmem_limit_bytes=VMEM_LIMIT),
        name="router",
    )(x, g, wr_t)


def _pack_bf16_pairs(h):
    half = h.shape[1] // 2
    lo = lax.bitcast_convert_type(h[:, :half].astype(BF16).astype(F32), U32)
    hi = lax.bitcast_convert_type(h[:, half:].astype(BF16).astype(F32), U32)
    return (lo >> 16) | (hi & jnp.uint32(0xFFFF0000))


def _unpack_bf16_pairs(w):
    lo = lax.bitcast_convert_type(w << 16, F32).astype(BF16)
    hi = lax.bitcast_convert_type(w & jnp.uint32(0xFFFF0000), F32).astype(BF16)
    return jnp.concatenate([lo, hi], axis=1)


def _dispatch_kernel(d0_ref, d1_ref, pad_ref, x_ref, g_ref, rows_ref, h_scr, zblk, sems, *, tm):
    i = pl.program_id(0)
    n = pl.num_programs(0)
    slot = i % 2
    groups = tm // SUBLANES

    def wait_slot(s):
        for _ in range(2 * groups):
            pltpu.make_async_copy(h_scr.at[s, 0], rows_ref.at[pl.ds(0, SUBLANES)], sems.at[s]).wait()

    @pl.when(i == 0)
    def _():
        zblk[...] = jnp.zeros(zblk.shape, U32)
        n_blocks = rows_ref.shape[0] // MOE_BLOCK
        n_used = pad_ref[2, 0]

        def blk_copy(row0):
            return pltpu.make_async_copy(zblk, rows_ref.at[pl.ds(row0, MOE_BLOCK)], sems.at[2])

        for wait in (False, True):
            for e in range(N_EXPERTS):
                @pl.when(pad_ref[1, e] > 0)
                def _():
                    cp = blk_copy(pl.multiple_of(pad_ref[0, e], MOE_BLOCK))
                    cp.wait() if wait else cp.start()

            def unused(b, carry):
                cp = blk_copy(pl.multiple_of(b * MOE_BLOCK, MOE_BLOCK))
                cp.wait() if wait else cp.start()
                return carry
            lax.fori_loop(n_used, n_blocks, unused, 0)

    @pl.when(i >= 2)
    def _():
        wait_slot(slot)

    h_scr[slot] = _pack_bf16_pairs(_rms(x_ref[...], g_ref[...])).reshape(groups, SUBLANES, h_scr.shape[3])

    def body(k, carry):
        for u in range(SUBLANES):
            r = k * SUBLANES + u
            src = h_scr.at[slot, k, pl.ds(u, 1)]
            pltpu.make_async_copy(src, rows_ref.at[pl.ds(d0_ref[r], 1)], sems.at[slot]).start()
            pltpu.make_async_copy(src, rows_ref.at[pl.ds(d1_ref[r], 1)], sems.at[slot]).start()
        return carry

    lax.fori_loop(0, groups, body, 0)

    @pl.when(i == n - 1)
    def _():
        wait_slot(slot)

    @pl.when(jnp.logical_and(i == n - 1, n >= 2))
    def _():
        wait_slot(1 - slot)


def _dispatch(dest_flat, pad_info, x, g, *, tm, n_rows):
    N, D = x.shape
    nt = N // tm
    return pl.pallas_call(
        functools.partial(_dispatch_kernel, tm=tm),
        out_shape=jax.ShapeDtypeStruct((n_rows, D // 2), U32), grid=(nt,),
        in_specs=[pl.BlockSpec((tm,), lambda i: (i,), memory_space=pltpu.SMEM),
                  pl.BlockSpec((tm,), lambda i: (i + nt,), memory_space=pltpu.SMEM),
                  pl.BlockSpec(memory_space=pltpu.SMEM),
                  pl.BlockSpec((tm, D), lambda i: (i, 0)),
                  pl.BlockSpec((1, D), lambda i: (0, 0))],
        out_specs=pl.BlockSpec(memory_space=pl.ANY),
        scratch_shapes=[pltpu.VMEM((2, tm // SUBLANES, SUBLANES, D // 2), U32), pltpu.VMEM((MOE_BLOCK, D // 2), U32),
                        pltpu.SemaphoreType.DMA((3,))],
        compiler_params=pltpu.CompilerParams(dimension_semantics=("arbitrary",), vmem_limit_bytes=VMEM_LIMIT),
        name="dispatch",
    )(dest_flat, dest_flat, pad_info, x, g)


def _expert_kernel(be_ref, nu_ref, x_ref, wg_ref, wu_ref, wd_ref, y_ref, wg_bf, wu_bf, wd_bf):
    i = pl.program_id(0)

    @pl.when(jnp.logical_or(i == 0, be_ref[i] != be_ref[jnp.maximum(i - 1, 0)]))
    def _():
        wg_bf[...] = wg_ref[0, 0].astype(BF16)
        wu_bf[...] = wu_ref[0, 0].astype(BF16)
        wd_bf[...] = wd_ref[0, 0].astype(BF16)

    @pl.when(i < nu_ref[0])
    def _():
        x = _unpack_bf16_pairs(x_ref[...])
        a = jnp.dot(x, wg_bf[...], preferred_element_type=F32)
        u = jnp.dot(x, wu_bf[...], preferred_element_type=F32)
        act = (a * jax.nn.sigmoid(a)) * u
        y_ref[...] = jnp.dot(act.astype(BF16), wd_bf[...], preferred_element_type=F32)

    @pl.when(i >= nu_ref[0])
    def _():
        y_ref[...] = jnp.zeros(y_ref.shape, F32)


def _experts(block_e, n_used, x_rows, w_gate, w_up, w_down, layer):
    n_rows = x_rows.shape[0]
    D, de = w_gate.shape[2:]
    n_blocks = n_rows // MOE_BLOCK
    grid_spec = pltpu.PrefetchScalarGridSpec(
        num_scalar_prefetch=2, grid=(n_blocks,),
        in_specs=[pl.BlockSpec((MOE_BLOCK, D // 2), lambda i, be, nu: (jnp.minimum(i, nu[0] - 1), 0)),
                  pl.BlockSpec((1, 1, D, de), lambda i, be, nu: (layer, be[i], 0, 0)),
                  pl.BlockSpec((1, 1, D, de), lambda i, be, nu: (layer, be[i], 0, 0)),
                  pl.BlockSpec((1, 1, de, D), lambda i, be, nu: (layer, be[i], 0, 0))],
        out_specs=pl.BlockSpec((MOE_BLOCK, D), lambda i, be, nu: (i, 0)),
        scratch_shapes=[pltpu.VMEM((D, de), BF16), pltpu.VMEM((D, de), BF16), pltpu.VMEM((de, D), BF16)])
    return pl.pallas_call(
        _expert_kernel, out_shape=jax.ShapeDtypeStruct((n_rows, D), F32), grid_spec=grid_spec,
        compiler_params=pltpu.CompilerParams(dimension_semantics=("arbitrary",), vmem_limit_bytes=VMEM_LIMIT),
        name="experts",
    )(block_e, n_used, x_rows, w_gate, w_up, w_down)


def _combine_kernel(dc0_ref, dc1_ref, dn0_ref, dn1_ref, x_ref, route_ref, g_ref, y_ref, *rest,
                    tm, final_norm, t_first):
    *o_refs, ybuf, sems = rest
    i = pl.program_id(0)
    n = pl.num_programs(0)
    slot = i % 2
    groups = tm // SUBLANES
    D = x_ref.shape[1]

    def gather(d_refs, s):
        def body(k, carry):
            for u in range(SUBLANES):
                r = k * SUBLANES + u
                for a in range(2):
                    pltpu.make_async_copy(y_ref.at[pl.ds(d_refs[a][r], 1)], ybuf.at[s, a, k, pl.ds(u, 1)],
                                          sems.at[s]).start()
            return carry
        lax.fori_loop(0, groups, body, 0)

    @pl.when(i == 0)
    def _():
        gather((dc0_ref, dc1_ref), 0)

    @pl.when(i + 1 < n)
    def _():
        gather((dn0_ref, dn1_ref), 1 - slot)

    for _ in range(2 * groups):
        pltpu.make_async_copy(y_ref.at[pl.ds(0, SUBLANES)], ybuf.at[slot, 0, 0], sems.at[slot]).wait()
    route = jnp.transpose(route_ref[...])
    g0 = route[:, ROUTE_G0:ROUTE_G0 + 1]
    g1 = route[:, ROUTE_G1:ROUTE_G1 + 1]
    out = x_ref[...] + (ybuf[slot, 0].reshape(tm, D) * g0 + ybuf[slot, 1].reshape(tm, D) * g1)
    if final_norm:
        out = _rms(out, g_ref[...])
    if t_first is None:
        o_refs[0][...] = out
    else:
        @pl.when(i < t_first)
        def _():
            o_refs[0][...] = out

        @pl.when(i >= t_first)
        def _():
            o_refs[1][...] = out


def _combine(dest_flat, x, route, g, y_rows, *, tm, final_norm, n_first=None):
    N, D = x.shape
    nt = N // tm
    row = lambda wdt: pl.BlockSpec((tm, wdt), lambda i: (i, 0))
    nxt = lambda i: jnp.minimum(i + 1, nt - 1)
    smem = lambda f: pl.BlockSpec((tm,), f, memory_space=pltpu.SMEM)
    if n_first is None:
        t_first, out_shape, out_specs = None, jax.ShapeDtypeStruct((N, D), F32), row(D)
    else:
        t_first = n_first // tm
        out_shape = (jax.ShapeDtypeStruct((n_first, D), F32), jax.ShapeDtypeStruct((N - n_first, D), F32))
        out_specs = (pl.BlockSpec((tm, D), lambda i: (jnp.minimum(i, t_first - 1), 0)),
                     pl.BlockSpec((tm, D), lambda i: (jnp.maximum(i - t_first, 0), 0)))
    return pl.pallas_call(
        functools.partial(_combine_kernel, tm=tm, final_norm=final_norm, t_first=t_first),
        out_shape=out_shape, grid=(nt,),
        in_specs=[smem(lambda i: (i,)), smem(lambda i: (i + nt,)),
                  smem(lambda i: (nxt(i),)), smem(lambda i: (nxt(i) + nt,)),
                  row(D), pl.BlockSpec((ROUTE_ROWS, tm), lambda i: (0, i)), pl.BlockSpec((1, D), lambda i: (0, 0)),
                  pl.BlockSpec(memory_space=pl.ANY)],
        out_specs=out_specs,
        scratch_shapes=[pltpu.VMEM((2, 2, tm // SUBLANES, SUBLANES, D), F32), pltpu.SemaphoreType.DMA((2,))],
        compiler_params=pltpu.CompilerParams(dimension_semantics=("arbitrary",), vmem_limit_bytes=VMEM_LIMIT),
        name="combine",
    )(dest_flat, dest_flat, dest_flat, dest_flat, x, route, g, y_rows)


def _hier_moe(x, g_ffn, w_rg, w_re, w_gate, w_up, w_down, layer, g_final, *, final_norm, n_first=None):
    N, D = x.shape
    tm = TOKEN_TILE
    wr_t = jnp.concatenate([w_rg.T, w_re.T, jnp.zeros((LANES - N_LOGITS, D), F32)], axis=0).astype(BF16)
    g_ffn = g_ffn.reshape(1, D)
    route, counts = _router(x, g_ffn, wr_t, tm=tm)
    counts = counts[:, 0].astype(I32)
    eid_t = route[ROUTE_E0:ROUTE_E1 + 1].astype(I32)
    rank_t = route[ROUTE_R0:ROUTE_R1 + 1].astype(I32)
    padded = (counts + MOE_BLOCK - 1) // MOE_BLOCK * MOE_BLOCK
    pad_end = jnp.cumsum(padded)
    pad_start = pad_end - padded
    expert_ids = jnp.arange(N_EXPERTS, dtype=I32)
    start_of = jnp.sum(jnp.where(eid_t[:, :, None] == expert_ids, pad_start, 0), axis=-1)
    dest_t = start_of + rank_t
    n_blocks = -(-(2 * N) // MOE_BLOCK) + N_EXPERTS
    n_rows = n_blocks * MOE_BLOCK
    n_used = (pad_end[-1] // MOE_BLOCK).astype(I32)
    block_start = jnp.minimum(jnp.arange(n_blocks, dtype=I32), n_used - 1) * MOE_BLOCK
    block_e = jnp.minimum(jnp.sum(block_start[:, None] >= pad_end[None, :], axis=1), N_EXPERTS - 1).astype(I32)
    pad_info = jnp.stack([pad_end - MOE_BLOCK, padded - counts, jnp.broadcast_to(n_used, (N_EXPERTS,))])
    dest_flat = dest_t.reshape(-1)
    x_rows = _dispatch(dest_flat, pad_info, x, g_ffn, tm=tm, n_rows=n_rows)
    y_rows = _experts(block_e, n_used.reshape(1), x_rows, w_gate, w_up, w_down, layer)
    return _combine(dest_flat, x, route, g_final.reshape(1, D), y_rows, tm=tm, final_norm=final_norm,
                    n_first=n_first)


def _rope_tables(pos, rot_lanes):
    inv = 1.0 / (ROPE_THETA ** (jnp.arange(ROT_HALF, dtype=F32) / ROT_HALF))
    ang = pos.astype(F32)[:, None] * inv[None, :]
    cos, sin = jnp.cos(ang), jnp.sin(ang)
    lane = jnp.arange(LANES)
    r = lane % HEAD_DIM
    active = (lane < rot_lanes)
    first = active & (r < ROT_HALF)
    second = active & (r >= ROT_HALF) & (r < 2 * ROT_HALF)
    cos_l = cos[:, r % ROT_HALF]
    sin_l = sin[:, r % ROT_HALF]
    c = jnp.where((first | second)[None, :], cos_l, 1.0)
    sa = jnp.where(first[None, :], -sin_l, 0.0)
    sb = jnp.where(second[None, :], sin_l, 0.0)
    return jnp.stack([c, sa, sb]).astype(F32)


def _prep_w_in(w_in):
    D = w_in.shape[0]
    assert w_in.shape[1] == sum(IN_SIZES)
    pad = jnp.zeros((D, IN_WIDTH_PAD - w_in.shape[1]), w_in.dtype)
    return jnp.concatenate([w_in, pad], axis=1).astype(BF16)


def kernel(x_prompt, x_sample, cache_k, cache_v, cache_idx_k, state_pool, norm_mix, norm_ffn, norm_final,
           par_w_in, par_pool_w, par_pool_scale, par_w_out, gm_w_in, gm_ln_g, gm_ln_b, gm_ws, gm_bs, gm_w_out,
           moe_router_group, moe_router_expert, moe_w_gate, moe_w_up, moe_w_down):
    Bp, Tp, D = x_prompt.shape
    Bs, Ts, _ = x_sample.shape
    past = cache_k.shape[2]
    Np, Ns = Bp * Tp, Bs * Ts
    depth = norm_mix.shape[0]
    assert depth == 2 and Ts == CHUNK and Tp % PROJ_TILE == 0 and Tp % QUERY_TILE == 0 and Tp % TOKEN_TILE == 0 and Ns % TOKEN_TILE == 0

    w_in_bf = _prep_w_in(par_w_in[0])
    pw_bf = par_pool_w[0].astype(BF16)
    ps = par_pool_scale[0].reshape(1, POOL_WIDTH)
    wo_bf = par_w_out[0].astype(BF16)
    g_mix0 = norm_mix[0].reshape(1, D)
    pos_p = jnp.arange(Tp, dtype=I32)
    pos_s = past + jnp.arange(Ts, dtype=I32)

    hist_p = jnp.zeros((Bp, HIST_ROWS, POOL_WIDTH), F32)
    hist_s = jnp.pad(state_pool[0], ((0, 0), (1, 0), (0, 0)))
    tk = KEY_BLOCK
    (q_p, qi_p, k_p, v_p, kiwi_p, kbf_p, vbf_p, kibf_p, yp_p, st_p) = _inproj(
        x_prompt, g_mix0, w_in_bf, _rope_tables(pos_p, LANES), _rope_tables(pos_p, IDX_DIM), hist_p, pw_bf, ps,
        tm=PROJ_TILE, pos0=0)
    (q_s, qi_s, k_s, v_s, kiwi_s, kbf_s, vbf_s, kibf_s, yp_s, st_s) = _inproj(
        x_sample, g_mix0, w_in_bf, _rope_tables(pos_s, LANES), _rope_tables(pos_s, IDX_DIM), hist_s, pw_bf, ps,
        tm=Ts, pos0=past)

    ls = past + Ts
    lsp = -(-ls // tk) * tk
    kpad = lambda a, ax: jnp.pad(a, [(0, lsp - ls) if d == ax else (0, 0) for d in range(a.ndim)])
    kall = kpad(jnp.concatenate([jnp.transpose(cache_k[0], (0, 2, 1, 3)).astype(BF16), kbf_s], axis=2), 2)
    vall = kpad(jnp.concatenate([cache_v[0].reshape(Bs, past, KV_WIDTH).astype(BF16), vbf_s], axis=1), 1)
    vall_t = jnp.transpose(vall.reshape(Bs, lsp // LANES, LANES, KV_WIDTH), (0, 1, 3, 2))
    kiall = kpad(jnp.concatenate([cache_idx_k[0].astype(BF16), kibf_s], axis=1), 1)
    qpad = lambda a, ax: jnp.pad(a, [(0, LANES - Ts) if d == ax else (0, 0) for d in range(a.ndim)])
    x1_s = _dsa(qpad(q_s, 2), qpad(qi_s, 2), qpad(kiwi_s, 1), kall, vall_t, kiall, qpad(x_sample, 1), qpad(yp_s, 1),
                wo_bf, tq=LANES, tk=tk, pos0=past, n_keys=ls, n_sel=min(TOPK_MAX, ls // 4))[:, :Ts]
    assert Tp % tk == 0
    x = jnp.pad(x1_s.reshape(Ns, D), ((Np, 0), (0, 0)))
    x = _dsa(q_p, qi_p, kiwi_p, kbf_p, vbf_p, kibf_p, x_prompt, yp_p, wo_bf,
             tq=QUERY_TILE, tk=tk, pos0=0, n_keys=Tp, n_sel=min(TOPK_MAX, Tp // 4), out_base=x)
    x = _hier_moe(x, norm_ffn[0], moe_router_group[0], moe_router_expert[0],
                  moe_w_gate, moe_w_up, moe_w_down, 0, norm_final, final_norm=False)

    cs = Ts
    tril = lambda n: jnp.tril(jnp.ones((n, n), bool))
    ws_p = jnp.where(tril(GM_CHUNK)[None], gm_ws[0], 0.0)
    ws_small = jnp.where(tril(cs)[None], gm_ws[0][:, :cs, :cs], 0.0)
    rep = GM_CHUNK // cs
    ws_s = jnp.einsum('ab,gts->gatbs', jnp.eye(rep, dtype=F32), ws_small).reshape(GM_GROUPS, GM_CHUNK, GM_CHUNK)
    ws2 = jnp.stack([ws_p, ws_s]).astype(BF16)
    gd = D // GM_GROUPS
    bias_p = jnp.repeat(jnp.transpose(gm_bs[0]), gd, axis=1)
    bias_s = jnp.tile(jnp.repeat(jnp.transpose(gm_bs[0][:, :cs]), gd, axis=1), (rep, 1))
    bias2 = jnp.stack([bias_p, bias_s])
    x, gm_v = _gmlp(x, norm_mix[1].reshape(1, D), gm_w_in[0].astype(BF16), gm_ln_g[0].reshape(1, D),
                    gm_ln_b[0].reshape(1, D), ws2, bias2, gm_w_out[0].astype(BF16),
                    tm=TOKEN_TILE, n_first=Np, n_v_rows=Ns)
    y_p, y_s = _hier_moe(x, norm_ffn[1], moe_router_group[1], moe_router_expert[1],
                         moe_w_gate, moe_w_up, moe_w_down, 1, norm_final, final_norm=True, n_first=Np)
    y_p = y_p.reshape(Bp, Tp, D)
    y_s = y_s.reshape(Bs, Ts, D)
    r4 = lambda a, b, t: a.reshape(1, b, t, N_KV_HEADS, HEAD_DIM)
    return (y_p, y_s,
            r4(k_p, Bp, Tp), r4(v_p, Bp, Tp), kiwi_p[:, :, :IDX_DIM][None], st_p[:, 1:][None],
            r4(k_s, Bs, Ts), r4(v_s, Bs, Ts), kiwi_s[:, :, :IDX_DIM][None], st_s[:, 1:][None],
            gm_v.reshape(1, Bs, Ts, D))
```

```python
import functools

import jax
import jax.numpy as jnp
from jax import lax
from jax.experimental import pallas as pl
from jax.experimental.pallas import tpu as pltpu

F32 = jnp.float32
BF16 = jnp.bfloat16
I32 = jnp.int32
U32 = jnp.uint32

LANES = 128
SUBLANES = 8
CHUNK = 64
POOL_WINDOWS = (2, 4, 8, 16)
POOL_GROUP_DIM = 128
POOL_WIDTH = 512
HIST_ROWS = 16
N_HEADS = 8
HEAD_DIM = 64
N_KV_HEADS = 4
Q_PER_KV = N_HEADS // N_KV_HEADS
ATT_WIDTH = N_HEADS * HEAD_DIM
KV_WIDTH = N_KV_HEADS * HEAD_DIM
N_IDX_HEADS = 8
IDX_DIM = 64
TOPK_MAX = 256
ROPE_THETA = 500000.0
ROT_HALF = HEAD_DIM // 8
GM_CHUNK = 128
GM_GROUPS = 8
N_EXPERT_GROUPS = 4
EXPERTS_PER_GROUP = 8
N_EXPERTS = 32
MOE_BLOCK = 512
RMS_EPS = 1e-6
LN_EPS = 1e-5

INT_MIN = -2147483648
LOG2_E = 1.4426950408889634
NEG_BIG = -1e30
VMEM_LIMIT = 48 * 1024 * 1024

TOKEN_TILE = 1024
PROJ_TILE = 512
KEY_BLOCK = 512
QUERY_TILE = 256
COUNT_SLAB = 64

IN_SIZES = (POOL_WIDTH, ATT_WIDTH, KV_WIDTH, KV_WIDTH, N_IDX_HEADS * IDX_DIM, IDX_DIM, N_IDX_HEADS)
COL_XP, COL_Q, COL_K, COL_V, COL_QI, COL_KIWI = (sum(IN_SIZES[:i]) for i in range(6))
IN_WIDTH_PAD = COL_KIWI + LANES


def _rms(x, g):
    return x * lax.rsqrt(jnp.mean(x * x, axis=-1, keepdims=True) + RMS_EPS) * g


def _rope128(x, c, sa, sb):
    return x * c + pltpu.roll(x, LANES - ROT_HALF, 1) * sa + pltpu.roll(x, ROT_HALF, 1) * sb


def _inproj_kernel(x_ref, g_ref, w_ref, rope_ref, ropeki_ref, hist_ref, pw_ref, ps_ref,
                   q_ref, qi_ref, k_ref, v_ref, kiwi_ref, kbf_ref, vbf_ref, kibf_ref, yp_ref, state_ref,
                   buf_ref, *, tm, pos0):
    j = pl.program_id(1)
    h = _rms(x_ref[0], g_ref[...])
    proj = jnp.dot(h.astype(BF16), w_ref[...], preferred_element_type=F32)
    c, sa, sb = rope_ref[0], rope_ref[1], rope_ref[2]

    def put_heads(ref, i, chunk):
        ref[0, 2 * i] = chunk[:, :HEAD_DIM].astype(BF16)
        ref[0, 2 * i + 1] = chunk[:, HEAD_DIM:].astype(BF16)

    for i in range(ATT_WIDTH // LANES):
        put_heads(q_ref, i, _rope128(proj[:, COL_Q + i * LANES:COL_Q + (i + 1) * LANES], c, sa, sb)
                  * (HEAD_DIM ** -0.5 * LOG2_E))
        put_heads(qi_ref, i, _rope128(proj[:, COL_QI + i * LANES:COL_QI + (i + 1) * LANES], c, sa, sb))
    for i in range(KV_WIDTH // LANES):
        sl = slice(i * LANES, (i + 1) * LANES)
        kr = _rope128(proj[:, COL_K + i * LANES:COL_K + (i + 1) * LANES], c, sa, sb)
        k_ref[0, :, sl] = kr
        put_heads(kbf_ref, i, kr)
    vv = proj[:, COL_V:COL_V + KV_WIDTH]
    v_ref[0] = vv
    if tm % LANES == 0:
        for cc in range(tm // LANES):
            vbf_ref[0, cc] = jnp.transpose(vv[cc * LANES:(cc + 1) * LANES, :]).astype(BF16)
    else:
        vbf_ref[0] = vv.astype(BF16)
    kiwi = _rope128(proj[:, COL_KIWI:COL_KIWI + LANES], ropeki_ref[0], ropeki_ref[1], ropeki_ref[2])
    kiwi_ref[0] = kiwi
    kibf_ref[0] = kiwi[:, :IDX_DIM].astype(BF16)

    @pl.when(j == 0)
    def _():
        buf_ref[0:HIST_ROWS, :] = hist_ref[0]

    xp = proj[:, COL_XP:COL_XP + POOL_WIDTH]
    buf_ref[HIST_ROWS:HIST_ROWS + tm, :] = xp
    pos = pos0 + j * tm + lax.broadcasted_iota(I32, (tm, 1), 0)
    for gi, w in enumerate(POOL_WINDOWS):
        c0 = gi * POOL_GROUP_DIM
        s = xp[:, c0:c0 + POOL_GROUP_DIM]
        for i in range(1, w):
            s = s + buf_ref[HIST_ROWS - i:HIST_ROWS - i + tm, c0:c0 + POOL_GROUP_DIM]
        cnt = jnp.minimum(pos + 1, w).astype(F32)
        d = s / cnt - xp[:, c0:c0 + POOL_GROUP_DIM]
        y = jnp.dot(d.astype(BF16), pw_ref[gi], preferred_element_type=F32)
        yp_ref[0, :, c0:c0 + POOL_GROUP_DIM] = (y * ps_ref[:, c0:c0 + POOL_GROUP_DIM]).astype(BF16)
    tail = buf_ref[tm:tm + HIST_ROWS, :]
    state_ref[0] = tail
    buf_ref[0:HIST_ROWS, :] = tail


def _inproj(x, g, w_bf, rope, ropeki, hist, pw_bf, ps, *, tm, pos0):
    B, T, D = x.shape
    nt = T // tm
    f = lambda shape, dt: jax.ShapeDtypeStruct(shape, dt)
    v_t = tm % LANES == 0
    out_shape = (
        f((B, N_HEADS, T, HEAD_DIM), BF16), f((B, N_IDX_HEADS, T, IDX_DIM), BF16),
        f((B, T, KV_WIDTH), F32), f((B, T, KV_WIDTH), F32), f((B, T, LANES), F32),
        f((B, N_KV_HEADS, T, HEAD_DIM), BF16),
        f((B, T // LANES, KV_WIDTH, LANES) if v_t else (B, T, KV_WIDTH), BF16),
        f((B, T, IDX_DIM), BF16),
        f((B, T, POOL_WIDTH), BF16), f((B, HIST_ROWS, POOL_WIDTH), F32),
    )
    tile = lambda wdt: pl.BlockSpec((1, tm, wdt), lambda b, j: (b, j, 0))
    heads = lambda n, wdt: pl.BlockSpec((1, n, tm, wdt), lambda b, j: (b, 0, j, 0))
    const2 = lambda s: pl.BlockSpec(s, lambda b, j: (0, 0))
    in_specs = [
        tile(D), const2((1, D)), const2((D, IN_WIDTH_PAD)),
        pl.BlockSpec((3, tm, LANES), lambda b, j: (0, j, 0)),
        pl.BlockSpec((3, tm, LANES), lambda b, j: (0, j, 0)),
        pl.BlockSpec((1, HIST_ROWS, POOL_WIDTH), lambda b, j: (b, 0, 0)),
        pl.BlockSpec((len(POOL_WINDOWS), POOL_GROUP_DIM, POOL_GROUP_DIM), lambda b, j: (0, 0, 0)),
        const2((1, POOL_WIDTH)),
    ]
    out_specs = (
        heads(N_HEADS, HEAD_DIM), heads(N_IDX_HEADS, IDX_DIM), tile(KV_WIDTH), tile(KV_WIDTH), tile(LANES),
        heads(N_KV_HEADS, HEAD_DIM),
        pl.BlockSpec((1, tm // LANES, KV_WIDTH, LANES), lambda b, j: (b, j, 0, 0)) if v_t else tile(KV_WIDTH),
        tile(IDX_DIM), tile(POOL_WIDTH),
        pl.BlockSpec((1, HIST_ROWS, POOL_WIDTH), lambda b, j: (b, 0, 0)),
    )
    return pl.pallas_call(
        functools.partial(_inproj_kernel, tm=tm, pos0=pos0),
        out_shape=out_shape, grid=(B, nt), in_specs=in_specs, out_specs=out_specs,
        scratch_shapes=[pltpu.VMEM((HIST_ROWS + tm, POOL_WIDTH), F32)],
        compiler_params=pltpu.CompilerParams(dimension_semantics=("parallel", "arbitrary"),
                                             vmem_limit_bytes=VMEM_LIMIT),
        name="inproj",
    )(x, g, w_bf, rope, ropeki, hist, pw_bf, ps)


def _dsa_kernel(q_ref, qi_ref, kiwi_ref, k_ref, vt_ref, ki_ref, x_ref, yp_ref, wo_ref, *rest,
                tq, tk, pos0, n_keys, n_sel):
    o_ref, key_buf, bias_buf, m_scr, l_scr, acc_scr, s_scr = rest[-7:]
    slab = COUNT_SLAB
    j = pl.program_id(1)
    base = pos0 + j * tq
    pos = base + lax.broadcasted_iota(I32, (1, tq), 1)
    limit = jnp.minimum((pos // CHUNK + 1) * CHUNK, n_keys)
    limit_max = jnp.minimum(((base + tq - 1) // CHUNK + 1) * CHUNK, n_keys)
    nkb = (limit_max + tk - 1) // tk
    nt = (((1,), (1,)), ((), ()))

    wi_t = jnp.transpose(kiwi_ref[0])[IDX_DIM:IDX_DIM + N_IDX_HEADS, :]

    def score_block(kb, carry):
        off = pl.multiple_of(kb * tk, tk)
        kiblk = ki_ref[0, pl.ds(off, tk), :]
        idx = jnp.zeros((tk, tq), F32)
        for h in range(N_IDX_HEADS):
            sc = lax.dot_general(kiblk, qi_ref[0, h], nt, preferred_element_type=F32)
            idx = idx + jnp.maximum(sc, 0.0) * wi_t[h:h + 1, :]
        idx = jnp.where(idx == 0.0, 0.0, idx)
        bits = lax.bitcast_convert_type(idx, I32)
        key = bits ^ ((bits >> 31) & 0x7FFFFFFF)
        key_buf[kb] = jnp.where(lax.broadcasted_iota(I32, (tk, tq), 0) < limit - kb * tk, key, INT_MIN)
        return carry

    lax.fori_loop(0, nkb, score_block, 0)

    def col_sum(a):
        return jnp.sum(a, axis=0, keepdims=True)

    def count_ge(cand):
        def body(kb, acc):
            kblk = key_buf[kb]
            for c in range(tk // slab):
                acc = acc + jnp.where(kblk[c * slab:(c + 1) * slab] >= cand, 1.0, 0.0)
            return acc
        return col_sum(lax.fori_loop(0, nkb, body, jnp.zeros((slab, tq), F32)))

    kf = float(n_sel)
    cnt0 = count_ge(jnp.zeros((1, tq), I32))
    t0 = jnp.where(cnt0 >= kf, 0, INT_MIN).astype(I32)

    def bit_body(i, carry):
        t, cnt = carry
        cand = t | lax.shift_left(jnp.int32(1), 30 - i)
        cnt_cand = count_ge(cand)
        keep = cnt_cand >= kf
        return jnp.where(keep, cand, t), jnp.where(keep, cnt_cand, cnt)

    t, cnt_ge = lax.fori_loop(0, 31, bit_body, (t0, cnt0))
    cnt_gt = count_ge(t + 1)
    need = kf - cnt_gt
    cnt_eq = cnt_ge - cnt_gt
    overfull = jnp.where(t != INT_MIN, cnt_eq - need, 0.0)
    slow = jnp.max(overfull) > 0.0

    @pl.when(jnp.logical_not(slow))
    def _():
        t_adm = jnp.maximum(t, INT_MIN + 1)

        def body(kb, carry):
            bias_buf[kb] = jnp.where(key_buf[kb] >= t_adm, 0.0, NEG_BIG)
            return carry
        lax.fori_loop(0, nkb, body, 0)

    @pl.when(slow)
    def _():
        tri = jnp.where(lax.broadcasted_iota(I32, (tk, tk), 1) <= lax.broadcasted_iota(I32, (tk, tk), 0),
                        1.0, 0.0).astype(BF16)

        def body(kb, seen):
            kblk = key_buf[kb]
            eq = jnp.where((kblk == t) & (kblk != INT_MIN), 1.0, 0.0)
            prefix = jnp.dot(tri, eq.astype(BF16), preferred_element_type=F32) + seen
            keep_tie = jnp.where(prefix <= need, eq, 0.0)
            sel = jnp.where(kblk > t, 1.0, keep_tie)
            bias_buf[kb] = jnp.where(sel > 0.0, 0.0, NEG_BIG)
            return seen + col_sum(eq)
        lax.fori_loop(0, nkb, body, jnp.zeros((1, tq), F32))

    sub = LANES
    nsub = tk // sub
    tl = LANES
    n_tiles = tq // tl
    m_scr[...] = jnp.full(m_scr.shape, NEG_BIG, F32)
    l_scr[...] = jnp.zeros(l_scr.shape, F32)
    acc_scr[...] = jnp.zeros(acc_scr.shape, F32)

    def tile_blocks(hq):
        return (jnp.minimum(((base + (hq + 1) * tl - 1) // CHUNK + 1) * CHUNK, n_keys) + tk - 1) // tk

    def attn_block(kb, members):
        for mi, hq in enumerate(members):
            for c in range(nsub):
                off = pl.multiple_of(kb * tk + c * sub, sub)
                for g in range(N_KV_HEADS):
                    s_scr[mi, c, g] = lax.dot_general(
                        k_ref[0, g, pl.ds(off, sub), :],
                        q_ref[0, Q_PER_KV * g:Q_PER_KV * (g + 1), hq * tl:(hq + 1) * tl].reshape(Q_PER_KV * tl, HEAD_DIM),
                        nt, preferred_element_type=F32)
        for mi, hq in enumerate(members):
            m = [m_scr[hq, g] for g in range(N_KV_HEADS)]
            l = [l_scr[hq, g] for g in range(N_KV_HEADS)]
            for c in range(nsub):
                bias = bias_buf[kb, c * sub:(c + 1) * sub, hq * tl:(hq + 1) * tl]
                bias2 = jnp.concatenate([bias] * Q_PER_KV, axis=1)
                for g in range(N_KV_HEADS):
                    s = s_scr[mi, c, g] + bias2
                    m_new = jnp.maximum(m[g], jnp.max(s, axis=0, keepdims=True))
                    alpha = jnp.exp2(m[g] - m_new)
                    p = jnp.exp2(s - m_new)
                    l[g] = alpha * l[g] + col_sum(p)
                    pv = jnp.dot(vt_ref[0, kb * nsub + c, g * HEAD_DIM:(g + 1) * HEAD_DIM, :], p.astype(BF16),
                                 preferred_element_type=F32)
                    acc_scr[hq, g] = alpha * acc_scr[hq, g] + pv
                    m[g] = m_new
            for g in range(N_KV_HEADS):
                m_scr[hq, g] = m[g]
                l_scr[hq, g] = l[g]

    for first in range(0, n_tiles, 2):
        members = tuple(range(first, min(first + 2, n_tiles)))
        shared = tile_blocks(members[0])
        lax.fori_loop(0, shared, lambda kb, c, ms=members: (attn_block(kb, ms), c)[1], 0)
        if len(members) == 2:
            lax.fori_loop(shared, tile_blocks(members[1]),
                          lambda kb, c, ms=members[1:]: (attn_block(kb, ms), c)[1], 0)

    y_att = []
    for hq in range(n_tiles):
        o_t = []
        for g in range(N_KV_HEADS):
            og = acc_scr[hq, g] / l_scr[hq, g]
            o_t.extend(og[:, hh * tl:(hh + 1) * tl] for hh in range(Q_PER_KV))
        y_att.append(jnp.transpose(jnp.concatenate(o_t, axis=0)).astype(BF16))
    y_att = jnp.concatenate(y_att, axis=0)

    y = jnp.dot(yp_ref[0], wo_ref[0:POOL_WIDTH, :], preferred_element_type=F32)
    y = y + jnp.dot(y_att, wo_ref[POOL_WIDTH:POOL_WIDTH + ATT_WIDTH, :], preferred_element_type=F32)
    o_ref[...] = (x_ref[0] + y).reshape(o_ref.shape)


def _dsa(q_hm, qi_hm, kiwi, k_hm, v_t, ki_all, x, yp, wo_bf, *, tq, tk, pos0, n_keys, n_sel, out_base=None):
    B, T, D = x.shape
    L = k_hm.shape[2]
    assert L % tk == 0 and T % tq == 0 and tq % LANES == 0
    nkb_max = L // tk
    tile = lambda wdt: pl.BlockSpec((1, tq, wdt), lambda b, j: (b, j, 0))
    in_specs = [pl.BlockSpec((1, N_HEADS, tq, HEAD_DIM), lambda b, j: (b, 0, j, 0)),
                pl.BlockSpec((1, N_IDX_HEADS, tq, IDX_DIM), lambda b, j: (b, 0, j, 0)),
                tile(LANES),
                pl.BlockSpec((1, N_KV_HEADS, L, HEAD_DIM), lambda b, j: (b, 0, 0, 0)),
                pl.BlockSpec((1, L // LANES, KV_WIDTH, LANES), lambda b, j: (b, 0, 0, 0)),
                pl.BlockSpec((1, L, IDX_DIM), lambda b, j: (b, 0, 0)),
                tile(D), tile(POOL_WIDTH), pl.BlockSpec((POOL_WIDTH + ATT_WIDTH, D), lambda b, j: (0, 0))]
    nq = T // tq
    args = (q_hm, qi_hm, kiwi, k_hm, v_t, ki_all, x, yp, wo_bf)
    if out_base is None:
        out_shape, out_spec, aliases = jax.ShapeDtypeStruct((B, T, D), F32), tile(D), {}
    else:
        out_shape = jax.ShapeDtypeStruct(out_base.shape, F32)
        out_spec = pl.BlockSpec((tq, D), lambda b, j: (b * nq + j, 0))
        in_specs, args, aliases = in_specs + [pl.BlockSpec(memory_space=pl.ANY)], args + (out_base,), {len(args): 0}
    return pl.pallas_call(
        functools.partial(_dsa_kernel, tq=tq, tk=tk, pos0=pos0, n_keys=n_keys, n_sel=n_sel),
        out_shape=out_shape, grid=(B, nq), in_specs=in_specs, out_specs=out_spec, input_output_aliases=aliases,
        scratch_shapes=[pltpu.VMEM((nkb_max, tk, tq), I32), pltpu.VMEM((nkb_max, tk, tq), F32),
                        pltpu.VMEM((tq // LANES, N_KV_HEADS, 1, Q_PER_KV * LANES), F32),
                        pltpu.VMEM((tq // LANES, N_KV_HEADS, 1, Q_PER_KV * LANES), F32),
                        pltpu.VMEM((tq // LANES, N_KV_HEADS, HEAD_DIM, Q_PER_KV * LANES), F32),
                        pltpu.VMEM((2, tk // LANES, N_KV_HEADS, LANES, Q_PER_KV * LANES), F32)],
        compiler_params=pltpu.CompilerParams(dimension_semantics=("parallel", "arbitrary"),
                                             vmem_limit_bytes=VMEM_LIMIT),
        name="dsa",
    )(*args)


def _gmlp_kernel(x_ref, g_ref, win_ref, lng_ref, lnb_ref, ws_ref, bias_ref, wout_ref, o_ref, v_ref, *, tm):
    x = x_ref[...]
    h = _rms(x, g_ref[...])
    z = jax.nn.gelu(jnp.dot(h.astype(BF16), win_ref[...], preferred_element_type=F32))
    half = z.shape[1] // 2
    u, v = z[:, :half], z[:, half:]
    mu = jnp.mean(v, axis=-1, keepdims=True)
    var = jnp.mean(jnp.square(v - mu), axis=-1, keepdims=True)
    vn = (v - mu) * lax.rsqrt(var + LN_EPS) * lng_ref[...] + lnb_ref[...]
    v_ref[...] = vn
    gd = half // GM_GROUPS
    gated = []
    for c in range(tm // GM_CHUNK):
        rows = slice(c * GM_CHUNK, (c + 1) * GM_CHUNK)
        vc = vn[rows].astype(BF16)
        mixed = jnp.concatenate(
            [jnp.dot(ws_ref[0, g], vc[:, g * gd:(g + 1) * gd], preferred_element_type=F32)
             for g in range(GM_GROUPS)], axis=1) + bias_ref[0]
        gated.append((u[rows] * mixed).astype(BF16))
    gated = jnp.concatenate(gated, axis=0)
    o_ref[...] = x + jnp.dot(gated, wout_ref[...], preferred_element_type=F32)


def _gmlp(x, g, win_bf, lng, lnb, ws2, bias2, wout_bf, *, tm, n_first, n_v_rows):
    N, D = x.shape
    half = win_bf.shape[1] // 2
    nt = N // tm
    t_first = n_first // tm
    variant = lambda i: jnp.where(i >= t_first, 1, 0)
    row = pl.BlockSpec((tm, D), lambda i: (i, 0))
    const = lambda s: pl.BlockSpec(s, lambda i: (0, 0))
    in_specs = [row, const((1, D)), const((D, 2 * half)), const((1, half)), const((1, half)),
                pl.BlockSpec((1, GM_GROUPS, GM_CHUNK, GM_CHUNK), lambda i: (variant(i), 0, 0, 0)),
                pl.BlockSpec((1, GM_CHUNK, half), lambda i: (variant(i), 0, 0)),
                const((half, D))]
    out_specs = (row, pl.BlockSpec((tm, half), lambda i: (jnp.maximum(i - t_first, 0), 0)))
    return pl.pallas_call(
        functools.partial(_gmlp_kernel, tm=tm),
        out_shape=(jax.ShapeDtypeStruct((N, D), F32), jax.ShapeDtypeStruct((n_v_rows, half), F32)),
        grid=(nt,), in_specs=in_specs, out_specs=out_specs,
        compiler_params=pltpu.CompilerParams(dimension_semantics=("arbitrary",), vmem_limit_bytes=VMEM_LIMIT),
        name="gmlp",
    )(x, g, win_bf, lng, lnb, ws2, bias2, wout_bf)


ROUTE_E0, ROUTE_E1, ROUTE_R0, ROUTE_R1, ROUTE_G0, ROUTE_G1 = range(6)
ROUTE_ROWS = 8
N_LOGITS = N_EXPERT_GROUPS + N_EXPERTS
LOGIT_ROWS = -(-N_LOGITS // SUBLANES) * SUBLANES


def _router_kernel(x_ref, g_ref, wr_ref, route_ref, cnt_ref, carry_ref, *, tm):
    i = pl.program_id(0)

    @pl.when(i == 0)
    def _():
        carry_ref[...] = jnp.zeros(carry_ref.shape, F32)

    h = _rms(x_ref[...], g_ref[...])
    nt = (((1,), (1,)), ((), ()))
    logits = lax.dot_general(wr_ref[...], h.astype(BF16), nt, preferred_element_type=F32)[:LOGIT_ROWS]
    row = lax.broadcasted_iota(I32, (LOGIT_ROWS, tm), 0).astype(F32)
    ninf = -jnp.inf
    big = float(LANES)
    cmax = lambda a: jnp.max(a, axis=0, keepdims=True)
    cmin = lambda a: jnp.min(a, axis=0, keepdims=True)
    csum = lambda a: jnp.sum(a, axis=0, keepdims=True)

    is_grp = row < N_EXPERT_GROUPS
    lg = jnp.where(is_grp, logits, ninf)
    mg = cmax(lg)
    g_sel = cmin(jnp.where(lg == mg, row, big))
    p_grp = 1.0 / csum(jnp.where(is_grp, jnp.exp(lg - mg), 0.0))
    lo = N_EXPERT_GROUPS + g_sel * EXPERTS_PER_GROUP
    le = jnp.where((row >= lo) & (row < lo + EXPERTS_PER_GROUP), logits, ninf)
    v1 = cmax(le)
    j1 = cmin(jnp.where(le == v1, row, big))
    le2 = jnp.where(row == j1, ninf, le)
    v2 = cmax(le2)
    j2 = cmin(jnp.where(le2 == v2, row, big))
    e0 = j1 - N_EXPERT_GROUPS
    e1 = j2 - N_EXPERT_GROUPS
    r = jnp.exp(v2 - v1)
    g0 = p_grp / (1.0 + r)
    g1 = p_grp * r / (1.0 + r)

    erow = lax.broadcasted_iota(I32, (N_EXPERTS, tm), 0).astype(F32)
    oh0 = jnp.where(erow == e0, 1.0, 0.0)
    oh1 = jnp.where(erow == e1, 1.0, 0.0)
    oh = oh0 + oh1
    earlier = jnp.where(lax.broadcasted_iota(I32, (tm, tm), 0) < lax.broadcasted_iota(I32, (tm, tm), 1),
                        1.0, 0.0).astype(BF16)
    before = jnp.dot(oh.astype(BF16), earlier, preferred_element_type=F32) + carry_ref[...]
    r0 = csum(oh0 * before)
    r1 = csum(oh1 * before)
    carry_ref[...] = carry_ref[...] + jnp.sum(oh, axis=1, keepdims=True)
    cnt_ref[...] = jnp.broadcast_to(carry_ref[...], cnt_ref.shape)

    rows = {ROUTE_E0: e0, ROUTE_E1: e1, ROUTE_R0: r0, ROUTE_R1: r1, ROUTE_G0: g0, ROUTE_G1: g1}
    zero = jnp.zeros((1, tm), F32)
    route_ref[...] = jnp.concatenate([rows.get(k, zero) for k in range(ROUTE_ROWS)], axis=0)


def _router(x, g, wr_t, *, tm):
    N, D = x.shape
    const = lambda s: pl.BlockSpec(s, lambda i: (0, 0))
    return pl.pallas_call(
        functools.partial(_router_kernel, tm=tm),
        out_shape=(jax.ShapeDtypeStruct((ROUTE_ROWS, N), F32), jax.ShapeDtypeStruct((N_EXPERTS, LANES), F32)),
        grid=(N // tm,), in_specs=[pl.BlockSpec((tm, D), lambda i: (i, 0)), const((1, D)), const((LANES, D))],
        out_specs=(pl.BlockSpec((ROUTE_ROWS, tm), lambda i: (0, i)), const((N_EXPERTS, LANES))),
        scratch_shapes=[pltpu.VMEM((N_EXPERTS, 1), F32)],
        compiler_params=pltpu.CompilerParams(dimension_semantics=("arbitrary",), vmem_limit_bytes=VMEM_LIMIT),
        name="router",
    )(x, g, wr_t)


def _pack_bf16_pairs(h):
    half = h.shape[1] // 2
    lo = lax.bitcast_convert_type(h[:, :half].astype(BF16).astype(F32), U32)
    hi = lax.bitcast_convert_type(h[:, half:].astype(BF16).astype(F32), U32)
    return (lo >> 16) | (hi & jnp.uint32(0xFFFF0000))


def _unpack_bf16_pairs(w):
    lo = lax.bitcast_convert_type(w << 16, F32).astype(BF16)
    hi = lax.bitcast_convert_type(w & jnp.uint32(0xFFFF0000), F32).astype(BF16)
    return jnp.concatenate([lo, hi], axis=1)


def _dispatch_kernel(d0_ref, d1_ref, pad_ref, x_ref, g_ref, rows_ref, h_scr, zblk, sems, *, tm):
    i = pl.program_id(0)
    n = pl.num_programs(0)
    slot = i % 2
    groups = tm // SUBLANES

    def wait_slot(s):
        for _ in range(2 * groups):
            pltpu.make_async_copy(h_scr.at[s, 0], rows_ref.at[pl.ds(0, SUBLANES)], sems.at[s]).wait()

    @pl.when(i == 0)
    def _():
        zblk[...] = jnp.zeros(zblk.shape, U32)
        n_blocks = rows_ref.shape[0] // MOE_BLOCK
        n_used = pad_ref[2, 0]

        def blk_copy(row0):
            return pltpu.make_async_copy(zblk, rows_ref.at[pl.ds(row0, MOE_BLOCK)], sems.at[2])

        for wait in (False, True):
            for e in range(N_EXPERTS):
                @pl.when(pad_ref[1, e] > 0)
                def _():
                    cp = blk_copy(pl.multiple_of(pad_ref[0, e], MOE_BLOCK))
                    cp.wait() if wait else cp.start()

            def unused(b, carry):
                cp = blk_copy(pl.multiple_of(b * MOE_BLOCK, MOE_BLOCK))
                cp.wait() if wait else cp.start()
                return carry
            lax.fori_loop(n_used, n_blocks, unused, 0)

    @pl.when(i >= 2)
    def _():
        wait_slot(slot)

    h_scr[slot] = _pack_bf16_pairs(_rms(x_ref[...], g_ref[...])).reshape(groups, SUBLANES, h_scr.shape[3])

    def body(k, carry):
        for u in range(SUBLANES):
            r = k * SUBLANES + u
            src = h_scr.at[slot, k, pl.ds(u, 1)]
            pltpu.make_async_copy(src, rows_ref.at[pl.ds(d0_ref[r], 1)], sems.at[slot]).start()
            pltpu.make_async_copy(src, rows_ref.at[pl.ds(d1_ref[r], 1)], sems.at[slot]).start()
        return carry

    lax.fori_loop(0, groups, body, 0)

    @pl.when(i == n - 1)
    def _():
        wait_slot(slot)

    @pl.when(jnp.logical_and(i == n - 1, n >= 2))
    def _():
        wait_slot(1 - slot)


def _dispatch(dest_flat, pad_info, x, g, *, tm, n_rows):
    N, D = x.shape
    nt = N // tm
    return pl.pallas_call(
        functools.partial(_dispatch_kernel, tm=tm),
        out_shape=jax.ShapeDtypeStruct((n_rows, D // 2), U32), grid=(nt,),
        in_specs=[pl.BlockSpec((tm,), lambda i: (i,), memory_space=pltpu.SMEM),
                  pl.BlockSpec((tm,), lambda i: (i + nt,), memory_space=pltpu.SMEM),
                  pl.BlockSpec(memory_space=pltpu.SMEM),
                  pl.BlockSpec((tm, D), lambda i: (i, 0)),
                  pl.BlockSpec((1, D), lambda i: (0, 0))],
        out_specs=pl.BlockSpec(memory_space=pl.ANY),
        scratch_shapes=[pltpu.VMEM((2, tm // SUBLANES, SUBLANES, D // 2), U32), pltpu.VMEM((MOE_BLOCK, D // 2), U32),
                        pltpu.SemaphoreType.DMA((3,))],
        compiler_params=pltpu.CompilerParams(dimension_semantics=("arbitrary",), vmem_limit_bytes=VMEM_LIMIT),
        name="dispatch",
    )(dest_flat, dest_flat, pad_info, x, g)


def _expert_kernel(be_ref, nu_ref, x_ref, wg_ref, wu_ref, wd_ref, y_ref, wg_bf, wu_bf, wd_bf):
    i = pl.program_id(0)

    @pl.when(jnp.logical_or(i == 0, be_ref[i] != be_ref[jnp.maximum(i - 1, 0)]))
    def _():
        wg_bf[...] = wg_ref[0, 0].astype(BF16)
        wu_bf[...] = wu_ref[0, 0].astype(BF16)
        wd_bf[...] = wd_ref[0, 0].astype(BF16)

    @pl.when(i < nu_ref[0])
    def _():
        x = _unpack_bf16_pairs(x_ref[...])
        a = jnp.dot(x, wg_bf[...], preferred_element_type=F32)
        u = jnp.dot(x, wu_bf[...], preferred_element_type=F32)
        act = (a * jax.nn.sigmoid(a)) * u
        y_ref[...] = jnp.dot(act.astype(BF16), wd_bf[...], preferred_element_type=F32)

    @pl.when(i >= nu_ref[0])
    def _():
        y_ref[...] = jnp.zeros(y_ref.shape, F32)


def _experts(block_e, n_used, x_rows, w_gate, w_up, w_down, layer):
    n_rows = x_rows.shape[0]
    D, de = w_gate.shape[2:]
    n_blocks = n_rows // MOE_BLOCK
    grid_spec = pltpu.PrefetchScalarGridSpec(
        num_scalar_prefetch=2, grid=(n_blocks,),
        in_specs=[pl.BlockSpec((MOE_BLOCK, D // 2), lambda i, be, nu: (jnp.minimum(i, nu[0] - 1), 0)),
                  pl.BlockSpec((1, 1, D, de), lambda i, be, nu: (layer, be[i], 0, 0)),
                  pl.BlockSpec((1, 1, D, de), lambda i, be, nu: (layer, be[i], 0, 0)),
                  pl.BlockSpec((1, 1, de, D), lambda i, be, nu: (layer, be[i], 0, 0))],
        out_specs=pl.BlockSpec((MOE_BLOCK, D), lambda i, be, nu: (i, 0)),
        scratch_shapes=[pltpu.VMEM((D, de), BF16), pltpu.VMEM((D, de), BF16), pltpu.VMEM((de, D), BF16)])
    return pl.pallas_call(
        _expert_kernel, out_shape=jax.ShapeDtypeStruct((n_rows, D), F32), grid_spec=grid_spec,
        compiler_params=pltpu.CompilerParams(dimension_semantics=("arbitrary",), vmem_limit_bytes=VMEM_LIMIT),
        name="experts",
    )(block_e, n_used, x_rows, w_gate, w_up, w_down)


def _combine_kernel(dc0_ref, dc1_ref, dn0_ref, dn1_ref, x_ref, route_ref, g_ref, y_ref, *rest,
                    tm, final_norm, t_first):
    *o_refs, ybuf, sems = rest
    i = pl.program_id(0)
    n = pl.num_programs(0)
    slot = i % 2
    groups = tm // SUBLANES
    D = x_ref.shape[1]

    def gather(d_refs, s):
        def body(k, carry):
            for u in range(SUBLANES):
                r = k * SUBLANES + u
                for a in range(2):
                    pltpu.make_async_copy(y_ref.at[pl.ds(d_refs[a][r], 1)], ybuf.at[s, a, k, pl.ds(u, 1)],
                                          sems.at[s]).start()
            return carry
        lax.fori_loop(0, groups, body, 0)

    @pl.when(i == 0)
    def _():
        gather((dc0_ref, dc1_ref), 0)

    @pl.when(i + 1 < n)
    def _():
        gather((dn0_ref, dn1_ref), 1 - slot)

    for _ in range(2 * groups):
        pltpu.make_async_copy(y_ref.at[pl.ds(0, SUBLANES)], ybuf.at[slot, 0, 0], sems.at[slot]).wait()
    route = jnp.transpose(route_ref[...])
    g0 = route[:, ROUTE_G0:ROUTE_G0 + 1]
    g1 = route[:, ROUTE_G1:ROUTE_G1 + 1]
    out = x_ref[...] + (ybuf[slot, 0].reshape(tm, D) * g0 + ybuf[slot, 1].reshape(tm, D) * g1)
    if final_norm:
        out = _rms(out, g_ref[...])
    if t_first is None:
        o_refs[0][...] = out
    else:
        @pl.when(i < t_first)
        def _():
            o_refs[0][...] = out

        @pl.when(i >= t_first)
        def _():
            o_refs[1][...] = out


def _combine(dest_flat, x, route, g, y_rows, *, tm, final_norm, n_first=None):
    N, D = x.shape
    nt = N // tm
    row = lambda wdt: pl.BlockSpec((tm, wdt), lambda i: (i, 0))
    nxt = lambda i: jnp.minimum(i + 1, nt - 1)
    smem = lambda f: pl.BlockSpec((tm,), f, memory_space=pltpu.SMEM)
    if n_first is None:
        t_first, out_shape, out_specs = None, jax.ShapeDtypeStruct((N, D), F32), row(D)
    else:
        t_first = n_first // tm
        out_shape = (jax.ShapeDtypeStruct((n_first, D), F32), jax.ShapeDtypeStruct((N - n_first, D), F32))
        out_specs = (pl.BlockSpec((tm, D), lambda i: (jnp.minimum(i, t_first - 1), 0)),
                     pl.BlockSpec((tm, D), lambda i: (jnp.maximum(i - t_first, 0), 0)))
    return pl.pallas_call(
        functools.partial(_combine_kernel, tm=tm, final_norm=final_norm, t_first=t_first),
        out_shape=out_shape, grid=(nt,),
        in_specs=[smem(lambda i: (i,)), smem(lambda i: (i + nt,)),
                  smem(lambda i: (nxt(i),)), smem(lambda i: (nxt(i) + nt,)),
                  row(D), pl.BlockSpec((ROUTE_ROWS, tm), lambda i: (0, i)), pl.BlockSpec((1, D), lambda i: (0, 0)),
                  pl.BlockSpec(memory_space=pl.ANY)],
        out_specs=out_specs,
        scratch_shapes=[pltpu.VMEM((2, 2, tm // SUBLANES, SUBLANES, D), F32), pltpu.SemaphoreType.DMA((2,))],
        compiler_params=pltpu.CompilerParams(dimension_semantics=("arbitrary",), vmem_limit_bytes=VMEM_LIMIT),
        name="combine",
    )(dest_flat, dest_flat, dest_flat, dest_flat, x, route, g, y_rows)


def _hier_moe(x, g_ffn, w_rg, w_re, w_gate, w_up, w_down, layer, g_final, *, final_norm, n_first=None):
    N, D = x.shape
    tm = TOKEN_TILE
    wr_t = jnp.concatenate([w_rg.T, w_re.T, jnp.zeros((LANES - N_LOGITS, D), F32)], axis=0).astype(BF16)
    g_ffn = g_ffn.reshape(1, D)
    route, counts = _router(x, g_ffn, wr_t, tm=tm)
    counts = counts[:, 0].astype(I32)
    eid_t = route[ROUTE_E0:ROUTE_E1 + 1].astype(I32)
    rank_t = route[ROUTE_R0:ROUTE_R1 + 1].astype(I32)
    padded = (counts + MOE_BLOCK - 1) // MOE_BLOCK * MOE_BLOCK
    pad_end = jnp.cumsum(padded)
    pad_start = pad_end - padded
    expert_ids = jnp.arange(N_EXPERTS, dtype=I32)
    start_of = jnp.sum(jnp.where(eid_t[:, :, None] == expert_ids, pad_start, 0), axis=-1)
    dest_t = start_of + rank_t
    n_blocks = -(-(2 * N) // MOE_BLOCK) + N_EXPERTS
    n_rows = n_blocks * MOE_BLOCK
    n_used = (pad_end[-1] // MOE_BLOCK).astype(I32)
    block_start = jnp.minimum(jnp.arange(n_blocks, dtype=I32), n_used - 1) * MOE_BLOCK
    block_e = jnp.minimum(jnp.sum(block_start[:, None] >= pad_end[None, :], axis=1), N_EXPERTS - 1).astype(I32)
    pad_info = jnp.stack([pad_end - MOE_BLOCK, padded - counts, jnp.broadcast_to(n_used, (N_EXPERTS,))])
    dest_flat = dest_t.reshape(-1)
    x_rows = _dispatch(dest_flat, pad_info, x, g_ffn, tm=tm, n_rows=n_rows)
    y_rows = _experts(block_e, n_used.reshape(1), x_rows, w_gate, w_up, w_down, layer)
    return _combine(dest_flat, x, route, g_final.reshape(1, D), y_rows, tm=tm, final_norm=final_norm,
                    n_first=n_first)


def _rope_tables(pos, rot_lanes):
    inv = 1.0 / (ROPE_THETA ** (jnp.arange(ROT_HALF, dtype=F32) / ROT_HALF))
    ang = pos.astype(F32)[:, None] * inv[None, :]
    cos, sin = jnp.cos(ang), jnp.sin(ang)
    lane = jnp.arange(LANES)
    r = lane % HEAD_DIM
    active = (lane < rot_lanes)
    first = active & (r < ROT_HALF)
    second = active & (r >= ROT_HALF) & (r < 2 * ROT_HALF)
    cos_l = cos[:, r % ROT_HALF]
    sin_l = sin[:, r % ROT_HALF]
    c = jnp.where((first | second)[None, :], cos_l, 1.0)
    sa = jnp.where(first[None, :], -sin_l, 0.0)
    sb = jnp.where(second[None, :], sin_l, 0.0)
    return jnp.stack([c, sa, sb]).astype(F32)


def _prep_w_in(w_in):
    D = w_in.shape[0]
    assert w_in.shape[1] == sum(IN_SIZES)
    pad = jnp.zeros((D, IN_WIDTH_PAD - w_in.shape[1]), w_in.dtype)
    return jnp.concatenate([w_in, pad], axis=1).astype(BF16)


def kernel(x_prompt, x_sample, cache_k, cache_v, cache_idx_k, state_pool, norm_mix, norm_ffn, norm_final,
           par_w_in, par_pool_w, par_pool_scale, par_w_out, gm_w_in, gm_ln_g, gm_ln_b, gm_ws, gm_bs, gm_w_out,
           moe_router_group, moe_router_expert, moe_w_gate, moe_w_up, moe_w_down):
    Bp, Tp, D = x_prompt.shape
    Bs, Ts, _ = x_sample.shape
    past = cache_k.shape[2]
    Np, Ns = Bp * Tp, Bs * Ts
    depth = norm_mix.shape[0]
    assert depth == 2 and Ts == CHUNK and Tp % PROJ_TILE == 0 and Tp % QUERY_TILE == 0 and Tp % TOKEN_TILE == 0 and Ns % TOKEN_TILE == 0

    w_in_bf = _prep_w_in(par_w_in[0])
    pw_bf = par_pool_w[0].astype(BF16)
    ps = par_pool_scale[0].reshape(1, POOL_WIDTH)
    wo_bf = par_w_out[0].astype(BF16)
    g_mix0 = norm_mix[0].reshape(1, D)
    pos_p = jnp.arange(Tp, dtype=I32)
    pos_s = past + jnp.arange(Ts, dtype=I32)

    hist_p = jnp.zeros((Bp, HIST_ROWS, POOL_WIDTH), F32)
    hist_s = jnp.pad(state_pool[0], ((0, 0), (1, 0), (0, 0)))
    tk = KEY_BLOCK
    (q_p, qi_p, k_p, v_p, kiwi_p, kbf_p, vbf_p, kibf_p, yp_p, st_p) = _inproj(
        x_prompt, g_mix0, w_in_bf, _rope_tables(pos_p, LANES), _rope_tables(pos_p, IDX_DIM), hist_p, pw_bf, ps,
        tm=PROJ_TILE, pos0=0)
    (q_s, qi_s, k_s, v_s, kiwi_s, kbf_s, vbf_s, kibf_s, yp_s, st_s) = _inproj(
        x_sample, g_mix0, w_in_bf, _rope_tables(pos_s, LANES), _rope_tables(pos_s, IDX_DIM), hist_s, pw_bf, ps,
        tm=Ts, pos0=past)

    ls = past + Ts
    lsp = -(-ls // tk) * tk
    kpad = lambda a, ax: jnp.pad(a, [(0, lsp - ls) if d == ax else (0, 0) for d in range(a.ndim)])
    kall = kpad(jnp.concatenate([jnp.transpose(cache_k[0], (0, 2, 1, 3)).astype(BF16), kbf_s], axis=2), 2)
    vall = kpad(jnp.concatenate([cache_v[0].reshape(Bs, past, KV_WIDTH).astype(BF16), vbf_s], axis=1), 1)
    vall_t = jnp.transpose(vall.reshape(Bs, lsp // LANES, LANES, KV_WIDTH), (0, 1, 3, 2))
    kiall = kpad(jnp.concatenate([cache_idx_k[0].astype(BF16), kibf_s], axis=1), 1)
    qpad = lambda a, ax: jnp.pad(a, [(0, LANES - Ts) if d == ax else (0, 0) for d in range(a.ndim)])
    x1_s = _dsa(qpad(q_s, 2), qpad(qi_s, 2), qpad(kiwi_s, 1), kall, vall_t, kiall, qpad(x_sample, 1), qpad(yp_s, 1),
                wo_bf, tq=LANES, tk=tk, pos0=past, n_keys=ls, n_sel=min(TOPK_MAX, ls // 4))[:, :Ts]
    assert Tp % tk == 0
    x = jnp.pad(x1_s.reshape(Ns, D), ((Np, 0), (0, 0)))
    x = _dsa(q_p, qi_p, kiwi_p, kbf_p, vbf_p, kibf_p, x_prompt, yp_p, wo_bf,
             tq=QUERY_TILE, tk=tk, pos0=0, n_keys=Tp, n_sel=min(TOPK_MAX, Tp // 4), out_base=x)
    x = _hier_moe(x, norm_ffn[0], moe_router_group[0], moe_router_expert[0],
                  moe_w_gate, moe_w_up, moe_w_down, 0, norm_final, final_norm=False)

    cs = Ts
    tril = lambda n: jnp.tril(jnp.ones((n, n), bool))
    ws_p = jnp.where(tril(GM_CHUNK)[None], gm_ws[0], 0.0)
    ws_small = jnp.where(tril(cs)[None], gm_ws[0][:, :cs, :cs], 0.0)
    rep = GM_CHUNK // cs
    ws_s = jnp.einsum('ab,gts->gatbs', jnp.eye(rep, dtype=F32), ws_small).reshape(GM_GROUPS, GM_CHUNK, GM_CHUNK)
    ws2 = jnp.stack([ws_p, ws_s]).astype(BF16)
    gd = D // GM_GROUPS
    bias_p = jnp.repeat(jnp.transpose(gm_bs[0]), gd, axis=1)
    bias_s = jnp.tile(jnp.repeat(jnp.transpose(gm_bs[0][:, :cs]), gd, axis=1), (rep, 1))
    bias2 = jnp.stack([bias_p, bias_s])
    x, gm_v = _gmlp(x, norm_mix[1].reshape(1, D), gm_w_in[0].astype(BF16), gm_ln_g[0].reshape(1, D),
                    gm_ln_b[0].reshape(1, D), ws2, bias2, gm_w_out[0].astype(BF16),
                    tm=TOKEN_TILE, n_first=Np, n_v_rows=Ns)
    y_p, y_s = _hier_moe(x, norm_ffn[1], moe_router_group[1], moe_router_expert[1],
                         moe_w_gate, moe_w_up, moe_w_down, 1, norm_final, final_norm=True, n_first=Np)
    y_p = y_p.reshape(Bp, Tp, D)
    y_s = y_s.reshape(Bs, Ts, D)
    r4 = lambda a, b, t: a.reshape(1, b, t, N_KV_HEADS, HEAD_DIM)
    return (y_p, y_s,
            r4(k_p, Bp, Tp), r4(v_p, Bp, Tp), kiwi_p[:, :, :IDX_DIM][None], st_p[:, 1:][None],
            r4(k_s, Bs, Ts), r4(v_s, Bs, Ts), kiwi_s[:, :, :IDX_DIM][None], st_s[:, 1:][None],
            gm_v.reshape(1, Bs, Ts, D))
```

```python
import functools

import jax
import jax.numpy as jnp
from jax import lax
from jax.experimental import pallas as pl
from jax.experimental.pallas import tpu as pltpu

F32 = jnp.float32
BF16 = jnp.bfloat16
I32 = jnp.int32
U32 = jnp.uint32

LANES = 128
SUBLANES = 8
CHUNK = 64
POOL_WINDOWS = (2, 4, 8, 16)
POOL_GROUP_DIM = 128
POOL_WIDTH = 512
HIST_ROWS = 16
N_HEADS = 8
HEAD_DIM = 64
N_KV_HEADS = 4
Q_PER_KV = N_HEADS // N_KV_HEADS
ATT_WIDTH = N_HEADS * HEAD_DIM
KV_WIDTH = N_KV_HEADS * HEAD_DIM
N_IDX_HEADS = 8
IDX_DIM = 64
TOPK_MAX = 256
ROPE_THETA = 500000.0
ROT_HALF = HEAD_DIM // 8
GM_CHUNK = 128
GM_GROUPS = 8
N_EXPERT_GROUPS = 4
EXPERTS_PER_GROUP = 8
N_EXPERTS = 32
MOE_BLOCK = 512
RMS_EPS = 1e-6
LN_EPS = 1e-5

INT_MIN = -2147483648
LOG2_E = 1.4426950408889634
NEG_BIG = -1e30
VMEM_LIMIT = 48 * 1024 * 1024

TOKEN_TILE = 1024
PROJ_TILE = 512
KEY_BLOCK = 512
QUERY_TILE = 256
COUNT_SLAB = 64

IN_SIZES = (POOL_WIDTH, ATT_WIDTH, KV_WIDTH, KV_WIDTH, N_IDX_HEADS * IDX_DIM, IDX_DIM, N_IDX_HEADS)
COL_XP, COL_Q, COL_K, COL_V, COL_QI, COL_KIWI = (sum(IN_SIZES[:i]) for i in range(6))
IN_WIDTH_PAD = COL_KIWI + LANES


def _rms(x, g):
    return x * lax.rsqrt(jnp.mean(x * x, axis=-1, keepdims=True) + RMS_EPS) * g


def _rope128(x, c, sa, sb):
    return x * c + pltpu.roll(x, LANES - ROT_HALF, 1) * sa + pltpu.roll(x, ROT_HALF, 1) * sb


def _inproj_kernel(x_ref, g_ref, w_ref, rope_ref, ropeki_ref, hist_ref, pw_ref, ps_ref,
                   q_ref, qi_ref, k_ref, v_ref, kiwi_ref, kbf_ref, vbf_ref, kibf_ref, yp_ref, state_ref,
                   buf_ref, *, tm, pos0):
    j = pl.program_id(1)
    h = _rms(x_ref[0], g_ref[...])
    proj = jnp.dot(h.astype(BF16), w_ref[...], preferred_element_type=F32)
    c, sa, sb = rope_ref[0], rope_ref[1], rope_ref[2]

    def put_heads(ref, i, chunk):
        ref[0, 2 * i] = chunk[:, :HEAD_DIM].astype(BF16)
        ref[0, 2 * i + 1] = chunk[:, HEAD_DIM:].astype(BF16)

    for i in range(ATT_WIDTH // LANES):
        put_heads(q_ref, i, _rope128(proj[:, COL_Q + i * LANES:COL_Q + (i + 1) * LANES], c, sa, sb)
                  * (HEAD_DIM ** -0.5 * LOG2_E))
        put_heads(qi_ref, i, _rope128(proj[:, COL_QI + i * LANES:COL_QI + (i + 1) * LANES], c, sa, sb))
    for i in range(KV_WIDTH // LANES):
        sl = slice(i * LANES, (i + 1) * LANES)
        kr = _rope128(proj[:, COL_K + i * LANES:COL_K + (i + 1) * LANES], c, sa, sb)
        k_ref[0, :, sl] = kr
        put_heads(kbf_ref, i, kr)
    vv = proj[:, COL_V:COL_V + KV_WIDTH]
    v_ref[0] = vv
    if tm % LANES == 0:
        for cc in range(tm // LANES):
            vbf_ref[0, cc] = jnp.transpose(vv[cc * LANES:(cc + 1) * LANES, :]).astype(BF16)
    else:
        vbf_ref[0] = vv.astype(BF16)
    kiwi = _rope128(proj[:, COL_KIWI:COL_KIWI + LANES], ropeki_ref[0], ropeki_ref[1], ropeki_ref[2])
    kiwi_ref[0] = kiwi
    kibf_ref[0] = kiwi[:, :IDX_DIM].astype(BF16)

    @pl.when(j == 0)
    def _():
        buf_ref[0:HIST_ROWS, :] = hist_ref[0]

    xp = proj[:, COL_XP:COL_XP + POOL_WIDTH]
    buf_ref[HIST_ROWS:HIST_ROWS + tm, :] = xp
    pos = pos0 + j * tm + lax.broadcasted_iota(I32, (tm, 1), 0)
    for gi, w in enumerate(POOL_WINDOWS):
        c0 = gi * POOL_GROUP_DIM
        s = xp[:, c0:c0 + POOL_GROUP_DIM]
        for i in range(1, w):
            s = s + buf_ref[HIST_ROWS - i:HIST_ROWS - i + tm, c0:c0 + POOL_GROUP_DIM]
        cnt = jnp.minimum(pos + 1, w).astype(F32)
        d = s / cnt - xp[:, c0:c0 + POOL_GROUP_DIM]
        y = jnp.dot(d.astype(BF16), pw_ref[gi], preferred_element_type=F32)
        yp_ref[0, :, c0:c0 + POOL_GROUP_DIM] = (y * ps_ref[:, c0:c0 + POOL_GROUP_DIM]).astype(BF16)
    tail = buf_ref[tm:tm + HIST_ROWS, :]
    state_ref[0] = tail
    buf_ref[0:HIST_ROWS, :] = tail


def _inproj(x, g, w_bf, rope, ropeki, hist, pw_bf, ps, *, tm, pos0):
    B, T, D = x.shape
    nt = T // tm
    f = lambda shape, dt: jax.ShapeDtypeStruct(shape, dt)
    v_t = tm % LANES == 0
    out_shape = (
        f((B, N_HEADS, T, HEAD_DIM), BF16), f((B, N_IDX_HEADS, T, IDX_DIM), BF16),
        f((B, T, KV_WIDTH), F32), f((B, T, KV_WIDTH), F32), f((B, T, LANES), F32),
        f((B, N_KV_HEADS, T, HEAD_DIM), BF16),
        f((B, T // LANES, KV_WIDTH, LANES) if v_t else (B, T, KV_WIDTH), BF16),
        f((B, T, IDX_DIM), BF16),
        f((B, T, POOL_WIDTH), BF16), f((B, HIST_ROWS, POOL_WIDTH), F32),
    )
    tile = lambda wdt: pl.BlockSpec((1, tm, wdt), lambda b, j: (b, j, 0))
    heads = lambda n, wdt: pl.BlockSpec((1, n, tm, wdt), lambda b, j: (b, 0, j, 0))
    const2 = lambda s: pl.BlockSpec(s, lambda b, j: (0, 0))
    in_specs = [
        tile(D), const2((1, D)), const2((D, IN_WIDTH_PAD)),
        pl.BlockSpec((3, tm, LANES), lambda b, j: (0, j, 0)),
        pl.BlockSpec((3, tm, LANES), lambda b, j: (0, j, 0)),
        pl.BlockSpec((1, HIST_ROWS, POOL_WIDTH), lambda b, j: (b, 0, 0)),
        pl.BlockSpec((len(POOL_WINDOWS), POOL_GROUP_DIM, POOL_GROUP_DIM), lambda b, j: (0, 0, 0)),
        const2((1, POOL_WIDTH)),
    ]
    out_specs = (
        heads(N_HEADS, HEAD_DIM), heads(N_IDX_HEADS, IDX_DIM), tile(KV_WIDTH), tile(KV_WIDTH), tile(LANES),
        heads(N_KV_HEADS, HEAD_DIM),
        pl.BlockSpec((1, tm // LANES, KV_WIDTH, LANES), lambda b, j: (b, j, 0, 0)) if v_t else tile(KV_WIDTH),
        tile(IDX_DIM), tile(POOL_WIDTH),
        pl.BlockSpec((1, HIST_ROWS, POOL_WIDTH), lambda b, j: (b, 0, 0)),
    )
    return pl.pallas_call(
        functools.partial(_inproj_kernel, tm=tm, pos0=pos0),
        out_shape=out_shape, grid=(B, nt), in_specs=in_specs, out_specs=out_specs,
        scratch_shapes=[pltpu.VMEM((HIST_ROWS + tm, POOL_WIDTH), F32)],
        compiler_params=pltpu.CompilerParams(dimension_semantics=("parallel", "arbitrary"),
                                             vmem_limit_bytes=VMEM_LIMIT),
        name="inproj",
    )(x, g, w_bf, rope, ropeki, hist, pw_bf, ps)


def _dsa_kernel(q_ref, qi_ref, kiwi_ref, k_ref, vt_ref, ki_ref, x_ref, yp_ref, wo_ref, *rest,
                tq, tk, pos0, n_keys, n_sel):
    o_ref, key_buf, bias_buf, m_scr, l_scr, acc_scr, s_scr = rest[-7:]
    slab = COUNT_SLAB
    j = pl.program_id(1)
    base = pos0 + j * tq
    pos = base + lax.broadcasted_iota(I32, (1, tq), 1)
    limit = jnp.minimum((pos // CHUNK + 1) * CHUNK, n_keys)
    limit_max = jnp.minimum(((base + tq - 1) // CHUNK + 1) * CHUNK, n_keys)
    nkb = (limit_max + tk - 1) // tk
    nt = (((1,), (1,)), ((), ()))

    wi_t = jnp.transpose(kiwi_ref[0])[IDX_DIM:IDX_DIM + N_IDX_HEADS, :]

    def score_block(kb, carry):
        off = pl.multiple_of(kb * tk, tk)
        kiblk = ki_ref[0, pl.ds(off, tk), :]
        idx = jnp.zeros((tk, tq), F32)
        for h in range(N_IDX_HEADS):
            sc = lax.dot_general(kiblk, qi_ref[0, h], nt, preferred_element_type=F32)
            idx = idx + jnp.maximum(sc, 0.0) * wi_t[h:h + 1, :]
        idx = jnp.where(idx == 0.0, 0.0, idx)
        bits = lax.bitcast_convert_type(idx, I32)
        key = bits ^ ((bits >> 31) & 0x7FFFFFFF)
        key_buf[kb] = jnp.where(lax.broadcasted_iota(I32, (tk, tq), 0) < limit - kb * tk, key, INT_MIN)
        return carry

    lax.fori_loop(0, nkb, score_block, 0)

    def col_sum(a):
        return jnp.sum(a, axis=0, keepdims=True)

    def count_ge(cand):
        def body(kb, acc):
            kblk = key_buf[kb]
            for c in range(tk // slab):
                acc = acc + jnp.where(kblk[c * slab:(c + 1) * slab] >= cand, 1.0, 0.0)
            return acc
        return col_sum(lax.fori_loop(0, nkb, body, jnp.zeros((slab, tq), F32)))

    kf = float(n_sel)
    cnt0 = count_ge(jnp.zeros((1, tq), I32))
    t0 = jnp.where(cnt0 >= kf, 0, INT_MIN).astype(I32)

    def bit_body(i, carry):
        t, cnt = carry
        cand = t | lax.shift_left(jnp.int32(1), 30 - i)
        cnt_cand = count_ge(cand)
        keep = cnt_cand >= kf
        return jnp.where(keep, cand, t), jnp.where(keep, cnt_cand, cnt)

    t, cnt_ge = lax.fori_loop(0, 31, bit_body, (t0, cnt0))
    cnt_gt = count_ge(t + 1)
    need = kf - cnt_gt
    cnt_eq = cnt_ge - cnt_gt
    overfull = jnp.where(t != INT_MIN, cnt_eq - need, 0.0)
    slow = jnp.max(overfull) > 0.0

    @pl.when(jnp.logical_not(slow))
    def _():
        t_adm = jnp.maximum(t, INT_MIN + 1)

        def body(kb, carry):
            bias_buf[kb] = jnp.where(key_buf[kb] >= t_adm, 0.0, NEG_BIG)
            return carry
        lax.fori_loop(0, nkb, body, 0)

    @pl.when(slow)
    def _():
        tri = jnp.where(lax.broadcasted_iota(I32, (tk, tk), 1) <= lax.broadcasted_iota(I32, (tk, tk), 0),
                        1.0, 0.0).astype(BF16)

        def body(kb, seen):
            kblk = key_buf[kb]
            eq = jnp.where((kblk == t) & (kblk != INT_MIN), 1.0, 0.0)
            prefix = jnp.dot(tri, eq.astype(BF16), preferred_element_type=F32) + seen
            keep_tie = jnp.where(prefix <= need, eq, 0.0)
            sel = jnp.where(kblk > t, 1.0, keep_tie)
            bias_buf[kb] = jnp.where(sel > 0.0, 0.0, NEG_BIG)
            return seen + col_sum(eq)
        lax.fori_loop(0, nkb, body, jnp.zeros((1, tq), F32))

    sub = LANES
    nsub = tk // sub
    tl = LANES
    n_tiles = tq // tl
    m_scr[...] = jnp.full(m_scr.shape, NEG_BIG, F32)
    l_scr[...] = jnp.zeros(l_scr.shape, F32)
    acc_scr[...] = jnp.zeros(acc_scr.shape, F32)

    def tile_blocks(hq):
        return (jnp.minimum(((base + (hq + 1) * tl - 1) // CHUNK + 1) * CHUNK, n_keys) + tk - 1) // tk

    def attn_block(kb, members):
        for mi, hq in enumerate(members):
            for c in range(nsub):
                off = pl.multiple_of(kb * tk + c * sub, sub)
                for g in range(N_KV_HEADS):
                    s_scr[mi, c, g] = lax.dot_general(
                        k_ref[0, g, pl.ds(off, sub), :],
                        q_ref[0, Q_PER_KV * g:Q_PER_KV * (g + 1), hq * tl:(hq + 1) * tl].reshape(Q_PER_KV * tl, HEAD_DIM),
                        nt, preferred_element_type=F32)
        for mi, hq in enumerate(members):
            m = [m_scr[hq, g] for g in range(N_KV_HEADS)]
            l = [l_scr[hq, g] for g in range(N_KV_HEADS)]
            for c in range(nsub):
                bias = bias_buf[kb, c * sub:(c + 1) * sub, hq * tl:(hq + 1) * tl]
                bias2 = jnp.concatenate([bias] * Q_PER_KV, axis=1)
                for g in range(N_KV_HEADS):
                    s = s_scr[mi, c, g] + bias2
                    m_new = jnp.maximum(m[g], jnp.max(s, axis=0, keepdims=True))
                    alpha = jnp.exp2(m[g] - m_new)
                    p = jnp.exp2(s - m_new)
                    l[g] = alpha * l[g] + col_sum(p)
                    pv = jnp.dot(vt_ref[0, kb * nsub + c, g * HEAD_DIM:(g + 1) * HEAD_DIM, :], p.astype(BF16),
                                 preferred_element_type=F32)
                    acc_scr[hq, g] = alpha * acc_scr[hq, g] + pv
                    m[g] = m_new
            for g in range(N_KV_HEADS):
                m_scr[hq, g] = m[g]
                l_scr[hq, g] = l[g]

    for first in range(0, n_tiles, 2):
        members = tuple(range(first, min(first + 2, n_tiles)))
        shared = tile_blocks(members[0])
        lax.fori_loop(0, shared, lambda kb, c, ms=members: (attn_block(kb, ms), c)[1], 0)
        if len(members) == 2:
            lax.fori_loop(shared, tile_blocks(members[1]),
                          lambda kb, c, ms=members[1:]: (attn_block(kb, ms), c)[1], 0)

    y_att = []
    for hq in range(n_tiles):
        o_t = []
        for g in range(N_KV_HEADS):
            og = acc_scr[hq, g] / l_scr[hq, g]
            o_t.extend(og[:, hh * tl:(hh + 1) * tl] for hh in range(Q_PER_KV))
        y_att.append(jnp.transpose(jnp.concatenate(o_t, axis=0)).astype(BF16))
    y_att = jnp.concatenate(y_att, axis=0)

    y = jnp.dot(yp_ref[0], wo_ref[0:POOL_WIDTH, :], preferred_element_type=F32)
    y = y + jnp.dot(y_att, wo_ref[POOL_WIDTH:POOL_WIDTH + ATT_WIDTH, :], preferred_element_type=F32)
    o_ref[...] = (x_ref[0] + y).reshape(o_ref.shape)


def _dsa(q_hm, qi_hm, kiwi, k_hm, v_t, ki_all, x, yp, wo_bf, *, tq, tk, pos0, n_keys, n_sel, out_base=None):
    B, T, D = x.shape
    L = k_hm.shape[2]
    assert L % tk == 0 and T % tq == 0 and tq % LANES == 0
    nkb_max = L // tk
    tile = lambda wdt: pl.BlockSpec((1, tq, wdt), lambda b, j: (b, j, 0))
    in_specs = [pl.BlockSpec((1, N_HEADS, tq, HEAD_DIM), lambda b, j: (b, 0, j, 0)),
                pl.BlockSpec((1, N_IDX_HEADS, tq, IDX_DIM), lambda b, j: (b, 0, j, 0)),
                tile(LANES),
                pl.BlockSpec((1, N_KV_HEADS, L, HEAD_DIM), lambda b, j: (b, 0, 0, 0)),
                pl.BlockSpec((1, L // LANES, KV_WIDTH, LANES), lambda b, j: (b, 0, 0, 0)),
                pl.BlockSpec((1, L, IDX_DIM), lambda b, j: (b, 0, 0)),
                tile(D), tile(POOL_WIDTH), pl.BlockSpec((POOL_WIDTH + ATT_WIDTH, D), lambda b, j: (0, 0))]
    nq = T // tq
    args = (q_hm, qi_hm, kiwi, k_hm, v_t, ki_all, x, yp, wo_bf)
    if out_base is None:
        out_shape, out_spec, aliases = jax.ShapeDtypeStruct((B, T, D), F32), tile(D), {}
    else:
        out_shape = jax.ShapeDtypeStruct(out_base.shape, F32)
        out_spec = pl.BlockSpec((tq, D), lambda b, j: (b * nq + j, 0))
        in_specs, args, aliases = in_specs + [pl.BlockSpec(memory_space=pl.ANY)], args + (out_base,), {len(args): 0}
    return pl.pallas_call(
        functools.partial(_dsa_kernel, tq=tq, tk=tk, pos0=pos0, n_keys=n_keys, n_sel=n_sel),
        out_shape=out_shape, grid=(B, nq), in_specs=in_specs, out_specs=out_spec, input_output_aliases=aliases,
        scratch_shapes=[pltpu.VMEM((nkb_max, tk, tq), I32), pltpu.VMEM((nkb_max, tk, tq), F32),
                        pltpu.VMEM((tq // LANES, N_KV_HEADS, 1, Q_PER_KV * LANES), F32),
                        pltpu.VMEM((tq // LANES, N_KV_HEADS, 1, Q_PER_KV * LANES), F32),
                        pltpu.VMEM((tq // LANES, N_KV_HEADS, HEAD_DIM, Q_PER_KV * LANES), F32),
                        pltpu.VMEM((2, tk // LANES, N_KV_HEADS, LANES, Q_PER_KV * LANES), F32)],
        compiler_params=pltpu.CompilerParams(dimension_semantics=("parallel", "arbitrary"),
                                             vmem_limit_bytes=VMEM_LIMIT),
        name="dsa",
    )(*args)


def _gmlp_kernel(x_ref, g_ref, win_ref, lng_ref, lnb_ref, ws_ref, bias_ref, wout_ref, o_ref, v_ref, *, tm):
    x = x_ref[...]
    h = _rms(x, g_ref[...])
    z = jax.nn.gelu(jnp.dot(h.astype(BF16), win_ref[...], preferred_element_type=F32))
    half = z.shape[1] // 2
    u, v = z[:, :half], z[:, half:]
    mu = jnp.mean(v, axis=-1, keepdims=True)
    var = jnp.mean(jnp.square(v - mu), axis=-1, keepdims=True)
    vn = (v - mu) * lax.rsqrt(var + LN_EPS) * lng_ref[...] + lnb_ref[...]
    v_ref[...] = vn
    gd = half // GM_GROUPS
    gated = []
    for c in range(tm // GM_CHUNK):
        rows = slice(c * GM_CHUNK, (c + 1) * GM_CHUNK)
        vc = vn[rows].astype(BF16)
        mixed = jnp.concatenate(
            [jnp.dot(ws_ref[0, g], vc[:, g * gd:(g + 1) * gd], preferred_element_type=F32)
             for g in range(GM_GROUPS)], axis=1) + bias_ref[0]
        gated.append((u[rows] * mixed).astype(BF16))
    gated = jnp.concatenate(gated, axis=0)
    o_ref[...] = x + jnp.dot(gated, wout_ref[...], preferred_element_type=F32)


def _gmlp(x, g, win_bf, lng, lnb, ws2, bias2, wout_bf, *, tm, n_first, n_v_rows):
    N, D = x.shape
    half = win_bf.shape[1] // 2
    nt = N // tm
    t_first = n_first // tm
    variant = lambda i: jnp.where(i >= t_first, 1, 0)
    row = pl.BlockSpec((tm, D), lambda i: (i, 0))
    const = lambda s: pl.BlockSpec(s, lambda i: (0, 0))
    in_specs = [row, const((1, D)), const((D, 2 * half)), const((1, half)), const((1, half)),
                pl.BlockSpec((1, GM_GROUPS, GM_CHUNK, GM_CHUNK), lambda i: (variant(i), 0, 0, 0)),
                pl.BlockSpec((1, GM_CHUNK, half), lambda i: (variant(i), 0, 0)),
                const((half, D))]
    out_specs = (row, pl.BlockSpec((tm, half), lambda i: (jnp.maximum(i - t_first, 0), 0)))
    return pl.pallas_call(
        functools.partial(_gmlp_kernel, tm=tm),
        out_shape=(jax.ShapeDtypeStruct((N, D), F32), jax.ShapeDtypeStruct((n_v_rows, half), F32)),
        grid=(nt,), in_specs=in_specs, out_specs=out_specs,
        compiler_params=pltpu.CompilerParams(dimension_semantics=("arbitrary",), vmem_limit_bytes=VMEM_LIMIT),
        name="gmlp",
    )(x, g, win_bf, lng, lnb, ws2, bias2, wout_bf)


ROUTE_E0, ROUTE_E1, ROUTE_R0, ROUTE_R1, ROUTE_G0, ROUTE_G1 = range(6)
ROUTE_ROWS = 8
N_LOGITS = N_EXPERT_GROUPS + N_EXPERTS
LOGIT_ROWS = -(-N_LOGITS // SUBLANES) * SUBLANES


def _router_kernel(x_ref, g_ref, wr_ref, route_ref, cnt_ref, carry_ref, *, tm):
    i = pl.program_id(0)

    @pl.when(i == 0)
    def _():
        carry_ref[...] = jnp.zeros(carry_ref.shape, F32)

    h = _rms(x_ref[...], g_ref[...])
    nt = (((1,), (1,)), ((), ()))
    logits = lax.dot_general(wr_ref[...], h.astype(BF16), nt, preferred_element_type=F32)[:LOGIT_ROWS]
    row = lax.broadcasted_iota(I32, (LOGIT_ROWS, tm), 0).astype(F32)
    ninf = -jnp.inf
    big = float(LANES)
    cmax = lambda a: jnp.max(a, axis=0, keepdims=True)
    cmin = lambda a: jnp.min(a, axis=0, keepdims=True)
    csum = lambda a: jnp.sum(a, axis=0, keepdims=True)

    is_grp = row < N_EXPERT_GROUPS
    lg = jnp.where(is_grp, logits, ninf)
    mg = cmax(lg)
    g_sel = cmin(jnp.where(lg == mg, row, big))
    p_grp = 1.0 / csum(jnp.where(is_grp, jnp.exp(lg - mg), 0.0))
    lo = N_EXPERT_GROUPS + g_sel * EXPERTS_PER_GROUP
    le = jnp.where((row >= lo) & (row < lo + EXPERTS_PER_GROUP), logits, ninf)
    v1 = cmax(le)
    j1 = cmin(jnp.where(le == v1, row, big))
    le2 = jnp.where(row == j1, ninf, le)
    v2 = cmax(le2)
    j2 = cmin(jnp.where(le2 == v2, row, big))
    e0 = j1 - N_EXPERT_GROUPS
    e1 = j2 - N_EXPERT_GROUPS
    r = jnp.exp(v2 - v1)
    g0 = p_grp / (1.0 + r)
    g1 = p_grp * r / (1.0 + r)

    erow = lax.broadcasted_iota(I32, (N_EXPERTS, tm), 0).astype(F32)
    oh0 = jnp.where(erow == e0, 1.0, 0.0)
    oh1 = jnp.where(erow == e1, 1.0, 0.0)
    oh = oh0 + oh1
    earlier = jnp.where(lax.broadcasted_iota(I32, (tm, tm), 0) < lax.broadcasted_iota(I32, (tm, tm), 1),
                        1.0, 0.0).astype(BF16)
    before = jnp.dot(oh.astype(BF16), earlier, preferred_element_type=F32) + carry_ref[...]
    r0 = csum(oh0 * before)
    r1 = csum(oh1 * before)
    carry_ref[...] = carry_ref[...] + jnp.sum(oh, axis=1, keepdims=True)
    cnt_ref[...] = jnp.broadcast_to(carry_ref[...], cnt_ref.shape)

    rows = {ROUTE_E0: e0, ROUTE_E1: e1, ROUTE_R0: r0, ROUTE_R1: r1, ROUTE_G0: g0, ROUTE_G1: g1}
    zero = jnp.zeros((1, tm), F32)
    route_ref[...] = jnp.concatenate([rows.get(k, zero) for k in range(ROUTE_ROWS)], axis=0)


def _router(x, g, wr_t, *, tm):
    N, D = x.shape
    const = lambda s: pl.BlockSpec(s, lambda i: (0, 0))
    return pl.pallas_call(
        functools.partial(_router_kernel, tm=tm),
        out_shape=(jax.ShapeDtypeStruct((ROUTE_ROWS, N), F32), jax.ShapeDtypeStruct((N_EXPERTS, LANES), F32)),
        grid=(N // tm,), in_specs=[pl.BlockSpec((tm, D), lambda i: (i, 0)), const((1, D)), const((LANES, D))],
        out_specs=(pl.BlockSpec((ROUTE_ROWS, tm), lambda i: (0, i)), const((N_EXPERTS, LANES))),
        scratch_shapes=[pltpu.VMEM((N_EXPERTS, 1), F32)],
        compiler_params=pltpu.CompilerParams(dimension_semantics=("arbitrary",), vmem_limit_bytes=VMEM_LIMIT),
        name="router",
    )(x, g, wr_t)


def _pack_bf16_pairs(h):
    half = h.shape[1] // 2
    lo = lax.bitcast_convert_type(h[:, :half].astype(BF16).astype(F32), U32)
    hi = lax.bitcast_convert_type(h[:, half:].astype(BF16).astype(F32), U32)
    return (lo >> 16) | (hi & jnp.uint32(0xFFFF0000))


def _unpack_bf16_pairs(w):
    lo = lax.bitcast_convert_type(w << 16, F32).astype(BF16)
    hi = lax.bitcast_convert_type(w & jnp.uint32(0xFFFF0000), F32).astype(BF16)
    return jnp.concatenate([lo, hi], axis=1)


def _dispatch_kernel(d0_ref, d1_ref, pad_ref, x_ref, g_ref, rows_ref, h_scr, zblk, sems, *, tm):
    i = pl.program_id(0)
    n = pl.num_programs(0)
    slot = i % 2
    groups = tm // SUBLANES

    def wait_slot(s):
        for _ in range(2 * groups):
            pltpu.make_async_copy(h_scr.at[s, 0], rows_ref.at[pl.ds(0, SUBLANES)], sems.at[s]).wait()

    @pl.when(i == 0)
    def _():
        zblk[...] = jnp.zeros(zblk.shape, U32)
        n_blocks = rows_ref.shape[0] // MOE_BLOCK
        n_used = pad_ref[2, 0]

        def blk_copy(row0):
            return pltpu.make_async_copy(zblk, rows_ref.at[pl.ds(row0, MOE_BLOCK)], sems.at[2])

        for wait in (False, True):
            for e in range(N_EXPERTS):
                @pl.when(pad_ref[1, e] > 0)
                def _():
                    cp = blk_copy(pl.multiple_of(pad_ref[0, e], MOE_BLOCK))
                    cp.wait() if wait else cp.start()

            def unused(b, carry):
                cp = blk_copy(pl.multiple_of(b * MOE_BLOCK, MOE_BLOCK))
                cp.wait() if wait else cp.start()
                return carry
            lax.fori_loop(n_used, n_blocks, unused, 0)

    @pl.when(i >= 2)
    def _():
        wait_slot(slot)

    h_scr[slot] = _pack_bf16_pairs(_rms(x_ref[...], g_ref[...])).reshape(groups, SUBLANES, h_scr.shape[3])

    def body(k, carry):
        for u in range(SUBLANES):
            r = k * SUBLANES + u
            src = h_scr.at[slot, k, pl.ds(u, 1)]
            pltpu.make_async_copy(src, rows_ref.at[pl.ds(d0_ref[r], 1)], sems.at[slot]).start()
            pltpu.make_async_copy(src, rows_ref.at[pl.ds(d1_ref[r], 1)], sems.at[slot]).start()
        return carry

    lax.fori_loop(0, groups, body, 0)

    @pl.when(i == n - 1)
    def _():
        wait_slot(slot)

    @pl.when(jnp.logical_and(i == n - 1, n >= 2))
    def _():
        wait_slot(1 - slot)


def _dispatch(dest_flat, pad_info, x, g, *, tm, n_rows):
    N, D = x.shape
    nt = N // tm
    return pl.pallas_call(
        functools.partial(_dispatch_kernel, tm=tm),
        out_shape=jax.ShapeDtypeStruct((n_rows, D // 2), U32), grid=(nt,),
        in_specs=[pl.BlockSpec((tm,), lambda i: (i,), memory_space=pltpu.SMEM),
                  pl.BlockSpec((tm,), lambda i: (i + nt,), memory_space=pltpu.SMEM),
                  pl.BlockSpec(memory_space=pltpu.SMEM),
                  pl.BlockSpec((tm, D), lambda i: (i, 0)),
                  pl.BlockSpec((1, D), lambda i: (0, 0))],
        out_specs=pl.BlockSpec(memory_space=pl.ANY),
        scratch_shapes=[pltpu.VMEM((2, tm // SUBLANES, SUBLANES, D // 2), U32), pltpu.VMEM((MOE_BLOCK, D // 2), U32),
                        pltpu.SemaphoreType.DMA((3,))],
        compiler_params=pltpu.CompilerParams(dimension_semantics=("arbitrary",), vmem_limit_bytes=VMEM_LIMIT),
        name="dispatch",
    )(dest_flat, dest_flat, pad_info, x, g)


def _expert_kernel(be_ref, nu_ref, x_ref, wg_ref, wu_ref, wd_ref, y_ref, wg_bf, wu_bf, wd_bf):
    i = pl.program_id(0)

    @pl.when(jnp.logical_or(i == 0, be_ref[i] != be_ref[jnp.maximum(i - 1, 0)]))
    def _():
        wg_bf[...] = wg_ref[0, 0].astype(BF16)
        wu_bf[...] = wu_ref[0, 0].astype(BF16)
        wd_bf[...] = wd_ref[0, 0].astype(BF16)

    @pl.when(i < nu_ref[0])
    def _():
        x = _unpack_bf16_pairs(x_ref[...])
        a = jnp.dot(x, wg_bf[...], preferred_element_type=F32)
        u = jnp.dot(x, wu_bf[...], preferred_element_type=F32)
        act = (a * jax.nn.sigmoid(a)) * u
        y_ref[...] = jnp.dot(act.astype(BF16), wd_bf[...], preferred_element_type=F32)

    @pl.when(i >= nu_ref[0])
    def _():
        y_ref[...] = jnp.zeros(y_ref.shape, F32)


def _experts(block_e, n_used, x_rows, w_gate, w_up, w_down, layer):
    n_rows = x_rows.shape[0]
    D, de = w_gate.shape[2:]
    n_blocks = n_rows // MOE_BLOCK
    grid_spec = pltpu.PrefetchScalarGridSpec(
        num_scalar_prefetch=2, grid=(n_blocks,),
        in_specs=[pl.BlockSpec((MOE_BLOCK, D // 2), lambda i, be, nu: (jnp.minimum(i, nu[0] - 1), 0)),
                  pl.BlockSpec((1, 1, D, de), lambda i, be, nu: (layer, be[i], 0, 0)),
                  pl.BlockSpec((1, 1, D, de), lambda i, be, nu: (layer, be[i], 0, 0)),
                  pl.BlockSpec((1, 1, de, D), lambda i, be, nu: (layer, be[i], 0, 0))],
        out_specs=pl.BlockSpec((MOE_BLOCK, D), lambda i, be, nu: (i, 0)),
        scratch_shapes=[pltpu.VMEM((D, de), BF16), pltpu.VMEM((D, de), BF16), pltpu.VMEM((de, D), BF16)])
    return pl.pallas_call(
        _expert_kernel, out_shape=jax.ShapeDtypeStruct((n_rows, D), F32), grid_spec=grid_spec,
        compiler_params=pltpu.CompilerParams(dimension_semantics=("arbitrary",), vmem_limit_bytes=VMEM_LIMIT),
        name="experts",
    )(block_e, n_used, x_rows, w_gate, w_up, w_down)


def _combine_kernel(dc0_ref, dc1_ref, dn0_ref, dn1_ref, x_ref, route_ref, g_ref, y_ref, *rest,
                    tm, final_norm, t_first):
    *o_refs, ybuf, sems = rest
    i = pl.program_id(0)
    n = pl.num_programs(0)
    slot = i % 2
    groups = tm // SUBLANES
    D = x_ref.shape[1]

    def gather(d_refs, s):
        def body(k, carry):
            for u in range(SUBLANES):
                r = k * SUBLANES + u
                for a in range(2):
                    pltpu.make_async_copy(y_ref.at[pl.ds(d_refs[a][r], 1)], ybuf.at[s, a, k, pl.ds(u, 1)],
                                          sems.at[s]).start()
            return carry
        lax.fori_loop(0, groups, body, 0)

    @pl.when(i == 0)
    def _():
        gather((dc0_ref, dc1_ref), 0)

    @pl.when(i + 1 < n)
    def _():
        gather((dn0_ref, dn1_ref), 1 - slot)

    for _ in range(2 * groups):
        pltpu.make_async_copy(y_ref.at[pl.ds(0, SUBLANES)], ybuf.at[slot, 0, 0], sems.at[slot]).wait()
    route = jnp.transpose(route_ref[...])
    g0 = route[:, ROUTE_G0:ROUTE_G0 + 1]
    g1 = route[:, ROUTE_G1:ROUTE_G1 + 1]
    out = x_ref[...] + (ybuf[slot, 0].reshape(tm, D) * g0 + ybuf[slot, 1].reshape(tm, D) * g1)
    if final_norm:
        out = _rms(out, g_ref[...])
    if t_first is None:
        o_refs[0][...] = out
    else:
        @pl.when(i < t_first)
        def _():
            o_refs[0][...] = out

        @pl.when(i >= t_first)
        def _():
            o_refs[1][...] = out


def _combine(dest_flat, x, route, g, y_rows, *, tm, final_norm, n_first=None):
    N, D = x.shape
    nt = N // tm
    row = lambda wdt: pl.BlockSpec((tm, wdt), lambda i: (i, 0))
    nxt = lambda i: jnp.minimum(i + 1, nt - 1)
    smem = lambda f: pl.BlockSpec((tm,), f, memory_space=pltpu.SMEM)
    if n_first is None:
        t_first, out_shape, out_specs = None, jax.ShapeDtypeStruct((N, D), F32), row(D)
    else:
        t_first = n_first // tm
        out_shape = (jax.ShapeDtypeStruct((n_first, D), F32), jax.ShapeDtypeStruct((N - n_first, D), F32))
        out_specs = (pl.BlockSpec((tm, D), lambda i: (jnp.minimum(i, t_first - 1), 0)),
                     pl.BlockSpec((tm, D), lambda i: (jnp.maximum(i - t_first, 0), 0)))
    return pl.pallas_call(
        functools.partial(_combine_kernel, tm=tm, final_norm=final_norm, t_first=t_first),
        out_shape=out_shape, grid=(nt,),
        in_specs=[smem(lambda i: (i,)), smem(lambda i: (i + nt,)),
                  smem(lambda i: (nxt(i),)), smem(lambda i: (nxt(i) + nt,)),
                  row(D), pl.BlockSpec((ROUTE_ROWS, tm), lambda i: (0, i)), pl.BlockSpec((1, D), lambda i: (0, 0)),
                  pl.BlockSpec(memory_space=pl.ANY)],
        out_specs=out_specs,
        scratch_shapes=[pltpu.VMEM((2, 2, tm // SUBLANES, SUBLANES, D), F32), pltpu.SemaphoreType.DMA((2,))],
        compiler_params=pltpu.CompilerParams(dimension_semantics=("arbitrary",), vmem_limit_bytes=VMEM_LIMIT),
        name="combine",
    )(dest_flat, dest_flat, dest_flat, dest_flat, x, route, g, y_rows)


def _hier_moe(x, g_ffn, w_rg, w_re, w_gate, w_up, w_down, layer, g_final, *, final_norm, n_first=None):
    N, D = x.shape
    tm = TOKEN_TILE
    wr_t = jnp.concatenate([w_rg.T, w_re.T, jnp.zeros((LANES - N_LOGITS, D), F32)], axis=0).astype(BF16)
    g_ffn = g_ffn.reshape(1, D)
    route, counts = _router(x, g_ffn, wr_t, tm=tm)
    counts = counts[:, 0].astype(I32)
    eid_t = route[ROUTE_E0:ROUTE_E1 + 1].astype(I32)
    rank_t = route[ROUTE_R0:ROUTE_R1 + 1].astype(I32)
    padded = (counts + MOE_BLOCK - 1) // MOE_BLOCK * MOE_BLOCK
    pad_end = jnp.cumsum(padded)
    pad_start = pad_end - padded
    expert_ids = jnp.arange(N_EXPERTS, dtype=I32)
    start_of = jnp.sum(jnp.where(eid_t[:, :, None] == expert_ids, pad_start, 0), axis=-1)
    dest_t = start_of + rank_t
    n_blocks = -(-(2 * N) // MOE_BLOCK) + N_EXPERTS
    n_rows = n_blocks * MOE_BLOCK
    n_used = (pad_end[-1] // MOE_BLOCK).astype(I32)
    block_start = jnp.minimum(jnp.arange(n_blocks, dtype=I32), n_used - 1) * MOE_BLOCK
    block_e = jnp.minimum(jnp.sum(block_start[:, None] >= pad_end[None, :], axis=1), N_EXPERTS - 1).astype(I32)
    pad_info = jnp.stack([pad_end - MOE_BLOCK, padded - counts, jnp.broadcast_to(n_used, (N_EXPERTS,))])
    dest_flat = dest_t.reshape(-1)
    x_rows = _dispatch(dest_flat, pad_info, x, g_ffn, tm=tm, n_rows=n_rows)
    y_rows = _experts(block_e, n_used.reshape(1), x_rows, w_gate, w_up, w_down, layer)
    return _combine(dest_flat, x, route, g_final.reshape(1, D), y_rows, tm=tm, final_norm=final_norm,
                    n_first=n_first)


def _rope_tables(pos, rot_lanes):
    inv = 1.0 / (ROPE_THETA ** (jnp.arange(ROT_HALF, dtype=F32) / ROT_HALF))
    ang = pos.astype(F32)[:, None] * inv[None, :]
    cos, sin = jnp.cos(ang), jnp.sin(ang)
    lane = jnp.arange(LANES)
    r = lane % HEAD_DIM
    active = (lane < rot_lanes)
    first = active & (r < ROT_HALF)
    second = active & (r >= ROT_HALF) & (r < 2 * ROT_HALF)
    cos_l = cos[:, r % ROT_HALF]
    sin_l = sin[:, r % ROT_HALF]
    c = jnp.where((first | second)[None, :], cos_l, 1.0)
    sa = jnp.where(first[None, :], -sin_l, 0.0)
    sb = jnp.where(second[None, :], sin_l, 0.0)
    return jnp.stack([c, sa, sb]).astype(F32)


def _prep_w_in(w_in):
    D = w_in.shape[0]
    assert w_in.shape[1] == sum(IN_SIZES)
    pad = jnp.zeros((D, IN_WIDTH_PAD - w_in.shape[1]), w_in.dtype)
    return jnp.concatenate([w_in, pad], axis=1).astype(BF16)


def kernel(x_prompt, x_sample, cache_k, cache_v, cache_idx_k, state_pool, norm_mix, norm_ffn, norm_final,
           par_w_in, par_pool_w, par_pool_scale, par_w_out, gm_w_in, gm_ln_g, gm_ln_b, gm_ws, gm_bs, gm_w_out,
           moe_router_group, moe_router_expert, moe_w_gate, moe_w_up, moe_w_down):
    Bp, Tp, D = x_prompt.shape
    Bs, Ts, _ = x_sample.shape
    past = cache_k.shape[2]
    Np, Ns = Bp * Tp, Bs * Ts
    depth = norm_mix.shape[0]
    assert depth == 2 and Ts == CHUNK and Tp % PROJ_TILE == 0 and Tp % QUERY_TILE == 0 and Tp % TOKEN_TILE == 0 and Ns % TOKEN_TILE == 0

    w_in_bf = _prep_w_in(par_w_in[0])
    pw_bf = par_pool_w[0].astype(BF16)
    ps = par_pool_scale[0].reshape(1, POOL_WIDTH)
    wo_bf = par_w_out[0].astype(BF16)
    g_mix0 = norm_mix[0].reshape(1, D)
    pos_p = jnp.arange(Tp, dtype=I32)
    pos_s = past + jnp.arange(Ts, dtype=I32)

    hist_p = jnp.zeros((Bp, HIST_ROWS, POOL_WIDTH), F32)
    hist_s = jnp.pad(state_pool[0], ((0, 0), (1, 0), (0, 0)))
    tk = KEY_BLOCK
    (q_p, qi_p, k_p, v_p, kiwi_p, kbf_p, vbf_p, kibf_p, yp_p, st_p) = _inproj(
        x_prompt, g_mix0, w_in_bf, _rope_tables(pos_p, LANES), _rope_tables(pos_p, IDX_DIM), hist_p, pw_bf, ps,
        tm=PROJ_TILE, pos0=0)
    (q_s, qi_s, k_s, v_s, kiwi_s, kbf_s, vbf_s, kibf_s, yp_s, st_s) = _inproj(
        x_sample, g_mix0, w_in_bf, _rope_tables(pos_s, LANES), _rope_tables(pos_s, IDX_DIM), hist_s, pw_bf, ps,
        tm=Ts, pos0=past)

    ls = past + Ts
    n_blk = -(-ls // tk)
    tk_s = -(-ls // (n_blk * LANES)) * LANES
    lsp = n_blk * tk_s
    kpad = lambda a, ax: jnp.pad(a, [(0, lsp - ls) if d == ax else (0, 0) for d in range(a.ndim)])
    kall = kpad(jnp.concatenate([jnp.transpose(cache_k[0], (0, 2, 1, 3)).astype(BF16), kbf_s], axis=2), 2)
    vall = kpad(jnp.concatenate([cache_v[0].reshape(Bs, past, KV_WIDTH).astype(BF16), vbf_s], axis=1), 1)
    vall_t = jnp.transpose(vall.reshape(Bs, lsp // LANES, LANES, KV_WIDTH), (0, 1, 3, 2))
    kiall = kpad(jnp.concatenate([cache_idx_k[0].astype(BF16), kibf_s], axis=1), 1)
    qpad = lambda a, ax: jnp.pad(a, [(0, LANES - Ts) if d == ax else (0, 0) for d in range(a.ndim)])
    x1_s = _dsa(qpad(q_s, 2), qpad(qi_s, 2), qpad(kiwi_s, 1), kall, vall_t, kiall, qpad(x_sample, 1), qpad(yp_s, 1),
                wo_bf, tq=LANES, tk=tk_s, pos0=past, n_keys=ls, n_sel=min(TOPK_MAX, ls // 4))[:, :Ts]
    assert Tp % tk == 0
    x = jnp.pad(x1_s.reshape(Ns, D), ((Np, 0), (0, 0)))
    x = _dsa(q_p, qi_p, kiwi_p, kbf_p, vbf_p, kibf_p, x_prompt, yp_p, wo_bf,
             tq=QUERY_TILE, tk=tk, pos0=0, n_keys=Tp, n_sel=min(TOPK_MAX, Tp // 4), out_base=x)
    x = _hier_moe(x, norm_ffn[0], moe_router_group[0], moe_router_expert[0],
                  moe_w_gate, moe_w_up, moe_w_down, 0, norm_final, final_norm=False)

    cs = Ts
    tril = lambda n: jnp.tril(jnp.ones((n, n), bool))
    ws_p = jnp.where(tril(GM_CHUNK)[None], gm_ws[0], 0.0)
    ws_small = jnp.where(tril(cs)[None], gm_ws[0][:, :cs, :cs], 0.0)
    rep = GM_CHUNK // cs
    ws_s = jnp.einsum('ab,gts->gatbs', jnp.eye(rep, dtype=F32), ws_small).reshape(GM_GROUPS, GM_CHUNK, GM_CHUNK)
    ws2 = jnp.stack([ws_p, ws_s]).astype(BF16)
    gd = D // GM_GROUPS
    bias_p = jnp.repeat(jnp.transpose(gm_bs[0]), gd, axis=1)
    bias_s = jnp.tile(jnp.repeat(jnp.transpose(gm_bs[0][:, :cs]), gd, axis=1), (rep, 1))
    bias2 = jnp.stack([bias_p, bias_s])
    x, gm_v = _gmlp(x, norm_mix[1].reshape(1, D), gm_w_in[0].astype(BF16), gm_ln_g[0].reshape(1, D),
                    gm_ln_b[0].reshape(1, D), ws2, bias2, gm_w_out[0].astype(BF16),
                    tm=TOKEN_TILE, n_first=Np, n_v_rows=Ns)
    y_p, y_s = _hier_moe(x, norm_ffn[1], moe_router_group[1], moe_router_expert[1],
                         moe_w_gate, moe_w_up, moe_w_down, 1, norm_final, final_norm=True, n_first=Np)
    y_p = y_p.reshape(Bp, Tp, D)
    y_s = y_s.reshape(Bs, Ts, D)
    r4 = lambda a, b, t: a.reshape(1, b, t, N_KV_HEADS, HEAD_DIM)
    return (y_p, y_s,
            r4(k_p, Bp, Tp), r4(v_p, Bp, Tp), kiwi_p[:, :, :IDX_DIM][None], st_p[:, 1:][None],
            r4(k_s, Bs, Ts), r4(v_s, Bs, Ts), kiwi_s[:, :, :IDX_DIM][None], st_s[:, 1:][None],
            gm_v.reshape(1, Bs, Ts, D))
```

```python
import functools

import jax
import jax.numpy as jnp
from jax import lax
from jax.experimental import pallas as pl
from jax.experimental.pallas import tpu as pltpu

F32 = jnp.float32
BF16 = jnp.bfloat16
I32 = jnp.int32
U32 = jnp.uint32

LANES = 128
SUBLANES = 8
CHUNK = 64
POOL_WINDOWS = (2, 4, 8, 16)
POOL_GROUP_DIM = 128
POOL_WIDTH = 512
HIST_ROWS = 16
N_HEADS = 8
HEAD_DIM = 64
N_KV_HEADS = 4
Q_PER_KV = N_HEADS // N_KV_HEADS
ATT_WIDTH = N_HEADS * HEAD_DIM
KV_WIDTH = N_KV_HEADS * HEAD_DIM
N_IDX_HEADS = 8
IDX_DIM = 64
TOPK_MAX = 256
ROPE_THETA = 500000.0
ROT_HALF = HEAD_DIM // 8
GM_CHUNK = 128
GM_GROUPS = 8
N_EXPERT_GROUPS = 4
EXPERTS_PER_GROUP = 8
N_EXPERTS = 32
MOE_BLOCK = 512
RMS_EPS = 1e-6
LN_EPS = 1e-5

INT_MIN = -2147483648
LOG2_E = 1.4426950408889634
NEG_BIG = -1e30
VMEM_LIMIT = 48 * 1024 * 1024

TOKEN_TILE = 1024
COMBINE_TILE = 512
assert TOKEN_TILE % COMBINE_TILE == 0
PROJ_TILE = 1024
KEY_BLOCK = 512
QUERY_TILE = 256
COUNT_SLAB = 64

IN_SIZES = (POOL_WIDTH, ATT_WIDTH, KV_WIDTH, KV_WIDTH, N_IDX_HEADS * IDX_DIM, IDX_DIM, N_IDX_HEADS)
COL_XP, COL_Q, COL_K, COL_V, COL_QI, COL_KIWI = (sum(IN_SIZES[:i]) for i in range(6))
IN_WIDTH_PAD = COL_KIWI + LANES


def _rms(x, g):
    return x * lax.rsqrt(jnp.mean(x * x, axis=-1, keepdims=True) + RMS_EPS) * g


def _rope128(x, c, sa, sb):
    return x * c + pltpu.roll(x, LANES - ROT_HALF, 1) * sa + pltpu.roll(x, ROT_HALF, 1) * sb


def _inproj_kernel(x_ref, g_ref, w_ref, rope_ref, ropeki_ref, hist_ref, pw_ref, ps_ref,
                   q_ref, qi_ref, k_ref, v_ref, kiwi_ref, kbf_ref, vbf_ref, kibf_ref, yp_ref, state_ref,
                   buf_ref, *, tm, pos0):
    j = pl.program_id(1)
    h = _rms(x_ref[0], g_ref[...])
    proj = jnp.dot(h.astype(BF16), w_ref[...], preferred_element_type=F32)
    c, sa, sb = rope_ref[0], rope_ref[1], rope_ref[2]

    def put_heads(ref, i, chunk):
        ref[0, 2 * i] = chunk[:, :HEAD_DIM].astype(BF16)
        ref[0, 2 * i + 1] = chunk[:, HEAD_DIM:].astype(BF16)

    for i in range(ATT_WIDTH // LANES):
        put_heads(q_ref, i, _rope128(proj[:, COL_Q + i * LANES:COL_Q + (i + 1) * LANES], c, sa, sb)
                  * (HEAD_DIM ** -0.5 * LOG2_E))
        put_heads(qi_ref, i, _rope128(proj[:, COL_QI + i * LANES:COL_QI + (i + 1) * LANES], c, sa, sb))
    for i in range(KV_WIDTH // LANES):
        sl = slice(i * LANES, (i + 1) * LANES)
        kr = _rope128(proj[:, COL_K + i * LANES:COL_K + (i + 1) * LANES], c, sa, sb)
        k_ref[0, :, sl] = kr
        put_heads(kbf_ref, i, kr)
    vv = proj[:, COL_V:COL_V + KV_WIDTH]
    v_ref[0] = vv
    if tm % LANES == 0:
        for cc in range(tm // LANES):
            vbf_ref[0, cc] = jnp.transpose(vv[cc * LANES:(cc + 1) * LANES, :]).astype(BF16)
    else:
        vbf_ref[0] = vv.astype(BF16)
    kiwi = _rope128(proj[:, COL_KIWI:COL_KIWI + LANES], ropeki_ref[0], ropeki_ref[1], ropeki_ref[2])
    kiwi_ref[0] = kiwi
    kibf_ref[0] = kiwi[:, :IDX_DIM].astype(BF16)

    @pl.when(j == 0)
    def _():
        buf_ref[0:HIST_ROWS, :] = hist_ref[0]

    xp = proj[:, COL_XP:COL_XP + POOL_WIDTH]
    buf_ref[HIST_ROWS:HIST_ROWS + tm, :] = xp
    pos = pos0 + j * tm + lax.broadcasted_iota(I32, (tm, 1), 0)
    for gi, w in enumerate(POOL_WINDOWS):
        c0 = gi * POOL_GROUP_DIM
        s = xp[:, c0:c0 + POOL_GROUP_DIM]
        for i in range(1, w):
            s = s + buf_ref[HIST_ROWS - i:HIST_ROWS - i + tm, c0:c0 + POOL_GROUP_DIM]
        cnt = jnp.minimum(pos + 1, w).astype(F32)
        d = s / cnt - xp[:, c0:c0 + POOL_GROUP_DIM]
        y = jnp.dot(d.astype(BF16), pw_ref[gi], preferred_element_type=F32)
        yp_ref[0, :, c0:c0 + POOL_GROUP_DIM] = (y * ps_ref[:, c0:c0 + POOL_GROUP_DIM]).astype(BF16)
    tail = buf_ref[tm:tm + HIST_ROWS, :]
    state_ref[0] = tail
    buf_ref[0:HIST_ROWS, :] = tail


def _inproj(x, g, w_bf, rope, ropeki, hist, pw_bf, ps, *, tm, pos0):
    B, T, D = x.shape
    nt = T // tm
    f = lambda shape, dt: jax.ShapeDtypeStruct(shape, dt)
    v_t = tm % LANES == 0
    out_shape = (
        f((B, N_HEADS, T, HEAD_DIM), BF16), f((B, N_IDX_HEADS, T, IDX_DIM), BF16),
        f((B, T, KV_WIDTH), F32), f((B, T, KV_WIDTH), F32), f((B, T, LANES), F32),
        f((B, N_KV_HEADS, T, HEAD_DIM), BF16),
        f((B, T // LANES, KV_WIDTH, LANES) if v_t else (B, T, KV_WIDTH), BF16),
        f((B, T, IDX_DIM), BF16),
        f((B, T, POOL_WIDTH), BF16), f((B, HIST_ROWS, POOL_WIDTH), F32),
    )
    tile = lambda wdt: pl.BlockSpec((1, tm, wdt), lambda b, j: (b, j, 0))
    heads = lambda n, wdt: pl.BlockSpec((1, n, tm, wdt), lambda b, j: (b, 0, j, 0))
    const2 = lambda s: pl.BlockSpec(s, lambda b, j: (0, 0))
    in_specs = [
        tile(D), const2((1, D)), const2((D, IN_WIDTH_PAD)),
        pl.BlockSpec((3, tm, LANES), lambda b, j: (0, j, 0)),
        pl.BlockSpec((3, tm, LANES), lambda b, j: (0, j, 0)),
        pl.BlockSpec((1, HIST_ROWS, POOL_WIDTH), lambda b, j: (b, 0, 0)),
        pl.BlockSpec((len(POOL_WINDOWS), POOL_GROUP_DIM, POOL_GROUP_DIM), lambda b, j: (0, 0, 0)),
        const2((1, POOL_WIDTH)),
    ]
    out_specs = (
        heads(N_HEADS, HEAD_DIM), heads(N_IDX_HEADS, IDX_DIM), tile(KV_WIDTH), tile(KV_WIDTH), tile(LANES),
        heads(N_KV_HEADS, HEAD_DIM),
        pl.BlockSpec((1, tm // LANES, KV_WIDTH, LANES), lambda b, j: (b, j, 0, 0)) if v_t else tile(KV_WIDTH),
        tile(IDX_DIM), tile(POOL_WIDTH),
        pl.BlockSpec((1, HIST_ROWS, POOL_WIDTH), lambda b, j: (b, 0, 0)),
    )
    return pl.pallas_call(
        functools.partial(_inproj_kernel, tm=tm, pos0=pos0),
        out_shape=out_shape, grid=(B, nt), in_specs=in_specs, out_specs=out_specs,
        scratch_shapes=[pltpu.VMEM((HIST_ROWS + tm, POOL_WIDTH), F32)],
        compiler_params=pltpu.CompilerParams(dimension_semantics=("parallel", "arbitrary"),
                                             vmem_limit_bytes=VMEM_LIMIT),
        name="inproj",
    )(x, g, w_bf, rope, ropeki, hist, pw_bf, ps)


def _dsa_kernel(q_ref, qi_ref, kiwi_ref, k_ref, vt_ref, ki_ref, x_ref, yp_ref, wo_ref, *rest,
                tq, tk, pos0, n_keys, n_sel):
    o_ref, key_buf, bias_buf, m_scr, l_scr, acc_scr, s_scr = rest[-7:]
    slab = COUNT_SLAB
    j = pl.program_id(1)
    base = pos0 + j * tq
    pos = base + lax.broadcasted_iota(I32, (1, tq), 1)
    limit = jnp.minimum((pos // CHUNK + 1) * CHUNK, n_keys)
    limit_max = jnp.minimum(((base + tq - 1) // CHUNK + 1) * CHUNK, n_keys)
    nkb = (limit_max + tk - 1) // tk
    nt = (((1,), (1,)), ((), ()))

    wi_t = jnp.transpose(kiwi_ref[0])[IDX_DIM:IDX_DIM + N_IDX_HEADS, :]

    def score_block(kb, carry):
        off = pl.multiple_of(kb * tk, tk)
        kiblk = ki_ref[0, pl.ds(off, tk), :]
        idx = jnp.zeros((tk, tq), F32)
        for h in range(N_IDX_HEADS):
            sc = lax.dot_general(kiblk, qi_ref[0, h], nt, preferred_element_type=F32)
            idx = idx + jnp.maximum(sc, 0.0) * wi_t[h:h + 1, :]
        idx = jnp.where(idx == 0.0, 0.0, idx)
        bits = lax.bitcast_convert_type(idx, I32)
        key = bits ^ ((bits >> 31) & 0x7FFFFFFF)
        key_buf[kb] = jnp.where(lax.broadcasted_iota(I32, (tk, tq), 0) < limit - kb * tk, key, INT_MIN)
        return carry

    lax.fori_loop(0, nkb, score_block, 0)

    def col_sum(a):
        return jnp.sum(a, axis=0, keepdims=True)

    def count_ge(cand):
        def body(kb, acc):
            kblk = key_buf[kb]
            for c in range(tk // slab):
                acc = acc + jnp.where(kblk[c * slab:(c + 1) * slab] >= cand, 1.0, 0.0)
            return acc
        return col_sum(lax.fori_loop(0, nkb, body, jnp.zeros((slab, tq), F32)))

    kf = float(n_sel)
    cnt0 = count_ge(jnp.zeros((1, tq), I32))
    t0 = jnp.where(cnt0 >= kf, 0, INT_MIN).astype(I32)

    def bit_body(i, carry):
        t, cnt = carry
        cand = t | lax.shift_left(jnp.int32(1), 30 - i)
        cnt_cand = count_ge(cand)
        keep = cnt_cand >= kf
        return jnp.where(keep, cand, t), jnp.where(keep, cnt_cand, cnt)

    t, cnt_ge = lax.fori_loop(0, 31, bit_body, (t0, cnt0))
    cnt_gt = count_ge(t + 1)
    need = kf - cnt_gt
    cnt_eq = cnt_ge - cnt_gt
    overfull = jnp.where(t != INT_MIN, cnt_eq - need, 0.0)
    slow = jnp.max(overfull) > 0.0

    @pl.when(jnp.logical_not(slow))
    def _():
        t_adm = jnp.maximum(t, INT_MIN + 1)

        def body(kb, carry):
            bias_buf[kb] = jnp.where(key_buf[kb] >= t_adm, 0.0, NEG_BIG)
            return carry
        lax.fori_loop(0, nkb, body, 0)

    @pl.when(slow)
    def _():
        tri = jnp.where(lax.broadcasted_iota(I32, (tk, tk), 1) <= lax.broadcasted_iota(I32, (tk, tk), 0),
                        1.0, 0.0).astype(BF16)

        def body(kb, seen):
            kblk = key_buf[kb]
            eq = jnp.where((kblk == t) & (kblk != INT_MIN), 1.0, 0.0)
            prefix = jnp.dot(tri, eq.astype(BF16), preferred_element_type=F32) + seen
            keep_tie = jnp.where(prefix <= need, eq, 0.0)
            sel = jnp.where(kblk > t, 1.0, keep_tie)
            bias_buf[kb] = jnp.where(sel > 0.0, 0.0, NEG_BIG)
            return seen + col_sum(eq)
        lax.fori_loop(0, nkb, body, jnp.zeros((1, tq), F32))

    sub = LANES
    nsub = tk // sub
    tl = LANES
    n_tiles = tq // tl
    m_scr[...] = jnp.full(m_scr.shape, NEG_BIG, F32)
    l_scr[...] = jnp.zeros(l_scr.shape, F32)
    acc_scr[...] = jnp.zeros(acc_scr.shape, F32)

    def tile_blocks(hq):
        return (jnp.minimum(((base + (hq + 1) * tl - 1) // CHUNK + 1) * CHUNK, n_keys) + tk - 1) // tk

    def attn_block(kb, members):
        for mi, hq in enumerate(members):
            for c in range(nsub):
                off = pl.multiple_of(kb * tk + c * sub, sub)
                for g in range(N_KV_HEADS):
                    s_scr[mi, c, g] = lax.dot_general(
                        k_ref[0, g, pl.ds(off, sub), :],
                        q_ref[0, Q_PER_KV * g:Q_PER_KV * (g + 1), hq * tl:(hq + 1) * tl].reshape(Q_PER_KV * tl, HEAD_DIM),
                        nt, preferred_element_type=F32)
        for mi, hq in enumerate(members):
            m = [m_scr[hq, g] for g in range(N_KV_HEADS)]
            l = [l_scr[hq, g] for g in range(N_KV_HEADS)]
            for c in range(nsub):
                bias = bias_buf[kb, c * sub:(c + 1) * sub, hq * tl:(hq + 1) * tl]
                bias2 = jnp.concatenate([bias] * Q_PER_KV, axis=1)
                for g in range(N_KV_HEADS):
                    s = s_scr[mi, c, g] + bias2
                    m_new = jnp.maximum(m[g], jnp.max(s, axis=0, keepdims=True))
                    alpha = jnp.exp2(m[g] - m_new)
                    p = jnp.exp2(s - m_new)
                    l[g] = alpha * l[g] + col_sum(p)
                    pv = jnp.dot(vt_ref[0, kb * nsub + c, g * HEAD_DIM:(g + 1) * HEAD_DIM, :], p.astype(BF16),
                                 preferred_element_type=F32)
                    acc_scr[hq, g] = alpha * acc_scr[hq, g] + pv
                    m[g] = m_new
            for g in range(N_KV_HEADS):
                m_scr[hq, g] = m[g]
                l_scr[hq, g] = l[g]

    for first in range(0, n_tiles, 2):
        members = tuple(range(first, min(first + 2, n_tiles)))
        shared = tile_blocks(members[0])
        lax.fori_loop(0, shared, lambda kb, c, ms=members: (attn_block(kb, ms), c)[1], 0)
        if len(members) == 2:
            lax.fori_loop(shared, tile_blocks(members[1]),
                          lambda kb, c, ms=members[1:]: (attn_block(kb, ms), c)[1], 0)

    y_att = []
    for hq in range(n_tiles):
        o_t = []
        for g in range(N_KV_HEADS):
            og = acc_scr[hq, g] / l_scr[hq, g]
            o_t.extend(og[:, hh * tl:(hh + 1) * tl] for hh in range(Q_PER_KV))
        y_att.append(jnp.transpose(jnp.concatenate(o_t, axis=0)).astype(BF16))
    y_att = jnp.concatenate(y_att, axis=0)

    y = jnp.dot(yp_ref[0], wo_ref[0:POOL_WIDTH, :], preferred_element_type=F32)
    y = y + jnp.dot(y_att, wo_ref[POOL_WIDTH:POOL_WIDTH + ATT_WIDTH, :], preferred_element_type=F32)
    o_ref[...] = (x_ref[0] + y).reshape(o_ref.shape)


def _dsa(q_hm, qi_hm, kiwi, k_hm, v_t, ki_all, x, yp, wo_bf, *, tq, tk, pos0, n_keys, n_sel, out_base=None):
    B, T, D = x.shape
    L = k_hm.shape[2]
    assert L % tk == 0 and T % tq == 0 and tq % LANES == 0
    nkb_max = L // tk
    tile = lambda wdt: pl.BlockSpec((1, tq, wdt), lambda b, j: (b, j, 0))
    in_specs = [pl.BlockSpec((1, N_HEADS, tq, HEAD_DIM), lambda b, j: (b, 0, j, 0)),
                pl.BlockSpec((1, N_IDX_HEADS, tq, IDX_DIM), lambda b, j: (b, 0, j, 0)),
                tile(LANES),
                pl.BlockSpec((1, N_KV_HEADS, L, HEAD_DIM), lambda b, j: (b, 0, 0, 0)),
                pl.BlockSpec((1, L // LANES, KV_WIDTH, LANES), lambda b, j: (b, 0, 0, 0)),
                pl.BlockSpec((1, L, IDX_DIM), lambda b, j: (b, 0, 0)),
                tile(D), tile(POOL_WIDTH), pl.BlockSpec((POOL_WIDTH + ATT_WIDTH, D), lambda b, j: (0, 0))]
    nq = T // tq
    args = (q_hm, qi_hm, kiwi, k_hm, v_t, ki_all, x, yp, wo_bf)
    if out_base is None:
        out_shape, out_spec, aliases = jax.ShapeDtypeStruct((B, T, D), F32), tile(D), {}
    else:
        out_shape = jax.ShapeDtypeStruct(out_base.shape, F32)
        out_spec = pl.BlockSpec((tq, D), lambda b, j: (b * nq + j, 0))
        in_specs, args, aliases = in_specs + [pl.BlockSpec(memory_space=pl.ANY)], args + (out_base,), {len(args): 0}
    return pl.pallas_call(
        functools.partial(_dsa_kernel, tq=tq, tk=tk, pos0=pos0, n_keys=n_keys, n_sel=n_sel),
        out_shape=out_shape, grid=(B, nq), in_specs=in_specs, out_specs=out_spec, input_output_aliases=aliases,
        scratch_shapes=[pltpu.VMEM((nkb_max, tk, tq), I32), pltpu.VMEM((nkb_max, tk, tq), F32),
                        pltpu.VMEM((tq // LANES, N_KV_HEADS, 1, Q_PER_KV * LANES), F32),
                        pltpu.VMEM((tq // LANES, N_KV_HEADS, 1, Q_PER_KV * LANES), F32),
                        pltpu.VMEM((tq // LANES, N_KV_HEADS, HEAD_DIM, Q_PER_KV * LANES), F32),
                        pltpu.VMEM((2, tk // LANES, N_KV_HEADS, LANES, Q_PER_KV * LANES), F32)],
        compiler_params=pltpu.CompilerParams(dimension_semantics=("parallel", "arbitrary"),
                                             vmem_limit_bytes=VMEM_LIMIT),
        name="dsa",
    )(*args)


def _gmlp_kernel(x_ref, g_ref, win_ref, lng_ref, lnb_ref, ws_ref, bias_ref, wout_ref, o_ref, v_ref, *, tm):
    x = x_ref[...]
    h = _rms(x, g_ref[...])
    z = jax.nn.gelu(jnp.dot(h.astype(BF16), win_ref[...], preferred_element_type=F32))
    half = z.shape[1] // 2
    u, v = z[:, :half], z[:, half:]
    mu = jnp.mean(v, axis=-1, keepdims=True)
    var = jnp.mean(jnp.square(v - mu), axis=-1, keepdims=True)
    vn = (v - mu) * lax.rsqrt(var + LN_EPS) * lng_ref[...] + lnb_ref[...]
    v_ref[...] = vn
    gd = half // GM_GROUPS
    gated = []
    for c in range(tm // GM_CHUNK):
        rows = slice(c * GM_CHUNK, (c + 1) * GM_CHUNK)
        vc = vn[rows].astype(BF16)
        mixed = jnp.concatenate(
            [jnp.dot(ws_ref[0, g], vc[:, g * gd:(g + 1) * gd], preferred_element_type=F32)
             for g in range(GM_GROUPS)], axis=1) + bias_ref[0]
        gated.append((u[rows] * mixed).astype(BF16))
    gated = jnp.concatenate(gated, axis=0)
    o_ref[...] = x + jnp.dot(gated, wout_ref[...], preferred_element_type=F32)


def _gmlp(x, g, win_bf, lng, lnb, ws2, bias2, wout_bf, *, tm, n_first, n_v_rows):
    N, D = x.shape
    half = win_bf.shape[1] // 2
    nt = N // tm
    t_first = n_first // tm
    variant = lambda i: jnp.where(i >= t_first, 1, 0)
    row = pl.BlockSpec((tm, D), lambda i: (i, 0))
    const = lambda s: pl.BlockSpec(s, lambda i: (0, 0))
    in_specs = [row, const((1, D)), const((D, 2 * half)), const((1, half)), const((1, half)),
                pl.BlockSpec((1, GM_GROUPS, GM_CHUNK, GM_CHUNK), lambda i: (variant(i), 0, 0, 0)),
                pl.BlockSpec((1, GM_CHUNK, half), lambda i: (variant(i), 0, 0)),
                const((half, D))]
    out_specs = (row, pl.BlockSpec((tm, half), lambda i: (jnp.maximum(i - t_first, 0), 0)))
    return pl.pallas_call(
        functools.partial(_gmlp_kernel, tm=tm),
        out_shape=(jax.ShapeDtypeStruct((N, D), F32), jax.ShapeDtypeStruct((n_v_rows, half), F32)),
        grid=(nt,), in_specs=in_specs, out_specs=out_specs,
        compiler_params=pltpu.CompilerParams(dimension_semantics=("arbitrary",), vmem_limit_bytes=VMEM_LIMIT),
        name="gmlp",
    )(x, g, win_bf, lng, lnb, ws2, bias2, wout_bf)


ROUTE_E0, ROUTE_E1, ROUTE_R0, ROUTE_R1, ROUTE_G0, ROUTE_G1 = range(6)
ROUTE_ROWS = 8
N_LOGITS = N_EXPERT_GROUPS + N_EXPERTS
LOGIT_ROWS = -(-N_LOGITS // SUBLANES) * SUBLANES


def _router_kernel(x_ref, g_ref, wr_ref, route_ref, cnt_ref, carry_ref, *, tm):
    i = pl.program_id(0)

    @pl.when(i == 0)
    def _():
        carry_ref[...] = jnp.zeros(carry_ref.shape, F32)

    h = _rms(x_ref[...], g_ref[...])
    nt = (((1,), (1,)), ((), ()))
    logits = lax.dot_general(wr_ref[...], h.astype(BF16), nt, preferred_element_type=F32)[:LOGIT_ROWS]
    row = lax.broadcasted_iota(I32, (LOGIT_ROWS, tm), 0).astype(F32)
    ninf = -jnp.inf
    big = float(LANES)
    cmax = lambda a: jnp.max(a, axis=0, keepdims=True)
    cmin = lambda a: jnp.min(a, axis=0, keepdims=True)
    csum = lambda a: jnp.sum(a, axis=0, keepdims=True)

    is_grp = row < N_EXPERT_GROUPS
    lg = jnp.where(is_grp, logits, ninf)
    mg = cmax(lg)
    g_sel = cmin(jnp.where(lg == mg, row, big))
    p_grp = 1.0 / csum(jnp.where(is_grp, jnp.exp(lg - mg), 0.0))
    lo = N_EXPERT_GROUPS + g_sel * EXPERTS_PER_GROUP
    le = jnp.where((row >= lo) & (row < lo + EXPERTS_PER_GROUP), logits, ninf)
    v1 = cmax(le)
    j1 = cmin(jnp.where(le == v1, row, big))
    le2 = jnp.where(row == j1, ninf, le)
    v2 = cmax(le2)
    j2 = cmin(jnp.where(le2 == v2, row, big))
    e0 = j1 - N_EXPERT_GROUPS
    e1 = j2 - N_EXPERT_GROUPS
    r = jnp.exp(v2 - v1)
    g0 = p_grp / (1.0 + r)
    g1 = p_grp * r / (1.0 + r)

    erow = lax.broadcasted_iota(I32, (N_EXPERTS, tm), 0).astype(F32)
    oh0 = jnp.where(erow == e0, 1.0, 0.0)
    oh1 = jnp.where(erow == e1, 1.0, 0.0)
    oh = oh0 + oh1
    earlier = jnp.where(lax.broadcasted_iota(I32, (tm, tm), 0) < lax.broadcasted_iota(I32, (tm, tm), 1),
                        1.0, 0.0).astype(BF16)
    before = jnp.dot(oh.astype(BF16), earlier, preferred_element_type=F32) + carry_ref[...]
    r0 = csum(oh0 * before)
    r1 = csum(oh1 * before)
    carry_ref[...] = carry_ref[...] + jnp.sum(oh, axis=1, keepdims=True)
    cnt_ref[...] = jnp.broadcast_to(carry_ref[...], cnt_ref.shape)

    rows = {ROUTE_E0: e0, ROUTE_E1: e1, ROUTE_R0: r0, ROUTE_R1: r1, ROUTE_G0: g0, ROUTE_G1: g1}
    zero = jnp.zeros((1, tm), F32)
    route_ref[...] = jnp.concatenate([rows.get(k, zero) for k in range(ROUTE_ROWS)], axis=0)


def _router(x, g, wr_t, *, tm):
    N, D = x.shape
    const = lambda s: pl.BlockSpec(s, lambda i: (0, 0))
    return pl.pallas_call(
        functools.partial(_router_kernel, tm=tm),
        out_shape=(jax.ShapeDtypeStruct((ROUTE_ROWS, N), F32), jax.ShapeDtypeStruct((N_EXPERTS, LANES), F32)),
        grid=(N // tm,), in_specs=[pl.BlockSpec((tm, D), lambda i: (i, 0)), const((1, D)), const((LANES, D))],
        out_specs=(pl.BlockSpec((ROUTE_ROWS, tm), lambda i: (0, i)), const((N_EXPERTS, LANES))),
        scratch_shapes=[pltpu.VMEM((N_EXPERTS, 1), F32)],
        compiler_params=pltpu.CompilerParams(dimension_semantics=("arbitrary",), vmem_limit_bytes=VMEM_LIMIT),
        name="router",
    )(x, g, wr_t)


def _pack_bf16_pairs(h):
    half = h.shape[1] // 2
    lo = lax.bitcast_convert_type(h[:, :half].astype(BF16).astype(F32), U32)
    hi = lax.bitcast_convert_type(h[:, half:].astype(BF16).astype(F32), U32)
    return (lo >> 16) | (hi & jnp.uint32(0xFFFF0000))


def _unpack_bf16_pairs(w):
    lo = lax.bitcast_convert_type(w << 16, F32).astype(BF16)
    hi = lax.bitcast_convert_type(w & jnp.uint32(0xFFFF0000), F32).astype(BF16)
    return jnp.concatenate([lo, hi], axis=1)


def _dispatch_kernel(d0_ref, d1_ref, pad_ref, x_ref, g_ref, rows_ref, h_scr, zblk, sems, *, tm):
    i = pl.program_id(0)
    n = pl.num_programs(0)
    slot = i % 2
    groups = tm // SUBLANES

    def wait_slot(s):
        for _ in range(2 * groups):
            pltpu.make_async_copy(h_scr.at[s, 0], rows_ref.at[pl.ds(0, SUBLANES)], sems.at[s]).wait()

    @pl.when(i == 0)
    def _():
        zblk[...] = jnp.zeros(zblk.shape, U32)
        n_blocks = rows_ref.shape[0] // MOE_BLOCK
        n_used = pad_ref[2, 0]

        def blk_copy(row0):
            return pltpu.make_async_copy(zblk, rows_ref.at[pl.ds(row0, MOE_BLOCK)], sems.at[2])

        for wait in (False, True):
            for e in range(N_EXPERTS):
                @pl.when(pad_ref[1, e] > 0)
                def _():
                    cp = blk_copy(pl.multiple_of(pad_ref[0, e], MOE_BLOCK))
                    cp.wait() if wait else cp.start()

            def unused(b, carry):
                cp = blk_copy(pl.multiple_of(b * MOE_BLOCK, MOE_BLOCK))
                cp.wait() if wait else cp.start()
                return carry
            lax.fori_loop(n_used, n_blocks, unused, 0)

    @pl.when(i >= 2)
    def _():
        wait_slot(slot)

    h_scr[slot] = _pack_bf16_pairs(_rms(x_ref[...], g_ref[...])).reshape(groups, SUBLANES, h_scr.shape[3])

    def body(k, carry):
        for u in range(SUBLANES):
            r = k * SUBLANES + u
            src = h_scr.at[slot, k, pl.ds(u, 1)]
            pltpu.make_async_copy(src, rows_ref.at[pl.ds(d0_ref[r], 1)], sems.at[slot]).start()
            pltpu.make_async_copy(src, rows_ref.at[pl.ds(d1_ref[r], 1)], sems.at[slot]).start()
        return carry

    lax.fori_loop(0, groups, body, 0)

    @pl.when(i == n - 1)
    def _():
        wait_slot(slot)

    @pl.when(jnp.logical_and(i == n - 1, n >= 2))
    def _():
        wait_slot(1 - slot)


def _dispatch(dest_flat, pad_info, x, g, *, tm, n_rows):
    N, D = x.shape
    nt = N // tm
    return pl.pallas_call(
        functools.partial(_dispatch_kernel, tm=tm),
        out_shape=jax.ShapeDtypeStruct((n_rows, D // 2), U32), grid=(nt,),
        in_specs=[pl.BlockSpec((tm,), lambda i: (i,), memory_space=pltpu.SMEM),
                  pl.BlockSpec((tm,), lambda i: (i + nt,), memory_space=pltpu.SMEM),
                  pl.BlockSpec(memory_space=pltpu.SMEM),
                  pl.BlockSpec((tm, D), lambda i: (i, 0)),
                  pl.BlockSpec((1, D), lambda i: (0, 0))],
        out_specs=pl.BlockSpec(memory_space=pl.ANY),
        scratch_shapes=[pltpu.VMEM((2, tm // SUBLANES, SUBLANES, D // 2), U32), pltpu.VMEM((MOE_BLOCK, D // 2), U32),
                        pltpu.SemaphoreType.DMA((3,))],
        compiler_params=pltpu.CompilerParams(dimension_semantics=("arbitrary",), vmem_limit_bytes=VMEM_LIMIT),
        name="dispatch",
    )(dest_flat, dest_flat, pad_info, x, g)


def _expert_kernel(be_ref, nu_ref, x_ref, wg_ref, wu_ref, wd_ref, y_ref, wg_bf, wu_bf, wd_bf):
    i = pl.program_id(0)

    @pl.when(jnp.logical_or(i == 0, be_ref[i] != be_ref[jnp.maximum(i - 1, 0)]))
    def _():
        wg_bf[...] = wg_ref[0, 0].astype(BF16)
        wu_bf[...] = wu_ref[0, 0].astype(BF16)
        wd_bf[...] = wd_ref[0, 0].astype(BF16)

    @pl.when(i < nu_ref[0])
    def _():
        x = _unpack_bf16_pairs(x_ref[...])
        a = jnp.dot(x, wg_bf[...], preferred_element_type=F32)
        u = jnp.dot(x, wu_bf[...], preferred_element_type=F32)
        act = (a * jax.nn.sigmoid(a)) * u
        y_ref[...] = jnp.dot(act.astype(BF16), wd_bf[...], preferred_element_type=F32)

    @pl.when(i >= nu_ref[0])
    def _():
        y_ref[...] = jnp.zeros(y_ref.shape, F32)


def _experts(block_e, n_used, x_rows, w_gate, w_up, w_down, layer):
    n_rows = x_rows.shape[0]
    D, de = w_gate.shape[2:]
    n_blocks = n_rows // MOE_BLOCK
    grid_spec = pltpu.PrefetchScalarGridSpec(
        num_scalar_prefetch=2, grid=(n_blocks,),
        in_specs=[pl.BlockSpec((MOE_BLOCK, D // 2), lambda i, be, nu: (jnp.minimum(i, nu[0] - 1), 0)),
                  pl.BlockSpec((1, 1, D, de), lambda i, be, nu: (layer, be[i], 0, 0)),
                  pl.BlockSpec((1, 1, D, de), lambda i, be, nu: (layer, be[i], 0, 0)),
                  pl.BlockSpec((1, 1, de, D), lambda i, be, nu: (layer, be[i], 0, 0))],
        out_specs=pl.BlockSpec((MOE_BLOCK, D), lambda i, be, nu: (i, 0)),
        scratch_shapes=[pltpu.VMEM((D, de), BF16), pltpu.VMEM((D, de), BF16), pltpu.VMEM((de, D), BF16)])
    return pl.pallas_call(
        _expert_kernel, out_shape=jax.ShapeDtypeStruct((n_rows, D), F32), grid_spec=grid_spec,
        compiler_params=pltpu.CompilerParams(dimension_semantics=("arbitrary",), vmem_limit_bytes=VMEM_LIMIT),
        name="experts",
    )(block_e, n_used, x_rows, w_gate, w_up, w_down)


def _combine_kernel(dc0_ref, dc1_ref, dn0_ref, dn1_ref, x_ref, route_ref, g_ref, y_ref, *rest,
                    tm, final_norm, t_first):
    *o_refs, ybuf, sems = rest
    i = pl.program_id(0)
    n = pl.num_programs(0)
    slot = i % 2
    groups = tm // SUBLANES
    D = x_ref.shape[1]

    def gather(d_refs, s):
        def body(k, carry):
            for u in range(SUBLANES):
                r = k * SUBLANES + u
                for a in range(2):
                    pltpu.make_async_copy(y_ref.at[pl.ds(d_refs[a][r], 1)], ybuf.at[s, a, k, pl.ds(u, 1)],
                                          sems.at[s]).start()
            return carry
        lax.fori_loop(0, groups, body, 0)

    @pl.when(i == 0)
    def _():
        gather((dc0_ref, dc1_ref), 0)

    @pl.when(i + 1 < n)
    def _():
        gather((dn0_ref, dn1_ref), 1 - slot)

    for _ in range(2 * groups):
        pltpu.make_async_copy(y_ref.at[pl.ds(0, SUBLANES)], ybuf.at[slot, 0, 0], sems.at[slot]).wait()
    route = jnp.transpose(route_ref[...])
    g0 = route[:, ROUTE_G0:ROUTE_G0 + 1]
    g1 = route[:, ROUTE_G1:ROUTE_G1 + 1]
    out = x_ref[...] + (ybuf[slot, 0].reshape(tm, D) * g0 + ybuf[slot, 1].reshape(tm, D) * g1)
    if final_norm:
        out = _rms(out, g_ref[...])
    if t_first is None:
        o_refs[0][...] = out
    else:
        @pl.when(i < t_first)
        def _():
            o_refs[0][...] = out

        @pl.when(i >= t_first)
        def _():
            o_refs[1][...] = out


def _combine(dest_flat, x, route, g, y_rows, *, tm, final_norm, n_first=None):
    N, D = x.shape
    nt = N // tm
    row = lambda wdt: pl.BlockSpec((tm, wdt), lambda i: (i, 0))
    nxt = lambda i: jnp.minimum(i + 1, nt - 1)
    smem = lambda f: pl.BlockSpec((tm,), f, memory_space=pltpu.SMEM)
    if n_first is None:
        t_first, out_shape, out_specs = None, jax.ShapeDtypeStruct((N, D), F32), row(D)
    else:
        t_first = n_first // tm
        out_shape = (jax.ShapeDtypeStruct((n_first, D), F32), jax.ShapeDtypeStruct((N - n_first, D), F32))
        out_specs = (pl.BlockSpec((tm, D), lambda i: (jnp.minimum(i, t_first - 1), 0)),
                     pl.BlockSpec((tm, D), lambda i: (jnp.maximum(i - t_first, 0), 0)))
    return pl.pallas_call(
        functools.partial(_combine_kernel, tm=tm, final_norm=final_norm, t_first=t_first),
        out_shape=out_shape, grid=(nt,),
        in_specs=[smem(lambda i: (i,)), smem(lambda i: (i + nt,)),
                  smem(lambda i: (nxt(i),)), smem(lambda i: (nxt(i) + nt,)),
                  row(D), pl.BlockSpec((ROUTE_ROWS, tm), lambda i: (0, i)), pl.BlockSpec((1, D), lambda i: (0, 0)),
                  pl.BlockSpec(memory_space=pl.ANY)],
        out_specs=out_specs,
        scratch_shapes=[pltpu.VMEM((2, 2, tm // SUBLANES, SUBLANES, D), F32), pltpu.SemaphoreType.DMA((2,))],
        compiler_params=pltpu.CompilerParams(dimension_semantics=("arbitrary",), vmem_limit_bytes=VMEM_LIMIT),
        name="combine",
    )(dest_flat, dest_flat, dest_flat, dest_flat, x, route, g, y_rows)


def _hier_moe(x, g_ffn, w_rg, w_re, w_gate, w_up, w_down, layer, g_final, *, final_norm, n_first=None):
    N, D = x.shape
    tm = TOKEN_TILE
    wr_t = jnp.concatenate([w_rg.T, w_re.T, jnp.zeros((LANES - N_LOGITS, D), F32)], axis=0).astype(BF16)
    g_ffn = g_ffn.reshape(1, D)
    route, counts = _router(x, g_ffn, wr_t, tm=tm)
    counts = counts[:, 0].astype(I32)
    eid_t = route[ROUTE_E0:ROUTE_E1 + 1].astype(I32)
    rank_t = route[ROUTE_R0:ROUTE_R1 + 1].astype(I32)
    padded = (counts + MOE_BLOCK - 1) // MOE_BLOCK * MOE_BLOCK
    pad_end = jnp.cumsum(padded)
    pad_start = pad_end - padded
    expert_ids = jnp.arange(N_EXPERTS, dtype=I32)
    start_of = jnp.sum(jnp.where(eid_t[:, :, None] == expert_ids, pad_start, 0), axis=-1)
    dest_t = start_of + rank_t
    n_blocks = -(-(2 * N) // MOE_BLOCK) + N_EXPERTS
    n_rows = n_blocks * MOE_BLOCK
    n_used = (pad_end[-1] // MOE_BLOCK).astype(I32)
    block_start = jnp.minimum(jnp.arange(n_blocks, dtype=I32), n_used - 1) * MOE_BLOCK
    block_e = jnp.minimum(jnp.sum(block_start[:, None] >= pad_end[None, :], axis=1), N_EXPERTS - 1).astype(I32)
    pad_info = jnp.stack([pad_end - MOE_BLOCK, padded - counts, jnp.broadcast_to(n_used, (N_EXPERTS,))])
    dest_flat = dest_t.reshape(-1)
    x_rows = _dispatch(dest_flat, pad_info, x, g_ffn, tm=tm, n_rows=n_rows)
    y_rows = _experts(block_e, n_used.reshape(1), x_rows, w_gate, w_up, w_down, layer)
    return _combine(dest_flat, x, route, g_final.reshape(1, D), y_rows, tm=COMBINE_TILE, final_norm=final_norm,
                    n_first=n_first)


def _rope_tables(pos, rot_lanes):
    inv = 1.0 / (ROPE_THETA ** (jnp.arange(ROT_HALF, dtype=F32) / ROT_HALF))
    ang = pos.astype(F32)[:, None] * inv[None, :]
    cos, sin = jnp.cos(ang), jnp.sin(ang)
    lane = jnp.arange(LANES)
    r = lane % HEAD_DIM
    active = (lane < rot_lanes)
    first = active & (r < ROT_HALF)
    second = active & (r >= ROT_HALF) & (r < 2 * ROT_HALF)
    cos_l = cos[:, r % ROT_HALF]
    sin_l = sin[:, r % ROT_HALF]
    c = jnp.where((first | second)[None, :], cos_l, 1.0)
    sa = jnp.where(first[None, :], -sin_l, 0.0)
    sb = jnp.where(second[None, :], sin_l, 0.0)
    return jnp.stack([c, sa, sb]).astype(F32)


def _prep_w_in(w_in):
    D = w_in.shape[0]
    assert w_in.shape[1] == sum(IN_SIZES)
    pad = jnp.zeros((D, IN_WIDTH_PAD - w_in.shape[1]), w_in.dtype)
    return jnp.concatenate([w_in, pad], axis=1).astype(BF16)


def kernel(x_prompt, x_sample, cache_k, cache_v, cache_idx_k, state_pool, norm_mix, norm_ffn, norm_final,
           par_w_in, par_pool_w, par_pool_scale, par_w_out, gm_w_in, gm_ln_g, gm_ln_b, gm_ws, gm_bs, gm_w_out,
           moe_router_group, moe_router_expert, moe_w_gate, moe_w_up, moe_w_down):
    Bp, Tp, D = x_prompt.shape
    Bs, Ts, _ = x_sample.shape
    past = cache_k.shape[2]
    Np, Ns = Bp * Tp, Bs * Ts
    depth = norm_mix.shape[0]
    assert depth == 2 and Ts == CHUNK and Tp % PROJ_TILE == 0 and Tp % QUERY_TILE == 0 and Tp % TOKEN_TILE == 0 and Ns % TOKEN_TILE == 0

    w_in_bf = _prep_w_in(par_w_in[0])
    pw_bf = par_pool_w[0].astype(BF16)
    ps = par_pool_scale[0].reshape(1, POOL_WIDTH)
    wo_bf = par_w_out[0].astype(BF16)
    g_mix0 = norm_mix[0].reshape(1, D)
    pos_p = jnp.arange(Tp, dtype=I32)
    pos_s = past + jnp.arange(Ts, dtype=I32)

    hist_p = jnp.zeros((Bp, HIST_ROWS, POOL_WIDTH), F32)
    hist_s = jnp.pad(state_pool[0], ((0, 0), (1, 0), (0, 0)))
    tk = KEY_BLOCK
    (q_p, qi_p, k_p, v_p, kiwi_p, kbf_p, vbf_p, kibf_p, yp_p, st_p) = _inproj(
        x_prompt, g_mix0, w_in_bf, _rope_tables(pos_p, LANES), _rope_tables(pos_p, IDX_DIM), hist_p, pw_bf, ps,
        tm=PROJ_TILE, pos0=0)
    (q_s, qi_s, k_s, v_s, kiwi_s, kbf_s, vbf_s, kibf_s, yp_s, st_s) = _inproj(
        x_sample, g_mix0, w_in_bf, _rope_tables(pos_s, LANES), _rope_tables(pos_s, IDX_DIM), hist_s, pw_bf, ps,
        tm=Ts, pos0=past)

    ls = past + Ts
    n_blk = -(-ls // tk)
    tk_s = -(-ls // (n_blk * LANES)) * LANES
    lsp = n_blk * tk_s
    kpad = lambda a, ax: jnp.pad(a, [(0, lsp - ls) if d == ax else (0, 0) for d in range(a.ndim)])
    kall = kpad(jnp.concatenate([jnp.transpose(cache_k[0], (0, 2, 1, 3)).astype(BF16), kbf_s], axis=2), 2)
    vall = kpad(jnp.concatenate([cache_v[0].reshape(Bs, past, KV_WIDTH).astype(BF16), vbf_s], axis=1), 1)
    vall_t = jnp.transpose(vall.reshape(Bs, lsp // LANES, LANES, KV_WIDTH), (0, 1, 3, 2))
    kiall = kpad(jnp.concatenate([cache_idx_k[0].astype(BF16), kibf_s], axis=1), 1)
    qpad = lambda a, ax: jnp.pad(a, [(0, LANES - Ts) if d == ax else (0, 0) for d in range(a.ndim)])
    x1_s = _dsa(qpad(q_s, 2), qpad(qi_s, 2), qpad(kiwi_s, 1), kall, vall_t, kiall, qpad(x_sample, 1), qpad(yp_s, 1),
                wo_bf, tq=LANES, tk=tk_s, pos0=past, n_keys=ls, n_sel=min(TOPK_MAX, ls // 4))[:, :Ts]
    assert Tp % tk == 0
    x = jnp.pad(x1_s.reshape(Ns, D), ((Np, 0), (0, 0)))
    x = _dsa(q_p, qi_p, kiwi_p, kbf_p, vbf_p, kibf_p, x_prompt, yp_p, wo_bf,
             tq=QUERY_TILE, tk=tk, pos0=0, n_keys=Tp, n_sel=min(TOPK_MAX, Tp // 4), out_base=x)
    x = _hier_moe(x, norm_ffn[0], moe_router_group[0], moe_router_expert[0],
                  moe_w_gate, moe_w_up, moe_w_down, 0, norm_final, final_norm=False)

    cs = Ts
    tril = lambda n: jnp.tril(jnp.ones((n, n), bool))
    ws_p = jnp.where(tril(GM_CHUNK)[None], gm_ws[0], 0.0)
    ws_small = jnp.where(tril(cs)[None], gm_ws[0][:, :cs, :cs], 0.0)
    rep = GM_CHUNK // cs
    ws_s = jnp.einsum('ab,gts->gatbs', jnp.eye(rep, dtype=F32), ws_small).reshape(GM_GROUPS, GM_CHUNK, GM_CHUNK)
    ws2 = jnp.stack([ws_p, ws_s]).astype(BF16)
    gd = D // GM_GROUPS
    bias_p = jnp.repeat(jnp.transpose(gm_bs[0]), gd, axis=1)
    bias_s = jnp.tile(jnp.repeat(jnp.transpose(gm_bs[0][:, :cs]), gd, axis=1), (rep, 1))
    bias2 = jnp.stack([bias_p, bias_s])
    x, gm_v = _gmlp(x, norm_mix[1].reshape(1, D), gm_w_in[0].astype(BF16), gm_ln_g[0].reshape(1, D),
                    gm_ln_b[0].reshape(1, D), ws2, bias2, gm_w_out[0].astype(BF16),
                    tm=TOKEN_TILE, n_first=Np, n_v_rows=Ns)
    y_p, y_s = _hier_moe(x, norm_ffn[1], moe_router_group[1], moe_router_expert[1],
                         moe_w_gate, moe_w_up, moe_w_down, 1, norm_final, final_norm=True, n_first=Np)
    y_p = y_p.reshape(Bp, Tp, D)
    y_s = y_s.reshape(Bs, Ts, D)
    r4 = lambda a, b, t: a.reshape(1, b, t, N_KV_HEADS, HEAD_DIM)
    return (y_p, y_s,
            r4(k_p, Bp, Tp), r4(v_p, Bp, Tp), kiwi_p[:, :, :IDX_DIM][None], st_p[:, 1:][None],
            r4(k_s, Bs, Ts), r4(v_s, Bs, Ts), kiwi_s[:, :, :IDX_DIM][None], st_s[:, 1:][None],
            gm_v.reshape(1, Bs, Ts, D))
```

```python
import functools

import jax
import jax.numpy as jnp
from jax import lax
from jax.experimental import pallas as pl
from jax.experimental.pallas import tpu as pltpu

F32 = jnp.float32
BF16 = jnp.bfloat16
I32 = jnp.int32
U32 = jnp.uint32

LANES = 128
SUBLANES = 8
CHUNK = 64
POOL_WINDOWS = (2, 4, 8, 16)
POOL_GROUP_DIM = 128
POOL_WIDTH = 512
HIST_ROWS = 16
N_HEADS = 8
HEAD_DIM = 64
N_KV_HEADS = 4
Q_PER_KV = N_HEADS // N_KV_HEADS
ATT_WIDTH = N_HEADS * HEAD_DIM
KV_WIDTH = N_KV_HEADS * HEAD_DIM
N_IDX_HEADS = 8
IDX_DIM = 64
TOPK_MAX = 256
ROPE_THETA = 500000.0
ROT_HALF = HEAD_DIM // 8
GM_CHUNK = 128
GM_GROUPS = 8
N_EXPERT_GROUPS = 4
EXPERTS_PER_GROUP = 8
N_EXPERTS = 32
MOE_BLOCK = 512
RMS_EPS = 1e-6
LN_EPS = 1e-5

INT_MIN = -2147483648
LOG2_E = 1.4426950408889634
NEG_BIG = -1e30
VMEM_LIMIT = 48 * 1024 * 1024

TOKEN_TILE = 1024
COMBINE_TILE = 512
assert TOKEN_TILE % COMBINE_TILE == 0
PROJ_TILE = 1024
KEY_BLOCK = 512
QUERY_TILE = 256
COUNT_SLAB = 64

IN_SIZES = (POOL_WIDTH, ATT_WIDTH, KV_WIDTH, KV_WIDTH, N_IDX_HEADS * IDX_DIM, IDX_DIM, N_IDX_HEADS)
COL_XP, COL_Q, COL_K, COL_V, COL_QI, COL_KIWI = (sum(IN_SIZES[:i]) for i in range(6))
IN_WIDTH_PAD = COL_KIWI + LANES


def _rms(x, g):
    return x * lax.rsqrt(jnp.mean(x * x, axis=-1, keepdims=True) + RMS_EPS) * g


def _rope128(x, c, sa, sb):
    return x * c + pltpu.roll(x, LANES - ROT_HALF, 1) * sa + pltpu.roll(x, ROT_HALF, 1) * sb


def _inproj_kernel(x_ref, g_ref, w_ref, rope_ref, ropeki_ref, hist_ref, pw_ref, ps_ref,
                   q_ref, qi_ref, k_ref, v_ref, kiwi_ref, kbf_ref, vbf_ref, kibf_ref, yp_ref, state_ref,
                   buf_ref, *, tm, pos0):
    j = pl.program_id(1)
    h = _rms(x_ref[0], g_ref[...])
    proj = jnp.dot(h.astype(BF16), w_ref[...], preferred_element_type=F32)
    c, sa, sb = rope_ref[0], rope_ref[1], rope_ref[2]

    def put_heads(ref, i, chunk):
        ref[0, 2 * i] = chunk[:, :HEAD_DIM].astype(BF16)
        ref[0, 2 * i + 1] = chunk[:, HEAD_DIM:].astype(BF16)

    for i in range(ATT_WIDTH // LANES):
        put_heads(q_ref, i, _rope128(proj[:, COL_Q + i * LANES:COL_Q + (i + 1) * LANES], c, sa, sb)
                  * (HEAD_DIM ** -0.5 * LOG2_E))
        put_heads(qi_ref, i, _rope128(proj[:, COL_QI + i * LANES:COL_QI + (i + 1) * LANES], c, sa, sb))
    for i in range(KV_WIDTH // LANES):
        sl = slice(i * LANES, (i + 1) * LANES)
        kr = _rope128(proj[:, COL_K + i * LANES:COL_K + (i + 1) * LANES], c, sa, sb)
        k_ref[0, :, sl] = kr
        kbf_ref[0, :, sl] = kr.astype(BF16)
    vv = proj[:, COL_V:COL_V + KV_WIDTH]
    v_ref[0] = vv
    if tm % LANES == 0:
        for cc in range(tm // LANES):
            vbf_ref[0, cc] = jnp.transpose(vv[cc * LANES:(cc + 1) * LANES, :]).astype(BF16)
    else:
        vbf_ref[0] = vv.astype(BF16)
    kiwi = _rope128(proj[:, COL_KIWI:COL_KIWI + LANES], ropeki_ref[0], ropeki_ref[1], ropeki_ref[2])
    kiwi_ref[0] = kiwi
    kibf_ref[0] = kiwi[:, :IDX_DIM].astype(BF16)

    @pl.when(j == 0)
    def _():
        buf_ref[0:HIST_ROWS, :] = hist_ref[0]

    xp = proj[:, COL_XP:COL_XP + POOL_WIDTH]
    buf_ref[HIST_ROWS:HIST_ROWS + tm, :] = xp
    pos = pos0 + j * tm + lax.broadcasted_iota(I32, (tm, 1), 0)
    for gi, w in enumerate(POOL_WINDOWS):
        c0 = gi * POOL_GROUP_DIM
        s = xp[:, c0:c0 + POOL_GROUP_DIM]
        for i in range(1, w):
            s = s + buf_ref[HIST_ROWS - i:HIST_ROWS - i + tm, c0:c0 + POOL_GROUP_DIM]
        cnt = jnp.minimum(pos + 1, w).astype(F32)
        d = s / cnt - xp[:, c0:c0 + POOL_GROUP_DIM]
        y = jnp.dot(d.astype(BF16), pw_ref[gi], preferred_element_type=F32)
        yp_ref[0, :, c0:c0 + POOL_GROUP_DIM] = (y * ps_ref[:, c0:c0 + POOL_GROUP_DIM]).astype(BF16)
    tail = buf_ref[tm:tm + HIST_ROWS, :]
    state_ref[0] = tail
    buf_ref[0:HIST_ROWS, :] = tail


def _inproj(x, g, w_bf, rope, ropeki, hist, pw_bf, ps, *, tm, pos0):
    B, T, D = x.shape
    nt = T // tm
    f = lambda shape, dt: jax.ShapeDtypeStruct(shape, dt)
    v_t = tm % LANES == 0
    out_shape = (
        f((B, N_HEADS, T, HEAD_DIM), BF16), f((B, N_IDX_HEADS, T, IDX_DIM), BF16),
        f((B, T, KV_WIDTH), F32), f((B, T, KV_WIDTH), F32), f((B, T, LANES), F32),
        f((B, T, KV_WIDTH), BF16),
        f((B, T // LANES, KV_WIDTH, LANES) if v_t else (B, T, KV_WIDTH), BF16),
        f((B, T, IDX_DIM), BF16),
        f((B, T, POOL_WIDTH), BF16), f((B, HIST_ROWS, POOL_WIDTH), F32),
    )
    tile = lambda wdt: pl.BlockSpec((1, tm, wdt), lambda b, j: (b, j, 0))
    heads = lambda n, wdt: pl.BlockSpec((1, n, tm, wdt), lambda b, j: (b, 0, j, 0))
    const2 = lambda s: pl.BlockSpec(s, lambda b, j: (0, 0))
    in_specs = [
        tile(D), const2((1, D)), const2((D, IN_WIDTH_PAD)),
        pl.BlockSpec((3, tm, LANES), lambda b, j: (0, j, 0)),
        pl.BlockSpec((3, tm, LANES), lambda b, j: (0, j, 0)),
        pl.BlockSpec((1, HIST_ROWS, POOL_WIDTH), lambda b, j: (b, 0, 0)),
        pl.BlockSpec((len(POOL_WINDOWS), POOL_GROUP_DIM, POOL_GROUP_DIM), lambda b, j: (0, 0, 0)),
        const2((1, POOL_WIDTH)),
    ]
    out_specs = (
        heads(N_HEADS, HEAD_DIM), heads(N_IDX_HEADS, IDX_DIM), tile(KV_WIDTH), tile(KV_WIDTH), tile(LANES),
        tile(KV_WIDTH),
        pl.BlockSpec((1, tm // LANES, KV_WIDTH, LANES), lambda b, j: (b, j, 0, 0)) if v_t else tile(KV_WIDTH),
        tile(IDX_DIM), tile(POOL_WIDTH),
        pl.BlockSpec((1, HIST_ROWS, POOL_WIDTH), lambda b, j: (b, 0, 0)),
    )
    return pl.pallas_call(
        functools.partial(_inproj_kernel, tm=tm, pos0=pos0),
        out_shape=out_shape, grid=(B, nt), in_specs=in_specs, out_specs=out_specs,
        scratch_shapes=[pltpu.VMEM((HIST_ROWS + tm, POOL_WIDTH), F32)],
        compiler_params=pltpu.CompilerParams(dimension_semantics=("parallel", "arbitrary"),
                                             vmem_limit_bytes=VMEM_LIMIT),
        name="inproj",
    )(x, g, w_bf, rope, ropeki, hist, pw_bf, ps)


def _dsa_kernel(q_ref, qi_ref, kiwi_ref, k_ref, vt_ref, ki_ref, x_ref, yp_ref, wo_ref, *rest,
                tq, tk, pos0, n_keys, n_sel):
    o_ref, key_buf, bias_buf, m_scr, l_scr, acc_scr, s_scr = rest[-7:]
    slab = COUNT_SLAB
    j = pl.program_id(1)
    base = pos0 + j * tq
    pos = base + lax.broadcasted_iota(I32, (1, tq), 1)
    limit = jnp.minimum((pos // CHUNK + 1) * CHUNK, n_keys)
    limit_max = jnp.minimum(((base + tq - 1) // CHUNK + 1) * CHUNK, n_keys)
    nkb = (limit_max + tk - 1) // tk
    nt = (((1,), (1,)), ((), ()))

    wi_t = jnp.transpose(kiwi_ref[0])[IDX_DIM:IDX_DIM + N_IDX_HEADS, :]

    def score_pair(i, carry):
        blocks = (2 * i, jnp.minimum(2 * i + 1, nkb - 1))
        scores = []
        for kb in blocks:
            off = pl.multiple_of(kb * tk, tk)
            kiblk = ki_ref[0, pl.ds(off, tk), :]
            scores.append([lax.dot_general(kiblk, qi_ref[0, h], nt, preferred_element_type=F32)
                           for h in range(N_IDX_HEADS)])
        for kb, sc_h in zip(blocks, scores):
            idx = jnp.zeros((tk, tq), F32)
            for h in range(N_IDX_HEADS):
                idx = idx + jnp.maximum(sc_h[h], 0.0) * wi_t[h:h + 1, :]
            idx = jnp.where(idx == 0.0, 0.0, idx)
            bits = lax.bitcast_convert_type(idx, I32)
            key = bits ^ ((bits >> 31) & 0x7FFFFFFF)
            key_buf[kb] = jnp.where(lax.broadcasted_iota(I32, (tk, tq), 0) < limit - kb * tk, key, INT_MIN)
        return carry

    lax.fori_loop(0, (nkb + 1) // 2, score_pair, 0)

    def col_sum(a):
        return jnp.sum(a, axis=0, keepdims=True)

    def count_ge(cand):
        def body(kb, acc):
            kblk = key_buf[kb]
            for c in range(tk // slab):
                acc = acc + jnp.where(kblk[c * slab:(c + 1) * slab] >= cand, 1.0, 0.0)
            return acc
        return col_sum(lax.fori_loop(0, nkb, body, jnp.zeros((slab, tq), F32)))

    kf = float(n_sel)
    cnt0 = count_ge(jnp.zeros((1, tq), I32))
    t0 = jnp.where(cnt0 >= kf, 0, INT_MIN).astype(I32)

    def bit_body(i, carry):
        t, cnt = carry
        cand = t | lax.shift_left(jnp.int32(1), 30 - i)
        cnt_cand = count_ge(cand)
        keep = cnt_cand >= kf
        return jnp.where(keep, cand, t), jnp.where(keep, cnt_cand, cnt)

    t, cnt_ge = lax.fori_loop(0, 31, bit_body, (t0, cnt0))
    cnt_gt = count_ge(t + 1)
    need = kf - cnt_gt
    cnt_eq = cnt_ge - cnt_gt
    overfull = jnp.where(t != INT_MIN, cnt_eq - need, 0.0)
    slow = jnp.max(overfull) > 0.0

    @pl.when(jnp.logical_not(slow))
    def _():
        t_adm = jnp.maximum(t, INT_MIN + 1)

        def body(kb, carry):
            bias_buf[kb] = jnp.where(key_buf[kb] >= t_adm, 0.0, NEG_BIG)
            return carry
        lax.fori_loop(0, nkb, body, 0)

    @pl.when(slow)
    def _():
        tri = jnp.where(lax.broadcasted_iota(I32, (tk, tk), 1) <= lax.broadcasted_iota(I32, (tk, tk), 0),
                        1.0, 0.0).astype(BF16)

        def body(kb, seen):
            kblk = key_buf[kb]
            eq = jnp.where((kblk == t) & (kblk != INT_MIN), 1.0, 0.0)
            prefix = jnp.dot(tri, eq.astype(BF16), preferred_element_type=F32) + seen
            keep_tie = jnp.where(prefix <= need, eq, 0.0)
            sel = jnp.where(kblk > t, 1.0, keep_tie)
            bias_buf[kb] = jnp.where(sel > 0.0, 0.0, NEG_BIG)
            return seen + col_sum(eq)
        lax.fori_loop(0, nkb, body, jnp.zeros((1, tq), F32))

    sub = LANES
    nsub = tk // sub
    tl = LANES
    n_tiles = tq // tl
    m_scr[...] = jnp.full(m_scr.shape, NEG_BIG, F32)
    l_scr[...] = jnp.zeros(l_scr.shape, F32)
    acc_scr[...] = jnp.zeros(acc_scr.shape, F32)

    def tile_blocks(hq):
        return (jnp.minimum(((base + (hq + 1) * tl - 1) // CHUNK + 1) * CHUNK, n_keys) + tk - 1) // tk

    def attn_block(kb, members):
        for mi, hq in enumerate(members):
            for c in range(nsub):
                off = pl.multiple_of(kb * tk + c * sub, sub)
                for g in range(N_KV_HEADS):
                    s_scr[mi, c, g] = lax.dot_general(
                        k_ref[0, pl.ds(off, sub), g * HEAD_DIM:(g + 1) * HEAD_DIM],
                        q_ref[0, Q_PER_KV * g:Q_PER_KV * (g + 1), hq * tl:(hq + 1) * tl].reshape(Q_PER_KV * tl, HEAD_DIM),
                        nt, preferred_element_type=F32)
        for mi, hq in enumerate(members):
            m = [m_scr[hq, g] for g in range(N_KV_HEADS)]
            l = [l_scr[hq, g] for g in range(N_KV_HEADS)]
            for c in range(nsub):
                bias = bias_buf[kb, c * sub:(c + 1) * sub, hq * tl:(hq + 1) * tl]
                bias2 = jnp.concatenate([bias] * Q_PER_KV, axis=1)
                for g in range(N_KV_HEADS):
                    s = s_scr[mi, c, g] + bias2
                    m_new = jnp.maximum(m[g], jnp.max(s, axis=0, keepdims=True))
                    alpha = jnp.exp2(m[g] - m_new)
                    p = jnp.exp2(s - m_new)
                    l[g] = alpha * l[g] + col_sum(p)
                    pv = jnp.dot(vt_ref[0, kb * nsub + c, g * HEAD_DIM:(g + 1) * HEAD_DIM, :], p.astype(BF16),
                                 preferred_element_type=F32)
                    acc_scr[hq, g] = alpha * acc_scr[hq, g] + pv
                    m[g] = m_new
            for g in range(N_KV_HEADS):
                m_scr[hq, g] = m[g]
                l_scr[hq, g] = l[g]

    for first in range(0, n_tiles, 2):
        members = tuple(range(first, min(first + 2, n_tiles)))
        shared = tile_blocks(members[0])
        lax.fori_loop(0, shared, lambda kb, c, ms=members: (attn_block(kb, ms), c)[1], 0)
        if len(members) == 2:
            lax.fori_loop(shared, tile_blocks(members[1]),
                          lambda kb, c, ms=members[1:]: (attn_block(kb, ms), c)[1], 0)

    y_att = []
    for hq in range(n_tiles):
        o_t = []
        for g in range(N_KV_HEADS):
            og = acc_scr[hq, g] / l_scr[hq, g]
            o_t.extend(og[:, hh * tl:(hh + 1) * tl] for hh in range(Q_PER_KV))
        y_att.append(jnp.transpose(jnp.concatenate(o_t, axis=0)).astype(BF16))
    y_att = jnp.concatenate(y_att, axis=0)

    y = jnp.dot(yp_ref[0], wo_ref[0:POOL_WIDTH, :], preferred_element_type=F32)
    y = y + jnp.dot(y_att, wo_ref[POOL_WIDTH:POOL_WIDTH + ATT_WIDTH, :], preferred_element_type=F32)
    o_ref[...] = (x_ref[0] + y).reshape(o_ref.shape)


def _dsa(q_hm, qi_hm, kiwi, k_hm, v_t, ki_all, x, yp, wo_bf, *, tq, tk, pos0, n_keys, n_sel, out_base=None):
    B, T, D = x.shape
    L = k_hm.shape[1]
    assert L % tk == 0 and T % tq == 0 and tq % LANES == 0
    nkb_max = L // tk
    tile = lambda wdt: pl.BlockSpec((1, tq, wdt), lambda b, j: (b, j, 0))
    in_specs = [pl.BlockSpec((1, N_HEADS, tq, HEAD_DIM), lambda b, j: (b, 0, j, 0)),
                pl.BlockSpec((1, N_IDX_HEADS, tq, IDX_DIM), lambda b, j: (b, 0, j, 0)),
                tile(LANES),
                pl.BlockSpec((1, L, KV_WIDTH), lambda b, j: (b, 0, 0)),
                pl.BlockSpec((1, L // LANES, KV_WIDTH, LANES), lambda b, j: (b, 0, 0, 0)),
                pl.BlockSpec((1, L, IDX_DIM), lambda b, j: (b, 0, 0)),
                tile(D), tile(POOL_WIDTH), pl.BlockSpec((POOL_WIDTH + ATT_WIDTH, D), lambda b, j: (0, 0))]
    nq = T // tq
    args = (q_hm, qi_hm, kiwi, k_hm, v_t, ki_all, x, yp, wo_bf)
    if out_base is None:
        out_shape, out_spec, aliases = jax.ShapeDtypeStruct((B, T, D), F32), tile(D), {}
    else:
        out_shape = jax.ShapeDtypeStruct(out_base.shape, F32)
        out_spec = pl.BlockSpec((tq, D), lambda b, j: (b * nq + j, 0))
        in_specs, args, aliases = in_specs + [pl.BlockSpec(memory_space=pl.ANY)], args + (out_base,), {len(args): 0}
    return pl.pallas_call(
        functools.partial(_dsa_kernel, tq=tq, tk=tk, pos0=pos0, n_keys=n_keys, n_sel=n_sel),
        out_shape=out_shape, grid=(B, nq), in_specs=in_specs, out_specs=out_spec, input_output_aliases=aliases,
        scratch_shapes=[pltpu.VMEM((nkb_max, tk, tq), I32), pltpu.VMEM((nkb_max, tk, tq), F32),
                        pltpu.VMEM((tq // LANES, N_KV_HEADS, 1, Q_PER_KV * LANES), F32),
                        pltpu.VMEM((tq // LANES, N_KV_HEADS, 1, Q_PER_KV * LANES), F32),
                        pltpu.VMEM((tq // LANES, N_KV_HEADS, HEAD_DIM, Q_PER_KV * LANES), F32),
                        pltpu.VMEM((2, tk // LANES, N_KV_HEADS, LANES, Q_PER_KV * LANES), F32)],
        compiler_params=pltpu.CompilerParams(dimension_semantics=("parallel", "arbitrary"),
                                             vmem_limit_bytes=VMEM_LIMIT),
        name="dsa",
    )(*args)


def _gmlp_kernel(x_ref, g_ref, win_ref, lng_ref, lnb_ref, ws_ref, bias_ref, wout_ref, o_ref, v_ref, *, tm):
    x = x_ref[...]
    h = _rms(x, g_ref[...])
    z = jax.nn.gelu(jnp.dot(h.astype(BF16), win_ref[...], preferred_element_type=F32))
    half = z.shape[1] // 2
    u, v = z[:, :half], z[:, half:]
    mu = jnp.mean(v, axis=-1, keepdims=True)
    var = jnp.mean(jnp.square(v - mu), axis=-1, keepdims=True)
    vn = (v - mu) * lax.rsqrt(var + LN_EPS) * lng_ref[...] + lnb_ref[...]
    v_ref[...] = vn
    gd = half // GM_GROUPS
    gated = []
    for c in range(tm // GM_CHUNK):
        rows = slice(c * GM_CHUNK, (c + 1) * GM_CHUNK)
        vc = vn[rows].astype(BF16)
        mixed = jnp.concatenate(
            [jnp.dot(ws_ref[0, g], vc[:, g * gd:(g + 1) * gd], preferred_element_type=F32)
             for g in range(GM_GROUPS)], axis=1) + bias_ref[0]
        gated.append((u[rows] * mixed).astype(BF16))
    gated = jnp.concatenate(gated, axis=0)
    o_ref[...] = x + jnp.dot(gated, wout_ref[...], preferred_element_type=F32)


def _gmlp(x, g, win_bf, lng, lnb, ws2, bias2, wout_bf, *, tm, n_first, n_v_rows):
    N, D = x.shape
    half = win_bf.shape[1] // 2
    nt = N // tm
    t_first = n_first // tm
    variant = lambda i: jnp.where(i >= t_first, 1, 0)
    row = pl.BlockSpec((tm, D), lambda i: (i, 0))
    const = lambda s: pl.BlockSpec(s, lambda i: (0, 0))
    in_specs = [row, const((1, D)), const((D, 2 * half)), const((1, half)), const((1, half)),
                pl.BlockSpec((1, GM_GROUPS, GM_CHUNK, GM_CHUNK), lambda i: (variant(i), 0, 0, 0)),
                pl.BlockSpec((1, GM_CHUNK, half), lambda i: (variant(i), 0, 0)),
                const((half, D))]
    out_specs = (row, pl.BlockSpec((tm, half), lambda i: (jnp.maximum(i - t_first, 0), 0)))
    return pl.pallas_call(
        functools.partial(_gmlp_kernel, tm=tm),
        out_shape=(jax.ShapeDtypeStruct((N, D), F32), jax.ShapeDtypeStruct((n_v_rows, half), F32)),
        grid=(nt,), in_specs=in_specs, out_specs=out_specs,
        compiler_params=pltpu.CompilerParams(dimension_semantics=("arbitrary",), vmem_limit_bytes=VMEM_LIMIT),
        name="gmlp",
    )(x, g, win_bf, lng, lnb, ws2, bias2, wout_bf)


ROUTE_E0, ROUTE_E1, ROUTE_R0, ROUTE_R1, ROUTE_G0, ROUTE_G1 = range(6)
ROUTE_ROWS = 8
N_LOGITS = N_EXPERT_GROUPS + N_EXPERTS
LOGIT_ROWS = -(-N_LOGITS // SUBLANES) * SUBLANES


def _router_kernel(x_ref, g_ref, wr_ref, route_ref, cnt_ref, carry_ref, *, tm):
    i = pl.program_id(0)

    @pl.when(i == 0)
    def _():
        carry_ref[...] = jnp.zeros(carry_ref.shape, F32)

    h = _rms(x_ref[...], g_ref[...])
    nt = (((1,), (1,)), ((), ()))
    logits = lax.dot_general(wr_ref[...], h.astype(BF16), nt, preferred_element_type=F32)[:LOGIT_ROWS]
    row = lax.broadcasted_iota(I32, (LOGIT_ROWS, tm), 0).astype(F32)
    ninf = -jnp.inf
    big = float(LANES)
    cmax = lambda a: jnp.max(a, axis=0, keepdims=True)
    cmin = lambda a: jnp.min(a, axis=0, keepdims=True)
    csum = lambda a: jnp.sum(a, axis=0, keepdims=True)

    is_grp = row < N_EXPERT_GROUPS
    lg = jnp.where(is_grp, logits, ninf)
    mg = cmax(lg)
    g_sel = cmin(jnp.where(lg == mg, row, big))
    p_grp = 1.0 / csum(jnp.where(is_grp, jnp.exp(lg - mg), 0.0))
    lo = N_EXPERT_GROUPS + g_sel * EXPERTS_PER_GROUP
    le = jnp.where((row >= lo) & (row < lo + EXPERTS_PER_GROUP), logits, ninf)
    v1 = cmax(le)
    j1 = cmin(jnp.where(le == v1, row, big))
    le2 = jnp.where(row == j1, ninf, le)
    v2 = cmax(le2)
    j2 = cmin(jnp.where(le2 == v2, row, big))
    e0 = j1 - N_EXPERT_GROUPS
    e1 = j2 - N_EXPERT_GROUPS
    r = jnp.exp(v2 - v1)
    g0 = p_grp / (1.0 + r)
    g1 = p_grp * r / (1.0 + r)

    erow = lax.broadcasted_iota(I32, (N_EXPERTS, tm), 0).astype(F32)
    oh0 = jnp.where(erow == e0, 1.0, 0.0)
    oh1 = jnp.where(erow == e1, 1.0, 0.0)
    oh = oh0 + oh1
    earlier = jnp.where(lax.broadcasted_iota(I32, (tm, tm), 0) < lax.broadcasted_iota(I32, (tm, tm), 1),
                        1.0, 0.0).astype(BF16)
    before = jnp.dot(oh.astype(BF16), earlier, preferred_element_type=F32) + carry_ref[...]
    r0 = csum(oh0 * before)
    r1 = csum(oh1 * before)
    carry_ref[...] = carry_ref[...] + jnp.sum(oh, axis=1, keepdims=True)
    cnt_ref[...] = jnp.broadcast_to(carry_ref[...], cnt_ref.shape)

    rows = {ROUTE_E0: e0, ROUTE_E1: e1, ROUTE_R0: r0, ROUTE_R1: r1, ROUTE_G0: g0, ROUTE_G1: g1}
    zero = jnp.zeros((1, tm), F32)
    route_ref[...] = jnp.concatenate([rows.get(k, zero) for k in range(ROUTE_ROWS)], axis=0)


def _router(x, g, wr_t, *, tm):
    N, D = x.shape
    const = lambda s: pl.BlockSpec(s, lambda i: (0, 0))
    return pl.pallas_call(
        functools.partial(_router_kernel, tm=tm),
        out_shape=(jax.ShapeDtypeStruct((ROUTE_ROWS, N), F32), jax.ShapeDtypeStruct((N_EXPERTS, LANES), F32)),
        grid=(N // tm,), in_specs=[pl.BlockSpec((tm, D), lambda i: (i, 0)), const((1, D)), const((LANES, D))],
        out_specs=(pl.BlockSpec((ROUTE_ROWS, tm), lambda i: (0, i)), const((N_EXPERTS, LANES))),
        scratch_shapes=[pltpu.VMEM((N_EXPERTS, 1), F32)],
        compiler_params=pltpu.CompilerParams(dimension_semantics=("arbitrary",), vmem_limit_bytes=VMEM_LIMIT),
        name="router",
    )(x, g, wr_t)


def _pack_bf16_pairs(h):
    half = h.shape[1] // 2
    lo = lax.bitcast_convert_type(h[:, :half].astype(BF16).astype(F32), U32)
    hi = lax.bitcast_convert_type(h[:, half:].astype(BF16).astype(F32), U32)
    return (lo >> 16) | (hi & jnp.uint32(0xFFFF0000))


def _unpack_bf16_pairs(w):
    lo = lax.bitcast_convert_type(w << 16, F32).astype(BF16)
    hi = lax.bitcast_convert_type(w & jnp.uint32(0xFFFF0000), F32).astype(BF16)
    return jnp.concatenate([lo, hi], axis=1)


def _dispatch_kernel(d0_ref, d1_ref, pad_ref, x_ref, g_ref, rows_ref, h_scr, zblk, sems, *, tm):
    i = pl.program_id(0)
    n = pl.num_programs(0)
    slot = i % 2
    groups = tm // SUBLANES

    def wait_slot(s):
        for _ in range(2 * groups):
            pltpu.make_async_copy(h_scr.at[s, 0], rows_ref.at[pl.ds(0, SUBLANES)], sems.at[s]).wait()

    @pl.when(i == 0)
    def _():
        zblk[...] = jnp.zeros(zblk.shape, U32)
        n_blocks = rows_ref.shape[0] // MOE_BLOCK
        n_used = pad_ref[2, 0]

        def blk_copy(row0):
            return pltpu.make_async_copy(zblk, rows_ref.at[pl.ds(row0, MOE_BLOCK)], sems.at[2])

        for wait in (False, True):
            for e in range(N_EXPERTS):
                @pl.when(pad_ref[1, e] > 0)
                def _():
                    cp = blk_copy(pl.multiple_of(pad_ref[0, e], MOE_BLOCK))
                    cp.wait() if wait else cp.start()

            def unused(b, carry):
                cp = blk_copy(pl.multiple_of(b * MOE_BLOCK, MOE_BLOCK))
                cp.wait() if wait else cp.start()
                return carry
            lax.fori_loop(n_used, n_blocks, unused, 0)

    @pl.when(i >= 2)
    def _():
        wait_slot(slot)

    h_scr[slot] = _pack_bf16_pairs(_rms(x_ref[...], g_ref[...])).reshape(groups, SUBLANES, h_scr.shape[3])

    def body(k, carry):
        for u in range(SUBLANES):
            r = k * SUBLANES + u
            src = h_scr.at[slot, k, pl.ds(u, 1)]
            pltpu.make_async_copy(src, rows_ref.at[pl.ds(d0_ref[r], 1)], sems.at[slot]).start()
            pltpu.make_async_copy(src, rows_ref.at[pl.ds(d1_ref[r], 1)], sems.at[slot]).start()
        return carry

    lax.fori_loop(0, groups, body, 0)

    @pl.when(i == n - 1)
    def _():
        wait_slot(slot)

    @pl.when(jnp.logical_and(i == n - 1, n >= 2))
    def _():
        wait_slot(1 - slot)


def _dispatch(dest_flat, pad_info, x, g, *, tm, n_rows):
    N, D = x.shape
    nt = N // tm
    return pl.pallas_call(
        functools.partial(_dispatch_kernel, tm=tm),
        out_shape=jax.ShapeDtypeStruct((n_rows, D // 2), U32), grid=(nt,),
        in_specs=[pl.BlockSpec((tm,), lambda i: (i,), memory_space=pltpu.SMEM),
                  pl.BlockSpec((tm,), lambda i: (i + nt,), memory_space=pltpu.SMEM),
                  pl.BlockSpec(memory_space=pltpu.SMEM),
                  pl.BlockSpec((tm, D), lambda i: (i, 0)),
                  pl.BlockSpec((1, D), lambda i: (0, 0))],
        out_specs=pl.BlockSpec(memory_space=pl.ANY),
        scratch_shapes=[pltpu.VMEM((2, tm // SUBLANES, SUBLANES, D // 2), U32), pltpu.VMEM((MOE_BLOCK, D // 2), U32),
                        pltpu.SemaphoreType.DMA((3,))],
        compiler_params=pltpu.CompilerParams(dimension_semantics=("arbitrary",), vmem_limit_bytes=VMEM_LIMIT),
        name="dispatch",
    )(dest_flat, dest_flat, pad_info, x, g)


def _expert_kernel(be_ref, nu_ref, x_ref, wg_ref, wu_ref, wd_ref, y_ref, wg_bf, wu_bf, wd_bf):
    i = pl.program_id(0)

    @pl.when(jnp.logical_or(i == 0, be_ref[i] != be_ref[jnp.maximum(i - 1, 0)]))
    def _():
        wg_bf[...] = wg_ref[0, 0].astype(BF16)
        wu_bf[...] = wu_ref[0, 0].astype(BF16)
        wd_bf[...] = wd_ref[0, 0].astype(BF16)

    @pl.when(i < nu_ref[0])
    def _():
        x = _unpack_bf16_pairs(x_ref[...])
        a = jnp.dot(x, wg_bf[...], preferred_element_type=F32)
        u = jnp.dot(x, wu_bf[...], preferred_element_type=F32)
        act = (a * jax.nn.sigmoid(a)) * u
        y_ref[...] = jnp.dot(act.astype(BF16), wd_bf[...], preferred_element_type=F32)

    @pl.when(i >= nu_ref[0])
    def _():
        y_ref[...] = jnp.zeros(y_ref.shape, F32)


def _experts(block_e, n_used, x_rows, w_gate, w_up, w_down, layer):
    n_rows = x_rows.shape[0]
    D, de = w_gate.shape[2:]
    n_blocks = n_rows // MOE_BLOCK
    grid_spec = pltpu.PrefetchScalarGridSpec(
        num_scalar_prefetch=2, grid=(n_blocks,),
        in_specs=[pl.BlockSpec((MOE_BLOCK, D // 2), lambda i, be, nu: (jnp.minimum(i, nu[0] - 1), 0)),
                  pl.BlockSpec((1, 1, D, de), lambda i, be, nu: (layer, be[i], 0, 0)),
                  pl.BlockSpec((1, 1, D, de), lambda i, be, nu: (layer, be[i], 0, 0)),
                  pl.BlockSpec((1, 1, de, D), lambda i, be, nu: (layer, be[i], 0, 0))],
        out_specs=pl.BlockSpec((MOE_BLOCK, D), lambda i, be, nu: (i, 0)),
        scratch_shapes=[pltpu.VMEM((D, de), BF16), pltpu.VMEM((D, de), BF16), pltpu.VMEM((de, D), BF16)])
    return pl.pallas_call(
        _expert_kernel, out_shape=jax.ShapeDtypeStruct((n_rows, D), F32), grid_spec=grid_spec,
        compiler_params=pltpu.CompilerParams(dimension_semantics=("arbitrary",), vmem_limit_bytes=VMEM_LIMIT),
        name="experts",
    )(block_e, n_used, x_rows, w_gate, w_up, w_down)


def _combine_kernel(dc0_ref, dc1_ref, dn0_ref, dn1_ref, x_ref, route_ref, g_ref, y_ref, *rest,
                    tm, final_norm, t_first):
    *o_refs, ybuf, sems = rest
    i = pl.program_id(0)
    n = pl.num_programs(0)
    slot = i % 2
    groups = tm // SUBLANES
    D = x_ref.shape[1]

    def gather(d_refs, s):
        def body(k, carry):
            for u in range(SUBLANES):
                r = k * SUBLANES + u
                for a in range(2):
                    pltpu.make_async_copy(y_ref.at[pl.ds(d_refs[a][r], 1)], ybuf.at[s, a, k, pl.ds(u, 1)],
                                          sems.at[s]).start()
            return carry
        lax.fori_loop(0, groups, body, 0)

    @pl.when(i == 0)
    def _():
        gather((dc0_ref, dc1_ref), 0)

    @pl.when(i + 1 < n)
    def _():
        gather((dn0_ref, dn1_ref), 1 - slot)

    for _ in range(2 * groups):
        pltpu.make_async_copy(y_ref.at[pl.ds(0, SUBLANES)], ybuf.at[slot, 0, 0], sems.at[slot]).wait()
    route = jnp.transpose(route_ref[...])
    g0 = route[:, ROUTE_G0:ROUTE_G0 + 1]
    g1 = route[:, ROUTE_G1:ROUTE_G1 + 1]
    out = x_ref[...] + (ybuf[slot, 0].reshape(tm, D) * g0 + ybuf[slot, 1].reshape(tm, D) * g1)
    if final_norm:
        out = _rms(out, g_ref[...])
    if t_first is None:
        o_refs[0][...] = out
    else:
        @pl.when(i < t_first)
        def _():
            o_refs[0][...] = out

        @pl.when(i >= t_first)
        def _():
            o_refs[1][...] = out


def _combine(dest_flat, x, route, g, y_rows, *, tm, final_norm, n_first=None):
    N, D = x.shape
    nt = N // tm
    row = lambda wdt: pl.BlockSpec((tm, wdt), lambda i: (i, 0))
    nxt = lambda i: jnp.minimum(i + 1, nt - 1)
    smem = lambda f: pl.BlockSpec((tm,), f, memory_space=pltpu.SMEM)
    if n_first is None:
        t_first, out_shape, out_specs = None, jax.ShapeDtypeStruct((N, D), F32), row(D)
    else:
        t_first = n_first // tm
        out_shape = (jax.ShapeDtypeStruct((n_first, D), F32), jax.ShapeDtypeStruct((N - n_first, D), F32))
        out_specs = (pl.BlockSpec((tm, D), lambda i: (jnp.minimum(i, t_first - 1), 0)),
                     pl.BlockSpec((tm, D), lambda i: (jnp.maximum(i - t_first, 0), 0)))
    return pl.pallas_call(
        functools.partial(_combine_kernel, tm=tm, final_norm=final_norm, t_first=t_first),
        out_shape=out_shape, grid=(nt,),
        in_specs=[smem(lambda i: (i,)), smem(lambda i: (i + nt,)),
                  smem(lambda i: (nxt(i),)), smem(lambda i: (nxt(i) + nt,)),
                  row(D), pl.BlockSpec((ROUTE_ROWS, tm), lambda i: (0, i)), pl.BlockSpec((1, D), lambda i: (0, 0)),
                  pl.BlockSpec(memory_space=pl.ANY)],
        out_specs=out_specs,
        scratch_shapes=[pltpu.VMEM((2, 2, tm // SUBLANES, SUBLANES, D), F32), pltpu.SemaphoreType.DMA((2,))],
        compiler_params=pltpu.CompilerParams(dimension_semantics=("arbitrary",), vmem_limit_bytes=VMEM_LIMIT),
        name="combine",
    )(dest_flat, dest_flat, dest_flat, dest_flat, x, route, g, y_rows)


def _hier_moe(x, g_ffn, w_rg, w_re, w_gate, w_up, w_down, layer, g_final, *, final_norm, n_first=None):
    N, D = x.shape
    tm = TOKEN_TILE
    wr_t = jnp.concatenate([w_rg.T, w_re.T, jnp.zeros((LANES - N_LOGITS, D), F32)], axis=0).astype(BF16)
    g_ffn = g_ffn.reshape(1, D)
    route, counts = _router(x, g_ffn, wr_t, tm=tm)
    counts = counts[:, 0].astype(I32)
    eid_t = route[ROUTE_E0:ROUTE_E1 + 1].astype(I32)
    rank_t = route[ROUTE_R0:ROUTE_R1 + 1].astype(I32)
    padded = (counts + MOE_BLOCK - 1) // MOE_BLOCK * MOE_BLOCK
    pad_end = jnp.cumsum(padded)
    pad_start = pad_end - padded
    expert_ids = jnp.arange(N_EXPERTS, dtype=I32)
    start_of = jnp.sum(jnp.where(eid_t[:, :, None] == expert_ids, pad_start, 0), axis=-1)
    dest_t = start_of + rank_t
    n_blocks = -(-(2 * N) // MOE_BLOCK) + N_EXPERTS
    n_rows = n_blocks * MOE_BLOCK
    n_used = (pad_end[-1] // MOE_BLOCK).astype(I32)
    block_start = jnp.minimum(jnp.arange(n_blocks, dtype=I32), n_used - 1) * MOE_BLOCK
    block_e = jnp.minimum(jnp.sum(block_start[:, None] >= pad_end[None, :], axis=1), N_EXPERTS - 1).astype(I32)
    pad_info = jnp.stack([pad_end - MOE_BLOCK, padded - counts, jnp.broadcast_to(n_used, (N_EXPERTS,))])
    dest_flat = dest_t.reshape(-1)
    x_rows = _dispatch(dest_flat, pad_info, x, g_ffn, tm=tm, n_rows=n_rows)
    y_rows = _experts(block_e, n_used.reshape(1), x_rows, w_gate, w_up, w_down, layer)
    return _combine(dest_flat, x, route, g_final.reshape(1, D), y_rows, tm=COMBINE_TILE, final_norm=final_norm,
                    n_first=n_first)


def _rope_tables(pos, rot_lanes):
    inv = 1.0 / (ROPE_THETA ** (jnp.arange(ROT_HALF, dtype=F32) / ROT_HALF))
    ang = pos.astype(F32)[:, None] * inv[None, :]
    cos, sin = jnp.cos(ang), jnp.sin(ang)
    lane = jnp.arange(LANES)
    r = lane % HEAD_DIM
    active = (lane < rot_lanes)
    first = active & (r < ROT_HALF)
    second = active & (r >= ROT_HALF) & (r < 2 * ROT_HALF)
    cos_l = cos[:, r % ROT_HALF]
    sin_l = sin[:, r % ROT_HALF]
    c = jnp.where((first | second)[None, :], cos_l, 1.0)
    sa = jnp.where(first[None, :], -sin_l, 0.0)
    sb = jnp.where(second[None, :], sin_l, 0.0)
    return jnp.stack([c, sa, sb]).astype(F32)


def _prep_w_in(w_in):
    D = w_in.shape[0]
    assert w_in.shape[1] == sum(IN_SIZES)
    pad = jnp.zeros((D, IN_WIDTH_PAD - w_in.shape[1]), w_in.dtype)
    return jnp.concatenate([w_in, pad], axis=1).astype(BF16)


def kernel(x_prompt, x_sample, cache_k, cache_v, cache_idx_k, state_pool, norm_mix, norm_ffn, norm_final,
           par_w_in, par_pool_w, par_pool_scale, par_w_out, gm_w_in, gm_ln_g, gm_ln_b, gm_ws, gm_bs, gm_w_out,
           moe_router_group, moe_router_expert, moe_w_gate, moe_w_up, moe_w_down):
    Bp, Tp, D = x_prompt.shape
    Bs, Ts, _ = x_sample.shape
    past = cache_k.shape[2]
    Np, Ns = Bp * Tp, Bs * Ts
    depth = norm_mix.shape[0]
    assert depth == 2 and Ts == CHUNK and Tp % PROJ_TILE == 0 and Tp % QUERY_TILE == 0 and Tp % TOKEN_TILE == 0 and Ns % TOKEN_TILE == 0

    w_in_bf = _prep_w_in(par_w_in[0])
    pw_bf = par_pool_w[0].astype(BF16)
    ps = par_pool_scale[0].reshape(1, POOL_WIDTH)
    wo_bf = par_w_out[0].astype(BF16)
    g_mix0 = norm_mix[0].reshape(1, D)
    pos_p = jnp.arange(Tp, dtype=I32)
    pos_s = past + jnp.arange(Ts, dtype=I32)

    hist_p = jnp.zeros((Bp, HIST_ROWS, POOL_WIDTH), F32)
    hist_s = jnp.pad(state_pool[0], ((0, 0), (1, 0), (0, 0)))
    tk = KEY_BLOCK
    (q_p, qi_p, k_p, v_p, kiwi_p, kbf_p, vbf_p, kibf_p, yp_p, st_p) = _inproj(
        x_prompt, g_mix0, w_in_bf, _rope_tables(pos_p, LANES), _rope_tables(pos_p, IDX_DIM), hist_p, pw_bf, ps,
        tm=PROJ_TILE, pos0=0)
    (q_s, qi_s, k_s, v_s, kiwi_s, kbf_s, vbf_s, kibf_s, yp_s, st_s) = _inproj(
        x_sample, g_mix0, w_in_bf, _rope_tables(pos_s, LANES), _rope_tables(pos_s, IDX_DIM), hist_s, pw_bf, ps,
        tm=Ts, pos0=past)

    ls = past + Ts
    n_blk = -(-ls // tk)
    tk_s = -(-ls // (n_blk * LANES)) * LANES
    lsp = n_blk * tk_s
    kpad = lambda a, ax: jnp.pad(a, [(0, lsp - ls) if d == ax else (0, 0) for d in range(a.ndim)])
    kall = kpad(jnp.concatenate([cache_k[0].reshape(Bs, past, KV_WIDTH).astype(BF16), kbf_s], axis=1), 1)
    vall = kpad(jnp.concatenate([cache_v[0].reshape(Bs, past, KV_WIDTH).astype(BF16), vbf_s], axis=1), 1)
    vall_t = jnp.transpose(vall.reshape(Bs, lsp // LANES, LANES, KV_WIDTH), (0, 1, 3, 2))
    kiall = kpad(jnp.concatenate([cache_idx_k[0].astype(BF16), kibf_s], axis=1), 1)
    qpad = lambda a, ax: jnp.pad(a, [(0, LANES - Ts) if d == ax else (0, 0) for d in range(a.ndim)])
    x1_s = _dsa(qpad(q_s, 2), qpad(qi_s, 2), qpad(kiwi_s, 1), kall, vall_t, kiall, qpad(x_sample, 1), qpad(yp_s, 1),
                wo_bf, tq=LANES, tk=tk_s, pos0=past, n_keys=ls, n_sel=min(TOPK_MAX, ls // 4))[:, :Ts]
    assert Tp % tk == 0
    x = jnp.pad(x1_s.reshape(Ns, D), ((Np, 0), (0, 0)))
    x = _dsa(q_p, qi_p, kiwi_p, kbf_p, vbf_p, kibf_p, x_prompt, yp_p, wo_bf,
             tq=QUERY_TILE, tk=tk, pos0=0, n_keys=Tp, n_sel=min(TOPK_MAX, Tp // 4), out_base=x)
    x = _hier_moe(x, norm_ffn[0], moe_router_group[0], moe_router_expert[0],
                  moe_w_gate, moe_w_up, moe_w_down, 0, norm_final, final_norm=False)

    cs = Ts
    tril = lambda n: jnp.tril(jnp.ones((n, n), bool))
    ws_p = jnp.where(tril(GM_CHUNK)[None], gm_ws[0], 0.0)
    ws_small = jnp.where(tril(cs)[None], gm_ws[0][:, :cs, :cs], 0.0)
    rep = GM_CHUNK // cs
    ws_s = jnp.einsum('ab,gts->gatbs', jnp.eye(rep, dtype=F32), ws_small).reshape(GM_GROUPS, GM_CHUNK, GM_CHUNK)
    ws2 = jnp.stack([ws_p, ws_s]).astype(BF16)
    gd = D // GM_GROUPS
    bias_p = jnp.repeat(jnp.transpose(gm_bs[0]), gd, axis=1)
    bias_s = jnp.tile(jnp.repeat(jnp.transpose(gm_bs[0][:, :cs]), gd, axis=1), (rep, 1))
    bias2 = jnp.stack([bias_p, bias_s])
    x, gm_v = _gmlp(x, norm_mix[1].reshape(1, D), gm_w_in[0].astype(BF16), gm_ln_g[0].reshape(1, D),
                    gm_ln_b[0].reshape(1, D), ws2, bias2, gm_w_out[0].astype(BF16),
                    tm=TOKEN_TILE, n_first=Np, n_v_rows=Ns)
    y_p, y_s = _hier_moe(x, norm_ffn[1], moe_router_group[1], moe_router_expert[1],
                         moe_w_gate, moe_w_up, moe_w_down, 1, norm_final, final_norm=True, n_first=Np)
    y_p = y_p.reshape(Bp, Tp, D)
    y_s = y_s.reshape(Bs, Ts, D)
    r4 = lambda a, b, t: a.reshape(1, b, t, N_KV_HEADS, HEAD_DIM)
    return (y_p, y_s,
            r4(k_p, Bp, Tp), r4(v_p, Bp, Tp), kiwi_p[:, :, :IDX_DIM][None], st_p[:, 1:][None],
            r4(k_s, Bs, Ts), r4(v_s, Bs, Ts), kiwi_s[:, :, :IDX_DIM][None], st_s[:, 1:][None],
            gm_v.reshape(1, Bs, Ts, D))
```

```python
import functools

import jax
import jax.numpy as jnp
from jax import lax
from jax.experimental import pallas as pl
from jax.experimental.pallas import tpu as pltpu

F32 = jnp.float32
BF16 = jnp.bfloat16
I32 = jnp.int32
U32 = jnp.uint32

LANES = 128
SUBLANES = 8
CHUNK = 64
POOL_WINDOWS = (2, 4, 8, 16)
POOL_GROUP_DIM = 128
POOL_WIDTH = 512
HIST_ROWS = 16
N_HEADS = 8
HEAD_DIM = 64
N_KV_HEADS = 4
Q_PER_KV = N_HEADS // N_KV_HEADS
ATT_WIDTH = N_HEADS * HEAD_DIM
KV_WIDTH = N_KV_HEADS * HEAD_DIM
N_IDX_HEADS = 8
IDX_DIM = 64
TOPK_MAX = 256
ROPE_THETA = 500000.0
ROT_HALF = HEAD_DIM // 8
GM_CHUNK = 128
GM_GROUPS = 8
N_EXPERT_GROUPS = 4
EXPERTS_PER_GROUP = 8
N_EXPERTS = 32
MOE_BLOCK = 512
RMS_EPS = 1e-6
LN_EPS = 1e-5

INT_MIN = -2147483648
LOG2_E = 1.4426950408889634
NEG_BIG = -1e30
VMEM_LIMIT = 48 * 1024 * 1024

TOKEN_TILE = 1024
COMBINE_TILE = 512
assert TOKEN_TILE % COMBINE_TILE == 0
PROJ_TILE = 1024
KEY_BLOCK = 512
QUERY_TILE = 256
COUNT_SLAB = 64

IN_SIZES = (POOL_WIDTH, ATT_WIDTH, KV_WIDTH, KV_WIDTH, N_IDX_HEADS * IDX_DIM, IDX_DIM, N_IDX_HEADS)
COL_XP, COL_Q, COL_K, COL_V, COL_QI, COL_KIWI = (sum(IN_SIZES[:i]) for i in range(6))
IN_WIDTH_PAD = COL_KIWI + LANES


def _rms(x, g):
    return x * lax.rsqrt(jnp.mean(x * x, axis=-1, keepdims=True) + RMS_EPS) * g


def _rope128(x, c, sa, sb):
    return x * c + pltpu.roll(x, LANES - ROT_HALF, 1) * sa + pltpu.roll(x, ROT_HALF, 1) * sb


def _inproj_kernel(x_ref, g_ref, w_ref, rope_ref, ropeki_ref, hist_ref, pw_ref, ps_ref,
                   q_ref, qi_ref, k_ref, v_ref, kiwi_ref, kbf_ref, vbf_ref, kibf_ref, yp_ref, state_ref,
                   buf_ref, *, tm, pos0):
    j = pl.program_id(1)
    h = _rms(x_ref[0], g_ref[...])
    proj = jnp.dot(h.astype(BF16), w_ref[...], preferred_element_type=F32)
    c, sa, sb = rope_ref[0], rope_ref[1], rope_ref[2]

    def put_heads(ref, i, chunk):
        ref[0, 2 * i] = chunk[:, :HEAD_DIM].astype(BF16)
        ref[0, 2 * i + 1] = chunk[:, HEAD_DIM:].astype(BF16)

    for i in range(ATT_WIDTH // LANES):
        put_heads(q_ref, i, _rope128(proj[:, COL_Q + i * LANES:COL_Q + (i + 1) * LANES], c, sa, sb)
                  * (HEAD_DIM ** -0.5 * LOG2_E))
        put_heads(qi_ref, i, _rope128(proj[:, COL_QI + i * LANES:COL_QI + (i + 1) * LANES], c, sa, sb))
    for i in range(KV_WIDTH // LANES):
        sl = slice(i * LANES, (i + 1) * LANES)
        kr = _rope128(proj[:, COL_K + i * LANES:COL_K + (i + 1) * LANES], c, sa, sb)
        k_ref[0, :, sl] = kr
        kbf_ref[0, :, sl] = kr.astype(BF16)
    vv = proj[:, COL_V:COL_V + KV_WIDTH]
    v_ref[0] = vv
    if tm % LANES == 0:
        for cc in range(tm // LANES):
            vbf_ref[0, cc] = jnp.transpose(vv[cc * LANES:(cc + 1) * LANES, :]).astype(BF16)
    else:
        vbf_ref[0] = vv.astype(BF16)
    kiwi = _rope128(proj[:, COL_KIWI:COL_KIWI + LANES], ropeki_ref[0], ropeki_ref[1], ropeki_ref[2])
    kiwi_ref[0] = kiwi
    kibf_ref[0] = kiwi[:, :IDX_DIM].astype(BF16)

    @pl.when(j == 0)
    def _():
        buf_ref[0:HIST_ROWS, :] = hist_ref[0]

    xp = proj[:, COL_XP:COL_XP + POOL_WIDTH]
    buf_ref[HIST_ROWS:HIST_ROWS + tm, :] = xp
    pos = pos0 + j * tm + lax.broadcasted_iota(I32, (tm, 1), 0)
    for gi, w in enumerate(POOL_WINDOWS):
        c0 = gi * POOL_GROUP_DIM
        s = xp[:, c0:c0 + POOL_GROUP_DIM]
        for i in range(1, w):
            s = s + buf_ref[HIST_ROWS - i:HIST_ROWS - i + tm, c0:c0 + POOL_GROUP_DIM]
        cnt = jnp.minimum(pos + 1, w).astype(F32)
        d = s / cnt - xp[:, c0:c0 + POOL_GROUP_DIM]
        y = jnp.dot(d.astype(BF16), pw_ref[gi], preferred_element_type=F32)
        yp_ref[0, :, c0:c0 + POOL_GROUP_DIM] = (y * ps_ref[:, c0:c0 + POOL_GROUP_DIM]).astype(BF16)
    tail = buf_ref[tm:tm + HIST_ROWS, :]
    state_ref[0] = tail
    buf_ref[0:HIST_ROWS, :] = tail


def _inproj(x, g, w_bf, rope, ropeki, hist, pw_bf, ps, *, tm, pos0):
    B, T, D = x.shape
    nt = T // tm
    f = lambda shape, dt: jax.ShapeDtypeStruct(shape, dt)
    v_t = tm % LANES == 0
    out_shape = (
        f((B, N_HEADS, T, HEAD_DIM), BF16), f((B, N_IDX_HEADS, T, IDX_DIM), BF16),
        f((B, T, KV_WIDTH), F32), f((B, T, KV_WIDTH), F32), f((B, T, LANES), F32),
        f((B, T, KV_WIDTH), BF16),
        f((B, T // LANES, KV_WIDTH, LANES) if v_t else (B, T, KV_WIDTH), BF16),
        f((B, T, IDX_DIM), BF16),
        f((B, T, POOL_WIDTH), BF16), f((B, HIST_ROWS, POOL_WIDTH), F32),
    )
    tile = lambda wdt: pl.BlockSpec((1, tm, wdt), lambda b, j: (b, j, 0))
    heads = lambda n, wdt: pl.BlockSpec((1, n, tm, wdt), lambda b, j: (b, 0, j, 0))
    const2 = lambda s: pl.BlockSpec(s, lambda b, j: (0, 0))
    in_specs = [
        tile(D), const2((1, D)), const2((D, IN_WIDTH_PAD)),
        pl.BlockSpec((3, tm, LANES), lambda b, j: (0, j, 0)),
        pl.BlockSpec((3, tm, LANES), lambda b, j: (0, j, 0)),
        pl.BlockSpec((1, HIST_ROWS, POOL_WIDTH), lambda b, j: (b, 0, 0)),
        pl.BlockSpec((len(POOL_WINDOWS), POOL_GROUP_DIM, POOL_GROUP_DIM), lambda b, j: (0, 0, 0)),
        const2((1, POOL_WIDTH)),
    ]
    out_specs = (
        heads(N_HEADS, HEAD_DIM), heads(N_IDX_HEADS, IDX_DIM), tile(KV_WIDTH), tile(KV_WIDTH), tile(LANES),
        tile(KV_WIDTH),
        pl.BlockSpec((1, tm // LANES, KV_WIDTH, LANES), lambda b, j: (b, j, 0, 0)) if v_t else tile(KV_WIDTH),
        tile(IDX_DIM), tile(POOL_WIDTH),
        pl.BlockSpec((1, HIST_ROWS, POOL_WIDTH), lambda b, j: (b, 0, 0)),
    )
    return pl.pallas_call(
        functools.partial(_inproj_kernel, tm=tm, pos0=pos0),
        out_shape=out_shape, grid=(B, nt), in_specs=in_specs, out_specs=out_specs,
        scratch_shapes=[pltpu.VMEM((HIST_ROWS + tm, POOL_WIDTH), F32)],
        compiler_params=pltpu.CompilerParams(dimension_semantics=("parallel", "arbitrary"),
                                             vmem_limit_bytes=VMEM_LIMIT),
        name="inproj",
    )(x, g, w_bf, rope, ropeki, hist, pw_bf, ps)


def _dsa_kernel(q_ref, qi_ref, kiwi_ref, k_ref, vt_ref, ki_ref, x_ref, yp_ref, wo_ref, *rest,
                tq, tk, pos0, n_keys, n_sel):
    o_ref, key_buf, bias_buf, m_scr, l_scr, acc_scr, s_scr = rest[-7:]
    slab = COUNT_SLAB
    j = pl.program_id(1)
    base = pos0 + j * tq
    pos = base + lax.broadcasted_iota(I32, (1, tq), 1)
    limit = jnp.minimum((pos // CHUNK + 1) * CHUNK, n_keys)
    limit_max = jnp.minimum(((base + tq - 1) // CHUNK + 1) * CHUNK, n_keys)
    nkb = (limit_max + tk - 1) // tk
    nt = (((1,), (1,)), ((), ()))

    wi_t = jnp.transpose(kiwi_ref[0])[IDX_DIM:IDX_DIM + N_IDX_HEADS, :]

    def score_pair(i, carry):
        blocks = (2 * i, jnp.minimum(2 * i + 1, nkb - 1))
        scores = []
        for kb in blocks:
            off = pl.multiple_of(kb * tk, tk)
            kiblk = ki_ref[0, pl.ds(off, tk), :]
            scores.append([lax.dot_general(kiblk, qi_ref[0, h], nt, preferred_element_type=F32)
                           for h in range(N_IDX_HEADS)])
        for kb, sc_h in zip(blocks, scores):
            idx = jnp.zeros((tk, tq), F32)
            for h in range(N_IDX_HEADS):
                idx = idx + jnp.maximum(sc_h[h], 0.0) * wi_t[h:h + 1, :]
            idx = jnp.where(idx == 0.0, 0.0, idx)
            bits = lax.bitcast_convert_type(idx, I32)
            key = bits ^ ((bits >> 31) & 0x7FFFFFFF)
            key_buf[kb] = jnp.where(lax.broadcasted_iota(I32, (tk, tq), 0) < limit - kb * tk, key, INT_MIN)
        return carry

    lax.fori_loop(0, (nkb + 1) // 2, score_pair, 0)

    def col_sum(a):
        return jnp.sum(a, axis=0, keepdims=True)

    def count_ge(cand):
        def body(kb, acc):
            kblk = key_buf[kb]
            for c in range(tk // slab):
                acc = acc + jnp.where(kblk[c * slab:(c + 1) * slab] >= cand, 1.0, 0.0)
            return acc
        return col_sum(lax.fori_loop(0, nkb, body, jnp.zeros((slab, tq), F32)))

    kf = float(n_sel)
    cnt0 = count_ge(jnp.zeros((1, tq), I32))
    t0 = jnp.where(cnt0 >= kf, 0, INT_MIN).astype(I32)

    def bit_body(i, carry):
        t, cnt = carry
        cand = t | lax.shift_left(jnp.int32(1), 30 - i)
        cnt_cand = count_ge(cand)
        keep = cnt_cand >= kf
        return jnp.where(keep, cand, t), jnp.where(keep, cnt_cand, cnt)

    t, cnt_ge = lax.fori_loop(0, 31, bit_body, (t0, cnt0))
    cnt_gt = count_ge(t + 1)
    need = kf - cnt_gt
    cnt_eq = cnt_ge - cnt_gt
    overfull = jnp.where(t != INT_MIN, cnt_eq - need, 0.0)
    slow = jnp.max(overfull) > 0.0

    @pl.when(jnp.logical_not(slow))
    def _():
        t_adm = jnp.maximum(t, INT_MIN + 1)

        def body(kb, carry):
            bias_buf[kb] = jnp.where(key_buf[kb] >= t_adm, 0.0, NEG_BIG)
            return carry
        lax.fori_loop(0, nkb, body, 0)

    @pl.when(slow)
    def _():
        tri = jnp.where(lax.broadcasted_iota(I32, (tk, tk), 1) <= lax.broadcasted_iota(I32, (tk, tk), 0),
                        1.0, 0.0).astype(BF16)

        def body(kb, seen):
            kblk = key_buf[kb]
            eq = jnp.where((kblk == t) & (kblk != INT_MIN), 1.0, 0.0)
            prefix = jnp.dot(tri, eq.astype(BF16), preferred_element_type=F32) + seen
            keep_tie = jnp.where(prefix <= need, eq, 0.0)
            sel = jnp.where(kblk > t, 1.0, keep_tie)
            bias_buf[kb] = jnp.where(sel > 0.0, 0.0, NEG_BIG)
            return seen + col_sum(eq)
        lax.fori_loop(0, nkb, body, jnp.zeros((1, tq), F32))

    sub = LANES
    nsub = tk // sub
    tl = LANES
    n_tiles = tq // tl
    m_scr[...] = jnp.full(m_scr.shape, NEG_BIG, F32)
    l_scr[...] = jnp.zeros(l_scr.shape, F32)
    acc_scr[...] = jnp.zeros(acc_scr.shape, F32)

    def tile_blocks(hq):
        return (jnp.minimum(((base + (hq + 1) * tl - 1) // CHUNK + 1) * CHUNK, n_keys) + tk - 1) // tk

    def attn_block(kb, members):
        for mi, hq in enumerate(members):
            for c in range(nsub):
                off = pl.multiple_of(kb * tk + c * sub, sub)
                for g in range(N_KV_HEADS):
                    s_scr[mi, c, g] = lax.dot_general(
                        k_ref[0, pl.ds(off, sub), g * HEAD_DIM:(g + 1) * HEAD_DIM],
                        q_ref[0, Q_PER_KV * g:Q_PER_KV * (g + 1), hq * tl:(hq + 1) * tl].reshape(Q_PER_KV * tl, HEAD_DIM),
                        nt, preferred_element_type=F32)
        for mi, hq in enumerate(members):
            m = [m_scr[hq, g] for g in range(N_KV_HEADS)]
            l = [l_scr[hq, g] for g in range(N_KV_HEADS)]
            for c in range(nsub):
                bias = bias_buf[kb, c * sub:(c + 1) * sub, hq * tl:(hq + 1) * tl]
                bias2 = jnp.concatenate([bias] * Q_PER_KV, axis=1)
                for g in range(N_KV_HEADS):
                    s = s_scr[mi, c, g] + bias2
                    m_new = jnp.maximum(m[g], jnp.max(s, axis=0, keepdims=True))
                    alpha = jnp.exp2(m[g] - m_new)
                    p = jnp.exp2(s - m_new)
                    l[g] = alpha * l[g] + col_sum(p)
                    pv = jnp.dot(vt_ref[0, kb * nsub + c, g * HEAD_DIM:(g + 1) * HEAD_DIM, :], p.astype(BF16),
                                 preferred_element_type=F32)
                    acc_scr[hq, g] = alpha * acc_scr[hq, g] + pv
                    m[g] = m_new
            for g in range(N_KV_HEADS):
                m_scr[hq, g] = m[g]
                l_scr[hq, g] = l[g]

    for first in range(0, n_tiles, 2):
        members = tuple(range(first, min(first + 2, n_tiles)))
        shared = tile_blocks(members[0])
        lax.fori_loop(0, shared, lambda kb, c, ms=members: (attn_block(kb, ms), c)[1], 0)
        if len(members) == 2:
            lax.fori_loop(shared, tile_blocks(members[1]),
                          lambda kb, c, ms=members[1:]: (attn_block(kb, ms), c)[1], 0)

    y_att = []
    for hq in range(n_tiles):
        o_t = []
        for g in range(N_KV_HEADS):
            og = acc_scr[hq, g] / l_scr[hq, g]
            o_t.extend(og[:, hh * tl:(hh + 1) * tl] for hh in range(Q_PER_KV))
        y_att.append(jnp.transpose(jnp.concatenate(o_t, axis=0)).astype(BF16))
    y_att = jnp.concatenate(y_att, axis=0)

    y = jnp.dot(yp_ref[0], wo_ref[0:POOL_WIDTH, :], preferred_element_type=F32)
    y = y + jnp.dot(y_att, wo_ref[POOL_WIDTH:POOL_WIDTH + ATT_WIDTH, :], preferred_element_type=F32)
    o_ref[...] = (x_ref[0] + y).reshape(o_ref.shape)


def _dsa(q_hm, qi_hm, kiwi, k_all, v_t, ki_all, x, yp, wo_bf, *, tq, tk, pos0, n_keys, n_sel, out_base=None):
    B, T, D = x.shape
    L = k_all.shape[1]
    assert L % tk == 0 and T % tq == 0 and tq % LANES == 0
    nkb_max = L // tk
    tile = lambda wdt: pl.BlockSpec((1, tq, wdt), lambda b, j: (b, j, 0))
    in_specs = [pl.BlockSpec((1, N_HEADS, tq, HEAD_DIM), lambda b, j: (b, 0, j, 0)),
                pl.BlockSpec((1, N_IDX_HEADS, tq, IDX_DIM), lambda b, j: (b, 0, j, 0)),
                tile(LANES),
                pl.BlockSpec((1, L, KV_WIDTH), lambda b, j: (b, 0, 0)),
                pl.BlockSpec((1, L // LANES, KV_WIDTH, LANES), lambda b, j: (b, 0, 0, 0)),
                pl.BlockSpec((1, L, IDX_DIM), lambda b, j: (b, 0, 0)),
                tile(D), tile(POOL_WIDTH), pl.BlockSpec((POOL_WIDTH + ATT_WIDTH, D), lambda b, j: (0, 0))]
    nq = T // tq
    args = (q_hm, qi_hm, kiwi, k_all, v_t, ki_all, x, yp, wo_bf)
    if out_base is None:
        out_shape, out_spec, aliases = jax.ShapeDtypeStruct((B, T, D), F32), tile(D), {}
    else:
        out_shape = jax.ShapeDtypeStruct(out_base.shape, F32)
        out_spec = pl.BlockSpec((tq, D), lambda b, j: (b * nq + j, 0))
        in_specs, args, aliases = in_specs + [pl.BlockSpec(memory_space=pl.ANY)], args + (out_base,), {len(args): 0}
    return pl.pallas_call(
        functools.partial(_dsa_kernel, tq=tq, tk=tk, pos0=pos0, n_keys=n_keys, n_sel=n_sel),
        out_shape=out_shape, grid=(B, nq), in_specs=in_specs, out_specs=out_spec, input_output_aliases=aliases,
        scratch_shapes=[pltpu.VMEM((nkb_max, tk, tq), I32), pltpu.VMEM((nkb_max, tk, tq), F32),
                        pltpu.VMEM((tq // LANES, N_KV_HEADS, 1, Q_PER_KV * LANES), F32),
                        pltpu.VMEM((tq // LANES, N_KV_HEADS, 1, Q_PER_KV * LANES), F32),
                        pltpu.VMEM((tq // LANES, N_KV_HEADS, HEAD_DIM, Q_PER_KV * LANES), F32),
                        pltpu.VMEM((2, tk // LANES, N_KV_HEADS, LANES, Q_PER_KV * LANES), F32)],
        compiler_params=pltpu.CompilerParams(dimension_semantics=("parallel", "arbitrary"),
                                             vmem_limit_bytes=VMEM_LIMIT),
        name="dsa",
    )(*args)


def _gmlp_kernel(x_ref, g_ref, win_ref, lng_ref, lnb_ref, ws_ref, bias_ref, wout_ref, o_ref, v_ref, *, tm):
    x = x_ref[...]
    h = _rms(x, g_ref[...])
    z = jax.nn.gelu(jnp.dot(h.astype(BF16), win_ref[...], preferred_element_type=F32))
    half = z.shape[1] // 2
    u, v = z[:, :half], z[:, half:]
    mu = jnp.mean(v, axis=-1, keepdims=True)
    var = jnp.mean(jnp.square(v - mu), axis=-1, keepdims=True)
    vn = (v - mu) * lax.rsqrt(var + LN_EPS) * lng_ref[...] + lnb_ref[...]
    v_ref[...] = vn
    gd = half // GM_GROUPS
    gated = []
    for c in range(tm // GM_CHUNK):
        rows = slice(c * GM_CHUNK, (c + 1) * GM_CHUNK)
        vc = vn[rows].astype(BF16)
        mixed = jnp.concatenate(
            [jnp.dot(ws_ref[0, g], vc[:, g * gd:(g + 1) * gd], preferred_element_type=F32)
             for g in range(GM_GROUPS)], axis=1) + bias_ref[0]
        gated.append((u[rows] * mixed).astype(BF16))
    gated = jnp.concatenate(gated, axis=0)
    o_ref[...] = x + jnp.dot(gated, wout_ref[...], preferred_element_type=F32)


def _gmlp(x, g, win_bf, lng, lnb, ws2, bias2, wout_bf, *, tm, n_first, n_v_rows):
    N, D = x.shape
    half = win_bf.shape[1] // 2
    nt = N // tm
    t_first = n_first // tm
    variant = lambda i: jnp.where(i >= t_first, 1, 0)
    row = pl.BlockSpec((tm, D), lambda i: (i, 0))
    const = lambda s: pl.BlockSpec(s, lambda i: (0, 0))
    in_specs = [row, const((1, D)), const((D, 2 * half)), const((1, half)), const((1, half)),
                pl.BlockSpec((1, GM_GROUPS, GM_CHUNK, GM_CHUNK), lambda i: (variant(i), 0, 0, 0)),
                pl.BlockSpec((1, GM_CHUNK, half), lambda i: (variant(i), 0, 0)),
                const((half, D))]
    out_specs = (row, pl.BlockSpec((tm, half), lambda i: (jnp.maximum(i - t_first, 0), 0)))
    return pl.pallas_call(
        functools.partial(_gmlp_kernel, tm=tm),
        out_shape=(jax.ShapeDtypeStruct((N, D), F32), jax.ShapeDtypeStruct((n_v_rows, half), F32)),
        grid=(nt,), in_specs=in_specs, out_specs=out_specs,
        compiler_params=pltpu.CompilerParams(dimension_semantics=("arbitrary",), vmem_limit_bytes=VMEM_LIMIT),
        name="gmlp",
    )(x, g, win_bf, lng, lnb, ws2, bias2, wout_bf)


ROUTE_E0, ROUTE_E1, ROUTE_R0, ROUTE_R1, ROUTE_G0, ROUTE_G1 = range(6)
ROUTE_ROWS = 8
N_LOGITS = N_EXPERT_GROUPS + N_EXPERTS
LOGIT_ROWS = -(-N_LOGITS // SUBLANES) * SUBLANES


def _router_kernel(x_ref, g_ref, wr_ref, route_ref, cnt_ref, carry_ref, *, tm):
    i = pl.program_id(0)

    @pl.when(i == 0)
    def _():
        carry_ref[...] = jnp.zeros(carry_ref.shape, F32)

    h = _rms(x_ref[...], g_ref[...])
    nt = (((1,), (1,)), ((), ()))
    logits = lax.dot_general(wr_ref[...], h.astype(BF16), nt, preferred_element_type=F32)[:LOGIT_ROWS]
    row = lax.broadcasted_iota(I32, (LOGIT_ROWS, tm), 0).astype(F32)
    ninf = -jnp.inf
    big = float(LANES)
    cmax = lambda a: jnp.max(a, axis=0, keepdims=True)
    cmin = lambda a: jnp.min(a, axis=0, keepdims=True)
    csum = lambda a: jnp.sum(a, axis=0, keepdims=True)

    is_grp = row < N_EXPERT_GROUPS
    lg = jnp.where(is_grp, logits, ninf)
    mg = cmax(lg)
    g_sel = cmin(jnp.where(lg == mg, row, big))
    p_grp = 1.0 / csum(jnp.where(is_grp, jnp.exp(lg - mg), 0.0))
    lo = N_EXPERT_GROUPS + g_sel * EXPERTS_PER_GROUP
    le = jnp.where((row >= lo) & (row < lo + EXPERTS_PER_GROUP), logits, ninf)
    v1 = cmax(le)
    j1 = cmin(jnp.where(le == v1, row, big))
    le2 = jnp.where(row == j1, ninf, le)
    v2 = cmax(le2)
    j2 = cmin(jnp.where(le2 == v2, row, big))
    e0 = j1 - N_EXPERT_GROUPS
    e1 = j2 - N_EXPERT_GROUPS
    r = jnp.exp(v2 - v1)
    g0 = p_grp / (1.0 + r)
    g1 = p_grp * r / (1.0 + r)

    erow = lax.broadcasted_iota(I32, (N_EXPERTS, tm), 0).astype(F32)
    oh0 = jnp.where(erow == e0, 1.0, 0.0)
    oh1 = jnp.where(erow == e1, 1.0, 0.0)
    oh = oh0 + oh1
    earlier = jnp.where(lax.broadcasted_iota(I32, (tm, tm), 0) < lax.broadcasted_iota(I32, (tm, tm), 1),
                        1.0, 0.0).astype(BF16)
    before = jnp.dot(oh.astype(BF16), earlier, preferred_element_type=F32) + carry_ref[...]
    r0 = csum(oh0 * before)
    r1 = csum(oh1 * before)
    carry_ref[...] = carry_ref[...] + jnp.sum(oh, axis=1, keepdims=True)
    cnt_ref[...] = jnp.broadcast_to(carry_ref[...], cnt_ref.shape)

    rows = {ROUTE_E0: e0, ROUTE_E1: e1, ROUTE_R0: r0, ROUTE_R1: r1, ROUTE_G0: g0, ROUTE_G1: g1}
    zero = jnp.zeros((1, tm), F32)
    route_ref[...] = jnp.concatenate([rows.get(k, zero) for k in range(ROUTE_ROWS)], axis=0)


def _router(x, g, wr_t, *, tm):
    N, D = x.shape
    const = lambda s: pl.BlockSpec(s, lambda i: (0, 0))
    return pl.pallas_call(
        functools.partial(_router_kernel, tm=tm),
        out_shape=(jax.ShapeDtypeStruct((ROUTE_ROWS, N), F32), jax.ShapeDtypeStruct((N_EXPERTS, LANES), F32)),
        grid=(N // tm,), in_specs=[pl.BlockSpec((tm, D), lambda i: (i, 0)), const((1, D)), const((LANES, D))],
        out_specs=(pl.BlockSpec((ROUTE_ROWS, tm), lambda i: (0, i)), const((N_EXPERTS, LANES))),
        scratch_shapes=[pltpu.VMEM((N_EXPERTS, 1), F32)],
        compiler_params=pltpu.CompilerParams(dimension_semantics=("arbitrary",), vmem_limit_bytes=VMEM_LIMIT),
        name="router",
    )(x, g, wr_t)


def _pack_bf16_pairs(h):
    half = h.shape[1] // 2
    lo = lax.bitcast_convert_type(h[:, :half].astype(BF16).astype(F32), U32)
    hi = lax.bitcast_convert_type(h[:, half:].astype(BF16).astype(F32), U32)
    return (lo >> 16) | (hi & jnp.uint32(0xFFFF0000))


def _unpack_bf16_pairs(w):
    lo = lax.bitcast_convert_type(w << 16, F32).astype(BF16)
    hi = lax.bitcast_convert_type(w & jnp.uint32(0xFFFF0000), F32).astype(BF16)
    return jnp.concatenate([lo, hi], axis=1)


def _dispatch_kernel(d0_ref, d1_ref, pad_ref, x_ref, g_ref, rows_ref, h_scr, zblk, sems, *, tm):
    i = pl.program_id(0)
    n = pl.num_programs(0)
    slot = i % 2
    groups = tm // SUBLANES

    def wait_slot(s):
        for _ in range(2 * groups):
            pltpu.make_async_copy(h_scr.at[s, 0], rows_ref.at[pl.ds(0, SUBLANES)], sems.at[s]).wait()

    @pl.when(i == 0)
    def _():
        zblk[...] = jnp.zeros(zblk.shape, U32)
        n_blocks = rows_ref.shape[0] // MOE_BLOCK
        n_used = pad_ref[2, 0]

        def blk_copy(row0):
            return pltpu.make_async_copy(zblk, rows_ref.at[pl.ds(row0, MOE_BLOCK)], sems.at[2])

        for wait in (False, True):
            for e in range(N_EXPERTS):
                @pl.when(pad_ref[1, e] > 0)
                def _():
                    cp = blk_copy(pl.multiple_of(pad_ref[0, e], MOE_BLOCK))
                    cp.wait() if wait else cp.start()

            def unused(b, carry):
                cp = blk_copy(pl.multiple_of(b * MOE_BLOCK, MOE_BLOCK))
                cp.wait() if wait else cp.start()
                return carry
            lax.fori_loop(n_used, n_blocks, unused, 0)

    @pl.when(i >= 2)
    def _():
        wait_slot(slot)

    h_scr[slot] = _pack_bf16_pairs(_rms(x_ref[...], g_ref[...])).reshape(groups, SUBLANES, h_scr.shape[3])

    def body(k, carry):
        for u in range(SUBLANES):
            r = k * SUBLANES + u
            src = h_scr.at[slot, k, pl.ds(u, 1)]
            pltpu.make_async_copy(src, rows_ref.at[pl.ds(d0_ref[r], 1)], sems.at[slot]).start()
            pltpu.make_async_copy(src, rows_ref.at[pl.ds(d1_ref[r], 1)], sems.at[slot]).start()
        return carry

    lax.fori_loop(0, groups, body, 0)

    @pl.when(i == n - 1)
    def _():
        wait_slot(slot)

    @pl.when(jnp.logical_and(i == n - 1, n >= 2))
    def _():
        wait_slot(1 - slot)


def _dispatch(dest_flat, pad_info, x, g, *, tm, n_rows):
    N, D = x.shape
    nt = N // tm
    return pl.pallas_call(
        functools.partial(_dispatch_kernel, tm=tm),
        out_shape=jax.ShapeDtypeStruct((n_rows, D // 2), U32), grid=(nt,),
        in_specs=[pl.BlockSpec((tm,), lambda i: (i,), memory_space=pltpu.SMEM),
                  pl.BlockSpec((tm,), lambda i: (i + nt,), memory_space=pltpu.SMEM),
                  pl.BlockSpec(memory_space=pltpu.SMEM),
                  pl.BlockSpec((tm, D), lambda i: (i, 0)),
                  pl.BlockSpec((1, D), lambda i: (0, 0))],
        out_specs=pl.BlockSpec(memory_space=pl.ANY),
        scratch_shapes=[pltpu.VMEM((2, tm // SUBLANES, SUBLANES, D // 2), U32), pltpu.VMEM((MOE_BLOCK, D // 2), U32),
                        pltpu.SemaphoreType.DMA((3,))],
        compiler_params=pltpu.CompilerParams(dimension_semantics=("arbitrary",), vmem_limit_bytes=VMEM_LIMIT),
        name="dispatch",
    )(dest_flat, dest_flat, pad_info, x, g)


def _expert_kernel(be_ref, nu_ref, x_ref, wg_ref, wu_ref, wd_ref, y_ref, wg_bf, wu_bf, wd_bf):
    i = pl.program_id(0)

    @pl.when(jnp.logical_or(i == 0, be_ref[i] != be_ref[jnp.maximum(i - 1, 0)]))
    def _():
        wg_bf[...] = wg_ref[0, 0].astype(BF16)
        wu_bf[...] = wu_ref[0, 0].astype(BF16)
        wd_bf[...] = wd_ref[0, 0].astype(BF16)

    @pl.when(i < nu_ref[0])
    def _():
        x = _unpack_bf16_pairs(x_ref[...])
        a = jnp.dot(x, wg_bf[...], preferred_element_type=F32)
        u = jnp.dot(x, wu_bf[...], preferred_element_type=F32)
        act = (a * jax.nn.sigmoid(a)) * u
        y_ref[...] = jnp.dot(act.astype(BF16), wd_bf[...], preferred_element_type=F32)

    @pl.when(i >= nu_ref[0])
    def _():
        y_ref[...] = jnp.zeros(y_ref.shape, F32)


def _experts(block_e, n_used, x_rows, w_gate, w_up, w_down, layer):
    n_rows = x_rows.shape[0]
    D, de = w_gate.shape[2:]
    n_blocks = n_rows // MOE_BLOCK
    grid_spec = pltpu.PrefetchScalarGridSpec(
        num_scalar_prefetch=2, grid=(n_blocks,),
        in_specs=[pl.BlockSpec((MOE_BLOCK, D // 2), lambda i, be, nu: (jnp.minimum(i, nu[0] - 1), 0)),
                  pl.BlockSpec((1, 1, D, de), lambda i, be, nu: (layer, be[i], 0, 0)),
                  pl.BlockSpec((1, 1, D, de), lambda i, be, nu: (layer, be[i], 0, 0)),
                  pl.BlockSpec((1, 1, de, D), lambda i, be, nu: (layer, be[i], 0, 0))],
        out_specs=pl.BlockSpec((MOE_BLOCK, D), lambda i, be, nu: (i, 0)),
        scratch_shapes=[pltpu.VMEM((D, de), BF16), pltpu.VMEM((D, de), BF16), pltpu.VMEM((de, D), BF16)])
    return pl.pallas_call(
        _expert_kernel, out_shape=jax.ShapeDtypeStruct((n_rows, D), F32), grid_spec=grid_spec,
        compiler_params=pltpu.CompilerParams(dimension_semantics=("arbitrary",), vmem_limit_bytes=VMEM_LIMIT),
        name="experts",
    )(block_e, n_used, x_rows, w_gate, w_up, w_down)


def _combine_kernel(dc0_ref, dc1_ref, dn0_ref, dn1_ref, x_ref, route_ref, g_ref, y_ref, *rest,
                    tm, final_norm, t_first):
    *o_refs, ybuf, sems = rest
    i = pl.program_id(0)
    n = pl.num_programs(0)
    slot = i % 2
    groups = tm // SUBLANES
    D = x_ref.shape[1]

    def gather(d_refs, s):
        def body(k, carry):
            for u in range(SUBLANES):
                r = k * SUBLANES + u
                for a in range(2):
                    pltpu.make_async_copy(y_ref.at[pl.ds(d_refs[a][r], 1)], ybuf.at[s, a, k, pl.ds(u, 1)],
                                          sems.at[s]).start()
            return carry
        lax.fori_loop(0, groups, body, 0)

    @pl.when(i == 0)
    def _():
        gather((dc0_ref, dc1_ref), 0)

    @pl.when(i + 1 < n)
    def _():
        gather((dn0_ref, dn1_ref), 1 - slot)

    for _ in range(2 * groups):
        pltpu.make_async_copy(y_ref.at[pl.ds(0, SUBLANES)], ybuf.at[slot, 0, 0], sems.at[slot]).wait()
    route = jnp.transpose(route_ref[...])
    g0 = route[:, ROUTE_G0:ROUTE_G0 + 1]
    g1 = route[:, ROUTE_G1:ROUTE_G1 + 1]
    out = x_ref[...] + (ybuf[slot, 0].reshape(tm, D) * g0 + ybuf[slot, 1].reshape(tm, D) * g1)
    if final_norm:
        out = _rms(out, g_ref[...])
    if t_first is None:
        o_refs[0][...] = out
    else:
        @pl.when(i < t_first)
        def _():
            o_refs[0][...] = out

        @pl.when(i >= t_first)
        def _():
            o_refs[1][...] = out


def _combine(dest_flat, x, route, g, y_rows, *, tm, final_norm, n_first=None):
    N, D = x.shape
    nt = N // tm
    row = lambda wdt: pl.BlockSpec((tm, wdt), lambda i: (i, 0))
    nxt = lambda i: jnp.minimum(i + 1, nt - 1)
    smem = lambda f: pl.BlockSpec((tm,), f, memory_space=pltpu.SMEM)
    if n_first is None:
        t_first, out_shape, out_specs = None, jax.ShapeDtypeStruct((N, D), F32), row(D)
    else:
        t_first = n_first // tm
        out_shape = (jax.ShapeDtypeStruct((n_first, D), F32), jax.ShapeDtypeStruct((N - n_first, D), F32))
        out_specs = (pl.BlockSpec((tm, D), lambda i: (jnp.minimum(i, t_first - 1), 0)),
                     pl.BlockSpec((tm, D), lambda i: (jnp.maximum(i - t_first, 0), 0)))
    return pl.pallas_call(
        functools.partial(_combine_kernel, tm=tm, final_norm=final_norm, t_first=t_first),
        out_shape=out_shape, grid=(nt,),
        in_specs=[smem(lambda i: (i,)), smem(lambda i: (i + nt,)),
                  smem(lambda i: (nxt(i),)), smem(lambda i: (nxt(i) + nt,)),
                  row(D), pl.BlockSpec((ROUTE_ROWS, tm), lambda i: (0, i)), pl.BlockSpec((1, D), lambda i: (0, 0)),
                  pl.BlockSpec(memory_space=pl.ANY)],
        out_specs=out_specs,
        scratch_shapes=[pltpu.VMEM((2, 2, tm // SUBLANES, SUBLANES, D), F32), pltpu.SemaphoreType.DMA((2,))],
        compiler_params=pltpu.CompilerParams(dimension_semantics=("arbitrary",), vmem_limit_bytes=VMEM_LIMIT),
        name="combine",
    )(dest_flat, dest_flat, dest_flat, dest_flat, x, route, g, y_rows)


def _hier_moe(x, g_ffn, w_rg, w_re, w_gate, w_up, w_down, layer, g_final, *, final_norm, n_first=None):
    N, D = x.shape
    tm = TOKEN_TILE
    wr_t = jnp.concatenate([w_rg.T, w_re.T, jnp.zeros((LANES - N_LOGITS, D), F32)], axis=0).astype(BF16)
    g_ffn = g_ffn.reshape(1, D)
    route, counts = _router(x, g_ffn, wr_t, tm=tm)
    counts = counts[:, 0].astype(I32)
    eid_t = route[ROUTE_E0:ROUTE_E1 + 1].astype(I32)
    rank_t = route[ROUTE_R0:ROUTE_R1 + 1].astype(I32)
    padded = (counts + MOE_BLOCK - 1) // MOE_BLOCK * MOE_BLOCK
    pad_end = jnp.cumsum(padded)
    pad_start = pad_end - padded
    expert_ids = jnp.arange(N_EXPERTS, dtype=I32)
    start_of = jnp.sum(jnp.where(eid_t[:, :, None] == expert_ids, pad_start, 0), axis=-1)
    dest_t = start_of + rank_t
    n_blocks = -(-(2 * N) // MOE_BLOCK) + N_EXPERTS
    n_rows = n_blocks * MOE_BLOCK
    n_used = (pad_end[-1] // MOE_BLOCK).astype(I32)
    block_start = jnp.minimum(jnp.arange(n_blocks, dtype=I32), n_used - 1) * MOE_BLOCK
    block_e = jnp.minimum(jnp.sum(block_start[:, None] >= pad_end[None, :], axis=1), N_EXPERTS - 1).astype(I32)
    pad_info = jnp.stack([pad_end - MOE_BLOCK, padded - counts, jnp.broadcast_to(n_used, (N_EXPERTS,))])
    dest_flat = dest_t.reshape(-1)
    x_rows = _dispatch(dest_flat, pad_info, x, g_ffn, tm=tm, n_rows=n_rows)
    y_rows = _experts(block_e, n_used.reshape(1), x_rows, w_gate, w_up, w_down, layer)
    return _combine(dest_flat, x, route, g_final.reshape(1, D), y_rows, tm=COMBINE_TILE, final_norm=final_norm,
                    n_first=n_first)


def _rope_tables(pos, rot_lanes):
    inv = 1.0 / (ROPE_THETA ** (jnp.arange(ROT_HALF, dtype=F32) / ROT_HALF))
    ang = pos.astype(F32)[:, None] * inv[None, :]
    cos, sin = jnp.cos(ang), jnp.sin(ang)
    lane = jnp.arange(LANES)
    r = lane % HEAD_DIM
    active = (lane < rot_lanes)
    first = active & (r < ROT_HALF)
    second = active & (r >= ROT_HALF) & (r < 2 * ROT_HALF)
    cos_l = cos[:, r % ROT_HALF]
    sin_l = sin[:, r % ROT_HALF]
    c = jnp.where((first | second)[None, :], cos_l, 1.0)
    sa = jnp.where(first[None, :], -sin_l, 0.0)
    sb = jnp.where(second[None, :], sin_l, 0.0)
    return jnp.stack([c, sa, sb]).astype(F32)


def _prep_w_in(w_in):
    D = w_in.shape[0]
    assert w_in.shape[1] == sum(IN_SIZES)
    pad = jnp.zeros((D, IN_WIDTH_PAD - w_in.shape[1]), w_in.dtype)
    return jnp.concatenate([w_in, pad], axis=1).astype(BF16)


def kernel(x_prompt, x_sample, cache_k, cache_v, cache_idx_k, state_pool, norm_mix, norm_ffn, norm_final,
           par_w_in, par_pool_w, par_pool_scale, par_w_out, gm_w_in, gm_ln_g, gm_ln_b, gm_ws, gm_bs, gm_w_out,
           moe_router_group, moe_router_expert, moe_w_gate, moe_w_up, moe_w_down):
    Bp, Tp, D = x_prompt.shape
    Bs, Ts, _ = x_sample.shape
    past = cache_k.shape[2]
    Np, Ns = Bp * Tp, Bs * Ts
    depth = norm_mix.shape[0]
    assert depth == 2 and Ts == CHUNK and Tp % PROJ_TILE == 0 and Tp % QUERY_TILE == 0 and Tp % TOKEN_TILE == 0 and Ns % TOKEN_TILE == 0

    w_in_bf = _prep_w_in(par_w_in[0])
    pw_bf = par_pool_w[0].astype(BF16)
    ps = par_pool_scale[0].reshape(1, POOL_WIDTH)
    wo_bf = par_w_out[0].astype(BF16)
    g_mix0 = norm_mix[0].reshape(1, D)
    pos_p = jnp.arange(Tp, dtype=I32)
    pos_s = past + jnp.arange(Ts, dtype=I32)

    hist_p = jnp.zeros((Bp, HIST_ROWS, POOL_WIDTH), F32)
    hist_s = jnp.pad(state_pool[0], ((0, 0), (1, 0), (0, 0)))
    tk = KEY_BLOCK
    (q_p, qi_p, k_p, v_p, kiwi_p, kbf_p, vbf_p, kibf_p, yp_p, st_p) = _inproj(
        x_prompt, g_mix0, w_in_bf, _rope_tables(pos_p, LANES), _rope_tables(pos_p, IDX_DIM), hist_p, pw_bf, ps,
        tm=PROJ_TILE, pos0=0)
    (q_s, qi_s, k_s, v_s, kiwi_s, kbf_s, vbf_s, kibf_s, yp_s, st_s) = _inproj(
        x_sample, g_mix0, w_in_bf, _rope_tables(pos_s, LANES), _rope_tables(pos_s, IDX_DIM), hist_s, pw_bf, ps,
        tm=Ts, pos0=past)

    ls = past + Ts
    n_blk = -(-ls // tk)
    tk_s = -(-ls // (n_blk * LANES)) * LANES
    lsp = n_blk * tk_s
    kpad = lambda a, ax: jnp.pad(a, [(0, lsp - ls) if d == ax else (0, 0) for d in range(a.ndim)])
    kall = kpad(jnp.concatenate([cache_k[0].reshape(Bs, past, KV_WIDTH).astype(BF16), kbf_s], axis=1), 1)
    vall = kpad(jnp.concatenate([cache_v[0].reshape(Bs, past, KV_WIDTH).astype(BF16), vbf_s], axis=1), 1)
    vall_t = jnp.transpose(vall.reshape(Bs, lsp // LANES, LANES, KV_WIDTH), (0, 1, 3, 2))
    kiall = kpad(jnp.concatenate([cache_idx_k[0].astype(BF16), kibf_s], axis=1), 1)
    qpad = lambda a, ax: jnp.pad(a, [(0, LANES - Ts) if d == ax else (0, 0) for d in range(a.ndim)])
    x1_s = _dsa(qpad(q_s, 2), qpad(qi_s, 2), qpad(kiwi_s, 1), kall, vall_t, kiall, qpad(x_sample, 1), qpad(yp_s, 1),
                wo_bf, tq=LANES, tk=tk_s, pos0=past, n_keys=ls, n_sel=min(TOPK_MAX, ls // 4))[:, :Ts]
    assert Tp % tk == 0
    x = jnp.pad(x1_s.reshape(Ns, D), ((Np, 0), (0, 0)))
    x = _dsa(q_p, qi_p, kiwi_p, kbf_p, vbf_p, kibf_p, x_prompt, yp_p, wo_bf,
             tq=QUERY_TILE, tk=tk, pos0=0, n_keys=Tp, n_sel=min(TOPK_MAX, Tp // 4), out_base=x)
    x = _hier_moe(x, norm_ffn[0], moe_router_group[0], moe_router_expert[0],
                  moe_w_gate, moe_w_up, moe_w_down, 0, norm_final, final_norm=False)

    cs = Ts
    tril = lambda n: jnp.tril(jnp.ones((n, n), bool))
    ws_p = jnp.where(tril(GM_CHUNK)[None], gm_ws[0], 0.0)
    ws_small = jnp.where(tril(cs)[None], gm_ws[0][:, :cs, :cs], 0.0)
    rep = GM_CHUNK // cs
    ws_s = jnp.einsum('ab,gts->gatbs', jnp.eye(rep, dtype=F32), ws_small).reshape(GM_GROUPS, GM_CHUNK, GM_CHUNK)
    ws2 = jnp.stack([ws_p, ws_s]).astype(BF16)
    gd = D // GM_GROUPS
    bias_p = jnp.repeat(jnp.transpose(gm_bs[0]), gd, axis=1)
    bias_s = jnp.tile(jnp.repeat(jnp.transpose(gm_bs[0][:, :cs]), gd, axis=1), (rep, 1))
    bias2 = jnp.stack([bias_p, bias_s])
    x, gm_v = _gmlp(x, norm_mix[1].reshape(1, D), gm_w_in[0].astype(BF16), gm_ln_g[0].reshape(1, D),
                    gm_ln_b[0].reshape(1, D), ws2, bias2, gm_w_out[0].astype(BF16),
                    tm=TOKEN_TILE, n_first=Np, n_v_rows=Ns)
    y_p, y_s = _hier_moe(x, norm_ffn[1], moe_router_group[1], moe_router_expert[1],
                         moe_w_gate, moe_w_up, moe_w_down, 1, norm_final, final_norm=True, n_first=Np)
    y_p = y_p.reshape(Bp, Tp, D)
    y_s = y_s.reshape(Bs, Ts, D)
    r4 = lambda a, b, t: a.reshape(1, b, t, N_KV_HEADS, HEAD_DIM)
    return (y_p, y_s,
            r4(k_p, Bp, Tp), r4(v_p, Bp, Tp), kiwi_p[:, :, :IDX_DIM][None], st_p[:, 1:][None],
            r4(k_s, Bs, Ts), r4(v_s, Bs, Ts), kiwi_s[:, :, :IDX_DIM][None], st_s[:, 1:][None],
            gm_v.reshape(1, Bs, Ts, D))
```

```python
import functools

import jax
import jax.numpy as jnp
from jax import lax
from jax.experimental import pallas as pl
from jax.experimental.pallas import tpu as pltpu

F32 = jnp.float32
BF16 = jnp.bfloat16
I32 = jnp.int32
U32 = jnp.uint32

LANES = 128
SUBLANES = 8
CHUNK = 64
POOL_WINDOWS = (2, 4, 8, 16)
POOL_GROUP_DIM = 128
POOL_WIDTH = 512
HIST_ROWS = 16
N_HEADS = 8
HEAD_DIM = 64
N_KV_HEADS = 4
Q_PER_KV = N_HEADS // N_KV_HEADS
ATT_WIDTH = N_HEADS * HEAD_DIM
KV_WIDTH = N_KV_HEADS * HEAD_DIM
N_IDX_HEADS = 8
IDX_DIM = 64
TOPK_MAX = 256
ROPE_THETA = 500000.0
ROT_HALF = HEAD_DIM // 8
GM_CHUNK = 128
GM_GROUPS = 8
N_EXPERT_GROUPS = 4
EXPERTS_PER_GROUP = 8
N_EXPERTS = 32
MOE_BLOCK = 512
RMS_EPS = 1e-6
LN_EPS = 1e-5

INT_MIN = -2147483648
LOG2_E = 1.4426950408889634
NEG_BIG = -1e30
VMEM_LIMIT = 48 * 1024 * 1024

TOKEN_TILE = 1024
COMBINE_TILE = 512
assert TOKEN_TILE % COMBINE_TILE == 0
PROJ_TILE = 1024
KEY_BLOCK = 512
QUERY_TILE = 256
COUNT_SLAB = 64

IN_SIZES = (POOL_WIDTH, ATT_WIDTH, KV_WIDTH, KV_WIDTH, N_IDX_HEADS * IDX_DIM, IDX_DIM, N_IDX_HEADS)
COL_XP, COL_Q, COL_K, COL_V, COL_QI, COL_KIWI = (sum(IN_SIZES[:i]) for i in range(6))
IN_WIDTH_PAD = COL_KIWI + LANES


def _rms(x, g):
    return x * lax.rsqrt(jnp.mean(x * x, axis=-1, keepdims=True) + RMS_EPS) * g


def _rope128(x, c, sa, sb):
    return x * c + pltpu.roll(x, LANES - ROT_HALF, 1) * sa + pltpu.roll(x, ROT_HALF, 1) * sb


def _inproj_kernel(x_ref, g_ref, w_ref, rope_ref, ropeki_ref, hist_ref, pw_ref, ps_ref,
                   q_ref, qi_ref, k_ref, v_ref, kiwi_ref, kbf_ref, vbf_ref, kibf_ref, yp_ref, state_ref,
                   buf_ref, *, tm, pos0):
    j = pl.program_id(1)
    h = _rms(x_ref[0], g_ref[...])
    proj = jnp.dot(h.astype(BF16), w_ref[...], preferred_element_type=F32)
    c, sa, sb = rope_ref[0], rope_ref[1], rope_ref[2]

    def put_heads(ref, i, chunk):
        ref[0, 2 * i] = chunk[:, :HEAD_DIM].astype(BF16)
        ref[0, 2 * i + 1] = chunk[:, HEAD_DIM:].astype(BF16)

    for i in range(ATT_WIDTH // LANES):
        put_heads(q_ref, i, _rope128(proj[:, COL_Q + i * LANES:COL_Q + (i + 1) * LANES], c, sa, sb)
                  * (HEAD_DIM ** -0.5 * LOG2_E))
        put_heads(qi_ref, i, _rope128(proj[:, COL_QI + i * LANES:COL_QI + (i + 1) * LANES], c, sa, sb))
    for i in range(KV_WIDTH // LANES):
        sl = slice(i * LANES, (i + 1) * LANES)
        kr = _rope128(proj[:, COL_K + i * LANES:COL_K + (i + 1) * LANES], c, sa, sb)
        k_ref[0, :, sl] = kr
        kbf_ref[0, :, sl] = kr.astype(BF16)
    vv = proj[:, COL_V:COL_V + KV_WIDTH]
    v_ref[0] = vv
    if tm % LANES == 0:
        for cc in range(tm // LANES):
            vbf_ref[0, cc] = jnp.transpose(vv[cc * LANES:(cc + 1) * LANES, :]).astype(BF16)
    else:
        vbf_ref[0] = vv.astype(BF16)
    kiwi = _rope128(proj[:, COL_KIWI:COL_KIWI + LANES], ropeki_ref[0], ropeki_ref[1], ropeki_ref[2])
    kiwi_ref[0] = kiwi
    kibf_ref[0] = kiwi[:, :IDX_DIM].astype(BF16)

    @pl.when(j == 0)
    def _():
        buf_ref[0:HIST_ROWS, :] = hist_ref[0]

    xp = proj[:, COL_XP:COL_XP + POOL_WIDTH]
    buf_ref[HIST_ROWS:HIST_ROWS + tm, :] = xp
    pos = pos0 + j * tm + lax.broadcasted_iota(I32, (tm, 1), 0)
    for gi, w in enumerate(POOL_WINDOWS):
        c0 = gi * POOL_GROUP_DIM
        s = xp[:, c0:c0 + POOL_GROUP_DIM]
        for i in range(1, w):
            s = s + buf_ref[HIST_ROWS - i:HIST_ROWS - i + tm, c0:c0 + POOL_GROUP_DIM]
        cnt = jnp.minimum(pos + 1, w).astype(F32)
        d = s / cnt - xp[:, c0:c0 + POOL_GROUP_DIM]
        y = jnp.dot(d.astype(BF16), pw_ref[gi], preferred_element_type=F32)
        yp_ref[0, :, c0:c0 + POOL_GROUP_DIM] = (y * ps_ref[:, c0:c0 + POOL_GROUP_DIM]).astype(BF16)
    tail = buf_ref[tm:tm + HIST_ROWS, :]
    state_ref[0] = tail
    buf_ref[0:HIST_ROWS, :] = tail


def _inproj(x, g, w_bf, rope, ropeki, hist, pw_bf, ps, *, tm, pos0):
    B, T, D = x.shape
    nt = T // tm
    f = lambda shape, dt: jax.ShapeDtypeStruct(shape, dt)
    v_t = tm % LANES == 0
    out_shape = (
        f((B, N_HEADS, T, HEAD_DIM), BF16), f((B, N_IDX_HEADS, T, IDX_DIM), BF16),
        f((B, T, KV_WIDTH), F32), f((B, T, KV_WIDTH), F32), f((B, T, LANES), F32),
        f((B, T, KV_WIDTH), BF16),
        f((B, T // LANES, KV_WIDTH, LANES) if v_t else (B, T, KV_WIDTH), BF16),
        f((B, T, IDX_DIM), BF16),
        f((B, T, POOL_WIDTH), BF16), f((B, HIST_ROWS, POOL_WIDTH), F32),
    )
    tile = lambda wdt: pl.BlockSpec((1, tm, wdt), lambda b, j: (b, j, 0))
    heads = lambda n, wdt: pl.BlockSpec((1, n, tm, wdt), lambda b, j: (b, 0, j, 0))
    const2 = lambda s: pl.BlockSpec(s, lambda b, j: (0, 0))
    in_specs = [
        tile(D), const2((1, D)), const2((D, IN_WIDTH_PAD)),
        pl.BlockSpec((3, tm, LANES), lambda b, j: (0, j, 0)),
        pl.BlockSpec((3, tm, LANES), lambda b, j: (0, j, 0)),
        pl.BlockSpec((1, HIST_ROWS, POOL_WIDTH), lambda b, j: (b, 0, 0)),
        pl.BlockSpec((len(POOL_WINDOWS), POOL_GROUP_DIM, POOL_GROUP_DIM), lambda b, j: (0, 0, 0)),
        const2((1, POOL_WIDTH)),
    ]
    out_specs = (
        heads(N_HEADS, HEAD_DIM), heads(N_IDX_HEADS, IDX_DIM), tile(KV_WIDTH), tile(KV_WIDTH), tile(LANES),
        tile(KV_WIDTH),
        pl.BlockSpec((1, tm // LANES, KV_WIDTH, LANES), lambda b, j: (b, j, 0, 0)) if v_t else tile(KV_WIDTH),
        tile(IDX_DIM), tile(POOL_WIDTH),
        pl.BlockSpec((1, HIST_ROWS, POOL_WIDTH), lambda b, j: (b, 0, 0)),
    )
    return pl.pallas_call(
        functools.partial(_inproj_kernel, tm=tm, pos0=pos0),
        out_shape=out_shape, grid=(B, nt), in_specs=in_specs, out_specs=out_specs,
        scratch_shapes=[pltpu.VMEM((HIST_ROWS + tm, POOL_WIDTH), F32)],
        compiler_params=pltpu.CompilerParams(dimension_semantics=("parallel", "arbitrary"),
                                             vmem_limit_bytes=VMEM_LIMIT),
        name="inproj",
    )(x, g, w_bf, rope, ropeki, hist, pw_bf, ps)


def _dsa_kernel(q_ref, qi_ref, kiwi_ref, k_ref, vt_ref, ki_ref, x_ref, yp_ref, wo_ref, *rest,
                tq, tk, pos0, n_keys, n_sel):
    o_ref, key_buf, bias_buf, m_scr, l_scr, acc_scr, s_scr = rest[-7:]
    slab = COUNT_SLAB
    j = pl.program_id(1)
    base = pos0 + j * tq
    pos = base + lax.broadcasted_iota(I32, (1, tq), 1)
    limit = jnp.minimum((pos // CHUNK + 1) * CHUNK, n_keys)
    limit_max = jnp.minimum(((base + tq - 1) // CHUNK + 1) * CHUNK, n_keys)
    nkb = (limit_max + tk - 1) // tk
    nt = (((1,), (1,)), ((), ()))

    wi_t = jnp.transpose(kiwi_ref[0])[IDX_DIM:IDX_DIM + N_IDX_HEADS, :]

    def score_pair(i, carry):
        blocks = (2 * i, jnp.minimum(2 * i + 1, nkb - 1))
        scores = []
        for kb in blocks:
            off = pl.multiple_of(kb * tk, tk)
            kiblk = ki_ref[0, pl.ds(off, tk), :]
            scores.append([lax.dot_general(kiblk, qi_ref[0, h], nt, preferred_element_type=F32)
                           for h in range(N_IDX_HEADS)])
        for kb, sc_h in zip(blocks, scores):
            idx = jnp.zeros((tk, tq), F32)
            for h in range(N_IDX_HEADS):
                idx = idx + jnp.maximum(sc_h[h], 0.0) * wi_t[h:h + 1, :]
            idx = jnp.where(idx == 0.0, 0.0, idx)
            bits = lax.bitcast_convert_type(idx, I32)
            key = bits ^ ((bits >> 31) & 0x7FFFFFFF)
            key_buf[kb] = jnp.where(lax.broadcasted_iota(I32, (tk, tq), 0) < limit - kb * tk, key, INT_MIN)
        return carry

    lax.fori_loop(0, (nkb + 1) // 2, score_pair, 0)

    def col_sum(a):
        return jnp.sum(a, axis=0, keepdims=True)

    def count_ge(cand):
        def body(kb, acc):
            kblk = key_buf[kb]
            for c in range(tk // slab):
                acc = acc + jnp.where(kblk[c * slab:(c + 1) * slab] >= cand, 1.0, 0.0)
            return acc
        return col_sum(lax.fori_loop(0, nkb, body, jnp.zeros((slab, tq), F32)))

    kf = float(n_sel)
    cnt0 = count_ge(jnp.zeros((1, tq), I32))
    t0 = jnp.where(cnt0 >= kf, 0, INT_MIN).astype(I32)

    def bit_body(i, carry):
        t, cnt = carry
        cand = t | lax.shift_left(jnp.int32(1), 30 - i)
        cnt_cand = count_ge(cand)
        keep = cnt_cand >= kf
        return jnp.where(keep, cand, t), jnp.where(keep, cnt_cand, cnt)

    t, cnt_ge = lax.fori_loop(0, 31, bit_body, (t0, cnt0))
    cnt_gt = count_ge(t + 1)
    need = kf - cnt_gt
    cnt_eq = cnt_ge - cnt_gt
    overfull = jnp.where(t != INT_MIN, cnt_eq - need, 0.0)
    slow = jnp.max(overfull) > 0.0

    @pl.when(jnp.logical_not(slow))
    def _():
        t_adm = jnp.maximum(t, INT_MIN + 1)

        def body(kb, carry):
            bias_buf[kb] = jnp.where(key_buf[kb] >= t_adm, 0.0, NEG_BIG)
            return carry
        lax.fori_loop(0, nkb, body, 0)

    @pl.when(slow)
    def _():
        tri = jnp.where(lax.broadcasted_iota(I32, (tk, tk), 1) <= lax.broadcasted_iota(I32, (tk, tk), 0),
                        1.0, 0.0).astype(BF16)

        def body(kb, seen):
            kblk = key_buf[kb]
            eq = jnp.where((kblk == t) & (kblk != INT_MIN), 1.0, 0.0)
            prefix = jnp.dot(tri, eq.astype(BF16), preferred_element_type=F32) + seen
            keep_tie = jnp.where(prefix <= need, eq, 0.0)
            sel = jnp.where(kblk > t, 1.0, keep_tie)
            bias_buf[kb] = jnp.where(sel > 0.0, 0.0, NEG_BIG)
            return seen + col_sum(eq)
        lax.fori_loop(0, nkb, body, jnp.zeros((1, tq), F32))

    sub = LANES
    nsub = tk // sub
    tl = LANES
    n_tiles = tq // tl
    m_scr[...] = jnp.full(m_scr.shape, NEG_BIG, F32)
    l_scr[...] = jnp.zeros(l_scr.shape, F32)
    acc_scr[...] = jnp.zeros(acc_scr.shape, F32)

    def tile_blocks(hq):
        return (jnp.minimum(((base + (hq + 1) * tl - 1) // CHUNK + 1) * CHUNK, n_keys) + tk - 1) // tk

    def attn_block(kb, members):
        for mi, hq in enumerate(members):
            for c in range(nsub):
                off = pl.multiple_of(kb * tk + c * sub, sub)
                for g in range(N_KV_HEADS):
                    s_scr[mi, c, g] = lax.dot_general(
                        k_ref[0, pl.ds(off, sub), g * HEAD_DIM:(g + 1) * HEAD_DIM],
                        q_ref[0, Q_PER_KV * g:Q_PER_KV * (g + 1), hq * tl:(hq + 1) * tl].reshape(Q_PER_KV * tl, HEAD_DIM),
                        nt, preferred_element_type=F32)
        for mi, hq in enumerate(members):
            m = [m_scr[hq, g] for g in range(N_KV_HEADS)]
            l = [l_scr[hq, g] for g in range(N_KV_HEADS)]
            for c in range(nsub):
                bias = bias_buf[kb, c * sub:(c + 1) * sub, hq * tl:(hq + 1) * tl]
                bias2 = jnp.concatenate([bias] * Q_PER_KV, axis=1)
                for g in range(N_KV_HEADS):
                    s = s_scr[mi, c, g] + bias2
                    m_new = jnp.maximum(m[g], jnp.max(s, axis=0, keepdims=True))
                    alpha = jnp.exp2(m[g] - m_new)
                    p = jnp.exp2(s - m_new)
                    l[g] = alpha * l[g] + jnp.sum(p.reshape(sub // SUBLANES, SUBLANES, Q_PER_KV * tl), axis=0)
                    pv = jnp.dot(vt_ref[0, kb * nsub + c, g * HEAD_DIM:(g + 1) * HEAD_DIM, :], p.astype(BF16),
                                 preferred_element_type=F32)
                    acc_scr[hq, g] = alpha * acc_scr[hq, g] + pv
                    m[g] = m_new
            for g in range(N_KV_HEADS):
                m_scr[hq, g] = m[g]
                l_scr[hq, g] = l[g]

    for first in range(0, n_tiles, 2):
        members = tuple(range(first, min(first + 2, n_tiles)))
        shared = tile_blocks(members[0])
        lax.fori_loop(0, shared, lambda kb, c, ms=members: (attn_block(kb, ms), c)[1], 0)
        if len(members) == 2:
            lax.fori_loop(shared, tile_blocks(members[1]),
                          lambda kb, c, ms=members[1:]: (attn_block(kb, ms), c)[1], 0)

    y_att = []
    for hq in range(n_tiles):
        o_t = []
        for g in range(N_KV_HEADS):
            og = acc_scr[hq, g] / col_sum(l_scr[hq, g])
            o_t.extend(og[:, hh * tl:(hh + 1) * tl] for hh in range(Q_PER_KV))
        y_att.append(jnp.transpose(jnp.concatenate(o_t, axis=0)).astype(BF16))
    y_att = jnp.concatenate(y_att, axis=0)

    y = jnp.dot(yp_ref[0], wo_ref[0:POOL_WIDTH, :], preferred_element_type=F32)
    y = y + jnp.dot(y_att, wo_ref[POOL_WIDTH:POOL_WIDTH + ATT_WIDTH, :], preferred_element_type=F32)
    o_ref[...] = (x_ref[0] + y).reshape(o_ref.shape)


def _dsa(q_hm, qi_hm, kiwi, k_all, v_t, ki_all, x, yp, wo_bf, *, tq, tk, pos0, n_keys, n_sel, out_base=None):
    B, T, D = x.shape
    L = k_all.shape[1]
    assert L % tk == 0 and T % tq == 0 and tq % LANES == 0
    nkb_max = L // tk
    tile = lambda wdt: pl.BlockSpec((1, tq, wdt), lambda b, j: (b, j, 0))
    in_specs = [pl.BlockSpec((1, N_HEADS, tq, HEAD_DIM), lambda b, j: (b, 0, j, 0)),
                pl.BlockSpec((1, N_IDX_HEADS, tq, IDX_DIM), lambda b, j: (b, 0, j, 0)),
                tile(LANES),
                pl.BlockSpec((1, L, KV_WIDTH), lambda b, j: (b, 0, 0)),
                pl.BlockSpec((1, L // LANES, KV_WIDTH, LANES), lambda b, j: (b, 0, 0, 0)),
                pl.BlockSpec((1, L, IDX_DIM), lambda b, j: (b, 0, 0)),
                tile(D), tile(POOL_WIDTH), pl.BlockSpec((POOL_WIDTH + ATT_WIDTH, D), lambda b, j: (0, 0))]
    nq = T // tq
    args = (q_hm, qi_hm, kiwi, k_all, v_t, ki_all, x, yp, wo_bf)
    if out_base is None:
        out_shape, out_spec, aliases = jax.ShapeDtypeStruct((B, T, D), F32), tile(D), {}
    else:
        out_shape = jax.ShapeDtypeStruct(out_base.shape, F32)
        out_spec = pl.BlockSpec((tq, D), lambda b, j: (b * nq + j, 0))
        in_specs, args, aliases = in_specs + [pl.BlockSpec(memory_space=pl.ANY)], args + (out_base,), {len(args): 0}
    return pl.pallas_call(
        functools.partial(_dsa_kernel, tq=tq, tk=tk, pos0=pos0, n_keys=n_keys, n_sel=n_sel),
        out_shape=out_shape, grid=(B, nq), in_specs=in_specs, out_specs=out_spec, input_output_aliases=aliases,
        scratch_shapes=[pltpu.VMEM((nkb_max, tk, tq), I32), pltpu.VMEM((nkb_max, tk, tq), F32),
                        pltpu.VMEM((tq // LANES, N_KV_HEADS, 1, Q_PER_KV * LANES), F32),
                        pltpu.VMEM((tq // LANES, N_KV_HEADS, SUBLANES, Q_PER_KV * LANES), F32),
                        pltpu.VMEM((tq // LANES, N_KV_HEADS, HEAD_DIM, Q_PER_KV * LANES), F32),
                        pltpu.VMEM((2, tk // LANES, N_KV_HEADS, LANES, Q_PER_KV * LANES), F32)],
        compiler_params=pltpu.CompilerParams(dimension_semantics=("parallel", "arbitrary"),
                                             vmem_limit_bytes=VMEM_LIMIT),
        name="dsa",
    )(*args)


def _gmlp_kernel(x_ref, g_ref, win_ref, lng_ref, lnb_ref, ws_ref, bias_ref, wout_ref, o_ref, v_ref, *, tm):
    x = x_ref[...]
    h = _rms(x, g_ref[...])
    z = jax.nn.gelu(jnp.dot(h.astype(BF16), win_ref[...], preferred_element_type=F32))
    half = z.shape[1] // 2
    u, v = z[:, :half], z[:, half:]
    mu = jnp.mean(v, axis=-1, keepdims=True)
    var = jnp.mean(jnp.square(v - mu), axis=-1, keepdims=True)
    vn = (v - mu) * lax.rsqrt(var + LN_EPS) * lng_ref[...] + lnb_ref[...]
    v_ref[...] = vn
    gd = half // GM_GROUPS
    gated = []
    for c in range(tm // GM_CHUNK):
        rows = slice(c * GM_CHUNK, (c + 1) * GM_CHUNK)
        vc = vn[rows].astype(BF16)
        mixed = jnp.concatenate(
            [jnp.dot(ws_ref[0, g], vc[:, g * gd:(g + 1) * gd], preferred_element_type=F32)
             for g in range(GM_GROUPS)], axis=1) + bias_ref[0]
        gated.append((u[rows] * mixed).astype(BF16))
    gated = jnp.concatenate(gated, axis=0)
    o_ref[...] = x + jnp.dot(gated, wout_ref[...], preferred_element_type=F32)


def _gmlp(x, g, win_bf, lng, lnb, ws2, bias2, wout_bf, *, tm, n_first, n_v_rows):
    N, D = x.shape
    half = win_bf.shape[1] // 2
    nt = N // tm
    t_first = n_first // tm
    variant = lambda i: jnp.where(i >= t_first, 1, 0)
    row = pl.BlockSpec((tm, D), lambda i: (i, 0))
    const = lambda s: pl.BlockSpec(s, lambda i: (0, 0))
    in_specs = [row, const((1, D)), const((D, 2 * half)), const((1, half)), const((1, half)),
                pl.BlockSpec((1, GM_GROUPS, GM_CHUNK, GM_CHUNK), lambda i: (variant(i), 0, 0, 0)),
                pl.BlockSpec((1, GM_CHUNK, half), lambda i: (variant(i), 0, 0)),
                const((half, D))]
    out_specs = (row, pl.BlockSpec((tm, half), lambda i: (jnp.maximum(i - t_first, 0), 0)))
    return pl.pallas_call(
        functools.partial(_gmlp_kernel, tm=tm),
        out_shape=(jax.ShapeDtypeStruct((N, D), F32), jax.ShapeDtypeStruct((n_v_rows, half), F32)),
        grid=(nt,), in_specs=in_specs, out_specs=out_specs,
        compiler_params=pltpu.CompilerParams(dimension_semantics=("arbitrary",), vmem_limit_bytes=VMEM_LIMIT),
        name="gmlp",
    )(x, g, win_bf, lng, lnb, ws2, bias2, wout_bf)


ROUTE_E0, ROUTE_E1, ROUTE_R0, ROUTE_R1, ROUTE_G0, ROUTE_G1 = range(6)
ROUTE_ROWS = 8
N_LOGITS = N_EXPERT_GROUPS + N_EXPERTS
LOGIT_ROWS = -(-N_LOGITS // SUBLANES) * SUBLANES


def _router_kernel(x_ref, g_ref, wr_ref, route_ref, cnt_ref, carry_ref, *, tm):
    i = pl.program_id(0)

    @pl.when(i == 0)
    def _():
        carry_ref[...] = jnp.zeros(carry_ref.shape, F32)

    h = _rms(x_ref[...], g_ref[...])
    nt = (((1,), (1,)), ((), ()))
    logits = lax.dot_general(wr_ref[...], h.astype(BF16), nt, preferred_element_type=F32)[:LOGIT_ROWS]
    row = lax.broadcasted_iota(I32, (LOGIT_ROWS, tm), 0).astype(F32)
    ninf = -jnp.inf
    big = float(LANES)
    cmax = lambda a: jnp.max(a, axis=0, keepdims=True)
    cmin = lambda a: jnp.min(a, axis=0, keepdims=True)
    csum = lambda a: jnp.sum(a, axis=0, keepdims=True)

    is_grp = row < N_EXPERT_GROUPS
    lg = jnp.where(is_grp, logits, ninf)
    mg = cmax(lg)
    g_sel = cmin(jnp.where(lg == mg, row, big))
    p_grp = 1.0 / csum(jnp.where(is_grp, jnp.exp(lg - mg), 0.0))
    lo = N_EXPERT_GROUPS + g_sel * EXPERTS_PER_GROUP
    le = jnp.where((row >= lo) & (row < lo + EXPERTS_PER_GROUP), logits, ninf)
    v1 = cmax(le)
    j1 = cmin(jnp.where(le == v1, row, big))
    le2 = jnp.where(row == j1, ninf, le)
    v2 = cmax(le2)
    j2 = cmin(jnp.where(le2 == v2, row, big))
    e0 = j1 - N_EXPERT_GROUPS
    e1 = j2 - N_EXPERT_GROUPS
    r = jnp.exp(v2 - v1)
    g0 = p_grp / (1.0 + r)
    g1 = p_grp * r / (1.0 + r)

    erow = lax.broadcasted_iota(I32, (N_EXPERTS, tm), 0).astype(F32)
    oh0 = jnp.where(erow == e0, 1.0, 0.0)
    oh1 = jnp.where(erow == e1, 1.0, 0.0)
    oh = oh0 + oh1
    earlier = jnp.where(lax.broadcasted_iota(I32, (tm, tm), 0) < lax.broadcasted_iota(I32, (tm, tm), 1),
                        1.0, 0.0).astype(BF16)
    before = jnp.dot(oh.astype(BF16), earlier, preferred_element_type=F32) + carry_ref[...]
    r0 = csum(oh0 * before)
    r1 = csum(oh1 * before)
    carry_ref[...] = carry_ref[...] + jnp.sum(oh, axis=1, keepdims=True)
    cnt_ref[...] = jnp.broadcast_to(carry_ref[...], cnt_ref.shape)

    rows = {ROUTE_E0: e0, ROUTE_E1: e1, ROUTE_R0: r0, ROUTE_R1: r1, ROUTE_G0: g0, ROUTE_G1: g1}
    zero = jnp.zeros((1, tm), F32)
    route_ref[...] = jnp.concatenate([rows.get(k, zero) for k in range(ROUTE_ROWS)], axis=0)


def _router(x, g, wr_t, *, tm):
    N, D = x.shape
    const = lambda s: pl.BlockSpec(s, lambda i: (0, 0))
    return pl.pallas_call(
        functools.partial(_router_kernel, tm=tm),
        out_shape=(jax.ShapeDtypeStruct((ROUTE_ROWS, N), F32), jax.ShapeDtypeStruct((N_EXPERTS, LANES), F32)),
        grid=(N // tm,), in_specs=[pl.BlockSpec((tm, D), lambda i: (i, 0)), const((1, D)), const((LANES, D))],
        out_specs=(pl.BlockSpec((ROUTE_ROWS, tm), lambda i: (0, i)), const((N_EXPERTS, LANES))),
        scratch_shapes=[pltpu.VMEM((N_EXPERTS, 1), F32)],
        compiler_params=pltpu.CompilerParams(dimension_semantics=("arbitrary",), vmem_limit_bytes=VMEM_LIMIT),
        name="router",
    )(x, g, wr_t)


def _pack_bf16_pairs(h):
    half = h.shape[1] // 2
    lo = lax.bitcast_convert_type(h[:, :half].astype(BF16).astype(F32), U32)
    hi = lax.bitcast_convert_type(h[:, half:].astype(BF16).astype(F32), U32)
    return (lo >> 16) | (hi & jnp.uint32(0xFFFF0000))


def _unpack_bf16_pairs(w):
    lo = lax.bitcast_convert_type(w << 16, F32).astype(BF16)
    hi = lax.bitcast_convert_type(w & jnp.uint32(0xFFFF0000), F32).astype(BF16)
    return jnp.concatenate([lo, hi], axis=1)


def _dispatch_kernel(d0_ref, d1_ref, pad_ref, x_ref, g_ref, rows_ref, h_scr, zblk, sems, *, tm):
    i = pl.program_id(0)
    n = pl.num_programs(0)
    slot = i % 2
    groups = tm // SUBLANES

    def wait_slot(s):
        for _ in range(2 * groups):
            pltpu.make_async_copy(h_scr.at[s, 0], rows_ref.at[pl.ds(0, SUBLANES)], sems.at[s]).wait()

    @pl.when(i == 0)
    def _():
        zblk[...] = jnp.zeros(zblk.shape, U32)
        n_blocks = rows_ref.shape[0] // MOE_BLOCK
        n_used = pad_ref[2, 0]

        def blk_copy(row0):
            return pltpu.make_async_copy(zblk, rows_ref.at[pl.ds(row0, MOE_BLOCK)], sems.at[2])

        for wait in (False, True):
            for e in range(N_EXPERTS):
                @pl.when(pad_ref[1, e] > 0)
                def _():
                    cp = blk_copy(pl.multiple_of(pad_ref[0, e], MOE_BLOCK))
                    cp.wait() if wait else cp.start()

            def unused(b, carry):
                cp = blk_copy(pl.multiple_of(b * MOE_BLOCK, MOE_BLOCK))
                cp.wait() if wait else cp.start()
                return carry
            lax.fori_loop(n_used, n_blocks, unused, 0)

    @pl.when(i >= 2)
    def _():
        wait_slot(slot)

    h_scr[slot] = _pack_bf16_pairs(_rms(x_ref[...], g_ref[...])).reshape(groups, SUBLANES, h_scr.shape[3])

    def body(k, carry):
        for u in range(SUBLANES):
            r = k * SUBLANES + u
            src = h_scr.at[slot, k, pl.ds(u, 1)]
            pltpu.make_async_copy(src, rows_ref.at[pl.ds(d0_ref[r], 1)], sems.at[slot]).start()
            pltpu.make_async_copy(src, rows_ref.at[pl.ds(d1_ref[r], 1)], sems.at[slot]).start()
        return carry

    lax.fori_loop(0, groups, body, 0)

    @pl.when(i == n - 1)
    def _():
        wait_slot(slot)

    @pl.when(jnp.logical_and(i == n - 1, n >= 2))
    def _():
        wait_slot(1 - slot)


def _dispatch(dest_flat, pad_info, x, g, *, tm, n_rows):
    N, D = x.shape
    nt = N // tm
    return pl.pallas_call(
        functools.partial(_dispatch_kernel, tm=tm),
        out_shape=jax.ShapeDtypeStruct((n_rows, D // 2), U32), grid=(nt,),
        in_specs=[pl.BlockSpec((tm,), lambda i: (i,), memory_space=pltpu.SMEM),
                  pl.BlockSpec((tm,), lambda i: (i + nt,), memory_space=pltpu.SMEM),
                  pl.BlockSpec(memory_space=pltpu.SMEM),
                  pl.BlockSpec((tm, D), lambda i: (i, 0)),
                  pl.BlockSpec((1, D), lambda i: (0, 0))],
        out_specs=pl.BlockSpec(memory_space=pl.ANY),
        scratch_shapes=[pltpu.VMEM((2, tm // SUBLANES, SUBLANES, D // 2), U32), pltpu.VMEM((MOE_BLOCK, D // 2), U32),
                        pltpu.SemaphoreType.DMA((3,))],
        compiler_params=pltpu.CompilerParams(dimension_semantics=("arbitrary",), vmem_limit_bytes=VMEM_LIMIT),
        name="dispatch",
    )(dest_flat, dest_flat, pad_info, x, g)


def _expert_kernel(be_ref, nu_ref, x_ref, wg_ref, wu_ref, wd_ref, y_ref, wg_bf, wu_bf, wd_bf):
    i = pl.program_id(0)

    @pl.when(jnp.logical_or(i == 0, be_ref[i] != be_ref[jnp.maximum(i - 1, 0)]))
    def _():
        wg_bf[...] = wg_ref[0, 0].astype(BF16)
        wu_bf[...] = wu_ref[0, 0].astype(BF16)
        wd_bf[...] = wd_ref[0, 0].astype(BF16)

    @pl.when(i < nu_ref[0])
    def _():
        x = _unpack_bf16_pairs(x_ref[...])
        a = jnp.dot(x, wg_bf[...], preferred_element_type=F32)
        u = jnp.dot(x, wu_bf[...], preferred_element_type=F32)
        act = (a * jax.nn.sigmoid(a)) * u
        y_ref[...] = jnp.dot(act.astype(BF16), wd_bf[...], preferred_element_type=F32)

    @pl.when(i >= nu_ref[0])
    def _():
        y_ref[...] = jnp.zeros(y_ref.shape, F32)


def _experts(block_e, n_used, x_rows, w_gate, w_up, w_down, layer):
    n_rows = x_rows.shape[0]
    D, de = w_gate.shape[2:]
    n_blocks = n_rows // MOE_BLOCK
    grid_spec = pltpu.PrefetchScalarGridSpec(
        num_scalar_prefetch=2, grid=(n_blocks,),
        in_specs=[pl.BlockSpec((MOE_BLOCK, D // 2), lambda i, be, nu: (jnp.minimum(i, nu[0] - 1), 0)),
                  pl.BlockSpec((1, 1, D, de), lambda i, be, nu: (layer, be[i], 0, 0)),
                  pl.BlockSpec((1, 1, D, de), lambda i, be, nu: (layer, be[i], 0, 0)),
                  pl.BlockSpec((1, 1, de, D), lambda i, be, nu: (layer, be[i], 0, 0))],
        out_specs=pl.BlockSpec((MOE_BLOCK, D), lambda i, be, nu: (i, 0)),
        scratch_shapes=[pltpu.VMEM((D, de), BF16), pltpu.VMEM((D, de), BF16), pltpu.VMEM((de, D), BF16)])
    return pl.pallas_call(
        _expert_kernel, out_shape=jax.ShapeDtypeStruct((n_rows, D), F32), grid_spec=grid_spec,
        compiler_params=pltpu.CompilerParams(dimension_semantics=("arbitrary",), vmem_limit_bytes=VMEM_LIMIT),
        name="experts",
    )(block_e, n_used, x_rows, w_gate, w_up, w_down)


def _combine_kernel(dc0_ref, dc1_ref, dn0_ref, dn1_ref, x_ref, route_ref, g_ref, y_ref, *rest,
                    tm, final_norm, t_first):
    *o_refs, ybuf, sems = rest
    i = pl.program_id(0)
    n = pl.num_programs(0)
    slot = i % 2
    groups = tm // SUBLANES
    D = x_ref.shape[1]

    def gather(d_refs, s):
        def body(k, carry):
            for u in range(SUBLANES):
                r = k * SUBLANES + u
                for a in range(2):
                    pltpu.make_async_copy(y_ref.at[pl.ds(d_refs[a][r], 1)], ybuf.at[s, a, k, pl.ds(u, 1)],
                                          sems.at[s]).start()
            return carry
        lax.fori_loop(0, groups, body, 0)

    @pl.when(i == 0)
    def _():
        gather((dc0_ref, dc1_ref), 0)

    @pl.when(i + 1 < n)
    def _():
        gather((dn0_ref, dn1_ref), 1 - slot)

    for _ in range(2 * groups):
        pltpu.make_async_copy(y_ref.at[pl.ds(0, SUBLANES)], ybuf.at[slot, 0, 0], sems.at[slot]).wait()
    route = jnp.transpose(route_ref[...])
    g0 = route[:, ROUTE_G0:ROUTE_G0 + 1]
    g1 = route[:, ROUTE_G1:ROUTE_G1 + 1]
    out = x_ref[...] + (ybuf[slot, 0].reshape(tm, D) * g0 + ybuf[slot, 1].reshape(tm, D) * g1)
    if final_norm:
        out = _rms(out, g_ref[...])
    if t_first is None:
        o_refs[0][...] = out
    else:
        @pl.when(i < t_first)
        def _():
            o_refs[0][...] = out

        @pl.when(i >= t_first)
        def _():
            o_refs[1][...] = out


def _combine(dest_flat, x, route, g, y_rows, *, tm, final_norm, n_first=None):
    N, D = x.shape
    nt = N // tm
    row = lambda wdt: pl.BlockSpec((tm, wdt), lambda i: (i, 0))
    nxt = lambda i: jnp.minimum(i + 1, nt - 1)
    smem = lambda f: pl.BlockSpec((tm,), f, memory_space=pltpu.SMEM)
    if n_first is None:
        t_first, out_shape, out_specs = None, jax.ShapeDtypeStruct((N, D), F32), row(D)
    else:
        t_first = n_first // tm
        out_shape = (jax.ShapeDtypeStruct((n_first, D), F32), jax.ShapeDtypeStruct((N - n_first, D), F32))
        out_specs = (pl.BlockSpec((tm, D), lambda i: (jnp.minimum(i, t_first - 1), 0)),
                     pl.BlockSpec((tm, D), lambda i: (jnp.maximum(i - t_first, 0), 0)))
    return pl.pallas_call(
        functools.partial(_combine_kernel, tm=tm, final_norm=final_norm, t_first=t_first),
        out_shape=out_shape, grid=(nt,),
        in_specs=[smem(lambda i: (i,)), smem(lambda i: (i + nt,)),
                  smem(lambda i: (nxt(i),)), smem(lambda i: (nxt(i) + nt,)),
                  row(D), pl.BlockSpec((ROUTE_ROWS, tm), lambda i: (0, i)), pl.BlockSpec((1, D), lambda i: (0, 0)),
                  pl.BlockSpec(memory_space=pl.ANY)],
        out_specs=out_specs,
        scratch_shapes=[pltpu.VMEM((2, 2, tm // SUBLANES, SUBLANES, D), F32), pltpu.SemaphoreType.DMA((2,))],
        compiler_params=pltpu.CompilerParams(dimension_semantics=("arbitrary",), vmem_limit_bytes=VMEM_LIMIT),
        name="combine",
    )(dest_flat, dest_flat, dest_flat, dest_flat, x, route, g, y_rows)


def _hier_moe(x, g_ffn, w_rg, w_re, w_gate, w_up, w_down, layer, g_final, *, final_norm, n_first=None):
    N, D = x.shape
    tm = TOKEN_TILE
    wr_t = jnp.concatenate([w_rg.T, w_re.T, jnp.zeros((LANES - N_LOGITS, D), F32)], axis=0).astype(BF16)
    g_ffn = g_ffn.reshape(1, D)
    route, counts = _router(x, g_ffn, wr_t, tm=tm)
    counts = counts[:, 0].astype(I32)
    eid_t = route[ROUTE_E0:ROUTE_E1 + 1].astype(I32)
    rank_t = route[ROUTE_R0:ROUTE_R1 + 1].astype(I32)
    padded = (counts + MOE_BLOCK - 1) // MOE_BLOCK * MOE_BLOCK
    pad_end = jnp.cumsum(padded)
    pad_start = pad_end - padded
    expert_ids = jnp.arange(N_EXPERTS, dtype=I32)
    start_of = jnp.sum(jnp.where(eid_t[:, :, None] == expert_ids, pad_start, 0), axis=-1)
    dest_t = start_of + rank_t
    n_blocks = -(-(2 * N) // MOE_BLOCK) + N_EXPERTS
    n_rows = n_blocks * MOE_BLOCK
    n_used = (pad_end[-1] // MOE_BLOCK).astype(I32)
    block_start = jnp.minimum(jnp.arange(n_blocks, dtype=I32), n_used - 1) * MOE_BLOCK
    block_e = jnp.minimum(jnp.sum(block_start[:, None] >= pad_end[None, :], axis=1), N_EXPERTS - 1).astype(I32)
    pad_info = jnp.stack([pad_end - MOE_BLOCK, padded - counts, jnp.broadcast_to(n_used, (N_EXPERTS,))])
    dest_flat = dest_t.reshape(-1)
    x_rows = _dispatch(dest_flat, pad_info, x, g_ffn, tm=tm, n_rows=n_rows)
    y_rows = _experts(block_e, n_used.reshape(1), x_rows, w_gate, w_up, w_down, layer)
    return _combine(dest_flat, x, route, g_final.reshape(1, D), y_rows, tm=COMBINE_TILE, final_norm=final_norm,
                    n_first=n_first)


def _rope_tables(pos, rot_lanes):
    inv = 1.0 / (ROPE_THETA ** (jnp.arange(ROT_HALF, dtype=F32) / ROT_HALF))
    ang = pos.astype(F32)[:, None] * inv[None, :]
    cos, sin = jnp.cos(ang), jnp.sin(ang)
    lane = jnp.arange(LANES)
    r = lane % HEAD_DIM
    active = (lane < rot_lanes)
    first = active & (r < ROT_HALF)
    second = active & (r >= ROT_HALF) & (r < 2 * ROT_HALF)
    cos_l = cos[:, r % ROT_HALF]
    sin_l = sin[:, r % ROT_HALF]
    c = jnp.where((first | second)[None, :], cos_l, 1.0)
    sa = jnp.where(first[None, :], -sin_l, 0.0)
    sb = jnp.where(second[None, :], sin_l, 0.0)
    return jnp.stack([c, sa, sb]).astype(F32)


def _prep_w_in(w_in):
    D = w_in.shape[0]
    assert w_in.shape[1] == sum(IN_SIZES)
    pad = jnp.zeros((D, IN_WIDTH_PAD - w_in.shape[1]), w_in.dtype)
    return jnp.concatenate([w_in, pad], axis=1).astype(BF16)


def kernel(x_prompt, x_sample, cache_k, cache_v, cache_idx_k, state_pool, norm_mix, norm_ffn, norm_final,
           par_w_in, par_pool_w, par_pool_scale, par_w_out, gm_w_in, gm_ln_g, gm_ln_b, gm_ws, gm_bs, gm_w_out,
           moe_router_group, moe_router_expert, moe_w_gate, moe_w_up, moe_w_down):
    Bp, Tp, D = x_prompt.shape
    Bs, Ts, _ = x_sample.shape
    past = cache_k.shape[2]
    Np, Ns = Bp * Tp, Bs * Ts
    depth = norm_mix.shape[0]
    assert depth == 2 and Ts == CHUNK and Tp % PROJ_TILE == 0 and Tp % QUERY_TILE == 0 and Tp % TOKEN_TILE == 0 and Ns % TOKEN_TILE == 0

    w_in_bf = _prep_w_in(par_w_in[0])
    pw_bf = par_pool_w[0].astype(BF16)
    ps = par_pool_scale[0].reshape(1, POOL_WIDTH)
    wo_bf = par_w_out[0].astype(BF16)
    g_mix0 = norm_mix[0].reshape(1, D)
    pos_p = jnp.arange(Tp, dtype=I32)
    pos_s = past + jnp.arange(Ts, dtype=I32)

    hist_p = jnp.zeros((Bp, HIST_ROWS, POOL_WIDTH), F32)
    hist_s = jnp.pad(state_pool[0], ((0, 0), (1, 0), (0, 0)))
    tk = KEY_BLOCK
    (q_p, qi_p, k_p, v_p, kiwi_p, kbf_p, vbf_p, kibf_p, yp_p, st_p) = _inproj(
        x_prompt, g_mix0, w_in_bf, _rope_tables(pos_p, LANES), _rope_tables(pos_p, IDX_DIM), hist_p, pw_bf, ps,
        tm=PROJ_TILE, pos0=0)
    (q_s, qi_s, k_s, v_s, kiwi_s, kbf_s, vbf_s, kibf_s, yp_s, st_s) = _inproj(
        x_sample, g_mix0, w_in_bf, _rope_tables(pos_s, LANES), _rope_tables(pos_s, IDX_DIM), hist_s, pw_bf, ps,
        tm=Ts, pos0=past)

    ls = past + Ts
    n_blk = -(-ls // tk)
    tk_s = -(-ls // (n_blk * LANES)) * LANES
    lsp = n_blk * tk_s
    kpad = lambda a, ax: jnp.pad(a, [(0, lsp - ls) if d == ax else (0, 0) for d in range(a.ndim)])
    kall = kpad(jnp.concatenate([cache_k[0].reshape(Bs, past, KV_WIDTH).astype(BF16), kbf_s], axis=1), 1)
    vall = kpad(jnp.concatenate([cache_v[0].reshape(Bs, past, KV_WIDTH).astype(BF16), vbf_s], axis=1), 1)
    vall_t = jnp.transpose(vall.reshape(Bs, lsp // LANES, LANES, KV_WIDTH), (0, 1, 3, 2))
    kiall = kpad(jnp.concatenate([cache_idx_k[0].astype(BF16), kibf_s], axis=1), 1)
    qpad = lambda a, ax: jnp.pad(a, [(0, LANES - Ts) if d == ax else (0, 0) for d in range(a.ndim)])
    x1_s = _dsa(qpad(q_s, 2), qpad(qi_s, 2), qpad(kiwi_s, 1), kall, vall_t, kiall, qpad(x_sample, 1), qpad(yp_s, 1),
                wo_bf, tq=LANES, tk=tk_s, pos0=past, n_keys=ls, n_sel=min(TOPK_MAX, ls // 4))[:, :Ts]
    assert Tp % tk == 0
    x = jnp.pad(x1_s.reshape(Ns, D), ((Np, 0), (0, 0)))
    x = _dsa(q_p, qi_p, kiwi_p, kbf_p, vbf_p, kibf_p, x_prompt, yp_p, wo_bf,
             tq=QUERY_TILE, tk=tk, pos0=0, n_keys=Tp, n_sel=min(TOPK_MAX, Tp // 4), out_base=x)
    x = _hier_moe(x, norm_ffn[0], moe_router_group[0], moe_router_expert[0],
                  moe_w_gate, moe_w_up, moe_w_down, 0, norm_final, final_norm=False)

    cs = Ts
    tril = lambda n: jnp.tril(jnp.ones((n, n), bool))
    ws_p = jnp.where(tril(GM_CHUNK)[None], gm_ws[0], 0.0)
    ws_small = jnp.where(tril(cs)[None], gm_ws[0][:, :cs, :cs], 0.0)
    rep = GM_CHUNK // cs
    ws_s = jnp.einsum('ab,gts->gatbs', jnp.eye(rep, dtype=F32), ws_small).reshape(GM_GROUPS, GM_CHUNK, GM_CHUNK)
    ws2 = jnp.stack([ws_p, ws_s]).astype(BF16)
    gd = D // GM_GROUPS
    bias_p = jnp.repeat(jnp.transpose(gm_bs[0]), gd, axis=1)
    bias_s = jnp.tile(jnp.repeat(jnp.transpose(gm_bs[0][:, :cs]), gd, axis=1), (rep, 1))
    bias2 = jnp.stack([bias_p, bias_s])
    x, gm_v = _gmlp(x, norm_mix[1].reshape(1, D), gm_w_in[0].astype(BF16), gm_ln_g[0].reshape(1, D),
                    gm_ln_b[0].reshape(1, D), ws2, bias2, gm_w_out[0].astype(BF16),
                    tm=TOKEN_TILE, n_first=Np, n_v_rows=Ns)
    y_p, y_s = _hier_moe(x, norm_ffn[1], moe_router_group[1], moe_router_expert[1],
                         moe_w_gate, moe_w_up, moe_w_down, 1, norm_final, final_norm=True, n_first=Np)
    y_p = y_p.reshape(Bp, Tp, D)
    y_s = y_s.reshape(Bs, Ts, D)
    r4 = lambda a, b, t: a.reshape(1, b, t, N_KV_HEADS, HEAD_DIM)
    return (y_p, y_s,
            r4(k_p, Bp, Tp), r4(v_p, Bp, Tp), kiwi_p[:, :, :IDX_DIM][None], st_p[:, 1:][None],
            r4(k_s, Bs, Ts), r4(v_s, Bs, Ts), kiwi_s[:, :, :IDX_DIM][None], st_s[:, 1:][None],
            gm_v.reshape(1, Bs, Ts, D))
```

```python
import functools

import jax
import jax.numpy as jnp
from jax import lax
from jax.experimental import pallas as pl
from jax.experimental.pallas import tpu as pltpu

F32 = jnp.float32
BF16 = jnp.bfloat16
I32 = jnp.int32
U32 = jnp.uint32

LANES = 128
SUBLANES = 8
CHUNK = 64
POOL_WINDOWS = (2, 4, 8, 16)
POOL_GROUP_DIM = 128
POOL_WIDTH = 512
HIST_ROWS = 16
N_HEADS = 8
HEAD_DIM = 64
N_KV_HEADS = 4
Q_PER_KV = N_HEADS // N_KV_HEADS
ATT_WIDTH = N_HEADS * HEAD_DIM
KV_WIDTH = N_KV_HEADS * HEAD_DIM
N_IDX_HEADS = 8
IDX_DIM = 64
TOPK_MAX = 256
ROPE_THETA = 500000.0
ROT_HALF = HEAD_DIM // 8
GM_CHUNK = 128
GM_GROUPS = 8
N_EXPERT_GROUPS = 4
EXPERTS_PER_GROUP = 8
N_EXPERTS = 32
MOE_BLOCK = 512
RMS_EPS = 1e-6
LN_EPS = 1e-5

INT_MIN = -2147483648
LOG2_E = 1.4426950408889634
NEG_BIG = -1e30
VMEM_LIMIT = 48 * 1024 * 1024

TOKEN_TILE = 1024
COMBINE_TILE = 512
assert TOKEN_TILE % COMBINE_TILE == 0
PROJ_TILE = 1024
KEY_BLOCK = 512
QUERY_TILE = 256
COUNT_SLAB = 64

IN_SIZES = (POOL_WIDTH, ATT_WIDTH, KV_WIDTH, KV_WIDTH, N_IDX_HEADS * IDX_DIM, IDX_DIM, N_IDX_HEADS)
COL_XP, COL_Q, COL_K, COL_V, COL_QI, COL_KIWI = (sum(IN_SIZES[:i]) for i in range(6))
IN_WIDTH_PAD = COL_KIWI + LANES


def _rms(x, g):
    return x * lax.rsqrt(jnp.mean(x * x, axis=-1, keepdims=True) + RMS_EPS) * g


def _rope128(x, c, sa, sb):
    return x * c + pltpu.roll(x, LANES - ROT_HALF, 1) * sa + pltpu.roll(x, ROT_HALF, 1) * sb


def _inproj_kernel(x_ref, g_ref, w_ref, rope_ref, ropeki_ref, hist_ref, pw_ref, ps_ref,
                   q_ref, qi_ref, k_ref, v_ref, kiwi_ref, kbf_ref, vbf_ref, kibf_ref, yp_ref, state_ref,
                   buf_ref, *, tm, pos0):
    j = pl.program_id(1)
    h = _rms(x_ref[0], g_ref[...])
    proj = jnp.dot(h.astype(BF16), w_ref[...], preferred_element_type=F32)
    c, sa, sb = rope_ref[0], rope_ref[1], rope_ref[2]

    def put_heads(ref, i, chunk):
        ref[0, 2 * i] = chunk[:, :HEAD_DIM].astype(BF16)
        ref[0, 2 * i + 1] = chunk[:, HEAD_DIM:].astype(BF16)

    for i in range(ATT_WIDTH // LANES):
        put_heads(q_ref, i, _rope128(proj[:, COL_Q + i * LANES:COL_Q + (i + 1) * LANES], c, sa, sb)
                  * (HEAD_DIM ** -0.5 * LOG2_E))
        put_heads(qi_ref, i, _rope128(proj[:, COL_QI + i * LANES:COL_QI + (i + 1) * LANES], c, sa, sb))
    for i in range(KV_WIDTH // LANES):
        sl = slice(i * LANES, (i + 1) * LANES)
        kr = _rope128(proj[:, COL_K + i * LANES:COL_K + (i + 1) * LANES], c, sa, sb)
        k_ref[0, :, sl] = kr
        kbf_ref[0, :, sl] = kr.astype(BF16)
    vv = proj[:, COL_V:COL_V + KV_WIDTH]
    v_ref[0] = vv
    if tm % LANES == 0:
        for cc in range(tm // LANES):
            vbf_ref[0, cc] = jnp.transpose(vv[cc * LANES:(cc + 1) * LANES, :]).astype(BF16)
    else:
        vbf_ref[0] = vv.astype(BF16)
    kiwi = _rope128(proj[:, COL_KIWI:COL_KIWI + LANES], ropeki_ref[0], ropeki_ref[1], ropeki_ref[2])
    kiwi_ref[0] = kiwi
    kibf_ref[0] = kiwi[:, :IDX_DIM].astype(BF16)

    @pl.when(j == 0)
    def _():
        buf_ref[0:HIST_ROWS, :] = hist_ref[0]

    xp = proj[:, COL_XP:COL_XP + POOL_WIDTH]
    buf_ref[HIST_ROWS:HIST_ROWS + tm, :] = xp
    pos = pos0 + j * tm + lax.broadcasted_iota(I32, (tm, 1), 0)
    for gi, w in enumerate(POOL_WINDOWS):
        c0 = gi * POOL_GROUP_DIM
        s = xp[:, c0:c0 + POOL_GROUP_DIM]
        for i in range(1, w):
            s = s + buf_ref[HIST_ROWS - i:HIST_ROWS - i + tm, c0:c0 + POOL_GROUP_DIM]
        cnt = jnp.minimum(pos + 1, w).astype(F32)
        d = s / cnt - xp[:, c0:c0 + POOL_GROUP_DIM]
        y = jnp.dot(d.astype(BF16), pw_ref[gi], preferred_element_type=F32)
        yp_ref[0, :, c0:c0 + POOL_GROUP_DIM] = (y * ps_ref[:, c0:c0 + POOL_GROUP_DIM]).astype(BF16)
    tail = buf_ref[tm:tm + HIST_ROWS, :]
    state_ref[0] = tail
    buf_ref[0:HIST_ROWS, :] = tail


def _inproj(x, g, w_bf, rope, ropeki, hist, pw_bf, ps, *, tm, pos0):
    B, T, D = x.shape
    nt = T // tm
    f = lambda shape, dt: jax.ShapeDtypeStruct(shape, dt)
    v_t = tm % LANES == 0
    out_shape = (
        f((B, N_HEADS, T, HEAD_DIM), BF16), f((B, N_IDX_HEADS, T, IDX_DIM), BF16),
        f((B, T, KV_WIDTH), F32), f((B, T, KV_WIDTH), F32), f((B, T, LANES), F32),
        f((B, T, KV_WIDTH), BF16),
        f((B, T // LANES, KV_WIDTH, LANES) if v_t else (B, T, KV_WIDTH), BF16),
        f((B, T, IDX_DIM), BF16),
        f((B, T, POOL_WIDTH), BF16), f((B, HIST_ROWS, POOL_WIDTH), F32),
    )
    tile = lambda wdt: pl.BlockSpec((1, tm, wdt), lambda b, j: (b, j, 0))
    heads = lambda n, wdt: pl.BlockSpec((1, n, tm, wdt), lambda b, j: (b, 0, j, 0))
    const2 = lambda s: pl.BlockSpec(s, lambda b, j: (0, 0))
    in_specs = [
        tile(D), const2((1, D)), const2((D, IN_WIDTH_PAD)),
        pl.BlockSpec((3, tm, LANES), lambda b, j: (0, j, 0)),
        pl.BlockSpec((3, tm, LANES), lambda b, j: (0, j, 0)),
        pl.BlockSpec((1, HIST_ROWS, POOL_WIDTH), lambda b, j: (b, 0, 0)),
        pl.BlockSpec((len(POOL_WINDOWS), POOL_GROUP_DIM, POOL_GROUP_DIM), lambda b, j: (0, 0, 0)),
        const2((1, POOL_WIDTH)),
    ]
    out_specs = (
        heads(N_HEADS, HEAD_DIM), heads(N_IDX_HEADS, IDX_DIM), tile(KV_WIDTH), tile(KV_WIDTH), tile(LANES),
        tile(KV_WIDTH),
        pl.BlockSpec((1, tm // LANES, KV_WIDTH, LANES), lambda b, j: (b, j, 0, 0)) if v_t else tile(KV_WIDTH),
        tile(IDX_DIM), tile(POOL_WIDTH),
        pl.BlockSpec((1, HIST_ROWS, POOL_WIDTH), lambda b, j: (b, 0, 0)),
    )
    return pl.pallas_call(
        functools.partial(_inproj_kernel, tm=tm, pos0=pos0),
        out_shape=out_shape, grid=(B, nt), in_specs=in_specs, out_specs=out_specs,
        scratch_shapes=[pltpu.VMEM((HIST_ROWS + tm, POOL_WIDTH), F32)],
        compiler_params=pltpu.CompilerParams(dimension_semantics=("parallel", "arbitrary"),
                                             vmem_limit_bytes=VMEM_LIMIT),
        name="inproj",
    )(x, g, w_bf, rope, ropeki, hist, pw_bf, ps)


def _dsa_kernel(q_ref, qi_ref, kiwi_ref, k_ref, vt_ref, ki_ref, x_ref, yp_ref, wo_ref, *rest,
                tq, tk, pos0, n_keys, n_sel):
    o_ref, key_buf, bias_buf, m_scr, l_scr, acc_scr, s_scr = rest[-7:]
    slab = COUNT_SLAB
    j = pl.program_id(1)
    base = pos0 + j * tq
    pos = base + lax.broadcasted_iota(I32, (1, tq), 1)
    limit = jnp.minimum((pos // CHUNK + 1) * CHUNK, n_keys)
    limit_max = jnp.minimum(((base + tq - 1) // CHUNK + 1) * CHUNK, n_keys)
    nkb = (limit_max + tk - 1) // tk
    nt = (((1,), (1,)), ((), ()))

    wi_t = jnp.transpose(kiwi_ref[0])[IDX_DIM:IDX_DIM + N_IDX_HEADS, :]

    def score_pair(i, carry):
        blocks = (2 * i, jnp.minimum(2 * i + 1, nkb - 1))
        scores = []
        for kb in blocks:
            off = pl.multiple_of(kb * tk, tk)
            kiblk = ki_ref[0, pl.ds(off, tk), :]
            scores.append([lax.dot_general(kiblk, qi_ref[0, h], nt, preferred_element_type=F32)
                           for h in range(N_IDX_HEADS)])
        for kb, sc_h in zip(blocks, scores):
            idx = jnp.zeros((tk, tq), F32)
            for h in range(N_IDX_HEADS):
                idx = idx + jnp.maximum(sc_h[h], 0.0) * wi_t[h:h + 1, :]
            bits = lax.bitcast_convert_type(idx, I32)
            sign = bits >> 31
            key = ((bits & 0x7FFFFFFF) ^ sign) - sign
            key_buf[kb] = jnp.where(lax.broadcasted_iota(I32, (tk, tq), 0) < limit - kb * tk, key, INT_MIN)
        return carry

    lax.fori_loop(0, (nkb + 1) // 2, score_pair, 0)

    def col_sum(a):
        return jnp.sum(a, axis=0, keepdims=True)

    def count_ge(cand):
        def body(kb, acc):
            kblk = key_buf[kb]
            for c in range(tk // slab):
                acc = acc + jnp.where(kblk[c * slab:(c + 1) * slab] >= cand, 1.0, 0.0)
            return acc
        return col_sum(lax.fori_loop(0, nkb, body, jnp.zeros((slab, tq), F32)))

    kf = float(n_sel)
    cnt0 = count_ge(jnp.zeros((1, tq), I32))
    t0 = jnp.where(cnt0 >= kf, 0, INT_MIN).astype(I32)

    def bit_body(i, carry):
        t, cnt = carry
        cand = t | lax.shift_left(jnp.int32(1), 30 - i)
        cnt_cand = count_ge(cand)
        keep = cnt_cand >= kf
        return jnp.where(keep, cand, t), jnp.where(keep, cnt_cand, cnt)

    t, cnt_ge = lax.fori_loop(0, 31, bit_body, (t0, cnt0))
    cnt_gt = count_ge(t + 1)
    need = kf - cnt_gt
    cnt_eq = cnt_ge - cnt_gt
    overfull = jnp.where(t != INT_MIN, cnt_eq - need, 0.0)
    slow = jnp.max(overfull) > 0.0

    @pl.when(jnp.logical_not(slow))
    def _():
        t_adm = jnp.maximum(t, INT_MIN + 1)

        def body(kb, carry):
            bias_buf[kb] = jnp.where(key_buf[kb] >= t_adm, 0.0, NEG_BIG)
            return carry
        lax.fori_loop(0, nkb, body, 0)

    @pl.when(slow)
    def _():
        tri = jnp.where(lax.broadcasted_iota(I32, (tk, tk), 1) <= lax.broadcasted_iota(I32, (tk, tk), 0),
                        1.0, 0.0).astype(BF16)

        def body(kb, seen):
            kblk = key_buf[kb]
            eq = jnp.where((kblk == t) & (kblk != INT_MIN), 1.0, 0.0)
            prefix = jnp.dot(tri, eq.astype(BF16), preferred_element_type=F32) + seen
            keep_tie = jnp.where(prefix <= need, eq, 0.0)
            sel = jnp.where(kblk > t, 1.0, keep_tie)
            bias_buf[kb] = jnp.where(sel > 0.0, 0.0, NEG_BIG)
            return seen + col_sum(eq)
        lax.fori_loop(0, nkb, body, jnp.zeros((1, tq), F32))

    sub = LANES
    nsub = tk // sub
    tl = LANES
    n_tiles = tq // tl
    m_scr[...] = jnp.full(m_scr.shape, NEG_BIG, F32)
    l_scr[...] = jnp.zeros(l_scr.shape, F32)
    acc_scr[...] = jnp.zeros(acc_scr.shape, F32)

    def tile_blocks(hq):
        return (jnp.minimum(((base + (hq + 1) * tl - 1) // CHUNK + 1) * CHUNK, n_keys) + tk - 1) // tk

    def attn_block(kb, members):
        for mi, hq in enumerate(members):
            for c in range(nsub):
                off = pl.multiple_of(kb * tk + c * sub, sub)
                for g in range(N_KV_HEADS):
                    s_scr[mi, c, g] = lax.dot_general(
                        k_ref[0, pl.ds(off, sub), g * HEAD_DIM:(g + 1) * HEAD_DIM],
                        q_ref[0, Q_PER_KV * g:Q_PER_KV * (g + 1), hq * tl:(hq + 1) * tl].reshape(Q_PER_KV * tl, HEAD_DIM),
                        nt, preferred_element_type=F32)
        for mi, hq in enumerate(members):
            m = [m_scr[hq, g] for g in range(N_KV_HEADS)]
            l = [l_scr[hq, g] for g in range(N_KV_HEADS)]
            for c in range(nsub):
                bias = bias_buf[kb, c * sub:(c + 1) * sub, hq * tl:(hq + 1) * tl]
                bias2 = jnp.concatenate([bias] * Q_PER_KV, axis=1)
                for g in range(N_KV_HEADS):
                    s = s_scr[mi, c, g] + bias2
                    m_new = jnp.maximum(m[g], jnp.max(s, axis=0, keepdims=True))
                    alpha = jnp.exp2(m[g] - m_new)
                    p = jnp.exp2(s - m_new)
                    l[g] = alpha * l[g] + jnp.sum(p.reshape(sub // SUBLANES, SUBLANES, Q_PER_KV * tl), axis=0)
                    pv = jnp.dot(vt_ref[0, kb * nsub + c, g * HEAD_DIM:(g + 1) * HEAD_DIM, :], p.astype(BF16),
                                 preferred_element_type=F32)
                    acc_scr[hq, g] = alpha * acc_scr[hq, g] + pv
                    m[g] = m_new
            for g in range(N_KV_HEADS):
                m_scr[hq, g] = m[g]
                l_scr[hq, g] = l[g]

    for first in range(0, n_tiles, 2):
        members = tuple(range(first, min(first + 2, n_tiles)))
        shared = tile_blocks(members[0])
        lax.fori_loop(0, shared, lambda kb, c, ms=members: (attn_block(kb, ms), c)[1], 0)
        if len(members) == 2:
            lax.fori_loop(shared, tile_blocks(members[1]),
                          lambda kb, c, ms=members[1:]: (attn_block(kb, ms), c)[1], 0)

    y_att = []
    for hq in range(n_tiles):
        o_t = []
        for g in range(N_KV_HEADS):
            og = acc_scr[hq, g] / col_sum(l_scr[hq, g])
            o_t.extend(og[:, hh * tl:(hh + 1) * tl] for hh in range(Q_PER_KV))
        y_att.append(jnp.transpose(jnp.concatenate(o_t, axis=0)).astype(BF16))
    y_att = jnp.concatenate(y_att, axis=0)

    y = jnp.dot(yp_ref[0], wo_ref[0:POOL_WIDTH, :], preferred_element_type=F32)
    y = y + jnp.dot(y_att, wo_ref[POOL_WIDTH:POOL_WIDTH + ATT_WIDTH, :], preferred_element_type=F32)
    o_ref[...] = (x_ref[0] + y).reshape(o_ref.shape)


def _dsa(q_hm, qi_hm, kiwi, k_all, v_t, ki_all, x, yp, wo_bf, *, tq, tk, pos0, n_keys, n_sel, out_base=None):
    B, T, D = x.shape
    L = k_all.shape[1]
    assert L % tk == 0 and T % tq == 0 and tq % LANES == 0
    nkb_max = L // tk
    tile = lambda wdt: pl.BlockSpec((1, tq, wdt), lambda b, j: (b, j, 0))
    in_specs = [pl.BlockSpec((1, N_HEADS, tq, HEAD_DIM), lambda b, j: (b, 0, j, 0)),
                pl.BlockSpec((1, N_IDX_HEADS, tq, IDX_DIM), lambda b, j: (b, 0, j, 0)),
                tile(LANES),
                pl.BlockSpec((1, L, KV_WIDTH), lambda b, j: (b, 0, 0)),
                pl.BlockSpec((1, L // LANES, KV_WIDTH, LANES), lambda b, j: (b, 0, 0, 0)),
                pl.BlockSpec((1, L, IDX_DIM), lambda b, j: (b, 0, 0)),
                tile(D), tile(POOL_WIDTH), pl.BlockSpec((POOL_WIDTH + ATT_WIDTH, D), lambda b, j: (0, 0))]
    nq = T // tq
    args = (q_hm, qi_hm, kiwi, k_all, v_t, ki_all, x, yp, wo_bf)
    if out_base is None:
        out_shape, out_spec, aliases = jax.ShapeDtypeStruct((B, T, D), F32), tile(D), {}
    else:
        out_shape = jax.ShapeDtypeStruct(out_base.shape, F32)
        out_spec = pl.BlockSpec((tq, D), lambda b, j: (b * nq + j, 0))
        in_specs, args, aliases = in_specs + [pl.BlockSpec(memory_space=pl.ANY)], args + (out_base,), {len(args): 0}
    return pl.pallas_call(
        functools.partial(_dsa_kernel, tq=tq, tk=tk, pos0=pos0, n_keys=n_keys, n_sel=n_sel),
        out_shape=out_shape, grid=(B, nq), in_specs=in_specs, out_specs=out_spec, input_output_aliases=aliases,
        scratch_shapes=[pltpu.VMEM((nkb_max, tk, tq), I32), pltpu.VMEM((nkb_max, tk, tq), F32),
                        pltpu.VMEM((tq // LANES, N_KV_HEADS, 1, Q_PER_KV * LANES), F32),
                        pltpu.VMEM((tq // LANES, N_KV_HEADS, SUBLANES, Q_PER_KV * LANES), F32),
                        pltpu.VMEM((tq // LANES, N_KV_HEADS, HEAD_DIM, Q_PER_KV * LANES), F32),
                        pltpu.VMEM((2, tk // LANES, N_KV_HEADS, LANES, Q_PER_KV * LANES), F32)],
        compiler_params=pltpu.CompilerParams(dimension_semantics=("parallel", "arbitrary"),
                                             vmem_limit_bytes=VMEM_LIMIT),
        name="dsa",
    )(*args)


def _gmlp_kernel(x_ref, g_ref, win_ref, lng_ref, lnb_ref, ws_ref, bias_ref, wout_ref, o_ref, v_ref, *, tm):
    x = x_ref[...]
    h = _rms(x, g_ref[...])
    z = jax.nn.gelu(jnp.dot(h.astype(BF16), win_ref[...], preferred_element_type=F32))
    half = z.shape[1] // 2
    u, v = z[:, :half], z[:, half:]
    mu = jnp.mean(v, axis=-1, keepdims=True)
    var = jnp.mean(jnp.square(v - mu), axis=-1, keepdims=True)
    vn = (v - mu) * lax.rsqrt(var + LN_EPS) * lng_ref[...] + lnb_ref[...]
    v_ref[...] = vn
    gd = half // GM_GROUPS
    gated = []
    for c in range(tm // GM_CHUNK):
        rows = slice(c * GM_CHUNK, (c + 1) * GM_CHUNK)
        vc = vn[rows].astype(BF16)
        mixed = jnp.concatenate(
            [jnp.dot(ws_ref[0, g], vc[:, g * gd:(g + 1) * gd], preferred_element_type=F32)
             for g in range(GM_GROUPS)], axis=1) + bias_ref[0]
        gated.append((u[rows] * mixed).astype(BF16))
    gated = jnp.concatenate(gated, axis=0)
    o_ref[...] = x + jnp.dot(gated, wout_ref[...], preferred_element_type=F32)


def _gmlp(x, g, win_bf, lng, lnb, ws2, bias2, wout_bf, *, tm, n_first, n_v_rows):
    N, D = x.shape
    half = win_bf.shape[1] // 2
    nt = N // tm
    t_first = n_first // tm
    variant = lambda i: jnp.where(i >= t_first, 1, 0)
    row = pl.BlockSpec((tm, D), lambda i: (i, 0))
    const = lambda s: pl.BlockSpec(s, lambda i: (0, 0))
    in_specs = [row, const((1, D)), const((D, 2 * half)), const((1, half)), const((1, half)),
                pl.BlockSpec((1, GM_GROUPS, GM_CHUNK, GM_CHUNK), lambda i: (variant(i), 0, 0, 0)),
                pl.BlockSpec((1, GM_CHUNK, half), lambda i: (variant(i), 0, 0)),
                const((half, D))]
    out_specs = (row, pl.BlockSpec((tm, half), lambda i: (jnp.maximum(i - t_first, 0), 0)))
    return pl.pallas_call(
        functools.partial(_gmlp_kernel, tm=tm),
        out_shape=(jax.ShapeDtypeStruct((N, D), F32), jax.ShapeDtypeStruct((n_v_rows, half), F32)),
        grid=(nt,), in_specs=in_specs, out_specs=out_specs,
        compiler_params=pltpu.CompilerParams(dimension_semantics=("arbitrary",), vmem_limit_bytes=VMEM_LIMIT),
        name="gmlp",
    )(x, g, win_bf, lng, lnb, ws2, bias2, wout_bf)


ROUTE_E0, ROUTE_E1, ROUTE_R0, ROUTE_R1, ROUTE_G0, ROUTE_G1 = range(6)
ROUTE_ROWS = 8
N_LOGITS = N_EXPERT_GROUPS + N_EXPERTS
LOGIT_ROWS = -(-N_LOGITS // SUBLANES) * SUBLANES


def _router_kernel(x_ref, g_ref, wr_ref, route_ref, cnt_ref, carry_ref, *, tm):
    i = pl.program_id(0)

    @pl.when(i == 0)
    def _():
        carry_ref[...] = jnp.zeros(carry_ref.shape, F32)

    h = _rms(x_ref[...], g_ref[...])
    nt = (((1,), (1,)), ((), ()))
    logits = lax.dot_general(wr_ref[...], h.astype(BF16), nt, preferred_element_type=F32)[:LOGIT_ROWS]
    row = lax.broadcasted_iota(I32, (LOGIT_ROWS, tm), 0).astype(F32)
    ninf = -jnp.inf
    big = float(LANES)
    cmax = lambda a: jnp.max(a, axis=0, keepdims=True)
    cmin = lambda a: jnp.min(a, axis=0, keepdims=True)
    csum = lambda a: jnp.sum(a, axis=0, keepdims=True)

    is_grp = row < N_EXPERT_GROUPS
    lg = jnp.where(is_grp, logits, ninf)
    mg = cmax(lg)
    g_sel = cmin(jnp.where(lg == mg, row, big))
    p_grp = 1.0 / csum(jnp.where(is_grp, jnp.exp(lg - mg), 0.0))
    lo = N_EXPERT_GROUPS + g_sel * EXPERTS_PER_GROUP
    le = jnp.where((row >= lo) & (row < lo + EXPERTS_PER_GROUP), logits, ninf)
    v1 = cmax(le)
    j1 = cmin(jnp.where(le == v1, row, big))
    le2 = jnp.where(row == j1, ninf, le)
    v2 = cmax(le2)
    j2 = cmin(jnp.where(le2 == v2, row, big))
    e0 = j1 - N_EXPERT_GROUPS
    e1 = j2 - N_EXPERT_GROUPS
    r = jnp.exp(v2 - v1)
    g0 = p_grp / (1.0 + r)
    g1 = p_grp * r / (1.0 + r)

    erow = lax.broadcasted_iota(I32, (N_EXPERTS, tm), 0).astype(F32)
    oh0 = jnp.where(erow == e0, 1.0, 0.0)
    oh1 = jnp.where(erow == e1, 1.0, 0.0)
    oh = oh0 + oh1
    earlier = jnp.where(lax.broadcasted_iota(I32, (tm, tm), 0) < lax.broadcasted_iota(I32, (tm, tm), 1),
                        1.0, 0.0).astype(BF16)
    before = jnp.dot(oh.astype(BF16), earlier, preferred_element_type=F32) + carry_ref[...]
    r0 = csum(oh0 * before)
    r1 = csum(oh1 * before)
    carry_ref[...] = carry_ref[...] + jnp.sum(oh, axis=1, keepdims=True)
    cnt_ref[...] = jnp.broadcast_to(carry_ref[...], cnt_ref.shape)

    rows = {ROUTE_E0: e0, ROUTE_E1: e1, ROUTE_R0: r0, ROUTE_R1: r1, ROUTE_G0: g0, ROUTE_G1: g1}
    zero = jnp.zeros((1, tm), F32)
    route_ref[...] = jnp.concatenate([rows.get(k, zero) for k in range(ROUTE_ROWS)], axis=0)


def _router(x, g, wr_t, *, tm):
    N, D = x.shape
    const = lambda s: pl.BlockSpec(s, lambda i: (0, 0))
    return pl.pallas_call(
        functools.partial(_router_kernel, tm=tm),
        out_shape=(jax.ShapeDtypeStruct((ROUTE_ROWS, N), F32), jax.ShapeDtypeStruct((N_EXPERTS, LANES), F32)),
        grid=(N // tm,), in_specs=[pl.BlockSpec((tm, D), lambda i: (i, 0)), const((1, D)), const((LANES, D))],
        out_specs=(pl.BlockSpec((ROUTE_ROWS, tm), lambda i: (0, i)), const((N_EXPERTS, LANES))),
        scratch_shapes=[pltpu.VMEM((N_EXPERTS, 1), F32)],
        compiler_params=pltpu.CompilerParams(dimension_semantics=("arbitrary",), vmem_limit_bytes=VMEM_LIMIT),
        name="router",
    )(x, g, wr_t)


def _pack_bf16_pairs(h):
    half = h.shape[1] // 2
    lo = lax.bitcast_convert_type(h[:, :half].astype(BF16).astype(F32), U32)
    hi = lax.bitcast_convert_type(h[:, half:].astype(BF16).astype(F32), U32)
    return (lo >> 16) | (hi & jnp.uint32(0xFFFF0000))


def _unpack_bf16_pairs(w):
    lo = lax.bitcast_convert_type(w << 16, F32).astype(BF16)
    hi = lax.bitcast_convert_type(w & jnp.uint32(0xFFFF0000), F32).astype(BF16)
    return jnp.concatenate([lo, hi], axis=1)


def _dispatch_kernel(d0_ref, d1_ref, pad_ref, x_ref, g_ref, rows_ref, h_scr, zblk, sems, *, tm):
    i = pl.program_id(0)
    n = pl.num_programs(0)
    slot = i % 2
    groups = tm // SUBLANES

    def wait_slot(s):
        for _ in range(2 * groups):
            pltpu.make_async_copy(h_scr.at[s, 0], rows_ref.at[pl.ds(0, SUBLANES)], sems.at[s]).wait()

    @pl.when(i == 0)
    def _():
        zblk[...] = jnp.zeros(zblk.shape, U32)
        n_blocks = rows_ref.shape[0] // MOE_BLOCK
        n_used = pad_ref[2, 0]

        def blk_copy(row0):
            return pltpu.make_async_copy(zblk, rows_ref.at[pl.ds(row0, MOE_BLOCK)], sems.at[2])

        for wait in (False, True):
            for e in range(N_EXPERTS):
                @pl.when(pad_ref[1, e] > 0)
                def _():
                    cp = blk_copy(pl.multiple_of(pad_ref[0, e], MOE_BLOCK))
                    cp.wait() if wait else cp.start()

            def unused(b, carry):
                cp = blk_copy(pl.multiple_of(b * MOE_BLOCK, MOE_BLOCK))
                cp.wait() if wait else cp.start()
                return carry
            lax.fori_loop(n_used, n_blocks, unused, 0)

    @pl.when(i >= 2)
    def _():
        wait_slot(slot)

    h_scr[slot] = _pack_bf16_pairs(_rms(x_ref[...], g_ref[...])).reshape(groups, SUBLANES, h_scr.shape[3])

    def body(k, carry):
        for u in range(SUBLANES):
            r = k * SUBLANES + u
            src = h_scr.at[slot, k, pl.ds(u, 1)]
            pltpu.make_async_copy(src, rows_ref.at[pl.ds(d0_ref[r], 1)], sems.at[slot]).start()
            pltpu.make_async_copy(src, rows_ref.at[pl.ds(d1_ref[r], 1)], sems.at[slot]).start()
        return carry

    lax.fori_loop(0, groups, body, 0)

    @pl.when(i == n - 1)
    def _():
        wait_slot(slot)

    @pl.when(jnp.logical_and(i == n - 1, n >= 2))
    def _():
        wait_slot(1 - slot)


def _dispatch(dest_flat, pad_info, x, g, *, tm, n_rows):
    N, D = x.shape
    nt = N // tm
    return pl.pallas_call(
        functools.partial(_dispatch_kernel, tm=tm),
        out_shape=jax.ShapeDtypeStruct((n_rows, D // 2), U32), grid=(nt,),
        in_specs=[pl.BlockSpec((tm,), lambda i: (i,), memory_space=pltpu.SMEM),
                  pl.BlockSpec((tm,), lambda i: (i + nt,), memory_space=pltpu.SMEM),
                  pl.BlockSpec(memory_space=pltpu.SMEM),
                  pl.BlockSpec((tm, D), lambda i: (i, 0)),
                  pl.BlockSpec((1, D), lambda i: (0, 0))],
        out_specs=pl.BlockSpec(memory_space=pl.ANY),
        scratch_shapes=[pltpu.VMEM((2, tm // SUBLANES, SUBLANES, D // 2), U32), pltpu.VMEM((MOE_BLOCK, D // 2), U32),
                        pltpu.SemaphoreType.DMA((3,))],
        compiler_params=pltpu.CompilerParams(dimension_semantics=("arbitrary",), vmem_limit_bytes=VMEM_LIMIT),
        name="dispatch",
    )(dest_flat, dest_flat, pad_info, x, g)


def _expert_kernel(be_ref, nu_ref, x_ref, wg_ref, wu_ref, wd_ref, y_ref, wg_bf, wu_bf, wd_bf):
    i = pl.program_id(0)

    @pl.when(jnp.logical_or(i == 0, be_ref[i] != be_ref[jnp.maximum(i - 1, 0)]))
    def _():
        wg_bf[...] = wg_ref[0, 0].astype(BF16)
        wu_bf[...] = wu_ref[0, 0].astype(BF16)
        wd_bf[...] = wd_ref[0, 0].astype(BF16)

    @pl.when(i < nu_ref[0])
    def _():
        x = _unpack_bf16_pairs(x_ref[...])
        a = jnp.dot(x, wg_bf[...], preferred_element_type=F32)
        u = jnp.dot(x, wu_bf[...], preferred_element_type=F32)
        act = (a * jax.nn.sigmoid(a)) * u
        y_ref[...] = jnp.dot(act.astype(BF16), wd_bf[...], preferred_element_type=F32)

    @pl.when(i >= nu_ref[0])
    def _():
        y_ref[...] = jnp.zeros(y_ref.shape, F32)


def _experts(block_e, n_used, x_rows, w_gate, w_up, w_down, layer):
    n_rows = x_rows.shape[0]
    D, de = w_gate.shape[2:]
    n_blocks = n_rows // MOE_BLOCK
    grid_spec = pltpu.PrefetchScalarGridSpec(
        num_scalar_prefetch=2, grid=(n_blocks,),
        in_specs=[pl.BlockSpec((MOE_BLOCK, D // 2), lambda i, be, nu: (jnp.minimum(i, nu[0] - 1), 0)),
                  pl.BlockSpec((1, 1, D, de), lambda i, be, nu: (layer, be[i], 0, 0)),
                  pl.BlockSpec((1, 1, D, de), lambda i, be, nu: (layer, be[i], 0, 0)),
                  pl.BlockSpec((1, 1, de, D), lambda i, be, nu: (layer, be[i], 0, 0))],
        out_specs=pl.BlockSpec((MOE_BLOCK, D), lambda i, be, nu: (i, 0)),
        scratch_shapes=[pltpu.VMEM((D, de), BF16), pltpu.VMEM((D, de), BF16), pltpu.VMEM((de, D), BF16)])
    return pl.pallas_call(
        _expert_kernel, out_shape=jax.ShapeDtypeStruct((n_rows, D), F32), grid_spec=grid_spec,
        compiler_params=pltpu.CompilerParams(dimension_semantics=("arbitrary",), vmem_limit_bytes=VMEM_LIMIT),
        name="experts",
    )(block_e, n_used, x_rows, w_gate, w_up, w_down)


def _combine_kernel(dc0_ref, dc1_ref, dn0_ref, dn1_ref, x_ref, route_ref, g_ref, y_ref, *rest,
                    tm, final_norm, t_first):
    *o_refs, ybuf, sems = rest
    i = pl.program_id(0)
    n = pl.num_programs(0)
    slot = i % 2
    groups = tm // SUBLANES
    D = x_ref.shape[1]

    def gather(d_refs, s):
        def body(k, carry):
            for u in range(SUBLANES):
                r = k * SUBLANES + u
                for a in range(2):
                    pltpu.make_async_copy(y_ref.at[pl.ds(d_refs[a][r], 1)], ybuf.at[s, a, k, pl.ds(u, 1)],
                                          sems.at[s]).start()
            return carry
        lax.fori_loop(0, groups, body, 0)

    @pl.when(i == 0)
    def _():
        gather((dc0_ref, dc1_ref), 0)

    @pl.when(i + 1 < n)
    def _():
        gather((dn0_ref, dn1_ref), 1 - slot)

    for _ in range(2 * groups):
        pltpu.make_async_copy(y_ref.at[pl.ds(0, SUBLANES)], ybuf.at[slot, 0, 0], sems.at[slot]).wait()
    route = jnp.transpose(route_ref[...])
    g0 = route[:, ROUTE_G0:ROUTE_G0 + 1]
    g1 = route[:, ROUTE_G1:ROUTE_G1 + 1]
    out = x_ref[...] + (ybuf[slot, 0].reshape(tm, D) * g0 + ybuf[slot, 1].reshape(tm, D) * g1)
    if final_norm:
        out = _rms(out, g_ref[...])
    if t_first is None:
        o_refs[0][...] = out
    else:
        @pl.when(i < t_first)
        def _():
            o_refs[0][...] = out

        @pl.when(i >= t_first)
        def _():
            o_refs[1][...] = out


def _combine(dest_flat, x, route, g, y_rows, *, tm, final_norm, n_first=None):
    N, D = x.shape
    nt = N // tm
    row = lambda wdt: pl.BlockSpec((tm, wdt), lambda i: (i, 0))
    nxt = lambda i: jnp.minimum(i + 1, nt - 1)
    smem = lambda f: pl.BlockSpec((tm,), f, memory_space=pltpu.SMEM)
    if n_first is None:
        t_first, out_shape, out_specs = None, jax.ShapeDtypeStruct((N, D), F32), row(D)
    else:
        t_first = n_first // tm
        out_shape = (jax.ShapeDtypeStruct((n_first, D), F32), jax.ShapeDtypeStruct((N - n_first, D), F32))
        out_specs = (pl.BlockSpec((tm, D), lambda i: (jnp.minimum(i, t_first - 1), 0)),
                     pl.BlockSpec((tm, D), lambda i: (jnp.maximum(i - t_first, 0), 0)))
    return pl.pallas_call(
        functools.partial(_combine_kernel, tm=tm, final_norm=final_norm, t_first=t_first),
        out_shape=out_shape, grid=(nt,),
        in_specs=[smem(lambda i: (i,)), smem(lambda i: (i + nt,)),
                  smem(lambda i: (nxt(i),)), smem(lambda i: (nxt(i) + nt,)),
                  row(D), pl.BlockSpec((ROUTE_ROWS, tm), lambda i: (0, i)), pl.BlockSpec((1, D), lambda i: (0, 0)),
                  pl.BlockSpec(memory_space=pl.ANY)],
        out_specs=out_specs,
        scratch_shapes=[pltpu.VMEM((2, 2, tm // SUBLANES, SUBLANES, D), F32), pltpu.SemaphoreType.DMA((2,))],
        compiler_params=pltpu.CompilerParams(dimension_semantics=("arbitrary",), vmem_limit_bytes=VMEM_LIMIT),
        name="combine",
    )(dest_flat, dest_flat, dest_flat, dest_flat, x, route, g, y_rows)


def _hier_moe(x, g_ffn, w_rg, w_re, w_gate, w_up, w_down, layer, g_final, *, final_norm, n_first=None):
    N, D = x.shape
    tm = TOKEN_TILE
    wr_t = jnp.concatenate([w_rg.T, w_re.T, jnp.zeros((LANES - N_LOGITS, D), F32)], axis=0).astype(BF16)
    g_ffn = g_ffn.reshape(1, D)
    route, counts = _router(x, g_ffn, wr_t, tm=tm)
    counts = counts[:, 0].astype(I32)
    eid_t = route[ROUTE_E0:ROUTE_E1 + 1].astype(I32)
    rank_t = route[ROUTE_R0:ROUTE_R1 + 1].astype(I32)
    padded = (counts + MOE_BLOCK - 1) // MOE_BLOCK * MOE_BLOCK
    pad_end = jnp.cumsum(padded)
    pad_start = pad_end - padded
    expert_ids = jnp.arange(N_EXPERTS, dtype=I32)
    start_of = jnp.sum(jnp.where(eid_t[:, :, None] == expert_ids, pad_start, 0), axis=-1)
    dest_t = start_of + rank_t
    n_blocks = -(-(2 * N) // MOE_BLOCK) + N_EXPERTS
    n_rows = n_blocks * MOE_BLOCK
    n_used = (pad_end[-1] // MOE_BLOCK).astype(I32)
    block_start = jnp.minimum(jnp.arange(n_blocks, dtype=I32), n_used - 1) * MOE_BLOCK
    block_e = jnp.minimum(jnp.sum(block_start[:, None] >= pad_end[None, :], axis=1), N_EXPERTS - 1).astype(I32)
    pad_info = jnp.stack([pad_end - MOE_BLOCK, padded - counts, jnp.broadcast_to(n_used, (N_EXPERTS,))])
    dest_flat = dest_t.reshape(-1)
    x_rows = _dispatch(dest_flat, pad_info, x, g_ffn, tm=tm, n_rows=n_rows)
    y_rows = _experts(block_e, n_used.reshape(1), x_rows, w_gate, w_up, w_down, layer)
    return _combine(dest_flat, x, route, g_final.reshape(1, D), y_rows, tm=COMBINE_TILE, final_norm=final_norm,
                    n_first=n_first)


def _rope_tables(pos, rot_lanes):
    inv = 1.0 / (ROPE_THETA ** (jnp.arange(ROT_HALF, dtype=F32) / ROT_HALF))
    ang = pos.astype(F32)[:, None] * inv[None, :]
    cos, sin = jnp.cos(ang), jnp.sin(ang)
    lane = jnp.arange(LANES)
    r = lane % HEAD_DIM
    active = (lane < rot_lanes)
    first = active & (r < ROT_HALF)
    second = active & (r >= ROT_HALF) & (r < 2 * ROT_HALF)
    cos_l = cos[:, r % ROT_HALF]
    sin_l = sin[:, r % ROT_HALF]
    c = jnp.where((first | second)[None, :], cos_l, 1.0)
    sa = jnp.where(first[None, :], -sin_l, 0.0)
    sb = jnp.where(second[None, :], sin_l, 0.0)
    return jnp.stack([c, sa, sb]).astype(F32)


def _prep_w_in(w_in):
    D = w_in.shape[0]
    assert w_in.shape[1] == sum(IN_SIZES)
    pad = jnp.zeros((D, IN_WIDTH_PAD - w_in.shape[1]), w_in.dtype)
    return jnp.concatenate([w_in, pad], axis=1).astype(BF16)


def kernel(x_prompt, x_sample, cache_k, cache_v, cache_idx_k, state_pool, norm_mix, norm_ffn, norm_final,
           par_w_in, par_pool_w, par_pool_scale, par_w_out, gm_w_in, gm_ln_g, gm_ln_b, gm_ws, gm_bs, gm_w_out,
           moe_router_group, moe_router_expert, moe_w_gate, moe_w_up, moe_w_down):
    Bp, Tp, D = x_prompt.shape
    Bs, Ts, _ = x_sample.shape
    past = cache_k.shape[2]
    Np, Ns = Bp * Tp, Bs * Ts
    depth = norm_mix.shape[0]
    assert depth == 2 and Ts == CHUNK and Tp % PROJ_TILE == 0 and Tp % QUERY_TILE == 0 and Tp % TOKEN_TILE == 0 and Ns % TOKEN_TILE == 0

    w_in_bf = _prep_w_in(par_w_in[0])
    pw_bf = par_pool_w[0].astype(BF16)
    ps = par_pool_scale[0].reshape(1, POOL_WIDTH)
    wo_bf = par_w_out[0].astype(BF16)
    g_mix0 = norm_mix[0].reshape(1, D)
    pos_p = jnp.arange(Tp, dtype=I32)
    pos_s = past + jnp.arange(Ts, dtype=I32)

    hist_p = jnp.zeros((Bp, HIST_ROWS, POOL_WIDTH), F32)
    hist_s = jnp.pad(state_pool[0], ((0, 0), (1, 0), (0, 0)))
    tk = KEY_BLOCK
    (q_p, qi_p, k_p, v_p, kiwi_p, kbf_p, vbf_p, kibf_p, yp_p, st_p) = _inproj(
        x_prompt, g_mix0, w_in_bf, _rope_tables(pos_p, LANES), _rope_tables(pos_p, IDX_DIM), hist_p, pw_bf, ps,
        tm=PROJ_TILE, pos0=0)
    (q_s, qi_s, k_s, v_s, kiwi_s, kbf_s, vbf_s, kibf_s, yp_s, st_s) = _inproj(
        x_sample, g_mix0, w_in_bf, _rope_tables(pos_s, LANES), _rope_tables(pos_s, IDX_DIM), hist_s, pw_bf, ps,
        tm=Ts, pos0=past)

    ls = past + Ts
    n_blk = -(-ls // tk)
    tk_s = -(-ls // (n_blk * LANES)) * LANES
    lsp = n_blk * tk_s
    kpad = lambda a, ax: jnp.pad(a, [(0, lsp - ls) if d == ax else (0, 0) for d in range(a.ndim)])
    kall = kpad(jnp.concatenate([cache_k[0].reshape(Bs, past, KV_WIDTH).astype(BF16), kbf_s], axis=1), 1)
    vall = kpad(jnp.concatenate([cache_v[0].reshape(Bs, past, KV_WIDTH).astype(BF16), vbf_s], axis=1), 1)
    vall_t = jnp.transpose(vall.reshape(Bs, lsp // LANES, LANES, KV_WIDTH), (0, 1, 3, 2))
    kiall = kpad(jnp.concatenate([cache_idx_k[0].astype(BF16), kibf_s], axis=1), 1)
    qpad = lambda a, ax: jnp.pad(a, [(0, LANES - Ts) if d == ax else (0, 0) for d in range(a.ndim)])
    x1_s = _dsa(qpad(q_s, 2), qpad(qi_s, 2), qpad(kiwi_s, 1), kall, vall_t, kiall, qpad(x_sample, 1), qpad(yp_s, 1),
                wo_bf, tq=LANES, tk=tk_s, pos0=past, n_keys=ls, n_sel=min(TOPK_MAX, ls // 4))[:, :Ts]
    assert Tp % tk == 0
    x = jnp.pad(x1_s.reshape(Ns, D), ((Np, 0), (0, 0)))
    x = _dsa(q_p, qi_p, kiwi_p, kbf_p, vbf_p, kibf_p, x_prompt, yp_p, wo_bf,
             tq=QUERY_TILE, tk=tk, pos0=0, n_keys=Tp, n_sel=min(TOPK_MAX, Tp // 4), out_base=x)
    x = _hier_moe(x, norm_ffn[0], moe_router_group[0], moe_router_expert[0],
                  moe_w_gate, moe_w_up, moe_w_down, 0, norm_final, final_norm=False)

    cs = Ts
    tril = lambda n: jnp.tril(jnp.ones((n, n), bool))
    ws_p = jnp.where(tril(GM_CHUNK)[None], gm_ws[0], 0.0)
    ws_small = jnp.where(tril(cs)[None], gm_ws[0][:, :cs, :cs], 0.0)
    rep = GM_CHUNK // cs
    ws_s = jnp.einsum('ab,gts->gatbs', jnp.eye(rep, dtype=F32), ws_small).reshape(GM_GROUPS, GM_CHUNK, GM_CHUNK)
    ws2 = jnp.stack([ws_p, ws_s]).astype(BF16)
    gd = D // GM_GROUPS
    bias_p = jnp.repeat(jnp.transpose(gm_bs[0]), gd, axis=1)
    bias_s = jnp.tile(jnp.repeat(jnp.transpose(gm_bs[0][:, :cs]), gd, axis=1), (rep, 1))
    bias2 = jnp.stack([bias_p, bias_s])
    x, gm_v = _gmlp(x, norm_mix[1].reshape(1, D), gm_w_in[0].astype(BF16), gm_ln_g[0].reshape(1, D),
                    gm_ln_b[0].reshape(1, D), ws2, bias2, gm_w_out[0].astype(BF16),
                    tm=TOKEN_TILE, n_first=Np, n_v_rows=Ns)
    y_p, y_s = _hier_moe(x, norm_ffn[1], moe_router_group[1], moe_router_expert[1],
                         moe_w_gate, moe_w_up, moe_w_down, 1, norm_final, final_norm=True, n_first=Np)
    y_p = y_p.reshape(Bp, Tp, D)
    y_s = y_s.reshape(Bs, Ts, D)
    r4 = lambda a, b, t: a.reshape(1, b, t, N_KV_HEADS, HEAD_DIM)
    return (y_p, y_s,
            r4(k_p, Bp, Tp), r4(v_p, Bp, Tp), kiwi_p[:, :, :IDX_DIM][None], st_p[:, 1:][None],
            r4(k_s, Bs, Ts), r4(v_s, Bs, Ts), kiwi_s[:, :, :IDX_DIM][None], st_s[:, 1:][None],
            gm_v.reshape(1, Bs, Ts, D))
```
